```python
import jax
import jax.numpy as jnp
from jax import lax
import numpy as np

D_MODEL = 1024
BATCH = 16
SEQ = 256
DEPTH = 1
DEC_BATCH = 8
DEC_SEQ = 1024
PAST_LEN = 512

GRID_W = 64
MIX_WIDTH = D_MODEL
DA = MIX_WIDTH // 2
HEAD_A = 64
H_A = DA // HEAD_A
DB = MIX_WIDTH - DA
H_B = 4
HEAD_B = DB // H_B
DECAY_LORA = 64
AAA_LORA = 64
GATE_LORA = 128
RWKV_COLS = 3 * DA + DECAY_LORA + AAA_LORA + GATE_LORA
MLSTM_COLS = 4 * DB + 4 * H_B
IN_COLS = RWKV_COLS + MLSTM_COLS
RWKV_SPLITS = (DA, 2 * DA, 3 * DA, 3 * DA + DECAY_LORA, 3 * DA + DECAY_LORA + AAA_LORA)
MLSTM_SPLITS = (DB, 2 * DB, 3 * DB, 4 * DB, 4 * DB + 2 * H_B)
MLSTM_CHUNK = 64
N_EXPERTS = 32
TOP_K = 4
D_FF = D_MODEL
SWIGLU_LIMIT = 7.0
SWIGLU_ALPHA = 1.702
MOE_BLOCK = 256
NORM_EPS = 1e-6
GN_EPS = 64e-5

kernel_name = 'hybrid_rwkv7_mlstm_moe_diffusion_step'


def rmsnorm(x, g):
    xf = x.astype(jnp.float32)
    y = xf * lax.rsqrt(jnp.mean(xf * xf, axis=-1, keepdims=True) + NORM_EPS)
    return (y * g.astype(jnp.float32)).astype(x.dtype)


def shift_prev(u):
    return jnp.pad(u, ((0, 0), (1, 0), (0, 0)))[:, :-1]


def shift_next(u):
    return jnp.pad(u, ((0, 0), (0, 1), (0, 0)))[:, 1:]


def seq_neighbour_mean(u):
    return 0.5 * (shift_prev(u) + shift_next(u))


def grid_neighbour_mean(u):
    Bsz, L, C = u.shape
    rows = L // GRID_W
    g = u.reshape(Bsz, rows, GRID_W, C)
    p = jnp.pad(g, ((0, 0), (1, 1), (1, 1), (0, 0)))
    nb = p[:, :-2, 1:-1] + p[:, 2:, 1:-1] + p[:, 1:-1, :-2] + p[:, 1:-1, 2:]
    return (0.25 * nb).reshape(Bsz, L, C)


def rwkv_scan(r, w, k, v, kk, a, S0, reverse):
    def step(S, inp):
        r_t, w_t, k_t, v_t, kk_t, a_t = inp
        s_kk = jnp.einsum('bhvk,bhk->bhv', S, kk_t)
        S = (S * w_t[:, :, None, :] - s_kk[..., None] * (kk_t * a_t)[:, :, None, :]
             + v_t[..., None] * k_t[:, :, None, :])
        return S, jnp.einsum('bhvk,bhk->bhv', S, r_t)
    xs = tuple(jnp.moveaxis(t, 1, 0) for t in (r, w, k, v, kk, a))
    S, ys = lax.scan(step, S0, xs, reverse=reverse)
    return jnp.moveaxis(ys, 0, 1), S


def mlstm_chunkwise(q, k, v, i_pre, log_f, C0, n0, m0):
    Bsz, H, L, Dh = q.shape
    nc = L // MLSTM_CHUNK
    lower = jnp.tril(jnp.ones((MLSTM_CHUNK, MLSTM_CHUNK), dtype=bool))

    def chunks(t):
        t = t.reshape((Bsz, H, nc, MLSTM_CHUNK) + t.shape[3:])
        return jnp.moveaxis(t, 2, 0)

    def step(carry, inp):
        C, n, m = carry
        qc, kc, vc, ic, fc = inp
        b = jnp.cumsum(fc, axis=-1)
        d_mat = jnp.where(lower, b[..., :, None] - b[..., None, :] + ic[..., None, :], -jnp.inf)
        inter = b + m[..., None]
        m_t = jnp.maximum(inter, jnp.max(d_mat, axis=-1))
        s = jnp.einsum('bhtd,bhsd->bhts', qc, kc) * jnp.exp(d_mat - m_t[..., None])
        carry_w = jnp.exp(inter - m_t)
        num = (jnp.einsum('bhts,bhse->bhte', s, vc)
               + carry_w[..., None] * jnp.einsum('bhtd,bhde->bhte', qc, C))
        den = jnp.sum(s, axis=-1) + carry_w * jnp.einsum('bhtd,bhd->bht', qc, n)
        h = num / jnp.maximum(jnp.abs(den), jnp.exp(-m_t))[..., None]
        b_last = b[..., -1]
        g = b_last[..., None] - b + ic
        m_new = jnp.maximum(b_last + m, jnp.max(g, axis=-1))
        wk = jnp.exp(g - m_new[..., None])
        decay = jnp.exp(b_last + m - m_new)
        C = decay[..., None, None] * C + jnp.einsum('bhs,bhsd,bhse->bhde', wk, kc, vc)
        n = decay[..., None] * n + jnp.einsum('bhs,bhsd->bhd', wk, kc)
        return (C, n, m_new), h

    (C, n, m), h = lax.scan(step, (C0, n0, m0), tuple(chunks(t) for t in (q, k, v, i_pre, log_f)))
    return jnp.moveaxis(h, 0, 2).reshape(Bsz, H, L, Dh), (C, n, m)


def token_mixers(h, lp, S0, C0, n0, m0, grid):
    f32 = jnp.float32
    Bsz, L, _ = h.shape
    z = (h @ lp['w_in']).astype(f32)
    zr, zm = z[..., :RWKV_COLS], z[..., RWKV_COLS:]

    local_mean = grid_neighbour_mean if grid else seq_neighbour_mean
    zr = zr + (local_mean(zr) - zr) * lp['rwkv_mu']
    r, k, v, xw, xa, xg = jnp.split(zr, RWKV_SPLITS, axis=-1)

    def heads_a(t):
        return t.reshape(Bsz, L, H_A, HEAD_A)

    rh, vh = heads_a(r), heads_a(v)
    kk = heads_a(k * lp['rwkv_kk'])
    kk = kk / jnp.maximum(jnp.sqrt(jnp.sum(kk * kk, axis=-1, keepdims=True)), 1e-12)
    lw = jnp.tanh(xw)
    gate = jax.nn.sigmoid(xg) @ lp['rwkv_g2']
    y_dirs, s_dirs = [], []
    for d in range(2):
        w_log = -jax.nn.softplus(-(lp['rwkv_w0'][d] + lw @ lp['rwkv_w2'][d])) - 0.5
        decay = jnp.exp(-jnp.exp(w_log))
        a = jax.nn.sigmoid(lp['rwkv_a0'][d] + xa @ lp['rwkv_a2'][d])
        kd = heads_a(k * (1.0 + (a - 1.0) * lp['rwkv_ka']))
        y, S = rwkv_scan(rh, heads_a(decay), kd, vh, kk, heads_a(a), S0[:, d].astype(f32), d == 1)
        mu = jnp.mean(y, axis=-1, keepdims=True)
        var = jnp.mean(jnp.square(y - mu), axis=-1, keepdims=True)
        yn = ((y - mu) * lax.rsqrt(var + GN_EPS)).reshape(Bsz, L, DA) * lp['rwkv_gn_w'] + lp['rwkv_gn_b']
        bonus = (jnp.sum(rh * kd * lp['rwkv_rk'], axis=-1, keepdims=True) * vh).reshape(Bsz, L, DA)
        y_dirs.append(yn + bonus)
        s_dirs.append(S)
    y_rwkv = (y_dirs[0] + y_dirs[1]) * gate

    q, km, vm, og, ig, fg = jnp.split(zm, MLSTM_SPLITS, axis=-1)
    qk = jnp.concatenate([q, km], axis=-1)
    cw = lp['mlstm_conv']
    qk = jax.nn.silu(cw[0] * shift_prev(qk) + cw[1] * qk + cw[2] * shift_next(qk))
    q, km = jnp.split(qk, 2, axis=-1)

    def heads_b(t):
        return t.reshape(Bsz, L, H_B, HEAD_B).transpose(0, 2, 1, 3)

    qh, kh, vmh = heads_b(q), heads_b(km) * (HEAD_B ** -0.5), heads_b(vm)
    ig = (ig.reshape(Bsz, L, 2, H_B) + lp['mlstm_bi']).transpose(2, 0, 3, 1)
    lf = jax.nn.log_sigmoid(fg.reshape(Bsz, L, 2, H_B) + lp['mlstm_bf']).transpose(2, 0, 3, 1)
    h_f, st_f = mlstm_chunkwise(qh, kh, vmh, ig[0], lf[0],
                                C0[:, 0].astype(f32), n0[:, 0].astype(f32), m0[:, 0].astype(f32))

    def flip(t):
        return jnp.flip(t, axis=2)

    h_b, st_b = mlstm_chunkwise(flip(qh), flip(kh), flip(vmh), flip(ig[1]), flip(lf[1]),
                                C0[:, 1].astype(f32), n0[:, 1].astype(f32), m0[:, 1].astype(f32))
    h_m = (h_f + flip(h_b)).transpose(0, 2, 1, 3)
    h_m = h_m * lax.rsqrt(jnp.mean(jnp.square(h_m), axis=-1, keepdims=True) + NORM_EPS)
    y_mlstm = h_m.reshape(Bsz, L, DB) * lp['mlstm_norm_g'] * jax.nn.sigmoid(og)

    out = jnp.concatenate([y_rwkv, y_mlstm], axis=-1).astype(h.dtype) @ lp['w_out']
    states = (jnp.stack(s_dirs, axis=1),
              jnp.stack([st_f[0], st_b[0]], axis=1),
              jnp.stack([st_f[1], st_b[1]], axis=1),
              jnp.stack([st_f[2], st_b[2]], axis=1))
    return out, states


def moe(h, lp):
    f32 = jnp.float32
    Bsz, L, D = h.shape
    x = h.reshape(-1, D)
    T = x.shape[0]
    logits = x.astype(f32) @ lp['router_w'].astype(f32) + lp['router_b'].astype(f32)
    top_val, top_idx = lax.top_k(logits, TOP_K)
    gates = jax.nn.softmax(top_val, axis=-1)
    TK = T * TOP_K
    flat_e = top_idx.reshape(-1)
    order = jnp.argsort(flat_e)
    sorted_e = flat_e[order]
    tok = order // TOP_K
    counts = jnp.bincount(flat_e, length=N_EXPERTS)
    padded = (counts + MOE_BLOCK - 1) // MOE_BLOCK * MOE_BLOCK
    pad_end = jnp.cumsum(padded)
    pad_start = pad_end - padded
    start = jnp.cumsum(counts) - counts
    dest = pad_start[sorted_e] + jnp.arange(TK) - start[sorted_e]
    n_blocks = -(-TK // MOE_BLOCK) + N_EXPERTS
    rows = jnp.full((n_blocks * MOE_BLOCK,), T, dtype=jnp.int32).at[dest].set(tok.astype(jnp.int32))
    block_e = jnp.minimum(jnp.searchsorted(pad_end, jnp.arange(n_blocks) * MOE_BLOCK, side='right'),
                          N_EXPERTS - 1)
    xpad = jnp.concatenate([x, jnp.zeros((1, D), x.dtype)], axis=0)
    xb = xpad[rows].reshape(n_blocks, MOE_BLOCK, D)
    w1, b1, w2, b2 = lp['moe_w1'], lp['moe_b1'], lp['moe_w2'], lp['moe_b2']

    def expert_block(args):
        xe, e = args
        hu = xe @ w1[e] + b1[e]
        glu, lin = jnp.split(hu, 2, axis=-1)
        glu = jnp.minimum(glu, SWIGLU_LIMIT)
        lin = jnp.clip(lin, -SWIGLU_LIMIT, SWIGLU_LIMIT)
        act = glu * jax.nn.sigmoid(SWIGLU_ALPHA * glu) * (lin + 1.0)
        return act @ w2[e] + b2[e]

    yb = lax.map(expert_block, (xb, block_e)).reshape(-1, D)
    y_sorted = yb[dest].astype(f32) * gates.reshape(-1)[order][:, None]
    out = jnp.zeros((T, D), f32).at[tok].add(y_sorted)
    return out.reshape(Bsz, L, D).astype(h.dtype)


def layer(x, mod, lp, S0, C0, n0, m0, grid):
    sh1, sc1, g1, sh2, sc2, g2 = jnp.split(mod, 6, axis=-1)
    hn = rmsnorm(x, lp['norm1_g']) * (1.0 + sc1) + sh1
    mix, states = token_mixers(hn, lp, S0, C0, n0, m0, grid)
    x = (x + g1 * mix).astype(x.dtype)
    hn = rmsnorm(x, lp['norm2_g']) * (1.0 + sc2) + sh2
    x = (x + g2 * moe(hn, lp)).astype(x.dtype)
    return x, states


def setup_inputs(seed: int = 0) -> dict:
    key = jax.random.key(seed)
    ks = jax.random.split(key, 40)
    f32 = jnp.float32

    def nrm(k, shape, s):
        return jax.random.normal(k, shape, f32) * s

    return {
        'x_prompt': nrm(ks[0], (BATCH, SEQ, D_MODEL), 1.0),
        'x_sample': nrm(ks[1], (DEC_BATCH, DEC_SEQ, D_MODEL), 1.0),
        'state_rwkv': nrm(ks[2], (DEC_BATCH, DEPTH, 2, H_A, HEAD_A, HEAD_A), 0.3),
        'state_mlstm_C': nrm(ks[3], (DEC_BATCH, DEPTH, 2, H_B, HEAD_B, HEAD_B), 0.3),
        'state_mlstm_n': nrm(ks[4], (DEC_BATCH, DEPTH, 2, H_B, HEAD_B), 0.3),
        'state_mlstm_m': nrm(ks[5], (DEC_BATCH, DEPTH, 2, H_B), 1.0),
        'c': nrm(ks[6], (DEC_BATCH, D_MODEL), 1.0),
        'c_ctx': nrm(ks[7], (D_MODEL,), 1.0),
        'ada_w': nrm(ks[8], (DEPTH, D_MODEL, 6 * D_MODEL), 0.5 * D_MODEL ** -0.5),
        'ada_b': nrm(ks[9], (DEPTH, 6 * D_MODEL), 0.02),
        'norm1_g': 1.0 + nrm(ks[10], (DEPTH, D_MODEL), 0.05),
        'norm2_g': 1.0 + nrm(ks[11], (DEPTH, D_MODEL), 0.05),
        'w_in': nrm(ks[12], (DEPTH, D_MODEL, IN_COLS), D_MODEL ** -0.5),
        'w_out': nrm(ks[13], (DEPTH, MIX_WIDTH, D_MODEL), MIX_WIDTH ** -0.5),
        'rwkv_mu': jax.random.uniform(ks[14], (DEPTH, RWKV_COLS), f32, 0.0, 1.0),
        'rwkv_w0': jax.random.uniform(ks[15], (DEPTH, 2, DA), f32, -4.0, 1.0),
        'rwkv_w2': nrm(ks[16], (DEPTH, 2, DECAY_LORA, DA), 0.1 * DECAY_LORA ** -0.5),
        'rwkv_a0': nrm(ks[17], (DEPTH, 2, DA), 0.5),
        'rwkv_a2': nrm(ks[18], (DEPTH, 2, AAA_LORA, DA), 0.1 * AAA_LORA ** -0.5),
        'rwkv_g2': nrm(ks[19], (DEPTH, GATE_LORA, DA), GATE_LORA ** -0.5),
        'rwkv_kk': 0.85 + nrm(ks[20], (DEPTH, DA), 0.05),
        'rwkv_ka': 1.0 + nrm(ks[21], (DEPTH, DA), 0.05),
        'rwkv_rk': nrm(ks[22], (DEPTH, H_A, HEAD_A), 0.1),
        'rwkv_gn_w': 1.0 + nrm(ks[23], (DEPTH, DA), 0.05),
        'rwkv_gn_b': nrm(ks[24], (DEPTH, DA), 0.01),
        'mlstm_conv': jnp.array([0.25, 0.5, 0.25], f32)[None, :, None] + nrm(ks[25], (DEPTH, 3, 2 * DB), 0.1),
        'mlstm_bi': nrm(ks[26], (DEPTH, 2, H_B), 0.5) - 1.0,
        'mlstm_bf': jnp.linspace(3.0, 6.0, H_B, dtype=f32) + nrm(ks[27], (DEPTH, 2, H_B), 0.1),
        'mlstm_norm_g': 1.0 + nrm(ks[28], (DEPTH, DB), 0.05),
        'router_w': nrm(ks[29], (DEPTH, D_MODEL, N_EXPERTS), D_MODEL ** -0.5),
        'router_b': nrm(ks[30], (DEPTH, N_EXPERTS), 0.01),
        'moe_w1': nrm(ks[31], (DEPTH, N_EXPERTS, D_MODEL, 2 * D_FF), D_MODEL ** -0.5),
        'moe_b1': nrm(ks[32], (DEPTH, N_EXPERTS, 2 * D_FF), 0.01),
        'moe_w2': nrm(ks[33], (DEPTH, N_EXPERTS, D_FF, D_MODEL), D_FF ** -0.5),
        'moe_b2': nrm(ks[34], (DEPTH, N_EXPERTS, D_MODEL), 0.01),
        'final_g': 1.0 + nrm(ks[35], (D_MODEL,), 0.05),
    }


def reference(x_prompt, x_sample, state_rwkv, state_mlstm_C, state_mlstm_n, state_mlstm_m, c, c_ctx,
              ada_w, ada_b, norm1_g, norm2_g, w_in, w_out, rwkv_mu, rwkv_w0, rwkv_w2, rwkv_a0, rwkv_a2,
              rwkv_g2, rwkv_kk, rwkv_ka, rwkv_rk, rwkv_gn_w, rwkv_gn_b, mlstm_conv, mlstm_bi, mlstm_bf,
              mlstm_norm_g, router_w, router_b, moe_w1, moe_b1, moe_w2, moe_b2, final_g):
    f32 = jnp.float32
    Bp = x_prompt.shape[0]
    S_zero = jnp.zeros((Bp, 2, H_A, HEAD_A, HEAD_A), f32)
    C_zero = jnp.zeros((Bp, 2, H_B, HEAD_B, HEAD_B), f32)
    n_zero = jnp.zeros((Bp, 2, H_B, HEAD_B), f32)
    m_zero = jnp.zeros((Bp, 2, H_B), f32)
    xp, xs = x_prompt, x_sample
    new_S, new_C, new_n, new_m = [], [], [], []
    for l in range(DEPTH):
        lp = {
            'norm1_g': norm1_g[l], 'norm2_g': norm2_g[l], 'w_in': w_in[l], 'w_out': w_out[l],
            'rwkv_mu': rwkv_mu[l], 'rwkv_w0': rwkv_w0[l], 'rwkv_w2': rwkv_w2[l], 'rwkv_a0': rwkv_a0[l],
            'rwkv_a2': rwkv_a2[l], 'rwkv_g2': rwkv_g2[l], 'rwkv_kk': rwkv_kk[l], 'rwkv_ka': rwkv_ka[l],
            'rwkv_rk': rwkv_rk[l], 'rwkv_gn_w': rwkv_gn_w[l], 'rwkv_gn_b': rwkv_gn_b[l],
            'mlstm_conv': mlstm_conv[l], 'mlstm_bi': mlstm_bi[l], 'mlstm_bf': mlstm_bf[l],
            'mlstm_norm_g': mlstm_norm_g[l], 'router_w': router_w[l], 'router_b': router_b[l],
            'moe_w1': moe_w1[l], 'moe_b1': moe_b1[l], 'moe_w2': moe_w2[l], 'moe_b2': moe_b2[l],
        }
        mod_p = (jax.nn.silu(c_ctx) @ ada_w[l] + ada_b[l])[None, None, :]
        mod_s = (jax.nn.silu(c) @ ada_w[l] + ada_b[l])[:, None, :]
        xp, st = layer(xp, mod_p, lp, S_zero, C_zero, n_zero, m_zero, False)
        new_S.append(st[0])
        new_C.append(st[1])
        new_n.append(st[2])
        new_m.append(st[3])
        xs, _ = layer(xs, mod_s, lp, state_rwkv[:, l], state_mlstm_C[:, l], state_mlstm_n[:, l],
                      state_mlstm_m[:, l], True)
    y_prompt = rmsnorm(xp, final_g)
    y_sample = rmsnorm(xs, final_g)
    sd = x_prompt.dtype
    new_state_rwkv = jnp.stack(new_S, axis=1).astype(sd)
    new_state_mlstm_C = jnp.stack(new_C, axis=1).astype(sd)
    new_state_mlstm_n = jnp.stack(new_n, axis=1).astype(sd)
    new_state_mlstm_m = jnp.stack(new_m, axis=1).astype(sd)
    return (y_prompt, y_sample, new_state_rwkv, new_state_mlstm_C, new_state_mlstm_n, new_state_mlstm_m)
```

```python
import functools

import jax
import jax.numpy as jnp
from jax import lax
from jax.experimental import pallas as pl
from jax.experimental.pallas import tpu as pltpu

F32 = jnp.float32
BF16 = jnp.bfloat16
HIGHEST = lax.Precision.HIGHEST

D_MODEL = 1024
DA = 512
HEAD_A = 64
H_A = DA // HEAD_A
DB = 512
H_B = 4
HEAD_B = DB // H_B
DECAY_LORA = 64
AAA_LORA = 64
GATE_LORA = 128
RWKV_COLS = 3 * DA + DECAY_LORA + AAA_LORA + GATE_LORA
MLSTM_MAIN = 4 * DB
N_GATES = 4 * H_B
GRID_W = 64
CHUNK = 64
N_EXPERTS = 32
TOP_K = 4
D_FF = D_MODEL
SWIGLU_LIMIT = 7.0
SWIGLU_ALPHA = 1.702
MOE_BLOCK = 256
NORM_EPS = 1e-6
GN_EPS = 64e-5
ROW_TILE = 256
SUBLANES = 8
LANES = 128
LANE_ROWS = D_MODEL // LANES
VMEM_LIMIT = 56 * 1024 * 1024


def _mm(a, b, precision=HIGHEST):
    return lax.dot_general(a, b, (((1,), (0,)), ((), ())), precision=precision,
                           preferred_element_type=F32)


def _mm_nt(a, b, precision=HIGHEST):
    return lax.dot_general(a, b, (((1,), (1,)), ((), ())), precision=precision,
                           preferred_element_type=F32)


def _mm_tn(a, b, precision=HIGHEST):
    return lax.dot_general(a, b, (((0,), (0,)), ((), ())), precision=precision,
                           preferred_element_type=F32)


def _mm_bf16(a, b):
    return lax.dot_general(a.astype(BF16), b.astype(BF16), (((1,), (0,)), ((), ())),
                           preferred_element_type=F32)


def _sigmoid(x):
    return 1.0 / (1.0 + jnp.exp(-x))


def _log_sigmoid(x):
    return jnp.minimum(x, 0.0) - jnp.log(1.0 + jnp.exp(-jnp.abs(x)))


def _params(*sem):
    return pltpu.CompilerParams(dimension_semantics=sem, vmem_limit_bytes=VMEM_LIMIT)


def _const_spec(shape):
    nd = len(shape)
    return pl.BlockSpec(shape, lambda *_: (0,) * nd)


def _tri(n, reverse, strict):
    t = lax.broadcasted_iota(jnp.int32, (n, n), 0)
    s = lax.broadcasted_iota(jnp.int32, (n, n), 1)
    if reverse:
        return (s > t) if strict else (s >= t)
    return (s < t) if strict else (s <= t)


def _seq_neighbours(ref, b, c, nc, cols):
    r0 = pl.multiple_of(c * CHUNK, CHUNK)
    zc = ref[b, pl.ds(r0, CHUNK), cols]
    row = lax.broadcasted_iota(jnp.int32, zc.shape, 0)
    p0 = pl.multiple_of(jnp.maximum(r0 - SUBLANES, 0), SUBLANES)
    n0 = pl.multiple_of(jnp.minimum(r0 + CHUNK, (nc - 1) * CHUNK), SUBLANES)
    before = ref[b, pl.ds(p0, SUBLANES), cols][SUBLANES - 1:SUBLANES]
    after = ref[b, pl.ds(n0, SUBLANES), cols][0:1]
    before = jnp.where(c > 0, before, 0.0)
    after = jnp.where(c < nc - 1, after, 0.0)
    prev = jnp.where(row == 0, before, pltpu.roll(zc, 1, 0))
    nxt = jnp.where(row == CHUNK - 1, after, pltpu.roll(zc, CHUNK - 1, 0))
    return zc, prev, nxt


def _ada_kernel(c_ref, w_ref, b_ref, o_ref):
    cc = c_ref[...]
    o_ref[...] = _mm(cc * _sigmoid(cc), w_ref[...]) + b_ref[...]


def _ada_mod(cc, ada_w, ada_b):
    rows = cc.shape[0]
    ncol = ada_w.shape[1]
    tn = 1536
    return pl.pallas_call(
        _ada_kernel,
        grid=(ncol // tn,),
        in_specs=[_const_spec((rows, D_MODEL)),
                  pl.BlockSpec((D_MODEL, tn), lambda j: (0, j)),
                  pl.BlockSpec((1, tn), lambda j: (0, j))],
        out_specs=pl.BlockSpec((rows, tn), lambda j: (0, j)),
        out_shape=jax.ShapeDtypeStruct((rows, ncol), F32),
        compiler_params=_params("arbitrary"),
        name="ada_mod",
    )(cc, ada_w, ada_b.reshape(1, ncol))


def _inproj_kernel(x_ref, mod_ref, g_ref, wr_ref, wm_ref, wg_ref, wgt_ref, zr_ref, zm_ref, zg_ref, zgt_ref):
    x = x_ref[...]
    mod = mod_ref[0]
    sh, sc = mod[:, 0:D_MODEL], mod[:, D_MODEL:2 * D_MODEL]
    y = x * lax.rsqrt(jnp.mean(x * x, axis=-1, keepdims=True) + NORM_EPS)
    hn = y * g_ref[...] * (1.0 + sc) + sh
    hb = hn.astype(BF16)
    zr_ref[...] = lax.dot_general(hb, wr_ref[...], (((1,), (0,)), ((), ())), preferred_element_type=F32)
    zm_ref[...] = lax.dot_general(hb, wm_ref[...], (((1,), (0,)), ((), ())), preferred_element_type=F32)
    zg_ref[...] = _mm(hn, wg_ref[...])
    zgt_ref[...] = _mm_nt(wgt_ref[...], hn)


def _inproj(x2, mod, norm_g, w_r, w_m, w_g, w_gt):
    rows = x2.shape[0]
    tiles_per_mod = rows // mod.shape[0] // ROW_TILE
    return pl.pallas_call(
        _inproj_kernel,
        grid=(rows // ROW_TILE,),
        in_specs=[pl.BlockSpec((ROW_TILE, D_MODEL), lambda i: (i, 0)),
                  pl.BlockSpec((1, 1, 6 * D_MODEL), lambda i: (i // tiles_per_mod, 0, 0)),
                  _const_spec((1, D_MODEL)),
                  _const_spec(w_r.shape), _const_spec(w_m.shape), _const_spec(w_g.shape),
                  _const_spec(w_gt.shape)],
        out_specs=[pl.BlockSpec((ROW_TILE, RWKV_COLS), lambda i: (i, 0)),
                   pl.BlockSpec((ROW_TILE, MLSTM_MAIN), lambda i: (i, 0)),
                   pl.BlockSpec((ROW_TILE, N_GATES), lambda i: (i, 0)),
                   pl.BlockSpec((N_GATES, ROW_TILE), lambda i: (0, i))],
        out_shape=[jax.ShapeDtypeStruct((rows, RWKV_COLS), F32),
                   jax.ShapeDtypeStruct((rows, MLSTM_MAIN), F32),
                   jax.ShapeDtypeStruct((rows, N_GATES), F32),
                   jax.ShapeDtypeStruct((N_GATES, rows), F32)],
        compiler_params=_params("arbitrary"),
        name="inproj",
    )(x2, mod, norm_g.reshape(1, D_MODEL), w_r, w_m, w_g, w_gt)


def _rwkv_kernel(zr_ref, s0_ref, mu_ref, kkw_ref, ka_ref, rk_ref, gnw_ref, gnb_ref, w0_ref, w2_ref,
                 a0_ref, a2_ref, g2_ref, bd_ref, y_ref, sout_ref,
                 r_s, k_s, v_s, kk_s, gate_s, lx_s, st_s, *, seq_len, grid):
    nc = seq_len // CHUNK
    bd = bd_ref[...]

    def mix_chunk(c, carry):
        r0 = pl.multiple_of(c * CHUNK, CHUNK)
        if grid:
            zc = zr_ref[0, pl.ds(r0, CHUNK), :]
            row = lax.broadcasted_iota(jnp.int32, zc.shape, 0)
            up0 = pl.multiple_of(jnp.maximum(c - 1, 0) * CHUNK, CHUNK)
            dn0 = pl.multiple_of(jnp.minimum(c + 1, nc - 1) * CHUNK, CHUNK)
            up = jnp.where(c > 0, zr_ref[0, pl.ds(up0, CHUNK), :], 0.0)
            down = jnp.where(c < nc - 1, zr_ref[0, pl.ds(dn0, CHUNK), :], 0.0)
            left = jnp.where(row == 0, 0.0, pltpu.roll(zc, 1, 0))
            right = jnp.where(row == CHUNK - 1, 0.0, pltpu.roll(zc, CHUNK - 1, 0))
            local = 0.25 * (up + down + left + right)
        else:
            zc, prev, nxt = _seq_neighbours(zr_ref, 0, c, nc, slice(None))
            local = 0.5 * (prev + nxt)
        z = zc + (local - zc) * mu_ref[...]
        r, k, v = z[:, 0:DA], z[:, DA:2 * DA], z[:, 2 * DA:3 * DA]
        lx = z[:, 3 * DA:3 * DA + DECAY_LORA + AAA_LORA]
        xg = z[:, 3 * DA + DECAY_LORA + AAA_LORA:]
        kk = k * kkw_ref[...]
        kk = kk / jnp.maximum(jnp.sqrt(_mm(kk * kk, bd)), 1e-12)
        lane = lax.broadcasted_iota(jnp.int32, lx.shape, 1)
        r_s[pl.ds(r0, CHUNK), :] = r
        k_s[pl.ds(r0, CHUNK), :] = k
        v_s[pl.ds(r0, CHUNK), :] = v
        kk_s[pl.ds(r0, CHUNK), :] = kk
        gate_s[pl.ds(r0, CHUNK), :] = _mm(_sigmoid(xg), g2_ref[...])
        lx_s[pl.ds(r0, CHUNK), :] = jnp.where(lane < DECAY_LORA, jnp.tanh(lx), lx)
        return carry

    lax.fori_loop(0, nc, mix_chunk, 0)

    for d in range(2):
        reverse = d == 1
        tri = _tri(CHUNK, reverse, False).astype(F32)
        strict = _tri(CHUNK, reverse, True)
        incl = _tri(CHUNK, reverse, False)
        last = 0 if reverse else CHUNK - 1
        st_s[...] = s0_ref[0, d]

        def scan_chunk(i, carry, d=d, reverse=reverse, tri=tri, strict=strict, incl=incl, last=last):
            c = nc - 1 - i if reverse else i
            r0 = pl.multiple_of(c * CHUNK, CHUNK)
            rows = pl.ds(r0, CHUNK)
            r, k, v, kk = r_s[rows, :], k_s[rows, :], v_s[rows, :], kk_s[rows, :]
            lx = lx_s[rows, :]
            wl = w0_ref[d:d + 1, :] + _mm(lx[:, 0:DECAY_LORA], w2_ref[d])
            logw = -jnp.exp(_log_sigmoid(wl) - 0.5)
            a = _sigmoid(a0_ref[d:d + 1, :] + _mm(lx[:, DECAY_LORA:], a2_ref[d]))
            kd = k * (1.0 + (a - 1.0) * ka_ref[...])
            kb = kk * a
            cum = _mm(tri, logw)
            clast = cum[last:last + 1]
            rt = r * jnp.exp(cum)
            at = -kk * jnp.exp(cum - logw)
            einv = jnp.exp(-cum)
            kt, bt = kd * einv, kb * einv
            edec = jnp.exp(clast - cum)
            kp, bp = kd * edec, kb * edec
            ptot = jnp.exp(clast)
            ys = []
            for h in range(H_A):
                sl = slice(h * HEAD_A, (h + 1) * HEAD_A)
                ar = jnp.concatenate([at[:, sl], rt[:, sl]], axis=0)
                kbt = jnp.concatenate([kt[:, sl], bt[:, sl]], axis=0)
                amat = _mm_nt(ar, kbt)
                a_ak = jnp.where(strict, amat[:CHUNK, :CHUNK], 0.0)
                a_ab = jnp.where(strict, amat[:CHUNK, CHUNK:], 0.0)
                a_rk = jnp.where(incl, amat[CHUNK:, :CHUNK], 0.0)
                a_rb = jnp.where(incl, amat[CHUNK:, CHUNK:], 0.0)
                s_prev = st_s[h]
                ars = _mm_nt(ar, s_prev)
                vh = v[:, sl]
                u = ars[:CHUNK] + _mm(a_ak, vh)
                npow = a_ab
                for j in range(6):
                    u = u + _mm(npow, u)
                    if j < 5:
                        npow = _mm(npow, npow)
                vu = jnp.concatenate([vh, u], axis=0)
                ys.append(ars[CHUNK:] + _mm(jnp.concatenate([a_rk, a_rb], axis=1), vu))
                kbp = jnp.concatenate([kp[:, sl], bp[:, sl]], axis=0)
                st_s[h] = s_prev * ptot[:, sl] + _mm_tn(vu, kbp)
            y = jnp.concatenate(ys, axis=1)
            mean = _mm(y, bd) * (1.0 / HEAD_A)
            yc = y - mean
            var = _mm(yc * yc, bd) * (1.0 / HEAD_A)
            out = yc * lax.rsqrt(var + GN_EPS) * gnw_ref[...] + gnb_ref[...]
            out = out + _mm(r * kd * rk_ref[...], bd) * v
            if reverse:
                y_ref[0, rows, :] = (y_ref[0, rows, :] + out) * gate_s[rows, :]
            else:
                y_ref[0, rows, :] = out
            return carry

        lax.fori_loop(0, nc, scan_chunk, 0)
        sout_ref[0, d] = st_s[...]


def _rwkv(zr, s0, p, grid):
    bsz, seq_len, _ = zr.shape
    kern = functools.partial(_rwkv_kernel, seq_len=seq_len, grid=grid)
    consts = [p['mu'], p['kkw'], p['ka'], p['rk'], p['gnw'], p['gnb'], p['w0'], p['w2'], p['a0'], p['a2'],
              p['g2'], p['bd64']]
    return pl.pallas_call(
        kern,
        grid=(bsz,),
        in_specs=[pl.BlockSpec((1, seq_len, RWKV_COLS), lambda b: (b, 0, 0)),
                  pl.BlockSpec((1, 2, H_A, HEAD_A, HEAD_A), lambda b: (b, 0, 0, 0, 0))]
                 + [_const_spec(a.shape) for a in consts],
        out_specs=[pl.BlockSpec((1, seq_len, DA), lambda b: (b, 0, 0)),
                   pl.BlockSpec((1, 2, H_A, HEAD_A, HEAD_A), lambda b: (b, 0, 0, 0, 0))],
        out_shape=[jax.ShapeDtypeStruct((bsz, seq_len, DA), F32),
                   jax.ShapeDtypeStruct((bsz, 2, H_A, HEAD_A, HEAD_A), F32)],
        scratch_shapes=[pltpu.VMEM((seq_len, DA), F32)] * 5
                       + [pltpu.VMEM((seq_len, DECAY_LORA + AAA_LORA), F32),
                          pltpu.VMEM((H_A, HEAD_A, HEAD_A), F32)],
        compiler_params=_params("arbitrary"),
        name="rwkv_scan",
    )(zr, s0, *consts)


def _mlstm_kernel(zm_ref, g_ref, gt_ref, c0_ref, n0_ref, m0_ref, conv_ref, bi_ref, bit_ref, bf_ref, bft_ref,
                  ng_ref, y_ref, cout_ref, nout_ref, mout_ref,
                  qk_s, h_s, c_s, n_s, m_s, *, seq_len):
    nc = seq_len // CHUNK

    def conv_chunk(c, carry):
        zc, prev, nxt = _seq_neighbours(zm_ref, 0, c, nc, slice(0, 2 * DB))
        u = conv_ref[0:1, :] * prev + conv_ref[1:2, :] * zc + conv_ref[2:3, :] * nxt
        qk = u * _sigmoid(u)
        lane = lax.broadcasted_iota(jnp.int32, qk.shape, 1)
        qk_s[pl.ds(pl.multiple_of(c * CHUNK, CHUNK), CHUNK), :] = jnp.where(lane >= DB, qk * (HEAD_B ** -0.5), qk)
        return carry

    lax.fori_loop(0, nc, conv_chunk, 0)

    for d in range(2):
        reverse = d == 1
        incl = _tri(CHUNK, reverse, False)
        tri_col = incl.astype(F32)
        tri_row = _tri(CHUNK, not reverse, False).astype(F32)
        last = 0 if reverse else CHUNK - 1
        c_s[...] = c0_ref[0, d]
        n_s[...] = n0_ref[0, d]
        m_s[...] = jnp.broadcast_to(m0_ref[0, d], (H_B, LANES))
        gsl = slice(d * H_B, (d + 1) * H_B)
        fsl = slice(2 * H_B + d * H_B, 2 * H_B + (d + 1) * H_B)

        def scan_chunk(i, carry, reverse=reverse, incl=incl, tri_col=tri_col, tri_row=tri_row, last=last,
                       gsl=gsl, fsl=fsl):
            c = nc - 1 - i if reverse else i
            r0 = pl.multiple_of(c * CHUNK, CHUNK)
            rows = pl.ds(r0, CHUNK)
            gcol = g_ref[0, rows, :]
            grow = gt_ref[0, c]
            i_col = gcol[:, gsl] + bi_ref[:, gsl]
            f_col = _log_sigmoid(gcol[:, fsl] + bf_ref[:, gsl])
            i_row = grow[gsl, :] + bit_ref[gsl, :]
            f_row = _log_sigmoid(grow[fsl, :] + bft_ref[gsl, :])
            b_col = _mm(tri_col, f_col)
            b_row = _mm(f_row, tri_row)
            b_last = b_col[last:last + 1]
            for h in range(H_B):
                sl = slice(h * HEAD_B, (h + 1) * HEAD_B)
                q = qk_s[rows, h * HEAD_B:(h + 1) * HEAD_B]
                k = qk_s[rows, DB + h * HEAD_B:DB + (h + 1) * HEAD_B]
                v = zm_ref[0, rows, 2 * DB + h * HEAD_B:2 * DB + (h + 1) * HEAD_B]
                bc, br = b_col[:, h:h + 1], b_row[h:h + 1, :]
                ic, ir = i_col[:, h:h + 1], i_row[h:h + 1, :]
                m_prev = m_s[h:h + 1, 0:1]
                dmat = jnp.where(incl, bc - br + ir, -jnp.inf)
                inter = bc + m_prev
                m_t = jnp.maximum(inter, jnp.max(dmat, axis=-1, keepdims=True))
                s = _mm_nt(q, k) * jnp.exp(dmat - m_t)
                carry_w = jnp.exp(inter - m_t)
                c_prev, n_prev = c_s[h], n_s[h:h + 1, :]
                num = _mm(s, v) + carry_w * _mm(q, c_prev)
                den = jnp.sum(s, axis=-1, keepdims=True) + carry_w * jnp.sum(q * n_prev, axis=-1, keepdims=True)
                hout = num / jnp.maximum(jnp.abs(den), jnp.exp(-m_t))
                bl = b_last[:, h:h + 1]
                g = bl - bc + ic
                m_new = jnp.maximum(bl + m_prev, jnp.max(g, axis=0, keepdims=True))
                wk = jnp.exp(g - m_new)
                decay = jnp.exp(bl + m_prev - m_new)
                c_s[h] = decay * c_prev + _mm_tn(k, wk * v)
                n_s[h:h + 1, :] = decay * n_prev + jnp.sum(wk * k, axis=0, keepdims=True)
                m_s[h:h + 1, :] = jnp.broadcast_to(m_new, (1, LANES))
                if reverse:
                    h_s[rows, sl] = h_s[rows, sl] + hout
                else:
                    h_s[rows, sl] = hout
            return carry

        lax.fori_loop(0, nc, scan_chunk, 0)
        cout_ref[0, d] = c_s[...]
        nout_ref[0, d] = n_s[...]
        mout_ref[0, d] = m_s[...]

    def norm_chunk(c, carry):
        rows = pl.ds(pl.multiple_of(c * CHUNK, CHUNK), CHUNK)
        og = zm_ref[0, rows, 3 * DB:4 * DB]
        hm = h_s[rows, :]
        parts = []
        for h in range(H_B):
            hh = hm[:, h * HEAD_B:(h + 1) * HEAD_B]
            parts.append(hh * lax.rsqrt(jnp.mean(hh * hh, axis=-1, keepdims=True) + NORM_EPS))
        y_ref[0, rows, :] = jnp.concatenate(parts, axis=1) * ng_ref[...] * _sigmoid(og)
        return carry

    lax.fori_loop(0, nc, norm_chunk, 0)


def _mlstm(zm, g, gt, c0, n0, m0, p):
    bsz, seq_len, _ = zm.shape
    nc = seq_len // CHUNK
    kern = functools.partial(_mlstm_kernel, seq_len=seq_len)
    consts = [p['conv'], p['bi'], p['bit'], p['bf'], p['bft'], p['ng']]
    return pl.pallas_call(
        kern,
        grid=(bsz,),
        in_specs=[pl.BlockSpec((1, seq_len, MLSTM_MAIN), lambda b: (b, 0, 0)),
                  pl.BlockSpec((1, seq_len, N_GATES), lambda b: (b, 0, 0)),
                  pl.BlockSpec((1, nc, N_GATES, CHUNK), lambda b: (b, 0, 0, 0)),
                  pl.BlockSpec((1, 2, H_B, HEAD_B, HEAD_B), lambda b: (b, 0, 0, 0, 0)),
                  pl.BlockSpec((1, 2, H_B, HEAD_B), lambda b: (b, 0, 0, 0)),
                  pl.BlockSpec((1, 2, H_B, 1), lambda b: (b, 0, 0, 0))]
                 + [_const_spec(a.shape) for a in consts],
        out_specs=[pl.BlockSpec((1, seq_len, DB), lambda b: (b, 0, 0)),
                   pl.BlockSpec((1, 2, H_B, HEAD_B, HEAD_B), lambda b: (b, 0, 0, 0, 0)),
                   pl.BlockSpec((1, 2, H_B, HEAD_B), lambda b: (b, 0, 0, 0)),
                   pl.BlockSpec((1, 2, H_B, LANES), lambda b: (b, 0, 0, 0))],
        out_shape=[jax.ShapeDtypeStruct((bsz, seq_len, DB), F32),
                   jax.ShapeDtypeStruct((bsz, 2, H_B, HEAD_B, HEAD_B), F32),
                   jax.ShapeDtypeStruct((bsz, 2, H_B, HEAD_B), F32),
                   jax.ShapeDtypeStruct((bsz, 2, H_B, LANES), F32)],
        scratch_shapes=[pltpu.VMEM((seq_len, 2 * DB), F32), pltpu.VMEM((seq_len, DB), F32),
                        pltpu.VMEM((H_B, HEAD_B, HEAD_B), F32), pltpu.VMEM((H_B, HEAD_B), F32),
                        pltpu.VMEM((H_B, LANES), F32)],
        compiler_params=_params("arbitrary"),
        name="mlstm_scan",
    )(zm, g, gt, c0, n0, m0, *consts)


def _outproj_kernel(x_ref, yr_ref, ym_ref, mod_ref, g_ref, wo_ref, rw_ref, rb_ref,
                    x1_ref, hn_ref, idx_ref, gate_ref):
    mod = mod_ref[0]
    g1 = mod[:, 2 * D_MODEL:3 * D_MODEL]
    sh2, sc2 = mod[:, 3 * D_MODEL:4 * D_MODEL], mod[:, 4 * D_MODEL:5 * D_MODEL]
    mix = _mm_bf16(yr_ref[...], wo_ref[0:DA, :]) + _mm_bf16(ym_ref[...], wo_ref[DA:, :])
    x1 = x_ref[...] + g1 * mix
    x1_ref[...] = x1
    y = x1 * lax.rsqrt(jnp.mean(x1 * x1, axis=-1, keepdims=True) + NORM_EPS)
    hn = y * g_ref[...] * (1.0 + sc2) + sh2
    hn_ref[...] = hn
    logits = _mm(hn, rw_ref[...]) + rb_ref[...]
    lane = lax.broadcasted_iota(jnp.int32, logits.shape, 1)
    vals, idxs = [], []
    for _ in range(TOP_K):
        top = jnp.max(logits, axis=-1, keepdims=True)
        pick = jnp.min(jnp.where(logits == top, lane, N_EXPERTS), axis=-1, keepdims=True)
        vals.append(top)
        idxs.append(pick)
        logits = jnp.where(lane == pick, -jnp.inf, logits)
    exps = [jnp.exp(v - vals[0]) for v in vals]
    total = exps[0] + exps[1] + exps[2] + exps[3]
    k_lane = lax.broadcasted_iota(jnp.int32, (x1.shape[0], TOP_K), 1)
    gates = jnp.zeros((x1.shape[0], TOP_K), F32)
    picks = jnp.zeros((x1.shape[0], TOP_K), jnp.int32)
    for j in range(TOP_K):
        gates = jnp.where(k_lane == j, exps[j] / total, gates)
        picks = jnp.where(k_lane == j, idxs[j], picks)
    idx_ref[...] = picks
    gate_ref[...] = gates


def _outproj(x2, yr, ym, mod, norm_g, w_out, router_w, router_b):
    rows = x2.shape[0]
    tiles_per_mod = rows // mod.shape[0] // ROW_TILE
    row_spec = lambda w: pl.BlockSpec((ROW_TILE, w), lambda i: (i, 0))
    return pl.pallas_call(
        _outproj_kernel,
        grid=(rows // ROW_TILE,),
        in_specs=[row_spec(D_MODEL), row_spec(DA), row_spec(DB),
                  pl.BlockSpec((1, 1, 6 * D_MODEL), lambda i: (i // tiles_per_mod, 0, 0)),
                  _const_spec((1, D_MODEL)), _const_spec(w_out.shape), _const_spec(router_w.shape),
                  _const_spec((1, N_EXPERTS))],
        out_specs=[row_spec(D_MODEL), row_spec(D_MODEL), row_spec(TOP_K), row_spec(TOP_K)],
        out_shape=[jax.ShapeDtypeStruct((rows, D_MODEL), F32), jax.ShapeDtypeStruct((rows, D_MODEL), F32),
                   jax.ShapeDtypeStruct((rows, TOP_K), jnp.int32), jax.ShapeDtypeStruct((rows, TOP_K), F32)],
        compiler_params=_params("arbitrary"),
        name="outproj_router",
    )(x2, yr, ym, mod, norm_g.reshape(1, D_MODEL), w_out, router_w, router_b.reshape(1, N_EXPERTS))


def _rank_kernel(idx_ref, rank_ref, count_ref, run_s):
    i = pl.program_id(0)

    @pl.when(i == 0)
    def _():
        run_s[...] = jnp.zeros_like(run_s)

    idx = idx_ref[...]
    rows = idx.shape[0]
    lane = lax.broadcasted_iota(jnp.int32, (rows, N_EXPERTS), 1)
    hot = jnp.zeros((rows, N_EXPERTS), F32)
    for j in range(TOP_K):
        hot = hot + jnp.where(lane == idx[:, j:j + 1], 1.0, 0.0)
    before = _mm_bf16(_tri(rows, False, True).astype(F32), hot) + run_s[...]
    k_lane = lax.broadcasted_iota(jnp.int32, (rows, TOP_K), 1)
    rank = jnp.zeros((rows, TOP_K), F32)
    for j in range(TOP_K):
        rj = jnp.sum(jnp.where(lane == idx[:, j:j + 1], before, 0.0), axis=-1, keepdims=True)
        rank = jnp.where(k_lane == j, rj, rank)
    rank_ref[...] = rank.astype(jnp.int32)
    run_s[...] = run_s[...] + jnp.sum(hot, axis=0, keepdims=True)
    count_ref[...] = run_s[...].astype(jnp.int32)


def _expert_ranks(idx):
    rows = idx.shape[0]
    return pl.pallas_call(
        _rank_kernel,
        grid=(rows // ROW_TILE,),
        in_specs=[pl.BlockSpec((ROW_TILE, TOP_K), lambda i: (i, 0))],
        out_specs=[pl.BlockSpec((ROW_TILE, TOP_K), lambda i: (i, 0)), _const_spec((1, N_EXPERTS))],
        out_shape=[jax.ShapeDtypeStruct((rows, TOP_K), jnp.int32),
                   jax.ShapeDtypeStruct((1, N_EXPERTS), jnp.int32)],
        scratch_shapes=[pltpu.VMEM((1, N_EXPERTS), F32)],
        compiler_params=_params("arbitrary"),
        name="expert_ranks",
    )(idx)


def _dispatch_kernel(dest_ref, x_ref, init_ref, xs_ref, sem):
    del init_ref

    def copy(r, slot):
        return pltpu.make_async_copy(x_ref.at[r], xs_ref.at[slot], sem)

    def issue(r, carry):
        for j in range(TOP_K):
            copy(r, dest_ref[r * TOP_K + j]).start()
        return carry

    lax.fori_loop(0, ROW_TILE, issue, 0)

    def drain(r, carry):
        for j in range(TOP_K):
            copy(0, 0).wait()
        return carry

    lax.fori_loop(0, ROW_TILE, drain, 0)


def _dispatch(dest_flat, x3, n_slots):
    rows = x3.shape[0]
    init = jnp.zeros((n_slots,) + x3.shape[1:], x3.dtype)
    return pl.pallas_call(
        _dispatch_kernel,
        grid=(rows // ROW_TILE,),
        in_specs=[pl.BlockSpec((ROW_TILE * TOP_K,), lambda i: (i,), memory_space=pltpu.SMEM),
                  pl.BlockSpec((ROW_TILE,) + x3.shape[1:], lambda i: (i, 0, 0)),
                  pl.BlockSpec(memory_space=pl.ANY)],
        out_specs=pl.BlockSpec(memory_space=pl.ANY),
        out_shape=jax.ShapeDtypeStruct(init.shape, init.dtype),
        scratch_shapes=[pltpu.SemaphoreType.DMA(())],
        input_output_aliases={2: 0},
        compiler_params=_params("arbitrary"),
        name="moe_dispatch",
    )(dest_flat, x3, init)


def _expert_kernel(be_ref, nb_ref, xs_ref, w1_ref, b1_ref, w2_ref, b2_ref, y_ref, w1_s, w2_s):
    i = pl.program_id(0)
    changed = jnp.logical_or(i == 0, be_ref[i] != be_ref[jnp.maximum(i - 1, 0)])

    @pl.when(jnp.logical_and(changed, i < nb_ref[0]))
    def _():
        w1_s[...] = w1_ref[0].astype(BF16)
        w2_s[...] = w2_ref[0].astype(BF16)

    @pl.when(i < nb_ref[0])
    def _():
        xb = xs_ref[...].astype(BF16)
        hu = lax.dot_general(xb, w1_s[...], (((1,), (0,)), ((), ())), preferred_element_type=F32) + b1_ref[0]
        glu = jnp.minimum(hu[:, :D_FF], SWIGLU_LIMIT)
        lin = jnp.clip(hu[:, D_FF:], -SWIGLU_LIMIT, SWIGLU_LIMIT)
        act = glu * _sigmoid(SWIGLU_ALPHA * glu) * (lin + 1.0)
        y_ref[...] = lax.dot_general(act.astype(BF16), w2_s[...], (((1,), (0,)), ((), ())),
                                     preferred_element_type=F32) + b2_ref[0]

    @pl.when(i >= nb_ref[0])
    def _():
        y_ref[...] = jnp.zeros_like(y_ref)


def _experts(block_e, n_used, xs, w1, b1, w2, b2):
    n_blocks = xs.shape[0] // MOE_BLOCK
    grid_spec = pltpu.PrefetchScalarGridSpec(
        num_scalar_prefetch=2,
        grid=(n_blocks,),
        in_specs=[pl.BlockSpec((MOE_BLOCK, D_MODEL), lambda i, be, nb: (i, 0)),
                  pl.BlockSpec((1, D_MODEL, 2 * D_FF), lambda i, be, nb: (be[i], 0, 0)),
                  pl.BlockSpec((1, 1, 2 * D_FF), lambda i, be, nb: (be[i], 0, 0)),
                  pl.BlockSpec((1, D_FF, D_MODEL), lambda i, be, nb: (be[i], 0, 0)),
                  pl.BlockSpec((1, 1, D_MODEL), lambda i, be, nb: (be[i], 0, 0))],
        out_specs=pl.BlockSpec((MOE_BLOCK, D_MODEL), lambda i, be, nb: (i, 0)),
        scratch_shapes=[pltpu.VMEM((D_MODEL, 2 * D_FF), BF16), pltpu.VMEM((D_FF, D_MODEL), BF16)],
    )
    return pl.pallas_call(
        _expert_kernel,
        grid_spec=grid_spec,
        out_shape=jax.ShapeDtypeStruct(xs.shape, F32),
        compiler_params=_params("arbitrary"),
        name="moe_experts",
    )(block_e, n_used, xs, w1, b1.reshape(N_EXPERTS, 1, 2 * D_FF), w2, b2.reshape(N_EXPERTS, 1, D_MODEL))


def _combine_kernel(dest_ref, yb_ref, x1_ref, gate_ref, mod_ref, fg_ref, o_ref, buf, sem):
    def copy(slot, j, r):
        return pltpu.make_async_copy(yb_ref.at[slot], buf.at[j, r], sem)

    def issue(r, carry):
        for j in range(TOP_K):
            copy(dest_ref[r * TOP_K + j], j, r).start()
        return carry

    lax.fori_loop(0, ROW_TILE, issue, 0)

    def drain(r, carry):
        for j in range(TOP_K):
            copy(0, 0, 0).wait()
        return carry

    lax.fori_loop(0, ROW_TILE, drain, 0)

    g2 = mod_ref[0, 5 * LANE_ROWS:6 * LANE_ROWS, :]
    gates = gate_ref[...]
    moe = jnp.zeros((ROW_TILE, LANE_ROWS, LANES), F32)
    for j in range(TOP_K):
        moe = moe + buf[j] * gates[:, j:j + 1, :]
    x2 = x1_ref[...] + g2 * moe
    ms = jnp.sum(jnp.sum(x2 * x2, axis=2, keepdims=True), axis=1, keepdims=True) * (1.0 / D_MODEL)
    o_ref[...] = x2 * lax.rsqrt(ms + NORM_EPS) * fg_ref[...]


def _combine(dest_flat, yb3, x1_3, gates_rep, mod, final_g, row_offset):
    rows = x1_3.shape[0]
    tiles_per_mod = rows // mod.shape[0] // ROW_TILE
    tile_offset = row_offset // ROW_TILE
    row_spec = pl.BlockSpec((ROW_TILE, LANE_ROWS, LANES), lambda i: (i, 0, 0))
    return pl.pallas_call(
        _combine_kernel,
        grid=(rows // ROW_TILE,),
        in_specs=[pl.BlockSpec((ROW_TILE * TOP_K,), lambda i: (i + tile_offset,), memory_space=pltpu.SMEM),
                  pl.BlockSpec(memory_space=pl.ANY),
                  row_spec,
                  pl.BlockSpec((ROW_TILE, TOP_K, LANES), lambda i: (i, 0, 0)),
                  pl.BlockSpec((1, 6 * LANE_ROWS, LANES), lambda i: (i // tiles_per_mod, 0, 0)),
                  _const_spec((LANE_ROWS, LANES))],
        out_specs=row_spec,
        out_shape=jax.ShapeDtypeStruct((rows, LANE_ROWS, LANES), F32),
        scratch_shapes=[pltpu.VMEM((TOP_K, ROW_TILE, LANE_ROWS, LANES), F32),
                        pltpu.SemaphoreType.DMA(())],
        compiler_params=_params("arbitrary"),
        name="moe_combine",
    )(dest_flat, yb3, x1_3, gates_rep, mod.reshape(mod.shape[0], 6 * LANE_ROWS, LANES),
      final_g.reshape(LANE_ROWS, LANES))


def _block_diag_ones(width, block):
    i = jnp.arange(width) // block
    return (i[:, None] == i[None, :]).astype(F32)


def _mixers(x, mod, lp, s0, c0, n0, m0, grid):
    bsz, seq_len, _ = x.shape
    nc = seq_len // CHUNK
    x2 = x.reshape(bsz * seq_len, D_MODEL)
    zr, zm, zg, zgt = _inproj(x2, mod, lp['norm1_g'], lp['w_r'], lp['w_m'], lp['w_g'], lp['w_gt'])
    gt = zgt.reshape(N_GATES, bsz, nc, CHUNK).transpose(1, 2, 0, 3)
    yr, s_new = _rwkv(zr.reshape(bsz, seq_len, RWKV_COLS), s0, lp, grid)
    ym, c_new, n_new, m_new = _mlstm(zm.reshape(bsz, seq_len, MLSTM_MAIN), zg.reshape(bsz, seq_len, N_GATES), gt,
                                     c0, n0, m0.reshape(bsz, 2, H_B, 1), lp)
    x1, hn, idx, gates = _outproj(x2, yr.reshape(-1, DA), ym.reshape(-1, DB), mod, lp['norm2_g'], lp['w_out'],
                                  lp['router_w'], lp['router_b'])
    return x1, hn, idx, gates, (s_new, c_new, n_new, m_new[..., 0])


def kernel(x_prompt, x_sample, state_rwkv, state_mlstm_C, state_mlstm_n, state_mlstm_m, c, c_ctx, ada_w, ada_b, norm1_g, norm2_g, w_in, w_out, rwkv_mu, rwkv_w0, rwkv_w2, rwkv_a0, rwkv_a2, rwkv_g2, rwkv_kk, rwkv_ka, rwkv_rk, rwkv_gn_w, rwkv_gn_b, mlstm_conv, mlstm_bi, mlstm_bf, mlstm_norm_g, router_w, router_b, moe_w1, moe_b1, moe_w2, moe_b2, final_g):
    bp, lp_len, _ = x_prompt.shape
    bs, ls_len, _ = x_sample.shape
    w = w_in[0]
    lp = {
        'norm1_g': norm1_g[0], 'norm2_g': norm2_g[0],
        'w_r': w[:, :RWKV_COLS].astype(BF16),
        'w_m': w[:, RWKV_COLS:RWKV_COLS + MLSTM_MAIN].astype(BF16),
        'w_g': w[:, RWKV_COLS + MLSTM_MAIN:],
        'w_gt': w[:, RWKV_COLS + MLSTM_MAIN:].T,
        'w_out': w_out[0].astype(BF16),
        'mu': rwkv_mu[0].reshape(1, RWKV_COLS), 'kkw': rwkv_kk[0].reshape(1, DA), 'ka': rwkv_ka[0].reshape(1, DA),
        'rk': rwkv_rk[0].reshape(1, DA), 'gnw': rwkv_gn_w[0].reshape(1, DA), 'gnb': rwkv_gn_b[0].reshape(1, DA),
        'w0': rwkv_w0[0], 'w2': rwkv_w2[0], 'a0': rwkv_a0[0], 'a2': rwkv_a2[0], 'g2': rwkv_g2[0],
        'bd64': _block_diag_ones(DA, HEAD_A),
        'conv': mlstm_conv[0], 'bi': mlstm_bi[0].reshape(1, 2 * H_B), 'bit': mlstm_bi[0].reshape(2 * H_B, 1),
        'bf': mlstm_bf[0].reshape(1, 2 * H_B), 'bft': mlstm_bf[0].reshape(2 * H_B, 1),
        'ng': mlstm_norm_g[0].reshape(1, DB),
        'router_w': router_w[0], 'router_b': router_b[0],
    }
    cc = jnp.concatenate([c_ctx[None, :], c, jnp.zeros((2 * SUBLANES - 1 - bs, D_MODEL), F32)], axis=0)
    mod = _ada_mod(cc, ada_w[0], ada_b[0])
    mod_p = mod[0:1].reshape(1, 1, 6 * D_MODEL)
    mod_s = mod[1:1 + bs].reshape(bs, 1, 6 * D_MODEL)

    zeros = lambda *shape: jnp.zeros(shape, F32)
    x1p, hnp, idxp, gatesp, st = _mixers(x_prompt, mod_p, lp, zeros(bp, 2, H_A, HEAD_A, HEAD_A),
                                         zeros(bp, 2, H_B, HEAD_B, HEAD_B), zeros(bp, 2, H_B, HEAD_B),
                                         zeros(bp, 2, H_B), False)
    x1s, hns, idxs, gatess, _ = _mixers(x_sample, mod_s, lp, state_rwkv[:, 0], state_mlstm_C[:, 0],
                                        state_mlstm_n[:, 0], state_mlstm_m[:, 0], True)

    n_p = bp * lp_len
    idx = jnp.concatenate([idxp, idxs], axis=0)
    hn = jnp.concatenate([hnp, hns], axis=0)
    n_tok = idx.shape[0]
    rank, counts = _expert_ranks(idx)
    counts = counts[0]
    padded = (counts + MOE_BLOCK - 1) // MOE_BLOCK * MOE_BLOCK
    pad_end = jnp.cumsum(padded)
    pad_start = pad_end - padded
    n_blocks = n_tok * TOP_K // MOE_BLOCK + N_EXPERTS
    block_e = jnp.minimum(jnp.searchsorted(pad_end, jnp.arange(n_blocks) * MOE_BLOCK, side='right'),
                          N_EXPERTS - 1).astype(jnp.int32)
    n_used = (pad_end[-1:] // MOE_BLOCK).astype(jnp.int32)
    dest = (pad_start[idx] + rank).astype(jnp.int32).reshape(-1)
    xs = _dispatch(dest, hn.reshape(n_tok, LANE_ROWS, LANES), n_blocks * MOE_BLOCK)
    yb = _experts(block_e, n_used, xs.reshape(-1, D_MODEL), moe_w1[0], moe_b1[0], moe_w2[0], moe_b2[0])
    yb3 = yb.reshape(-1, LANE_ROWS, LANES)
    rep = lambda g: jnp.broadcast_to(g[:, :, None], g.shape + (LANES,))
    as_tiles = lambda x: x.reshape(-1, LANE_ROWS, LANES)
    y_prompt = _combine(dest, yb3, as_tiles(x1p), rep(gatesp), mod_p, final_g, 0).reshape(x_prompt.shape)
    y_sample = _combine(dest, yb3, as_tiles(x1s), rep(gatess), mod_s, final_g, n_p).reshape(x_sample.shape)

    s_new, c_new, n_new, m_new = st
    return (y_prompt, y_sample, s_new[:, None], c_new[:, None], n_new[:, None], m_new[:, None])
```

```python
import functools

import jax
import jax.numpy as jnp
from jax import lax
from jax.experimental import pallas as pl
from jax.experimental.pallas import tpu as pltpu

F32 = jnp.float32
BF16 = jnp.bfloat16
HIGHEST = lax.Precision.HIGHEST

D_MODEL = 1024
DA = 512
HEAD_A = 64
H_A = DA // HEAD_A
DB = 512
H_B = 4
HEAD_B = DB // H_B
DECAY_LORA = 64
AAA_LORA = 64
GATE_LORA = 128
RWKV_COLS = 3 * DA + DECAY_LORA + AAA_LORA + GATE_LORA
MLSTM_MAIN = 4 * DB
N_GATES = 4 * H_B
GRID_W = 64
CHUNK = 64
N_EXPERTS = 32
TOP_K = 4
D_FF = D_MODEL
SWIGLU_LIMIT = 7.0
SWIGLU_ALPHA = 1.702
MOE_BLOCK = 256
NORM_EPS = 1e-6
GN_EPS = 64e-5
ROW_TILE = 256
SUBLANES = 8
LANES = 128
LANE_ROWS = D_MODEL // LANES
VMEM_LIMIT = 56 * 1024 * 1024
PACK = 4
PACK_W = PACK * HEAD_A
N_PACKS = H_A // PACK

NN = (((1,), (0,)), ((), ()))
NT = (((1,), (1,)), ((), ()))
TN = (((0,), (0,)), ((), ()))


def _mm(a, b, precision=HIGHEST):
    return lax.dot_general(a, b, NN, precision=precision, preferred_element_type=F32)


def _mm_nt(a, b, precision=HIGHEST):
    return lax.dot_general(a, b, NT, precision=precision, preferred_element_type=F32)


def _mm_tn(a, b, precision=HIGHEST):
    return lax.dot_general(a, b, TN, precision=precision, preferred_element_type=F32)


def _dot_bf16(a, b, dims=NN):
    return lax.dot_general(a.astype(BF16), b.astype(BF16), dims, preferred_element_type=F32)


def _sigmoid(x):
    return 1.0 / (1.0 + jnp.exp(-x))


def _log_sigmoid(x):
    return jnp.minimum(x, 0.0) - jnp.log(1.0 + jnp.exp(-jnp.abs(x)))


def _params(*sem):
    return pltpu.CompilerParams(dimension_semantics=sem, vmem_limit_bytes=VMEM_LIMIT)


def _const_spec(shape):
    nd = len(shape)
    return pl.BlockSpec(shape, lambda *_: (0,) * nd)


def _tri(n, reverse, strict):
    t = lax.broadcasted_iota(jnp.int32, (n, n), 0)
    s = lax.broadcasted_iota(jnp.int32, (n, n), 1)
    if reverse:
        return (s > t) if strict else (s >= t)
    return (s < t) if strict else (s <= t)


def _seq_neighbours(ref, b, c, nc, cols):
    r0 = pl.multiple_of(c * CHUNK, CHUNK)
    zc = ref[b, pl.ds(r0, CHUNK), cols]
    row = lax.broadcasted_iota(jnp.int32, zc.shape, 0)
    p0 = pl.multiple_of(jnp.maximum(r0 - SUBLANES, 0), SUBLANES)
    n0 = pl.multiple_of(jnp.minimum(r0 + CHUNK, (nc - 1) * CHUNK), SUBLANES)
    before = ref[b, pl.ds(p0, SUBLANES), cols][SUBLANES - 1:SUBLANES]
    after = ref[b, pl.ds(n0, SUBLANES), cols][0:1]
    before = jnp.where(c > 0, before, 0.0)
    after = jnp.where(c < nc - 1, after, 0.0)
    prev = jnp.where(row == 0, before, pltpu.roll(zc, 1, 0))
    nxt = jnp.where(row == CHUNK - 1, after, pltpu.roll(zc, CHUNK - 1, 0))
    return zc, prev, nxt


def _ada_kernel(c_ref, w_ref, b_ref, o_ref):
    cc = c_ref[...]
    o_ref[...] = _mm(cc * _sigmoid(cc), w_ref[...]) + b_ref[...]


def _ada_mod(cc, ada_w, ada_b):
    rows = cc.shape[0]
    ncol = ada_w.shape[1]
    tn = 1536
    return pl.pallas_call(
        _ada_kernel,
        grid=(ncol // tn,),
        in_specs=[_const_spec((rows, D_MODEL)),
                  pl.BlockSpec((D_MODEL, tn), lambda j: (0, j)),
                  pl.BlockSpec((1, tn), lambda j: (0, j))],
        out_specs=pl.BlockSpec((rows, tn), lambda j: (0, j)),
        out_shape=jax.ShapeDtypeStruct((rows, ncol), F32),
        compiler_params=_params("arbitrary"),
        name="ada_mod",
    )(cc, ada_w, ada_b.reshape(1, ncol))


def _inproj_kernel(x_ref, mod_ref, g_ref, wr_ref, wm_ref, wg_ref, wgt_ref, zr_ref, zm_ref, zg_ref, zgt_ref):
    x = x_ref[...]
    mod = mod_ref[0]
    sh, sc = mod[:, 0:D_MODEL], mod[:, D_MODEL:2 * D_MODEL]
    y = x * lax.rsqrt(jnp.mean(x * x, axis=-1, keepdims=True) + NORM_EPS)
    hn = y * g_ref[...] * (1.0 + sc) + sh
    hb = hn.astype(BF16)
    zr_ref[...] = lax.dot_general(hb, wr_ref[...], NN, preferred_element_type=F32)
    zm_ref[...] = lax.dot_general(hb, wm_ref[...], NN, preferred_element_type=F32)
    zg_ref[...] = _mm(hn, wg_ref[...])
    zgt_ref[...] = _mm_nt(wgt_ref[...], hn)


def _inproj(x2, mod, norm_g, w_r, w_m, w_g, w_gt):
    rows = x2.shape[0]
    tiles_per_mod = rows // mod.shape[0] // ROW_TILE
    return pl.pallas_call(
        _inproj_kernel,
        grid=(rows // ROW_TILE,),
        in_specs=[pl.BlockSpec((ROW_TILE, D_MODEL), lambda i: (i, 0)),
                  pl.BlockSpec((1, 1, 6 * D_MODEL), lambda i: (i // tiles_per_mod, 0, 0)),
                  _const_spec((1, D_MODEL)),
                  _const_spec(w_r.shape), _const_spec(w_m.shape), _const_spec(w_g.shape),
                  _const_spec(w_gt.shape)],
        out_specs=[pl.BlockSpec((ROW_TILE, RWKV_COLS), lambda i: (i, 0)),
                   pl.BlockSpec((ROW_TILE, MLSTM_MAIN), lambda i: (i, 0)),
                   pl.BlockSpec((ROW_TILE, N_GATES), lambda i: (i, 0)),
                   pl.BlockSpec((N_GATES, ROW_TILE), lambda i: (0, i))],
        out_shape=[jax.ShapeDtypeStruct((rows, RWKV_COLS), F32),
                   jax.ShapeDtypeStruct((rows, MLSTM_MAIN), F32),
                   jax.ShapeDtypeStruct((rows, N_GATES), F32),
                   jax.ShapeDtypeStruct((N_GATES, rows), F32)],
        compiler_params=_params("arbitrary"),
        name="inproj",
    )(x2, mod, norm_g.reshape(1, D_MODEL), w_r, w_m, w_g, w_gt)


def _group_sum(x, ones_bd):
    hi = x.astype(BF16)
    lo = (x - hi.astype(F32)).astype(BF16)
    return (lax.dot_general(hi, ones_bd, NN, preferred_element_type=F32)
            + lax.dot_general(lo, ones_bd, NN, preferred_element_type=F32))


def _block_diag(x, ones_bd):
    return jnp.concatenate([x.astype(BF16)] * PACK, axis=0) * ones_bd


def _rwkv_kernel(zr_ref, s0_ref, mu_ref, kkw_ref, ka_ref, rk_ref, gnw_ref, gnb_ref, w0_ref, w2_ref,
                 a0_ref, a2_ref, g2_ref, ones_ref, y_ref, sout_ref,
                 r_s, k_s, v_s, kk_s, gate_s, lx_s, st_s, *, seq_len, grid):
    nc = seq_len // CHUNK
    ones_bd = ones_ref[...]

    def mix_chunk(c, carry):
        r0 = pl.multiple_of(c * CHUNK, CHUNK)
        if grid:
            zc = zr_ref[0, pl.ds(r0, CHUNK), :]
            row = lax.broadcasted_iota(jnp.int32, zc.shape, 0)
            up0 = pl.multiple_of(jnp.maximum(c - 1, 0) * CHUNK, CHUNK)
            dn0 = pl.multiple_of(jnp.minimum(c + 1, nc - 1) * CHUNK, CHUNK)
            up = jnp.where(c > 0, zr_ref[0, pl.ds(up0, CHUNK), :], 0.0)
            down = jnp.where(c < nc - 1, zr_ref[0, pl.ds(dn0, CHUNK), :], 0.0)
            left = jnp.where(row == 0, 0.0, pltpu.roll(zc, 1, 0))
            right = jnp.where(row == CHUNK - 1, 0.0, pltpu.roll(zc, CHUNK - 1, 0))
            local = 0.25 * (up + down + left + right)
        else:
            zc, prev, nxt = _seq_neighbours(zr_ref, 0, c, nc, slice(None))
            local = 0.5 * (prev + nxt)
        z = zc + (local - zc) * mu_ref[...]
        r, k, v = z[:, 0:DA], z[:, DA:2 * DA], z[:, 2 * DA:3 * DA]
        lx = z[:, 3 * DA:3 * DA + DECAY_LORA + AAA_LORA]
        xg = z[:, 3 * DA + DECAY_LORA + AAA_LORA:]
        kk = k * kkw_ref[...]
        sq = kk * kk
        ss = jnp.concatenate([_group_sum(sq[:, p * PACK_W:(p + 1) * PACK_W], ones_bd) for p in range(N_PACKS)],
                             axis=1)
        kk = kk / jnp.maximum(jnp.sqrt(ss), 1e-12)
        lane = lax.broadcasted_iota(jnp.int32, lx.shape, 1)
        r_s[pl.ds(r0, CHUNK), :] = r
        k_s[pl.ds(r0, CHUNK), :] = k
        v_s[pl.ds(r0, CHUNK), :] = v
        kk_s[pl.ds(r0, CHUNK), :] = kk
        gate_s[pl.ds(r0, CHUNK), :] = _mm(_sigmoid(xg), g2_ref[...])
        lx_s[pl.ds(r0, CHUNK), :] = jnp.where(lane < DECAY_LORA, jnp.tanh(lx), lx)
        return carry

    lax.fori_loop(0, nc, mix_chunk, 0)

    rowblk = lax.broadcasted_iota(jnp.int32, (PACK_W, PACK_W), 0) // HEAD_A
    colblk = lax.broadcasted_iota(jnp.int32, (PACK_W, PACK_W), 1) // HEAD_A
    for d in range(2):
        for p in range(N_PACKS):
            rows_ = jnp.concatenate([s0_ref[0, d, p * PACK + h] for h in range(PACK)], axis=0)
            st_s[d, p] = jnp.where(rowblk == colblk, jnp.concatenate([rows_] * PACK, axis=1), 0.0)

    t_idx = lax.broadcasted_iota(jnp.int32, (CHUNK, PACK_W), 0)
    s_idx = lax.broadcasted_iota(jnp.int32, (CHUNK, PACK_W), 1) % CHUNK

    def scan_step(i, carry):
        first_touch = i < nc // 2
        for d in range(2):
            reverse = d == 1
            c = nc - 1 - i if reverse else i
            rows = pl.ds(pl.multiple_of(c * CHUNK, CHUNK), CHUNK)
            strict = (s_idx > t_idx) if reverse else (s_idx < t_idx)
            incl = (s_idx >= t_idx) if reverse else (s_idx <= t_idx)
            last = 0 if reverse else CHUNK - 1
            r, k, v, kk = r_s[rows, :], k_s[rows, :], v_s[rows, :], kk_s[rows, :]
            lx = lx_s[rows, :]
            wl = w0_ref[d:d + 1, :] + _mm(lx[:, 0:DECAY_LORA], w2_ref[d])
            logw = -jnp.exp(_log_sigmoid(wl) - 0.5)
            a = _sigmoid(a0_ref[d:d + 1, :] + _mm(lx[:, DECAY_LORA:], a2_ref[d]))
            kd = k * (1.0 + (a - 1.0) * ka_ref[...])
            kb = kk * a
            cum = _mm(_tri(CHUNK, reverse, False).astype(F32), logw)
            clast = cum[last:last + 1]
            rt = r * jnp.exp(cum)
            at = -kk * jnp.exp(cum - logw)
            einv = jnp.exp(-cum)
            kt, bt = kd * einv, kb * einv
            edec = jnp.exp(clast - cum)
            kp, bp = kd * edec, kb * edec
            ptot = jnp.exp(clast)
            rkd = r * kd * rk_ref[...]
            outs = []
            for p in range(N_PACKS):
                cs = slice(p * PACK_W, (p + 1) * PACK_W)
                ar = jnp.concatenate([at[:, cs], rt[:, cs]], axis=0).astype(BF16)
                a_k = lax.dot_general(ar, _block_diag(kt[:, cs], ones_bd), NT, preferred_element_type=F32)
                a_b = lax.dot_general(ar, _block_diag(bt[:, cs], ones_bd), NT, preferred_element_type=F32)
                s_prev = st_s[d, p]
                ars = lax.dot_general(ar, s_prev.astype(BF16), NT, preferred_element_type=F32)
                a_ak = jnp.where(strict, a_k[:CHUNK], 0.0)
                a_rk = jnp.where(incl, a_k[CHUNK:], 0.0)
                npow = jnp.where(strict, a_b[:CHUNK], 0.0)
                a_rb = jnp.where(incl, a_b[CHUNK:], 0.0)
                v_bd = _block_diag(v[:, cs], ones_bd)
                u = ars[:CHUNK] + _dot_bf16(a_ak, v_bd)
                for j in range(6):
                    nb = npow.astype(BF16)
                    u = u + lax.dot_general(nb, _block_diag(u, ones_bd), NN, preferred_element_type=F32)
                    if j < 5:
                        npow = lax.dot_general(nb, _block_diag(npow, ones_bd), NN, preferred_element_type=F32)
                y = ars[CHUNK:] + _dot_bf16(a_rk, v_bd) + _dot_bf16(a_rb, _block_diag(u, ones_bd))
                vu = jnp.concatenate([v[:, cs], u], axis=0)
                kbp = jnp.concatenate([kp[:, cs], bp[:, cs]], axis=0)
                st_s[d, p] = s_prev * ptot[:, cs] + _dot_bf16(vu, kbp, TN) * ones_bd.astype(F32)
                mean = _group_sum(y, ones_bd) * (1.0 / HEAD_A)
                yc = y - mean
                var = _group_sum(yc * yc, ones_bd) * (1.0 / HEAD_A)
                out = yc * lax.rsqrt(var + GN_EPS) * gnw_ref[:, cs] + gnb_ref[:, cs]
                outs.append(out + _group_sum(rkd[:, cs], ones_bd) * v[:, cs])
            out = jnp.concatenate(outs, axis=1)

            @pl.when(first_touch)
            def _():
                y_ref[0, rows, :] = out

            @pl.when(jnp.logical_not(first_touch))
            def _():
                y_ref[0, rows, :] = (y_ref[0, rows, :] + out) * gate_s[rows, :]
        return carry

    lax.fori_loop(0, nc, scan_step, 0)

    for d in range(2):
        for p in range(N_PACKS):
            s_fin = st_s[d, p]
            for h in range(PACK):
                sout_ref[0, d, p * PACK + h] = s_fin[h * HEAD_A:(h + 1) * HEAD_A, h * HEAD_A:(h + 1) * HEAD_A]


def _rwkv(zr, s0, p, grid):
    bsz, seq_len, _ = zr.shape
    assert (seq_len // CHUNK) % 2 == 0
    kern = functools.partial(_rwkv_kernel, seq_len=seq_len, grid=grid)
    consts = [p['mu'], p['kkw'], p['ka'], p['rk'], p['gnw'], p['gnb'], p['w0'], p['w2'], p['a0'], p['a2'],
              p['g2'], p['ones_bd']]
    return pl.pallas_call(
        kern,
        grid=(bsz,),
        in_specs=[pl.BlockSpec((1, seq_len, RWKV_COLS), lambda b: (b, 0, 0)),
                  pl.BlockSpec((1, 2, H_A, HEAD_A, HEAD_A), lambda b: (b, 0, 0, 0, 0))]
                 + [_const_spec(a.shape) for a in consts],
        out_specs=[pl.BlockSpec((1, seq_len, DA), lambda b: (b, 0, 0)),
                   pl.BlockSpec((1, 2, H_A, HEAD_A, HEAD_A), lambda b: (b, 0, 0, 0, 0))],
        out_shape=[jax.ShapeDtypeStruct((bsz, seq_len, DA), F32),
                   jax.ShapeDtypeStruct((bsz, 2, H_A, HEAD_A, HEAD_A), F32)],
        scratch_shapes=[pltpu.VMEM((seq_len, DA), F32)] * 5
                       + [pltpu.VMEM((seq_len, DECAY_LORA + AAA_LORA), F32),
                          pltpu.VMEM((2, N_PACKS, PACK_W, PACK_W), F32)],
        compiler_params=_params("arbitrary"),
        name="rwkv_scan",
    )(zr, s0, *consts)


def _mlstm_kernel(zm_ref, g_ref, gt_ref, c0_ref, n0_ref, m0_ref, conv_ref, bi_ref, bit_ref, bf_ref, bft_ref,
                  ng_ref, y_ref, cout_ref, nout_ref, mout_ref,
                  qk_s, h_s, c_s, n_s, m_s, *, seq_len):
    nc = seq_len // CHUNK

    def conv_chunk(c, carry):
        zc, prev, nxt = _seq_neighbours(zm_ref, 0, c, nc, slice(0, 2 * DB))
        u = conv_ref[0:1, :] * prev + conv_ref[1:2, :] * zc + conv_ref[2:3, :] * nxt
        qk = u * _sigmoid(u)
        lane = lax.broadcasted_iota(jnp.int32, qk.shape, 1)
        qk_s[pl.ds(pl.multiple_of(c * CHUNK, CHUNK), CHUNK), :] = jnp.where(lane >= DB, qk * (HEAD_B ** -0.5), qk)
        return carry

    lax.fori_loop(0, nc, conv_chunk, 0)

    for d in range(2):
        reverse = d == 1
        incl = _tri(CHUNK, reverse, False)
        tri_col = incl.astype(F32)
        tri_row = _tri(CHUNK, not reverse, False).astype(F32)
        last = 0 if reverse else CHUNK - 1
        c_s[...] = c0_ref[0, d]
        n_s[...] = n0_ref[0, d]
        m_s[...] = jnp.broadcast_to(m0_ref[0, d], (H_B, LANES))
        gsl = slice(d * H_B, (d + 1) * H_B)
        fsl = slice(2 * H_B + d * H_B, 2 * H_B + (d + 1) * H_B)

        def scan_chunk(i, carry, reverse=reverse, incl=incl, tri_col=tri_col, tri_row=tri_row, last=last,
                       gsl=gsl, fsl=fsl):
            c = nc - 1 - i if reverse else i
            r0 = pl.multiple_of(c * CHUNK, CHUNK)
            rows = pl.ds(r0, CHUNK)
            gcol = g_ref[0, rows, :]
            grow = gt_ref[0, c]
            i_col = gcol[:, gsl] + bi_ref[:, gsl]
            f_col = _log_sigmoid(gcol[:, fsl] + bf_ref[:, gsl])
            i_row = grow[gsl, :] + bit_ref[gsl, :]
            f_row = _log_sigmoid(grow[fsl, :] + bft_ref[gsl, :])
            b_col = _mm(tri_col, f_col)
            b_row = _mm(f_row, tri_row)
            b_last = b_col[last:last + 1]
            for h in range(H_B):
                sl = slice(h * HEAD_B, (h + 1) * HEAD_B)
                q = qk_s[rows, h * HEAD_B:(h + 1) * HEAD_B]
                k = qk_s[rows, DB + h * HEAD_B:DB + (h + 1) * HEAD_B]
                v = zm_ref[0, rows, 2 * DB + h * HEAD_B:2 * DB + (h + 1) * HEAD_B]
                bc, br = b_col[:, h:h + 1], b_row[h:h + 1, :]
                ic, ir = i_col[:, h:h + 1], i_row[h:h + 1, :]
                m_prev = m_s[h:h + 1, 0:1]
                dmat = jnp.where(incl, bc - br + ir, -jnp.inf)
                inter = bc + m_prev
                m_t = jnp.maximum(inter, jnp.max(dmat, axis=-1, keepdims=True))
                s = _mm_nt(q, k) * jnp.exp(dmat - m_t)
                carry_w = jnp.exp(inter - m_t)
                c_prev, n_prev = c_s[h], n_s[h:h + 1, :]
                num = _mm(s, v) + carry_w * _mm(q, c_prev)
                den = jnp.sum(s, axis=-1, keepdims=True) + carry_w * jnp.sum(q * n_prev, axis=-1, keepdims=True)
                hout = num / jnp.maximum(jnp.abs(den), jnp.exp(-m_t))
                bl = b_last[:, h:h + 1]
                g = bl - bc + ic
                m_new = jnp.maximum(bl + m_prev, jnp.max(g, axis=0, keepdims=True))
                wk = jnp.exp(g - m_new)
                decay = jnp.exp(bl + m_prev - m_new)
                c_s[h] = decay * c_prev + _mm_tn(k, wk * v)
                n_s[h:h + 1, :] = decay * n_prev + jnp.sum(wk * k, axis=0, keepdims=True)
                m_s[h:h + 1, :] = jnp.broadcast_to(m_new, (1, LANES))
                if reverse:
                    h_s[rows, sl] = h_s[rows, sl] + hout
                else:
                    h_s[rows, sl] = hout
            return carry

        lax.fori_loop(0, nc, scan_chunk, 0)
        cout_ref[0, d] = c_s[...]
        nout_ref[0, d] = n_s[...]
        mout_ref[0, d] = m_s[...]

    def norm_chunk(c, carry):
        rows = pl.ds(pl.multiple_of(c * CHUNK, CHUNK), CHUNK)
        og = zm_ref[0, rows, 3 * DB:4 * DB]
        hm = h_s[rows, :]
        parts = []
        for h in range(H_B):
            hh = hm[:, h * HEAD_B:(h + 1) * HEAD_B]
            parts.append(hh * lax.rsqrt(jnp.mean(hh * hh, axis=-1, keepdims=True) + NORM_EPS))
        y_ref[0, rows, :] = jnp.concatenate(parts, axis=1) * ng_ref[...] * _sigmoid(og)
        return carry

    lax.fori_loop(0, nc, norm_chunk, 0)


def _mlstm(zm, g, gt, c0, n0, m0, p):
    bsz, seq_len, _ = zm.shape
    nc = seq_len // CHUNK
    kern = functools.partial(_mlstm_kernel, seq_len=seq_len)
    consts = [p['conv'], p['bi'], p['bit'], p['bf'], p['bft'], p['ng']]
    return pl.pallas_call(
        kern,
        grid=(bsz,),
        in_specs=[pl.BlockSpec((1, seq_len, MLSTM_MAIN), lambda b: (b, 0, 0)),
                  pl.BlockSpec((1, seq_len, N_GATES), lambda b: (b, 0, 0)),
                  pl.BlockSpec((1, nc, N_GATES, CHUNK), lambda b: (b, 0, 0, 0)),
                  pl.BlockSpec((1, 2, H_B, HEAD_B, HEAD_B), lambda b: (b, 0, 0, 0, 0)),
                  pl.BlockSpec((1, 2, H_B, HEAD_B), lambda b: (b, 0, 0, 0)),
                  pl.BlockSpec((1, 2, H_B, 1), lambda b: (b, 0, 0, 0))]
                 + [_const_spec(a.shape) for a in consts],
        out_specs=[pl.BlockSpec((1, seq_len, DB), lambda b: (b, 0, 0)),
                   pl.BlockSpec((1, 2, H_B, HEAD_B, HEAD_B), lambda b: (b, 0, 0, 0, 0)),
                   pl.BlockSpec((1, 2, H_B, HEAD_B), lambda b: (b, 0, 0, 0)),
                   pl.BlockSpec((1, 2, H_B, LANES), lambda b: (b, 0, 0, 0))],
        out_shape=[jax.ShapeDtypeStruct((bsz, seq_len, DB), F32),
                   jax.ShapeDtypeStruct((bsz, 2, H_B, HEAD_B, HEAD_B), F32),
                   jax.ShapeDtypeStruct((bsz, 2, H_B, HEAD_B), F32),
                   jax.ShapeDtypeStruct((bsz, 2, H_B, LANES), F32)],
        scratch_shapes=[pltpu.VMEM((seq_len, 2 * DB), F32), pltpu.VMEM((seq_len, DB), F32),
                        pltpu.VMEM((H_B, HEAD_B, HEAD_B), F32), pltpu.VMEM((H_B, HEAD_B), F32),
                        pltpu.VMEM((H_B, LANES), F32)],
        compiler_params=_params("arbitrary"),
        name="mlstm_scan",
    )(zm, g, gt, c0, n0, m0, *consts)


def _outproj_kernel(x_ref, yr_ref, ym_ref, mod_ref, g_ref, wo_ref, rw_ref, rb_ref,
                    x1_ref, hn_ref, idx_ref, gate_ref):
    mod = mod_ref[0]
    g1 = mod[:, 2 * D_MODEL:3 * D_MODEL]
    sh2, sc2 = mod[:, 3 * D_MODEL:4 * D_MODEL], mod[:, 4 * D_MODEL:5 * D_MODEL]
    mix = _dot_bf16(yr_ref[...], wo_ref[0:DA, :]) + _dot_bf16(ym_ref[...], wo_ref[DA:, :])
    x1 = x_ref[...] + g1 * mix
    x1_ref[...] = x1
    y = x1 * lax.rsqrt(jnp.mean(x1 * x1, axis=-1, keepdims=True) + NORM_EPS)
    hn = y * g_ref[...] * (1.0 + sc2) + sh2
    hn_ref[...] = hn
    logits = _mm(hn, rw_ref[...]) + rb_ref[...]
    lane = lax.broadcasted_iota(jnp.int32, logits.shape, 1)
    vals, idxs = [], []
    for _ in range(TOP_K):
        top = jnp.max(logits, axis=-1, keepdims=True)
        pick = jnp.min(jnp.where(logits == top, lane, N_EXPERTS), axis=-1, keepdims=True)
        vals.append(top)
        idxs.append(pick)
        logits = jnp.where(lane == pick, -jnp.inf, logits)
    exps = [jnp.exp(v - vals[0]) for v in vals]
    total = exps[0] + exps[1] + exps[2] + exps[3]
    k_lane = lax.broadcasted_iota(jnp.int32, (x1.shape[0], TOP_K), 1)
    gates = jnp.zeros((x1.shape[0], TOP_K), F32)
    picks = jnp.zeros((x1.shape[0], TOP_K), jnp.int32)
    for j in range(TOP_K):
        gates = jnp.where(k_lane == j, exps[j] / total, gates)
        picks = jnp.where(k_lane == j, idxs[j], picks)
    idx_ref[...] = picks
    gate_ref[...] = gates


def _outproj(x2, yr, ym, mod, norm_g, w_out, router_w, router_b):
    rows = x2.shape[0]
    tiles_per_mod = rows // mod.shape[0] // ROW_TILE
    row_spec = lambda w: pl.BlockSpec((ROW_TILE, w), lambda i: (i, 0))
    return pl.pallas_call(
        _outproj_kernel,
        grid=(rows // ROW_TILE,),
        in_specs=[row_spec(D_MODEL), row_spec(DA), row_spec(DB),
                  pl.BlockSpec((1, 1, 6 * D_MODEL), lambda i: (i // tiles_per_mod, 0, 0)),
                  _const_spec((1, D_MODEL)), _const_spec(w_out.shape), _const_spec(router_w.shape),
                  _const_spec((1, N_EXPERTS))],
        out_specs=[row_spec(D_MODEL), row_spec(D_MODEL), row_spec(TOP_K), row_spec(TOP_K)],
        out_shape=[jax.ShapeDtypeStruct((rows, D_MODEL), F32), jax.ShapeDtypeStruct((rows, D_MODEL), F32),
                   jax.ShapeDtypeStruct((rows, TOP_K), jnp.int32), jax.ShapeDtypeStruct((rows, TOP_K), F32)],
        compiler_params=_params("arbitrary"),
        name="outproj_router",
    )(x2, yr, ym, mod, norm_g.reshape(1, D_MODEL), w_out, router_w, router_b.reshape(1, N_EXPERTS))


def _rank_kernel(idx_ref, rank_ref, count_ref, run_s):
    i = pl.program_id(0)

    @pl.when(i == 0)
    def _():
        run_s[...] = jnp.zeros_like(run_s)

    idx = idx_ref[...]
    rows = idx.shape[0]
    lane = lax.broadcasted_iota(jnp.int32, (rows, N_EXPERTS), 1)
    hot = jnp.zeros((rows, N_EXPERTS), F32)
    for j in range(TOP_K):
        hot = hot + jnp.where(lane == idx[:, j:j + 1], 1.0, 0.0)
    before = _dot_bf16(_tri(rows, False, True).astype(F32), hot) + run_s[...]
    k_lane = lax.broadcasted_iota(jnp.int32, (rows, TOP_K), 1)
    rank = jnp.zeros((rows, TOP_K), F32)
    for j in range(TOP_K):
        rj = jnp.sum(jnp.where(lane == idx[:, j:j + 1], before, 0.0), axis=-1, keepdims=True)
        rank = jnp.where(k_lane == j, rj, rank)
    rank_ref[...] = rank.astype(jnp.int32)
    run_s[...] = run_s[...] + jnp.sum(hot, axis=0, keepdims=True)
    count_ref[...] = run_s[...].astype(jnp.int32)


def _expert_ranks(idx):
    rows = idx.shape[0]
    return pl.pallas_call(
        _rank_kernel,
        grid=(rows // ROW_TILE,),
        in_specs=[pl.BlockSpec((ROW_TILE, TOP_K), lambda i: (i, 0))],
        out_specs=[pl.BlockSpec((ROW_TILE, TOP_K), lambda i: (i, 0)), _const_spec((1, N_EXPERTS))],
        out_shape=[jax.ShapeDtypeStruct((rows, TOP_K), jnp.int32),
                   jax.ShapeDtypeStruct((1, N_EXPERTS), jnp.int32)],
        scratch_shapes=[pltpu.VMEM((1, N_EXPERTS), F32)],
        compiler_params=_params("arbitrary"),
        name="expert_ranks",
    )(idx)


def _dispatch_kernel(dest_ref, x_ref, init_ref, xs_ref, sem):
    del init_ref

    def copy(r, slot):
        return pltpu.make_async_copy(x_ref.at[r], xs_ref.at[slot], sem)

    def issue(r, carry):
        for j in range(TOP_K):
            copy(r, dest_ref[r * TOP_K + j]).start()
        return carry

    lax.fori_loop(0, ROW_TILE, issue, 0)

    def drain(r, carry):
        for j in range(TOP_K):
            copy(0, 0).wait()
        return carry

    lax.fori_loop(0, ROW_TILE, drain, 0)


def _dispatch(dest_flat, x3, n_slots):
    rows = x3.shape[0]
    init = jnp.zeros((n_slots,) + x3.shape[1:], x3.dtype)
    return pl.pallas_call(
        _dispatch_kernel,
        grid=(rows // ROW_TILE,),
        in_specs=[pl.BlockSpec((ROW_TILE * TOP_K,), lambda i: (i,), memory_space=pltpu.SMEM),
                  pl.BlockSpec((ROW_TILE,) + x3.shape[1:], lambda i: (i, 0, 0)),
                  pl.BlockSpec(memory_space=pl.ANY)],
        out_specs=pl.BlockSpec(memory_space=pl.ANY),
        out_shape=jax.ShapeDtypeStruct(init.shape, init.dtype),
        scratch_shapes=[pltpu.SemaphoreType.DMA(())],
        input_output_aliases={2: 0},
        compiler_params=_params("arbitrary"),
        name="moe_dispatch",
    )(dest_flat, x3, init)


def _expert_kernel(be_ref, nb_ref, xs_ref, w1_ref, b1_ref, w2_ref, b2_ref, y_ref, w1_s, w2_s):
    i = pl.program_id(0)
    changed = jnp.logical_or(i == 0, be_ref[i] != be_ref[jnp.maximum(i - 1, 0)])

    @pl.when(jnp.logical_and(changed, i < nb_ref[0]))
    def _():
        w1_s[...] = w1_ref[0].astype(BF16)
        w2_s[...] = w2_ref[0].astype(BF16)

    @pl.when(i < nb_ref[0])
    def _():
        xb = xs_ref[...].astype(BF16)
        hu = lax.dot_general(xb, w1_s[...], NN, preferred_element_type=F32) + b1_ref[0]
        glu = jnp.minimum(hu[:, :D_FF], SWIGLU_LIMIT)
        lin = jnp.clip(hu[:, D_FF:], -SWIGLU_LIMIT, SWIGLU_LIMIT)
        act = glu * _sigmoid(SWIGLU_ALPHA * glu) * (lin + 1.0)
        y_ref[...] = lax.dot_general(act.astype(BF16), w2_s[...], NN, preferred_element_type=F32) + b2_ref[0]

    @pl.when(i >= nb_ref[0])
    def _():
        y_ref[...] = jnp.zeros_like(y_ref)


def _experts(block_e, n_used, xs, w1, b1, w2, b2):
    n_blocks = xs.shape[0] // MOE_BLOCK
    grid_spec = pltpu.PrefetchScalarGridSpec(
        num_scalar_prefetch=2,
        grid=(n_blocks,),
        in_specs=[pl.BlockSpec((MOE_BLOCK, D_MODEL), lambda i, be, nb: (i, 0)),
                  pl.BlockSpec((1, D_MODEL, 2 * D_FF), lambda i, be, nb: (be[i], 0, 0)),
                  pl.BlockSpec((1, 1, 2 * D_FF), lambda i, be, nb: (be[i], 0, 0)),
                  pl.BlockSpec((1, D_FF, D_MODEL), lambda i, be, nb: (be[i], 0, 0)),
                  pl.BlockSpec((1, 1, D_MODEL), lambda i, be, nb: (be[i], 0, 0))],
        out_specs=pl.BlockSpec((MOE_BLOCK, D_MODEL), lambda i, be, nb: (i, 0)),
        scratch_shapes=[pltpu.VMEM((D_MODEL, 2 * D_FF), BF16), pltpu.VMEM((D_FF, D_MODEL), BF16)],
    )
    return pl.pallas_call(
        _expert_kernel,
        grid_spec=grid_spec,
        out_shape=jax.ShapeDtypeStruct(xs.shape, F32),
        compiler_params=_params("arbitrary"),
        name="moe_experts",
    )(block_e, n_used, xs, w1, b1.reshape(N_EXPERTS, 1, 2 * D_FF), w2, b2.reshape(N_EXPERTS, 1, D_MODEL))


def _combine_kernel(dest_ref, yb_ref, x1_ref, gate_ref, mod_ref, fg_ref, o_ref, buf, sem):
    def copy(slot, j, r):
        return pltpu.make_async_copy(yb_ref.at[slot], buf.at[j, r], sem)

    def issue(r, carry):
        for j in range(TOP_K):
            copy(dest_ref[r * TOP_K + j], j, r).start()
        return carry

    lax.fori_loop(0, ROW_TILE, issue, 0)

    def drain(r, carry):
        for j in range(TOP_K):
            copy(0, 0, 0).wait()
        return carry

    lax.fori_loop(0, ROW_TILE, drain, 0)

    g2 = mod_ref[0, 5 * LANE_ROWS:6 * LANE_ROWS, :]
    gates = gate_ref[...]
    moe = jnp.zeros((ROW_TILE, LANE_ROWS, LANES), F32)
    for j in range(TOP_K):
        moe = moe + buf[j] * gates[:, j:j + 1, :]
    x2 = x1_ref[...] + g2 * moe
    ms = jnp.sum(jnp.sum(x2 * x2, axis=2, keepdims=True), axis=1, keepdims=True) * (1.0 / D_MODEL)
    o_ref[...] = x2 * lax.rsqrt(ms + NORM_EPS) * fg_ref[...]


def _combine(dest_flat, yb3, x1_3, gates_rep, mod, final_g, row_offset):
    rows = x1_3.shape[0]
    tiles_per_mod = rows // mod.shape[0] // ROW_TILE
    tile_offset = row_offset // ROW_TILE
    row_spec = pl.BlockSpec((ROW_TILE, LANE_ROWS, LANES), lambda i: (i, 0, 0))
    return pl.pallas_call(
        _combine_kernel,
        grid=(rows // ROW_TILE,),
        in_specs=[pl.BlockSpec((ROW_TILE * TOP_K,), lambda i: (i + tile_offset,), memory_space=pltpu.SMEM),
                  pl.BlockSpec(memory_space=pl.ANY),
                  row_spec,
                  pl.BlockSpec((ROW_TILE, TOP_K, LANES), lambda i: (i, 0, 0)),
                  pl.BlockSpec((1, 6 * LANE_ROWS, LANES), lambda i: (i // tiles_per_mod, 0, 0)),
                  _const_spec((LANE_ROWS, LANES))],
        out_specs=row_spec,
        out_shape=jax.ShapeDtypeStruct((rows, LANE_ROWS, LANES), F32),
        scratch_shapes=[pltpu.VMEM((TOP_K, ROW_TILE, LANE_ROWS, LANES), F32),
                        pltpu.SemaphoreType.DMA(())],
        compiler_params=_params("arbitrary"),
        name="moe_combine",
    )(dest_flat, yb3, x1_3, gates_rep, mod.reshape(mod.shape[0], 6 * LANE_ROWS, LANES),
      final_g.reshape(LANE_ROWS, LANES))


def _block_diag_ones(width, block):
    i = jnp.arange(width) // block
    return (i[:, None] == i[None, :]).astype(F32)


def _mixers(x, mod, lp, s0, c0, n0, m0, grid):
    bsz, seq_len, _ = x.shape
    nc = seq_len // CHUNK
    x2 = x.reshape(bsz * seq_len, D_MODEL)
    zr, zm, zg, zgt = _inproj(x2, mod, lp['norm1_g'], lp['w_r'], lp['w_m'], lp['w_g'], lp['w_gt'])
    gt = zgt.reshape(N_GATES, bsz, nc, CHUNK).transpose(1, 2, 0, 3)
    yr, s_new = _rwkv(zr.reshape(bsz, seq_len, RWKV_COLS), s0, lp, grid)
    ym, c_new, n_new, m_new = _mlstm(zm.reshape(bsz, seq_len, MLSTM_MAIN), zg.reshape(bsz, seq_len, N_GATES), gt,
                                     c0, n0, m0.reshape(bsz, 2, H_B, 1), lp)
    x1, hn, idx, gates = _outproj(x2, yr.reshape(-1, DA), ym.reshape(-1, DB), mod, lp['norm2_g'], lp['w_out'],
                                  lp['router_w'], lp['router_b'])
    return x1, hn, idx, gates, (s_new, c_new, n_new, m_new[..., 0])


def kernel(x_prompt, x_sample, state_rwkv, state_mlstm_C, state_mlstm_n, state_mlstm_m, c, c_ctx, ada_w, ada_b, norm1_g, norm2_g, w_in, w_out, rwkv_mu, rwkv_w0, rwkv_w2, rwkv_a0, rwkv_a2, rwkv_g2, rwkv_kk, rwkv_ka, rwkv_rk, rwkv_gn_w, rwkv_gn_b, mlstm_conv, mlstm_bi, mlstm_bf, mlstm_norm_g, router_w, router_b, moe_w1, moe_b1, moe_w2, moe_b2, final_g):
    bp, lp_len, _ = x_prompt.shape
    bs, ls_len, _ = x_sample.shape
    w = w_in[0]
    lp = {
        'norm1_g': norm1_g[0], 'norm2_g': norm2_g[0],
        'w_r': w[:, :RWKV_COLS].astype(BF16),
        'w_m': w[:, RWKV_COLS:RWKV_COLS + MLSTM_MAIN].astype(BF16),
        'w_g': w[:, RWKV_COLS + MLSTM_MAIN:],
        'w_gt': w[:, RWKV_COLS + MLSTM_MAIN:].T,
        'w_out': w_out[0].astype(BF16),
        'mu': rwkv_mu[0].reshape(1, RWKV_COLS), 'kkw': rwkv_kk[0].reshape(1, DA), 'ka': rwkv_ka[0].reshape(1, DA),
        'rk': rwkv_rk[0].reshape(1, DA), 'gnw': rwkv_gn_w[0].reshape(1, DA), 'gnb': rwkv_gn_b[0].reshape(1, DA),
        'w0': rwkv_w0[0], 'w2': rwkv_w2[0], 'a0': rwkv_a0[0], 'a2': rwkv_a2[0], 'g2': rwkv_g2[0],
        'ones_bd': _block_diag_ones(PACK_W, HEAD_A).astype(BF16),
        'conv': mlstm_conv[0], 'bi': mlstm_bi[0].reshape(1, 2 * H_B), 'bit': mlstm_bi[0].reshape(2 * H_B, 1),
        'bf': mlstm_bf[0].reshape(1, 2 * H_B), 'bft': mlstm_bf[0].reshape(2 * H_B, 1),
        'ng': mlstm_norm_g[0].reshape(1, DB),
        'router_w': router_w[0], 'router_b': router_b[0],
    }
    cc = jnp.concatenate([c_ctx[None, :], c, jnp.zeros((2 * SUBLANES - 1 - bs, D_MODEL), F32)], axis=0)
    mod = _ada_mod(cc, ada_w[0], ada_b[0])
    mod_p = mod[0:1].reshape(1, 1, 6 * D_MODEL)
    mod_s = mod[1:1 + bs].reshape(bs, 1, 6 * D_MODEL)

    zeros = lambda *shape: jnp.zeros(shape, F32)
    x1p, hnp, idxp, gatesp, st = _mixers(x_prompt, mod_p, lp, zeros(bp, 2, H_A, HEAD_A, HEAD_A),
                                         zeros(bp, 2, H_B, HEAD_B, HEAD_B), zeros(bp, 2, H_B, HEAD_B),
                                         zeros(bp, 2, H_B), False)
    x1s, hns, idxs, gatess, _ = _mixers(x_sample, mod_s, lp, state_rwkv[:, 0], state_mlstm_C[:, 0],
                                        state_mlstm_n[:, 0], state_mlstm_m[:, 0], True)

    n_p = bp * lp_len
    idx = jnp.concatenate([idxp, idxs], axis=0)
    hn = jnp.concatenate([hnp, hns], axis=0)
    n_tok = idx.shape[0]
    rank, counts = _expert_ranks(idx)
    counts = counts[0]
    padded = (counts + MOE_BLOCK - 1) // MOE_BLOCK * MOE_BLOCK
    pad_end = jnp.cumsum(padded)
    pad_start = pad_end - padded
    n_blocks = n_tok * TOP_K // MOE_BLOCK + N_EXPERTS
    block_e = jnp.minimum(jnp.searchsorted(pad_end, jnp.arange(n_blocks) * MOE_BLOCK, side='right'),
                          N_EXPERTS - 1).astype(jnp.int32)
    n_used = (pad_end[-1:] // MOE_BLOCK).astype(jnp.int32)
    dest = (pad_start[idx] + rank).astype(jnp.int32).reshape(-1)
    xs = _dispatch(dest, hn.reshape(n_tok, LANE_ROWS, LANES), n_blocks * MOE_BLOCK)
    yb = _experts(block_e, n_used, xs.reshape(-1, D_MODEL), moe_w1[0], moe_b1[0], moe_w2[0], moe_b2[0])
    yb3 = yb.reshape(-1, LANE_ROWS, LANES)
    rep = lambda g: jnp.broadcast_to(g[:, :, None], g.shape + (LANES,))
    as_tiles = lambda x: x.reshape(-1, LANE_ROWS, LANES)
    y_prompt = _combine(dest, yb3, as_tiles(x1p), rep(gatesp), mod_p, final_g, 0).reshape(x_prompt.shape)
    y_sample = _combine(dest, yb3, as_tiles(x1s), rep(gatess), mod_s, final_g, n_p).reshape(x_sample.shape)

    s_new, c_new, n_new, m_new = st
    return (y_prompt, y_sample, s_new[:, None], c_new[:, None], n_new[:, None], m_new[:, None])
```

```python
import functools

import jax
import jax.numpy as jnp
from jax import lax
from jax.experimental import pallas as pl
from jax.experimental.pallas import tpu as pltpu

F32 = jnp.float32
BF16 = jnp.bfloat16
HIGHEST = lax.Precision.HIGHEST

D_MODEL = 1024
DA = 512
HEAD_A = 64
H_A = DA // HEAD_A
DB = 512
H_B = 4
HEAD_B = DB // H_B
DECAY_LORA = 64
AAA_LORA = 64
GATE_LORA = 128
RWKV_COLS = 3 * DA + DECAY_LORA + AAA_LORA + GATE_LORA
MLSTM_MAIN = 4 * DB
N_GATES = 4 * H_B
GRID_W = 64
CHUNK = 64
N_EXPERTS = 32
TOP_K = 4
D_FF = D_MODEL
SWIGLU_LIMIT = 7.0
SWIGLU_ALPHA = 1.702
MOE_BLOCK = 256
NORM_EPS = 1e-6
GN_EPS = 64e-5
ROW_TILE = 256
SUBLANES = 8
LANES = 128
LANE_ROWS = D_MODEL // LANES
VMEM_LIMIT = 56 * 1024 * 1024
PACK = 4
PACK_W = PACK * HEAD_A
N_PACKS = H_A // PACK

NN = (((1,), (0,)), ((), ()))
NT = (((1,), (1,)), ((), ()))
TN = (((0,), (0,)), ((), ()))


def _mm(a, b, precision=HIGHEST):
    return lax.dot_general(a, b, NN, precision=precision, preferred_element_type=F32)


def _mm_nt(a, b, precision=HIGHEST):
    return lax.dot_general(a, b, NT, precision=precision, preferred_element_type=F32)


def _mm_tn(a, b, precision=HIGHEST):
    return lax.dot_general(a, b, TN, precision=precision, preferred_element_type=F32)


def _dot_bf16(a, b, dims=NN):
    return lax.dot_general(a.astype(BF16), b.astype(BF16), dims, preferred_element_type=F32)


def _sigmoid(x):
    return 1.0 / (1.0 + jnp.exp(-x))


def _log_sigmoid(x):
    return jnp.minimum(x, 0.0) - jnp.log(1.0 + jnp.exp(-jnp.abs(x)))


def _params(*sem):
    return pltpu.CompilerParams(dimension_semantics=sem, vmem_limit_bytes=VMEM_LIMIT)


def _const_spec(shape):
    nd = len(shape)
    return pl.BlockSpec(shape, lambda *_: (0,) * nd)


def _tri(n, reverse, strict):
    t = lax.broadcasted_iota(jnp.int32, (n, n), 0)
    s = lax.broadcasted_iota(jnp.int32, (n, n), 1)
    if reverse:
        return (s > t) if strict else (s >= t)
    return (s < t) if strict else (s <= t)


def _seq_neighbours(ref, b, c, nc, cols):
    r0 = pl.multiple_of(c * CHUNK, CHUNK)
    zc = ref[b, pl.ds(r0, CHUNK), cols]
    row = lax.broadcasted_iota(jnp.int32, zc.shape, 0)
    p0 = pl.multiple_of(jnp.maximum(r0 - SUBLANES, 0), SUBLANES)
    n0 = pl.multiple_of(jnp.minimum(r0 + CHUNK, (nc - 1) * CHUNK), SUBLANES)
    before = ref[b, pl.ds(p0, SUBLANES), cols][SUBLANES - 1:SUBLANES]
    after = ref[b, pl.ds(n0, SUBLANES), cols][0:1]
    before = jnp.where(c > 0, before, 0.0)
    after = jnp.where(c < nc - 1, after, 0.0)
    prev = jnp.where(row == 0, before, pltpu.roll(zc, 1, 0))
    nxt = jnp.where(row == CHUNK - 1, after, pltpu.roll(zc, CHUNK - 1, 0))
    return zc, prev, nxt


def _ada_kernel(c_ref, w_ref, b_ref, o_ref):
    cc = c_ref[...]
    o_ref[...] = _dot_bf16(cc * _sigmoid(cc), w_ref[...]) + b_ref[...]


def _ada_mod(cc, ada_w, ada_b):
    rows = cc.shape[0]
    ncol = ada_w.shape[1]
    tn = 1536
    return pl.pallas_call(
        _ada_kernel,
        grid=(ncol // tn,),
        in_specs=[_const_spec((rows, D_MODEL)),
                  pl.BlockSpec((D_MODEL, tn), lambda j: (0, j)),
                  pl.BlockSpec((1, tn), lambda j: (0, j))],
        out_specs=pl.BlockSpec((rows, tn), lambda j: (0, j)),
        out_shape=jax.ShapeDtypeStruct((rows, ncol), F32),
        compiler_params=_params("arbitrary"),
        name="ada_mod",
    )(cc, ada_w, ada_b.reshape(1, ncol))


def _inproj_kernel(x_ref, mod_ref, g_ref, wr_ref, wm_ref, wg_ref, wgt_ref, zr_ref, zm_ref, zg_ref, zgt_ref):
    x = x_ref[...]
    mod = mod_ref[0]
    sh, sc = mod[:, 0:D_MODEL], mod[:, D_MODEL:2 * D_MODEL]
    y = x * lax.rsqrt(jnp.mean(x * x, axis=-1, keepdims=True) + NORM_EPS)
    hn = y * g_ref[...] * (1.0 + sc) + sh
    hb = hn.astype(BF16)
    zr_ref[...] = lax.dot_general(hb, wr_ref[...], NN, preferred_element_type=F32)
    zm_ref[...] = lax.dot_general(hb, wm_ref[...], NN, preferred_element_type=F32)
    zg_ref[...] = _dot_bf16(hb, wg_ref[...])
    zgt_ref[...] = _dot_bf16(wgt_ref[...], hb, NT)


def _inproj(x2, mod, norm_g, w_r, w_m, w_g, w_gt):
    rows = x2.shape[0]
    tiles_per_mod = rows // mod.shape[0] // ROW_TILE
    return pl.pallas_call(
        _inproj_kernel,
        grid=(rows // ROW_TILE,),
        in_specs=[pl.BlockSpec((ROW_TILE, D_MODEL), lambda i: (i, 0)),
                  pl.BlockSpec((1, 1, 6 * D_MODEL), lambda i: (i // tiles_per_mod, 0, 0)),
                  _const_spec((1, D_MODEL)),
                  _const_spec(w_r.shape), _const_spec(w_m.shape), _const_spec(w_g.shape),
                  _const_spec(w_gt.shape)],
        out_specs=[pl.BlockSpec((ROW_TILE, RWKV_COLS), lambda i: (i, 0)),
                   pl.BlockSpec((ROW_TILE, MLSTM_MAIN), lambda i: (i, 0)),
                   pl.BlockSpec((ROW_TILE, N_GATES), lambda i: (i, 0)),
                   pl.BlockSpec((N_GATES, ROW_TILE), lambda i: (0, i))],
        out_shape=[jax.ShapeDtypeStruct((rows, RWKV_COLS), F32),
                   jax.ShapeDtypeStruct((rows, MLSTM_MAIN), F32),
                   jax.ShapeDtypeStruct((rows, N_GATES), F32),
                   jax.ShapeDtypeStruct((N_GATES, rows), F32)],
        compiler_params=_params("arbitrary"),
        name="inproj",
    )(x2, mod, norm_g.reshape(1, D_MODEL), w_r, w_m, w_g, w_gt)


def _group_sum(x, ones_bd):
    hi = x.astype(BF16)
    lo = (x - hi.astype(F32)).astype(BF16)
    return (lax.dot_general(hi, ones_bd, NN, preferred_element_type=F32)
            + lax.dot_general(lo, ones_bd, NN, preferred_element_type=F32))


def _block_diag(x, ones_bd):
    return jnp.concatenate([x.astype(BF16)] * PACK, axis=0) * ones_bd


def _rwkv_kernel(zr_ref, s0_ref, mu_ref, kkw_ref, ka_ref, rk_ref, gnw_ref, gnb_ref, w0_ref, w2_ref,
                 a0_ref, a2_ref, g2_ref, ones_ref, y_ref, sout_ref,
                 r_s, k_s, v_s, kk_s, gate_s, lx_s, st_s, *, seq_len, grid):
    nc = seq_len // CHUNK
    ones_bd = ones_ref[...]

    def mix_chunk(c, carry):
        r0 = pl.multiple_of(c * CHUNK, CHUNK)
        if grid:
            zc = zr_ref[0, pl.ds(r0, CHUNK), :]
            row = lax.broadcasted_iota(jnp.int32, zc.shape, 0)
            up0 = pl.multiple_of(jnp.maximum(c - 1, 0) * CHUNK, CHUNK)
            dn0 = pl.multiple_of(jnp.minimum(c + 1, nc - 1) * CHUNK, CHUNK)
            up = jnp.where(c > 0, zr_ref[0, pl.ds(up0, CHUNK), :], 0.0)
            down = jnp.where(c < nc - 1, zr_ref[0, pl.ds(dn0, CHUNK), :], 0.0)
            left = jnp.where(row == 0, 0.0, pltpu.roll(zc, 1, 0))
            right = jnp.where(row == CHUNK - 1, 0.0, pltpu.roll(zc, CHUNK - 1, 0))
            local = 0.25 * (up + down + left + right)
        else:
            zc, prev, nxt = _seq_neighbours(zr_ref, 0, c, nc, slice(None))
            local = 0.5 * (prev + nxt)
        z = zc + (local - zc) * mu_ref[...]
        r, k, v = z[:, 0:DA], z[:, DA:2 * DA], z[:, 2 * DA:3 * DA]
        lx = z[:, 3 * DA:3 * DA + DECAY_LORA + AAA_LORA]
        xg = z[:, 3 * DA + DECAY_LORA + AAA_LORA:]
        kk = k * kkw_ref[...]
        sq = kk * kk
        ss = jnp.concatenate([_group_sum(sq[:, p * PACK_W:(p + 1) * PACK_W], ones_bd) for p in range(N_PACKS)],
                             axis=1)
        kk = kk / jnp.maximum(jnp.sqrt(ss), 1e-12)
        lane = lax.broadcasted_iota(jnp.int32, lx.shape, 1)
        r_s[pl.ds(r0, CHUNK), :] = r
        k_s[pl.ds(r0, CHUNK), :] = k
        v_s[pl.ds(r0, CHUNK), :] = v
        kk_s[pl.ds(r0, CHUNK), :] = kk
        gate_s[pl.ds(r0, CHUNK), :] = _dot_bf16(_sigmoid(xg), g2_ref[...])
        lx_s[pl.ds(r0, CHUNK), :] = jnp.where(lane < DECAY_LORA, jnp.tanh(lx), lx)
        return carry

    lax.fori_loop(0, nc, mix_chunk, 0)

    rowblk = lax.broadcasted_iota(jnp.int32, (PACK_W, PACK_W), 0) // HEAD_A
    colblk = lax.broadcasted_iota(jnp.int32, (PACK_W, PACK_W), 1) // HEAD_A
    for d in range(2):
        for p in range(N_PACKS):
            rows_ = jnp.concatenate([s0_ref[0, d, p * PACK + h] for h in range(PACK)], axis=0)
            st_s[d, p] = jnp.where(rowblk == colblk, jnp.concatenate([rows_] * PACK, axis=1), 0.0)

    t_idx = lax.broadcasted_iota(jnp.int32, (CHUNK, PACK_W), 0)
    s_idx = lax.broadcasted_iota(jnp.int32, (CHUNK, PACK_W), 1) % CHUNK

    def scan_step(i, carry):
        first_touch = i < nc // 2
        for d in range(2):
            reverse = d == 1
            c = nc - 1 - i if reverse else i
            rows = pl.ds(pl.multiple_of(c * CHUNK, CHUNK), CHUNK)
            strict = (s_idx > t_idx) if reverse else (s_idx < t_idx)
            incl = (s_idx >= t_idx) if reverse else (s_idx <= t_idx)
            last = 0 if reverse else CHUNK - 1
            r, k, v, kk = r_s[rows, :], k_s[rows, :], v_s[rows, :], kk_s[rows, :]
            lx = lx_s[rows, :]
            wl = w0_ref[d:d + 1, :] + _dot_bf16(lx[:, 0:DECAY_LORA], w2_ref[d])
            logw = -jnp.exp(_log_sigmoid(wl) - 0.5)
            a = _sigmoid(a0_ref[d:d + 1, :] + _dot_bf16(lx[:, DECAY_LORA:], a2_ref[d]))
            kd = k * (1.0 + (a - 1.0) * ka_ref[...])
            kb = kk * a
            cum = _mm(_tri(CHUNK, reverse, False).astype(F32), logw)
            clast = cum[last:last + 1]
            rt = r * jnp.exp(cum)
            at = -kk * jnp.exp(cum - logw)
            einv = jnp.exp(-cum)
            kt, bt = kd * einv, kb * einv
            edec = jnp.exp(clast - cum)
            kp, bp = kd * edec, kb * edec
            ptot = jnp.exp(clast)
            rkd = r * kd * rk_ref[...]
            outs = []
            for p in range(N_PACKS):
                cs = slice(p * PACK_W, (p + 1) * PACK_W)
                ar = jnp.concatenate([at[:, cs], rt[:, cs]], axis=0).astype(BF16)
                a_k = lax.dot_general(ar, _block_diag(kt[:, cs], ones_bd), NT, preferred_element_type=F32)
                a_b = lax.dot_general(ar, _block_diag(bt[:, cs], ones_bd), NT, preferred_element_type=F32)
                s_prev = st_s[d, p]
                ars = lax.dot_general(ar, s_prev.astype(BF16), NT, preferred_element_type=F32)
                a_ak = jnp.where(strict, a_k[:CHUNK], 0.0)
                a_rk = jnp.where(incl, a_k[CHUNK:], 0.0)
                npow = jnp.where(strict, a_b[:CHUNK], 0.0)
                a_rb = jnp.where(incl, a_b[CHUNK:], 0.0)
                v_bd = _block_diag(v[:, cs], ones_bd)
                u = ars[:CHUNK] + _dot_bf16(a_ak, v_bd)
                for j in range(6):
                    nb = npow.astype(BF16)
                    u = u + lax.dot_general(nb, _block_diag(u, ones_bd), NN, preferred_element_type=F32)
                    if j < 5:
                        npow = lax.dot_general(nb, _block_diag(npow, ones_bd), NN, preferred_element_type=F32)
                y = ars[CHUNK:] + _dot_bf16(a_rk, v_bd) + _dot_bf16(a_rb, _block_diag(u, ones_bd))
                vu = jnp.concatenate([v[:, cs], u], axis=0)
                kbp = jnp.concatenate([kp[:, cs], bp[:, cs]], axis=0)
                st_s[d, p] = s_prev * ptot[:, cs] + _dot_bf16(vu, kbp, TN) * ones_bd.astype(F32)
                mean = _group_sum(y, ones_bd) * (1.0 / HEAD_A)
                yc = y - mean
                var = _group_sum(yc * yc, ones_bd) * (1.0 / HEAD_A)
                out = yc * lax.rsqrt(var + GN_EPS) * gnw_ref[:, cs] + gnb_ref[:, cs]
                outs.append(out + _group_sum(rkd[:, cs], ones_bd) * v[:, cs])
            out = jnp.concatenate(outs, axis=1)

            @pl.when(first_touch)
            def _():
                y_ref[0, rows, :] = out

            @pl.when(jnp.logical_not(first_touch))
            def _():
                y_ref[0, rows, :] = (y_ref[0, rows, :] + out) * gate_s[rows, :]
        return carry

    lax.fori_loop(0, nc, scan_step, 0)

    for d in range(2):
        for p in range(N_PACKS):
            s_fin = st_s[d, p]
            for h in range(PACK):
                sout_ref[0, d, p * PACK + h] = s_fin[h * HEAD_A:(h + 1) * HEAD_A, h * HEAD_A:(h + 1) * HEAD_A]


def _rwkv(zr, s0, p, grid):
    bsz, seq_len, _ = zr.shape
    assert (seq_len // CHUNK) % 2 == 0
    kern = functools.partial(_rwkv_kernel, seq_len=seq_len, grid=grid)
    consts = [p['mu'], p['kkw'], p['ka'], p['rk'], p['gnw'], p['gnb'], p['w0'], p['w2'], p['a0'], p['a2'],
              p['g2'], p['ones_bd']]
    return pl.pallas_call(
        kern,
        grid=(bsz,),
        in_specs=[pl.BlockSpec((1, seq_len, RWKV_COLS), lambda b: (b, 0, 0)),
                  pl.BlockSpec((1, 2, H_A, HEAD_A, HEAD_A), lambda b: (b, 0, 0, 0, 0))]
                 + [_const_spec(a.shape) for a in consts],
        out_specs=[pl.BlockSpec((1, seq_len, DA), lambda b: (b, 0, 0)),
                   pl.BlockSpec((1, 2, H_A, HEAD_A, HEAD_A), lambda b: (b, 0, 0, 0, 0))],
        out_shape=[jax.ShapeDtypeStruct((bsz, seq_len, DA), F32),
                   jax.ShapeDtypeStruct((bsz, 2, H_A, HEAD_A, HEAD_A), F32)],
        scratch_shapes=[pltpu.VMEM((seq_len, DA), F32)] * 5
                       + [pltpu.VMEM((seq_len, DECAY_LORA + AAA_LORA), F32),
                          pltpu.VMEM((2, N_PACKS, PACK_W, PACK_W), F32)],
        compiler_params=_params("arbitrary"),
        name="rwkv_scan",
    )(zr, s0, *consts)


def _mlstm_kernel(zm_ref, g_ref, gt_ref, c0_ref, n0_ref, m0_ref, conv_ref, bi_ref, bit_ref, bf_ref, bft_ref,
                  ng_ref, y_ref, cout_ref, nout_ref, mout_ref,
                  qk_s, h_s, c_s, n_s, m_s, *, seq_len):
    nc = seq_len // CHUNK

    def conv_chunk(c, carry):
        zc, prev, nxt = _seq_neighbours(zm_ref, 0, c, nc, slice(0, 2 * DB))
        u = conv_ref[0:1, :] * prev + conv_ref[1:2, :] * zc + conv_ref[2:3, :] * nxt
        qk = u * _sigmoid(u)
        lane = lax.broadcasted_iota(jnp.int32, qk.shape, 1)
        qk_s[pl.ds(pl.multiple_of(c * CHUNK, CHUNK), CHUNK), :] = jnp.where(lane >= DB, qk * (HEAD_B ** -0.5), qk)
        return carry

    lax.fori_loop(0, nc, conv_chunk, 0)

    for d in range(2):
        c_s[d] = c0_ref[0, d]
        n_s[d] = n0_ref[0, d]
        m_s[d] = jnp.broadcast_to(m0_ref[0, d], (H_B, LANES))

    def scan_step(i, carry):
        for d in range(2):
            reverse = d == 1
            incl = _tri(CHUNK, reverse, False)
            tri_col = incl.astype(F32)
            tri_row = _tri(CHUNK, not reverse, False).astype(F32)
            last = 0 if reverse else CHUNK - 1
            gsl = slice(d * H_B, (d + 1) * H_B)
            fsl = slice(2 * H_B + d * H_B, 2 * H_B + (d + 1) * H_B)
            c = nc - 1 - i if reverse else i
            rows = pl.ds(pl.multiple_of(c * CHUNK, CHUNK), CHUNK)
            gcol = g_ref[0, rows, :]
            grow = gt_ref[0, c]
            i_col = gcol[:, gsl] + bi_ref[:, gsl]
            f_col = _log_sigmoid(gcol[:, fsl] + bf_ref[:, gsl])
            i_row = grow[gsl, :] + bit_ref[gsl, :]
            f_row = _log_sigmoid(grow[fsl, :] + bft_ref[gsl, :])
            b_col = _mm(tri_col, f_col)
            b_row = _mm(f_row, tri_row)
            b_last = b_col[last:last + 1]
            for h in range(H_B):
                sl = slice(h * HEAD_B, (h + 1) * HEAD_B)
                q = qk_s[rows, h * HEAD_B:(h + 1) * HEAD_B]
                qb = q.astype(BF16)
                k = qk_s[rows, DB + h * HEAD_B:DB + (h + 1) * HEAD_B]
                v = zm_ref[0, rows, 2 * DB + h * HEAD_B:2 * DB + (h + 1) * HEAD_B]
                bc, br = b_col[:, h:h + 1], b_row[h:h + 1, :]
                ic, ir = i_col[:, h:h + 1], i_row[h:h + 1, :]
                m_prev = m_s[d, h:h + 1, 0:1]
                dmat = jnp.where(incl, bc - br + ir, -jnp.inf)
                inter = bc + m_prev
                m_t = jnp.maximum(inter, jnp.max(dmat, axis=-1, keepdims=True))
                s = _dot_bf16(qb, k, NT) * jnp.exp(dmat - m_t)
                carry_w = jnp.exp(inter - m_t)
                c_prev, n_prev = c_s[d, h], n_s[d, h:h + 1, :]
                num = _dot_bf16(s, v) + carry_w * _dot_bf16(qb, c_prev)
                den = jnp.sum(s, axis=-1, keepdims=True) + carry_w * jnp.sum(q * n_prev, axis=-1, keepdims=True)
                hout = num / jnp.maximum(jnp.abs(den), jnp.exp(-m_t))
                bl = b_last[:, h:h + 1]
                g = bl - bc + ic
                m_new = jnp.maximum(bl + m_prev, jnp.max(g, axis=0, keepdims=True))
                wk = jnp.exp(g - m_new)
                decay = jnp.exp(bl + m_prev - m_new)
                c_s[d, h] = decay * c_prev + _dot_bf16(k, wk * v, TN)
                n_s[d, h:h + 1, :] = decay * n_prev + jnp.sum(wk * k, axis=0, keepdims=True)
                m_s[d, h:h + 1, :] = jnp.broadcast_to(m_new, (1, LANES))
                h_s[d, rows, sl] = hout
        return carry

    lax.fori_loop(0, nc, scan_step, 0)
    cout_ref[0] = c_s[...]
    nout_ref[0] = n_s[...]
    mout_ref[0] = m_s[...]

    def norm_chunk(c, carry):
        rows = pl.ds(pl.multiple_of(c * CHUNK, CHUNK), CHUNK)
        og = zm_ref[0, rows, 3 * DB:4 * DB]
        hm = h_s[0, rows, :] + h_s[1, rows, :]
        parts = []
        for h in range(H_B):
            hh = hm[:, h * HEAD_B:(h + 1) * HEAD_B]
            parts.append(hh * lax.rsqrt(jnp.mean(hh * hh, axis=-1, keepdims=True) + NORM_EPS))
        y_ref[0, rows, :] = jnp.concatenate(parts, axis=1) * ng_ref[...] * _sigmoid(og)
        return carry

    lax.fori_loop(0, nc, norm_chunk, 0)


def _mlstm(zm, g, gt, c0, n0, m0, p):
    bsz, seq_len, _ = zm.shape
    nc = seq_len // CHUNK
    kern = functools.partial(_mlstm_kernel, seq_len=seq_len)
    consts = [p['conv'], p['bi'], p['bit'], p['bf'], p['bft'], p['ng']]
    return pl.pallas_call(
        kern,
        grid=(bsz,),
        in_specs=[pl.BlockSpec((1, seq_len, MLSTM_MAIN), lambda b: (b, 0, 0)),
                  pl.BlockSpec((1, seq_len, N_GATES), lambda b: (b, 0, 0)),
                  pl.BlockSpec((1, nc, N_GATES, CHUNK), lambda b: (b, 0, 0, 0)),
                  pl.BlockSpec((1, 2, H_B, HEAD_B, HEAD_B), lambda b: (b, 0, 0, 0, 0)),
                  pl.BlockSpec((1, 2, H_B, HEAD_B), lambda b: (b, 0, 0, 0)),
                  pl.BlockSpec((1, 2, H_B, 1), lambda b: (b, 0, 0, 0))]
                 + [_const_spec(a.shape) for a in consts],
        out_specs=[pl.BlockSpec((1, seq_len, DB), lambda b: (b, 0, 0)),
                   pl.BlockSpec((1, 2, H_B, HEAD_B, HEAD_B), lambda b: (b, 0, 0, 0, 0)),
                   pl.BlockSpec((1, 2, H_B, HEAD_B), lambda b: (b, 0, 0, 0)),
                   pl.BlockSpec((1, 2, H_B, LANES), lambda b: (b, 0, 0, 0))],
        out_shape=[jax.ShapeDtypeStruct((bsz, seq_len, DB), F32),
                   jax.ShapeDtypeStruct((bsz, 2, H_B, HEAD_B, HEAD_B), F32),
                   jax.ShapeDtypeStruct((bsz, 2, H_B, HEAD_B), F32),
                   jax.ShapeDtypeStruct((bsz, 2, H_B, LANES), F32)],
        scratch_shapes=[pltpu.VMEM((seq_len, 2 * DB), F32), pltpu.VMEM((2, seq_len, DB), F32),
                        pltpu.VMEM((2, H_B, HEAD_B, HEAD_B), F32), pltpu.VMEM((2, H_B, HEAD_B), F32),
                        pltpu.VMEM((2, H_B, LANES), F32)],
        compiler_params=_params("arbitrary"),
        name="mlstm_scan",
    )(zm, g, gt, c0, n0, m0, *consts)


def _outproj_kernel(x_ref, yr_ref, ym_ref, mod_ref, g_ref, wo_ref, rw_ref, rb_ref,
                    x1_ref, hn_ref, idx_ref, gate_ref):
    mod = mod_ref[0]
    g1 = mod[:, 2 * D_MODEL:3 * D_MODEL]
    sh2, sc2 = mod[:, 3 * D_MODEL:4 * D_MODEL], mod[:, 4 * D_MODEL:5 * D_MODEL]
    mix = _dot_bf16(yr_ref[...], wo_ref[0:DA, :]) + _dot_bf16(ym_ref[...], wo_ref[DA:, :])
    x1 = x_ref[...] + g1 * mix
    x1_ref[...] = x1
    y = x1 * lax.rsqrt(jnp.mean(x1 * x1, axis=-1, keepdims=True) + NORM_EPS)
    hn = y * g_ref[...] * (1.0 + sc2) + sh2
    hn_ref[...] = hn
    logits = _dot_bf16(hn, rw_ref[...]) + rb_ref[...]
    lane = lax.broadcasted_iota(jnp.int32, logits.shape, 1)
    vals, idxs = [], []
    for _ in range(TOP_K):
        top = jnp.max(logits, axis=-1, keepdims=True)
        pick = jnp.min(jnp.where(logits == top, lane, N_EXPERTS), axis=-1, keepdims=True)
        vals.append(top)
        idxs.append(pick)
        logits = jnp.where(lane == pick, -jnp.inf, logits)
    exps = [jnp.exp(v - vals[0]) for v in vals]
    total = exps[0] + exps[1] + exps[2] + exps[3]
    k_lane = lax.broadcasted_iota(jnp.int32, (x1.shape[0], TOP_K), 1)
    gates = jnp.zeros((x1.shape[0], TOP_K), F32)
    picks = jnp.zeros((x1.shape[0], TOP_K), jnp.int32)
    for j in range(TOP_K):
        gates = jnp.where(k_lane == j, exps[j] / total, gates)
        picks = jnp.where(k_lane == j, idxs[j], picks)
    idx_ref[...] = picks
    gate_ref[...] = gates


def _outproj(x2, yr, ym, mod, norm_g, w_out, router_w, router_b):
    rows = x2.shape[0]
    tiles_per_mod = rows // mod.shape[0] // ROW_TILE
    row_spec = lambda w: pl.BlockSpec((ROW_TILE, w), lambda i: (i, 0))
    return pl.pallas_call(
        _outproj_kernel,
        grid=(rows // ROW_TILE,),
        in_specs=[row_spec(D_MODEL), row_spec(DA), row_spec(DB),
                  pl.BlockSpec((1, 1, 6 * D_MODEL), lambda i: (i // tiles_per_mod, 0, 0)),
                  _const_spec((1, D_MODEL)), _const_spec(w_out.shape), _const_spec(router_w.shape),
                  _const_spec((1, N_EXPERTS))],
        out_specs=[row_spec(D_MODEL), row_spec(D_MODEL), row_spec(TOP_K), row_spec(TOP_K)],
        out_shape=[jax.ShapeDtypeStruct((rows, D_MODEL), F32), jax.ShapeDtypeStruct((rows, D_MODEL), F32),
                   jax.ShapeDtypeStruct((rows, TOP_K), jnp.int32), jax.ShapeDtypeStruct((rows, TOP_K), F32)],
        compiler_params=_params("arbitrary"),
        name="outproj_router",
    )(x2, yr, ym, mod, norm_g.reshape(1, D_MODEL), w_out, router_w, router_b.reshape(1, N_EXPERTS))


def _rank_kernel(idx_ref, rank_ref, count_ref, run_s):
    i = pl.program_id(0)

    @pl.when(i == 0)
    def _():
        run_s[...] = jnp.zeros_like(run_s)

    idx = idx_ref[...]
    rows = idx.shape[0]
    lane = lax.broadcasted_iota(jnp.int32, (rows, N_EXPERTS), 1)
    hot = jnp.zeros((rows, N_EXPERTS), F32)
    for j in range(TOP_K):
        hot = hot + jnp.where(lane == idx[:, j:j + 1], 1.0, 0.0)
    before = _dot_bf16(_tri(rows, False, True).astype(F32), hot) + run_s[...]
    k_lane = lax.broadcasted_iota(jnp.int32, (rows, TOP_K), 1)
    rank = jnp.zeros((rows, TOP_K), F32)
    for j in range(TOP_K):
        rj = jnp.sum(jnp.where(lane == idx[:, j:j + 1], before, 0.0), axis=-1, keepdims=True)
        rank = jnp.where(k_lane == j, rj, rank)
    rank_ref[...] = rank.astype(jnp.int32)
    run_s[...] = run_s[...] + jnp.sum(hot, axis=0, keepdims=True)
    count_ref[...] = run_s[...].astype(jnp.int32)


def _expert_ranks(idx):
    rows = idx.shape[0]
    return pl.pallas_call(
        _rank_kernel,
        grid=(rows // ROW_TILE,),
        in_specs=[pl.BlockSpec((ROW_TILE, TOP_K), lambda i: (i, 0))],
        out_specs=[pl.BlockSpec((ROW_TILE, TOP_K), lambda i: (i, 0)), _const_spec((1, N_EXPERTS))],
        out_shape=[jax.ShapeDtypeStruct((rows, TOP_K), jnp.int32),
                   jax.ShapeDtypeStruct((1, N_EXPERTS), jnp.int32)],
        scratch_shapes=[pltpu.VMEM((1, N_EXPERTS), F32)],
        compiler_params=_params("arbitrary"),
        name="expert_ranks",
    )(idx)


def _dispatch_kernel(dest_ref, x_ref, init_ref, xs_ref, sem):
    del init_ref

    def copy(r, slot):
        return pltpu.make_async_copy(x_ref.at[pl.ds(r, 1)], xs_ref.at[pl.ds(slot, 1)], sem)

    def issue(r, carry):
        for j in range(TOP_K):
            copy(r, dest_ref[r * TOP_K + j]).start()
        return carry

    lax.fori_loop(0, ROW_TILE, issue, 0)

    def drain(r, carry):
        for j in range(TOP_K):
            copy(0, 0).wait()
        return carry

    lax.fori_loop(0, ROW_TILE, drain, 0)


def _dispatch(dest_flat, x2, n_slots):
    rows = x2.shape[0]
    init = jnp.zeros((n_slots, D_MODEL), x2.dtype)
    return pl.pallas_call(
        _dispatch_kernel,
        grid=(rows // ROW_TILE,),
        in_specs=[pl.BlockSpec((ROW_TILE * TOP_K,), lambda i: (i,), memory_space=pltpu.SMEM),
                  pl.BlockSpec((ROW_TILE, D_MODEL), lambda i: (i, 0)),
                  pl.BlockSpec(memory_space=pl.ANY)],
        out_specs=pl.BlockSpec(memory_space=pl.ANY),
        out_shape=jax.ShapeDtypeStruct(init.shape, init.dtype),
        scratch_shapes=[pltpu.SemaphoreType.DMA(())],
        input_output_aliases={2: 0},
        compiler_params=_params("arbitrary"),
        name="moe_dispatch",
    )(dest_flat, x2, init)


def _expert_kernel(be_ref, nb_ref, xs_ref, w1_ref, b1_ref, w2_ref, b2_ref, y_ref, w1_s, w2_s):
    i = pl.program_id(0)
    changed = jnp.logical_or(i == 0, be_ref[i] != be_ref[jnp.maximum(i - 1, 0)])

    @pl.when(jnp.logical_and(changed, i < nb_ref[0]))
    def _():
        w1_s[...] = w1_ref[0].astype(BF16)
        w2_s[...] = w2_ref[0].astype(BF16)

    @pl.when(i < nb_ref[0])
    def _():
        xb = xs_ref[...].astype(BF16)
        hu = lax.dot_general(xb, w1_s[...], NN, preferred_element_type=F32) + b1_ref[0]
        glu = jnp.minimum(hu[:, :D_FF], SWIGLU_LIMIT)
        lin = jnp.clip(hu[:, D_FF:], -SWIGLU_LIMIT, SWIGLU_LIMIT)
        act = glu * _sigmoid(SWIGLU_ALPHA * glu) * (lin + 1.0)
        y_ref[...] = lax.dot_general(act.astype(BF16), w2_s[...], NN, preferred_element_type=F32) + b2_ref[0]

    @pl.when(i >= nb_ref[0])
    def _():
        y_ref[...] = jnp.zeros_like(y_ref)


def _experts(block_e, n_used, xs, w1, b1, w2, b2):
    n_blocks = xs.shape[0] // MOE_BLOCK
    grid_spec = pltpu.PrefetchScalarGridSpec(
        num_scalar_prefetch=2,
        grid=(n_blocks,),
        in_specs=[pl.BlockSpec((MOE_BLOCK, D_MODEL), lambda i, be, nb: (i, 0)),
                  pl.BlockSpec((1, D_MODEL, 2 * D_FF), lambda i, be, nb: (be[i], 0, 0)),
                  pl.BlockSpec((1, 1, 2 * D_FF), lambda i, be, nb: (be[i], 0, 0)),
                  pl.BlockSpec((1, D_FF, D_MODEL), lambda i, be, nb: (be[i], 0, 0)),
                  pl.BlockSpec((1, 1, D_MODEL), lambda i, be, nb: (be[i], 0, 0))],
        out_specs=pl.BlockSpec((MOE_BLOCK, D_MODEL), lambda i, be, nb: (i, 0)),
        scratch_shapes=[pltpu.VMEM((D_MODEL, 2 * D_FF), BF16), pltpu.VMEM((D_FF, D_MODEL), BF16)],
    )
    return pl.pallas_call(
        _expert_kernel,
        grid_spec=grid_spec,
        out_shape=jax.ShapeDtypeStruct(xs.shape, F32),
        compiler_params=_params("arbitrary"),
        name="moe_experts",
    )(block_e, n_used, xs, w1, b1.reshape(N_EXPERTS, 1, 2 * D_FF), w2, b2.reshape(N_EXPERTS, 1, D_MODEL))


def _combine_kernel(dest_ref, yb_ref, x1_ref, gate_ref, mod_ref, fg_ref, o_ref, buf, sem):
    def copy(slot, j, r):
        return pltpu.make_async_copy(yb_ref.at[pl.ds(slot, 1)], buf.at[j, pl.ds(r, 1)], sem)

    def issue(r, carry):
        for j in range(TOP_K):
            copy(dest_ref[r * TOP_K + j], j, r).start()
        return carry

    lax.fori_loop(0, ROW_TILE, issue, 0)

    def drain(r, carry):
        for j in range(TOP_K):
            copy(0, 0, 0).wait()
        return carry

    lax.fori_loop(0, ROW_TILE, drain, 0)

    g2 = mod_ref[0][:, 5 * D_MODEL:6 * D_MODEL]
    gates = gate_ref[...]
    moe = jnp.zeros((ROW_TILE, D_MODEL), F32)
    for j in range(TOP_K):
        moe = moe + buf[j] * gates[:, j:j + 1]
    x2 = x1_ref[...] + g2 * moe
    y = x2 * lax.rsqrt(jnp.mean(x2 * x2, axis=-1, keepdims=True) + NORM_EPS)
    o_ref[...] = y * fg_ref[...]


def _combine(dest_flat, yb, x1, gates, mod, final_g, row_offset):
    rows = x1.shape[0]
    tiles_per_mod = rows // mod.shape[0] // ROW_TILE
    tile_offset = row_offset // ROW_TILE
    row_spec = pl.BlockSpec((ROW_TILE, D_MODEL), lambda i: (i, 0))
    return pl.pallas_call(
        _combine_kernel,
        grid=(rows // ROW_TILE,),
        in_specs=[pl.BlockSpec((ROW_TILE * TOP_K,), lambda i: (i + tile_offset,), memory_space=pltpu.SMEM),
                  pl.BlockSpec(memory_space=pl.ANY),
                  row_spec,
                  pl.BlockSpec((ROW_TILE, TOP_K), lambda i: (i, 0)),
                  pl.BlockSpec((1, 1, 6 * D_MODEL), lambda i: (i // tiles_per_mod, 0, 0)),
                  _const_spec((1, D_MODEL))],
        out_specs=row_spec,
        out_shape=jax.ShapeDtypeStruct((rows, D_MODEL), F32),
        scratch_shapes=[pltpu.VMEM((TOP_K, ROW_TILE, D_MODEL), F32), pltpu.SemaphoreType.DMA(())],
        compiler_params=_params("arbitrary"),
        name="moe_combine",
    )(dest_flat, yb, x1, gates, mod, final_g.reshape(1, D_MODEL))


def _block_diag_ones(width, block):
    i = jnp.arange(width) // block
    return (i[:, None] == i[None, :]).astype(F32)


def _mixers(x, mod, lp, s0, c0, n0, m0, grid):
    bsz, seq_len, _ = x.shape
    nc = seq_len // CHUNK
    x2 = x.reshape(bsz * seq_len, D_MODEL)
    zr, zm, zg, zgt = _inproj(x2, mod, lp['norm1_g'], lp['w_r'], lp['w_m'], lp['w_g'], lp['w_gt'])
    gt = zgt.reshape(N_GATES, bsz, nc, CHUNK).transpose(1, 2, 0, 3)
    yr, s_new = _rwkv(zr.reshape(bsz, seq_len, RWKV_COLS), s0, lp, grid)
    ym, c_new, n_new, m_new = _mlstm(zm.reshape(bsz, seq_len, MLSTM_MAIN), zg.reshape(bsz, seq_len, N_GATES), gt,
                                     c0, n0, m0.reshape(bsz, 2, H_B, 1), lp)
    x1, hn, idx, gates = _outproj(x2, yr.reshape(-1, DA), ym.reshape(-1, DB), mod, lp['norm2_g'], lp['w_out'],
                                  lp['router_w'], lp['router_b'])
    return x1, hn, idx, gates, (s_new, c_new, n_new, m_new[..., 0])


def kernel(x_prompt, x_sample, state_rwkv, state_mlstm_C, state_mlstm_n, state_mlstm_m, c, c_ctx, ada_w, ada_b, norm1_g, norm2_g, w_in, w_out, rwkv_mu, rwkv_w0, rwkv_w2, rwkv_a0, rwkv_a2, rwkv_g2, rwkv_kk, rwkv_ka, rwkv_rk, rwkv_gn_w, rwkv_gn_b, mlstm_conv, mlstm_bi, mlstm_bf, mlstm_norm_g, router_w, router_b, moe_w1, moe_b1, moe_w2, moe_b2, final_g):
    bp, lp_len, _ = x_prompt.shape
    bs, ls_len, _ = x_sample.shape
    w = w_in[0]
    lp = {
        'norm1_g': norm1_g[0], 'norm2_g': norm2_g[0],
        'w_r': w[:, :RWKV_COLS].astype(BF16),
        'w_m': w[:, RWKV_COLS:RWKV_COLS + MLSTM_MAIN].astype(BF16),
        'w_g': w[:, RWKV_COLS + MLSTM_MAIN:],
        'w_gt': w[:, RWKV_COLS + MLSTM_MAIN:].T,
        'w_out': w_out[0].astype(BF16),
        'mu': rwkv_mu[0].reshape(1, RWKV_COLS), 'kkw': rwkv_kk[0].reshape(1, DA), 'ka': rwkv_ka[0].reshape(1, DA),
        'rk': rwkv_rk[0].reshape(1, DA), 'gnw': rwkv_gn_w[0].reshape(1, DA), 'gnb': rwkv_gn_b[0].reshape(1, DA),
        'w0': rwkv_w0[0], 'w2': rwkv_w2[0], 'a0': rwkv_a0[0], 'a2': rwkv_a2[0], 'g2': rwkv_g2[0],
        'ones_bd': _block_diag_ones(PACK_W, HEAD_A).astype(BF16),
        'conv': mlstm_conv[0], 'bi': mlstm_bi[0].reshape(1, 2 * H_B), 'bit': mlstm_bi[0].reshape(2 * H_B, 1),
        'bf': mlstm_bf[0].reshape(1, 2 * H_B), 'bft': mlstm_bf[0].reshape(2 * H_B, 1),
        'ng': mlstm_norm_g[0].reshape(1, DB),
        'router_w': router_w[0], 'router_b': router_b[0],
    }
    cc = jnp.concatenate([c_ctx[None, :], c, jnp.zeros((2 * SUBLANES - 1 - bs, D_MODEL), F32)], axis=0)
    mod = _ada_mod(cc, ada_w[0], ada_b[0])
    mod_p = mod[0:1].reshape(1, 1, 6 * D_MODEL)
    mod_s = mod[1:1 + bs].reshape(bs, 1, 6 * D_MODEL)

    zeros = lambda *shape: jnp.zeros(shape, F32)
    x1p, hnp, idxp, gatesp, st = _mixers(x_prompt, mod_p, lp, zeros(bp, 2, H_A, HEAD_A, HEAD_A),
                                         zeros(bp, 2, H_B, HEAD_B, HEAD_B), zeros(bp, 2, H_B, HEAD_B),
                                         zeros(bp, 2, H_B), False)
    x1s, hns, idxs, gatess, _ = _mixers(x_sample, mod_s, lp, state_rwkv[:, 0], state_mlstm_C[:, 0],
                                        state_mlstm_n[:, 0], state_mlstm_m[:, 0], True)

    n_p = bp * lp_len
    idx = jnp.concatenate([idxp, idxs], axis=0)
    hn = jnp.concatenate([hnp, hns], axis=0)
    n_tok = idx.shape[0]
    rank, counts = _expert_ranks(idx)
    counts = counts[0]
    padded = (counts + MOE_BLOCK - 1) // MOE_BLOCK * MOE_BLOCK
    pad_end = jnp.cumsum(padded)
    pad_start = pad_end - padded
    n_blocks = n_tok * TOP_K // MOE_BLOCK + N_EXPERTS
    block_start = jnp.arange(n_blocks, dtype=jnp.int32) * MOE_BLOCK
    block_e = jnp.minimum(jnp.sum(pad_end[None, :] <= block_start[:, None], axis=1), N_EXPERTS - 1).astype(jnp.int32)
    n_used = (pad_end[-1:] // MOE_BLOCK).astype(jnp.int32)
    dest = (pad_start[idx] + rank).astype(jnp.int32).reshape(-1)
    xs = _dispatch(dest, hn, n_blocks * MOE_BLOCK)
    yb = _experts(block_e, n_used, xs, moe_w1[0], moe_b1[0], moe_w2[0], moe_b2[0])
    y_prompt = _combine(dest, yb, x1p, gatesp, mod_p, final_g, 0).reshape(x_prompt.shape)
    y_sample = _combine(dest, yb, x1s, gatess, mod_s, final_g, n_p).reshape(x_sample.shape)

    s_new, c_new, n_new, m_new = st
    return (y_prompt, y_sample, s_new[:, None], c_new[:, None], n_new[:, None], m_new[:, None])
```

```python
import functools

import jax
import jax.numpy as jnp
from jax import lax
from jax.experimental import pallas as pl
from jax.experimental.pallas import tpu as pltpu

F32 = jnp.float32
BF16 = jnp.bfloat16
HIGHEST = lax.Precision.HIGHEST

D_MODEL = 1024
DA = 512
HEAD_A = 64
H_A = DA // HEAD_A
DB = 512
H_B = 4
HEAD_B = DB // H_B
DECAY_LORA = 64
AAA_LORA = 64
GATE_LORA = 128
RWKV_COLS = 3 * DA + DECAY_LORA + AAA_LORA + GATE_LORA
MLSTM_MAIN = 4 * DB
N_GATES = 4 * H_B
GRID_W = 64
CHUNK = 64
N_EXPERTS = 32
TOP_K = 4
D_FF = D_MODEL
SWIGLU_LIMIT = 7.0
SWIGLU_ALPHA = 1.702
MOE_BLOCK = 256
NORM_EPS = 1e-6
GN_EPS = 64e-5
ROW_TILE = 256
SUBLANES = 8
LANES = 128
LANE_ROWS = D_MODEL // LANES
VMEM_LIMIT = 56 * 1024 * 1024
PACK = 4
PACK_W = PACK * HEAD_A
N_PACKS = H_A // PACK

NN = (((1,), (0,)), ((), ()))
NT = (((1,), (1,)), ((), ()))
TN = (((0,), (0,)), ((), ()))


def _mm(a, b, precision=HIGHEST):
    return lax.dot_general(a, b, NN, precision=precision, preferred_element_type=F32)


def _mm_nt(a, b, precision=HIGHEST):
    return lax.dot_general(a, b, NT, precision=precision, preferred_element_type=F32)


def _mm_tn(a, b, precision=HIGHEST):
    return lax.dot_general(a, b, TN, precision=precision, preferred_element_type=F32)


def _dot_bf16(a, b, dims=NN):
    return lax.dot_general(a.astype(BF16), b.astype(BF16), dims, preferred_element_type=F32)


def _sigmoid(x):
    return 1.0 / (1.0 + jnp.exp(-x))


def _log_sigmoid(x):
    return jnp.minimum(x, 0.0) - jnp.log(1.0 + jnp.exp(-jnp.abs(x)))


def _params(*sem):
    return pltpu.CompilerParams(dimension_semantics=sem, vmem_limit_bytes=VMEM_LIMIT)


def _const_spec(shape):
    nd = len(shape)
    return pl.BlockSpec(shape, lambda *_: (0,) * nd)


def _tri(n, reverse, strict):
    t = lax.broadcasted_iota(jnp.int32, (n, n), 0)
    s = lax.broadcasted_iota(jnp.int32, (n, n), 1)
    if reverse:
        return (s > t) if strict else (s >= t)
    return (s < t) if strict else (s <= t)


def _seq_neighbours(ref, b, c, nc, cols):
    r0 = pl.multiple_of(c * CHUNK, CHUNK)
    zc = ref[b, pl.ds(r0, CHUNK), cols]
    row = lax.broadcasted_iota(jnp.int32, zc.shape, 0)
    p0 = pl.multiple_of(jnp.maximum(r0 - SUBLANES, 0), SUBLANES)
    n0 = pl.multiple_of(jnp.minimum(r0 + CHUNK, (nc - 1) * CHUNK), SUBLANES)
    before = ref[b, pl.ds(p0, SUBLANES), cols][SUBLANES - 1:SUBLANES]
    after = ref[b, pl.ds(n0, SUBLANES), cols][0:1]
    before = jnp.where(c > 0, before, 0.0)
    after = jnp.where(c < nc - 1, after, 0.0)
    prev = jnp.where(row == 0, before, pltpu.roll(zc, 1, 0))
    nxt = jnp.where(row == CHUNK - 1, after, pltpu.roll(zc, CHUNK - 1, 0))
    return zc, prev, nxt


def _ada_kernel(c_ref, w_ref, b_ref, o_ref):
    cc = c_ref[...]
    o_ref[...] = _dot_bf16(cc * _sigmoid(cc), w_ref[...]) + b_ref[...]


def _ada_mod(cc, ada_w, ada_b):
    rows = cc.shape[0]
    ncol = ada_w.shape[1]
    tn = 1536
    return pl.pallas_call(
        _ada_kernel,
        grid=(ncol // tn,),
        in_specs=[_const_spec((rows, D_MODEL)),
                  pl.BlockSpec((D_MODEL, tn), lambda j: (0, j)),
                  pl.BlockSpec((1, tn), lambda j: (0, j))],
        out_specs=pl.BlockSpec((rows, tn), lambda j: (0, j)),
        out_shape=jax.ShapeDtypeStruct((rows, ncol), F32),
        compiler_params=_params("arbitrary"),
        name="ada_mod",
    )(cc, ada_w, ada_b.reshape(1, ncol))


def _inproj_kernel(x_ref, mod_ref, g_ref, wr_ref, wm_ref, wg_ref, wgt_ref, zr_ref, zm_ref, zg_ref, zgt_ref):
    x = x_ref[...]
    mod = mod_ref[0]
    sh, sc = mod[:, 0:D_MODEL], mod[:, D_MODEL:2 * D_MODEL]
    y = x * lax.rsqrt(jnp.mean(x * x, axis=-1, keepdims=True) + NORM_EPS)
    hn = y * g_ref[...] * (1.0 + sc) + sh
    hb = hn.astype(BF16)
    zr_ref[...] = lax.dot_general(hb, wr_ref[...], NN, preferred_element_type=F32)
    zm_ref[...] = lax.dot_general(hb, wm_ref[...], NN, preferred_element_type=F32)
    zg_ref[...] = _dot_bf16(hb, wg_ref[...])
    zgt_ref[...] = _dot_bf16(wgt_ref[...], hb, NT)


def _inproj(x2, mod, norm_g, w_r, w_m, w_g, w_gt):
    rows = x2.shape[0]
    tiles_per_mod = rows // mod.shape[0] // ROW_TILE
    return pl.pallas_call(
        _inproj_kernel,
        grid=(rows // ROW_TILE,),
        in_specs=[pl.BlockSpec((ROW_TILE, D_MODEL), lambda i: (i, 0)),
                  pl.BlockSpec((1, 1, 6 * D_MODEL), lambda i: (i // tiles_per_mod, 0, 0)),
                  _const_spec((1, D_MODEL)),
                  _const_spec(w_r.shape), _const_spec(w_m.shape), _const_spec(w_g.shape),
                  _const_spec(w_gt.shape)],
        out_specs=[pl.BlockSpec((ROW_TILE, RWKV_COLS), lambda i: (i, 0)),
                   pl.BlockSpec((ROW_TILE, MLSTM_MAIN), lambda i: (i, 0)),
                   pl.BlockSpec((ROW_TILE, N_GATES), lambda i: (i, 0)),
                   pl.BlockSpec((N_GATES, ROW_TILE), lambda i: (0, i))],
        out_shape=[jax.ShapeDtypeStruct((rows, RWKV_COLS), F32),
                   jax.ShapeDtypeStruct((rows, MLSTM_MAIN), F32),
                   jax.ShapeDtypeStruct((rows, N_GATES), F32),
                   jax.ShapeDtypeStruct((N_GATES, rows), F32)],
        compiler_params=_params("arbitrary"),
        name="inproj",
    )(x2, mod, norm_g.reshape(1, D_MODEL), w_r, w_m, w_g, w_gt)


def _group_sum(x, ones_bd):
    hi = x.astype(BF16)
    lo = (x - hi.astype(F32)).astype(BF16)
    return (lax.dot_general(hi, ones_bd, NN, preferred_element_type=F32)
            + lax.dot_general(lo, ones_bd, NN, preferred_element_type=F32))


def _block_diag(x, ones_bd):
    return jnp.concatenate([x.astype(BF16)] * PACK, axis=0) * ones_bd


def _rwkv_kernel(zr_ref, s0_ref, mu_ref, kkw_ref, ka_ref, rk_ref, gnw_ref, gnb_ref, w0_ref, w2_ref,
                 a0_ref, a2_ref, g2_ref, ones_ref, y_ref, sout_ref,
                 r_s, k_s, v_s, kk_s, gate_s, lx_s, st_s, *, seq_len, grid):
    nc = seq_len // CHUNK
    ones_bd = ones_ref[...]

    def mix_chunk(c, carry):
        r0 = pl.multiple_of(c * CHUNK, CHUNK)
        if grid:
            zc = zr_ref[0, pl.ds(r0, CHUNK), :]
            row = lax.broadcasted_iota(jnp.int32, zc.shape, 0)
            up0 = pl.multiple_of(jnp.maximum(c - 1, 0) * CHUNK, CHUNK)
            dn0 = pl.multiple_of(jnp.minimum(c + 1, nc - 1) * CHUNK, CHUNK)
            up = jnp.where(c > 0, zr_ref[0, pl.ds(up0, CHUNK), :], 0.0)
            down = jnp.where(c < nc - 1, zr_ref[0, pl.ds(dn0, CHUNK), :], 0.0)
            left = jnp.where(row == 0, 0.0, pltpu.roll(zc, 1, 0))
            right = jnp.where(row == CHUNK - 1, 0.0, pltpu.roll(zc, CHUNK - 1, 0))
            local = 0.25 * (up + down + left + right)
        else:
            zc, prev, nxt = _seq_neighbours(zr_ref, 0, c, nc, slice(None))
            local = 0.5 * (prev + nxt)
        z = zc + (local - zc) * mu_ref[...]
        r, k, v = z[:, 0:DA], z[:, DA:2 * DA], z[:, 2 * DA:3 * DA]
        lx = z[:, 3 * DA:3 * DA + DECAY_LORA + AAA_LORA]
        xg = z[:, 3 * DA + DECAY_LORA + AAA_LORA:]
        kk = k * kkw_ref[...]
        sq = kk * kk
        ss = jnp.concatenate([_group_sum(sq[:, p * PACK_W:(p + 1) * PACK_W], ones_bd) for p in range(N_PACKS)],
                             axis=1)
        kk = kk / jnp.maximum(jnp.sqrt(ss), 1e-12)
        lane = lax.broadcasted_iota(jnp.int32, lx.shape, 1)
        r_s[pl.ds(r0, CHUNK), :] = r
        k_s[pl.ds(r0, CHUNK), :] = k
        v_s[pl.ds(r0, CHUNK), :] = v
        kk_s[pl.ds(r0, CHUNK), :] = kk
        gate_s[pl.ds(r0, CHUNK), :] = _dot_bf16(_sigmoid(xg), g2_ref[...])
        lx_s[pl.ds(r0, CHUNK), :] = jnp.where(lane < DECAY_LORA, jnp.tanh(lx), lx)
        return carry

    lax.fori_loop(0, nc, mix_chunk, 0)

    rowblk = lax.broadcasted_iota(jnp.int32, (PACK_W, PACK_W), 0) // HEAD_A
    colblk = lax.broadcasted_iota(jnp.int32, (PACK_W, PACK_W), 1) // HEAD_A
    for d in range(2):
        for p in range(N_PACKS):
            rows_ = jnp.concatenate([s0_ref[0, d, p * PACK + h] for h in range(PACK)], axis=0)
            st_s[d, p] = jnp.where(rowblk == colblk, jnp.concatenate([rows_] * PACK, axis=1), 0.0)

    t_idx = lax.broadcasted_iota(jnp.int32, (CHUNK, PACK_W), 0)
    s_idx = lax.broadcasted_iota(jnp.int32, (CHUNK, PACK_W), 1) % CHUNK

    def scan_step(i, carry):
        first_touch = i < nc // 2
        dot = functools.partial(lax.dot_general, preferred_element_type=F32)
        chains, row_sel = [], []
        for d in range(2):
            reverse = d == 1
            c = nc - 1 - i if reverse else i
            rows = pl.ds(pl.multiple_of(c * CHUNK, CHUNK), CHUNK)
            row_sel.append(rows)
            strict = (s_idx > t_idx) if reverse else (s_idx < t_idx)
            incl = (s_idx >= t_idx) if reverse else (s_idx <= t_idx)
            last = 0 if reverse else CHUNK - 1
            r, k, v, kk = r_s[rows, :], k_s[rows, :], v_s[rows, :], kk_s[rows, :]
            lx = lx_s[rows, :]
            wl = w0_ref[d:d + 1, :] + _dot_bf16(lx[:, 0:DECAY_LORA], w2_ref[d])
            logw = -jnp.exp(_log_sigmoid(wl) - 0.5)
            a = _sigmoid(a0_ref[d:d + 1, :] + _dot_bf16(lx[:, DECAY_LORA:], a2_ref[d]))
            kd = k * (1.0 + (a - 1.0) * ka_ref[...])
            kb = kk * a
            cum = _mm(_tri(CHUNK, reverse, False).astype(F32), logw)
            clast = cum[last:last + 1]
            rt = r * jnp.exp(cum)
            at = -kk * jnp.exp(cum - logw)
            einv = jnp.exp(-cum)
            kt, bt = kd * einv, kb * einv
            edec = jnp.exp(clast - cum)
            kp, bp = kd * edec, kb * edec
            ptot = jnp.exp(clast)
            rkd = r * kd * rk_ref[...]
            for p in range(N_PACKS):
                cs = slice(p * PACK_W, (p + 1) * PACK_W)
                chains.append(dict(
                    d=d, p=p, cs=cs, strict=strict, incl=incl, v=v[:, cs], rkd=rkd[:, cs], ptot=ptot[:, cs],
                    ar=jnp.concatenate([at[:, cs], rt[:, cs]], axis=0).astype(BF16),
                    kt_bd=_block_diag(kt[:, cs], ones_bd), bt_bd=_block_diag(bt[:, cs], ones_bd),
                    v_bd=_block_diag(v[:, cs], ones_bd),
                    kbp=jnp.concatenate([kp[:, cs], bp[:, cs]], axis=0).astype(BF16)))

        for ch in chains:
            ch['s_prev'] = st_s[ch['d'], ch['p']]
            ch['a_k'] = dot(ch['ar'], ch['kt_bd'], NT)
            ch['a_b'] = dot(ch['ar'], ch['bt_bd'], NT)
            ch['ars'] = dot(ch['ar'], ch['s_prev'].astype(BF16), NT)
        for ch in chains:
            a_ak = jnp.where(ch['strict'], ch['a_k'][:CHUNK], 0.0).astype(BF16)
            ch['a_rk'] = jnp.where(ch['incl'], ch['a_k'][CHUNK:], 0.0).astype(BF16)
            ch['a_rb'] = jnp.where(ch['incl'], ch['a_b'][CHUNK:], 0.0).astype(BF16)
            ch['npow'] = jnp.where(ch['strict'], ch['a_b'][:CHUNK], 0.0)
            ch['u'] = ch['ars'][:CHUNK] + dot(a_ak, ch['v_bd'], NN)
        for j in range(6):
            for ch in chains:
                ch['nb'] = ch['npow'].astype(BF16)
                ch['u'] = ch['u'] + dot(ch['nb'], _block_diag(ch['u'], ones_bd), NN)
            if j < 5:
                for ch in chains:
                    ch['npow'] = dot(ch['nb'], _block_diag(ch['npow'], ones_bd), NN)
        for ch in chains:
            u = ch['u']
            ch['y'] = (ch['ars'][CHUNK:] + dot(ch['a_rk'], ch['v_bd'], NN)
                       + dot(ch['a_rb'], _block_diag(u, ones_bd), NN))
            vu = jnp.concatenate([ch['v'], u], axis=0).astype(BF16)
            st_s[ch['d'], ch['p']] = ch['s_prev'] * ch['ptot'] + dot(vu, ch['kbp'], TN) * ones_bd.astype(F32)
        outs = [[], []]
        for ch in chains:
            y, cs = ch['y'], ch['cs']
            mean = _group_sum(y, ones_bd) * (1.0 / HEAD_A)
            yc = y - mean
            var = _group_sum(yc * yc, ones_bd) * (1.0 / HEAD_A)
            out = yc * lax.rsqrt(var + GN_EPS) * gnw_ref[:, cs] + gnb_ref[:, cs]
            outs[ch['d']].append(out + _group_sum(ch['rkd'], ones_bd) * ch['v'])
        for d in range(2):
            rows = row_sel[d]
            out = jnp.concatenate(outs[d], axis=1)

            @pl.when(first_touch)
            def _(rows=rows, out=out):
                y_ref[0, rows, :] = out

            @pl.when(jnp.logical_not(first_touch))
            def _(rows=rows, out=out):
                y_ref[0, rows, :] = (y_ref[0, rows, :] + out) * gate_s[rows, :]
        return carry

    lax.fori_loop(0, nc, scan_step, 0)

    for d in range(2):
        for p in range(N_PACKS):
            s_fin = st_s[d, p]
            for h in range(PACK):
                sout_ref[0, d, p * PACK + h] = s_fin[h * HEAD_A:(h + 1) * HEAD_A, h * HEAD_A:(h + 1) * HEAD_A]


def _rwkv(zr, s0, p, grid):
    bsz, seq_len, _ = zr.shape
    assert (seq_len // CHUNK) % 2 == 0
    kern = functools.partial(_rwkv_kernel, seq_len=seq_len, grid=grid)
    consts = [p['mu'], p['kkw'], p['ka'], p['rk'], p['gnw'], p['gnb'], p['w0'], p['w2'], p['a0'], p['a2'],
              p['g2'], p['ones_bd']]
    return pl.pallas_call(
        kern,
        grid=(bsz,),
        in_specs=[pl.BlockSpec((1, seq_len, RWKV_COLS), lambda b: (b, 0, 0)),
                  pl.BlockSpec((1, 2, H_A, HEAD_A, HEAD_A), lambda b: (b, 0, 0, 0, 0))]
                 + [_const_spec(a.shape) for a in consts],
        out_specs=[pl.BlockSpec((1, seq_len, DA), lambda b: (b, 0, 0)),
                   pl.BlockSpec((1, 2, H_A, HEAD_A, HEAD_A), lambda b: (b, 0, 0, 0, 0))],
        out_shape=[jax.ShapeDtypeStruct((bsz, seq_len, DA), F32),
                   jax.ShapeDtypeStruct((bsz, 2, H_A, HEAD_A, HEAD_A), F32)],
        scratch_shapes=[pltpu.VMEM((seq_len, DA), F32)] * 5
                       + [pltpu.VMEM((seq_len, DECAY_LORA + AAA_LORA), F32),
                          pltpu.VMEM((2, N_PACKS, PACK_W, PACK_W), F32)],
        compiler_params=_params("arbitrary"),
        name="rwkv_scan",
    )(zr, s0, *consts)


def _mlstm_kernel(zm_ref, g_ref, gt_ref, c0_ref, n0_ref, m0_ref, conv_ref, bi_ref, bit_ref, bf_ref, bft_ref,
                  ng_ref, y_ref, cout_ref, nout_ref, mout_ref,
                  qk_s, h_s, c_s, n_s, m_s, *, seq_len):
    nc = seq_len // CHUNK

    def conv_chunk(c, carry):
        zc, prev, nxt = _seq_neighbours(zm_ref, 0, c, nc, slice(0, 2 * DB))
        u = conv_ref[0:1, :] * prev + conv_ref[1:2, :] * zc + conv_ref[2:3, :] * nxt
        qk = u * _sigmoid(u)
        lane = lax.broadcasted_iota(jnp.int32, qk.shape, 1)
        qk_s[pl.ds(pl.multiple_of(c * CHUNK, CHUNK), CHUNK), :] = jnp.where(lane >= DB, qk * (HEAD_B ** -0.5), qk)
        return carry

    lax.fori_loop(0, nc, conv_chunk, 0)

    for d in range(2):
        c_s[d] = c0_ref[0, d]
        n_s[d] = n0_ref[0, d]
        m_s[d] = jnp.broadcast_to(m0_ref[0, d], (H_B, LANES))

    def scan_step(i, carry):
        chains = []
        for d in range(2):
            reverse = d == 1
            incl = _tri(CHUNK, reverse, False)
            tri_col = incl.astype(F32)
            tri_row = _tri(CHUNK, not reverse, False).astype(F32)
            last = 0 if reverse else CHUNK - 1
            gsl = slice(d * H_B, (d + 1) * H_B)
            fsl = slice(2 * H_B + d * H_B, 2 * H_B + (d + 1) * H_B)
            c = nc - 1 - i if reverse else i
            rows = pl.ds(pl.multiple_of(c * CHUNK, CHUNK), CHUNK)
            gcol = g_ref[0, rows, :]
            grow = gt_ref[0, c]
            i_col = gcol[:, gsl] + bi_ref[:, gsl]
            f_col = _log_sigmoid(gcol[:, fsl] + bf_ref[:, gsl])
            i_row = grow[gsl, :] + bit_ref[gsl, :]
            f_row = _log_sigmoid(grow[fsl, :] + bft_ref[gsl, :])
            b_col = _mm(tri_col, f_col)
            b_row = _mm(f_row, tri_row)
            b_last = b_col[last:last + 1]
            for h in range(H_B):
                q = qk_s[rows, h * HEAD_B:(h + 1) * HEAD_B]
                chains.append(dict(
                    d=d, h=h, rows=rows, incl=incl, q=q, qb=q.astype(BF16),
                    k=qk_s[rows, DB + h * HEAD_B:DB + (h + 1) * HEAD_B],
                    v=zm_ref[0, rows, 2 * DB + h * HEAD_B:2 * DB + (h + 1) * HEAD_B],
                    bc=b_col[:, h:h + 1], br=b_row[h:h + 1, :], ic=i_col[:, h:h + 1], ir=i_row[h:h + 1, :],
                    bl=b_last[:, h:h + 1]))
        for ch in chains:
            d, h = ch['d'], ch['h']
            ch['qk'] = _dot_bf16(ch['qb'], ch['k'], NT)
            ch['c_prev'] = c_s[d, h]
            ch['qc'] = _dot_bf16(ch['qb'], ch['c_prev'])
        for ch in chains:
            d, h = ch['d'], ch['h']
            m_prev = m_s[d, h:h + 1, 0:1]
            dmat = jnp.where(ch['incl'], ch['bc'] - ch['br'] + ch['ir'], -jnp.inf)
            inter = ch['bc'] + m_prev
            m_t = jnp.maximum(inter, jnp.max(dmat, axis=-1, keepdims=True))
            s = ch['qk'] * jnp.exp(dmat - m_t)
            carry_w = jnp.exp(inter - m_t)
            n_prev = n_s[d, h:h + 1, :]
            num = _dot_bf16(s, ch['v']) + carry_w * ch['qc']
            den = (jnp.sum(s, axis=-1, keepdims=True)
                   + carry_w * jnp.sum(ch['q'] * n_prev, axis=-1, keepdims=True))
            h_s[d, ch['rows'], h * HEAD_B:(h + 1) * HEAD_B] = num / jnp.maximum(jnp.abs(den), jnp.exp(-m_t))
            g = ch['bl'] - ch['bc'] + ch['ic']
            m_new = jnp.maximum(ch['bl'] + m_prev, jnp.max(g, axis=0, keepdims=True))
            wk = jnp.exp(g - m_new)
            decay = jnp.exp(ch['bl'] + m_prev - m_new)
            c_s[d, h] = decay * ch['c_prev'] + _dot_bf16(ch['k'], wk * ch['v'], TN)
            n_s[d, h:h + 1, :] = decay * n_prev + jnp.sum(wk * ch['k'], axis=0, keepdims=True)
            m_s[d, h:h + 1, :] = jnp.broadcast_to(m_new, (1, LANES))
        return carry

    lax.fori_loop(0, nc, scan_step, 0)
    cout_ref[0] = c_s[...]
    nout_ref[0] = n_s[...]
    mout_ref[0] = m_s[...]

    def norm_chunk(c, carry):
        rows = pl.ds(pl.multiple_of(c * CHUNK, CHUNK), CHUNK)
        og = zm_ref[0, rows, 3 * DB:4 * DB]
        hm = h_s[0, rows, :] + h_s[1, rows, :]
        parts = []
        for h in range(H_B):
            hh = hm[:, h * HEAD_B:(h + 1) * HEAD_B]
            parts.append(hh * lax.rsqrt(jnp.mean(hh * hh, axis=-1, keepdims=True) + NORM_EPS))
        y_ref[0, rows, :] = jnp.concatenate(parts, axis=1) * ng_ref[...] * _sigmoid(og)
        return carry

    lax.fori_loop(0, nc, norm_chunk, 0)


def _mlstm(zm, g, gt, c0, n0, m0, p):
    bsz, seq_len, _ = zm.shape
    nc = seq_len // CHUNK
    kern = functools.partial(_mlstm_kernel, seq_len=seq_len)
    consts = [p['conv'], p['bi'], p['bit'], p['bf'], p['bft'], p['ng']]
    return pl.pallas_call(
        kern,
        grid=(bsz,),
        in_specs=[pl.BlockSpec((1, seq_len, MLSTM_MAIN), lambda b: (b, 0, 0)),
                  pl.BlockSpec((1, seq_len, N_GATES), lambda b: (b, 0, 0)),
                  pl.BlockSpec((1, nc, N_GATES, CHUNK), lambda b: (b, 0, 0, 0)),
                  pl.BlockSpec((1, 2, H_B, HEAD_B, HEAD_B), lambda b: (b, 0, 0, 0, 0)),
                  pl.BlockSpec((1, 2, H_B, HEAD_B), lambda b: (b, 0, 0, 0)),
                  pl.BlockSpec((1, 2, H_B, 1), lambda b: (b, 0, 0, 0))]
                 + [_const_spec(a.shape) for a in consts],
        out_specs=[pl.BlockSpec((1, seq_len, DB), lambda b: (b, 0, 0)),
                   pl.BlockSpec((1, 2, H_B, HEAD_B, HEAD_B), lambda b: (b, 0, 0, 0, 0)),
                   pl.BlockSpec((1, 2, H_B, HEAD_B), lambda b: (b, 0, 0, 0)),
                   pl.BlockSpec((1, 2, H_B, LANES), lambda b: (b, 0, 0, 0))],
        out_shape=[jax.ShapeDtypeStruct((bsz, seq_len, DB), F32),
                   jax.ShapeDtypeStruct((bsz, 2, H_B, HEAD_B, HEAD_B), F32),
                   jax.ShapeDtypeStruct((bsz, 2, H_B, HEAD_B), F32),
                   jax.ShapeDtypeStruct((bsz, 2, H_B, LANES), F32)],
        scratch_shapes=[pltpu.VMEM((seq_len, 2 * DB), F32), pltpu.VMEM((2, seq_len, DB), F32),
                        pltpu.VMEM((2, H_B, HEAD_B, HEAD_B), F32), pltpu.VMEM((2, H_B, HEAD_B), F32),
                        pltpu.VMEM((2, H_B, LANES), F32)],
        compiler_params=_params("arbitrary"),
        name="mlstm_scan",
    )(zm, g, gt, c0, n0, m0, *consts)


def _outproj_kernel(x_ref, yr_ref, ym_ref, mod_ref, g_ref, wo_ref, rw_ref, rb_ref,
                    x1_ref, hn_ref, idx_ref, gate_ref):
    mod = mod_ref[0]
    g1 = mod[:, 2 * D_MODEL:3 * D_MODEL]
    sh2, sc2 = mod[:, 3 * D_MODEL:4 * D_MODEL], mod[:, 4 * D_MODEL:5 * D_MODEL]
    mix = _dot_bf16(yr_ref[...], wo_ref[0:DA, :]) + _dot_bf16(ym_ref[...], wo_ref[DA:, :])
    x1 = x_ref[...] + g1 * mix
    x1_ref[...] = x1
    y = x1 * lax.rsqrt(jnp.mean(x1 * x1, axis=-1, keepdims=True) + NORM_EPS)
    hn = y * g_ref[...] * (1.0 + sc2) + sh2
    hn_ref[...] = hn
    logits = _dot_bf16(hn, rw_ref[...]) + rb_ref[...]
    lane = lax.broadcasted_iota(jnp.int32, logits.shape, 1)
    vals, idxs = [], []
    for _ in range(TOP_K):
        top = jnp.max(logits, axis=-1, keepdims=True)
        pick = jnp.min(jnp.where(logits == top, lane, N_EXPERTS), axis=-1, keepdims=True)
        vals.append(top)
        idxs.append(pick)
        logits = jnp.where(lane == pick, -jnp.inf, logits)
    exps = [jnp.exp(v - vals[0]) for v in vals]
    total = exps[0] + exps[1] + exps[2] + exps[3]
    k_lane = lax.broadcasted_iota(jnp.int32, (x1.shape[0], TOP_K), 1)
    gates = jnp.zeros((x1.shape[0], TOP_K), F32)
    picks = jnp.zeros((x1.shape[0], TOP_K), jnp.int32)
    for j in range(TOP_K):
        gates = jnp.where(k_lane == j, exps[j] / total, gates)
        picks = jnp.where(k_lane == j, idxs[j], picks)
    idx_ref[...] = picks
    gate_ref[...] = gates


def _outproj(x2, yr, ym, mod, norm_g, w_out, router_w, router_b):
    rows = x2.shape[0]
    tiles_per_mod = rows // mod.shape[0] // ROW_TILE
    row_spec = lambda w: pl.BlockSpec((ROW_TILE, w), lambda i: (i, 0))
    return pl.pallas_call(
        _outproj_kernel,
        grid=(rows // ROW_TILE,),
        in_specs=[row_spec(D_MODEL), row_spec(DA), row_spec(DB),
                  pl.BlockSpec((1, 1, 6 * D_MODEL), lambda i: (i // tiles_per_mod, 0, 0)),
                  _const_spec((1, D_MODEL)), _const_spec(w_out.shape), _const_spec(router_w.shape),
                  _const_spec((1, N_EXPERTS))],
        out_specs=[row_spec(D_MODEL), row_spec(D_MODEL), row_spec(TOP_K), row_spec(TOP_K)],
        out_shape=[jax.ShapeDtypeStruct((rows, D_MODEL), F32), jax.ShapeDtypeStruct((rows, D_MODEL), F32),
                   jax.ShapeDtypeStruct((rows, TOP_K), jnp.int32), jax.ShapeDtypeStruct((rows, TOP_K), F32)],
        compiler_params=_params("arbitrary"),
        name="outproj_router",
    )(x2, yr, ym, mod, norm_g.reshape(1, D_MODEL), w_out, router_w, router_b.reshape(1, N_EXPERTS))


def _rank_kernel(idx_ref, rank_ref, count_ref, run_s):
    i = pl.program_id(0)

    @pl.when(i == 0)
    def _():
        run_s[...] = jnp.zeros_like(run_s)

    idx = idx_ref[...]
    rows = idx.shape[0]
    lane = lax.broadcasted_iota(jnp.int32, (rows, N_EXPERTS), 1)
    hot = jnp.zeros((rows, N_EXPERTS), F32)
    for j in range(TOP_K):
        hot = hot + jnp.where(lane == idx[:, j:j + 1], 1.0, 0.0)
    before = _dot_bf16(_tri(rows, False, True).astype(F32), hot) + run_s[...]
    k_lane = lax.broadcasted_iota(jnp.int32, (rows, TOP_K), 1)
    rank = jnp.zeros((rows, TOP_K), F32)
    for j in range(TOP_K):
        rj = jnp.sum(jnp.where(lane == idx[:, j:j + 1], before, 0.0), axis=-1, keepdims=True)
        rank = jnp.where(k_lane == j, rj, rank)
    rank_ref[...] = rank.astype(jnp.int32)
    run_s[...] = run_s[...] + jnp.sum(hot, axis=0, keepdims=True)
    count_ref[...] = run_s[...].astype(jnp.int32)


def _expert_ranks(idx):
    rows = idx.shape[0]
    return pl.pallas_call(
        _rank_kernel,
        grid=(rows // ROW_TILE,),
        in_specs=[pl.BlockSpec((ROW_TILE, TOP_K), lambda i: (i, 0))],
        out_specs=[pl.BlockSpec((ROW_TILE, TOP_K), lambda i: (i, 0)), _const_spec((1, N_EXPERTS))],
        out_shape=[jax.ShapeDtypeStruct((rows, TOP_K), jnp.int32),
                   jax.ShapeDtypeStruct((1, N_EXPERTS), jnp.int32)],
        scratch_shapes=[pltpu.VMEM((1, N_EXPERTS), F32)],
        compiler_params=_params("arbitrary"),
        name="expert_ranks",
    )(idx)


def _dispatch_kernel(dest_ref, x_ref, init_ref, xs_ref, sem):
    del init_ref

    def copy(r, slot):
        return pltpu.make_async_copy(x_ref.at[pl.ds(r, 1)], xs_ref.at[pl.ds(slot, 1)], sem)

    def issue(r, carry):
        for j in range(TOP_K):
            copy(r, dest_ref[r * TOP_K + j]).start()
        return carry

    lax.fori_loop(0, ROW_TILE, issue, 0)

    def drain(r, carry):
        for j in range(TOP_K):
            copy(0, 0).wait()
        return carry

    lax.fori_loop(0, ROW_TILE, drain, 0)


def _dispatch(dest_flat, x2, n_slots):
    rows = x2.shape[0]
    init = jnp.zeros((n_slots, D_MODEL), x2.dtype)
    return pl.pallas_call(
        _dispatch_kernel,
        grid=(rows // ROW_TILE,),
        in_specs=[pl.BlockSpec((ROW_TILE * TOP_K,), lambda i: (i,), memory_space=pltpu.SMEM),
                  pl.BlockSpec((ROW_TILE, D_MODEL), lambda i: (i, 0)),
                  pl.BlockSpec(memory_space=pl.ANY)],
        out_specs=pl.BlockSpec(memory_space=pl.ANY),
        out_shape=jax.ShapeDtypeStruct(init.shape, init.dtype),
        scratch_shapes=[pltpu.SemaphoreType.DMA(())],
        input_output_aliases={2: 0},
        compiler_params=_params("arbitrary"),
        name="moe_dispatch",
    )(dest_flat, x2, init)


def _expert_kernel(be_ref, nb_ref, xs_ref, w1_ref, b1_ref, w2_ref, b2_ref, y_ref, w1_s, w2_s):
    i = pl.program_id(0)
    changed = jnp.logical_or(i == 0, be_ref[i] != be_ref[jnp.maximum(i - 1, 0)])

    @pl.when(jnp.logical_and(changed, i < nb_ref[0]))
    def _():
        w1_s[...] = w1_ref[0].astype(BF16)
        w2_s[...] = w2_ref[0].astype(BF16)

    @pl.when(i < nb_ref[0])
    def _():
        xb = xs_ref[...].astype(BF16)
        hu = lax.dot_general(xb, w1_s[...], NN, preferred_element_type=F32) + b1_ref[0]
        glu = jnp.minimum(hu[:, :D_FF], SWIGLU_LIMIT)
        lin = jnp.clip(hu[:, D_FF:], -SWIGLU_LIMIT, SWIGLU_LIMIT)
        act = glu * _sigmoid(SWIGLU_ALPHA * glu) * (lin + 1.0)
        y_ref[...] = lax.dot_general(act.astype(BF16), w2_s[...], NN, preferred_element_type=F32) + b2_ref[0]

    @pl.when(i >= nb_ref[0])
    def _():
        y_ref[...] = jnp.zeros_like(y_ref)


def _experts(block_e, n_used, xs, w1, b1, w2, b2):
    n_blocks = xs.shape[0] // MOE_BLOCK
    grid_spec = pltpu.PrefetchScalarGridSpec(
        num_scalar_prefetch=2,
        grid=(n_blocks,),
        in_specs=[pl.BlockSpec((MOE_BLOCK, D_MODEL), lambda i, be, nb: (i, 0)),
                  pl.BlockSpec((1, D_MODEL, 2 * D_FF), lambda i, be, nb: (be[i], 0, 0)),
                  pl.BlockSpec((1, 1, 2 * D_FF), lambda i, be, nb: (be[i], 0, 0)),
                  pl.BlockSpec((1, D_FF, D_MODEL), lambda i, be, nb: (be[i], 0, 0)),
                  pl.BlockSpec((1, 1, D_MODEL), lambda i, be, nb: (be[i], 0, 0))],
        out_specs=pl.BlockSpec((MOE_BLOCK, D_MODEL), lambda i, be, nb: (i, 0)),
        scratch_shapes=[pltpu.VMEM((D_MODEL, 2 * D_FF), BF16), pltpu.VMEM((D_FF, D_MODEL), BF16)],
    )
    return pl.pallas_call(
        _expert_kernel,
        grid_spec=grid_spec,
        out_shape=jax.ShapeDtypeStruct(xs.shape, F32),
        compiler_params=_params("arbitrary"),
        name="moe_experts",
    )(block_e, n_used, xs, w1, b1.reshape(N_EXPERTS, 1, 2 * D_FF), w2, b2.reshape(N_EXPERTS, 1, D_MODEL))


def _combine_kernel(dest_ref, yb_ref, x1_ref, gate_ref, mod_ref, fg_ref, o_ref, buf, sem):
    def copy(slot, j, r):
        return pltpu.make_async_copy(yb_ref.at[pl.ds(slot, 1)], buf.at[j, pl.ds(r, 1)], sem)

    def issue(r, carry):
        for j in range(TOP_K):
            copy(dest_ref[r * TOP_K + j], j, r).start()
        return carry

    lax.fori_loop(0, ROW_TILE, issue, 0)

    def drain(r, carry):
        for j in range(TOP_K):
            copy(0, 0, 0).wait()
        return carry

    lax.fori_loop(0, ROW_TILE, drain, 0)

    g2 = mod_ref[0][:, 5 * D_MODEL:6 * D_MODEL]
    gates = gate_ref[...]
    moe = jnp.zeros((ROW_TILE, D_MODEL), F32)
    for j in range(TOP_K):
        moe = moe + buf[j] * gates[:, j:j + 1]
    x2 = x1_ref[...] + g2 * moe
    y = x2 * lax.rsqrt(jnp.mean(x2 * x2, axis=-1, keepdims=True) + NORM_EPS)
    o_ref[...] = y * fg_ref[...]


def _combine(dest_flat, yb, x1, gates, mod, final_g, row_offset):
    rows = x1.shape[0]
    tiles_per_mod = rows // mod.shape[0] // ROW_TILE
    tile_offset = row_offset // ROW_TILE
    row_spec = pl.BlockSpec((ROW_TILE, D_MODEL), lambda i: (i, 0))
    return pl.pallas_call(
        _combine_kernel,
        grid=(rows // ROW_TILE,),
        in_specs=[pl.BlockSpec((ROW_TILE * TOP_K,), lambda i: (i + tile_offset,), memory_space=pltpu.SMEM),
                  pl.BlockSpec(memory_space=pl.ANY),
                  row_spec,
                  pl.BlockSpec((ROW_TILE, TOP_K), lambda i: (i, 0)),
                  pl.BlockSpec((1, 1, 6 * D_MODEL), lambda i: (i // tiles_per_mod, 0, 0)),
                  _const_spec((1, D_MODEL))],
        out_specs=row_spec,
        out_shape=jax.ShapeDtypeStruct((rows, D_MODEL), F32),
        scratch_shapes=[pltpu.VMEM((TOP_K, ROW_TILE, D_MODEL), F32), pltpu.SemaphoreType.DMA(())],
        compiler_params=_params("arbitrary"),
        name="moe_combine",
    )(dest_flat, yb, x1, gates, mod, final_g.reshape(1, D_MODEL))


def _block_diag_ones(width, block):
    i = jnp.arange(width) // block
    return (i[:, None] == i[None, :]).astype(F32)


def _mixers(x, mod, lp, s0, c0, n0, m0, grid):
    bsz, seq_len, _ = x.shape
    nc = seq_len // CHUNK
    x2 = x.reshape(bsz * seq_len, D_MODEL)
    zr, zm, zg, zgt = _inproj(x2, mod, lp['norm1_g'], lp['w_r'], lp['w_m'], lp['w_g'], lp['w_gt'])
    gt = zgt.reshape(N_GATES, bsz, nc, CHUNK).transpose(1, 2, 0, 3)
    yr, s_new = _rwkv(zr.reshape(bsz, seq_len, RWKV_COLS), s0, lp, grid)
    ym, c_new, n_new, m_new = _mlstm(zm.reshape(bsz, seq_len, MLSTM_MAIN), zg.reshape(bsz, seq_len, N_GATES), gt,
                                     c0, n0, m0.reshape(bsz, 2, H_B, 1), lp)
    x1, hn, idx, gates = _outproj(x2, yr.reshape(-1, DA), ym.reshape(-1, DB), mod, lp['norm2_g'], lp['w_out'],
                                  lp['router_w'], lp['router_b'])
    return x1, hn, idx, gates, (s_new, c_new, n_new, m_new[..., 0])


def kernel(x_prompt, x_sample, state_rwkv, state_mlstm_C, state_mlstm_n, state_mlstm_m, c, c_ctx, ada_w, ada_b, norm1_g, norm2_g, w_in, w_out, rwkv_mu, rwkv_w0, rwkv_w2, rwkv_a0, rwkv_a2, rwkv_g2, rwkv_kk, rwkv_ka, rwkv_rk, rwkv_gn_w, rwkv_gn_b, mlstm_conv, mlstm_bi, mlstm_bf, mlstm_norm_g, router_w, router_b, moe_w1, moe_b1, moe_w2, moe_b2, final_g):
    bp, lp_len, _ = x_prompt.shape
    bs, ls_len, _ = x_sample.shape
    w = w_in[0]
    lp = {
        'norm1_g': norm1_g[0], 'norm2_g': norm2_g[0],
        'w_r': w[:, :RWKV_COLS].astype(BF16),
        'w_m': w[:, RWKV_COLS:RWKV_COLS + MLSTM_MAIN].astype(BF16),
        'w_g': w[:, RWKV_COLS + MLSTM_MAIN:],
        'w_gt': w[:, RWKV_COLS + MLSTM_MAIN:].T,
        'w_out': w_out[0].astype(BF16),
        'mu': rwkv_mu[0].reshape(1, RWKV_COLS), 'kkw': rwkv_kk[0].reshape(1, DA), 'ka': rwkv_ka[0].reshape(1, DA),
        'rk': rwkv_rk[0].reshape(1, DA), 'gnw': rwkv_gn_w[0].reshape(1, DA), 'gnb': rwkv_gn_b[0].reshape(1, DA),
        'w0': rwkv_w0[0], 'w2': rwkv_w2[0], 'a0': rwkv_a0[0], 'a2': rwkv_a2[0], 'g2': rwkv_g2[0],
        'ones_bd': _block_diag_ones(PACK_W, HEAD_A).astype(BF16),
        'conv': mlstm_conv[0], 'bi': mlstm_bi[0].reshape(1, 2 * H_B), 'bit': mlstm_bi[0].reshape(2 * H_B, 1),
        'bf': mlstm_bf[0].reshape(1, 2 * H_B), 'bft': mlstm_bf[0].reshape(2 * H_B, 1),
        'ng': mlstm_norm_g[0].reshape(1, DB),
        'router_w': router_w[0], 'router_b': router_b[0],
    }
    cc = jnp.concatenate([c_ctx[None, :], c, jnp.zeros((2 * SUBLANES - 1 - bs, D_MODEL), F32)], axis=0)
    mod = _ada_mod(cc, ada_w[0], ada_b[0])
    mod_p = mod[0:1].reshape(1, 1, 6 * D_MODEL)
    mod_s = mod[1:1 + bs].reshape(bs, 1, 6 * D_MODEL)

    zeros = lambda *shape: jnp.zeros(shape, F32)
    x1p, hnp, idxp, gatesp, st = _mixers(x_prompt, mod_p, lp, zeros(bp, 2, H_A, HEAD_A, HEAD_A),
                                         zeros(bp, 2, H_B, HEAD_B, HEAD_B), zeros(bp, 2, H_B, HEAD_B),
                                         zeros(bp, 2, H_B), False)
    x1s, hns, idxs, gatess, _ = _mixers(x_sample, mod_s, lp, state_rwkv[:, 0], state_mlstm_C[:, 0],
                                        state_mlstm_n[:, 0], state_mlstm_m[:, 0], True)

    n_p = bp * lp_len
    idx = jnp.concatenate([idxp, idxs], axis=0)
    hn = jnp.concatenate([hnp, hns], axis=0)
    n_tok = idx.shape[0]
    rank, counts = _expert_ranks(idx)
    counts = counts[0]
    padded = (counts + MOE_BLOCK - 1) // MOE_BLOCK * MOE_BLOCK
    pad_end = jnp.cumsum(padded)
    pad_start = pad_end - padded
    n_blocks = n_tok * TOP_K // MOE_BLOCK + N_EXPERTS
    block_start = jnp.arange(n_blocks, dtype=jnp.int32) * MOE_BLOCK
    block_e = jnp.minimum(jnp.sum(pad_end[None, :] <= block_start[:, None], axis=1), N_EXPERTS - 1).astype(jnp.int32)
    n_used = (pad_end[-1:] // MOE_BLOCK).astype(jnp.int32)
    dest = (pad_start[idx] + rank).astype(jnp.int32).reshape(-1)
    xs = _dispatch(dest, hn, n_blocks * MOE_BLOCK)
    yb = _experts(block_e, n_used, xs, moe_w1[0], moe_b1[0], moe_w2[0], moe_b2[0])
    y_prompt = _combine(dest, yb, x1p, gatesp, mod_p, final_g, 0).reshape(x_prompt.shape)
    y_sample = _combine(dest, yb, x1s, gatess, mod_s, final_g, n_p).reshape(x_sample.shape)

    s_new, c_new, n_new, m_new = st
    return (y_prompt, y_sample, s_new[:, None], c_new[:, None], n_new[:, None], m_new[:, None])
```

```python
import functools

import jax
import jax.numpy as jnp
from jax import lax
from jax.experimental import pallas as pl
from jax.experimental.pallas import tpu as pltpu

F32 = jnp.float32
BF16 = jnp.bfloat16
HIGHEST = lax.Precision.HIGHEST

D_MODEL = 1024
DA = 512
HEAD_A = 64
H_A = DA // HEAD_A
DB = 512
H_B = 4
HEAD_B = DB // H_B
DECAY_LORA = 64
AAA_LORA = 64
GATE_LORA = 128
RWKV_COLS = 3 * DA + DECAY_LORA + AAA_LORA + GATE_LORA
MLSTM_MAIN = 4 * DB
N_GATES = 4 * H_B
GRID_W = 64
CHUNK = 64
N_EXPERTS = 32
TOP_K = 4
D_FF = D_MODEL
SWIGLU_LIMIT = 7.0
SWIGLU_ALPHA = 1.702
MOE_BLOCK = 256
NORM_EPS = 1e-6
GN_EPS = 64e-5
ROW_TILE = 256
DMA_UNROLL = 4
SUBLANES = 8
LANES = 128
LANE_ROWS = D_MODEL // LANES
VMEM_LIMIT = 56 * 1024 * 1024
PACK = 4
PACK_W = PACK * HEAD_A
N_PACKS = H_A // PACK

NN = (((1,), (0,)), ((), ()))
NT = (((1,), (1,)), ((), ()))
TN = (((0,), (0,)), ((), ()))


def _mm(a, b, precision=HIGHEST):
    return lax.dot_general(a, b, NN, precision=precision, preferred_element_type=F32)


def _mm_nt(a, b, precision=HIGHEST):
    return lax.dot_general(a, b, NT, precision=precision, preferred_element_type=F32)


def _mm_tn(a, b, precision=HIGHEST):
    return lax.dot_general(a, b, TN, precision=precision, preferred_element_type=F32)


def _dot_bf16(a, b, dims=NN):
    return lax.dot_general(a.astype(BF16), b.astype(BF16), dims, preferred_element_type=F32)


def _sigmoid(x):
    return 1.0 / (1.0 + jnp.exp(-x))


def _log_sigmoid(x):
    return jnp.minimum(x, 0.0) - jnp.log(1.0 + jnp.exp(-jnp.abs(x)))


def _params(*sem, **kw):
    return pltpu.CompilerParams(dimension_semantics=sem, vmem_limit_bytes=VMEM_LIMIT, **kw)


def _split_bf16(x):
    hi = x.astype(BF16)
    return hi, (x - hi.astype(F32)).astype(BF16)


def _const_spec(shape):
    nd = len(shape)
    return pl.BlockSpec(shape, lambda *_: (0,) * nd)


def _tri(n, reverse, strict):
    t = lax.broadcasted_iota(jnp.int32, (n, n), 0)
    s = lax.broadcasted_iota(jnp.int32, (n, n), 1)
    if reverse:
        return (s > t) if strict else (s >= t)
    return (s < t) if strict else (s <= t)


def _seq_neighbours(ref, b, c, nc, cols):
    r0 = pl.multiple_of(c * CHUNK, CHUNK)
    zc = ref[b, pl.ds(r0, CHUNK), cols]
    row = lax.broadcasted_iota(jnp.int32, zc.shape, 0)
    p0 = pl.multiple_of(jnp.maximum(r0 - SUBLANES, 0), SUBLANES)
    n0 = pl.multiple_of(jnp.minimum(r0 + CHUNK, (nc - 1) * CHUNK), SUBLANES)
    before = ref[b, pl.ds(p0, SUBLANES), cols][SUBLANES - 1:SUBLANES]
    after = ref[b, pl.ds(n0, SUBLANES), cols][0:1]
    before = jnp.where(c > 0, before, 0.0)
    after = jnp.where(c < nc - 1, after, 0.0)
    prev = jnp.where(row == 0, before, pltpu.roll(zc, 1, 0))
    nxt = jnp.where(row == CHUNK - 1, after, pltpu.roll(zc, CHUNK - 1, 0))
    return zc, prev, nxt


def _ada_kernel(c_ref, w_ref, b_ref, o_ref):
    cc = c_ref[...]
    o_ref[...] = _dot_bf16(cc * _sigmoid(cc), w_ref[...]) + b_ref[...]


def _ada_mod(cc, ada_w, ada_b):
    rows = cc.shape[0]
    ncol = ada_w.shape[1]
    tn = 1536
    return pl.pallas_call(
        _ada_kernel,
        grid=(ncol // tn,),
        in_specs=[_const_spec((rows, D_MODEL)),
                  pl.BlockSpec((D_MODEL, tn), lambda j: (0, j)),
                  pl.BlockSpec((1, tn), lambda j: (0, j))],
        out_specs=pl.BlockSpec((rows, tn), lambda j: (0, j)),
        out_shape=jax.ShapeDtypeStruct((rows, ncol), F32),
        compiler_params=_params("arbitrary"),
        name="ada_mod",
    )(cc, ada_w, ada_b.reshape(1, ncol))


def _inproj_kernel(x_ref, mod_ref, g_ref, wr_ref, wm_ref, wg_ref, wgt_ref, zr_ref, zm_ref, zg_ref, zgt_ref):
    x = x_ref[...]
    mod = mod_ref[0]
    sh, sc = mod[:, 0:D_MODEL], mod[:, D_MODEL:2 * D_MODEL]
    y = x * lax.rsqrt(jnp.mean(x * x, axis=-1, keepdims=True) + NORM_EPS)
    hn = y * g_ref[...] * (1.0 + sc) + sh
    hb = hn.astype(BF16)
    zr_ref[...] = lax.dot_general(hb, wr_ref[...], NN, preferred_element_type=F32)
    zm_ref[...] = lax.dot_general(hb, wm_ref[...], NN, preferred_element_type=F32)
    zg_ref[...] = _dot_bf16(hb, wg_ref[...])
    zgt_ref[...] = _dot_bf16(wgt_ref[...], hb, NT)


def _inproj(x2, mod, norm_g, w_r, w_m, w_g, w_gt):
    rows = x2.shape[0]
    tiles_per_mod = rows // mod.shape[0] // ROW_TILE
    return pl.pallas_call(
        _inproj_kernel,
        grid=(rows // ROW_TILE,),
        in_specs=[pl.BlockSpec((ROW_TILE, D_MODEL), lambda i: (i, 0)),
                  pl.BlockSpec((1, 1, 6 * D_MODEL), lambda i: (i // tiles_per_mod, 0, 0)),
                  _const_spec((1, D_MODEL)),
                  _const_spec(w_r.shape), _const_spec(w_m.shape), _const_spec(w_g.shape),
                  _const_spec(w_gt.shape)],
        out_specs=[pl.BlockSpec((ROW_TILE, RWKV_COLS), lambda i: (i, 0)),
                   pl.BlockSpec((ROW_TILE, MLSTM_MAIN), lambda i: (i, 0)),
                   pl.BlockSpec((ROW_TILE, N_GATES), lambda i: (i, 0)),
                   pl.BlockSpec((N_GATES, ROW_TILE), lambda i: (0, i))],
        out_shape=[jax.ShapeDtypeStruct((rows, RWKV_COLS), F32),
                   jax.ShapeDtypeStruct((rows, MLSTM_MAIN), F32),
                   jax.ShapeDtypeStruct((rows, N_GATES), F32),
                   jax.ShapeDtypeStruct((N_GATES, rows), F32)],
        compiler_params=_params("arbitrary"),
        name="inproj",
    )(x2, mod, norm_g.reshape(1, D_MODEL), w_r, w_m, w_g, w_gt)


def _group_sum(x, ones_bd):
    hi = x.astype(BF16)
    lo = (x - hi.astype(F32)).astype(BF16)
    return (lax.dot_general(hi, ones_bd, NN, preferred_element_type=F32)
            + lax.dot_general(lo, ones_bd, NN, preferred_element_type=F32))


def _block_diag(x, ones_bd):
    return jnp.concatenate([x.astype(BF16)] * PACK, axis=0) * ones_bd


def _rwkv_kernel(zr_ref, s0_ref, mu_ref, kkw_ref, ka_ref, rk_ref, gnw_ref, gnb_ref, w0_ref, w2_ref,
                 a0_ref, a2_ref, g2_ref, ones_ref, y_ref, sout_ref,
                 r_s, k_s, v_s, kk_s, gate_s, lx_s, st_s, *, seq_len, grid):
    nc = seq_len // CHUNK
    ones_bd = ones_ref[...]

    def mix_chunk(c, carry):
        r0 = pl.multiple_of(c * CHUNK, CHUNK)
        if grid:
            zc = zr_ref[0, pl.ds(r0, CHUNK), :]
            row = lax.broadcasted_iota(jnp.int32, zc.shape, 0)
            up0 = pl.multiple_of(jnp.maximum(c - 1, 0) * CHUNK, CHUNK)
            dn0 = pl.multiple_of(jnp.minimum(c + 1, nc - 1) * CHUNK, CHUNK)
            up = jnp.where(c > 0, zr_ref[0, pl.ds(up0, CHUNK), :], 0.0)
            down = jnp.where(c < nc - 1, zr_ref[0, pl.ds(dn0, CHUNK), :], 0.0)
            left = jnp.where(row == 0, 0.0, pltpu.roll(zc, 1, 0))
            right = jnp.where(row == CHUNK - 1, 0.0, pltpu.roll(zc, CHUNK - 1, 0))
            local = 0.25 * (up + down + left + right)
        else:
            zc, prev, nxt = _seq_neighbours(zr_ref, 0, c, nc, slice(None))
            local = 0.5 * (prev + nxt)
        z = zc + (local - zc) * mu_ref[...]
        r, k, v = z[:, 0:DA], z[:, DA:2 * DA], z[:, 2 * DA:3 * DA]
        lx = z[:, 3 * DA:3 * DA + DECAY_LORA + AAA_LORA]
        xg = z[:, 3 * DA + DECAY_LORA + AAA_LORA:]
        kk = k * kkw_ref[...]
        sq = kk * kk
        ss = jnp.concatenate([_group_sum(sq[:, p * PACK_W:(p + 1) * PACK_W], ones_bd) for p in range(N_PACKS)],
                             axis=1)
        kk = kk / jnp.maximum(jnp.sqrt(ss), 1e-12)
        lane = lax.broadcasted_iota(jnp.int32, lx.shape, 1)
        r_s[pl.ds(r0, CHUNK), :] = r
        k_s[pl.ds(r0, CHUNK), :] = k
        v_s[pl.ds(r0, CHUNK), :] = v
        kk_s[pl.ds(r0, CHUNK), :] = kk
        gate_s[pl.ds(r0, CHUNK), :] = _dot_bf16(_sigmoid(xg), g2_ref[...])
        lx_s[pl.ds(r0, CHUNK), :] = jnp.where(lane < DECAY_LORA, jnp.tanh(lx), lx)
        return carry

    lax.fori_loop(0, nc, mix_chunk, 0)

    rowblk = lax.broadcasted_iota(jnp.int32, (PACK_W, PACK_W), 0) // HEAD_A
    colblk = lax.broadcasted_iota(jnp.int32, (PACK_W, PACK_W), 1) // HEAD_A
    for d in range(2):
        for p in range(N_PACKS):
            rows_ = jnp.concatenate([s0_ref[0, d, p * PACK + h] for h in range(PACK)], axis=0)
            st_s[d, p] = jnp.where(rowblk == colblk, jnp.concatenate([rows_] * PACK, axis=1), 0.0)

    t_idx = lax.broadcasted_iota(jnp.int32, (CHUNK, PACK_W), 0)
    s_idx = lax.broadcasted_iota(jnp.int32, (CHUNK, PACK_W), 1) % CHUNK

    def scan_step(i, carry):
        first_touch = i < nc // 2
        dot = functools.partial(lax.dot_general, preferred_element_type=F32)
        chains, row_sel = [], []
        for d in range(2):
            reverse = d == 1
            c = nc - 1 - i if reverse else i
            rows = pl.ds(pl.multiple_of(c * CHUNK, CHUNK), CHUNK)
            row_sel.append(rows)
            strict = (s_idx > t_idx) if reverse else (s_idx < t_idx)
            incl = (s_idx >= t_idx) if reverse else (s_idx <= t_idx)
            last = 0 if reverse else CHUNK - 1
            r, k, v, kk = r_s[rows, :], k_s[rows, :], v_s[rows, :], kk_s[rows, :]
            lx = lx_s[rows, :]
            wl = w0_ref[d:d + 1, :] + _dot_bf16(lx[:, 0:DECAY_LORA], w2_ref[d])
            logw = -jnp.exp(_log_sigmoid(wl) - 0.5)
            a = _sigmoid(a0_ref[d:d + 1, :] + _dot_bf16(lx[:, DECAY_LORA:], a2_ref[d]))
            kd = k * (1.0 + (a - 1.0) * ka_ref[...])
            kb = kk * a
            tri = _tri(CHUNK, reverse, False).astype(BF16)
            w1, w_rest = _split_bf16(logw)
            w2, w3 = _split_bf16(w_rest)
            cum = dot(tri, w1, NN) + dot(tri, w2, NN) + dot(tri, w3, NN)
            clast = cum[last:last + 1]
            rt = r * jnp.exp(cum)
            at = -kk * jnp.exp(cum - logw)
            einv = jnp.exp(-cum)
            kt, bt = kd * einv, kb * einv
            edec = jnp.exp(clast - cum)
            kp, bp = kd * edec, kb * edec
            ptot = jnp.exp(clast)
            rkd = r * kd * rk_ref[...]
            for p in range(N_PACKS):
                cs = slice(p * PACK_W, (p + 1) * PACK_W)
                chains.append(dict(
                    d=d, p=p, cs=cs, strict=strict, incl=incl, v=v[:, cs], rkd=rkd[:, cs], ptot=ptot[:, cs],
                    ar=jnp.concatenate([at[:, cs], rt[:, cs]], axis=0).astype(BF16),
                    kt_bd=_block_diag(kt[:, cs], ones_bd), bt_bd=_block_diag(bt[:, cs], ones_bd),
                    v_bd=_block_diag(v[:, cs], ones_bd),
                    kbp=jnp.concatenate([kp[:, cs], bp[:, cs]], axis=0)))

        for ch in chains:
            ch['s_prev'] = st_s[ch['d'], ch['p']]
            ch['a_k'] = dot(ch['ar'], ch['kt_bd'], NT)
            ch['a_b'] = dot(ch['ar'], ch['bt_bd'], NT)
            ch['ars'] = dot(ch['ar'], ch['s_prev'].astype(BF16), NT)
        for ch in chains:
            a_ak = jnp.where(ch['strict'], ch['a_k'][:CHUNK], 0.0).astype(BF16)
            ch['a_rk'] = jnp.where(ch['incl'], ch['a_k'][CHUNK:], 0.0).astype(BF16)
            ch['a_rb'] = jnp.where(ch['incl'], ch['a_b'][CHUNK:], 0.0).astype(BF16)
            n = jnp.where(ch['strict'], ch['a_b'][:CHUNK], 0.0)
            ch['n_hi'], ch['n_lo'] = _split_bf16(n)
            ch['pw'] = ch['n_hi']
            ch['m'] = jnp.where(s_idx == t_idx, 1.0, 0.0) + n
            ch['rhs'] = ch['ars'][:CHUNK] + dot(a_ak, ch['v_bd'], NN)
        for j in range(1, 6):
            for ch in chains:
                ch['pw'] = dot(ch['pw'], _block_diag(ch['pw'], ones_bd), NN).astype(BF16)
            for ch in chains:
                ch['m'] = ch['m'] + dot(ch['m'].astype(BF16), _block_diag(ch['pw'], ones_bd), NN)
        for ch in chains:
            ch['mb'] = ch['m'].astype(BF16)
            ch['u'] = dot(ch['mb'], _block_diag(ch['rhs'], ones_bd), NN)
        for ch in chains:
            u_hi, u_lo = _split_bf16(ch['u'])
            u_hi_bd = _block_diag(u_hi, ones_bd)
            nu = (dot(ch['n_hi'], u_hi_bd, NN) + dot(ch['n_hi'], _block_diag(u_lo, ones_bd), NN)
                  + dot(ch['n_lo'], u_hi_bd, NN))
            ch['resid'] = ch['rhs'] - ch['u'] + nu
        for ch in chains:
            ch['u'] = ch['u'] + dot(ch['mb'], _block_diag(ch['resid'], ones_bd), NN)
        for ch in chains:
            u = ch['u']
            ch['y'] = (ch['ars'][CHUNK:] + dot(ch['a_rk'], ch['v_bd'], NN)
                       + dot(ch['a_rb'], _block_diag(u, ones_bd), NN))
            vu_hi, vu_lo = _split_bf16(jnp.concatenate([ch['v'], u], axis=0))
            kbp_hi, kbp_lo = _split_bf16(ch['kbp'])
            grown = dot(vu_hi, kbp_hi, TN) + dot(vu_hi, kbp_lo, TN) + dot(vu_lo, kbp_hi, TN)
            st_s[ch['d'], ch['p']] = ch['s_prev'] * ch['ptot'] + grown * ones_bd.astype(F32)
        outs = [[], []]
        for ch in chains:
            y, cs = ch['y'], ch['cs']
            sums = dot(jnp.concatenate([y, ch['rkd']], axis=0).astype(BF16), ones_bd, NN)
            yc = y - sums[:CHUNK] * (1.0 / HEAD_A)
            var = dot((yc * yc).astype(BF16), ones_bd, NN) * (1.0 / HEAD_A)
            out = yc * lax.rsqrt(var + GN_EPS) * gnw_ref[:, cs] + gnb_ref[:, cs]
            outs[ch['d']].append(out + sums[CHUNK:] * ch['v'])
        for d in range(2):
            rows = row_sel[d]
            out = jnp.concatenate(outs[d], axis=1)

            @pl.when(first_touch)
            def _(rows=rows, out=out):
                y_ref[0, rows, :] = out

            @pl.when(jnp.logical_not(first_touch))
            def _(rows=rows, out=out):
                y_ref[0, rows, :] = (y_ref[0, rows, :] + out) * gate_s[rows, :]
        return carry

    lax.fori_loop(0, nc, scan_step, 0)

    for d in range(2):
        for p in range(N_PACKS):
            s_fin = st_s[d, p]
            for h in range(PACK):
                sout_ref[0, d, p * PACK + h] = s_fin[h * HEAD_A:(h + 1) * HEAD_A, h * HEAD_A:(h + 1) * HEAD_A]


def _rwkv(zr, s0, p, grid):
    bsz, seq_len, _ = zr.shape
    assert (seq_len // CHUNK) % 2 == 0
    kern = functools.partial(_rwkv_kernel, seq_len=seq_len, grid=grid)
    consts = [p['mu'], p['kkw'], p['ka'], p['rk'], p['gnw'], p['gnb'], p['w0'], p['w2'], p['a0'], p['a2'],
              p['g2'], p['ones_bd']]
    return pl.pallas_call(
        kern,
        grid=(bsz,),
        in_specs=[pl.BlockSpec((1, seq_len, RWKV_COLS), lambda b: (b, 0, 0)),
                  pl.BlockSpec((1, 2, H_A, HEAD_A, HEAD_A), lambda b: (b, 0, 0, 0, 0))]
                 + [_const_spec(a.shape) for a in consts],
        out_specs=[pl.BlockSpec((1, seq_len, DA), lambda b: (b, 0, 0)),
                   pl.BlockSpec((1, 2, H_A, HEAD_A, HEAD_A), lambda b: (b, 0, 0, 0, 0))],
        out_shape=[jax.ShapeDtypeStruct((bsz, seq_len, DA), F32),
                   jax.ShapeDtypeStruct((bsz, 2, H_A, HEAD_A, HEAD_A), F32)],
        scratch_shapes=[pltpu.VMEM((seq_len, DA), F32)] * 5
                       + [pltpu.VMEM((seq_len, DECAY_LORA + AAA_LORA), F32),
                          pltpu.VMEM((2, N_PACKS, PACK_W, PACK_W), F32)],
        compiler_params=_params("arbitrary"),
        name="rwkv_scan",
    )(zr, s0, *consts)


def _mlstm_kernel(zm_ref, g_ref, gt_ref, c0_ref, n0_ref, m0_ref, conv_ref, bi_ref, bit_ref, bf_ref, bft_ref,
                  ng_ref, y_ref, cout_ref, nout_ref, mout_ref,
                  qk_s, h_s, c_s, n_s, m_s, *, seq_len):
    nc = seq_len // CHUNK

    def conv_chunk(c, carry):
        zc, prev, nxt = _seq_neighbours(zm_ref, 0, c, nc, slice(0, 2 * DB))
        u = conv_ref[0:1, :] * prev + conv_ref[1:2, :] * zc + conv_ref[2:3, :] * nxt
        qk = u * _sigmoid(u)
        lane = lax.broadcasted_iota(jnp.int32, qk.shape, 1)
        qk_s[pl.ds(pl.multiple_of(c * CHUNK, CHUNK), CHUNK), :] = jnp.where(lane >= DB, qk * (HEAD_B ** -0.5), qk)
        return carry

    lax.fori_loop(0, nc, conv_chunk, 0)

    for d in range(2):
        c_s[d] = c0_ref[0, d]
        n_s[d] = n0_ref[0, d]
        m_s[d] = jnp.broadcast_to(m0_ref[0, d], (H_B, LANES))

    def scan_step(i, carry):
        chains = []
        for d in range(2):
            reverse = d == 1
            incl = _tri(CHUNK, reverse, False)
            tri_col = incl.astype(F32)
            tri_row = _tri(CHUNK, not reverse, False).astype(F32)
            last = 0 if reverse else CHUNK - 1
            gsl = slice(d * H_B, (d + 1) * H_B)
            fsl = slice(2 * H_B + d * H_B, 2 * H_B + (d + 1) * H_B)
            c = nc - 1 - i if reverse else i
            rows = pl.ds(pl.multiple_of(c * CHUNK, CHUNK), CHUNK)
            gcol = g_ref[0, rows, :]
            grow = gt_ref[0, c]
            i_col = gcol[:, gsl] + bi_ref[:, gsl]
            f_col = _log_sigmoid(gcol[:, fsl] + bf_ref[:, gsl])
            i_row = grow[gsl, :] + bit_ref[gsl, :]
            f_row = _log_sigmoid(grow[fsl, :] + bft_ref[gsl, :])
            b_col = _mm(tri_col, f_col)
            b_row = _mm(f_row, tri_row)
            b_last = b_col[last:last + 1]
            for h in range(H_B):
                q = qk_s[rows, h * HEAD_B:(h + 1) * HEAD_B]
                chains.append(dict(
                    d=d, h=h, rows=rows, incl=incl, q=q, qb=q.astype(BF16),
                    k=qk_s[rows, DB + h * HEAD_B:DB + (h + 1) * HEAD_B],
                    v=zm_ref[0, rows, 2 * DB + h * HEAD_B:2 * DB + (h + 1) * HEAD_B],
                    bc=b_col[:, h:h + 1], br=b_row[h:h + 1, :], ic=i_col[:, h:h + 1], ir=i_row[h:h + 1, :],
                    bl=b_last[:, h:h + 1]))
        for ch in chains:
            d, h = ch['d'], ch['h']
            ch['qk'] = _dot_bf16(ch['qb'], ch['k'], NT)
            ch['c_prev'] = c_s[d, h]
            ch['qc'] = _dot_bf16(ch['qb'], ch['c_prev'])
        for ch in chains:
            d, h = ch['d'], ch['h']
            m_prev = m_s[d, h:h + 1, 0:1]
            dmat = jnp.where(ch['incl'], ch['bc'] - ch['br'] + ch['ir'], -jnp.inf)
            inter = ch['bc'] + m_prev
            m_t = jnp.maximum(inter, jnp.max(dmat, axis=-1, keepdims=True))
            s = ch['qk'] * jnp.exp(dmat - m_t)
            carry_w = jnp.exp(inter - m_t)
            n_prev = n_s[d, h:h + 1, :]
            num = _dot_bf16(s, ch['v']) + carry_w * ch['qc']
            den = (jnp.sum(s, axis=-1, keepdims=True)
                   + carry_w * jnp.sum(ch['q'] * n_prev, axis=-1, keepdims=True))
            h_s[d, ch['rows'], h * HEAD_B:(h + 1) * HEAD_B] = num / jnp.maximum(jnp.abs(den), jnp.exp(-m_t))
            g = ch['bl'] - ch['bc'] + ch['ic']
            m_new = jnp.maximum(ch['bl'] + m_prev, jnp.max(g, axis=0, keepdims=True))
            wk = jnp.exp(g - m_new)
            decay = jnp.exp(ch['bl'] + m_prev - m_new)
            c_s[d, h] = decay * ch['c_prev'] + _dot_bf16(ch['k'], wk * ch['v'], TN)
            n_s[d, h:h + 1, :] = decay * n_prev + jnp.sum(wk * ch['k'], axis=0, keepdims=True)
            m_s[d, h:h + 1, :] = jnp.broadcast_to(m_new, (1, LANES))
        return carry

    lax.fori_loop(0, nc, scan_step, 0)
    cout_ref[0] = c_s[...]
    nout_ref[0] = n_s[...]
    mout_ref[0] = m_s[...]

    def norm_chunk(c, carry):
        rows = pl.ds(pl.multiple_of(c * CHUNK, CHUNK), CHUNK)
        og = zm_ref[0, rows, 3 * DB:4 * DB]
        hm = h_s[0, rows, :] + h_s[1, rows, :]
        parts = []
        for h in range(H_B):
            hh = hm[:, h * HEAD_B:(h + 1) * HEAD_B]
            parts.append(hh * lax.rsqrt(jnp.mean(hh * hh, axis=-1, keepdims=True) + NORM_EPS))
        y_ref[0, rows, :] = jnp.concatenate(parts, axis=1) * ng_ref[...] * _sigmoid(og)
        return carry

    lax.fori_loop(0, nc, norm_chunk, 0)


def _mlstm(zm, g, gt, c0, n0, m0, p):
    bsz, seq_len, _ = zm.shape
    nc = seq_len // CHUNK
    kern = functools.partial(_mlstm_kernel, seq_len=seq_len)
    consts = [p['conv'], p['bi'], p['bit'], p['bf'], p['bft'], p['ng']]
    return pl.pallas_call(
        kern,
        grid=(bsz,),
        in_specs=[pl.BlockSpec((1, seq_len, MLSTM_MAIN), lambda b: (b, 0, 0)),
                  pl.BlockSpec((1, seq_len, N_GATES), lambda b: (b, 0, 0)),
                  pl.BlockSpec((1, nc, N_GATES, CHUNK), lambda b: (b, 0, 0, 0)),
                  pl.BlockSpec((1, 2, H_B, HEAD_B, HEAD_B), lambda b: (b, 0, 0, 0, 0)),
                  pl.BlockSpec((1, 2, H_B, HEAD_B), lambda b: (b, 0, 0, 0)),
                  pl.BlockSpec((1, 2, H_B, 1), lambda b: (b, 0, 0, 0))]
                 + [_const_spec(a.shape) for a in consts],
        out_specs=[pl.BlockSpec((1, seq_len, DB), lambda b: (b, 0, 0)),
                   pl.BlockSpec((1, 2, H_B, HEAD_B, HEAD_B), lambda b: (b, 0, 0, 0, 0)),
                   pl.BlockSpec((1, 2, H_B, HEAD_B), lambda b: (b, 0, 0, 0)),
                   pl.BlockSpec((1, 2, H_B, LANES), lambda b: (b, 0, 0, 0))],
        out_shape=[jax.ShapeDtypeStruct((bsz, seq_len, DB), F32),
                   jax.ShapeDtypeStruct((bsz, 2, H_B, HEAD_B, HEAD_B), F32),
                   jax.ShapeDtypeStruct((bsz, 2, H_B, HEAD_B), F32),
                   jax.ShapeDtypeStruct((bsz, 2, H_B, LANES), F32)],
        scratch_shapes=[pltpu.VMEM((seq_len, 2 * DB), F32), pltpu.VMEM((2, seq_len, DB), F32),
                        pltpu.VMEM((2, H_B, HEAD_B, HEAD_B), F32), pltpu.VMEM((2, H_B, HEAD_B), F32),
                        pltpu.VMEM((2, H_B, LANES), F32)],
        compiler_params=_params("arbitrary"),
        name="mlstm_scan",
    )(zm, g, gt, c0, n0, m0, *consts)


def _outproj_kernel(x_ref, yr_ref, ym_ref, mod_ref, g_ref, wo_ref, rw_ref, rb_ref,
                    x1_ref, hn_ref, idx_ref, gate_ref):
    mod = mod_ref[0]
    g1 = mod[:, 2 * D_MODEL:3 * D_MODEL]
    sh2, sc2 = mod[:, 3 * D_MODEL:4 * D_MODEL], mod[:, 4 * D_MODEL:5 * D_MODEL]
    mix = _dot_bf16(yr_ref[...], wo_ref[0:DA, :]) + _dot_bf16(ym_ref[...], wo_ref[DA:, :])
    x1 = x_ref[...] + g1 * mix
    x1_ref[...] = x1
    y = x1 * lax.rsqrt(jnp.mean(x1 * x1, axis=-1, keepdims=True) + NORM_EPS)
    hn = y * g_ref[...] * (1.0 + sc2) + sh2
    hn_ref[...] = hn
    logits = _dot_bf16(hn, rw_ref[...]) + rb_ref[...]
    lane = lax.broadcasted_iota(jnp.int32, logits.shape, 1)
    vals, idxs = [], []
    for _ in range(TOP_K):
        top = jnp.max(logits, axis=-1, keepdims=True)
        pick = jnp.min(jnp.where(logits == top, lane, N_EXPERTS), axis=-1, keepdims=True)
        vals.append(top)
        idxs.append(pick)
        logits = jnp.where(lane == pick, -jnp.inf, logits)
    exps = [jnp.exp(v - vals[0]) for v in vals]
    total = exps[0] + exps[1] + exps[2] + exps[3]
    k_lane = lax.broadcasted_iota(jnp.int32, (x1.shape[0], TOP_K), 1)
    gates = jnp.zeros((x1.shape[0], TOP_K), F32)
    picks = jnp.zeros((x1.shape[0], TOP_K), jnp.int32)
    for j in range(TOP_K):
        gates = jnp.where(k_lane == j, exps[j] / total, gates)
        picks = jnp.where(k_lane == j, idxs[j], picks)
    idx_ref[...] = picks
    gate_ref[...] = gates


def _outproj(x2, yr, ym, mod, norm_g, w_out, router_w, router_b):
    rows = x2.shape[0]
    tiles_per_mod = rows // mod.shape[0] // ROW_TILE
    row_spec = lambda w: pl.BlockSpec((ROW_TILE, w), lambda i: (i, 0))
    return pl.pallas_call(
        _outproj_kernel,
        grid=(rows // ROW_TILE,),
        in_specs=[row_spec(D_MODEL), row_spec(DA), row_spec(DB),
                  pl.BlockSpec((1, 1, 6 * D_MODEL), lambda i: (i // tiles_per_mod, 0, 0)),
                  _const_spec((1, D_MODEL)), _const_spec(w_out.shape), _const_spec(router_w.shape),
                  _const_spec((1, N_EXPERTS))],
        out_specs=[row_spec(D_MODEL), row_spec(D_MODEL), row_spec(TOP_K), row_spec(TOP_K)],
        out_shape=[jax.ShapeDtypeStruct((rows, D_MODEL), F32), jax.ShapeDtypeStruct((rows, D_MODEL), F32),
                   jax.ShapeDtypeStruct((rows, TOP_K), jnp.int32), jax.ShapeDtypeStruct((rows, TOP_K), F32)],
        compiler_params=_params("arbitrary"),
        name="outproj_router",
    )(x2, yr, ym, mod, norm_g.reshape(1, D_MODEL), w_out, router_w, router_b.reshape(1, N_EXPERTS))


def _rank_kernel(idx_ref, rank_ref, count_ref, run_s):
    i = pl.program_id(0)

    @pl.when(i == 0)
    def _():
        run_s[...] = jnp.zeros_like(run_s)

    idx = idx_ref[...]
    rows = idx.shape[0]
    lane = lax.broadcasted_iota(jnp.int32, (rows, N_EXPERTS), 1)
    hot = jnp.zeros((rows, N_EXPERTS), F32)
    for j in range(TOP_K):
        hot = hot + jnp.where(lane == idx[:, j:j + 1], 1.0, 0.0)
    before = _dot_bf16(_tri(rows, False, True).astype(F32), hot) + run_s[...]
    k_lane = lax.broadcasted_iota(jnp.int32, (rows, TOP_K), 1)
    rank = jnp.zeros((rows, TOP_K), F32)
    for j in range(TOP_K):
        rj = jnp.sum(jnp.where(lane == idx[:, j:j + 1], before, 0.0), axis=-1, keepdims=True)
        rank = jnp.where(k_lane == j, rj, rank)
    rank_ref[...] = rank.astype(jnp.int32)
    run_s[...] = run_s[...] + jnp.sum(hot, axis=0, keepdims=True)
    count_ref[...] = run_s[...].astype(jnp.int32)


def _expert_ranks(idx):
    rows = idx.shape[0]
    return pl.pallas_call(
        _rank_kernel,
        grid=(rows // ROW_TILE,),
        in_specs=[pl.BlockSpec((ROW_TILE, TOP_K), lambda i: (i, 0))],
        out_specs=[pl.BlockSpec((ROW_TILE, TOP_K), lambda i: (i, 0)), _const_spec((1, N_EXPERTS))],
        out_shape=[jax.ShapeDtypeStruct((rows, TOP_K), jnp.int32),
                   jax.ShapeDtypeStruct((1, N_EXPERTS), jnp.int32)],
        scratch_shapes=[pltpu.VMEM((1, N_EXPERTS), F32)],
        compiler_params=_params("arbitrary"),
        name="expert_ranks",
    )(idx)


def _dispatch_kernel(dest_ref, x_ref, init_ref, xs_ref, sem):
    del init_ref

    def issue(g, carry):
        for q in range(DMA_UNROLL):
            r = g * DMA_UNROLL + q
            for j in range(TOP_K):
                slot = dest_ref[r * TOP_K + j]
                pltpu.make_async_copy(x_ref.at[pl.ds(r, 1)], xs_ref.at[pl.ds(slot, 1)], sem).start()
        return carry

    lax.fori_loop(0, ROW_TILE // DMA_UNROLL, issue, 0)
    all_rows = xs_ref.at[pl.ds(0, ROW_TILE * TOP_K)]
    pltpu.make_async_copy(all_rows, all_rows, sem).wait()


def _dispatch(dest_flat, x2, n_slots):
    rows = x2.shape[0]
    init = jnp.zeros((n_slots, D_MODEL), x2.dtype)
    return pl.pallas_call(
        _dispatch_kernel,
        grid=(rows // ROW_TILE,),
        in_specs=[pl.BlockSpec((ROW_TILE * TOP_K,), lambda i: (i,), memory_space=pltpu.SMEM),
                  pl.BlockSpec((ROW_TILE, D_MODEL), lambda i: (i, 0)),
                  pl.BlockSpec(memory_space=pl.ANY)],
        out_specs=pl.BlockSpec(memory_space=pl.ANY),
        out_shape=jax.ShapeDtypeStruct(init.shape, init.dtype),
        scratch_shapes=[pltpu.SemaphoreType.DMA(())],
        input_output_aliases={2: 0},
        compiler_params=_params("arbitrary", disable_bounds_checks=True),
        name="moe_dispatch",
    )(dest_flat, x2, init)


def _expert_kernel(be_ref, nb_ref, xs_ref, w1_ref, b1_ref, w2_ref, b2_ref, y_ref, w1_s, w2_s):
    i = pl.program_id(0)
    changed = jnp.logical_or(i == 0, be_ref[i] != be_ref[jnp.maximum(i - 1, 0)])

    @pl.when(jnp.logical_and(changed, i < nb_ref[0]))
    def _():
        w1_s[...] = w1_ref[0].astype(BF16)
        w2_s[...] = w2_ref[0].astype(BF16)

    @pl.when(i < nb_ref[0])
    def _():
        xb = xs_ref[...].astype(BF16)
        hu = lax.dot_general(xb, w1_s[...], NN, preferred_element_type=F32) + b1_ref[0]
        glu = jnp.minimum(hu[:, :D_FF], SWIGLU_LIMIT)
        lin = jnp.clip(hu[:, D_FF:], -SWIGLU_LIMIT, SWIGLU_LIMIT)
        act = glu * _sigmoid(SWIGLU_ALPHA * glu) * (lin + 1.0)
        y_ref[...] = lax.dot_general(act.astype(BF16), w2_s[...], NN, preferred_element_type=F32) + b2_ref[0]

    @pl.when(i >= nb_ref[0])
    def _():
        y_ref[...] = jnp.zeros_like(y_ref)


def _experts(block_e, n_used, xs, w1, b1, w2, b2):
    n_blocks = xs.shape[0] // MOE_BLOCK
    grid_spec = pltpu.PrefetchScalarGridSpec(
        num_scalar_prefetch=2,
        grid=(n_blocks,),
        in_specs=[pl.BlockSpec((MOE_BLOCK, D_MODEL), lambda i, be, nb: (i, 0)),
                  pl.BlockSpec((1, D_MODEL, 2 * D_FF), lambda i, be, nb: (be[i], 0, 0)),
                  pl.BlockSpec((1, 1, 2 * D_FF), lambda i, be, nb: (be[i], 0, 0)),
                  pl.BlockSpec((1, D_FF, D_MODEL), lambda i, be, nb: (be[i], 0, 0)),
                  pl.BlockSpec((1, 1, D_MODEL), lambda i, be, nb: (be[i], 0, 0))],
        out_specs=pl.BlockSpec((MOE_BLOCK, D_MODEL), lambda i, be, nb: (i, 0)),
        scratch_shapes=[pltpu.VMEM((D_MODEL, 2 * D_FF), BF16), pltpu.VMEM((D_FF, D_MODEL), BF16)],
    )
    return pl.pallas_call(
        _expert_kernel,
        grid_spec=grid_spec,
        out_shape=jax.ShapeDtypeStruct(xs.shape, F32),
        compiler_params=_params("arbitrary"),
        name="moe_experts",
    )(block_e, n_used, xs, w1, b1.reshape(N_EXPERTS, 1, 2 * D_FF), w2, b2.reshape(N_EXPERTS, 1, D_MODEL))


def _combine_kernel(dest_ref, yb_ref, x1_ref, gate_ref, mod_ref, fg_ref, o_ref, buf, sem):
    def issue(g, carry):
        for q in range(DMA_UNROLL):
            r = g * DMA_UNROLL + q
            for j in range(TOP_K):
                slot = dest_ref[r * TOP_K + j]
                pltpu.make_async_copy(yb_ref.at[pl.ds(slot, 1)], buf.at[j, pl.ds(r, 1)], sem).start()
        return carry

    lax.fori_loop(0, ROW_TILE // DMA_UNROLL, issue, 0)
    pltpu.make_async_copy(buf, buf, sem).wait()

    g2 = mod_ref[0][:, 5 * D_MODEL:6 * D_MODEL]
    gates = gate_ref[...]
    moe = jnp.zeros((ROW_TILE, D_MODEL), F32)
    for j in range(TOP_K):
        moe = moe + buf[j] * gates[:, j:j + 1]
    x2 = x1_ref[...] + g2 * moe
    y = x2 * lax.rsqrt(jnp.mean(x2 * x2, axis=-1, keepdims=True) + NORM_EPS)
    o_ref[...] = y * fg_ref[...]


def _combine(dest_flat, yb, x1, gates, mod, final_g, row_offset):
    rows = x1.shape[0]
    tiles_per_mod = rows // mod.shape[0] // ROW_TILE
    tile_offset = row_offset // ROW_TILE
    row_spec = pl.BlockSpec((ROW_TILE, D_MODEL), lambda i: (i, 0))
    return pl.pallas_call(
        _combine_kernel,
        grid=(rows // ROW_TILE,),
        in_specs=[pl.BlockSpec((ROW_TILE * TOP_K,), lambda i: (i + tile_offset,), memory_space=pltpu.SMEM),
                  pl.BlockSpec(memory_space=pl.ANY),
                  row_spec,
                  pl.BlockSpec((ROW_TILE, TOP_K), lambda i: (i, 0)),
                  pl.BlockSpec((1, 1, 6 * D_MODEL), lambda i: (i // tiles_per_mod, 0, 0)),
                  _const_spec((1, D_MODEL))],
        out_specs=row_spec,
        out_shape=jax.ShapeDtypeStruct((rows, D_MODEL), F32),
        scratch_shapes=[pltpu.VMEM((TOP_K, ROW_TILE, D_MODEL), F32), pltpu.SemaphoreType.DMA(())],
        compiler_params=_params("arbitrary", disable_bounds_checks=True),
        name="moe_combine",
    )(dest_flat, yb, x1, gates, mod, final_g.reshape(1, D_MODEL))


def _block_diag_ones(width, block):
    i = jnp.arange(width) // block
    return (i[:, None] == i[None, :]).astype(F32)


def _mixers(x, mod, lp, s0, c0, n0, m0, grid):
    bsz, seq_len, _ = x.shape
    nc = seq_len // CHUNK
    x2 = x.reshape(bsz * seq_len, D_MODEL)
    zr, zm, zg, zgt = _inproj(x2, mod, lp['norm1_g'], lp['w_r'], lp['w_m'], lp['w_g'], lp['w_gt'])
    gt = zgt.reshape(N_GATES, bsz, nc, CHUNK).transpose(1, 2, 0, 3)
    yr, s_new = _rwkv(zr.reshape(bsz, seq_len, RWKV_COLS), s0, lp, grid)
    ym, c_new, n_new, m_new = _mlstm(zm.reshape(bsz, seq_len, MLSTM_MAIN), zg.reshape(bsz, seq_len, N_GATES), gt,
                                     c0, n0, m0.reshape(bsz, 2, H_B, 1), lp)
    x1, hn, idx, gates = _outproj(x2, yr.reshape(-1, DA), ym.reshape(-1, DB), mod, lp['norm2_g'], lp['w_out'],
                                  lp['router_w'], lp['router_b'])
    return x1, hn, idx, gates, (s_new, c_new, n_new, m_new[..., 0])


def kernel(x_prompt, x_sample, state_rwkv, state_mlstm_C, state_mlstm_n, state_mlstm_m, c, c_ctx, ada_w, ada_b, norm1_g, norm2_g, w_in, w_out, rwkv_mu, rwkv_w0, rwkv_w2, rwkv_a0, rwkv_a2, rwkv_g2, rwkv_kk, rwkv_ka, rwkv_rk, rwkv_gn_w, rwkv_gn_b, mlstm_conv, mlstm_bi, mlstm_bf, mlstm_norm_g, router_w, router_b, moe_w1, moe_b1, moe_w2, moe_b2, final_g):
    bp, lp_len, _ = x_prompt.shape
    bs, ls_len, _ = x_sample.shape
    w = w_in[0]
    lp = {
        'norm1_g': norm1_g[0], 'norm2_g': norm2_g[0],
        'w_r': w[:, :RWKV_COLS].astype(BF16),
        'w_m': w[:, RWKV_COLS:RWKV_COLS + MLSTM_MAIN].astype(BF16),
        'w_g': w[:, RWKV_COLS + MLSTM_MAIN:],
        'w_gt': w[:, RWKV_COLS + MLSTM_MAIN:].T,
        'w_out': w_out[0].astype(BF16),
        'mu': rwkv_mu[0].reshape(1, RWKV_COLS), 'kkw': rwkv_kk[0].reshape(1, DA), 'ka': rwkv_ka[0].reshape(1, DA),
        'rk': rwkv_rk[0].reshape(1, DA), 'gnw': rwkv_gn_w[0].reshape(1, DA), 'gnb': rwkv_gn_b[0].reshape(1, DA),
        'w0': rwkv_w0[0], 'w2': rwkv_w2[0], 'a0': rwkv_a0[0], 'a2': rwkv_a2[0], 'g2': rwkv_g2[0],
        'ones_bd': _block_diag_ones(PACK_W, HEAD_A).astype(BF16),
        'conv': mlstm_conv[0], 'bi': mlstm_bi[0].reshape(1, 2 * H_B), 'bit': mlstm_bi[0].reshape(2 * H_B, 1),
        'bf': mlstm_bf[0].reshape(1, 2 * H_B), 'bft': mlstm_bf[0].reshape(2 * H_B, 1),
        'ng': mlstm_norm_g[0].reshape(1, DB),
        'router_w': router_w[0], 'router_b': router_b[0],
    }
    cc = jnp.concatenate([c_ctx[None, :], c, jnp.zeros((2 * SUBLANES - 1 - bs, D_MODEL), F32)], axis=0)
    mod = _ada_mod(cc, ada_w[0], ada_b[0])
    mod_p = mod[0:1].reshape(1, 1, 6 * D_MODEL)
    mod_s = mod[1:1 + bs].reshape(bs, 1, 6 * D_MODEL)

    zeros = lambda *shape: jnp.zeros(shape, F32)
    x1p, hnp, idxp, gatesp, st = _mixers(x_prompt, mod_p, lp, zeros(bp, 2, H_A, HEAD_A, HEAD_A),
                                         zeros(bp, 2, H_B, HEAD_B, HEAD_B), zeros(bp, 2, H_B, HEAD_B),
                                         zeros(bp, 2, H_B), False)
    x1s, hns, idxs, gatess, _ = _mixers(x_sample, mod_s, lp, state_rwkv[:, 0], state_mlstm_C[:, 0],
                                        state_mlstm_n[:, 0], state_mlstm_m[:, 0], True)

    n_p = bp * lp_len
    idx = jnp.concatenate([idxp, idxs], axis=0)
    hn = jnp.concatenate([hnp, hns], axis=0)
    n_tok = idx.shape[0]
    rank, counts = _expert_ranks(idx)
    counts = counts[0]
    padded = (counts + MOE_BLOCK - 1) // MOE_BLOCK * MOE_BLOCK
    pad_end = jnp.cumsum(padded)
    pad_start = pad_end - padded
    n_blocks = n_tok * TOP_K // MOE_BLOCK + N_EXPERTS
    block_start = jnp.arange(n_blocks, dtype=jnp.int32) * MOE_BLOCK
    block_e = jnp.minimum(jnp.sum(pad_end[None, :] <= block_start[:, None], axis=1), N_EXPERTS - 1).astype(jnp.int32)
    n_used = (pad_end[-1:] // MOE_BLOCK).astype(jnp.int32)
    dest = (pad_start[idx] + rank).astype(jnp.int32).reshape(-1)
    xs = _dispatch(dest, hn, n_blocks * MOE_BLOCK)
    yb = _experts(block_e, n_used, xs, moe_w1[0], moe_b1[0], moe_w2[0], moe_b2[0])
    y_prompt = _combine(dest, yb, x1p, gatesp, mod_p, final_g, 0).reshape(x_prompt.shape)
    y_sample = _combine(dest, yb, x1s, gatess, mod_s, final_g, n_p).reshape(x_sample.shape)

    s_new, c_new, n_new, m_new = st
    return (y_prompt, y_sample, s_new[:, None], c_new[:, None], n_new[:, None], m_new[:, None])
```

```python
import functools

import jax
import jax.numpy as jnp
from jax import lax
from jax.experimental import pallas as pl
from jax.experimental.pallas import tpu as pltpu

F32 = jnp.float32
BF16 = jnp.bfloat16
HIGHEST = lax.Precision.HIGHEST

D_MODEL = 1024
DA = 512
HEAD_A = 64
H_A = DA // HEAD_A
DB = 512
H_B = 4
HEAD_B = DB // H_B
DECAY_LORA = 64
AAA_LORA = 64
GATE_LORA = 128
RWKV_COLS = 3 * DA + DECAY_LORA + AAA_LORA + GATE_LORA
MLSTM_MAIN = 4 * DB
N_GATES = 4 * H_B
GRID_W = 64
CHUNK = 64
N_EXPERTS = 32
TOP_K = 4
D_FF = D_MODEL
SWIGLU_LIMIT = 7.0
SWIGLU_ALPHA = 1.702
MOE_BLOCK = 256
NORM_EPS = 1e-6
GN_EPS = 64e-5
ROW_TILE = 256
DMA_UNROLL = 4
SUBLANES = 8
LANES = 128
LANE_ROWS = D_MODEL // LANES
VMEM_LIMIT = 56 * 1024 * 1024
PACK = 4
PACK_W = PACK * HEAD_A
N_PACKS = H_A // PACK

NN = (((1,), (0,)), ((), ()))
NT = (((1,), (1,)), ((), ()))
TN = (((0,), (0,)), ((), ()))


def _mm(a, b, precision=HIGHEST):
    return lax.dot_general(a, b, NN, precision=precision, preferred_element_type=F32)


def _mm_nt(a, b, precision=HIGHEST):
    return lax.dot_general(a, b, NT, precision=precision, preferred_element_type=F32)


def _mm_tn(a, b, precision=HIGHEST):
    return lax.dot_general(a, b, TN, precision=precision, preferred_element_type=F32)


def _dot_bf16(a, b, dims=NN):
    return lax.dot_general(a.astype(BF16), b.astype(BF16), dims, preferred_element_type=F32)


def _sigmoid(x):
    return 1.0 / (1.0 + jnp.exp(-x))


def _log_sigmoid(x):
    return jnp.minimum(x, 0.0) - jnp.log(1.0 + jnp.exp(-jnp.abs(x)))


def _params(*sem, **kw):
    return pltpu.CompilerParams(dimension_semantics=sem, vmem_limit_bytes=VMEM_LIMIT, **kw)


def _split_bf16(x):
    hi = x.astype(BF16)
    return hi, (x - hi.astype(F32)).astype(BF16)


def _const_spec(shape):
    nd = len(shape)
    return pl.BlockSpec(shape, lambda *_: (0,) * nd)


def _tri(n, reverse, strict):
    t = lax.broadcasted_iota(jnp.int32, (n, n), 0)
    s = lax.broadcasted_iota(jnp.int32, (n, n), 1)
    if reverse:
        return (s > t) if strict else (s >= t)
    return (s < t) if strict else (s <= t)


def _seq_neighbours(ref, b, c, nc, cols):
    r0 = pl.multiple_of(c * CHUNK, CHUNK)
    zc = ref[b, pl.ds(r0, CHUNK), cols]
    row = lax.broadcasted_iota(jnp.int32, zc.shape, 0)
    p0 = pl.multiple_of(jnp.maximum(r0 - SUBLANES, 0), SUBLANES)
    n0 = pl.multiple_of(jnp.minimum(r0 + CHUNK, (nc - 1) * CHUNK), SUBLANES)
    before = ref[b, pl.ds(p0, SUBLANES), cols][SUBLANES - 1:SUBLANES]
    after = ref[b, pl.ds(n0, SUBLANES), cols][0:1]
    before = jnp.where(c > 0, before, 0.0)
    after = jnp.where(c < nc - 1, after, 0.0)
    prev = jnp.where(row == 0, before, pltpu.roll(zc, 1, 0))
    nxt = jnp.where(row == CHUNK - 1, after, pltpu.roll(zc, CHUNK - 1, 0))
    return zc, prev, nxt


def _ada_kernel(c_ref, w_ref, b_ref, o_ref):
    cc = c_ref[...]
    o_ref[...] = _dot_bf16(cc * _sigmoid(cc), w_ref[...]) + b_ref[...]


def _ada_mod(cc, ada_w, ada_b):
    rows = cc.shape[0]
    ncol = ada_w.shape[1]
    tn = 1536
    return pl.pallas_call(
        _ada_kernel,
        grid=(ncol // tn,),
        in_specs=[_const_spec((rows, D_MODEL)),
                  pl.BlockSpec((D_MODEL, tn), lambda j: (0, j)),
                  pl.BlockSpec((1, tn), lambda j: (0, j))],
        out_specs=pl.BlockSpec((rows, tn), lambda j: (0, j)),
        out_shape=jax.ShapeDtypeStruct((rows, ncol), F32),
        compiler_params=_params("arbitrary"),
        name="ada_mod",
    )(cc, ada_w, ada_b.reshape(1, ncol))


def _inproj_kernel(x_ref, mod_ref, g_ref, wr_ref, wm_ref, wg_ref, wgt_ref, zr_ref, zm_ref, zg_ref, zgt_ref):
    x = x_ref[...]
    mod = mod_ref[0]
    sh, sc = mod[:, 0:D_MODEL], mod[:, D_MODEL:2 * D_MODEL]
    y = x * lax.rsqrt(jnp.mean(x * x, axis=-1, keepdims=True) + NORM_EPS)
    hn = y * g_ref[...] * (1.0 + sc) + sh
    hb = hn.astype(BF16)
    zr_ref[...] = lax.dot_general(hb, wr_ref[...], NN, preferred_element_type=F32)
    zm_ref[...] = lax.dot_general(hb, wm_ref[...], NN, preferred_element_type=F32)
    zg_ref[...] = _dot_bf16(hb, wg_ref[...])
    zgt_ref[...] = _dot_bf16(wgt_ref[...], hb, NT)


def _inproj(x2, mod, norm_g, w_r, w_m, w_g, w_gt):
    rows = x2.shape[0]
    tiles_per_mod = rows // mod.shape[0] // ROW_TILE
    return pl.pallas_call(
        _inproj_kernel,
        grid=(rows // ROW_TILE,),
        in_specs=[pl.BlockSpec((ROW_TILE, D_MODEL), lambda i: (i, 0)),
                  pl.BlockSpec((1, 1, 6 * D_MODEL), lambda i: (i // tiles_per_mod, 0, 0)),
                  _const_spec((1, D_MODEL)),
                  _const_spec(w_r.shape), _const_spec(w_m.shape), _const_spec(w_g.shape),
                  _const_spec(w_gt.shape)],
        out_specs=[pl.BlockSpec((ROW_TILE, RWKV_COLS), lambda i: (i, 0)),
                   pl.BlockSpec((ROW_TILE, MLSTM_MAIN), lambda i: (i, 0)),
                   pl.BlockSpec((ROW_TILE, N_GATES), lambda i: (i, 0)),
                   pl.BlockSpec((N_GATES, ROW_TILE), lambda i: (0, i))],
        out_shape=[jax.ShapeDtypeStruct((rows, RWKV_COLS), F32),
                   jax.ShapeDtypeStruct((rows, MLSTM_MAIN), F32),
                   jax.ShapeDtypeStruct((rows, N_GATES), F32),
                   jax.ShapeDtypeStruct((N_GATES, rows), F32)],
        compiler_params=_params("arbitrary"),
        name="inproj",
    )(x2, mod, norm_g.reshape(1, D_MODEL), w_r, w_m, w_g, w_gt)


def _group_sum(x, ones_bd):
    hi = x.astype(BF16)
    lo = (x - hi.astype(F32)).astype(BF16)
    return (lax.dot_general(hi, ones_bd, NN, preferred_element_type=F32)
            + lax.dot_general(lo, ones_bd, NN, preferred_element_type=F32))


def _block_diag(x, ones_bd):
    return jnp.concatenate([x.astype(BF16)] * PACK, axis=0) * ones_bd


def _rwkv_kernel(zr_ref, s0_ref, mu_ref, kkw_ref, ka_ref, rk_ref, gnw_ref, gnb_ref, w0_ref, w2_ref,
                 a0_ref, a2_ref, g2_ref, ones_ref, y_ref, sout_ref,
                 r_s, k_s, v_s, kk_s, gate_s, lx_s, st_s, *, seq_len, grid):
    nc = seq_len // CHUNK
    ones_bd = ones_ref[...]

    def mix_chunk(c, carry):
        r0 = pl.multiple_of(c * CHUNK, CHUNK)
        if grid:
            zc = zr_ref[0, pl.ds(r0, CHUNK), :]
            row = lax.broadcasted_iota(jnp.int32, zc.shape, 0)
            up0 = pl.multiple_of(jnp.maximum(c - 1, 0) * CHUNK, CHUNK)
            dn0 = pl.multiple_of(jnp.minimum(c + 1, nc - 1) * CHUNK, CHUNK)
            up = jnp.where(c > 0, zr_ref[0, pl.ds(up0, CHUNK), :], 0.0)
            down = jnp.where(c < nc - 1, zr_ref[0, pl.ds(dn0, CHUNK), :], 0.0)
            left = jnp.where(row == 0, 0.0, pltpu.roll(zc, 1, 0))
            right = jnp.where(row == CHUNK - 1, 0.0, pltpu.roll(zc, CHUNK - 1, 0))
            local = 0.25 * (up + down + left + right)
        else:
            zc, prev, nxt = _seq_neighbours(zr_ref, 0, c, nc, slice(None))
            local = 0.5 * (prev + nxt)
        z = zc + (local - zc) * mu_ref[...]
        r, k, v = z[:, 0:DA], z[:, DA:2 * DA], z[:, 2 * DA:3 * DA]
        lx = z[:, 3 * DA:3 * DA + DECAY_LORA + AAA_LORA]
        xg = z[:, 3 * DA + DECAY_LORA + AAA_LORA:]
        kk = k * kkw_ref[...]
        sq = kk * kk
        ss = jnp.concatenate([_group_sum(sq[:, p * PACK_W:(p + 1) * PACK_W], ones_bd) for p in range(N_PACKS)],
                             axis=1)
        kk = kk / jnp.maximum(jnp.sqrt(ss), 1e-12)
        lane = lax.broadcasted_iota(jnp.int32, lx.shape, 1)
        r_s[pl.ds(r0, CHUNK), :] = r
        k_s[pl.ds(r0, CHUNK), :] = k
        v_s[pl.ds(r0, CHUNK), :] = v
        kk_s[pl.ds(r0, CHUNK), :] = kk
        gate_s[pl.ds(r0, CHUNK), :] = _dot_bf16(_sigmoid(xg), g2_ref[...])
        lx_s[pl.ds(r0, CHUNK), :] = jnp.where(lane < DECAY_LORA, jnp.tanh(lx), lx)
        return carry

    lax.fori_loop(0, nc, mix_chunk, 0)

    rowblk = lax.broadcasted_iota(jnp.int32, (PACK_W, PACK_W), 0) // HEAD_A
    colblk = lax.broadcasted_iota(jnp.int32, (PACK_W, PACK_W), 1) // HEAD_A
    for d in range(2):
        for p in range(N_PACKS):
            rows_ = jnp.concatenate([s0_ref[0, d, p * PACK + h] for h in range(PACK)], axis=0)
            st_s[d, p] = jnp.where(rowblk == colblk, jnp.concatenate([rows_] * PACK, axis=1), 0.0)

    t_idx = lax.broadcasted_iota(jnp.int32, (CHUNK, PACK_W), 0)
    s_idx = lax.broadcasted_iota(jnp.int32, (CHUNK, PACK_W), 1) % CHUNK

    def scan_step(i, carry):
        first_touch = i < nc // 2
        dot = functools.partial(lax.dot_general, preferred_element_type=F32)
        chains, row_sel = [], []
        for d in range(2):
            reverse = d == 1
            c = nc - 1 - i if reverse else i
            rows = pl.ds(pl.multiple_of(c * CHUNK, CHUNK), CHUNK)
            row_sel.append(rows)
            strict = (s_idx > t_idx) if reverse else (s_idx < t_idx)
            incl = (s_idx >= t_idx) if reverse else (s_idx <= t_idx)
            last = 0 if reverse else CHUNK - 1
            r, k, v, kk = r_s[rows, :], k_s[rows, :], v_s[rows, :], kk_s[rows, :]
            lx = lx_s[rows, :]
            wl = w0_ref[d:d + 1, :] + _dot_bf16(lx[:, 0:DECAY_LORA], w2_ref[d])
            logw = -jnp.exp(_log_sigmoid(wl) - 0.5)
            a = _sigmoid(a0_ref[d:d + 1, :] + _dot_bf16(lx[:, DECAY_LORA:], a2_ref[d]))
            kd = k * (1.0 + (a - 1.0) * ka_ref[...])
            kb = kk * a
            tri = _tri(CHUNK, reverse, False).astype(BF16)
            w1, w_rest = _split_bf16(logw)
            w2, w3 = _split_bf16(w_rest)
            cum = dot(tri, w1, NN) + dot(tri, w2, NN) + dot(tri, w3, NN)
            clast = cum[last:last + 1]
            rt = r * jnp.exp(cum)
            at = -kk * jnp.exp(cum - logw)
            einv = jnp.exp(-cum)
            kt, bt = kd * einv, kb * einv
            edec = jnp.exp(clast - cum)
            kp, bp = kd * edec, kb * edec
            ptot = jnp.exp(clast)
            rkd = r * kd * rk_ref[...]
            for p in range(N_PACKS):
                cs = slice(p * PACK_W, (p + 1) * PACK_W)
                chains.append(dict(
                    d=d, p=p, cs=cs, strict=strict, incl=incl, v=v[:, cs], rkd=rkd[:, cs], ptot=ptot[:, cs],
                    ar=jnp.concatenate([at[:, cs], rt[:, cs]], axis=0).astype(BF16),
                    kt_bd=_block_diag(kt[:, cs], ones_bd), bt_bd=_block_diag(bt[:, cs], ones_bd),
                    v_bd=_block_diag(v[:, cs], ones_bd),
                    kbp=jnp.concatenate([kp[:, cs], bp[:, cs]], axis=0)))

        for ch in chains:
            ch['s_prev'] = st_s[ch['d'], ch['p']]
            ch['a_k'] = dot(ch['ar'], ch['kt_bd'], NT)
            ch['a_b'] = dot(ch['ar'], ch['bt_bd'], NT)
            ch['ars'] = dot(ch['ar'], ch['s_prev'].astype(BF16), NT)
        for ch in chains:
            a_ak = jnp.where(ch['strict'], ch['a_k'][:CHUNK], 0.0)
            a_rk = jnp.where(ch['incl'], ch['a_k'][CHUNK:], 0.0)
            ch['a_rb'] = jnp.where(ch['incl'], ch['a_b'][CHUNK:], 0.0).astype(BF16)
            n = jnp.where(ch['strict'], ch['a_b'][:CHUNK], 0.0)
            ch['n_hi'], n_lo = _split_bf16(n)
            ch['n_both'] = jnp.concatenate([ch['n_hi'], n_lo], axis=0)
            ch['pw'] = ch['n_hi']
            ch['m'] = jnp.where(s_idx == t_idx, 1.0, 0.0) + n
            akv = dot(jnp.concatenate([a_ak, a_rk], axis=0).astype(BF16), ch['v_bd'], NN)
            ch['rhs'] = ch['ars'][:CHUNK] + akv[:CHUNK]
            ch['y'] = ch['ars'][CHUNK:] + akv[CHUNK:]
        for ch in chains:
            ch['pw'] = dot(ch['pw'], _block_diag(ch['pw'], ones_bd), NN).astype(BF16)
        for j in range(1, 5):
            for ch in chains:
                pw_bd = _block_diag(ch['pw'], ones_bd)
                if j < 4:
                    both = dot(jnp.concatenate([ch['m'].astype(BF16), ch['pw']], axis=0), pw_bd, NN)
                    ch['m'] = ch['m'] + both[:CHUNK]
                    ch['pw'] = both[CHUNK:].astype(BF16)
                else:
                    ch['m'] = ch['m'] + dot(ch['m'].astype(BF16), pw_bd, NN)
        for ch in chains:
            ch['mb'] = ch['m'].astype(BF16)
            ch['u'] = dot(ch['mb'], _block_diag(ch['rhs'], ones_bd), NN)
        for ch in chains:
            u_hi, u_lo = _split_bf16(ch['u'])
            both = dot(ch['n_both'], _block_diag(u_hi, ones_bd), NN)
            nu = both[:CHUNK] + both[CHUNK:] + dot(ch['n_hi'], _block_diag(u_lo, ones_bd), NN)
            ch['resid'] = ch['rhs'] - ch['u'] + nu
        for ch in chains:
            ch['u'] = ch['u'] + dot(ch['mb'], _block_diag(ch['resid'], ones_bd), NN)
        for ch in chains:
            u = ch['u']
            ch['y'] = ch['y'] + dot(ch['a_rb'], _block_diag(u, ones_bd), NN)
            vu_hi, vu_lo = _split_bf16(jnp.concatenate([ch['v'], u], axis=0))
            kbp_hi, kbp_lo = _split_bf16(ch['kbp'])
            grown = dot(vu_hi, kbp_hi, TN) + dot(vu_hi, kbp_lo, TN) + dot(vu_lo, kbp_hi, TN)
            st_s[ch['d'], ch['p']] = ch['s_prev'] * ch['ptot'] + grown * ones_bd.astype(F32)
        n_ch = len(chains)
        sums = dot(jnp.concatenate([ch['y'] for ch in chains] + [ch['rkd'] for ch in chains],
                                   axis=0).astype(BF16), ones_bd, NN)
        ycs = [ch['y'] - sums[q * CHUNK:(q + 1) * CHUNK] * (1.0 / HEAD_A) for q, ch in enumerate(chains)]
        var = dot(jnp.concatenate([yc * yc for yc in ycs], axis=0).astype(BF16), ones_bd, NN) * (1.0 / HEAD_A)
        outs = [[], []]
        for q, ch in enumerate(chains):
            cs = ch['cs']
            out = ycs[q] * lax.rsqrt(var[q * CHUNK:(q + 1) * CHUNK] + GN_EPS) * gnw_ref[:, cs] + gnb_ref[:, cs]
            outs[ch['d']].append(out + sums[(n_ch + q) * CHUNK:(n_ch + q + 1) * CHUNK] * ch['v'])
        for d in range(2):
            rows = row_sel[d]
            out = jnp.concatenate(outs[d], axis=1)

            @pl.when(first_touch)
            def _(rows=rows, out=out):
                y_ref[0, rows, :] = out

            @pl.when(jnp.logical_not(first_touch))
            def _(rows=rows, out=out):
                y_ref[0, rows, :] = (y_ref[0, rows, :] + out) * gate_s[rows, :]
        return carry

    lax.fori_loop(0, nc, scan_step, 0)

    for d in range(2):
        for p in range(N_PACKS):
            s_fin = st_s[d, p]
            for h in range(PACK):
                sout_ref[0, d, p * PACK + h] = s_fin[h * HEAD_A:(h + 1) * HEAD_A, h * HEAD_A:(h + 1) * HEAD_A]


def _rwkv(zr, s0, p, grid):
    bsz, seq_len, _ = zr.shape
    assert (seq_len // CHUNK) % 2 == 0
    kern = functools.partial(_rwkv_kernel, seq_len=seq_len, grid=grid)
    consts = [p['mu'], p['kkw'], p['ka'], p['rk'], p['gnw'], p['gnb'], p['w0'], p['w2'], p['a0'], p['a2'],
              p['g2'], p['ones_bd']]
    return pl.pallas_call(
        kern,
        grid=(bsz,),
        in_specs=[pl.BlockSpec((1, seq_len, RWKV_COLS), lambda b: (b, 0, 0)),
                  pl.BlockSpec((1, 2, H_A, HEAD_A, HEAD_A), lambda b: (b, 0, 0, 0, 0))]
                 + [_const_spec(a.shape) for a in consts],
        out_specs=[pl.BlockSpec((1, seq_len, DA), lambda b: (b, 0, 0)),
                   pl.BlockSpec((1, 2, H_A, HEAD_A, HEAD_A), lambda b: (b, 0, 0, 0, 0))],
        out_shape=[jax.ShapeDtypeStruct((bsz, seq_len, DA), F32),
                   jax.ShapeDtypeStruct((bsz, 2, H_A, HEAD_A, HEAD_A), F32)],
        scratch_shapes=[pltpu.VMEM((seq_len, DA), F32)] * 5
                       + [pltpu.VMEM((seq_len, DECAY_LORA + AAA_LORA), F32),
                          pltpu.VMEM((2, N_PACKS, PACK_W, PACK_W), F32)],
        compiler_params=_params("arbitrary"),
        name="rwkv_scan",
    )(zr, s0, *consts)


def _mlstm_kernel(zm_ref, g_ref, gt_ref, c0_ref, n0_ref, m0_ref, conv_ref, bi_ref, bit_ref, bf_ref, bft_ref,
                  ng_ref, y_ref, cout_ref, nout_ref, mout_ref,
                  qk_s, h_s, c_s, n_s, m_s, *, seq_len):
    nc = seq_len // CHUNK

    def conv_chunk(c, carry):
        zc, prev, nxt = _seq_neighbours(zm_ref, 0, c, nc, slice(0, 2 * DB))
        u = conv_ref[0:1, :] * prev + conv_ref[1:2, :] * zc + conv_ref[2:3, :] * nxt
        qk = u * _sigmoid(u)
        lane = lax.broadcasted_iota(jnp.int32, qk.shape, 1)
        qk_s[pl.ds(pl.multiple_of(c * CHUNK, CHUNK), CHUNK), :] = jnp.where(lane >= DB, qk * (HEAD_B ** -0.5), qk)
        return carry

    lax.fori_loop(0, nc, conv_chunk, 0)

    for d in range(2):
        c_s[d] = c0_ref[0, d]
        n_s[d] = n0_ref[0, d]
        m_s[d] = jnp.broadcast_to(m0_ref[0, d], (H_B, LANES))

    def scan_step(i, carry):
        chains = []
        for d in range(2):
            reverse = d == 1
            incl = _tri(CHUNK, reverse, False)
            tri_col = incl.astype(F32)
            tri_row = _tri(CHUNK, not reverse, False).astype(F32)
            last = 0 if reverse else CHUNK - 1
            gsl = slice(d * H_B, (d + 1) * H_B)
            fsl = slice(2 * H_B + d * H_B, 2 * H_B + (d + 1) * H_B)
            c = nc - 1 - i if reverse else i
            rows = pl.ds(pl.multiple_of(c * CHUNK, CHUNK), CHUNK)
            gcol = g_ref[0, rows, :]
            grow = gt_ref[0, c]
            i_col = gcol[:, gsl] + bi_ref[:, gsl]
            f_col = _log_sigmoid(gcol[:, fsl] + bf_ref[:, gsl])
            i_row = grow[gsl, :] + bit_ref[gsl, :]
            f_row = _log_sigmoid(grow[fsl, :] + bft_ref[gsl, :])
            b_col = _mm(tri_col, f_col)
            b_row = _mm(f_row, tri_row)
            b_last = b_col[last:last + 1]
            for h in range(H_B):
                q = qk_s[rows, h * HEAD_B:(h + 1) * HEAD_B]
                chains.append(dict(
                    d=d, h=h, rows=rows, incl=incl, q=q, qb=q.astype(BF16),
                    k=qk_s[rows, DB + h * HEAD_B:DB + (h + 1) * HEAD_B],
                    v=zm_ref[0, rows, 2 * DB + h * HEAD_B:2 * DB + (h + 1) * HEAD_B],
                    bc=b_col[:, h:h + 1], br=b_row[h:h + 1, :], ic=i_col[:, h:h + 1], ir=i_row[h:h + 1, :],
                    bl=b_last[:, h:h + 1]))
        for ch in chains:
            d, h = ch['d'], ch['h']
            ch['qk'] = _dot_bf16(ch['qb'], ch['k'], NT)
            ch['c_prev'] = c_s[d, h]
            ch['qc'] = _dot_bf16(ch['qb'], ch['c_prev'])
        for ch in chains:
            d, h = ch['d'], ch['h']
            m_prev = m_s[d, h:h + 1, 0:1]
            dmat = jnp.where(ch['incl'], ch['bc'] - ch['br'] + ch['ir'], -jnp.inf)
            inter = ch['bc'] + m_prev
            m_t = jnp.maximum(inter, jnp.max(dmat, axis=-1, keepdims=True))
            s = ch['qk'] * jnp.exp(dmat - m_t)
            carry_w = jnp.exp(inter - m_t)
            n_prev = n_s[d, h:h + 1, :]
            num = _dot_bf16(s, ch['v']) + carry_w * ch['qc']
            den = (jnp.sum(s, axis=-1, keepdims=True)
                   + carry_w * jnp.sum(ch['q'] * n_prev, axis=-1, keepdims=True))
            h_s[d, ch['rows'], h * HEAD_B:(h + 1) * HEAD_B] = num / jnp.maximum(jnp.abs(den), jnp.exp(-m_t))
            g = ch['bl'] - ch['bc'] + ch['ic']
            m_new = jnp.maximum(ch['bl'] + m_prev, jnp.max(g, axis=0, keepdims=True))
            wk = jnp.exp(g - m_new)
            decay = jnp.exp(ch['bl'] + m_prev - m_new)
            c_s[d, h] = decay * ch['c_prev'] + _dot_bf16(ch['k'], wk * ch['v'], TN)
            n_s[d, h:h + 1, :] = decay * n_prev + jnp.sum(wk * ch['k'], axis=0, keepdims=True)
            m_s[d, h:h + 1, :] = jnp.broadcast_to(m_new, (1, LANES))
        return carry

    lax.fori_loop(0, nc, scan_step, 0)
    cout_ref[0] = c_s[...]
    nout_ref[0] = n_s[...]
    mout_ref[0] = m_s[...]

    def norm_chunk(c, carry):
        rows = pl.ds(pl.multiple_of(c * CHUNK, CHUNK), CHUNK)
        og = zm_ref[0, rows, 3 * DB:4 * DB]
        hm = h_s[0, rows, :] + h_s[1, rows, :]
        parts = []
        for h in range(H_B):
            hh = hm[:, h * HEAD_B:(h + 1) * HEAD_B]
            parts.append(hh * lax.rsqrt(jnp.mean(hh * hh, axis=-1, keepdims=True) + NORM_EPS))
        y_ref[0, rows, :] = jnp.concatenate(parts, axis=1) * ng_ref[...] * _sigmoid(og)
        return carry

    lax.fori_loop(0, nc, norm_chunk, 0)


def _mlstm(zm, g, gt, c0, n0, m0, p):
    bsz, seq_len, _ = zm.shape
    nc = seq_len // CHUNK
    kern = functools.partial(_mlstm_kernel, seq_len=seq_len)
    consts = [p['conv'], p['bi'], p['bit'], p['bf'], p['bft'], p['ng']]
    return pl.pallas_call(
        kern,
        grid=(bsz,),
        in_specs=[pl.BlockSpec((1, seq_len, MLSTM_MAIN), lambda b: (b, 0, 0)),
                  pl.BlockSpec((1, seq_len, N_GATES), lambda b: (b, 0, 0)),
                  pl.BlockSpec((1, nc, N_GATES, CHUNK), lambda b: (b, 0, 0, 0)),
                  pl.BlockSpec((1, 2, H_B, HEAD_B, HEAD_B), lambda b: (b, 0, 0, 0, 0)),
                  pl.BlockSpec((1, 2, H_B, HEAD_B), lambda b: (b, 0, 0, 0)),
                  pl.BlockSpec((1, 2, H_B, 1), lambda b: (b, 0, 0, 0))]
                 + [_const_spec(a.shape) for a in consts],
        out_specs=[pl.BlockSpec((1, seq_len, DB), lambda b: (b, 0, 0)),
                   pl.BlockSpec((1, 2, H_B, HEAD_B, HEAD_B), lambda b: (b, 0, 0, 0, 0)),
                   pl.BlockSpec((1, 2, H_B, HEAD_B), lambda b: (b, 0, 0, 0)),
                   pl.BlockSpec((1, 2, H_B, LANES), lambda b: (b, 0, 0, 0))],
        out_shape=[jax.ShapeDtypeStruct((bsz, seq_len, DB), F32),
                   jax.ShapeDtypeStruct((bsz, 2, H_B, HEAD_B, HEAD_B), F32),
                   jax.ShapeDtypeStruct((bsz, 2, H_B, HEAD_B), F32),
                   jax.ShapeDtypeStruct((bsz, 2, H_B, LANES), F32)],
        scratch_shapes=[pltpu.VMEM((seq_len, 2 * DB), F32), pltpu.VMEM((2, seq_len, DB), F32),
                        pltpu.VMEM((2, H_B, HEAD_B, HEAD_B), F32), pltpu.VMEM((2, H_B, HEAD_B), F32),
                        pltpu.VMEM((2, H_B, LANES), F32)],
        compiler_params=_params("arbitrary"),
        name="mlstm_scan",
    )(zm, g, gt, c0, n0, m0, *consts)


def _outproj_kernel(x_ref, yr_ref, ym_ref, mod_ref, g_ref, wo_ref, rw_ref, rb_ref,
                    x1_ref, hn_ref, idx_ref, gate_ref):
    mod = mod_ref[0]
    g1 = mod[:, 2 * D_MODEL:3 * D_MODEL]
    sh2, sc2 = mod[:, 3 * D_MODEL:4 * D_MODEL], mod[:, 4 * D_MODEL:5 * D_MODEL]
    mix = _dot_bf16(yr_ref[...], wo_ref[0:DA, :]) + _dot_bf16(ym_ref[...], wo_ref[DA:, :])
    x1 = x_ref[...] + g1 * mix
    x1_ref[...] = x1
    y = x1 * lax.rsqrt(jnp.mean(x1 * x1, axis=-1, keepdims=True) + NORM_EPS)
    hn = y * g_ref[...] * (1.0 + sc2) + sh2
    hn_ref[...] = hn
    logits = _dot_bf16(hn, rw_ref[...]) + rb_ref[...]
    lane = lax.broadcasted_iota(jnp.int32, logits.shape, 1)
    vals, idxs = [], []
    for _ in range(TOP_K):
        top = jnp.max(logits, axis=-1, keepdims=True)
        pick = jnp.min(jnp.where(logits == top, lane, N_EXPERTS), axis=-1, keepdims=True)
        vals.append(top)
        idxs.append(pick)
        logits = jnp.where(lane == pick, -jnp.inf, logits)
    exps = [jnp.exp(v - vals[0]) for v in vals]
    total = exps[0] + exps[1] + exps[2] + exps[3]
    k_lane = lax.broadcasted_iota(jnp.int32, (x1.shape[0], TOP_K), 1)
    gates = jnp.zeros((x1.shape[0], TOP_K), F32)
    picks = jnp.zeros((x1.shape[0], TOP_K), jnp.int32)
    for j in range(TOP_K):
        gates = jnp.where(k_lane == j, exps[j] / total, gates)
        picks = jnp.where(k_lane == j, idxs[j], picks)
    idx_ref[...] = picks
    gate_ref[...] = gates


def _outproj(x2, yr, ym, mod, norm_g, w_out, router_w, router_b):
    rows = x2.shape[0]
    tiles_per_mod = rows // mod.shape[0] // ROW_TILE
    row_spec = lambda w: pl.BlockSpec((ROW_TILE, w), lambda i: (i, 0))
    return pl.pallas_call(
        _outproj_kernel,
        grid=(rows // ROW_TILE,),
        in_specs=[row_spec(D_MODEL), row_spec(DA), row_spec(DB),
                  pl.BlockSpec((1, 1, 6 * D_MODEL), lambda i: (i // tiles_per_mod, 0, 0)),
                  _const_spec((1, D_MODEL)), _const_spec(w_out.shape), _const_spec(router_w.shape),
                  _const_spec((1, N_EXPERTS))],
        out_specs=[row_spec(D_MODEL), row_spec(D_MODEL), row_spec(TOP_K), row_spec(TOP_K)],
        out_shape=[jax.ShapeDtypeStruct((rows, D_MODEL), F32), jax.ShapeDtypeStruct((rows, D_MODEL), F32),
                   jax.ShapeDtypeStruct((rows, TOP_K), jnp.int32), jax.ShapeDtypeStruct((rows, TOP_K), F32)],
        compiler_params=_params("arbitrary"),
        name="outproj_router",
    )(x2, yr, ym, mod, norm_g.reshape(1, D_MODEL), w_out, router_w, router_b.reshape(1, N_EXPERTS))


def _rank_kernel(idx_ref, rank_ref, count_ref, run_s):
    i = pl.program_id(0)

    @pl.when(i == 0)
    def _():
        run_s[...] = jnp.zeros_like(run_s)

    idx = idx_ref[...]
    rows = idx.shape[0]
    lane = lax.broadcasted_iota(jnp.int32, (rows, N_EXPERTS), 1)
    hot = jnp.zeros((rows, N_EXPERTS), F32)
    for j in range(TOP_K):
        hot = hot + jnp.where(lane == idx[:, j:j + 1], 1.0, 0.0)
    before = _dot_bf16(_tri(rows, False, True).astype(F32), hot) + run_s[...]
    k_lane = lax.broadcasted_iota(jnp.int32, (rows, TOP_K), 1)
    rank = jnp.zeros((rows, TOP_K), F32)
    for j in range(TOP_K):
        rj = jnp.sum(jnp.where(lane == idx[:, j:j + 1], before, 0.0), axis=-1, keepdims=True)
        rank = jnp.where(k_lane == j, rj, rank)
    rank_ref[...] = rank.astype(jnp.int32)
    run_s[...] = run_s[...] + jnp.sum(hot, axis=0, keepdims=True)
    count_ref[...] = run_s[...].astype(jnp.int32)


def _expert_ranks(idx):
    rows = idx.shape[0]
    return pl.pallas_call(
        _rank_kernel,
        grid=(rows // ROW_TILE,),
        in_specs=[pl.BlockSpec((ROW_TILE, TOP_K), lambda i: (i, 0))],
        out_specs=[pl.BlockSpec((ROW_TILE, TOP_K), lambda i: (i, 0)), _const_spec((1, N_EXPERTS))],
        out_shape=[jax.ShapeDtypeStruct((rows, TOP_K), jnp.int32),
                   jax.ShapeDtypeStruct((1, N_EXPERTS), jnp.int32)],
        scratch_shapes=[pltpu.VMEM((1, N_EXPERTS), F32)],
        compiler_params=_params("arbitrary"),
        name="expert_ranks",
    )(idx)


def _dispatch_kernel(dest_ref, x_ref, init_ref, xs_ref, sem):
    del init_ref

    def issue(g, carry):
        for q in range(DMA_UNROLL):
            r = g * DMA_UNROLL + q
            for j in range(TOP_K):
                slot = dest_ref[r * TOP_K + j]
                pltpu.make_async_copy(x_ref.at[pl.ds(r, 1)], xs_ref.at[pl.ds(slot, 1)], sem).start()
        return carry

    lax.fori_loop(0, ROW_TILE // DMA_UNROLL, issue, 0)
    all_rows = xs_ref.at[pl.ds(0, ROW_TILE * TOP_K)]
    pltpu.make_async_copy(all_rows, all_rows, sem).wait()


def _dispatch(dest_flat, x2, n_slots):
    rows = x2.shape[0]
    init = jnp.zeros((n_slots, D_MODEL), x2.dtype)
    return pl.pallas_call(
        _dispatch_kernel,
        grid=(rows // ROW_TILE,),
        in_specs=[pl.BlockSpec((ROW_TILE * TOP_K,), lambda i: (i,), memory_space=pltpu.SMEM),
                  pl.BlockSpec((ROW_TILE, D_MODEL), lambda i: (i, 0)),
                  pl.BlockSpec(memory_space=pl.ANY)],
        out_specs=pl.BlockSpec(memory_space=pl.ANY),
        out_shape=jax.ShapeDtypeStruct(init.shape, init.dtype),
        scratch_shapes=[pltpu.SemaphoreType.DMA(())],
        input_output_aliases={2: 0},
        compiler_params=_params("arbitrary", disable_bounds_checks=True),
        name="moe_dispatch",
    )(dest_flat, x2, init)


def _expert_kernel(be_ref, nb_ref, xs_ref, w1_ref, b1_ref, w2_ref, b2_ref, y_ref, w1_s, w2_s):
    i = pl.program_id(0)
    changed = jnp.logical_or(i == 0, be_ref[i] != be_ref[jnp.maximum(i - 1, 0)])

    @pl.when(jnp.logical_and(changed, i < nb_ref[0]))
    def _():
        w1_s[...] = w1_ref[0].astype(BF16)
        w2_s[...] = w2_ref[0].astype(BF16)

    @pl.when(i < nb_ref[0])
    def _():
        xb = xs_ref[...].astype(BF16)
        hu = lax.dot_general(xb, w1_s[...], NN, preferred_element_type=F32) + b1_ref[0]
        glu = jnp.minimum(hu[:, :D_FF], SWIGLU_LIMIT)
        lin = jnp.clip(hu[:, D_FF:], -SWIGLU_LIMIT, SWIGLU_LIMIT)
        act = glu * _sigmoid(SWIGLU_ALPHA * glu) * (lin + 1.0)
        y_ref[...] = lax.dot_general(act.astype(BF16), w2_s[...], NN, preferred_element_type=F32) + b2_ref[0]

    @pl.when(i >= nb_ref[0])
    def _():
        y_ref[...] = jnp.zeros_like(y_ref)


def _experts(block_e, n_used, xs, w1, b1, w2, b2):
    n_blocks = xs.shape[0] // MOE_BLOCK
    grid_spec = pltpu.PrefetchScalarGridSpec(
        num_scalar_prefetch=2,
        grid=(n_blocks,),
        in_specs=[pl.BlockSpec((MOE_BLOCK, D_MODEL), lambda i, be, nb: (i, 0)),
                  pl.BlockSpec((1, D_MODEL, 2 * D_FF), lambda i, be, nb: (be[i], 0, 0)),
                  pl.BlockSpec((1, 1, 2 * D_FF), lambda i, be, nb: (be[i], 0, 0)),
                  pl.BlockSpec((1, D_FF, D_MODEL), lambda i, be, nb: (be[i], 0, 0)),
                  pl.BlockSpec((1, 1, D_MODEL), lambda i, be, nb: (be[i], 0, 0))],
        out_specs=pl.BlockSpec((MOE_BLOCK, D_MODEL), lambda i, be, nb: (i, 0)),
        scratch_shapes=[pltpu.VMEM((D_MODEL, 2 * D_FF), BF16), pltpu.VMEM((D_FF, D_MODEL), BF16)],
    )
    return pl.pallas_call(
        _expert_kernel,
        grid_spec=grid_spec,
        out_shape=jax.ShapeDtypeStruct(xs.shape, F32),
        compiler_params=_params("arbitrary"),
        name="moe_experts",
    )(block_e, n_used, xs, w1, b1.reshape(N_EXPERTS, 1, 2 * D_FF), w2, b2.reshape(N_EXPERTS, 1, D_MODEL))


def _combine_kernel(dest_ref, yb_ref, x1_ref, gate_ref, mod_ref, fg_ref, o_ref, buf, sem):
    def issue(g, carry):
        for q in range(DMA_UNROLL):
            r = g * DMA_UNROLL + q
            for j in range(TOP_K):
                slot = dest_ref[r * TOP_K + j]
                pltpu.make_async_copy(yb_ref.at[pl.ds(slot, 1)], buf.at[j, pl.ds(r, 1)], sem).start()
        return carry

    lax.fori_loop(0, ROW_TILE // DMA_UNROLL, issue, 0)
    pltpu.make_async_copy(buf, buf, sem).wait()

    g2 = mod_ref[0][:, 5 * D_MODEL:6 * D_MODEL]
    gates = gate_ref[...]
    moe = jnp.zeros((ROW_TILE, D_MODEL), F32)
    for j in range(TOP_K):
        moe = moe + buf[j] * gates[:, j:j + 1]
    x2 = x1_ref[...] + g2 * moe
    y = x2 * lax.rsqrt(jnp.mean(x2 * x2, axis=-1, keepdims=True) + NORM_EPS)
    o_ref[...] = y * fg_ref[...]


def _combine(dest_flat, yb, x1, gates, mod, final_g, row_offset):
    rows = x1.shape[0]
    tiles_per_mod = rows // mod.shape[0] // ROW_TILE
    tile_offset = row_offset // ROW_TILE
    row_spec = pl.BlockSpec((ROW_TILE, D_MODEL), lambda i: (i, 0))
    return pl.pallas_call(
        _combine_kernel,
        grid=(rows // ROW_TILE,),
        in_specs=[pl.BlockSpec((ROW_TILE * TOP_K,), lambda i: (i + tile_offset,), memory_space=pltpu.SMEM),
                  pl.BlockSpec(memory_space=pl.ANY),
                  row_spec,
                  pl.BlockSpec((ROW_TILE, TOP_K), lambda i: (i, 0)),
                  pl.BlockSpec((1, 1, 6 * D_MODEL), lambda i: (i // tiles_per_mod, 0, 0)),
                  _const_spec((1, D_MODEL))],
        out_specs=row_spec,
        out_shape=jax.ShapeDtypeStruct((rows, D_MODEL), F32),
        scratch_shapes=[pltpu.VMEM((TOP_K, ROW_TILE, D_MODEL), F32), pltpu.SemaphoreType.DMA(())],
        compiler_params=_params("arbitrary", disable_bounds_checks=True),
        name="moe_combine",
    )(dest_flat, yb, x1, gates, mod, final_g.reshape(1, D_MODEL))


def _block_diag_ones(width, block):
    i = jnp.arange(width) // block
    return (i[:, None] == i[None, :]).astype(F32)


def _mixers(x, mod, lp, s0, c0, n0, m0, grid):
    bsz, seq_len, _ = x.shape
    nc = seq_len // CHUNK
    x2 = x.reshape(bsz * seq_len, D_MODEL)
    zr, zm, zg, zgt = _inproj(x2, mod, lp['norm1_g'], lp['w_r'], lp['w_m'], lp['w_g'], lp['w_gt'])
    gt = zgt.reshape(N_GATES, bsz, nc, CHUNK).transpose(1, 2, 0, 3)
    yr, s_new = _rwkv(zr.reshape(bsz, seq_len, RWKV_COLS), s0, lp, grid)
    ym, c_new, n_new, m_new = _mlstm(zm.reshape(bsz, seq_len, MLSTM_MAIN), zg.reshape(bsz, seq_len, N_GATES), gt,
                                     c0, n0, m0.reshape(bsz, 2, H_B, 1), lp)
    x1, hn, idx, gates = _outproj(x2, yr.reshape(-1, DA), ym.reshape(-1, DB), mod, lp['norm2_g'], lp['w_out'],
                                  lp['router_w'], lp['router_b'])
    return x1, hn, idx, gates, (s_new, c_new, n_new, m_new[..., 0])


def kernel(x_prompt, x_sample, state_rwkv, state_mlstm_C, state_mlstm_n, state_mlstm_m, c, c_ctx, ada_w, ada_b, norm1_g, norm2_g, w_in, w_out, rwkv_mu, rwkv_w0, rwkv_w2, rwkv_a0, rwkv_a2, rwkv_g2, rwkv_kk, rwkv_ka, rwkv_rk, rwkv_gn_w, rwkv_gn_b, mlstm_conv, mlstm_bi, mlstm_bf, mlstm_norm_g, router_w, router_b, moe_w1, moe_b1, moe_w2, moe_b2, final_g):
    bp, lp_len, _ = x_prompt.shape
    bs, ls_len, _ = x_sample.shape
    w = w_in[0]
    lp = {
        'norm1_g': norm1_g[0], 'norm2_g': norm2_g[0],
        'w_r': w[:, :RWKV_COLS].astype(BF16),
        'w_m': w[:, RWKV_COLS:RWKV_COLS + MLSTM_MAIN].astype(BF16),
        'w_g': w[:, RWKV_COLS + MLSTM_MAIN:],
        'w_gt': w[:, RWKV_COLS + MLSTM_MAIN:].T,
        'w_out': w_out[0].astype(BF16),
        'mu': rwkv_mu[0].reshape(1, RWKV_COLS), 'kkw': rwkv_kk[0].reshape(1, DA), 'ka': rwkv_ka[0].reshape(1, DA),
        'rk': rwkv_rk[0].reshape(1, DA), 'gnw': rwkv_gn_w[0].reshape(1, DA), 'gnb': rwkv_gn_b[0].reshape(1, DA),
        'w0': rwkv_w0[0], 'w2': rwkv_w2[0], 'a0': rwkv_a0[0], 'a2': rwkv_a2[0], 'g2': rwkv_g2[0],
        'ones_bd': _block_diag_ones(PACK_W, HEAD_A).astype(BF16),
        'conv': mlstm_conv[0], 'bi': mlstm_bi[0].reshape(1, 2 * H_B), 'bit': mlstm_bi[0].reshape(2 * H_B, 1),
        'bf': mlstm_bf[0].reshape(1, 2 * H_B), 'bft': mlstm_bf[0].reshape(2 * H_B, 1),
        'ng': mlstm_norm_g[0].reshape(1, DB),
        'router_w': router_w[0], 'router_b': router_b[0],
    }
    cc = jnp.concatenate([c_ctx[None, :], c, jnp.zeros((2 * SUBLANES - 1 - bs, D_MODEL), F32)], axis=0)
    mod = _ada_mod(cc, ada_w[0], ada_b[0])
    mod_p = mod[0:1].reshape(1, 1, 6 * D_MODEL)
    mod_s = mod[1:1 + bs].reshape(bs, 1, 6 * D_MODEL)

    zeros = lambda *shape: jnp.zeros(shape, F32)
    x1p, hnp, idxp, gatesp, st = _mixers(x_prompt, mod_p, lp, zeros(bp, 2, H_A, HEAD_A, HEAD_A),
                                         zeros(bp, 2, H_B, HEAD_B, HEAD_B), zeros(bp, 2, H_B, HEAD_B),
                                         zeros(bp, 2, H_B), False)
    x1s, hns, idxs, gatess, _ = _mixers(x_sample, mod_s, lp, state_rwkv[:, 0], state_mlstm_C[:, 0],
                                        state_mlstm_n[:, 0], state_mlstm_m[:, 0], True)

    n_p = bp * lp_len
    idx = jnp.concatenate([idxp, idxs], axis=0)
    hn = jnp.concatenate([hnp, hns], axis=0)
    n_tok = idx.shape[0]
    rank, counts = _expert_ranks(idx)
    counts = counts[0]
    padded = (counts + MOE_BLOCK - 1) // MOE_BLOCK * MOE_BLOCK
    pad_end = jnp.cumsum(padded)
    pad_start = pad_end - padded
    n_blocks = n_tok * TOP_K // MOE_BLOCK + N_EXPERTS
    block_start = jnp.arange(n_blocks, dtype=jnp.int32) * MOE_BLOCK
    block_e = jnp.minimum(jnp.sum(pad_end[None, :] <= block_start[:, None], axis=1), N_EXPERTS - 1).astype(jnp.int32)
    n_used = (pad_end[-1:] // MOE_BLOCK).astype(jnp.int32)
    dest = (pad_start[idx] + rank).astype(jnp.int32).reshape(-1)
    xs = _dispatch(dest, hn, n_blocks * MOE_BLOCK)
    yb = _experts(block_e, n_used, xs, moe_w1[0], moe_b1[0], moe_w2[0], moe_b2[0])
    y_prompt = _combine(dest, yb, x1p, gatesp, mod_p, final_g, 0).reshape(x_prompt.shape)
    y_sample = _combine(dest, yb, x1s, gatess, mod_s, final_g, n_p).reshape(x_sample.shape)

    s_new, c_new, n_new, m_new = st
    return (y_prompt, y_sample, s_new[:, None], c_new[:, None], n_new[:, None], m_new[:, None])
```

```python
import functools

import jax
import jax.numpy as jnp
from jax import lax
from jax.experimental import pallas as pl
from jax.experimental.pallas import tpu as pltpu

F32 = jnp.float32
BF16 = jnp.bfloat16
HIGHEST = lax.Precision.HIGHEST

D_MODEL = 1024
DA = 512
HEAD_A = 64
H_A = DA // HEAD_A
DB = 512
H_B = 4
HEAD_B = DB // H_B
DECAY_LORA = 64
AAA_LORA = 64
GATE_LORA = 128
RWKV_COLS = 3 * DA + DECAY_LORA + AAA_LORA + GATE_LORA
MLSTM_MAIN = 4 * DB
N_GATES = 4 * H_B
GRID_W = 64
CHUNK = 64
N_EXPERTS = 32
TOP_K = 4
D_FF = D_MODEL
SWIGLU_LIMIT = 7.0
SWIGLU_ALPHA = 1.702
MOE_BLOCK = 256
NORM_EPS = 1e-6
GN_EPS = 64e-5
ROW_TILE = 256
DMA_UNROLL = 4
SUBLANES = 8
LANES = 128
LANE_ROWS = D_MODEL // LANES
VMEM_LIMIT = 56 * 1024 * 1024
PACK = 4
PACK_W = PACK * HEAD_A
N_PACKS = H_A // PACK

NN = (((1,), (0,)), ((), ()))
NT = (((1,), (1,)), ((), ()))
TN = (((0,), (0,)), ((), ()))


def _mm(a, b, precision=HIGHEST):
    return lax.dot_general(a, b, NN, precision=precision, preferred_element_type=F32)


def _mm_nt(a, b, precision=HIGHEST):
    return lax.dot_general(a, b, NT, precision=precision, preferred_element_type=F32)


def _mm_tn(a, b, precision=HIGHEST):
    return lax.dot_general(a, b, TN, precision=precision, preferred_element_type=F32)


def _dot_bf16(a, b, dims=NN):
    return lax.dot_general(a.astype(BF16), b.astype(BF16), dims, preferred_element_type=F32)


def _sigmoid(x):
    return 1.0 / (1.0 + jnp.exp(-x))


def _log_sigmoid(x):
    return jnp.minimum(x, 0.0) - jnp.log(1.0 + jnp.exp(-jnp.abs(x)))


def _params(*sem, **kw):
    return pltpu.CompilerParams(dimension_semantics=sem, vmem_limit_bytes=VMEM_LIMIT, **kw)


def _split_bf16(x):
    hi = x.astype(BF16)
    return hi, (x - hi.astype(F32)).astype(BF16)


def _const_spec(shape):
    nd = len(shape)
    return pl.BlockSpec(shape, lambda *_: (0,) * nd)


def _tri(n, reverse, strict):
    t = lax.broadcasted_iota(jnp.int32, (n, n), 0)
    s = lax.broadcasted_iota(jnp.int32, (n, n), 1)
    if reverse:
        return (s > t) if strict else (s >= t)
    return (s < t) if strict else (s <= t)


def _seq_neighbours(ref, b, c, nc, cols):
    r0 = pl.multiple_of(c * CHUNK, CHUNK)
    zc = ref[b, pl.ds(r0, CHUNK), cols]
    row = lax.broadcasted_iota(jnp.int32, zc.shape, 0)
    p0 = pl.multiple_of(jnp.maximum(r0 - SUBLANES, 0), SUBLANES)
    n0 = pl.multiple_of(jnp.minimum(r0 + CHUNK, (nc - 1) * CHUNK), SUBLANES)
    before = ref[b, pl.ds(p0, SUBLANES), cols][SUBLANES - 1:SUBLANES]
    after = ref[b, pl.ds(n0, SUBLANES), cols][0:1]
    before = jnp.where(c > 0, before, 0.0)
    after = jnp.where(c < nc - 1, after, 0.0)
    prev = jnp.where(row == 0, before, pltpu.roll(zc, 1, 0))
    nxt = jnp.where(row == CHUNK - 1, after, pltpu.roll(zc, CHUNK - 1, 0))
    return zc, prev, nxt


def _ada_kernel(c_ref, w_ref, b_ref, o_ref):
    cc = c_ref[...]
    o_ref[...] = _dot_bf16(cc * _sigmoid(cc), w_ref[...]) + b_ref[...]


def _ada_mod(cc, ada_w, ada_b):
    rows = cc.shape[0]
    ncol = ada_w.shape[1]
    tn = 1536
    return pl.pallas_call(
        _ada_kernel,
        grid=(ncol // tn,),
        in_specs=[_const_spec((rows, D_MODEL)),
                  pl.BlockSpec((D_MODEL, tn), lambda j: (0, j)),
                  pl.BlockSpec((1, tn), lambda j: (0, j))],
        out_specs=pl.BlockSpec((rows, tn), lambda j: (0, j)),
        out_shape=jax.ShapeDtypeStruct((rows, ncol), F32),
        compiler_params=_params("arbitrary"),
        name="ada_mod",
    )(cc, ada_w, ada_b.reshape(1, ncol))


def _inproj_kernel(x_ref, mod_ref, g_ref, wr_ref, wm_ref, wg_ref, wgt_ref, zr_ref, zm_ref, zg_ref, zgt_ref):
    x = x_ref[...]
    mod = mod_ref[0]
    sh, sc = mod[:, 0:D_MODEL], mod[:, D_MODEL:2 * D_MODEL]
    y = x * lax.rsqrt(jnp.mean(x * x, axis=-1, keepdims=True) + NORM_EPS)
    hn = y * g_ref[...] * (1.0 + sc) + sh
    hb = hn.astype(BF16)
    zr_ref[...] = lax.dot_general(hb, wr_ref[...], NN, preferred_element_type=F32)
    zm_ref[...] = lax.dot_general(hb, wm_ref[...], NN, preferred_element_type=F32)
    zg_ref[...] = _dot_bf16(hb, wg_ref[...])
    zgt_ref[...] = _dot_bf16(wgt_ref[...], hb, NT)


def _inproj(x2, mod, norm_g, w_r, w_m, w_g, w_gt):
    rows = x2.shape[0]
    tiles_per_mod = rows // mod.shape[0] // ROW_TILE
    return pl.pallas_call(
        _inproj_kernel,
        grid=(rows // ROW_TILE,),
        in_specs=[pl.BlockSpec((ROW_TILE, D_MODEL), lambda i: (i, 0)),
                  pl.BlockSpec((1, 1, 6 * D_MODEL), lambda i: (i // tiles_per_mod, 0, 0)),
                  _const_spec((1, D_MODEL)),
                  _const_spec(w_r.shape), _const_spec(w_m.shape), _const_spec(w_g.shape),
                  _const_spec(w_gt.shape)],
        out_specs=[pl.BlockSpec((ROW_TILE, RWKV_COLS), lambda i: (i, 0)),
                   pl.BlockSpec((ROW_TILE, MLSTM_MAIN), lambda i: (i, 0)),
                   pl.BlockSpec((ROW_TILE, N_GATES), lambda i: (i, 0)),
                   pl.BlockSpec((N_GATES, ROW_TILE), lambda i: (0, i))],
        out_shape=[jax.ShapeDtypeStruct((rows, RWKV_COLS), F32),
                   jax.ShapeDtypeStruct((rows, MLSTM_MAIN), F32),
                   jax.ShapeDtypeStruct((rows, N_GATES), F32),
                   jax.ShapeDtypeStruct((N_GATES, rows), F32)],
        compiler_params=_params("arbitrary"),
        name="inproj",
    )(x2, mod, norm_g.reshape(1, D_MODEL), w_r, w_m, w_g, w_gt)


def _group_sum(x, ones_bd):
    hi = x.astype(BF16)
    lo = (x - hi.astype(F32)).astype(BF16)
    return (lax.dot_general(hi, ones_bd, NN, preferred_element_type=F32)
            + lax.dot_general(lo, ones_bd, NN, preferred_element_type=F32))


def _block_diag(x, ones_bd):
    return jnp.concatenate([x.astype(BF16)] * PACK, axis=0) * ones_bd


def _rwkv_kernel(zr_ref, s0_ref, mu_ref, kkw_ref, ka_ref, rk_ref, gnw_ref, gnb_ref, w0_ref, w2_ref,
                 a0_ref, a2_ref, g2_ref, ones_ref, y_ref, sout_ref,
                 r_s, k_s, v_s, kk_s, gate_s, lx_s, st_s, *, seq_len, grid):
    nc = seq_len // CHUNK
    ones_bd = ones_ref[...]

    def mix_chunk(c, carry):
        r0 = pl.multiple_of(c * CHUNK, CHUNK)
        if grid:
            zc = zr_ref[0, pl.ds(r0, CHUNK), :]
            row = lax.broadcasted_iota(jnp.int32, zc.shape, 0)
            up0 = pl.multiple_of(jnp.maximum(c - 1, 0) * CHUNK, CHUNK)
            dn0 = pl.multiple_of(jnp.minimum(c + 1, nc - 1) * CHUNK, CHUNK)
            up = jnp.where(c > 0, zr_ref[0, pl.ds(up0, CHUNK), :], 0.0)
            down = jnp.where(c < nc - 1, zr_ref[0, pl.ds(dn0, CHUNK), :], 0.0)
            left = jnp.where(row == 0, 0.0, pltpu.roll(zc, 1, 0))
            right = jnp.where(row == CHUNK - 1, 0.0, pltpu.roll(zc, CHUNK - 1, 0))
            local = 0.25 * (up + down + left + right)
        else:
            zc, prev, nxt = _seq_neighbours(zr_ref, 0, c, nc, slice(None))
            local = 0.5 * (prev + nxt)
        z = zc + (local - zc) * mu_ref[...]
        r, k, v = z[:, 0:DA], z[:, DA:2 * DA], z[:, 2 * DA:3 * DA]
        lx = z[:, 3 * DA:3 * DA + DECAY_LORA + AAA_LORA]
        xg = z[:, 3 * DA + DECAY_LORA + AAA_LORA:]
        kk = k * kkw_ref[...]
        sq = kk * kk
        ss = jnp.concatenate([_group_sum(sq[:, p * PACK_W:(p + 1) * PACK_W], ones_bd) for p in range(N_PACKS)],
                             axis=1)
        kk = kk / jnp.maximum(jnp.sqrt(ss), 1e-12)
        lane = lax.broadcasted_iota(jnp.int32, lx.shape, 1)
        r_s[pl.ds(r0, CHUNK), :] = r
        k_s[pl.ds(r0, CHUNK), :] = k
        v_s[pl.ds(r0, CHUNK), :] = v
        kk_s[pl.ds(r0, CHUNK), :] = kk
        gate_s[pl.ds(r0, CHUNK), :] = _dot_bf16(_sigmoid(xg), g2_ref[...])
        lx_s[pl.ds(r0, CHUNK), :] = jnp.where(lane < DECAY_LORA, jnp.tanh(lx), lx)
        return carry

    lax.fori_loop(0, nc, mix_chunk, 0)

    rowblk = lax.broadcasted_iota(jnp.int32, (PACK_W, PACK_W), 0) // HEAD_A
    colblk = lax.broadcasted_iota(jnp.int32, (PACK_W, PACK_W), 1) // HEAD_A
    for d in range(2):
        for p in range(N_PACKS):
            rows_ = jnp.concatenate([s0_ref[0, d, p * PACK + h] for h in range(PACK)], axis=0)
            st_s[d, p] = jnp.where(rowblk == colblk, jnp.concatenate([rows_] * PACK, axis=1), 0.0)

    t_idx = lax.broadcasted_iota(jnp.int32, (CHUNK, PACK_W), 0)
    s_idx = lax.broadcasted_iota(jnp.int32, (CHUNK, PACK_W), 1) % CHUNK

    def scan_step(i, carry):
        first_touch = i < nc // 2
        dot = functools.partial(lax.dot_general, preferred_element_type=F32)
        chains, row_sel = [], []
        for d in range(2):
            reverse = d == 1
            c = nc - 1 - i if reverse else i
            rows = pl.ds(pl.multiple_of(c * CHUNK, CHUNK), CHUNK)
            row_sel.append(rows)
            strict = (s_idx > t_idx) if reverse else (s_idx < t_idx)
            incl = (s_idx >= t_idx) if reverse else (s_idx <= t_idx)
            last = 0 if reverse else CHUNK - 1
            r, k, v, kk = r_s[rows, :], k_s[rows, :], v_s[rows, :], kk_s[rows, :]
            lx = lx_s[rows, :]
            wl = w0_ref[d:d + 1, :] + _dot_bf16(lx[:, 0:DECAY_LORA], w2_ref[d])
            logw = -jnp.exp(_log_sigmoid(wl) - 0.5)
            a = _sigmoid(a0_ref[d:d + 1, :] + _dot_bf16(lx[:, DECAY_LORA:], a2_ref[d]))
            kd = k * (1.0 + (a - 1.0) * ka_ref[...])
            kb = kk * a
            tri = _tri(CHUNK, reverse, False).astype(BF16)
            w1, w_rest = _split_bf16(logw)
            w2, w3 = _split_bf16(w_rest)
            cum = dot(tri, w1, NN) + dot(tri, w2, NN) + dot(tri, w3, NN)
            clast = cum[last:last + 1]
            rt = r * jnp.exp(cum)
            at = -kk * jnp.exp(cum - logw)
            einv = jnp.exp(-cum)
            kt, bt = kd * einv, kb * einv
            edec = jnp.exp(clast - cum)
            kp, bp = kd * edec, kb * edec
            ptot = jnp.exp(clast)
            rkd = r * kd * rk_ref[...]
            for p in range(N_PACKS):
                cs = slice(p * PACK_W, (p + 1) * PACK_W)
                chains.append(dict(
                    d=d, p=p, cs=cs, strict=strict, incl=incl, v=v[:, cs], rkd=rkd[:, cs], ptot=ptot[:, cs],
                    ar=jnp.concatenate([at[:, cs], rt[:, cs]], axis=0).astype(BF16),
                    kt_bd=_block_diag(kt[:, cs], ones_bd), bt_bd=_block_diag(bt[:, cs], ones_bd),
                    v_bd=_block_diag(v[:, cs], ones_bd),
                    kbp=jnp.concatenate([kp[:, cs], bp[:, cs]], axis=0)))

        for ch in chains:
            ch['s_prev'] = st_s[ch['d'], ch['p']]
            ch['a_k'] = dot(ch['ar'], ch['kt_bd'], NT)
            ch['a_b'] = dot(ch['ar'], ch['bt_bd'], NT)
            ch['ars'] = dot(ch['ar'], ch['s_prev'].astype(BF16), NT)
        for ch in chains:
            a_ak = jnp.where(ch['strict'], ch['a_k'][:CHUNK], 0.0)
            a_rk = jnp.where(ch['incl'], ch['a_k'][CHUNK:], 0.0)
            ch['a_rb'] = jnp.where(ch['incl'], ch['a_b'][CHUNK:], 0.0).astype(BF16)
            n = jnp.where(ch['strict'], ch['a_b'][:CHUNK], 0.0)
            ch['n_hi'], n_lo = _split_bf16(n)
            ch['n_both'] = jnp.concatenate([ch['n_hi'], n_lo], axis=0)
            ch['pw'] = ch['n_hi']
            ch['m'] = jnp.where(s_idx == t_idx, 1.0, 0.0) + n
            akv = dot(jnp.concatenate([a_ak, a_rk], axis=0).astype(BF16), ch['v_bd'], NN)
            ch['rhs'] = ch['ars'][:CHUNK] + akv[:CHUNK]
            ch['y'] = ch['ars'][CHUNK:] + akv[CHUNK:]
        for ch in chains:
            ch['pw'] = dot(ch['pw'], _block_diag(ch['pw'], ones_bd), NN).astype(BF16)
        for j in range(1, 5):
            for ch in chains:
                pw_bd = _block_diag(ch['pw'], ones_bd)
                if j < 4:
                    both = dot(jnp.concatenate([ch['m'].astype(BF16), ch['pw']], axis=0), pw_bd, NN)
                    ch['m'] = ch['m'] + both[:CHUNK]
                    ch['pw'] = both[CHUNK:].astype(BF16)
                else:
                    ch['m'] = ch['m'] + dot(ch['m'].astype(BF16), pw_bd, NN)
        for ch in chains:
            ch['mb'] = ch['m'].astype(BF16)
            ch['u'] = dot(ch['mb'], _block_diag(ch['rhs'], ones_bd), NN)
        for ch in chains:
            u_hi, u_lo = _split_bf16(ch['u'])
            both = dot(ch['n_both'], _block_diag(u_hi, ones_bd), NN)
            nu = both[:CHUNK] + both[CHUNK:] + dot(ch['n_hi'], _block_diag(u_lo, ones_bd), NN)
            ch['resid'] = ch['rhs'] - ch['u'] + nu
        for ch in chains:
            ch['u'] = ch['u'] + dot(ch['mb'], _block_diag(ch['resid'], ones_bd), NN)
        for ch in chains:
            u = ch['u']
            ch['y'] = ch['y'] + dot(ch['a_rb'], _block_diag(u, ones_bd), NN)
            vu_hi, vu_lo = _split_bf16(jnp.concatenate([ch['v'], u], axis=0))
            kbp_hi, kbp_lo = _split_bf16(ch['kbp'])
            grown = dot(vu_hi, kbp_hi, TN) + dot(vu_hi, kbp_lo, TN) + dot(vu_lo, kbp_hi, TN)
            st_s[ch['d'], ch['p']] = ch['s_prev'] * ch['ptot'] + grown * ones_bd.astype(F32)
        n_ch = len(chains)
        sums = dot(jnp.concatenate([ch['y'] for ch in chains] + [ch['rkd'] for ch in chains],
                                   axis=0).astype(BF16), ones_bd, NN)
        ycs = [ch['y'] - sums[q * CHUNK:(q + 1) * CHUNK] * (1.0 / HEAD_A) for q, ch in enumerate(chains)]
        var = dot(jnp.concatenate([yc * yc for yc in ycs], axis=0).astype(BF16), ones_bd, NN) * (1.0 / HEAD_A)
        outs = [[], []]
        for q, ch in enumerate(chains):
            cs = ch['cs']
            out = ycs[q] * lax.rsqrt(var[q * CHUNK:(q + 1) * CHUNK] + GN_EPS) * gnw_ref[:, cs] + gnb_ref[:, cs]
            outs[ch['d']].append(out + sums[(n_ch + q) * CHUNK:(n_ch + q + 1) * CHUNK] * ch['v'])
        for d in range(2):
            rows = row_sel[d]
            out = jnp.concatenate(outs[d], axis=1)

            @pl.when(first_touch)
            def _(rows=rows, out=out):
                y_ref[0, rows, :] = out

            @pl.when(jnp.logical_not(first_touch))
            def _(rows=rows, out=out):
                y_ref[0, rows, :] = (y_ref[0, rows, :] + out) * gate_s[rows, :]
        return carry

    lax.fori_loop(0, nc, scan_step, 0)

    for d in range(2):
        for p in range(N_PACKS):
            s_fin = st_s[d, p]
            for h in range(PACK):
                sout_ref[0, d, p * PACK + h] = s_fin[h * HEAD_A:(h + 1) * HEAD_A, h * HEAD_A:(h + 1) * HEAD_A]


def _rwkv(zr, s0, p, grid):
    bsz, seq_len, _ = zr.shape
    assert (seq_len // CHUNK) % 2 == 0
    kern = functools.partial(_rwkv_kernel, seq_len=seq_len, grid=grid)
    consts = [p['mu'], p['kkw'], p['ka'], p['rk'], p['gnw'], p['gnb'], p['w0'], p['w2'], p['a0'], p['a2'],
              p['g2'], p['ones_bd']]
    return pl.pallas_call(
        kern,
        grid=(bsz,),
        in_specs=[pl.BlockSpec((1, seq_len, RWKV_COLS), lambda b: (b, 0, 0)),
                  pl.BlockSpec((1, 2, H_A, HEAD_A, HEAD_A), lambda b: (b, 0, 0, 0, 0))]
                 + [_const_spec(a.shape) for a in consts],
        out_specs=[pl.BlockSpec((1, seq_len, DA), lambda b: (b, 0, 0)),
                   pl.BlockSpec((1, 2, H_A, HEAD_A, HEAD_A), lambda b: (b, 0, 0, 0, 0))],
        out_shape=[jax.ShapeDtypeStruct((bsz, seq_len, DA), F32),
                   jax.ShapeDtypeStruct((bsz, 2, H_A, HEAD_A, HEAD_A), F32)],
        scratch_shapes=[pltpu.VMEM((seq_len, DA), F32)] * 5
                       + [pltpu.VMEM((seq_len, DECAY_LORA + AAA_LORA), F32),
                          pltpu.VMEM((2, N_PACKS, PACK_W, PACK_W), F32)],
        compiler_params=_params("arbitrary"),
        name="rwkv_scan",
    )(zr, s0, *consts)


def _mlstm_kernel(zm_ref, g_ref, gt_ref, c0_ref, n0_ref, m0_ref, conv_ref, bi_ref, bit_ref, bf_ref, bft_ref,
                  ng_ref, y_ref, cout_ref, nout_ref, mout_ref,
                  qk_s, h_s, c_s, n_s, m_s, *, seq_len):
    nc = seq_len // CHUNK

    def conv_chunk(c, carry):
        zc, prev, nxt = _seq_neighbours(zm_ref, 0, c, nc, slice(0, 2 * DB))
        u = conv_ref[0:1, :] * prev + conv_ref[1:2, :] * zc + conv_ref[2:3, :] * nxt
        qk = u * _sigmoid(u)
        lane = lax.broadcasted_iota(jnp.int32, qk.shape, 1)
        qk_s[pl.ds(pl.multiple_of(c * CHUNK, CHUNK), CHUNK), :] = jnp.where(lane >= DB, qk * (HEAD_B ** -0.5), qk)
        return carry

    lax.fori_loop(0, nc, conv_chunk, 0)

    for d in range(2):
        c_s[d] = c0_ref[0, d]
        n_s[d] = n0_ref[0, d]
        m_s[d] = jnp.broadcast_to(m0_ref[0, d], (H_B, LANES))

    def scan_step(i, carry):
        chains = []
        for d in range(2):
            reverse = d == 1
            incl = _tri(CHUNK, reverse, False)
            tri_col = incl.astype(F32)
            tri_row = _tri(CHUNK, not reverse, False).astype(F32)
            last = 0 if reverse else CHUNK - 1
            gsl = slice(d * H_B, (d + 1) * H_B)
            fsl = slice(2 * H_B + d * H_B, 2 * H_B + (d + 1) * H_B)
            c = nc - 1 - i if reverse else i
            rows = pl.ds(pl.multiple_of(c * CHUNK, CHUNK), CHUNK)
            gcol = g_ref[0, rows, :]
            grow = gt_ref[0, c]
            i_col = gcol[:, gsl] + bi_ref[:, gsl]
            f_col = _log_sigmoid(gcol[:, fsl] + bf_ref[:, gsl])
            i_row = grow[gsl, :] + bit_ref[gsl, :]
            f_row = _log_sigmoid(grow[fsl, :] + bft_ref[gsl, :])
            b_col = _mm(tri_col, f_col)
            b_row = _mm(f_row, tri_row)
            b_last = b_col[last:last + 1]
            for h in range(H_B):
                q = qk_s[rows, h * HEAD_B:(h + 1) * HEAD_B]
                chains.append(dict(
                    d=d, h=h, rows=rows, incl=incl, q=q, qb=q.astype(BF16),
                    k=qk_s[rows, DB + h * HEAD_B:DB + (h + 1) * HEAD_B],
                    v=zm_ref[0, rows, 2 * DB + h * HEAD_B:2 * DB + (h + 1) * HEAD_B],
                    bc=b_col[:, h:h + 1], br=b_row[h:h + 1, :], ic=i_col[:, h:h + 1], ir=i_row[h:h + 1, :],
                    bl=b_last[:, h:h + 1]))
        for ch in chains:
            d, h = ch['d'], ch['h']
            ch['qk'] = _dot_bf16(ch['qb'], ch['k'], NT)
            ch['c_prev'] = c_s[d, h]
            ch['qc'] = _dot_bf16(ch['qb'], ch['c_prev'])
        for ch in chains:
            d, h = ch['d'], ch['h']
            m_prev = m_s[d, h:h + 1, 0:1]
            dmat = jnp.where(ch['incl'], ch['bc'] - ch['br'] + ch['ir'], -jnp.inf)
            inter = ch['bc'] + m_prev
            m_t = jnp.maximum(inter, jnp.max(dmat, axis=-1, keepdims=True))
            s = ch['qk'] * jnp.exp(dmat - m_t)
            carry_w = jnp.exp(inter - m_t)
            n_prev = n_s[d, h:h + 1, :]
            num = _dot_bf16(s, ch['v']) + carry_w * ch['qc']
            den = (jnp.sum(s, axis=-1, keepdims=True)
                   + carry_w * jnp.sum(ch['q'] * n_prev, axis=-1, keepdims=True))
            h_s[d, ch['rows'], h * HEAD_B:(h + 1) * HEAD_B] = num / jnp.maximum(jnp.abs(den), jnp.exp(-m_t))
            g = ch['bl'] - ch['bc'] + ch['ic']
            m_new = jnp.maximum(ch['bl'] + m_prev, jnp.max(g, axis=0, keepdims=True))
            wk = jnp.exp(g - m_new)
            decay = jnp.exp(ch['bl'] + m_prev - m_new)
            c_s[d, h] = decay * ch['c_prev'] + _dot_bf16(ch['k'], wk * ch['v'], TN)
            n_s[d, h:h + 1, :] = decay * n_prev + jnp.sum(wk * ch['k'], axis=0, keepdims=True)
            m_s[d, h:h + 1, :] = jnp.broadcast_to(m_new, (1, LANES))
        return carry

    lax.fori_loop(0, nc, scan_step, 0)
    cout_ref[0] = c_s[...]
    nout_ref[0] = n_s[...]
    mout_ref[0] = m_s[...]

    def norm_chunk(c, carry):
        rows = pl.ds(pl.multiple_of(c * CHUNK, CHUNK), CHUNK)
        og = zm_ref[0, rows, 3 * DB:4 * DB]
        hm = h_s[0, rows, :] + h_s[1, rows, :]
        parts = []
        for h in range(H_B):
            hh = hm[:, h * HEAD_B:(h + 1) * HEAD_B]
            parts.append(hh * lax.rsqrt(jnp.mean(hh * hh, axis=-1, keepdims=True) + NORM_EPS))
        y_ref[0, rows, :] = jnp.concatenate(parts, axis=1) * ng_ref[...] * _sigmoid(og)
        return carry

    lax.fori_loop(0, nc, norm_chunk, 0)


def _mlstm(zm, g, gt, c0, n0, m0, p):
    bsz, seq_len, _ = zm.shape
    nc = seq_len // CHUNK
    kern = functools.partial(_mlstm_kernel, seq_len=seq_len)
    consts = [p['conv'], p['bi'], p['bit'], p['bf'], p['bft'], p['ng']]
    return pl.pallas_call(
        kern,
        grid=(bsz,),
        in_specs=[pl.BlockSpec((1, seq_len, MLSTM_MAIN), lambda b: (b, 0, 0)),
                  pl.BlockSpec((1, seq_len, N_GATES), lambda b: (b, 0, 0)),
                  pl.BlockSpec((1, nc, N_GATES, CHUNK), lambda b: (b, 0, 0, 0)),
                  pl.BlockSpec((1, 2, H_B, HEAD_B, HEAD_B), lambda b: (b, 0, 0, 0, 0)),
                  pl.BlockSpec((1, 2, H_B, HEAD_B), lambda b: (b, 0, 0, 0)),
                  pl.BlockSpec((1, 2, H_B, 1), lambda b: (b, 0, 0, 0))]
                 + [_const_spec(a.shape) for a in consts],
        out_specs=[pl.BlockSpec((1, seq_len, DB), lambda b: (b, 0, 0)),
                   pl.BlockSpec((1, 2, H_B, HEAD_B, HEAD_B), lambda b: (b, 0, 0, 0, 0)),
                   pl.BlockSpec((1, 2, H_B, HEAD_B), lambda b: (b, 0, 0, 0)),
                   pl.BlockSpec((1, 2, H_B, LANES), lambda b: (b, 0, 0, 0))],
        out_shape=[jax.ShapeDtypeStruct((bsz, seq_len, DB), F32),
                   jax.ShapeDtypeStruct((bsz, 2, H_B, HEAD_B, HEAD_B), F32),
                   jax.ShapeDtypeStruct((bsz, 2, H_B, HEAD_B), F32),
                   jax.ShapeDtypeStruct((bsz, 2, H_B, LANES), F32)],
        scratch_shapes=[pltpu.VMEM((seq_len, 2 * DB), F32), pltpu.VMEM((2, seq_len, DB), F32),
                        pltpu.VMEM((2, H_B, HEAD_B, HEAD_B), F32), pltpu.VMEM((2, H_B, HEAD_B), F32),
                        pltpu.VMEM((2, H_B, LANES), F32)],
        compiler_params=_params("arbitrary"),
        name="mlstm_scan",
    )(zm, g, gt, c0, n0, m0, *consts)


def _outproj_kernel(x_ref, yr_ref, ym_ref, mod_ref, g_ref, wo_ref, rw_ref, rb_ref,
                    x1_ref, hn_ref, idx_ref, gate_ref):
    mod = mod_ref[0]
    g1 = mod[:, 2 * D_MODEL:3 * D_MODEL]
    sh2, sc2 = mod[:, 3 * D_MODEL:4 * D_MODEL], mod[:, 4 * D_MODEL:5 * D_MODEL]
    mix = _dot_bf16(yr_ref[...], wo_ref[0:DA, :]) + _dot_bf16(ym_ref[...], wo_ref[DA:, :])
    x1 = x_ref[...] + g1 * mix
    x1_ref[...] = x1
    y = x1 * lax.rsqrt(jnp.mean(x1 * x1, axis=-1, keepdims=True) + NORM_EPS)
    hn = y * g_ref[...] * (1.0 + sc2) + sh2
    hn_ref[...] = hn
    logits = _dot_bf16(hn, rw_ref[...]) + rb_ref[...]
    lane = lax.broadcasted_iota(jnp.int32, logits.shape, 1)
    vals, idxs = [], []
    for _ in range(TOP_K):
        top = jnp.max(logits, axis=-1, keepdims=True)
        pick = jnp.min(jnp.where(logits == top, lane, N_EXPERTS), axis=-1, keepdims=True)
        vals.append(top)
        idxs.append(pick)
        logits = jnp.where(lane == pick, -jnp.inf, logits)
    exps = [jnp.exp(v - vals[0]) for v in vals]
    total = exps[0] + exps[1] + exps[2] + exps[3]
    k_lane = lax.broadcasted_iota(jnp.int32, (x1.shape[0], TOP_K), 1)
    gates = jnp.zeros((x1.shape[0], TOP_K), F32)
    picks = jnp.zeros((x1.shape[0], TOP_K), jnp.int32)
    for j in range(TOP_K):
        gates = jnp.where(k_lane == j, exps[j] / total, gates)
        picks = jnp.where(k_lane == j, idxs[j], picks)
    idx_ref[...] = picks
    gate_ref[...] = gates


def _outproj(x2, yr, ym, mod, norm_g, w_out, router_w, router_b):
    rows = x2.shape[0]
    tiles_per_mod = rows // mod.shape[0] // ROW_TILE
    row_spec = lambda w: pl.BlockSpec((ROW_TILE, w), lambda i: (i, 0))
    return pl.pallas_call(
        _outproj_kernel,
        grid=(rows // ROW_TILE,),
        in_specs=[row_spec(D_MODEL), row_spec(DA), row_spec(DB),
                  pl.BlockSpec((1, 1, 6 * D_MODEL), lambda i: (i // tiles_per_mod, 0, 0)),
                  _const_spec((1, D_MODEL)), _const_spec(w_out.shape), _const_spec(router_w.shape),
                  _const_spec((1, N_EXPERTS))],
        out_specs=[row_spec(D_MODEL), row_spec(D_MODEL), row_spec(TOP_K), row_spec(TOP_K)],
        out_shape=[jax.ShapeDtypeStruct((rows, D_MODEL), F32), jax.ShapeDtypeStruct((rows, D_MODEL), F32),
                   jax.ShapeDtypeStruct((rows, TOP_K), jnp.int32), jax.ShapeDtypeStruct((rows, TOP_K), F32)],
        compiler_params=_params("arbitrary"),
        name="outproj_router",
    )(x2, yr, ym, mod, norm_g.reshape(1, D_MODEL), w_out, router_w, router_b.reshape(1, N_EXPERTS))


def _rank_kernel(idx_ref, rank_ref, count_ref, run_s):
    i = pl.program_id(0)

    @pl.when(i == 0)
    def _():
        run_s[...] = jnp.zeros_like(run_s)

    idx = idx_ref[...]
    rows = idx.shape[0]
    lane = lax.broadcasted_iota(jnp.int32, (rows, N_EXPERTS), 1)
    hot = jnp.zeros((rows, N_EXPERTS), F32)
    for j in range(TOP_K):
        hot = hot + jnp.where(lane == idx[:, j:j + 1], 1.0, 0.0)
    before = _dot_bf16(_tri(rows, False, True).astype(F32), hot) + run_s[...]
    k_lane = lax.broadcasted_iota(jnp.int32, (rows, TOP_K), 1)
    rank = jnp.zeros((rows, TOP_K), F32)
    for j in range(TOP_K):
        rj = jnp.sum(jnp.where(lane == idx[:, j:j + 1], before, 0.0), axis=-1, keepdims=True)
        rank = jnp.where(k_lane == j, rj, rank)
    rank_ref[...] = rank.astype(jnp.int32)
    run_s[...] = run_s[...] + jnp.sum(hot, axis=0, keepdims=True)
    count_ref[...] = run_s[...].astype(jnp.int32)


def _expert_ranks(idx):
    rows = idx.shape[0]
    return pl.pallas_call(
        _rank_kernel,
        grid=(rows // ROW_TILE,),
        in_specs=[pl.BlockSpec((ROW_TILE, TOP_K), lambda i: (i, 0))],
        out_specs=[pl.BlockSpec((ROW_TILE, TOP_K), lambda i: (i, 0)), _const_spec((1, N_EXPERTS))],
        out_shape=[jax.ShapeDtypeStruct((rows, TOP_K), jnp.int32),
                   jax.ShapeDtypeStruct((1, N_EXPERTS), jnp.int32)],
        scratch_shapes=[pltpu.VMEM((1, N_EXPERTS), F32)],
        compiler_params=_params("arbitrary"),
        name="expert_ranks",
    )(idx)


IDX_STRIDE = 4 * MOE_BLOCK
IDX_NEXT_SRC, IDX_PREV_DST, IDX_SRC, IDX_DST = (k * MOE_BLOCK for k in range(4))


def _expert_kernel(be_ref, nb_ref, idx_ref, hn_ref, w1_ref, b1_ref, w2_ref, b2_ref, yk_ref,
                   xbuf0, xbuf1, ybuf0, ybuf1, w1_s, w2_s, gsem, ssem):
    i = pl.program_id(0)
    n_used = nb_ref[0]
    xbuf, ybuf = (xbuf0, xbuf1), (ybuf0, ybuf1)

    def start_gather(base, s, r):
        tok = idx_ref[base + r]
        pltpu.make_async_copy(hn_ref.at[pl.ds(tok, 1)], xbuf[s].at[pl.ds(r, 1)], gsem.at[s]).start()

    def start_scatter(base, s, r):
        row = idx_ref[base + r]
        pltpu.make_async_copy(ybuf[s].at[pl.ds(r, 1)], yk_ref.at[pl.ds(row, 1)], ssem.at[s]).start()

    def looped(start, base, s):
        def body(g, carry):
            for q in range(DMA_UNROLL):
                start(base, s, g * DMA_UNROLL + q)
            return carry
        lax.fori_loop(0, MOE_BLOCK // DMA_UNROLL, body, 0)

    def wait_block(buf, sem, s):
        pltpu.make_async_copy(buf[s], buf[s], sem.at[s]).wait()

    @pl.when(i == 0)
    def _():
        looped(start_gather, IDX_SRC, 0)
        ybuf1[...] = jnp.zeros((MOE_BLOCK, D_MODEL), F32)

    def step(s):
        changed = jnp.logical_or(i == 0, be_ref[i] != be_ref[jnp.maximum(i - 1, 0)])

        @pl.when(changed)
        def _():
            w1_s[...] = w1_ref[0].astype(BF16)
            w2_s[...] = w2_ref[0].astype(BF16)

        wait_block(xbuf, gsem, s)

        @pl.when(i > 0)
        def _():
            wait_block(ybuf, ssem, s)

        for r in range(MOE_BLOCK):
            start_gather(IDX_NEXT_SRC, 1 - s, r)
        for r in range(MOE_BLOCK):
            start_scatter(IDX_PREV_DST, 1 - s, r)
        xb = xbuf[s][...].astype(BF16)
        hu = lax.dot_general(xb, w1_s[...], NN, preferred_element_type=F32) + b1_ref[0]
        glu = jnp.minimum(hu[:, :D_FF], SWIGLU_LIMIT)
        lin = jnp.clip(hu[:, D_FF:], -SWIGLU_LIMIT, SWIGLU_LIMIT)
        act = glu * _sigmoid(SWIGLU_ALPHA * glu) * (lin + 1.0)
        ybuf[s][...] = lax.dot_general(act.astype(BF16), w2_s[...], NN, preferred_element_type=F32) + b2_ref[0]

        @pl.when(i == n_used - 1)
        def _():
            wait_block(ybuf, ssem, 1 - s)
            looped(start_scatter, IDX_DST, s)
            wait_block(ybuf, ssem, s)
            wait_block(xbuf, gsem, 1 - s)

    for s in range(2):
        @pl.when(jnp.logical_and(i < n_used, i % 2 == s))
        def _(s=s):
            step(s)


def _experts(block_e, n_used, idx_steps, hn, n_out_rows, w1, b1, w2, b2):
    n_blocks = block_e.shape[0]
    grid_spec = pltpu.PrefetchScalarGridSpec(
        num_scalar_prefetch=2,
        grid=(n_blocks,),
        in_specs=[pl.BlockSpec((IDX_STRIDE,), lambda i, be, nb: (i,), memory_space=pltpu.SMEM),
                  pl.BlockSpec(memory_space=pl.ANY),
                  pl.BlockSpec((1, D_MODEL, 2 * D_FF), lambda i, be, nb: (be[i], 0, 0)),
                  pl.BlockSpec((1, 1, 2 * D_FF), lambda i, be, nb: (be[i], 0, 0)),
                  pl.BlockSpec((1, D_FF, D_MODEL), lambda i, be, nb: (be[i], 0, 0)),
                  pl.BlockSpec((1, 1, D_MODEL), lambda i, be, nb: (be[i], 0, 0))],
        out_specs=pl.BlockSpec(memory_space=pl.ANY),
        scratch_shapes=[pltpu.VMEM((MOE_BLOCK, D_MODEL), F32)] * 4
                       + [pltpu.VMEM((D_MODEL, 2 * D_FF), BF16), pltpu.VMEM((D_FF, D_MODEL), BF16),
                        pltpu.SemaphoreType.DMA((2,)), pltpu.SemaphoreType.DMA((2,))],
    )
    return pl.pallas_call(
        _expert_kernel,
        grid_spec=grid_spec,
        out_shape=jax.ShapeDtypeStruct((n_out_rows, D_MODEL), F32),
        compiler_params=_params("arbitrary", disable_bounds_checks=True),
        name="moe_experts",
    )(block_e, n_used, idx_steps, hn, w1, b1.reshape(N_EXPERTS, 1, 2 * D_FF), w2, b2.reshape(N_EXPERTS, 1, D_MODEL))


def _combine_kernel(y0_ref, y1_ref, y2_ref, y3_ref, x1_ref, gate_ref, mod_ref, fg_ref, o_ref):
    g2 = mod_ref[0][:, 5 * D_MODEL:6 * D_MODEL]
    gates = gate_ref[...]
    moe = jnp.zeros((ROW_TILE, D_MODEL), F32)
    for j, y_ref in enumerate((y0_ref, y1_ref, y2_ref, y3_ref)):
        moe = moe + y_ref[...] * gates[:, j:j + 1]
    x2 = x1_ref[...] + g2 * moe
    y = x2 * lax.rsqrt(jnp.mean(x2 * x2, axis=-1, keepdims=True) + NORM_EPS)
    o_ref[...] = y * fg_ref[...]


def _combine(yk, n_tok, x1, gates, mod, final_g, row_offset):
    rows = x1.shape[0]
    tiles_per_mod = rows // mod.shape[0] // ROW_TILE
    tile_offset = row_offset // ROW_TILE
    tiles_tok = n_tok // ROW_TILE
    row_spec = pl.BlockSpec((ROW_TILE, D_MODEL), lambda i: (i, 0))
    y_specs = [pl.BlockSpec((ROW_TILE, D_MODEL), lambda i, j=j: (j * tiles_tok + tile_offset + i, 0))
               for j in range(TOP_K)]
    return pl.pallas_call(
        _combine_kernel,
        grid=(rows // ROW_TILE,),
        in_specs=y_specs + [row_spec,
                            pl.BlockSpec((ROW_TILE, TOP_K), lambda i: (i, 0)),
                            pl.BlockSpec((1, 1, 6 * D_MODEL), lambda i: (i // tiles_per_mod, 0, 0)),
                            _const_spec((1, D_MODEL))],
        out_specs=row_spec,
        out_shape=jax.ShapeDtypeStruct((rows, D_MODEL), F32),
        compiler_params=_params("arbitrary"),
        name="moe_combine",
    )(yk, yk, yk, yk, x1, gates, mod, final_g.reshape(1, D_MODEL))


def _block_diag_ones(width, block):
    i = jnp.arange(width) // block
    return (i[:, None] == i[None, :]).astype(F32)


def _mixers(x, mod, lp, s0, c0, n0, m0, grid):
    bsz, seq_len, _ = x.shape
    nc = seq_len // CHUNK
    x2 = x.reshape(bsz * seq_len, D_MODEL)
    zr, zm, zg, zgt = _inproj(x2, mod, lp['norm1_g'], lp['w_r'], lp['w_m'], lp['w_g'], lp['w_gt'])
    gt = zgt.reshape(N_GATES, bsz, nc, CHUNK).transpose(1, 2, 0, 3)
    yr, s_new = _rwkv(zr.reshape(bsz, seq_len, RWKV_COLS), s0, lp, grid)
    ym, c_new, n_new, m_new = _mlstm(zm.reshape(bsz, seq_len, MLSTM_MAIN), zg.reshape(bsz, seq_len, N_GATES), gt,
                                     c0, n0, m0.reshape(bsz, 2, H_B, 1), lp)
    x1, hn, idx, gates = _outproj(x2, yr.reshape(-1, DA), ym.reshape(-1, DB), mod, lp['norm2_g'], lp['w_out'],
                                  lp['router_w'], lp['router_b'])
    return x1, hn, idx, gates, (s_new, c_new, n_new, m_new[..., 0])


def kernel(x_prompt, x_sample, state_rwkv, state_mlstm_C, state_mlstm_n, state_mlstm_m, c, c_ctx, ada_w, ada_b, norm1_g, norm2_g, w_in, w_out, rwkv_mu, rwkv_w0, rwkv_w2, rwkv_a0, rwkv_a2, rwkv_g2, rwkv_kk, rwkv_ka, rwkv_rk, rwkv_gn_w, rwkv_gn_b, mlstm_conv, mlstm_bi, mlstm_bf, mlstm_norm_g, router_w, router_b, moe_w1, moe_b1, moe_w2, moe_b2, final_g):
    bp, lp_len, _ = x_prompt.shape
    bs, ls_len, _ = x_sample.shape
    w = w_in[0]
    lp = {
        'norm1_g': norm1_g[0], 'norm2_g': norm2_g[0],
        'w_r': w[:, :RWKV_COLS].astype(BF16),
        'w_m': w[:, RWKV_COLS:RWKV_COLS + MLSTM_MAIN].astype(BF16),
        'w_g': w[:, RWKV_COLS + MLSTM_MAIN:],
        'w_gt': w[:, RWKV_COLS + MLSTM_MAIN:].T,
        'w_out': w_out[0].astype(BF16),
        'mu': rwkv_mu[0].reshape(1, RWKV_COLS), 'kkw': rwkv_kk[0].reshape(1, DA), 'ka': rwkv_ka[0].reshape(1, DA),
        'rk': rwkv_rk[0].reshape(1, DA), 'gnw': rwkv_gn_w[0].reshape(1, DA), 'gnb': rwkv_gn_b[0].reshape(1, DA),
        'w0': rwkv_w0[0], 'w2': rwkv_w2[0], 'a0': rwkv_a0[0], 'a2': rwkv_a2[0], 'g2': rwkv_g2[0],
        'ones_bd': _block_diag_ones(PACK_W, HEAD_A).astype(BF16),
        'conv': mlstm_conv[0], 'bi': mlstm_bi[0].reshape(1, 2 * H_B), 'bit': mlstm_bi[0].reshape(2 * H_B, 1),
        'bf': mlstm_bf[0].reshape(1, 2 * H_B), 'bft': mlstm_bf[0].reshape(2 * H_B, 1),
        'ng': mlstm_norm_g[0].reshape(1, DB),
        'router_w': router_w[0], 'router_b': router_b[0],
    }
    cc = jnp.concatenate([c_ctx[None, :], c, jnp.zeros((2 * SUBLANES - 1 - bs, D_MODEL), F32)], axis=0)
    mod = _ada_mod(cc, ada_w[0], ada_b[0])
    mod_p = mod[0:1].reshape(1, 1, 6 * D_MODEL)
    mod_s = mod[1:1 + bs].reshape(bs, 1, 6 * D_MODEL)

    zeros = lambda *shape: jnp.zeros(shape, F32)
    x1p, hnp, idxp, gatesp, st = _mixers(x_prompt, mod_p, lp, zeros(bp, 2, H_A, HEAD_A, HEAD_A),
                                         zeros(bp, 2, H_B, HEAD_B, HEAD_B), zeros(bp, 2, H_B, HEAD_B),
                                         zeros(bp, 2, H_B), False)
    x1s, hns, idxs, gatess, _ = _mixers(x_sample, mod_s, lp, state_rwkv[:, 0], state_mlstm_C[:, 0],
                                        state_mlstm_n[:, 0], state_mlstm_m[:, 0], True)

    n_p = bp * lp_len
    idx = jnp.concatenate([idxp, idxs], axis=0)
    hn = jnp.concatenate([hnp, hns], axis=0)
    n_tok = idx.shape[0]
    rank, counts = _expert_ranks(idx)
    counts = counts[0]
    padded = (counts + MOE_BLOCK - 1) // MOE_BLOCK * MOE_BLOCK
    pad_end = jnp.cumsum(padded)
    pad_start = pad_end - padded
    n_blocks = n_tok * TOP_K // MOE_BLOCK + N_EXPERTS
    block_start = jnp.arange(n_blocks, dtype=jnp.int32) * MOE_BLOCK
    block_e = jnp.minimum(jnp.sum(pad_end[None, :] <= block_start[:, None], axis=1), N_EXPERTS - 1).astype(jnp.int32)
    n_used = (pad_end[-1:] // MOE_BLOCK).astype(jnp.int32)
    dest = (pad_start[idx] + rank).astype(jnp.int32).reshape(-1)
    n_slots = n_blocks * MOE_BLOCK
    slot_ids = jnp.arange(n_slots, dtype=jnp.int32)
    owner = jnp.full((n_slots,), -1, jnp.int32).at[dest].set(jnp.arange(n_tok * TOP_K, dtype=jnp.int32))
    tok, choice = owner // TOP_K, owner % TOP_K
    src = jnp.where(owner >= 0, tok, 0).reshape(n_blocks, MOE_BLOCK)
    spare = TOP_K * n_tok + slot_ids % MOE_BLOCK
    dst = jnp.where(owner >= 0, choice * n_tok + tok, spare).reshape(n_blocks, MOE_BLOCK)
    prev_dst = jnp.concatenate([spare[None, :MOE_BLOCK], dst[:-1]], axis=0)
    idx_steps = jnp.concatenate([jnp.roll(src, -1, axis=0), prev_dst, src, dst], axis=1).reshape(-1)
    yk = _experts(block_e, n_used, idx_steps, hn, TOP_K * n_tok + MOE_BLOCK, moe_w1[0], moe_b1[0], moe_w2[0],
                  moe_b2[0])
    y_prompt = _combine(yk, n_tok, x1p, gatesp, mod_p, final_g, 0).reshape(x_prompt.shape)
    y_sample = _combine(yk, n_tok, x1s, gatess, mod_s, final_g, n_p).reshape(x_sample.shape)

    s_new, c_new, n_new, m_new = st
    return (y_prompt, y_sample, s_new[:, None], c_new[:, None], n_new[:, None], m_new[:, None])
```

```python
import functools

import jax
import jax.numpy as jnp
from jax import lax
from jax.experimental import pallas as pl
from jax.experimental.pallas import tpu as pltpu

F32 = jnp.float32
BF16 = jnp.bfloat16
HIGHEST = lax.Precision.HIGHEST

D_MODEL = 1024
DA = 512
HEAD_A = 64
H_A = DA // HEAD_A
DB = 512
H_B = 4
HEAD_B = DB // H_B
DECAY_LORA = 64
AAA_LORA = 64
GATE_LORA = 128
RWKV_COLS = 3 * DA + DECAY_LORA + AAA_LORA + GATE_LORA
MLSTM_MAIN = 4 * DB
N_GATES = 4 * H_B
GRID_W = 64
CHUNK = 64
N_EXPERTS = 32
TOP_K = 4
D_FF = D_MODEL
SWIGLU_LIMIT = 7.0
SWIGLU_ALPHA = 1.702
MOE_BLOCK = 256
NORM_EPS = 1e-6
GN_EPS = 64e-5
ROW_TILE = 256
DMA_UNROLL = 4
SUBLANES = 8
LANES = 128
LANE_ROWS = D_MODEL // LANES
VMEM_LIMIT = 56 * 1024 * 1024
PACK = 4
PACK_W = PACK * HEAD_A
N_PACKS = H_A // PACK

NN = (((1,), (0,)), ((), ()))
NT = (((1,), (1,)), ((), ()))
TN = (((0,), (0,)), ((), ()))


def _mm(a, b, precision=HIGHEST):
    return lax.dot_general(a, b, NN, precision=precision, preferred_element_type=F32)


def _mm_nt(a, b, precision=HIGHEST):
    return lax.dot_general(a, b, NT, precision=precision, preferred_element_type=F32)


def _mm_tn(a, b, precision=HIGHEST):
    return lax.dot_general(a, b, TN, precision=precision, preferred_element_type=F32)


def _dot_bf16(a, b, dims=NN):
    return lax.dot_general(a.astype(BF16), b.astype(BF16), dims, preferred_element_type=F32)


def _sigmoid(x):
    return 1.0 / (1.0 + jnp.exp(-x))


def _log_sigmoid(x):
    return jnp.minimum(x, 0.0) - jnp.log(1.0 + jnp.exp(-jnp.abs(x)))


def _params(*sem, **kw):
    return pltpu.CompilerParams(dimension_semantics=sem, vmem_limit_bytes=VMEM_LIMIT, **kw)


def _split_bf16(x):
    hi = x.astype(BF16)
    return hi, (x - hi.astype(F32)).astype(BF16)


def _const_spec(shape):
    nd = len(shape)
    return pl.BlockSpec(shape, lambda *_: (0,) * nd)


def _tri(n, reverse, strict):
    t = lax.broadcasted_iota(jnp.int32, (n, n), 0)
    s = lax.broadcasted_iota(jnp.int32, (n, n), 1)
    if reverse:
        return (s > t) if strict else (s >= t)
    return (s < t) if strict else (s <= t)


def _seq_neighbours(ref, b, c, nc, cols):
    r0 = pl.multiple_of(c * CHUNK, CHUNK)
    zc = ref[b, pl.ds(r0, CHUNK), cols]
    row = lax.broadcasted_iota(jnp.int32, zc.shape, 0)
    p0 = pl.multiple_of(jnp.maximum(r0 - SUBLANES, 0), SUBLANES)
    n0 = pl.multiple_of(jnp.minimum(r0 + CHUNK, (nc - 1) * CHUNK), SUBLANES)
    before = ref[b, pl.ds(p0, SUBLANES), cols][SUBLANES - 1:SUBLANES]
    after = ref[b, pl.ds(n0, SUBLANES), cols][0:1]
    before = jnp.where(c > 0, before, 0.0)
    after = jnp.where(c < nc - 1, after, 0.0)
    prev = jnp.where(row == 0, before, pltpu.roll(zc, 1, 0))
    nxt = jnp.where(row == CHUNK - 1, after, pltpu.roll(zc, CHUNK - 1, 0))
    return zc, prev, nxt


def _ada_kernel(c_ref, w_ref, b_ref, o_ref):
    cc = c_ref[...]
    o_ref[...] = _dot_bf16(cc * _sigmoid(cc), w_ref[...]) + b_ref[...]


def _ada_mod(cc, ada_w, ada_b):
    rows = cc.shape[0]
    ncol = ada_w.shape[1]
    tn = 1536
    return pl.pallas_call(
        _ada_kernel,
        grid=(ncol // tn,),
        in_specs=[_const_spec((rows, D_MODEL)),
                  pl.BlockSpec((D_MODEL, tn), lambda j: (0, j)),
                  pl.BlockSpec((1, tn), lambda j: (0, j))],
        out_specs=pl.BlockSpec((rows, tn), lambda j: (0, j)),
        out_shape=jax.ShapeDtypeStruct((rows, ncol), F32),
        compiler_params=_params("arbitrary"),
        name="ada_mod",
    )(cc, ada_w, ada_b.reshape(1, ncol))


def _inproj_kernel(x_ref, mod_ref, g_ref, wr_ref, wm_ref, wg_ref, wgt_ref, zr_ref, zm_ref, zg_ref, zgt_ref):
    x = x_ref[...]
    mod = mod_ref[0]
    sh, sc = mod[:, 0:D_MODEL], mod[:, D_MODEL:2 * D_MODEL]
    y = x * lax.rsqrt(jnp.mean(x * x, axis=-1, keepdims=True) + NORM_EPS)
    hn = y * g_ref[...] * (1.0 + sc) + sh
    hb = hn.astype(BF16)
    zr_ref[...] = lax.dot_general(hb, wr_ref[...], NN, preferred_element_type=F32)
    zm_ref[...] = lax.dot_general(hb, wm_ref[...], NN, preferred_element_type=F32)
    zg_ref[...] = _dot_bf16(hb, wg_ref[...])
    zgt_ref[...] = _dot_bf16(wgt_ref[...], hb, NT)


def _inproj(x2, mod, norm_g, w_r, w_m, w_g, w_gt):
    rows = x2.shape[0]
    tiles_per_mod = rows // mod.shape[0] // ROW_TILE
    return pl.pallas_call(
        _inproj_kernel,
        grid=(rows // ROW_TILE,),
        in_specs=[pl.BlockSpec((ROW_TILE, D_MODEL), lambda i: (i, 0)),
                  pl.BlockSpec((1, 1, 6 * D_MODEL), lambda i: (i // tiles_per_mod, 0, 0)),
                  _const_spec((1, D_MODEL)),
                  _const_spec(w_r.shape), _const_spec(w_m.shape), _const_spec(w_g.shape),
                  _const_spec(w_gt.shape)],
        out_specs=[pl.BlockSpec((ROW_TILE, RWKV_COLS), lambda i: (i, 0)),
                   pl.BlockSpec((ROW_TILE, MLSTM_MAIN), lambda i: (i, 0)),
                   pl.BlockSpec((ROW_TILE, N_GATES), lambda i: (i, 0)),
                   pl.BlockSpec((N_GATES, ROW_TILE), lambda i: (0, i))],
        out_shape=[jax.ShapeDtypeStruct((rows, RWKV_COLS), F32),
                   jax.ShapeDtypeStruct((rows, MLSTM_MAIN), F32),
                   jax.ShapeDtypeStruct((rows, N_GATES), F32),
                   jax.ShapeDtypeStruct((N_GATES, rows), F32)],
        compiler_params=_params("arbitrary"),
        name="inproj",
    )(x2, mod, norm_g.reshape(1, D_MODEL), w_r, w_m, w_g, w_gt)


def _group_sum(x, ones_bd):
    hi = x.astype(BF16)
    lo = (x - hi.astype(F32)).astype(BF16)
    return (lax.dot_general(hi, ones_bd, NN, preferred_element_type=F32)
            + lax.dot_general(lo, ones_bd, NN, preferred_element_type=F32))


def _block_diag(x, ones_bd):
    return jnp.concatenate([x.astype(BF16)] * PACK, axis=0) * ones_bd


def _rwkv_kernel(zr_ref, s0_ref, mu_ref, kkw_ref, ka_ref, rk_ref, gnw_ref, gnb_ref, w0_ref, w2_ref,
                 a0_ref, a2_ref, g2_ref, ones_ref, y_ref, sout_ref,
                 r_s, k_s, v_s, kk_s, gate_s, lx_s, st_s, *, seq_len, grid):
    nc = seq_len // CHUNK
    ones_bd = ones_ref[...]

    def mix_chunk(c, carry):
        r0 = pl.multiple_of(c * CHUNK, CHUNK)
        if grid:
            zc = zr_ref[0, pl.ds(r0, CHUNK), :]
            row = lax.broadcasted_iota(jnp.int32, zc.shape, 0)
            up0 = pl.multiple_of(jnp.maximum(c - 1, 0) * CHUNK, CHUNK)
            dn0 = pl.multiple_of(jnp.minimum(c + 1, nc - 1) * CHUNK, CHUNK)
            up = jnp.where(c > 0, zr_ref[0, pl.ds(up0, CHUNK), :], 0.0)
            down = jnp.where(c < nc - 1, zr_ref[0, pl.ds(dn0, CHUNK), :], 0.0)
            left = jnp.where(row == 0, 0.0, pltpu.roll(zc, 1, 0))
            right = jnp.where(row == CHUNK - 1, 0.0, pltpu.roll(zc, CHUNK - 1, 0))
            local = 0.25 * (up + down + left + right)
        else:
            zc, prev, nxt = _seq_neighbours(zr_ref, 0, c, nc, slice(None))
            local = 0.5 * (prev + nxt)
        z = zc + (local - zc) * mu_ref[...]
        r, k, v = z[:, 0:DA], z[:, DA:2 * DA], z[:, 2 * DA:3 * DA]
        lx = z[:, 3 * DA:3 * DA + DECAY_LORA + AAA_LORA]
        xg = z[:, 3 * DA + DECAY_LORA + AAA_LORA:]
        kk = k * kkw_ref[...]
        sq = kk * kk
        ss = jnp.concatenate([_group_sum(sq[:, p * PACK_W:(p + 1) * PACK_W], ones_bd) for p in range(N_PACKS)],
                             axis=1)
        kk = kk / jnp.maximum(jnp.sqrt(ss), 1e-12)
        lane = lax.broadcasted_iota(jnp.int32, lx.shape, 1)
        r_s[pl.ds(r0, CHUNK), :] = r
        k_s[pl.ds(r0, CHUNK), :] = k
        v_s[pl.ds(r0, CHUNK), :] = v
        kk_s[pl.ds(r0, CHUNK), :] = kk
        gate_s[pl.ds(r0, CHUNK), :] = _dot_bf16(_sigmoid(xg), g2_ref[...])
        lx_s[pl.ds(r0, CHUNK), :] = jnp.where(lane < DECAY_LORA, jnp.tanh(lx), lx)
        return carry

    lax.fori_loop(0, nc, mix_chunk, 0)

    rowblk = lax.broadcasted_iota(jnp.int32, (PACK_W, PACK_W), 0) // HEAD_A
    colblk = lax.broadcasted_iota(jnp.int32, (PACK_W, PACK_W), 1) // HEAD_A
    for d in range(2):
        for p in range(N_PACKS):
            rows_ = jnp.concatenate([s0_ref[0, d, p * PACK + h] for h in range(PACK)], axis=0)
            st_s[d, p] = jnp.where(rowblk == colblk, jnp.concatenate([rows_] * PACK, axis=1), 0.0)

    t_idx = lax.broadcasted_iota(jnp.int32, (CHUNK, PACK_W), 0)
    s_idx = lax.broadcasted_iota(jnp.int32, (CHUNK, PACK_W), 1) % CHUNK

    def scan_step(i, carry):
        first_touch = i < nc // 2
        dot = functools.partial(lax.dot_general, preferred_element_type=F32)
        chains, row_sel = [], []
        for d in range(2):
            reverse = d == 1
            c = nc - 1 - i if reverse else i
            rows = pl.ds(pl.multiple_of(c * CHUNK, CHUNK), CHUNK)
            row_sel.append(rows)
            strict = (s_idx > t_idx) if reverse else (s_idx < t_idx)
            incl = (s_idx >= t_idx) if reverse else (s_idx <= t_idx)
            last = 0 if reverse else CHUNK - 1
            r, k, v, kk = r_s[rows, :], k_s[rows, :], v_s[rows, :], kk_s[rows, :]
            lx = lx_s[rows, :]
            wl = w0_ref[d:d + 1, :] + _dot_bf16(lx[:, 0:DECAY_LORA], w2_ref[d])
            logw = -jnp.exp(_log_sigmoid(wl) - 0.5)
            a = _sigmoid(a0_ref[d:d + 1, :] + _dot_bf16(lx[:, DECAY_LORA:], a2_ref[d]))
            kd = k * (1.0 + (a - 1.0) * ka_ref[...])
            kb = kk * a
            tri = _tri(CHUNK, reverse, False).astype(BF16)
            w1, w_rest = _split_bf16(logw)
            w2, w3 = _split_bf16(w_rest)
            cum = dot(tri, w1, NN) + dot(tri, w2, NN) + dot(tri, w3, NN)
            clast = cum[last:last + 1]
            rt = r * jnp.exp(cum)
            at = -kk * jnp.exp(cum - logw)
            einv = jnp.exp(-cum)
            kt, bt = kd * einv, kb * einv
            edec = jnp.exp(clast - cum)
            kp, bp = kd * edec, kb * edec
            ptot = jnp.exp(clast)
            rkd = r * kd * rk_ref[...]
            for p in range(N_PACKS):
                cs = slice(p * PACK_W, (p + 1) * PACK_W)
                chains.append(dict(
                    d=d, p=p, cs=cs, strict=strict, incl=incl, v=v[:, cs], rkd=rkd[:, cs], ptot=ptot[:, cs],
                    ar=jnp.concatenate([at[:, cs], rt[:, cs]], axis=0).astype(BF16),
                    kt_bd=_block_diag(kt[:, cs], ones_bd), bt_bd=_block_diag(bt[:, cs], ones_bd),
                    v_bd=_block_diag(v[:, cs], ones_bd),
                    kbp=jnp.concatenate([kp[:, cs], bp[:, cs]], axis=0)))

        for ch in chains:
            ch['s_prev'] = st_s[ch['d'], ch['p']]
            ch['a_k'] = dot(ch['ar'], ch['kt_bd'], NT)
            ch['a_b'] = dot(ch['ar'], ch['bt_bd'], NT)
            ch['ars'] = dot(ch['ar'], ch['s_prev'].astype(BF16), NT)
        for ch in chains:
            a_ak = jnp.where(ch['strict'], ch['a_k'][:CHUNK], 0.0)
            a_rk = jnp.where(ch['incl'], ch['a_k'][CHUNK:], 0.0)
            ch['a_rb'] = jnp.where(ch['incl'], ch['a_b'][CHUNK:], 0.0).astype(BF16)
            n = jnp.where(ch['strict'], ch['a_b'][:CHUNK], 0.0)
            ch['n_hi'], n_lo = _split_bf16(n)
            ch['n_both'] = jnp.concatenate([ch['n_hi'], n_lo], axis=0)
            ch['pw'] = ch['n_hi']
            ch['m'] = jnp.where(s_idx == t_idx, 1.0, 0.0) + n
            akv = dot(jnp.concatenate([a_ak, a_rk], axis=0).astype(BF16), ch['v_bd'], NN)
            ch['rhs'] = ch['ars'][:CHUNK] + akv[:CHUNK]
            ch['y'] = ch['ars'][CHUNK:] + akv[CHUNK:]
        for ch in chains:
            ch['pw'] = dot(ch['pw'], _block_diag(ch['pw'], ones_bd), NN).astype(BF16)
        for j in range(1, 5):
            for ch in chains:
                pw_bd = _block_diag(ch['pw'], ones_bd)
                if j < 4:
                    both = dot(jnp.concatenate([ch['m'].astype(BF16), ch['pw']], axis=0), pw_bd, NN)
                    ch['m'] = ch['m'] + both[:CHUNK]
                    ch['pw'] = both[CHUNK:].astype(BF16)
                else:
                    ch['m'] = ch['m'] + dot(ch['m'].astype(BF16), pw_bd, NN)
        for ch in chains:
            ch['mb'] = ch['m'].astype(BF16)
            ch['u'] = dot(ch['mb'], _block_diag(ch['rhs'], ones_bd), NN)
        for ch in chains:
            u_hi, u_lo = _split_bf16(ch['u'])
            both = dot(ch['n_both'], _block_diag(u_hi, ones_bd), NN)
            nu = both[:CHUNK] + both[CHUNK:] + dot(ch['n_hi'], _block_diag(u_lo, ones_bd), NN)
            ch['resid'] = ch['rhs'] - ch['u'] + nu
        for ch in chains:
            ch['u'] = ch['u'] + dot(ch['mb'], _block_diag(ch['resid'], ones_bd), NN)
        for ch in chains:
            u = ch['u']
            ch['y'] = ch['y'] + dot(ch['a_rb'], _block_diag(u, ones_bd), NN)
            vu_hi, vu_lo = _split_bf16(jnp.concatenate([ch['v'], u], axis=0))
            kbp_hi, kbp_lo = _split_bf16(ch['kbp'])
            grown = dot(vu_hi, kbp_hi, TN) + dot(vu_hi, kbp_lo, TN) + dot(vu_lo, kbp_hi, TN)
            st_s[ch['d'], ch['p']] = ch['s_prev'] * ch['ptot'] + grown * ones_bd.astype(F32)
        n_ch = len(chains)
        sums = dot(jnp.concatenate([ch['y'] for ch in chains] + [ch['rkd'] for ch in chains],
                                   axis=0).astype(BF16), ones_bd, NN)
        ycs = [ch['y'] - sums[q * CHUNK:(q + 1) * CHUNK] * (1.0 / HEAD_A) for q, ch in enumerate(chains)]
        var = dot(jnp.concatenate([yc * yc for yc in ycs], axis=0).astype(BF16), ones_bd, NN) * (1.0 / HEAD_A)
        outs = [[], []]
        for q, ch in enumerate(chains):
            cs = ch['cs']
            out = ycs[q] * lax.rsqrt(var[q * CHUNK:(q + 1) * CHUNK] + GN_EPS) * gnw_ref[:, cs] + gnb_ref[:, cs]
            outs[ch['d']].append(out + sums[(n_ch + q) * CHUNK:(n_ch + q + 1) * CHUNK] * ch['v'])
        for d in range(2):
            rows = row_sel[d]
            out = jnp.concatenate(outs[d], axis=1)

            @pl.when(first_touch)
            def _(rows=rows, out=out):
                y_ref[0, rows, :] = out

            @pl.when(jnp.logical_not(first_touch))
            def _(rows=rows, out=out):
                y_ref[0, rows, :] = (y_ref[0, rows, :] + out) * gate_s[rows, :]
        return carry

    lax.fori_loop(0, nc, scan_step, 0)

    for d in range(2):
        for p in range(N_PACKS):
            s_fin = st_s[d, p]
            for h in range(PACK):
                sout_ref[0, d, p * PACK + h] = s_fin[h * HEAD_A:(h + 1) * HEAD_A, h * HEAD_A:(h + 1) * HEAD_A]


def _rwkv(zr, s0, p, grid):
    bsz, seq_len, _ = zr.shape
    assert (seq_len // CHUNK) % 2 == 0
    kern = functools.partial(_rwkv_kernel, seq_len=seq_len, grid=grid)
    consts = [p['mu'], p['kkw'], p['ka'], p['rk'], p['gnw'], p['gnb'], p['w0'], p['w2'], p['a0'], p['a2'],
              p['g2'], p['ones_bd']]
    return pl.pallas_call(
        kern,
        grid=(bsz,),
        in_specs=[pl.BlockSpec((1, seq_len, RWKV_COLS), lambda b: (b, 0, 0)),
                  pl.BlockSpec((1, 2, H_A, HEAD_A, HEAD_A), lambda b: (b, 0, 0, 0, 0))]
                 + [_const_spec(a.shape) for a in consts],
        out_specs=[pl.BlockSpec((1, seq_len, DA), lambda b: (b, 0, 0)),
                   pl.BlockSpec((1, 2, H_A, HEAD_A, HEAD_A), lambda b: (b, 0, 0, 0, 0))],
        out_shape=[jax.ShapeDtypeStruct((bsz, seq_len, DA), F32),
                   jax.ShapeDtypeStruct((bsz, 2, H_A, HEAD_A, HEAD_A), F32)],
        scratch_shapes=[pltpu.VMEM((seq_len, DA), F32)] * 5
                       + [pltpu.VMEM((seq_len, DECAY_LORA + AAA_LORA), F32),
                          pltpu.VMEM((2, N_PACKS, PACK_W, PACK_W), F32)],
        compiler_params=_params("arbitrary"),
        name="rwkv_scan",
    )(zr, s0, *consts)


def _mlstm_kernel(zm_ref, g_ref, gt_ref, c0_ref, n0_ref, m0_ref, conv_ref, bi_ref, bit_ref, bf_ref, bft_ref,
                  ng_ref, erep_ref, y_ref, cout_ref, nout_ref, mout_ref,
                  qk_s, h_s, c_s, n_s, m_s, *, seq_len):
    nc = seq_len // CHUNK

    def conv_chunk(c, carry):
        zc, prev, nxt = _seq_neighbours(zm_ref, 0, c, nc, slice(0, 2 * DB))
        u = conv_ref[0:1, :] * prev + conv_ref[1:2, :] * zc + conv_ref[2:3, :] * nxt
        qk = u * _sigmoid(u)
        lane = lax.broadcasted_iota(jnp.int32, qk.shape, 1)
        qk_s[pl.ds(pl.multiple_of(c * CHUNK, CHUNK), CHUNK), :] = jnp.where(lane >= DB, qk * (HEAD_B ** -0.5), qk)
        return carry

    lax.fori_loop(0, nc, conv_chunk, 0)

    for d in range(2):
        c_s[d] = c0_ref[0, d]
        n_s[d] = n0_ref[0, d]
        m_s[d] = jnp.broadcast_to(m0_ref[0, d], (H_B, LANES))

    e_rep = erep_ref[...]
    ones_tl = jnp.ones((CHUNK, LANES), BF16)
    row_id = lax.broadcasted_iota(jnp.int32, (CHUNK, LANES), 0)

    def scan_max(x, reverse):
        shift = 1
        while shift < CHUNK:
            if reverse:
                moved = jnp.where(row_id < CHUNK - shift, pltpu.roll(x, CHUNK - shift, 0), -jnp.inf)
            else:
                moved = jnp.where(row_id >= shift, pltpu.roll(x, shift, 0), -jnp.inf)
            x = jnp.maximum(x, moved)
            shift *= 2
        return x

    def dot3(a, b, dims):
        a_hi, a_lo = _split_bf16(a)
        b_hi, b_lo = _split_bf16(b)
        dg = functools.partial(lax.dot_general, dimension_numbers=dims, preferred_element_type=F32)
        return dg(a_hi, b_hi) + dg(a_hi, b_lo) + dg(a_lo, b_hi)

    def scan_step(i, carry):
        dot = functools.partial(lax.dot_general, preferred_element_type=F32)
        chains = []
        for d in range(2):
            reverse = d == 1
            incl = _tri(CHUNK, reverse, False)
            tri_b = incl.astype(BF16)
            tri_row = _tri(CHUNK, not reverse, False).astype(F32)
            last = 0 if reverse else CHUNK - 1
            gsl = slice(d * H_B, (d + 1) * H_B)
            fsl = slice(2 * H_B + d * H_B, 2 * H_B + (d + 1) * H_B)
            c = nc - 1 - i if reverse else i
            rows = pl.ds(pl.multiple_of(c * CHUNK, CHUNK), CHUNK)
            g1, g_rest = _split_bf16(g_ref[0, rows, :])
            g2, g3 = _split_bf16(g_rest)
            g_rep = dot(g1, e_rep, NN) + dot(g2, e_rep, NN) + dot(g3, e_rep, NN)
            grow = gt_ref[0, c]
            i_row = grow[gsl, :] + bit_ref[gsl, :]
            f_row = _log_sigmoid(grow[fsl, :] + bft_ref[gsl, :])
            b_row = _mm(f_row, tri_row)
            for h in range(H_B):
                ji, jf = d * H_B + h, 2 * H_B + d * H_B + h
                ic = g_rep[:, ji * LANES:(ji + 1) * LANES] + bi_ref[:, ji:ji + 1]
                fc = _log_sigmoid(g_rep[:, jf * LANES:(jf + 1) * LANES] + bf_ref[:, ji:ji + 1])
                f1, f_rest = _split_bf16(fc)
                f2, f3 = _split_bf16(f_rest)
                bc = dot(tri_b, f1, NN) + dot(tri_b, f2, NN) + dot(tri_b, f3, NN)
                q = qk_s[rows, h * HEAD_B:(h + 1) * HEAD_B]
                chains.append(dict(
                    d=d, h=h, rows=rows, incl=incl, reverse=reverse, q=q, qb=q.astype(BF16),
                    k=qk_s[rows, DB + h * HEAD_B:DB + (h + 1) * HEAD_B],
                    v=zm_ref[0, rows, 2 * DB + h * HEAD_B:2 * DB + (h + 1) * HEAD_B],
                    bc=bc, ic=ic, bl=bc[last:last + 1, :], br=b_row[h:h + 1, :], ir=i_row[h:h + 1, :]))
        for ch in chains:
            d, h = ch['d'], ch['h']
            ch['qk'] = _dot_bf16(ch['qb'], ch['k'], NT)
            ch['c_prev'] = c_s[d, h]
            ch['qc'] = _dot_bf16(ch['qb'], ch['c_prev'])
            ch['n_prev'] = n_s[d, h:h + 1, :]
            ch['qn'] = dot3(ch['q'], jnp.broadcast_to(ch['n_prev'], (HEAD_B, HEAD_B)), NT)
        for ch in chains:
            d, h = ch['d'], ch['h']
            bc, ic, bl = ch['bc'], ch['ic'], ch['bl']
            m_prev = m_s[d, h:h + 1, :]
            m_t = bc + jnp.maximum(m_prev, scan_max(ic - bc, ch['reverse']))
            dmat = jnp.where(ch['incl'], bc[:, :CHUNK] - ch['br'] + ch['ir'], -jnp.inf)
            s = ch['qk'] * jnp.exp(dmat - m_t[:, :CHUNK])
            carry_w = jnp.exp(bc + m_prev - m_t)
            s_hi, s_lo = _split_bf16(s)
            den = dot(s_hi, ones_tl, NN) + dot(s_lo, ones_tl, NN) + carry_w * ch['qn']
            num = dot(s_hi, ch['v'].astype(BF16), NN) + carry_w * ch['qc']
            h_s[d, ch['rows'], h * HEAD_B:(h + 1) * HEAD_B] = num / jnp.maximum(jnp.abs(den), jnp.exp(-m_t))
            g = bl - bc + ic
            m_new = jnp.maximum(bl + m_prev, jnp.max(g, axis=0, keepdims=True))
            wk = jnp.exp(g - m_new)
            decay = jnp.exp(bl + m_prev - m_new)
            c_s[d, h] = decay * ch['c_prev'] + _dot_bf16(ch['k'], wk * ch['v'], TN)
            n_s[d, h:h + 1, :] = decay * ch['n_prev'] + jnp.sum(wk * ch['k'], axis=0, keepdims=True)
            m_s[d, h:h + 1, :] = m_new
        return carry

    lax.fori_loop(0, nc, scan_step, 0)
    cout_ref[0] = c_s[...]
    nout_ref[0] = n_s[...]
    mout_ref[0] = m_s[...]

    def norm_chunk(c, carry):
        rows = pl.ds(pl.multiple_of(c * CHUNK, CHUNK), CHUNK)
        og = zm_ref[0, rows, 3 * DB:4 * DB]
        hm = h_s[0, rows, :] + h_s[1, rows, :]
        parts = []
        for h in range(H_B):
            hh = hm[:, h * HEAD_B:(h + 1) * HEAD_B]
            parts.append(hh * lax.rsqrt(jnp.mean(hh * hh, axis=-1, keepdims=True) + NORM_EPS))
        y_ref[0, rows, :] = jnp.concatenate(parts, axis=1) * ng_ref[...] * _sigmoid(og)
        return carry

    lax.fori_loop(0, nc, norm_chunk, 0)


def _mlstm(zm, g, gt, c0, n0, m0, p):
    bsz, seq_len, _ = zm.shape
    nc = seq_len // CHUNK
    kern = functools.partial(_mlstm_kernel, seq_len=seq_len)
    consts = [p['conv'], p['bi'], p['bit'], p['bf'], p['bft'], p['ng'], p['e_rep']]
    return pl.pallas_call(
        kern,
        grid=(bsz,),
        in_specs=[pl.BlockSpec((1, seq_len, MLSTM_MAIN), lambda b: (b, 0, 0)),
                  pl.BlockSpec((1, seq_len, N_GATES), lambda b: (b, 0, 0)),
                  pl.BlockSpec((1, nc, N_GATES, CHUNK), lambda b: (b, 0, 0, 0)),
                  pl.BlockSpec((1, 2, H_B, HEAD_B, HEAD_B), lambda b: (b, 0, 0, 0, 0)),
                  pl.BlockSpec((1, 2, H_B, HEAD_B), lambda b: (b, 0, 0, 0)),
                  pl.BlockSpec((1, 2, H_B, 1), lambda b: (b, 0, 0, 0))]
                 + [_const_spec(a.shape) for a in consts],
        out_specs=[pl.BlockSpec((1, seq_len, DB), lambda b: (b, 0, 0)),
                   pl.BlockSpec((1, 2, H_B, HEAD_B, HEAD_B), lambda b: (b, 0, 0, 0, 0)),
                   pl.BlockSpec((1, 2, H_B, HEAD_B), lambda b: (b, 0, 0, 0)),
                   pl.BlockSpec((1, 2, H_B, LANES), lambda b: (b, 0, 0, 0))],
        out_shape=[jax.ShapeDtypeStruct((bsz, seq_len, DB), F32),
                   jax.ShapeDtypeStruct((bsz, 2, H_B, HEAD_B, HEAD_B), F32),
                   jax.ShapeDtypeStruct((bsz, 2, H_B, HEAD_B), F32),
                   jax.ShapeDtypeStruct((bsz, 2, H_B, LANES), F32)],
        scratch_shapes=[pltpu.VMEM((seq_len, 2 * DB), F32), pltpu.VMEM((2, seq_len, DB), F32),
                        pltpu.VMEM((2, H_B, HEAD_B, HEAD_B), F32), pltpu.VMEM((2, H_B, HEAD_B), F32),
                        pltpu.VMEM((2, H_B, LANES), F32)],
        compiler_params=_params("arbitrary"),
        name="mlstm_scan",
    )(zm, g, gt, c0, n0, m0, *consts)


def _outproj_kernel(x_ref, yr_ref, ym_ref, mod_ref, g_ref, wo_ref, rw_ref, rb_ref,
                    x1_ref, hn_ref, idx_ref, gate_ref):
    mod = mod_ref[0]
    g1 = mod[:, 2 * D_MODEL:3 * D_MODEL]
    sh2, sc2 = mod[:, 3 * D_MODEL:4 * D_MODEL], mod[:, 4 * D_MODEL:5 * D_MODEL]
    mix = _dot_bf16(yr_ref[...], wo_ref[0:DA, :]) + _dot_bf16(ym_ref[...], wo_ref[DA:, :])
    x1 = x_ref[...] + g1 * mix
    x1_ref[...] = x1
    y = x1 * lax.rsqrt(jnp.mean(x1 * x1, axis=-1, keepdims=True) + NORM_EPS)
    hn = y * g_ref[...] * (1.0 + sc2) + sh2
    hn_ref[...] = hn
    logits = _dot_bf16(hn, rw_ref[...]) + rb_ref[...]
    lane = lax.broadcasted_iota(jnp.int32, logits.shape, 1)
    vals, idxs = [], []
    for _ in range(TOP_K):
        top = jnp.max(logits, axis=-1, keepdims=True)
        pick = jnp.min(jnp.where(logits == top, lane, N_EXPERTS), axis=-1, keepdims=True)
        vals.append(top)
        idxs.append(pick)
        logits = jnp.where(lane == pick, -jnp.inf, logits)
    exps = [jnp.exp(v - vals[0]) for v in vals]
    total = exps[0] + exps[1] + exps[2] + exps[3]
    k_lane = lax.broadcasted_iota(jnp.int32, (x1.shape[0], TOP_K), 1)
    gates = jnp.zeros((x1.shape[0], TOP_K), F32)
    picks = jnp.zeros((x1.shape[0], TOP_K), jnp.int32)
    for j in range(TOP_K):
        gates = jnp.where(k_lane == j, exps[j] / total, gates)
        picks = jnp.where(k_lane == j, idxs[j], picks)
    idx_ref[...] = picks
    gate_ref[...] = gates


def _outproj(x2, yr, ym, mod, norm_g, w_out, router_w, router_b):
    rows = x2.shape[0]
    tiles_per_mod = rows // mod.shape[0] // ROW_TILE
    row_spec = lambda w: pl.BlockSpec((ROW_TILE, w), lambda i: (i, 0))
    return pl.pallas_call(
        _outproj_kernel,
        grid=(rows // ROW_TILE,),
        in_specs=[row_spec(D_MODEL), row_spec(DA), row_spec(DB),
                  pl.BlockSpec((1, 1, 6 * D_MODEL), lambda i: (i // tiles_per_mod, 0, 0)),
                  _const_spec((1, D_MODEL)), _const_spec(w_out.shape), _const_spec(router_w.shape),
                  _const_spec((1, N_EXPERTS))],
        out_specs=[row_spec(D_MODEL), row_spec(D_MODEL), row_spec(TOP_K), row_spec(TOP_K)],
        out_shape=[jax.ShapeDtypeStruct((rows, D_MODEL), F32), jax.ShapeDtypeStruct((rows, D_MODEL), F32),
                   jax.ShapeDtypeStruct((rows, TOP_K), jnp.int32), jax.ShapeDtypeStruct((rows, TOP_K), F32)],
        compiler_params=_params("arbitrary"),
        name="outproj_router",
    )(x2, yr, ym, mod, norm_g.reshape(1, D_MODEL), w_out, router_w, router_b.reshape(1, N_EXPERTS))


def _rank_kernel(idx_ref, rank_ref, count_ref, run_s):
    i = pl.program_id(0)

    @pl.when(i == 0)
    def _():
        run_s[...] = jnp.zeros_like(run_s)

    idx = idx_ref[...]
    rows = idx.shape[0]
    lane = lax.broadcasted_iota(jnp.int32, (rows, N_EXPERTS), 1)
    hot = jnp.zeros((rows, N_EXPERTS), F32)
    for j in range(TOP_K):
        hot = hot + jnp.where(lane == idx[:, j:j + 1], 1.0, 0.0)
    before = _dot_bf16(_tri(rows, False, True).astype(F32), hot) + run_s[...]
    k_lane = lax.broadcasted_iota(jnp.int32, (rows, TOP_K), 1)
    rank = jnp.zeros((rows, TOP_K), F32)
    for j in range(TOP_K):
        rj = jnp.sum(jnp.where(lane == idx[:, j:j + 1], before, 0.0), axis=-1, keepdims=True)
        rank = jnp.where(k_lane == j, rj, rank)
    rank_ref[...] = rank.astype(jnp.int32)
    run_s[...] = run_s[...] + jnp.sum(hot, axis=0, keepdims=True)
    count_ref[...] = run_s[...].astype(jnp.int32)


def _expert_ranks(idx):
    rows = idx.shape[0]
    return pl.pallas_call(
        _rank_kernel,
        grid=(rows // ROW_TILE,),
        in_specs=[pl.BlockSpec((ROW_TILE, TOP_K), lambda i: (i, 0))],
        out_specs=[pl.BlockSpec((ROW_TILE, TOP_K), lambda i: (i, 0)), _const_spec((1, N_EXPERTS))],
        out_shape=[jax.ShapeDtypeStruct((rows, TOP_K), jnp.int32),
                   jax.ShapeDtypeStruct((1, N_EXPERTS), jnp.int32)],
        scratch_shapes=[pltpu.VMEM((1, N_EXPERTS), F32)],
        compiler_params=_params("arbitrary"),
        name="expert_ranks",
    )(idx)


def _dispatch_kernel(dest_ref, pend_ref, x_ref, xs_ref, zero_s, sem):
    @pl.when(pl.program_id(0) == 0)
    def _():
        zero_s[...] = jnp.zeros_like(zero_s)

        def zero_block(start):
            return pltpu.make_async_copy(zero_s, xs_ref.at[pl.ds(pl.multiple_of(start, MOE_BLOCK), MOE_BLOCK)], sem)

        def has_rows(e):
            return pend_ref[e] > (pend_ref[e - 1] if e else 0)

        used_end = pend_ref[N_EXPERTS - 1]
        n_tail = (xs_ref.shape[0] - used_end) // MOE_BLOCK

        def start_tail(b, carry):
            zero_block(used_end + b * MOE_BLOCK).start()
            return carry

        def wait_tail(b, carry):
            zero_block(0).wait()
            return carry

        for e in range(N_EXPERTS):
            @pl.when(has_rows(e))
            def _(e=e):
                zero_block(pend_ref[e] - MOE_BLOCK).start()
        lax.fori_loop(0, n_tail, start_tail, 0)
        for e in range(N_EXPERTS):
            @pl.when(has_rows(e))
            def _():
                zero_block(0).wait()
        lax.fori_loop(0, n_tail, wait_tail, 0)

    def issue(g, carry):
        for q in range(DMA_UNROLL):
            r = g * DMA_UNROLL + q
            for j in range(TOP_K):
                slot = dest_ref[r * TOP_K + j]
                pltpu.make_async_copy(x_ref.at[pl.ds(r, 1)], xs_ref.at[pl.ds(slot, 1)], sem).start()
        return carry

    lax.fori_loop(0, ROW_TILE // DMA_UNROLL, issue, 0)
    all_rows = xs_ref.at[pl.ds(0, ROW_TILE * TOP_K)]
    pltpu.make_async_copy(all_rows, all_rows, sem).wait()


def _dispatch(dest_flat, pad_end, x2, n_slots):
    rows = x2.shape[0]
    return pl.pallas_call(
        _dispatch_kernel,
        grid=(rows // ROW_TILE,),
        in_specs=[pl.BlockSpec((ROW_TILE * TOP_K,), lambda i: (i,), memory_space=pltpu.SMEM),
                  pl.BlockSpec(memory_space=pltpu.SMEM),
                  pl.BlockSpec((ROW_TILE, D_MODEL), lambda i: (i, 0))],
        out_specs=pl.BlockSpec(memory_space=pl.ANY),
        out_shape=jax.ShapeDtypeStruct((n_slots, D_MODEL), x2.dtype),
        scratch_shapes=[pltpu.VMEM((MOE_BLOCK, D_MODEL), F32), pltpu.SemaphoreType.DMA(())],
        compiler_params=_params("arbitrary", disable_bounds_checks=True),
        name="moe_dispatch",
    )(dest_flat, pad_end, x2)


def _expert_kernel(be_ref, nb_ref, xs_ref, w1_ref, b1_ref, w2_ref, b2_ref, y_ref, w1_s, w2_s):
    i = pl.program_id(0)
    changed = jnp.logical_or(i == 0, be_ref[i] != be_ref[jnp.maximum(i - 1, 0)])

    @pl.when(jnp.logical_and(changed, i < nb_ref[0]))
    def _():
        w1_s[...] = w1_ref[0].astype(BF16)
        w2_s[...] = w2_ref[0].astype(BF16)

    @pl.when(i < nb_ref[0])
    def _():
        xb = xs_ref[...].astype(BF16)
        hu = lax.dot_general(xb, w1_s[...], NN, preferred_element_type=F32) + b1_ref[0]
        glu = jnp.minimum(hu[:, :D_FF], SWIGLU_LIMIT)
        lin = jnp.clip(hu[:, D_FF:], -SWIGLU_LIMIT, SWIGLU_LIMIT)
        act = glu * _sigmoid(SWIGLU_ALPHA * glu) * (lin + 1.0)
        y_ref[...] = lax.dot_general(act.astype(BF16), w2_s[...], NN, preferred_element_type=F32) + b2_ref[0]

    @pl.when(i >= nb_ref[0])
    def _():
        y_ref[...] = jnp.zeros_like(y_ref)


def _experts(block_e, n_used, xs, w1, b1, w2, b2):
    n_blocks = xs.shape[0] // MOE_BLOCK
    grid_spec = pltpu.PrefetchScalarGridSpec(
        num_scalar_prefetch=2,
        grid=(n_blocks,),
        in_specs=[pl.BlockSpec((MOE_BLOCK, D_MODEL), lambda i, be, nb: (jnp.minimum(i, nb[0] - 1), 0)),
                  pl.BlockSpec((1, D_MODEL, 2 * D_FF), lambda i, be, nb: (be[i], 0, 0)),
                  pl.BlockSpec((1, 1, 2 * D_FF), lambda i, be, nb: (be[i], 0, 0)),
                  pl.BlockSpec((1, D_FF, D_MODEL), lambda i, be, nb: (be[i], 0, 0)),
                  pl.BlockSpec((1, 1, D_MODEL), lambda i, be, nb: (be[i], 0, 0))],
        out_specs=pl.BlockSpec((MOE_BLOCK, D_MODEL), lambda i, be, nb: (i, 0)),
        scratch_shapes=[pltpu.VMEM((D_MODEL, 2 * D_FF), BF16), pltpu.VMEM((D_FF, D_MODEL), BF16)],
    )
    return pl.pallas_call(
        _expert_kernel,
        grid_spec=grid_spec,
        out_shape=jax.ShapeDtypeStruct(xs.shape, F32),
        compiler_params=_params("arbitrary"),
        name="moe_experts",
    )(block_e, n_used, xs, w1, b1.reshape(N_EXPERTS, 1, 2 * D_FF), w2, b2.reshape(N_EXPERTS, 1, D_MODEL))


def _combine_kernel(dest_ref, yb_ref, x1_ref, gate_ref, mod_ref, fg_ref, o_ref, buf, sem):
    def issue(g, carry):
        for q in range(DMA_UNROLL):
            r = g * DMA_UNROLL + q
            for j in range(TOP_K):
                slot = dest_ref[r * TOP_K + j]
                pltpu.make_async_copy(yb_ref.at[pl.ds(slot, 1)], buf.at[j, pl.ds(r, 1)], sem).start()
        return carry

    lax.fori_loop(0, ROW_TILE // DMA_UNROLL, issue, 0)
    pltpu.make_async_copy(buf, buf, sem).wait()

    g2 = mod_ref[0][:, 5 * D_MODEL:6 * D_MODEL]
    gates = gate_ref[...]
    moe = jnp.zeros((ROW_TILE, D_MODEL), F32)
    for j in range(TOP_K):
        moe = moe + buf[j] * gates[:, j:j + 1]
    x2 = x1_ref[...] + g2 * moe
    y = x2 * lax.rsqrt(jnp.mean(x2 * x2, axis=-1, keepdims=True) + NORM_EPS)
    o_ref[...] = y * fg_ref[...]


def _combine(dest_flat, yb, x1, gates, mod, final_g, row_offset):
    rows = x1.shape[0]
    tiles_per_mod = rows // mod.shape[0] // ROW_TILE
    tile_offset = row_offset // ROW_TILE
    row_spec = pl.BlockSpec((ROW_TILE, D_MODEL), lambda i: (i, 0))
    return pl.pallas_call(
        _combine_kernel,
        grid=(rows // ROW_TILE,),
        in_specs=[pl.BlockSpec((ROW_TILE * TOP_K,), lambda i: (i + tile_offset,), memory_space=pltpu.SMEM),
                  pl.BlockSpec(memory_space=pl.ANY),
                  row_spec,
                  pl.BlockSpec((ROW_TILE, TOP_K), lambda i: (i, 0)),
                  pl.BlockSpec((1, 1, 6 * D_MODEL), lambda i: (i // tiles_per_mod, 0, 0)),
                  _const_spec((1, D_MODEL))],
        out_specs=row_spec,
        out_shape=jax.ShapeDtypeStruct((rows, D_MODEL), F32),
        scratch_shapes=[pltpu.VMEM((TOP_K, ROW_TILE, D_MODEL), F32), pltpu.SemaphoreType.DMA(())],
        compiler_params=_params("arbitrary", disable_bounds_checks=True),
        name="moe_combine",
    )(dest_flat, yb, x1, gates, mod, final_g.reshape(1, D_MODEL))


def _block_diag_ones(width, block):
    i = jnp.arange(width) // block
    return (i[:, None] == i[None, :]).astype(F32)


def _mixers(x, mod, lp, s0, c0, n0, m0, grid):
    bsz, seq_len, _ = x.shape
    nc = seq_len // CHUNK
    x2 = x.reshape(bsz * seq_len, D_MODEL)
    zr, zm, zg, zgt = _inproj(x2, mod, lp['norm1_g'], lp['w_r'], lp['w_m'], lp['w_g'], lp['w_gt'])
    gt = zgt.reshape(N_GATES, bsz, nc, CHUNK).transpose(1, 2, 0, 3)
    yr, s_new = _rwkv(zr.reshape(bsz, seq_len, RWKV_COLS), s0, lp, grid)
    ym, c_new, n_new, m_new = _mlstm(zm.reshape(bsz, seq_len, MLSTM_MAIN), zg.reshape(bsz, seq_len, N_GATES), gt,
                                     c0, n0, m0.reshape(bsz, 2, H_B, 1), lp)
    x1, hn, idx, gates = _outproj(x2, yr.reshape(-1, DA), ym.reshape(-1, DB), mod, lp['norm2_g'], lp['w_out'],
                                  lp['router_w'], lp['router_b'])
    return x1, hn, idx, gates, (s_new, c_new, n_new, m_new[..., 0])


def kernel(x_prompt, x_sample, state_rwkv, state_mlstm_C, state_mlstm_n, state_mlstm_m, c, c_ctx, ada_w, ada_b, norm1_g, norm2_g, w_in, w_out, rwkv_mu, rwkv_w0, rwkv_w2, rwkv_a0, rwkv_a2, rwkv_g2, rwkv_kk, rwkv_ka, rwkv_rk, rwkv_gn_w, rwkv_gn_b, mlstm_conv, mlstm_bi, mlstm_bf, mlstm_norm_g, router_w, router_b, moe_w1, moe_b1, moe_w2, moe_b2, final_g):
    bp, lp_len, _ = x_prompt.shape
    bs, ls_len, _ = x_sample.shape
    w = w_in[0]
    lp = {
        'norm1_g': norm1_g[0], 'norm2_g': norm2_g[0],
        'w_r': w[:, :RWKV_COLS].astype(BF16),
        'w_m': w[:, RWKV_COLS:RWKV_COLS + MLSTM_MAIN].astype(BF16),
        'w_g': w[:, RWKV_COLS + MLSTM_MAIN:],
        'w_gt': w[:, RWKV_COLS + MLSTM_MAIN:].T,
        'w_out': w_out[0].astype(BF16),
        'mu': rwkv_mu[0].reshape(1, RWKV_COLS), 'kkw': rwkv_kk[0].reshape(1, DA), 'ka': rwkv_ka[0].reshape(1, DA),
        'rk': rwkv_rk[0].reshape(1, DA), 'gnw': rwkv_gn_w[0].reshape(1, DA), 'gnb': rwkv_gn_b[0].reshape(1, DA),
        'w0': rwkv_w0[0], 'w2': rwkv_w2[0], 'a0': rwkv_a0[0], 'a2': rwkv_a2[0], 'g2': rwkv_g2[0],
        'ones_bd': _block_diag_ones(PACK_W, HEAD_A).astype(BF16),
        'conv': mlstm_conv[0], 'bi': mlstm_bi[0].reshape(1, 2 * H_B), 'bit': mlstm_bi[0].reshape(2 * H_B, 1),
        'bf': mlstm_bf[0].reshape(1, 2 * H_B), 'bft': mlstm_bf[0].reshape(2 * H_B, 1),
        'ng': mlstm_norm_g[0].reshape(1, DB),
        'e_rep': jnp.repeat(jnp.eye(N_GATES, dtype=BF16), LANES, axis=1),
        'router_w': router_w[0], 'router_b': router_b[0],
    }
    cc = jnp.concatenate([c_ctx[None, :], c, jnp.zeros((2 * SUBLANES - 1 - bs, D_MODEL), F32)], axis=0)
    mod = _ada_mod(cc, ada_w[0], ada_b[0])
    mod_p = mod[0:1].reshape(1, 1, 6 * D_MODEL)
    mod_s = mod[1:1 + bs].reshape(bs, 1, 6 * D_MODEL)

    zeros = lambda *shape: jnp.zeros(shape, F32)
    x1p, hnp, idxp, gatesp, st = _mixers(x_prompt, mod_p, lp, zeros(bp, 2, H_A, HEAD_A, HEAD_A),
                                         zeros(bp, 2, H_B, HEAD_B, HEAD_B), zeros(bp, 2, H_B, HEAD_B),
                                         zeros(bp, 2, H_B), False)
    x1s, hns, idxs, gatess, _ = _mixers(x_sample, mod_s, lp, state_rwkv[:, 0], state_mlstm_C[:, 0],
                                        state_mlstm_n[:, 0], state_mlstm_m[:, 0], True)

    n_p = bp * lp_len
    idx = jnp.concatenate([idxp, idxs], axis=0)
    hn = jnp.concatenate([hnp, hns], axis=0)
    n_tok = idx.shape[0]
    rank, counts = _expert_ranks(idx)
    counts = counts[0]
    padded = (counts + MOE_BLOCK - 1) // MOE_BLOCK * MOE_BLOCK
    pad_end = jnp.cumsum(padded)
    pad_start = pad_end - padded
    n_blocks = n_tok * TOP_K // MOE_BLOCK + N_EXPERTS
    block_start = jnp.arange(n_blocks, dtype=jnp.int32) * MOE_BLOCK
    block_e = jnp.minimum(jnp.sum(pad_end[None, :] <= block_start[:, None], axis=1), N_EXPERTS - 1).astype(jnp.int32)
    n_used = (pad_end[-1:] // MOE_BLOCK).astype(jnp.int32)
    dest = (pad_start[idx] + rank).astype(jnp.int32).reshape(-1)
    xs = _dispatch(dest, pad_end.astype(jnp.int32), hn, n_blocks * MOE_BLOCK)
    yb = _experts(block_e, n_used, xs, moe_w1[0], moe_b1[0], moe_w2[0], moe_b2[0])
    y_prompt = _combine(dest, yb, x1p, gatesp, mod_p, final_g, 0).reshape(x_prompt.shape)
    y_sample = _combine(dest, yb, x1s, gatess, mod_s, final_g, n_p).reshape(x_sample.shape)

    s_new, c_new, n_new, m_new = st
    return (y_prompt, y_sample, s_new[:, None], c_new[:, None], n_new[:, None], m_new[:, None])
```

```python
import functools

import jax
import jax.numpy as jnp
from jax import lax
from jax.experimental import pallas as pl
from jax.experimental.pallas import tpu as pltpu

F32 = jnp.float32
BF16 = jnp.bfloat16
HIGHEST = lax.Precision.HIGHEST

D_MODEL = 1024
DA = 512
HEAD_A = 64
H_A = DA // HEAD_A
DB = 512
H_B = 4
HEAD_B = DB // H_B
DECAY_LORA = 64
AAA_LORA = 64
GATE_LORA = 128
RWKV_COLS = 3 * DA + DECAY_LORA + AAA_LORA + GATE_LORA
MLSTM_MAIN = 4 * DB
N_GATES = 4 * H_B
GRID_W = 64
CHUNK = 64
N_EXPERTS = 32
TOP_K = 4
D_FF = D_MODEL
SWIGLU_LIMIT = 7.0
SWIGLU_ALPHA = 1.702
MOE_BLOCK = 512
NORM_EPS = 1e-6
GN_EPS = 64e-5
ROW_TILE = 256
DMA_UNROLL = 4
SUBLANES = 8
LANES = 128
LANE_ROWS = D_MODEL // LANES
VMEM_LIMIT = 56 * 1024 * 1024
PACK = 4
PACK_W = PACK * HEAD_A
N_PACKS = H_A // PACK
RWKV_GROUP = 2

NN = (((1,), (0,)), ((), ()))
NT = (((1,), (1,)), ((), ()))
TN = (((0,), (0,)), ((), ()))


def _mm(a, b, precision=HIGHEST):
    return lax.dot_general(a, b, NN, precision=precision, preferred_element_type=F32)


def _mm_nt(a, b, precision=HIGHEST):
    return lax.dot_general(a, b, NT, precision=precision, preferred_element_type=F32)


def _mm_tn(a, b, precision=HIGHEST):
    return lax.dot_general(a, b, TN, precision=precision, preferred_element_type=F32)


def _dot_bf16(a, b, dims=NN):
    return lax.dot_general(a.astype(BF16), b.astype(BF16), dims, preferred_element_type=F32)


def _sigmoid(x):
    return 1.0 / (1.0 + jnp.exp(-x))


def _log_sigmoid(x):
    return jnp.minimum(x, 0.0) - jnp.log(1.0 + jnp.exp(-jnp.abs(x)))


def _params(*sem, **kw):
    return pltpu.CompilerParams(dimension_semantics=sem, vmem_limit_bytes=VMEM_LIMIT, **kw)


def _split_bf16(x):
    hi = x.astype(BF16)
    return hi, (x - hi.astype(F32)).astype(BF16)


def _const_spec(shape):
    nd = len(shape)
    return pl.BlockSpec(shape, lambda *_: (0,) * nd)


def _tri(n, reverse, strict):
    t = lax.broadcasted_iota(jnp.int32, (n, n), 0)
    s = lax.broadcasted_iota(jnp.int32, (n, n), 1)
    if reverse:
        return (s > t) if strict else (s >= t)
    return (s < t) if strict else (s <= t)


def _seq_neighbours(ref, b, c, nc, cols):
    r0 = pl.multiple_of(c * CHUNK, CHUNK)
    zc = ref[b, pl.ds(r0, CHUNK), cols]
    row = lax.broadcasted_iota(jnp.int32, zc.shape, 0)
    p0 = pl.multiple_of(jnp.maximum(r0 - SUBLANES, 0), SUBLANES)
    n0 = pl.multiple_of(jnp.minimum(r0 + CHUNK, (nc - 1) * CHUNK), SUBLANES)
    before = ref[b, pl.ds(p0, SUBLANES), cols][SUBLANES - 1:SUBLANES]
    after = ref[b, pl.ds(n0, SUBLANES), cols][0:1]
    before = jnp.where(c > 0, before, 0.0)
    after = jnp.where(c < nc - 1, after, 0.0)
    prev = jnp.where(row == 0, before, pltpu.roll(zc, 1, 0))
    nxt = jnp.where(row == CHUNK - 1, after, pltpu.roll(zc, CHUNK - 1, 0))
    return zc, prev, nxt


def _ada_kernel(c_ref, w_ref, b_ref, o_ref):
    cc = c_ref[...]
    o_ref[...] = _dot_bf16(cc * _sigmoid(cc), w_ref[...]) + b_ref[...]


def _ada_mod(cc, ada_w, ada_b):
    rows = cc.shape[0]
    ncol = ada_w.shape[1]
    tn = 1536
    return pl.pallas_call(
        _ada_kernel,
        grid=(ncol // tn,),
        in_specs=[_const_spec((rows, D_MODEL)),
                  pl.BlockSpec((D_MODEL, tn), lambda j: (0, j)),
                  pl.BlockSpec((1, tn), lambda j: (0, j))],
        out_specs=pl.BlockSpec((rows, tn), lambda j: (0, j)),
        out_shape=jax.ShapeDtypeStruct((rows, ncol), F32),
        compiler_params=_params("arbitrary"),
        name="ada_mod",
    )(cc, ada_w, ada_b.reshape(1, ncol))


def _inproj_kernel(x_ref, mod_ref, g_ref, wr_ref, wm_ref, wg_ref, wgt_ref, zr_ref, zm_ref, zg_ref, zgt_ref):
    x = x_ref[...]
    mod = mod_ref[0]
    sh, sc = mod[:, 0:D_MODEL], mod[:, D_MODEL:2 * D_MODEL]
    y = x * lax.rsqrt(jnp.mean(x * x, axis=-1, keepdims=True) + NORM_EPS)
    hn = y * g_ref[...] * (1.0 + sc) + sh
    hb = hn.astype(BF16)
    zr_ref[...] = lax.dot_general(hb, wr_ref[...], NN, preferred_element_type=F32)
    zm_ref[...] = lax.dot_general(hb, wm_ref[...], NN, preferred_element_type=F32)
    zg_ref[...] = _dot_bf16(hb, wg_ref[...])
    zgt_ref[...] = _dot_bf16(wgt_ref[...], hb, NT)


def _inproj(x2, mod, norm_g, w_r, w_m, w_g, w_gt):
    rows = x2.shape[0]
    tiles_per_mod = rows // mod.shape[0] // ROW_TILE
    return pl.pallas_call(
        _inproj_kernel,
        grid=(rows // ROW_TILE,),
        in_specs=[pl.BlockSpec((ROW_TILE, D_MODEL), lambda i: (i, 0)),
                  pl.BlockSpec((1, 1, 6 * D_MODEL), lambda i: (i // tiles_per_mod, 0, 0)),
                  _const_spec((1, D_MODEL)),
                  _const_spec(w_r.shape), _const_spec(w_m.shape), _const_spec(w_g.shape),
                  _const_spec(w_gt.shape)],
        out_specs=[pl.BlockSpec((ROW_TILE, RWKV_COLS), lambda i: (i, 0)),
                   pl.BlockSpec((ROW_TILE, MLSTM_MAIN), lambda i: (i, 0)),
                   pl.BlockSpec((ROW_TILE, N_GATES), lambda i: (i, 0)),
                   pl.BlockSpec((N_GATES, ROW_TILE), lambda i: (0, i))],
        out_shape=[jax.ShapeDtypeStruct((rows, RWKV_COLS), F32),
                   jax.ShapeDtypeStruct((rows, MLSTM_MAIN), F32),
                   jax.ShapeDtypeStruct((rows, N_GATES), F32),
                   jax.ShapeDtypeStruct((N_GATES, rows), F32)],
        compiler_params=_params("arbitrary"),
        name="inproj",
    )(x2, mod, norm_g.reshape(1, D_MODEL), w_r, w_m, w_g, w_gt)


def _group_sum(x, ones_bd):
    hi = x.astype(BF16)
    lo = (x - hi.astype(F32)).astype(BF16)
    return (lax.dot_general(hi, ones_bd, NN, preferred_element_type=F32)
            + lax.dot_general(lo, ones_bd, NN, preferred_element_type=F32))


def _block_diag(x, ones_bd):
    return jnp.concatenate([x.astype(BF16)] * PACK, axis=0) * ones_bd


def _rwkv_kernel(zr_ref, s0_ref, mu_ref, kkw_ref, ka_ref, rk_ref, gnw_ref, gnb_ref, w0_ref, w2_ref,
                 a0_ref, a2_ref, g2_ref, ones_ref, y_ref, sout_ref,
                 r_s, k_s, v_s, kk_s, gate_s, lx_s, st_s, *, seq_len, grid):
    nc = seq_len // CHUNK
    ones_bd = ones_ref[...]

    def mix_chunk(c, carry):
        r0 = pl.multiple_of(c * CHUNK, CHUNK)
        for b in range(RWKV_GROUP):
            if grid:
                zc = zr_ref[b, pl.ds(r0, CHUNK), :]
                row = lax.broadcasted_iota(jnp.int32, zc.shape, 0)
                up0 = pl.multiple_of(jnp.maximum(c - 1, 0) * CHUNK, CHUNK)
                dn0 = pl.multiple_of(jnp.minimum(c + 1, nc - 1) * CHUNK, CHUNK)
                up = jnp.where(c > 0, zr_ref[b, pl.ds(up0, CHUNK), :], 0.0)
                down = jnp.where(c < nc - 1, zr_ref[b, pl.ds(dn0, CHUNK), :], 0.0)
                left = jnp.where(row == 0, 0.0, pltpu.roll(zc, 1, 0))
                right = jnp.where(row == CHUNK - 1, 0.0, pltpu.roll(zc, CHUNK - 1, 0))
                local = 0.25 * (up + down + left + right)
            else:
                zc, prev, nxt = _seq_neighbours(zr_ref, b, c, nc, slice(None))
                local = 0.5 * (prev + nxt)
            z = zc + (local - zc) * mu_ref[...]
            r, k, v = z[:, 0:DA], z[:, DA:2 * DA], z[:, 2 * DA:3 * DA]
            lx = z[:, 3 * DA:3 * DA + DECAY_LORA + AAA_LORA]
            xg = z[:, 3 * DA + DECAY_LORA + AAA_LORA:]
            kk = k * kkw_ref[...]
            sq = kk * kk
            ss = jnp.concatenate([_group_sum(sq[:, p * PACK_W:(p + 1) * PACK_W], ones_bd) for p in range(N_PACKS)],
                                 axis=1)
            kk = kk / jnp.maximum(jnp.sqrt(ss), 1e-12)
            lane = lax.broadcasted_iota(jnp.int32, lx.shape, 1)
            r_s[b, pl.ds(r0, CHUNK), :] = r
            k_s[b, pl.ds(r0, CHUNK), :] = k
            v_s[b, pl.ds(r0, CHUNK), :] = v
            kk_s[b, pl.ds(r0, CHUNK), :] = kk
            gate_s[b, pl.ds(r0, CHUNK), :] = _dot_bf16(_sigmoid(xg), g2_ref[...])
            lx_s[b, pl.ds(r0, CHUNK), :] = jnp.where(lane < DECAY_LORA, jnp.tanh(lx), lx)
        return carry

    lax.fori_loop(0, nc, mix_chunk, 0)

    rowblk = lax.broadcasted_iota(jnp.int32, (PACK_W, PACK_W), 0) // HEAD_A
    colblk = lax.broadcasted_iota(jnp.int32, (PACK_W, PACK_W), 1) // HEAD_A
    for b in range(RWKV_GROUP):
        for d in range(2):
            for p in range(N_PACKS):
                rows_ = jnp.concatenate([s0_ref[b, d, p * PACK + h] for h in range(PACK)], axis=0)
                st_s[b, d, p] = jnp.where(rowblk == colblk, jnp.concatenate([rows_] * PACK, axis=1), 0.0)

    t_idx = lax.broadcasted_iota(jnp.int32, (CHUNK, PACK_W), 0)
    s_idx = lax.broadcasted_iota(jnp.int32, (CHUNK, PACK_W), 1) % CHUNK

    def scan_step(i, carry):
        first_touch = i < nc // 2
        dot = functools.partial(lax.dot_general, preferred_element_type=F32)
        chains, out_sel = [], []
        for b, d in [(b, d) for b in range(RWKV_GROUP) for d in range(2)]:
            reverse = d == 1
            c = nc - 1 - i if reverse else i
            rows = pl.ds(pl.multiple_of(c * CHUNK, CHUNK), CHUNK)
            out_sel.append((b, rows))
            strict = (s_idx > t_idx) if reverse else (s_idx < t_idx)
            incl = (s_idx >= t_idx) if reverse else (s_idx <= t_idx)
            last = 0 if reverse else CHUNK - 1
            r, k, v, kk = r_s[b, rows, :], k_s[b, rows, :], v_s[b, rows, :], kk_s[b, rows, :]
            lx = lx_s[b, rows, :]
            wl = w0_ref[d:d + 1, :] + _dot_bf16(lx[:, 0:DECAY_LORA], w2_ref[d])
            logw = -jnp.exp(_log_sigmoid(wl) - 0.5)
            a = _sigmoid(a0_ref[d:d + 1, :] + _dot_bf16(lx[:, DECAY_LORA:], a2_ref[d]))
            kd = k * (1.0 + (a - 1.0) * ka_ref[...])
            kb = kk * a
            tri = _tri(CHUNK, reverse, False).astype(BF16)
            w1, w_rest = _split_bf16(logw)
            w2, w3 = _split_bf16(w_rest)
            cum = dot(tri, w1, NN) + dot(tri, w2, NN) + dot(tri, w3, NN)
            clast = cum[last:last + 1]
            rt = r * jnp.exp(cum)
            at = -kk * jnp.exp(cum - logw)
            einv = jnp.exp(-cum)
            kt, bt = kd * einv, kb * einv
            edec = jnp.exp(clast - cum)
            kp, bp = kd * edec, kb * edec
            ptot = jnp.exp(clast)
            rkd = r * kd * rk_ref[...]
            for p in range(N_PACKS):
                cs = slice(p * PACK_W, (p + 1) * PACK_W)
                chains.append(dict(
                    out=len(out_sel) - 1, st=(b, d, p), cs=cs, strict=strict, incl=incl, v=v[:, cs],
                    rkd=rkd[:, cs], ptot=ptot[:, cs],
                    ar=jnp.concatenate([at[:, cs], rt[:, cs]], axis=0).astype(BF16),
                    kt_bd=_block_diag(kt[:, cs], ones_bd), bt_bd=_block_diag(bt[:, cs], ones_bd),
                    v_bd=_block_diag(v[:, cs], ones_bd),
                    kbp=jnp.concatenate([kp[:, cs], bp[:, cs]], axis=0)))

        for ch in chains:
            ch['s_prev'] = st_s[ch['st']]
            ch['a_k'] = dot(ch['ar'], ch['kt_bd'], NT)
            ch['a_b'] = dot(ch['ar'], ch['bt_bd'], NT)
            ch['ars'] = dot(ch['ar'], ch['s_prev'].astype(BF16), NT)
        for ch in chains:
            a_ak = jnp.where(ch['strict'], ch['a_k'][:CHUNK], 0.0)
            a_rk = jnp.where(ch['incl'], ch['a_k'][CHUNK:], 0.0)
            ch['a_rb'] = jnp.where(ch['incl'], ch['a_b'][CHUNK:], 0.0).astype(BF16)
            n = jnp.where(ch['strict'], ch['a_b'][:CHUNK], 0.0)
            ch['n_hi'], n_lo = _split_bf16(n)
            ch['n_both'] = jnp.concatenate([ch['n_hi'], n_lo], axis=0)
            ch['pw'] = ch['n_hi']
            ch['m'] = jnp.where(s_idx == t_idx, 1.0, 0.0) + n
            akv = dot(jnp.concatenate([a_ak, a_rk], axis=0).astype(BF16), ch['v_bd'], NN)
            ch['rhs'] = ch['ars'][:CHUNK] + akv[:CHUNK]
            ch['y'] = ch['ars'][CHUNK:] + akv[CHUNK:]
        for ch in chains:
            ch['pw'] = dot(ch['pw'], _block_diag(ch['pw'], ones_bd), NN).astype(BF16)
        for j in range(1, 5):
            for ch in chains:
                pw_bd = _block_diag(ch['pw'], ones_bd)
                if j < 4:
                    both = dot(jnp.concatenate([ch['m'].astype(BF16), ch['pw']], axis=0), pw_bd, NN)
                    ch['m'] = ch['m'] + both[:CHUNK]
                    ch['pw'] = both[CHUNK:].astype(BF16)
                else:
                    ch['m'] = ch['m'] + dot(ch['m'].astype(BF16), pw_bd, NN)
        for ch in chains:
            ch['mb'] = ch['m'].astype(BF16)
            ch['u'] = dot(ch['mb'], _block_diag(ch['rhs'], ones_bd), NN)
        for ch in chains:
            u_hi, u_lo = _split_bf16(ch['u'])
            both = dot(ch['n_both'], _block_diag(u_hi, ones_bd), NN)
            nu = both[:CHUNK] + both[CHUNK:] + dot(ch['n_hi'], _block_diag(u_lo, ones_bd), NN)
            ch['resid'] = ch['rhs'] - ch['u'] + nu
        for ch in chains:
            ch['u'] = ch['u'] + dot(ch['mb'], _block_diag(ch['resid'], ones_bd), NN)
        for ch in chains:
            u = ch['u']
            ch['y'] = ch['y'] + dot(ch['a_rb'], _block_diag(u, ones_bd), NN)
            vu_hi, vu_lo = _split_bf16(jnp.concatenate([ch['v'], u], axis=0))
            kbp_hi, kbp_lo = _split_bf16(ch['kbp'])
            grown = dot(vu_hi, kbp_hi, TN) + dot(vu_hi, kbp_lo, TN) + dot(vu_lo, kbp_hi, TN)
            st_s[ch['st']] = ch['s_prev'] * ch['ptot'] + grown * ones_bd.astype(F32)
        n_ch = len(chains)
        sums = dot(jnp.concatenate([ch['y'] for ch in chains] + [ch['rkd'] for ch in chains],
                                   axis=0).astype(BF16), ones_bd, NN)
        ycs = [ch['y'] - sums[q * CHUNK:(q + 1) * CHUNK] * (1.0 / HEAD_A) for q, ch in enumerate(chains)]
        var = dot(jnp.concatenate([yc * yc for yc in ycs], axis=0).astype(BF16), ones_bd, NN) * (1.0 / HEAD_A)
        outs = [[] for _ in out_sel]
        for q, ch in enumerate(chains):
            cs = ch['cs']
            out = ycs[q] * lax.rsqrt(var[q * CHUNK:(q + 1) * CHUNK] + GN_EPS) * gnw_ref[:, cs] + gnb_ref[:, cs]
            outs[ch['out']].append(out + sums[(n_ch + q) * CHUNK:(n_ch + q + 1) * CHUNK] * ch['v'])
        for (b, rows), parts in zip(out_sel, outs):
            out = jnp.concatenate(parts, axis=1)

            @pl.when(first_touch)
            def _(b=b, rows=rows, out=out):
                y_ref[b, rows, :] = out

            @pl.when(jnp.logical_not(first_touch))
            def _(b=b, rows=rows, out=out):
                y_ref[b, rows, :] = (y_ref[b, rows, :] + out) * gate_s[b, rows, :]
        return carry

    lax.fori_loop(0, nc, scan_step, 0)

    for b in range(RWKV_GROUP):
        for d in range(2):
            for p in range(N_PACKS):
                s_fin = st_s[b, d, p]
                for h in range(PACK):
                    sout_ref[b, d, p * PACK + h] = s_fin[h * HEAD_A:(h + 1) * HEAD_A, h * HEAD_A:(h + 1) * HEAD_A]


def _rwkv(zr, s0, p, grid):
    bsz, seq_len, _ = zr.shape
    assert (seq_len // CHUNK) % 2 == 0
    kern = functools.partial(_rwkv_kernel, seq_len=seq_len, grid=grid)
    consts = [p['mu'], p['kkw'], p['ka'], p['rk'], p['gnw'], p['gnb'], p['w0'], p['w2'], p['a0'], p['a2'],
              p['g2'], p['ones_bd']]
    assert bsz % RWKV_GROUP == 0
    return pl.pallas_call(
        kern,
        grid=(bsz // RWKV_GROUP,),
        in_specs=[pl.BlockSpec((RWKV_GROUP, seq_len, RWKV_COLS), lambda b: (b, 0, 0),
                               pipeline_mode=pl.Buffered(1)),
                  pl.BlockSpec((RWKV_GROUP, 2, H_A, HEAD_A, HEAD_A), lambda b: (b, 0, 0, 0, 0))]
                 + [_const_spec(a.shape) for a in consts],
        out_specs=[pl.BlockSpec((RWKV_GROUP, seq_len, DA), lambda b: (b, 0, 0)),
                   pl.BlockSpec((RWKV_GROUP, 2, H_A, HEAD_A, HEAD_A), lambda b: (b, 0, 0, 0, 0))],
        out_shape=[jax.ShapeDtypeStruct((bsz, seq_len, DA), F32),
                   jax.ShapeDtypeStruct((bsz, 2, H_A, HEAD_A, HEAD_A), F32)],
        scratch_shapes=[pltpu.VMEM((RWKV_GROUP, seq_len, DA), F32)] * 5
                       + [pltpu.VMEM((RWKV_GROUP, seq_len, DECAY_LORA + AAA_LORA), F32),
                          pltpu.VMEM((RWKV_GROUP, 2, N_PACKS, PACK_W, PACK_W), F32)],
        compiler_params=_params("arbitrary"),
        name="rwkv_scan",
    )(zr, s0, *consts)


def _mlstm_kernel(zm_ref, g_ref, gt_ref, c0_ref, n0_ref, m0_ref, conv_ref, bi_ref, bit_ref, bf_ref, bft_ref,
                  ng_ref, erep_ref, y_ref, cout_ref, nout_ref, mout_ref,
                  qk_s, h_s, c_s, n_s, m_s, *, seq_len):
    nc = seq_len // CHUNK

    def conv_chunk(c, carry):
        zc, prev, nxt = _seq_neighbours(zm_ref, 0, c, nc, slice(0, 2 * DB))
        u = conv_ref[0:1, :] * prev + conv_ref[1:2, :] * zc + conv_ref[2:3, :] * nxt
        qk = u * _sigmoid(u)
        lane = lax.broadcasted_iota(jnp.int32, qk.shape, 1)
        qk_s[pl.ds(pl.multiple_of(c * CHUNK, CHUNK), CHUNK), :] = jnp.where(lane >= DB, qk * (HEAD_B ** -0.5), qk)
        return carry

    lax.fori_loop(0, nc, conv_chunk, 0)

    for d in range(2):
        c_s[d] = c0_ref[0, d]
        n_s[d] = n0_ref[0, d]
        m_s[d] = jnp.broadcast_to(m0_ref[0, d], (H_B, LANES))

    e_rep = erep_ref[...]
    ones_tl = jnp.ones((CHUNK, LANES), BF16)
    row_id = lax.broadcasted_iota(jnp.int32, (CHUNK, LANES), 0)

    def scan_max(x, reverse):
        shift = 1
        while shift < CHUNK:
            if reverse:
                moved = jnp.where(row_id < CHUNK - shift, pltpu.roll(x, CHUNK - shift, 0), -jnp.inf)
            else:
                moved = jnp.where(row_id >= shift, pltpu.roll(x, shift, 0), -jnp.inf)
            x = jnp.maximum(x, moved)
            shift *= 2
        return x

    def dot3(a, b, dims):
        a_hi, a_lo = _split_bf16(a)
        b_hi, b_lo = _split_bf16(b)
        dg = functools.partial(lax.dot_general, dimension_numbers=dims, preferred_element_type=F32)
        return dg(a_hi, b_hi) + dg(a_hi, b_lo) + dg(a_lo, b_hi)

    def scan_step(i, carry):
        dot = functools.partial(lax.dot_general, preferred_element_type=F32)
        chains = []
        for d in range(2):
            reverse = d == 1
            incl = _tri(CHUNK, reverse, False)
            tri_b = incl.astype(BF16)
            tri_row = _tri(CHUNK, not reverse, False).astype(F32)
            last = 0 if reverse else CHUNK - 1
            gsl = slice(d * H_B, (d + 1) * H_B)
            fsl = slice(2 * H_B + d * H_B, 2 * H_B + (d + 1) * H_B)
            c = nc - 1 - i if reverse else i
            rows = pl.ds(pl.multiple_of(c * CHUNK, CHUNK), CHUNK)
            g1, g_rest = _split_bf16(g_ref[0, rows, :])
            g2, g3 = _split_bf16(g_rest)
            g_rep = dot(g1, e_rep, NN) + dot(g2, e_rep, NN) + dot(g3, e_rep, NN)
            grow = gt_ref[0, c]
            i_row = grow[gsl, :] + bit_ref[gsl, :]
            f_row = _log_sigmoid(grow[fsl, :] + bft_ref[gsl, :])
            b_row = _mm(f_row, tri_row)
            for h in range(H_B):
                ji, jf = d * H_B + h, 2 * H_B + d * H_B + h
                ic = g_rep[:, ji * LANES:(ji + 1) * LANES] + bi_ref[:, ji:ji + 1]
                fc = _log_sigmoid(g_rep[:, jf * LANES:(jf + 1) * LANES] + bf_ref[:, ji:ji + 1])
                f1, f_rest = _split_bf16(fc)
                f2, f3 = _split_bf16(f_rest)
                bc = dot(tri_b, f1, NN) + dot(tri_b, f2, NN) + dot(tri_b, f3, NN)
                q = qk_s[rows, h * HEAD_B:(h + 1) * HEAD_B]
                chains.append(dict(
                    d=d, h=h, rows=rows, incl=incl, reverse=reverse, q=q, qb=q.astype(BF16),
                    k=qk_s[rows, DB + h * HEAD_B:DB + (h + 1) * HEAD_B],
                    v=zm_ref[0, rows, 2 * DB + h * HEAD_B:2 * DB + (h + 1) * HEAD_B],
                    bc=bc, ic=ic, bl=bc[last:last + 1, :], br=b_row[h:h + 1, :], ir=i_row[h:h + 1, :]))
        for ch in chains:
            d, h = ch['d'], ch['h']
            ch['qk'] = _dot_bf16(ch['qb'], ch['k'], NT)
            ch['c_prev'] = c_s[d, h]
            ch['qc'] = _dot_bf16(ch['qb'], ch['c_prev'])
            ch['n_prev'] = n_s[d, h:h + 1, :]
            ch['qn'] = dot3(ch['q'], jnp.broadcast_to(ch['n_prev'], (HEAD_B, HEAD_B)), NT)
        for ch in chains:
            d, h = ch['d'], ch['h']
            bc, ic, bl = ch['bc'], ch['ic'], ch['bl']
            m_prev = m_s[d, h:h + 1, :]
            m_t = bc + jnp.maximum(m_prev, scan_max(ic - bc, ch['reverse']))
            dmat = jnp.where(ch['incl'], bc[:, :CHUNK] - ch['br'] + ch['ir'], -jnp.inf)
            s = ch['qk'] * jnp.exp(dmat - m_t[:, :CHUNK])
            carry_w = jnp.exp(bc + m_prev - m_t)
            s_hi, s_lo = _split_bf16(s)
            den = dot(s_hi, ones_tl, NN) + dot(s_lo, ones_tl, NN) + carry_w * ch['qn']
            num = dot(s_hi, ch['v'].astype(BF16), NN) + carry_w * ch['qc']
            h_s[d, ch['rows'], h * HEAD_B:(h + 1) * HEAD_B] = num / jnp.maximum(jnp.abs(den), jnp.exp(-m_t))
            g = bl - bc + ic
            m_new = jnp.maximum(bl + m_prev, jnp.max(g, axis=0, keepdims=True))
            wk = jnp.exp(g - m_new)
            decay = jnp.exp(bl + m_prev - m_new)
            c_s[d, h] = decay * ch['c_prev'] + _dot_bf16(ch['k'], wk * ch['v'], TN)
            n_s[d, h:h + 1, :] = decay * ch['n_prev'] + jnp.sum(wk * ch['k'], axis=0, keepdims=True)
            m_s[d, h:h + 1, :] = m_new
        return carry

    lax.fori_loop(0, nc, scan_step, 0)
    cout_ref[0] = c_s[...]
    nout_ref[0] = n_s[...]
    mout_ref[0] = m_s[...]

    def norm_chunk(c, carry):
        rows = pl.ds(pl.multiple_of(c * CHUNK, CHUNK), CHUNK)
        og = zm_ref[0, rows, 3 * DB:4 * DB]
        hm = h_s[0, rows, :] + h_s[1, rows, :]
        parts = []
        for h in range(H_B):
            hh = hm[:, h * HEAD_B:(h + 1) * HEAD_B]
            parts.append(hh * lax.rsqrt(jnp.mean(hh * hh, axis=-1, keepdims=True) + NORM_EPS))
        y_ref[0, rows, :] = jnp.concatenate(parts, axis=1) * ng_ref[...] * _sigmoid(og)
        return carry

    lax.fori_loop(0, nc, norm_chunk, 0)


def _mlstm(zm, g, gt, c0, n0, m0, p):
    bsz, seq_len, _ = zm.shape
    nc = seq_len // CHUNK
    kern = functools.partial(_mlstm_kernel, seq_len=seq_len)
    consts = [p['conv'], p['bi'], p['bit'], p['bf'], p['bft'], p['ng'], p['e_rep']]
    return pl.pallas_call(
        kern,
        grid=(bsz,),
        in_specs=[pl.BlockSpec((1, seq_len, MLSTM_MAIN), lambda b: (b, 0, 0)),
                  pl.BlockSpec((1, seq_len, N_GATES), lambda b: (b, 0, 0)),
                  pl.BlockSpec((1, nc, N_GATES, CHUNK), lambda b: (b, 0, 0, 0)),
                  pl.BlockSpec((1, 2, H_B, HEAD_B, HEAD_B), lambda b: (b, 0, 0, 0, 0)),
                  pl.BlockSpec((1, 2, H_B, HEAD_B), lambda b: (b, 0, 0, 0)),
                  pl.BlockSpec((1, 2, H_B, 1), lambda b: (b, 0, 0, 0))]
                 + [_const_spec(a.shape) for a in consts],
        out_specs=[pl.BlockSpec((1, seq_len, DB), lambda b: (b, 0, 0)),
                   pl.BlockSpec((1, 2, H_B, HEAD_B, HEAD_B), lambda b: (b, 0, 0, 0, 0)),
                   pl.BlockSpec((1, 2, H_B, HEAD_B), lambda b: (b, 0, 0, 0)),
                   pl.BlockSpec((1, 2, H_B, LANES), lambda b: (b, 0, 0, 0))],
        out_shape=[jax.ShapeDtypeStruct((bsz, seq_len, DB), F32),
                   jax.ShapeDtypeStruct((bsz, 2, H_B, HEAD_B, HEAD_B), F32),
                   jax.ShapeDtypeStruct((bsz, 2, H_B, HEAD_B), F32),
                   jax.ShapeDtypeStruct((bsz, 2, H_B, LANES), F32)],
        scratch_shapes=[pltpu.VMEM((seq_len, 2 * DB), F32), pltpu.VMEM((2, seq_len, DB), F32),
                        pltpu.VMEM((2, H_B, HEAD_B, HEAD_B), F32), pltpu.VMEM((2, H_B, HEAD_B), F32),
                        pltpu.VMEM((2, H_B, LANES), F32)],
        compiler_params=_params("arbitrary"),
        name="mlstm_scan",
    )(zm, g, gt, c0, n0, m0, *consts)


def _outproj_kernel(x_ref, yr_ref, ym_ref, mod_ref, g_ref, wo_ref, rw_ref, rb_ref,
                    x1_ref, hn_ref, idx_ref, gate_ref):
    mod = mod_ref[0]
    g1 = mod[:, 2 * D_MODEL:3 * D_MODEL]
    sh2, sc2 = mod[:, 3 * D_MODEL:4 * D_MODEL], mod[:, 4 * D_MODEL:5 * D_MODEL]
    mix = _dot_bf16(yr_ref[...], wo_ref[0:DA, :]) + _dot_bf16(ym_ref[...], wo_ref[DA:, :])
    x1 = x_ref[...] + g1 * mix
    x1_ref[...] = x1
    y = x1 * lax.rsqrt(jnp.mean(x1 * x1, axis=-1, keepdims=True) + NORM_EPS)
    hn = y * g_ref[...] * (1.0 + sc2) + sh2
    hn_ref[...] = hn
    logits = _dot_bf16(hn, rw_ref[...]) + rb_ref[...]
    lane = lax.broadcasted_iota(jnp.int32, logits.shape, 1)
    vals, idxs = [], []
    for _ in range(TOP_K):
        top = jnp.max(logits, axis=-1, keepdims=True)
        pick = jnp.min(jnp.where(logits == top, lane, N_EXPERTS), axis=-1, keepdims=True)
        vals.append(top)
        idxs.append(pick)
        logits = jnp.where(lane == pick, -jnp.inf, logits)
    exps = [jnp.exp(v - vals[0]) for v in vals]
    total = exps[0] + exps[1] + exps[2] + exps[3]
    k_lane = lax.broadcasted_iota(jnp.int32, (x1.shape[0], TOP_K), 1)
    gates = jnp.zeros((x1.shape[0], TOP_K), F32)
    picks = jnp.zeros((x1.shape[0], TOP_K), jnp.int32)
    for j in range(TOP_K):
        gates = jnp.where(k_lane == j, exps[j] / total, gates)
        picks = jnp.where(k_lane == j, idxs[j], picks)
    idx_ref[...] = picks
    gate_ref[...] = gates


def _outproj(x2, yr, ym, mod, norm_g, w_out, router_w, router_b):
    rows = x2.shape[0]
    tiles_per_mod = rows // mod.shape[0] // ROW_TILE
    row_spec = lambda w: pl.BlockSpec((ROW_TILE, w), lambda i: (i, 0))
    return pl.pallas_call(
        _outproj_kernel,
        grid=(rows // ROW_TILE,),
        in_specs=[row_spec(D_MODEL), row_spec(DA), row_spec(DB),
                  pl.BlockSpec((1, 1, 6 * D_MODEL), lambda i: (i // tiles_per_mod, 0, 0)),
                  _const_spec((1, D_MODEL)), _const_spec(w_out.shape), _const_spec(router_w.shape),
                  _const_spec((1, N_EXPERTS))],
        out_specs=[row_spec(D_MODEL), row_spec(D_MODEL), row_spec(TOP_K), row_spec(TOP_K)],
        out_shape=[jax.ShapeDtypeStruct((rows, D_MODEL), F32), jax.ShapeDtypeStruct((rows, D_MODEL), F32),
                   jax.ShapeDtypeStruct((rows, TOP_K), jnp.int32), jax.ShapeDtypeStruct((rows, TOP_K), F32)],
        compiler_params=_params("arbitrary"),
        name="outproj_router",
    )(x2, yr, ym, mod, norm_g.reshape(1, D_MODEL), w_out, router_w, router_b.reshape(1, N_EXPERTS))


def _rank_kernel(idx_ref, rank_ref, count_ref, run_s):
    i = pl.program_id(0)

    @pl.when(i == 0)
    def _():
        run_s[...] = jnp.zeros_like(run_s)

    idx = idx_ref[...]
    rows = idx.shape[0]
    lane = lax.broadcasted_iota(jnp.int32, (rows, N_EXPERTS), 1)
    hot = jnp.zeros((rows, N_EXPERTS), F32)
    for j in range(TOP_K):
        hot = hot + jnp.where(lane == idx[:, j:j + 1], 1.0, 0.0)
    before = _dot_bf16(_tri(rows, False, True).astype(F32), hot) + run_s[...]
    k_lane = lax.broadcasted_iota(jnp.int32, (rows, TOP_K), 1)
    rank = jnp.zeros((rows, TOP_K), F32)
    for j in range(TOP_K):
        rj = jnp.sum(jnp.where(lane == idx[:, j:j + 1], before, 0.0), axis=-1, keepdims=True)
        rank = jnp.where(k_lane == j, rj, rank)
    rank_ref[...] = rank.astype(jnp.int32)
    run_s[...] = run_s[...] + jnp.sum(hot, axis=0, keepdims=True)
    count_ref[...] = run_s[...].astype(jnp.int32)


def _expert_ranks(idx):
    rows = idx.shape[0]
    return pl.pallas_call(
        _rank_kernel,
        grid=(rows // ROW_TILE,),
        in_specs=[pl.BlockSpec((ROW_TILE, TOP_K), lambda i: (i, 0))],
        out_specs=[pl.BlockSpec((ROW_TILE, TOP_K), lambda i: (i, 0)), _const_spec((1, N_EXPERTS))],
        out_shape=[jax.ShapeDtypeStruct((rows, TOP_K), jnp.int32),
                   jax.ShapeDtypeStruct((1, N_EXPERTS), jnp.int32)],
        scratch_shapes=[pltpu.VMEM((1, N_EXPERTS), F32)],
        compiler_params=_params("arbitrary"),
        name="expert_ranks",
    )(idx)


def _dispatch_kernel(dest_ref, pend_ref, x_ref, xs_ref, zero_s, sem):
    @pl.when(pl.program_id(0) == 0)
    def _():
        zero_s[...] = jnp.zeros_like(zero_s)

        def zero_block(start):
            return pltpu.make_async_copy(zero_s, xs_ref.at[pl.ds(pl.multiple_of(start, MOE_BLOCK), MOE_BLOCK)], sem)

        def has_rows(e):
            return pend_ref[e] > (pend_ref[e - 1] if e else 0)

        used_end = pend_ref[N_EXPERTS - 1]
        n_tail = (xs_ref.shape[0] - used_end) // MOE_BLOCK

        def start_tail(b, carry):
            zero_block(used_end + b * MOE_BLOCK).start()
            return carry

        def wait_tail(b, carry):
            zero_block(0).wait()
            return carry

        for e in range(N_EXPERTS):
            @pl.when(has_rows(e))
            def _(e=e):
                zero_block(pend_ref[e] - MOE_BLOCK).start()
        lax.fori_loop(0, n_tail, start_tail, 0)
        for e in range(N_EXPERTS):
            @pl.when(has_rows(e))
            def _():
                zero_block(0).wait()
        lax.fori_loop(0, n_tail, wait_tail, 0)

    def issue(g, carry):
        for q in range(DMA_UNROLL):
            r = g * DMA_UNROLL + q
            for j in range(TOP_K):
                slot = dest_ref[r * TOP_K + j]
                pltpu.make_async_copy(x_ref.at[pl.ds(r, 1)], xs_ref.at[pl.ds(slot, 1)], sem).start()
        return carry

    lax.fori_loop(0, ROW_TILE // DMA_UNROLL, issue, 0)
    all_rows = xs_ref.at[pl.ds(0, ROW_TILE * TOP_K)]
    pltpu.make_async_copy(all_rows, all_rows, sem).wait()


def _dispatch(dest_flat, pad_end, x2, n_slots):
    rows = x2.shape[0]
    return pl.pallas_call(
        _dispatch_kernel,
        grid=(rows // ROW_TILE,),
        in_specs=[pl.BlockSpec((ROW_TILE * TOP_K,), lambda i: (i,), memory_space=pltpu.SMEM),
                  pl.BlockSpec(memory_space=pltpu.SMEM),
                  pl.BlockSpec((ROW_TILE, D_MODEL), lambda i: (i, 0))],
        out_specs=pl.BlockSpec(memory_space=pl.ANY),
        out_shape=jax.ShapeDtypeStruct((n_slots, D_MODEL), x2.dtype),
        scratch_shapes=[pltpu.VMEM((MOE_BLOCK, D_MODEL), F32), pltpu.SemaphoreType.DMA(())],
        compiler_params=_params("arbitrary", disable_bounds_checks=True),
        name="moe_dispatch",
    )(dest_flat, pad_end, x2)


def _expert_kernel(be_ref, nb_ref, xs_ref, w1_ref, b1_ref, w2_ref, b2_ref, y_ref, w1_s, w2_s):
    i = pl.program_id(0)
    changed = jnp.logical_or(i == 0, be_ref[i] != be_ref[jnp.maximum(i - 1, 0)])

    @pl.when(jnp.logical_and(changed, i < nb_ref[0]))
    def _():
        w1_s[...] = w1_ref[0].astype(BF16)
        w2_s[...] = w2_ref[0].astype(BF16)

    @pl.when(i < nb_ref[0])
    def _():
        xb = xs_ref[...].astype(BF16)
        hu = lax.dot_general(xb, w1_s[...], NN, preferred_element_type=F32) + b1_ref[0]
        glu = jnp.minimum(hu[:, :D_FF], SWIGLU_LIMIT)
        lin = jnp.clip(hu[:, D_FF:], -SWIGLU_LIMIT, SWIGLU_LIMIT)
        act = glu * _sigmoid(SWIGLU_ALPHA * glu) * (lin + 1.0)
        y_ref[...] = lax.dot_general(act.astype(BF16), w2_s[...], NN, preferred_element_type=F32) + b2_ref[0]

    @pl.when(i >= nb_ref[0])
    def _():
        y_ref[...] = jnp.zeros_like(y_ref)


def _experts(block_e, n_used, xs, w1, b1, w2, b2):
    n_blocks = xs.shape[0] // MOE_BLOCK
    grid_spec = pltpu.PrefetchScalarGridSpec(
        num_scalar_prefetch=2,
        grid=(n_blocks,),
        in_specs=[pl.BlockSpec((MOE_BLOCK, D_MODEL), lambda i, be, nb: (jnp.minimum(i, nb[0] - 1), 0)),
                  pl.BlockSpec((1, D_MODEL, 2 * D_FF), lambda i, be, nb: (be[i], 0, 0)),
                  pl.BlockSpec((1, 1, 2 * D_FF), lambda i, be, nb: (be[i], 0, 0)),
                  pl.BlockSpec((1, D_FF, D_MODEL), lambda i, be, nb: (be[i], 0, 0)),
                  pl.BlockSpec((1, 1, D_MODEL), lambda i, be, nb: (be[i], 0, 0))],
        out_specs=pl.BlockSpec((MOE_BLOCK, D_MODEL), lambda i, be, nb: (i, 0)),
        scratch_shapes=[pltpu.VMEM((D_MODEL, 2 * D_FF), BF16), pltpu.VMEM((D_FF, D_MODEL), BF16)],
    )
    return pl.pallas_call(
        _expert_kernel,
        grid_spec=grid_spec,
        out_shape=jax.ShapeDtypeStruct(xs.shape, F32),
        compiler_params=_params("arbitrary"),
        name="moe_experts",
    )(block_e, n_used, xs, w1, b1.reshape(N_EXPERTS, 1, 2 * D_FF), w2, b2.reshape(N_EXPERTS, 1, D_MODEL))


def _combine_kernel(dest_ref, yb_ref, x1_ref, gate_ref, mod_ref, fg_ref, o_ref, buf, sem):
    def issue(g, carry):
        for q in range(DMA_UNROLL):
            r = g * DMA_UNROLL + q
            for j in range(TOP_K):
                slot = dest_ref[r * TOP_K + j]
                pltpu.make_async_copy(yb_ref.at[pl.ds(slot, 1)], buf.at[j, pl.ds(r, 1)], sem).start()
        return carry

    lax.fori_loop(0, ROW_TILE // DMA_UNROLL, issue, 0)
    pltpu.make_async_copy(buf, buf, sem).wait()

    g2 = mod_ref[0][:, 5 * D_MODEL:6 * D_MODEL]
    gates = gate_ref[...]
    moe = jnp.zeros((ROW_TILE, D_MODEL), F32)
    for j in range(TOP_K):
        moe = moe + buf[j] * gates[:, j:j + 1]
    x2 = x1_ref[...] + g2 * moe
    y = x2 * lax.rsqrt(jnp.mean(x2 * x2, axis=-1, keepdims=True) + NORM_EPS)
    o_ref[...] = y * fg_ref[...]


def _combine(dest_flat, yb, x1, gates, mod, final_g, row_offset):
    rows = x1.shape[0]
    tiles_per_mod = rows // mod.shape[0] // ROW_TILE
    tile_offset = row_offset // ROW_TILE
    row_spec = pl.BlockSpec((ROW_TILE, D_MODEL), lambda i: (i, 0))
    return pl.pallas_call(
        _combine_kernel,
        grid=(rows // ROW_TILE,),
        in_specs=[pl.BlockSpec((ROW_TILE * TOP_K,), lambda i: (i + tile_offset,), memory_space=pltpu.SMEM),
                  pl.BlockSpec(memory_space=pl.ANY),
                  row_spec,
                  pl.BlockSpec((ROW_TILE, TOP_K), lambda i: (i, 0)),
                  pl.BlockSpec((1, 1, 6 * D_MODEL), lambda i: (i // tiles_per_mod, 0, 0)),
                  _const_spec((1, D_MODEL))],
        out_specs=row_spec,
        out_shape=jax.ShapeDtypeStruct((rows, D_MODEL), F32),
        scratch_shapes=[pltpu.VMEM((TOP_K, ROW_TILE, D_MODEL), F32), pltpu.SemaphoreType.DMA(())],
        compiler_params=_params("arbitrary", disable_bounds_checks=True),
        name="moe_combine",
    )(dest_flat, yb, x1, gates, mod, final_g.reshape(1, D_MODEL))


def _block_diag_ones(width, block):
    i = jnp.arange(width) // block
    return (i[:, None] == i[None, :]).astype(F32)


def _mixers(x, mod, lp, s0, c0, n0, m0, grid):
    bsz, seq_len, _ = x.shape
    nc = seq_len // CHUNK
    x2 = x.reshape(bsz * seq_len, D_MODEL)
    zr, zm, zg, zgt = _inproj(x2, mod, lp['norm1_g'], lp['w_r'], lp['w_m'], lp['w_g'], lp['w_gt'])
    gt = zgt.reshape(N_GATES, bsz, nc, CHUNK).transpose(1, 2, 0, 3)
    yr, s_new = _rwkv(zr.reshape(bsz, seq_len, RWKV_COLS), s0, lp, grid)
    ym, c_new, n_new, m_new = _mlstm(zm.reshape(bsz, seq_len, MLSTM_MAIN), zg.reshape(bsz, seq_len, N_GATES), gt,
                                     c0, n0, m0.reshape(bsz, 2, H_B, 1), lp)
    x1, hn, idx, gates = _outproj(x2, yr.reshape(-1, DA), ym.reshape(-1, DB), mod, lp['norm2_g'], lp['w_out'],
                                  lp['router_w'], lp['router_b'])
    return x1, hn, idx, gates, (s_new, c_new, n_new, m_new[..., 0])


def kernel(x_prompt, x_sample, state_rwkv, state_mlstm_C, state_mlstm_n, state_mlstm_m, c, c_ctx, ada_w, ada_b, norm1_g, norm2_g, w_in, w_out, rwkv_mu, rwkv_w0, rwkv_w2, rwkv_a0, rwkv_a2, rwkv_g2, rwkv_kk, rwkv_ka, rwkv_rk, rwkv_gn_w, rwkv_gn_b, mlstm_conv, mlstm_bi, mlstm_bf, mlstm_norm_g, router_w, router_b, moe_w1, moe_b1, moe_w2, moe_b2, final_g):
    bp, lp_len, _ = x_prompt.shape
    bs, ls_len, _ = x_sample.shape
    w = w_in[0]
    lp = {
        'norm1_g': norm1_g[0], 'norm2_g': norm2_g[0],
        'w_r': w[:, :RWKV_COLS].astype(BF16),
        'w_m': w[:, RWKV_COLS:RWKV_COLS + MLSTM_MAIN].astype(BF16),
        'w_g': w[:, RWKV_COLS + MLSTM_MAIN:],
        'w_gt': w[:, RWKV_COLS + MLSTM_MAIN:].T,
        'w_out': w_out[0].astype(BF16),
        'mu': rwkv_mu[0].reshape(1, RWKV_COLS), 'kkw': rwkv_kk[0].reshape(1, DA), 'ka': rwkv_ka[0].reshape(1, DA),
        'rk': rwkv_rk[0].reshape(1, DA), 'gnw': rwkv_gn_w[0].reshape(1, DA), 'gnb': rwkv_gn_b[0].reshape(1, DA),
        'w0': rwkv_w0[0], 'w2': rwkv_w2[0], 'a0': rwkv_a0[0], 'a2': rwkv_a2[0], 'g2': rwkv_g2[0],
        'ones_bd': _block_diag_ones(PACK_W, HEAD_A).astype(BF16),
        'conv': mlstm_conv[0], 'bi': mlstm_bi[0].reshape(1, 2 * H_B), 'bit': mlstm_bi[0].reshape(2 * H_B, 1),
        'bf': mlstm_bf[0].reshape(1, 2 * H_B), 'bft': mlstm_bf[0].reshape(2 * H_B, 1),
        'ng': mlstm_norm_g[0].reshape(1, DB),
        'e_rep': jnp.repeat(jnp.eye(N_GATES, dtype=BF16), LANES, axis=1),
        'router_w': router_w[0], 'router_b': router_b[0],
    }
    cc = jnp.concatenate([c_ctx[None, :], c, jnp.zeros((2 * SUBLANES - 1 - bs, D_MODEL), F32)], axis=0)
    mod = _ada_mod(cc, ada_w[0], ada_b[0])
    mod_p = mod[0:1].reshape(1, 1, 6 * D_MODEL)
    mod_s = mod[1:1 + bs].reshape(bs, 1, 6 * D_MODEL)

    zeros = lambda *shape: jnp.zeros(shape, F32)
    x1p, hnp, idxp, gatesp, st = _mixers(x_prompt, mod_p, lp, zeros(bp, 2, H_A, HEAD_A, HEAD_A),
                                         zeros(bp, 2, H_B, HEAD_B, HEAD_B), zeros(bp, 2, H_B, HEAD_B),
                                         zeros(bp, 2, H_B), False)
    x1s, hns, idxs, gatess, _ = _mixers(x_sample, mod_s, lp, state_rwkv[:, 0], state_mlstm_C[:, 0],
                                        state_mlstm_n[:, 0], state_mlstm_m[:, 0], True)

    n_p = bp * lp_len
    idx = jnp.concatenate([idxp, idxs], axis=0)
    hn = jnp.concatenate([hnp, hns], axis=0)
    n_tok = idx.shape[0]
    rank, counts = _expert_ranks(idx)
    counts = counts[0]
    padded = (counts + MOE_BLOCK - 1) // MOE_BLOCK * MOE_BLOCK
    pad_end = jnp.cumsum(padded)
    pad_start = pad_end - padded
    n_blocks = n_tok * TOP_K // MOE_BLOCK + N_EXPERTS
    block_start = jnp.arange(n_blocks, dtype=jnp.int32) * MOE_BLOCK
    block_e = jnp.minimum(jnp.sum(pad_end[None, :] <= block_start[:, None], axis=1), N_EXPERTS - 1).astype(jnp.int32)
    n_used = (pad_end[-1:] // MOE_BLOCK).astype(jnp.int32)
    dest = (pad_start[idx] + rank).astype(jnp.int32).reshape(-1)
    xs = _dispatch(dest, pad_end.astype(jnp.int32), hn, n_blocks * MOE_BLOCK)
    yb = _experts(block_e, n_used, xs, moe_w1[0], moe_b1[0], moe_w2[0], moe_b2[0])
    y_prompt = _combine(dest, yb, x1p, gatesp, mod_p, final_g, 0).reshape(x_prompt.shape)
    y_sample = _combine(dest, yb, x1s, gatess, mod_s, final_g, n_p).reshape(x_sample.shape)

    s_new, c_new, n_new, m_new = st
    return (y_prompt, y_sample, s_new[:, None], c_new[:, None], n_new[:, None], m_new[:, None])
```

```python
import functools

import jax
import jax.numpy as jnp
from jax import lax
from jax.experimental import pallas as pl
from jax.experimental.pallas import tpu as pltpu

F32 = jnp.float32
BF16 = jnp.bfloat16
HIGHEST = lax.Precision.HIGHEST

D_MODEL = 1024
DA = 512
HEAD_A = 64
H_A = DA // HEAD_A
DB = 512
H_B = 4
HEAD_B = DB // H_B
DECAY_LORA = 64
AAA_LORA = 64
GATE_LORA = 128
RWKV_COLS = 3 * DA + DECAY_LORA + AAA_LORA + GATE_LORA
MLSTM_MAIN = 4 * DB
N_GATES = 4 * H_B
GRID_W = 64
CHUNK = 64
N_EXPERTS = 32
TOP_K = 4
D_FF = D_MODEL
SWIGLU_LIMIT = 7.0
SWIGLU_ALPHA = 1.702
MOE_BLOCK = 512
NORM_EPS = 1e-6
GN_EPS = 64e-5
ROW_TILE = 256
PROJ_TILE = 512
DMA_UNROLL = 4
SUBLANES = 8
LANES = 128
LANE_ROWS = D_MODEL // LANES
VMEM_LIMIT = 56 * 1024 * 1024
PACK = 4
PACK_W = PACK * HEAD_A
N_PACKS = H_A // PACK
RWKV_GROUP = 2
MLSTM_GROUP = 2

NN = (((1,), (0,)), ((), ()))
NT = (((1,), (1,)), ((), ()))
TN = (((0,), (0,)), ((), ()))


def _mm(a, b, precision=HIGHEST):
    return lax.dot_general(a, b, NN, precision=precision, preferred_element_type=F32)


def _mm_nt(a, b, precision=HIGHEST):
    return lax.dot_general(a, b, NT, precision=precision, preferred_element_type=F32)


def _mm_tn(a, b, precision=HIGHEST):
    return lax.dot_general(a, b, TN, precision=precision, preferred_element_type=F32)


def _dot_bf16(a, b, dims=NN):
    return lax.dot_general(a.astype(BF16), b.astype(BF16), dims, preferred_element_type=F32)


def _sigmoid(x):
    return 1.0 / (1.0 + jnp.exp(-x))


def _log_sigmoid(x):
    return jnp.minimum(x, 0.0) - jnp.log(1.0 + jnp.exp(-jnp.abs(x)))


def _params(*sem, **kw):
    return pltpu.CompilerParams(dimension_semantics=sem, vmem_limit_bytes=VMEM_LIMIT, **kw)


def _split_bf16(x):
    hi = x.astype(BF16)
    return hi, (x - hi.astype(F32)).astype(BF16)


def _const_spec(shape):
    nd = len(shape)
    return pl.BlockSpec(shape, lambda *_: (0,) * nd)


def _tri(n, reverse, strict):
    t = lax.broadcasted_iota(jnp.int32, (n, n), 0)
    s = lax.broadcasted_iota(jnp.int32, (n, n), 1)
    if reverse:
        return (s > t) if strict else (s >= t)
    return (s < t) if strict else (s <= t)


def _seq_neighbours(ref, b, c, nc, cols):
    r0 = pl.multiple_of(c * CHUNK, CHUNK)
    zc = ref[b, pl.ds(r0, CHUNK), cols]
    row = lax.broadcasted_iota(jnp.int32, zc.shape, 0)
    p0 = pl.multiple_of(jnp.maximum(r0 - SUBLANES, 0), SUBLANES)
    n0 = pl.multiple_of(jnp.minimum(r0 + CHUNK, (nc - 1) * CHUNK), SUBLANES)
    before = ref[b, pl.ds(p0, SUBLANES), cols][SUBLANES - 1:SUBLANES]
    after = ref[b, pl.ds(n0, SUBLANES), cols][0:1]
    before = jnp.where(c > 0, before, 0.0)
    after = jnp.where(c < nc - 1, after, 0.0)
    prev = jnp.where(row == 0, before, pltpu.roll(zc, 1, 0))
    nxt = jnp.where(row == CHUNK - 1, after, pltpu.roll(zc, CHUNK - 1, 0))
    return zc, prev, nxt


def _ada_kernel(c_ref, w_ref, b_ref, o_ref):
    cc = c_ref[...]
    o_ref[...] = _dot_bf16(cc * _sigmoid(cc), w_ref[...]) + b_ref[...]


def _ada_mod(cc, ada_w, ada_b):
    rows = cc.shape[0]
    ncol = ada_w.shape[1]
    tn = 1536
    return pl.pallas_call(
        _ada_kernel,
        grid=(ncol // tn,),
        in_specs=[_const_spec((rows, D_MODEL)),
                  pl.BlockSpec((D_MODEL, tn), lambda j: (0, j)),
                  pl.BlockSpec((1, tn), lambda j: (0, j))],
        out_specs=pl.BlockSpec((rows, tn), lambda j: (0, j)),
        out_shape=jax.ShapeDtypeStruct((rows, ncol), F32),
        compiler_params=_params("arbitrary"),
        name="ada_mod",
    )(cc, ada_w, ada_b.reshape(1, ncol))


def _inproj_kernel(x_ref, mod_ref, g_ref, wr_ref, wm_ref, wg_ref, wgt_ref, zr_ref, zm_ref, zg_ref, zgt_ref):
    x = x_ref[...]
    mod = mod_ref[0]
    sh, sc = mod[:, 0:D_MODEL], mod[:, D_MODEL:2 * D_MODEL]
    y = x * lax.rsqrt(jnp.mean(x * x, axis=-1, keepdims=True) + NORM_EPS)
    hn = y * g_ref[...] * (1.0 + sc) + sh
    hb = hn.astype(BF16)
    zr_ref[...] = lax.dot_general(hb, wr_ref[...], NN, preferred_element_type=F32)
    zm_ref[...] = lax.dot_general(hb, wm_ref[...], NN, preferred_element_type=F32)
    zg_ref[...] = _dot_bf16(hb, wg_ref[...])
    zgt_ref[...] = _dot_bf16(wgt_ref[...], hb, NT)


def _inproj(x2, mod, norm_g, w_r, w_m, w_g, w_gt):
    rows = x2.shape[0]
    tiles_per_mod = rows // mod.shape[0] // PROJ_TILE
    return pl.pallas_call(
        _inproj_kernel,
        grid=(rows // PROJ_TILE,),
        in_specs=[pl.BlockSpec((PROJ_TILE, D_MODEL), lambda i: (i, 0)),
                  pl.BlockSpec((1, 1, 6 * D_MODEL), lambda i: (i // tiles_per_mod, 0, 0)),
                  _const_spec((1, D_MODEL)),
                  _const_spec(w_r.shape), _const_spec(w_m.shape), _const_spec(w_g.shape),
                  _const_spec(w_gt.shape)],
        out_specs=[pl.BlockSpec((PROJ_TILE, RWKV_COLS), lambda i: (i, 0)),
                   pl.BlockSpec((PROJ_TILE, MLSTM_MAIN), lambda i: (i, 0)),
                   pl.BlockSpec((PROJ_TILE, N_GATES), lambda i: (i, 0)),
                   pl.BlockSpec((N_GATES, PROJ_TILE), lambda i: (0, i))],
        out_shape=[jax.ShapeDtypeStruct((rows, RWKV_COLS), F32),
                   jax.ShapeDtypeStruct((rows, MLSTM_MAIN), F32),
                   jax.ShapeDtypeStruct((rows, N_GATES), F32),
                   jax.ShapeDtypeStruct((N_GATES, rows), F32)],
        compiler_params=_params("arbitrary"),
        name="inproj",
    )(x2, mod, norm_g.reshape(1, D_MODEL), w_r, w_m, w_g, w_gt)


def _group_sum(x, ones_bd):
    hi = x.astype(BF16)
    lo = (x - hi.astype(F32)).astype(BF16)
    return (lax.dot_general(hi, ones_bd, NN, preferred_element_type=F32)
            + lax.dot_general(lo, ones_bd, NN, preferred_element_type=F32))


def _block_diag(x, ones_bd):
    return jnp.concatenate([x.astype(BF16)] * PACK, axis=0) * ones_bd


def _rwkv_kernel(zr_ref, s0_ref, mu_ref, kkw_ref, ka_ref, rk_ref, gnw_ref, gnb_ref, w0_ref, w2_ref,
                 a0_ref, a2_ref, g2_ref, ones_ref, y_ref, sout_ref,
                 r_s, k_s, v_s, kk_s, gate_s, lx_s, st_s, *, seq_len, grid):
    nc = seq_len // CHUNK
    ones_bd = ones_ref[...]

    def mix_chunk(c, carry):
        r0 = pl.multiple_of(c * CHUNK, CHUNK)
        for b in range(RWKV_GROUP):
            if grid:
                zc = zr_ref[b, pl.ds(r0, CHUNK), :]
                row = lax.broadcasted_iota(jnp.int32, zc.shape, 0)
                up0 = pl.multiple_of(jnp.maximum(c - 1, 0) * CHUNK, CHUNK)
                dn0 = pl.multiple_of(jnp.minimum(c + 1, nc - 1) * CHUNK, CHUNK)
                up = jnp.where(c > 0, zr_ref[b, pl.ds(up0, CHUNK), :], 0.0)
                down = jnp.where(c < nc - 1, zr_ref[b, pl.ds(dn0, CHUNK), :], 0.0)
                left = jnp.where(row == 0, 0.0, pltpu.roll(zc, 1, 0))
                right = jnp.where(row == CHUNK - 1, 0.0, pltpu.roll(zc, CHUNK - 1, 0))
                local = 0.25 * (up + down + left + right)
            else:
                zc, prev, nxt = _seq_neighbours(zr_ref, b, c, nc, slice(None))
                local = 0.5 * (prev + nxt)
            z = zc + (local - zc) * mu_ref[...]
            r, k, v = z[:, 0:DA], z[:, DA:2 * DA], z[:, 2 * DA:3 * DA]
            lx = z[:, 3 * DA:3 * DA + DECAY_LORA + AAA_LORA]
            xg = z[:, 3 * DA + DECAY_LORA + AAA_LORA:]
            kk = k * kkw_ref[...]
            sq = kk * kk
            ss = jnp.concatenate([_group_sum(sq[:, p * PACK_W:(p + 1) * PACK_W], ones_bd) for p in range(N_PACKS)],
                                 axis=1)
            kk = kk / jnp.maximum(jnp.sqrt(ss), 1e-12)
            lane = lax.broadcasted_iota(jnp.int32, lx.shape, 1)
            r_s[b, pl.ds(r0, CHUNK), :] = r
            k_s[b, pl.ds(r0, CHUNK), :] = k
            v_s[b, pl.ds(r0, CHUNK), :] = v
            kk_s[b, pl.ds(r0, CHUNK), :] = kk
            gate_s[b, pl.ds(r0, CHUNK), :] = _dot_bf16(_sigmoid(xg), g2_ref[...])
            lx_s[b, pl.ds(r0, CHUNK), :] = jnp.where(lane < DECAY_LORA, jnp.tanh(lx), lx)
        return carry

    lax.fori_loop(0, nc, mix_chunk, 0)

    rowblk = lax.broadcasted_iota(jnp.int32, (PACK_W, PACK_W), 0) // HEAD_A
    colblk = lax.broadcasted_iota(jnp.int32, (PACK_W, PACK_W), 1) // HEAD_A
    for b in range(RWKV_GROUP):
        for d in range(2):
            for p in range(N_PACKS):
                rows_ = jnp.concatenate([s0_ref[b, d, p * PACK + h] for h in range(PACK)], axis=0)
                st_s[b, d, p] = jnp.where(rowblk == colblk, jnp.concatenate([rows_] * PACK, axis=1), 0.0)

    t_idx = lax.broadcasted_iota(jnp.int32, (CHUNK, PACK_W), 0)
    s_idx = lax.broadcasted_iota(jnp.int32, (CHUNK, PACK_W), 1) % CHUNK

    def scan_step(i, carry):
        first_touch = i < nc // 2
        dot = functools.partial(lax.dot_general, preferred_element_type=F32)
        chains, out_sel = [], []
        for b, d in [(b, d) for b in range(RWKV_GROUP) for d in range(2)]:
            reverse = d == 1
            c = nc - 1 - i if reverse else i
            rows = pl.ds(pl.multiple_of(c * CHUNK, CHUNK), CHUNK)
            out_sel.append((b, rows))
            strict = (s_idx > t_idx) if reverse else (s_idx < t_idx)
            incl = (s_idx >= t_idx) if reverse else (s_idx <= t_idx)
            last = 0 if reverse else CHUNK - 1
            r, k, v, kk = r_s[b, rows, :], k_s[b, rows, :], v_s[b, rows, :], kk_s[b, rows, :]
            lx = lx_s[b, rows, :]
            wl = w0_ref[d:d + 1, :] + _dot_bf16(lx[:, 0:DECAY_LORA], w2_ref[d])
            logw = -jnp.exp(_log_sigmoid(wl) - 0.5)
            a = _sigmoid(a0_ref[d:d + 1, :] + _dot_bf16(lx[:, DECAY_LORA:], a2_ref[d]))
            kd = k * (1.0 + (a - 1.0) * ka_ref[...])
            kb = kk * a
            tri = _tri(CHUNK, reverse, False).astype(BF16)
            w1, w_rest = _split_bf16(logw)
            w2, w3 = _split_bf16(w_rest)
            cum = dot(tri, w1, NN) + dot(tri, w2, NN) + dot(tri, w3, NN)
            clast = cum[last:last + 1]
            rt = r * jnp.exp(cum)
            at = -kk * jnp.exp(cum - logw)
            einv = jnp.exp(-cum)
            kt, bt = kd * einv, kb * einv
            edec = jnp.exp(clast - cum)
            kp, bp = kd * edec, kb * edec
            ptot = jnp.exp(clast)
            rkd = r * kd * rk_ref[...]
            for p in range(N_PACKS):
                cs = slice(p * PACK_W, (p + 1) * PACK_W)
                chains.append(dict(
                    out=len(out_sel) - 1, st=(b, d, p), cs=cs, strict=strict, incl=incl, v=v[:, cs],
                    rkd=rkd[:, cs], ptot=ptot[:, cs],
                    ar=jnp.concatenate([at[:, cs], rt[:, cs]], axis=0).astype(BF16),
                    kt_bd=_block_diag(kt[:, cs], ones_bd), bt_bd=_block_diag(bt[:, cs], ones_bd),
                    v_bd=_block_diag(v[:, cs], ones_bd),
                    kbp=jnp.concatenate([kp[:, cs], bp[:, cs]], axis=0)))

        for ch in chains:
            ch['s_prev'] = st_s[ch['st']]
            ch['a_k'] = dot(ch['ar'], ch['kt_bd'], NT)
            ch['a_b'] = dot(ch['ar'], ch['bt_bd'], NT)
            ch['ars'] = dot(ch['ar'], ch['s_prev'].astype(BF16), NT)
        for ch in chains:
            a_ak = jnp.where(ch['strict'], ch['a_k'][:CHUNK], 0.0)
            a_rk = jnp.where(ch['incl'], ch['a_k'][CHUNK:], 0.0)
            ch['a_rb'] = jnp.where(ch['incl'], ch['a_b'][CHUNK:], 0.0).astype(BF16)
            n = jnp.where(ch['strict'], ch['a_b'][:CHUNK], 0.0)
            ch['n_hi'], n_lo = _split_bf16(n)
            ch['n_both'] = jnp.concatenate([ch['n_hi'], n_lo], axis=0)
            ch['pw'] = ch['n_hi']
            ch['m'] = jnp.where(s_idx == t_idx, 1.0, 0.0) + n
            akv = dot(jnp.concatenate([a_ak, a_rk], axis=0).astype(BF16), ch['v_bd'], NN)
            ch['rhs'] = ch['ars'][:CHUNK] + akv[:CHUNK]
            ch['y'] = ch['ars'][CHUNK:] + akv[CHUNK:]
        for ch in chains:
            ch['pw'] = dot(ch['pw'], _block_diag(ch['pw'], ones_bd), NN).astype(BF16)
        for j in range(1, 5):
            for ch in chains:
                pw_bd = _block_diag(ch['pw'], ones_bd)
                if j < 4:
                    both = dot(jnp.concatenate([ch['m'].astype(BF16), ch['pw']], axis=0), pw_bd, NN)
                    ch['m'] = ch['m'] + both[:CHUNK]
                    ch['pw'] = both[CHUNK:].astype(BF16)
                else:
                    ch['m'] = ch['m'] + dot(ch['m'].astype(BF16), pw_bd, NN)
        for ch in chains:
            ch['mb'] = ch['m'].astype(BF16)
            ch['u'] = dot(ch['mb'], _block_diag(ch['rhs'], ones_bd), NN)
        for ch in chains:
            u_hi, u_lo = _split_bf16(ch['u'])
            both = dot(ch['n_both'], _block_diag(u_hi, ones_bd), NN)
            nu = both[:CHUNK] + both[CHUNK:] + dot(ch['n_hi'], _block_diag(u_lo, ones_bd), NN)
            ch['resid'] = ch['rhs'] - ch['u'] + nu
        for ch in chains:
            ch['u'] = ch['u'] + dot(ch['mb'], _block_diag(ch['resid'], ones_bd), NN)
        for ch in chains:
            u = ch['u']
            ch['y'] = ch['y'] + dot(ch['a_rb'], _block_diag(u, ones_bd), NN)
            vu_hi, vu_lo = _split_bf16(jnp.concatenate([ch['v'], u], axis=0))
            kbp_hi, kbp_lo = _split_bf16(ch['kbp'])
            grown = dot(vu_hi, kbp_hi, TN) + dot(vu_hi, kbp_lo, TN) + dot(vu_lo, kbp_hi, TN)
            st_s[ch['st']] = ch['s_prev'] * ch['ptot'] + grown * ones_bd.astype(F32)
        n_ch = len(chains)
        sums = dot(jnp.concatenate([ch['y'] for ch in chains] + [ch['rkd'] for ch in chains],
                                   axis=0).astype(BF16), ones_bd, NN)
        ycs = [ch['y'] - sums[q * CHUNK:(q + 1) * CHUNK] * (1.0 / HEAD_A) for q, ch in enumerate(chains)]
        var = dot(jnp.concatenate([yc * yc for yc in ycs], axis=0).astype(BF16), ones_bd, NN) * (1.0 / HEAD_A)
        outs = [[] for _ in out_sel]
        for q, ch in enumerate(chains):
            cs = ch['cs']
            out = ycs[q] * lax.rsqrt(var[q * CHUNK:(q + 1) * CHUNK] + GN_EPS) * gnw_ref[:, cs] + gnb_ref[:, cs]
            outs[ch['out']].append(out + sums[(n_ch + q) * CHUNK:(n_ch + q + 1) * CHUNK] * ch['v'])
        for (b, rows), parts in zip(out_sel, outs):
            out = jnp.concatenate(parts, axis=1)

            @pl.when(first_touch)
            def _(b=b, rows=rows, out=out):
                y_ref[b, rows, :] = out

            @pl.when(jnp.logical_not(first_touch))
            def _(b=b, rows=rows, out=out):
                y_ref[b, rows, :] = (y_ref[b, rows, :] + out) * gate_s[b, rows, :]
        return carry

    lax.fori_loop(0, nc, scan_step, 0)

    for b in range(RWKV_GROUP):
        for d in range(2):
            for p in range(N_PACKS):
                s_fin = st_s[b, d, p]
                for h in range(PACK):
                    sout_ref[b, d, p * PACK + h] = s_fin[h * HEAD_A:(h + 1) * HEAD_A, h * HEAD_A:(h + 1) * HEAD_A]


def _rwkv(zr, s0, p, grid):
    bsz, seq_len, _ = zr.shape
    assert (seq_len // CHUNK) % 2 == 0
    kern = functools.partial(_rwkv_kernel, seq_len=seq_len, grid=grid)
    consts = [p['mu'], p['kkw'], p['ka'], p['rk'], p['gnw'], p['gnb'], p['w0'], p['w2'], p['a0'], p['a2'],
              p['g2'], p['ones_bd']]
    assert bsz % RWKV_GROUP == 0
    return pl.pallas_call(
        kern,
        grid=(bsz // RWKV_GROUP,),
        in_specs=[pl.BlockSpec((RWKV_GROUP, seq_len, RWKV_COLS), lambda b: (b, 0, 0),
                               pipeline_mode=pl.Buffered(1)),
                  pl.BlockSpec((RWKV_GROUP, 2, H_A, HEAD_A, HEAD_A), lambda b: (b, 0, 0, 0, 0))]
                 + [_const_spec(a.shape) for a in consts],
        out_specs=[pl.BlockSpec((RWKV_GROUP, seq_len, DA), lambda b: (b, 0, 0)),
                   pl.BlockSpec((RWKV_GROUP, 2, H_A, HEAD_A, HEAD_A), lambda b: (b, 0, 0, 0, 0))],
        out_shape=[jax.ShapeDtypeStruct((bsz, seq_len, DA), F32),
                   jax.ShapeDtypeStruct((bsz, 2, H_A, HEAD_A, HEAD_A), F32)],
        scratch_shapes=[pltpu.VMEM((RWKV_GROUP, seq_len, DA), F32)] * 5
                       + [pltpu.VMEM((RWKV_GROUP, seq_len, DECAY_LORA + AAA_LORA), F32),
                          pltpu.VMEM((RWKV_GROUP, 2, N_PACKS, PACK_W, PACK_W), F32)],
        compiler_params=_params("arbitrary"),
        name="rwkv_scan",
    )(zr, s0, *consts)


def _mlstm_kernel(zm_ref, g_ref, gt_ref, c0_ref, n0_ref, m0_ref, conv_ref, bi_ref, bit_ref, bf_ref, bft_ref,
                  ng_ref, erep_ref, y_ref, cout_ref, nout_ref, mout_ref,
                  qk_s, h_s, c_s, n_s, m_s, *, seq_len):
    nc = seq_len // CHUNK

    def conv_chunk(c, carry):
        for b in range(MLSTM_GROUP):
            zc, prev, nxt = _seq_neighbours(zm_ref, b, c, nc, slice(0, 2 * DB))
            u = conv_ref[0:1, :] * prev + conv_ref[1:2, :] * zc + conv_ref[2:3, :] * nxt
            qk = u * _sigmoid(u)
            lane = lax.broadcasted_iota(jnp.int32, qk.shape, 1)
            qk_s[b, pl.ds(pl.multiple_of(c * CHUNK, CHUNK), CHUNK), :] = jnp.where(lane >= DB,
                                                                                  qk * (HEAD_B ** -0.5), qk)
        return carry

    lax.fori_loop(0, nc, conv_chunk, 0)

    for b in range(MLSTM_GROUP):
        for d in range(2):
            c_s[b, d] = c0_ref[b, d]
            n_s[b, d] = n0_ref[b, d]
            m_s[b, d] = jnp.broadcast_to(m0_ref[b, d], (H_B, LANES))

    e_rep = erep_ref[...]
    ones_tl = jnp.ones((CHUNK, LANES), BF16)
    row_id = lax.broadcasted_iota(jnp.int32, (CHUNK, LANES), 0)

    def scan_max(x, reverse):
        shift = 1
        while shift < CHUNK:
            if reverse:
                moved = jnp.where(row_id < CHUNK - shift, pltpu.roll(x, CHUNK - shift, 0), -jnp.inf)
            else:
                moved = jnp.where(row_id >= shift, pltpu.roll(x, shift, 0), -jnp.inf)
            x = jnp.maximum(x, moved)
            shift *= 2
        return x

    def dot3(a, b, dims):
        a_hi, a_lo = _split_bf16(a)
        b_hi, b_lo = _split_bf16(b)
        dg = functools.partial(lax.dot_general, dimension_numbers=dims, preferred_element_type=F32)
        return dg(a_hi, b_hi) + dg(a_hi, b_lo) + dg(a_lo, b_hi)

    def scan_step(i, carry):
        dot = functools.partial(lax.dot_general, preferred_element_type=F32)
        chains = []
        for b, d in [(b, d) for b in range(MLSTM_GROUP) for d in range(2)]:
            reverse = d == 1
            incl = _tri(CHUNK, reverse, False)
            tri_b = incl.astype(BF16)
            tri_row = _tri(CHUNK, not reverse, False).astype(F32)
            last = 0 if reverse else CHUNK - 1
            gsl = slice(d * H_B, (d + 1) * H_B)
            fsl = slice(2 * H_B + d * H_B, 2 * H_B + (d + 1) * H_B)
            c = nc - 1 - i if reverse else i
            rows = pl.ds(pl.multiple_of(c * CHUNK, CHUNK), CHUNK)
            g1, g_rest = _split_bf16(g_ref[b, rows, :])
            g2, g3 = _split_bf16(g_rest)
            g_rep = dot(g1, e_rep, NN) + dot(g2, e_rep, NN) + dot(g3, e_rep, NN)
            grow = gt_ref[b, c]
            i_row = grow[gsl, :] + bit_ref[gsl, :]
            f_row = _log_sigmoid(grow[fsl, :] + bft_ref[gsl, :])
            b_row = _mm(f_row, tri_row)
            for h in range(H_B):
                ji, jf = d * H_B + h, 2 * H_B + d * H_B + h
                ic = g_rep[:, ji * LANES:(ji + 1) * LANES] + bi_ref[:, ji:ji + 1]
                fc = _log_sigmoid(g_rep[:, jf * LANES:(jf + 1) * LANES] + bf_ref[:, ji:ji + 1])
                f1, f_rest = _split_bf16(fc)
                f2, f3 = _split_bf16(f_rest)
                bc = dot(tri_b, f1, NN) + dot(tri_b, f2, NN) + dot(tri_b, f3, NN)
                q = qk_s[b, rows, h * HEAD_B:(h + 1) * HEAD_B]
                chains.append(dict(
                    b=b, d=d, h=h, rows=rows, incl=incl, reverse=reverse, q=q, qb=q.astype(BF16),
                    k=qk_s[b, rows, DB + h * HEAD_B:DB + (h + 1) * HEAD_B],
                    v=zm_ref[b, rows, 2 * DB + h * HEAD_B:2 * DB + (h + 1) * HEAD_B],
                    bc=bc, ic=ic, bl=bc[last:last + 1, :], br=b_row[h:h + 1, :], ir=i_row[h:h + 1, :]))
        for ch in chains:
            b, d, h = ch['b'], ch['d'], ch['h']
            ch['qk'] = _dot_bf16(ch['qb'], ch['k'], NT)
            ch['c_prev'] = c_s[b, d, h]
            ch['qc'] = _dot_bf16(ch['qb'], ch['c_prev'])
            ch['n_prev'] = n_s[b, d, h:h + 1, :]
            ch['qn'] = dot3(ch['q'], jnp.broadcast_to(ch['n_prev'], (HEAD_B, HEAD_B)), NT)
        for ch in chains:
            b, d, h = ch['b'], ch['d'], ch['h']
            bc, ic, bl = ch['bc'], ch['ic'], ch['bl']
            m_prev = m_s[b, d, h:h + 1, :]
            m_t = bc + jnp.maximum(m_prev, scan_max(ic - bc, ch['reverse']))
            dmat = jnp.where(ch['incl'], bc[:, :CHUNK] - ch['br'] + ch['ir'], -jnp.inf)
            s = ch['qk'] * jnp.exp(dmat - m_t[:, :CHUNK])
            carry_w = jnp.exp(bc + m_prev - m_t)
            s_hi, s_lo = _split_bf16(s)
            den = dot(s_hi, ones_tl, NN) + dot(s_lo, ones_tl, NN) + carry_w * ch['qn']
            num = dot(s_hi, ch['v'].astype(BF16), NN) + carry_w * ch['qc']
            h_s[b, d, ch['rows'], h * HEAD_B:(h + 1) * HEAD_B] = num / jnp.maximum(jnp.abs(den), jnp.exp(-m_t))
            g = bl - bc + ic
            m_new = jnp.maximum(bl + m_prev, jnp.max(g, axis=0, keepdims=True))
            wk = jnp.exp(g - m_new)
            decay = jnp.exp(bl + m_prev - m_new)
            c_s[b, d, h] = decay * ch['c_prev'] + _dot_bf16(ch['k'], wk * ch['v'], TN)
            n_s[b, d, h:h + 1, :] = decay * ch['n_prev'] + jnp.sum(wk * ch['k'], axis=0, keepdims=True)
            m_s[b, d, h:h + 1, :] = m_new
        return carry

    lax.fori_loop(0, nc, scan_step, 0)
    cout_ref[...] = c_s[...]
    nout_ref[...] = n_s[...]
    mout_ref[...] = m_s[...]

    def norm_chunk(c, carry):
        rows = pl.ds(pl.multiple_of(c * CHUNK, CHUNK), CHUNK)
        for b in range(MLSTM_GROUP):
            og = zm_ref[b, rows, 3 * DB:4 * DB]
            hm = h_s[b, 0, rows, :] + h_s[b, 1, rows, :]
            parts = []
            for h in range(H_B):
                hh = hm[:, h * HEAD_B:(h + 1) * HEAD_B]
                parts.append(hh * lax.rsqrt(jnp.mean(hh * hh, axis=-1, keepdims=True) + NORM_EPS))
            y_ref[b, rows, :] = jnp.concatenate(parts, axis=1) * ng_ref[...] * _sigmoid(og)
        return carry

    lax.fori_loop(0, nc, norm_chunk, 0)


def _mlstm(zm, g, gt, c0, n0, m0, p):
    bsz, seq_len, _ = zm.shape
    nc = seq_len // CHUNK
    kern = functools.partial(_mlstm_kernel, seq_len=seq_len)
    consts = [p['conv'], p['bi'], p['bit'], p['bf'], p['bft'], p['ng'], p['e_rep']]
    grp = MLSTM_GROUP
    assert bsz % grp == 0
    return pl.pallas_call(
        kern,
        grid=(bsz // grp,),
        in_specs=[pl.BlockSpec((grp, seq_len, MLSTM_MAIN), lambda b: (b, 0, 0), pipeline_mode=pl.Buffered(1)),
                  pl.BlockSpec((grp, seq_len, N_GATES), lambda b: (b, 0, 0)),
                  pl.BlockSpec((grp, nc, N_GATES, CHUNK), lambda b: (b, 0, 0, 0)),
                  pl.BlockSpec((grp, 2, H_B, HEAD_B, HEAD_B), lambda b: (b, 0, 0, 0, 0)),
                  pl.BlockSpec((grp, 2, H_B, HEAD_B), lambda b: (b, 0, 0, 0)),
                  pl.BlockSpec((grp, 2, H_B, 1), lambda b: (b, 0, 0, 0))]
                 + [_const_spec(a.shape) for a in consts],
        out_specs=[pl.BlockSpec((grp, seq_len, DB), lambda b: (b, 0, 0)),
                   pl.BlockSpec((grp, 2, H_B, HEAD_B, HEAD_B), lambda b: (b, 0, 0, 0, 0)),
                   pl.BlockSpec((grp, 2, H_B, HEAD_B), lambda b: (b, 0, 0, 0)),
                   pl.BlockSpec((grp, 2, H_B, LANES), lambda b: (b, 0, 0, 0))],
        out_shape=[jax.ShapeDtypeStruct((bsz, seq_len, DB), F32),
                   jax.ShapeDtypeStruct((bsz, 2, H_B, HEAD_B, HEAD_B), F32),
                   jax.ShapeDtypeStruct((bsz, 2, H_B, HEAD_B), F32),
                   jax.ShapeDtypeStruct((bsz, 2, H_B, LANES), F32)],
        scratch_shapes=[pltpu.VMEM((grp, seq_len, 2 * DB), F32), pltpu.VMEM((grp, 2, seq_len, DB), F32),
                        pltpu.VMEM((grp, 2, H_B, HEAD_B, HEAD_B), F32), pltpu.VMEM((grp, 2, H_B, HEAD_B), F32),
                        pltpu.VMEM((grp, 2, H_B, LANES), F32)],
        compiler_params=_params("arbitrary"),
        name="mlstm_scan",
    )(zm, g, gt, c0, n0, m0, *consts)


def _outproj_kernel(x_ref, yr_ref, ym_ref, mod_ref, g_ref, wo_ref, rw_ref, rb_ref,
                    x1_ref, hn_ref, idx_ref, gate_ref):
    mod = mod_ref[0]
    g1 = mod[:, 2 * D_MODEL:3 * D_MODEL]
    sh2, sc2 = mod[:, 3 * D_MODEL:4 * D_MODEL], mod[:, 4 * D_MODEL:5 * D_MODEL]
    mix = _dot_bf16(yr_ref[...], wo_ref[0:DA, :]) + _dot_bf16(ym_ref[...], wo_ref[DA:, :])
    x1 = x_ref[...] + g1 * mix
    x1_ref[...] = x1
    y = x1 * lax.rsqrt(jnp.mean(x1 * x1, axis=-1, keepdims=True) + NORM_EPS)
    hn = y * g_ref[...] * (1.0 + sc2) + sh2
    hn_ref[...] = hn
    logits = _dot_bf16(hn, rw_ref[...]) + rb_ref[...]
    lane = lax.broadcasted_iota(jnp.int32, logits.shape, 1)
    vals, idxs = [], []
    for _ in range(TOP_K):
        top = jnp.max(logits, axis=-1, keepdims=True)
        pick = jnp.min(jnp.where(logits == top, lane, N_EXPERTS), axis=-1, keepdims=True)
        vals.append(top)
        idxs.append(pick)
        logits = jnp.where(lane == pick, -jnp.inf, logits)
    exps = [jnp.exp(v - vals[0]) for v in vals]
    total = exps[0] + exps[1] + exps[2] + exps[3]
    k_lane = lax.broadcasted_iota(jnp.int32, (x1.shape[0], TOP_K), 1)
    gates = jnp.zeros((x1.shape[0], TOP_K), F32)
    picks = jnp.zeros((x1.shape[0], TOP_K), jnp.int32)
    for j in range(TOP_K):
        gates = jnp.where(k_lane == j, exps[j] / total, gates)
        picks = jnp.where(k_lane == j, idxs[j], picks)
    idx_ref[...] = picks
    gate_ref[...] = gates


def _outproj(x2, yr, ym, mod, norm_g, w_out, router_w, router_b):
    rows = x2.shape[0]
    tiles_per_mod = rows // mod.shape[0] // PROJ_TILE
    row_spec = lambda w: pl.BlockSpec((PROJ_TILE, w), lambda i: (i, 0))
    return pl.pallas_call(
        _outproj_kernel,
        grid=(rows // PROJ_TILE,),
        in_specs=[row_spec(D_MODEL), row_spec(DA), row_spec(DB),
                  pl.BlockSpec((1, 1, 6 * D_MODEL), lambda i: (i // tiles_per_mod, 0, 0)),
                  _const_spec((1, D_MODEL)), _const_spec(w_out.shape), _const_spec(router_w.shape),
                  _const_spec((1, N_EXPERTS))],
        out_specs=[row_spec(D_MODEL), row_spec(D_MODEL), row_spec(TOP_K), row_spec(TOP_K)],
        out_shape=[jax.ShapeDtypeStruct((rows, D_MODEL), F32), jax.ShapeDtypeStruct((rows, D_MODEL), F32),
                   jax.ShapeDtypeStruct((rows, TOP_K), jnp.int32), jax.ShapeDtypeStruct((rows, TOP_K), F32)],
        compiler_params=_params("arbitrary"),
        name="outproj_router",
    )(x2, yr, ym, mod, norm_g.reshape(1, D_MODEL), w_out, router_w, router_b.reshape(1, N_EXPERTS))


def _rank_kernel(idx_ref, rank_ref, count_ref, run_s):
    i = pl.program_id(0)

    @pl.when(i == 0)
    def _():
        run_s[...] = jnp.zeros_like(run_s)

    idx = idx_ref[...]
    rows = idx.shape[0]
    lane = lax.broadcasted_iota(jnp.int32, (rows, N_EXPERTS), 1)
    hot = jnp.zeros((rows, N_EXPERTS), F32)
    for j in range(TOP_K):
        hot = hot + jnp.where(lane == idx[:, j:j + 1], 1.0, 0.0)
    before = _dot_bf16(_tri(rows, False, True).astype(F32), hot) + run_s[...]
    k_lane = lax.broadcasted_iota(jnp.int32, (rows, TOP_K), 1)
    rank = jnp.zeros((rows, TOP_K), F32)
    for j in range(TOP_K):
        rj = jnp.sum(jnp.where(lane == idx[:, j:j + 1], before, 0.0), axis=-1, keepdims=True)
        rank = jnp.where(k_lane == j, rj, rank)
    rank_ref[...] = rank.astype(jnp.int32)
    run_s[...] = run_s[...] + jnp.sum(hot, axis=0, keepdims=True)
    count_ref[...] = run_s[...].astype(jnp.int32)


def _expert_ranks(idx):
    rows = idx.shape[0]
    return pl.pallas_call(
        _rank_kernel,
        grid=(rows // ROW_TILE,),
        in_specs=[pl.BlockSpec((ROW_TILE, TOP_K), lambda i: (i, 0))],
        out_specs=[pl.BlockSpec((ROW_TILE, TOP_K), lambda i: (i, 0)), _const_spec((1, N_EXPERTS))],
        out_shape=[jax.ShapeDtypeStruct((rows, TOP_K), jnp.int32),
                   jax.ShapeDtypeStruct((1, N_EXPERTS), jnp.int32)],
        scratch_shapes=[pltpu.VMEM((1, N_EXPERTS), F32)],
        compiler_params=_params("arbitrary"),
        name="expert_ranks",
    )(idx)


def _dispatch_kernel(dest_ref, pend_ref, x_ref, xs_ref, zero_s, sem):
    @pl.when(pl.program_id(0) == 0)
    def _():
        zero_s[...] = jnp.zeros_like(zero_s)

        def zero_block(start):
            return pltpu.make_async_copy(zero_s, xs_ref.at[pl.ds(pl.multiple_of(start, MOE_BLOCK), MOE_BLOCK)], sem)

        def has_rows(e):
            return pend_ref[e] > (pend_ref[e - 1] if e else 0)

        used_end = pend_ref[N_EXPERTS - 1]
        n_tail = (xs_ref.shape[0] - used_end) // MOE_BLOCK

        def start_tail(b, carry):
            zero_block(used_end + b * MOE_BLOCK).start()
            return carry

        def wait_tail(b, carry):
            zero_block(0).wait()
            return carry

        for e in range(N_EXPERTS):
            @pl.when(has_rows(e))
            def _(e=e):
                zero_block(pend_ref[e] - MOE_BLOCK).start()
        lax.fori_loop(0, n_tail, start_tail, 0)
        for e in range(N_EXPERTS):
            @pl.when(has_rows(e))
            def _():
                zero_block(0).wait()
        lax.fori_loop(0, n_tail, wait_tail, 0)

    def issue(g, carry):
        for q in range(DMA_UNROLL):
            r = g * DMA_UNROLL + q
            for j in range(TOP_K):
                slot = dest_ref[r * TOP_K + j]
                pltpu.make_async_copy(x_ref.at[pl.ds(r, 1)], xs_ref.at[pl.ds(slot, 1)], sem).start()
        return carry

    lax.fori_loop(0, ROW_TILE // DMA_UNROLL, issue, 0)
    all_rows = xs_ref.at[pl.ds(0, ROW_TILE * TOP_K)]
    pltpu.make_async_copy(all_rows, all_rows, sem).wait()


def _dispatch(dest_flat, pad_end, x2, n_slots):
    rows = x2.shape[0]
    return pl.pallas_call(
        _dispatch_kernel,
        grid=(rows // ROW_TILE,),
        in_specs=[pl.BlockSpec((ROW_TILE * TOP_K,), lambda i: (i,), memory_space=pltpu.SMEM),
                  pl.BlockSpec(memory_space=pltpu.SMEM),
                  pl.BlockSpec((ROW_TILE, D_MODEL), lambda i: (i, 0))],
        out_specs=pl.BlockSpec(memory_space=pl.ANY),
        out_shape=jax.ShapeDtypeStruct((n_slots, D_MODEL), x2.dtype),
        scratch_shapes=[pltpu.VMEM((MOE_BLOCK, D_MODEL), F32), pltpu.SemaphoreType.DMA(())],
        compiler_params=_params("arbitrary", disable_bounds_checks=True),
        name="moe_dispatch",
    )(dest_flat, pad_end, x2)


def _expert_kernel(be_ref, nb_ref, xs_ref, w1_ref, b1_ref, w2_ref, b2_ref, y_ref, w1_s, w2_s):
    i = pl.program_id(0)
    changed = jnp.logical_or(i == 0, be_ref[i] != be_ref[jnp.maximum(i - 1, 0)])

    @pl.when(jnp.logical_and(changed, i < nb_ref[0]))
    def _():
        w1_s[...] = w1_ref[0].astype(BF16)
        w2_s[...] = w2_ref[0].astype(BF16)

    @pl.when(i < nb_ref[0])
    def _():
        xb = xs_ref[...].astype(BF16)
        hu = lax.dot_general(xb, w1_s[...], NN, preferred_element_type=F32) + b1_ref[0]
        glu = jnp.minimum(hu[:, :D_FF], SWIGLU_LIMIT)
        lin = jnp.clip(hu[:, D_FF:], -SWIGLU_LIMIT, SWIGLU_LIMIT)
        act = glu * _sigmoid(SWIGLU_ALPHA * glu) * (lin + 1.0)
        y_ref[...] = lax.dot_general(act.astype(BF16), w2_s[...], NN, preferred_element_type=F32) + b2_ref[0]

    @pl.when(i >= nb_ref[0])
    def _():
        y_ref[...] = jnp.zeros_like(y_ref)


def _experts(block_e, n_used, xs, w1, b1, w2, b2):
    n_blocks = xs.shape[0] // MOE_BLOCK
    grid_spec = pltpu.PrefetchScalarGridSpec(
        num_scalar_prefetch=2,
        grid=(n_blocks,),
        in_specs=[pl.BlockSpec((MOE_BLOCK, D_MODEL), lambda i, be, nb: (jnp.minimum(i, nb[0] - 1), 0)),
                  pl.BlockSpec((1, D_MODEL, 2 * D_FF), lambda i, be, nb: (be[i], 0, 0)),
                  pl.BlockSpec((1, 1, 2 * D_FF), lambda i, be, nb: (be[i], 0, 0)),
                  pl.BlockSpec((1, D_FF, D_MODEL), lambda i, be, nb: (be[i], 0, 0)),
                  pl.BlockSpec((1, 1, D_MODEL), lambda i, be, nb: (be[i], 0, 0))],
        out_specs=pl.BlockSpec((MOE_BLOCK, D_MODEL), lambda i, be, nb: (i, 0)),
        scratch_shapes=[pltpu.VMEM((D_MODEL, 2 * D_FF), BF16), pltpu.VMEM((D_FF, D_MODEL), BF16)],
    )
    return pl.pallas_call(
        _expert_kernel,
        grid_spec=grid_spec,
        out_shape=jax.ShapeDtypeStruct(xs.shape, F32),
        compiler_params=_params("arbitrary"),
        name="moe_experts",
    )(block_e, n_used, xs, w1, b1.reshape(N_EXPERTS, 1, 2 * D_FF), w2, b2.reshape(N_EXPERTS, 1, D_MODEL))


def _combine_kernel(dest_ref, yb_ref, x1_ref, gate_ref, mod_ref, fg_ref, o_ref, buf, sem):
    def issue(g, carry):
        for q in range(DMA_UNROLL):
            r = g * DMA_UNROLL + q
            for j in range(TOP_K):
                slot = dest_ref[r * TOP_K + j]
                pltpu.make_async_copy(yb_ref.at[pl.ds(slot, 1)], buf.at[j, pl.ds(r, 1)], sem).start()
        return carry

    lax.fori_loop(0, ROW_TILE // DMA_UNROLL, issue, 0)
    pltpu.make_async_copy(buf, buf, sem).wait()

    g2 = mod_ref[0][:, 5 * D_MODEL:6 * D_MODEL]
    gates = gate_ref[...]
    moe = jnp.zeros((ROW_TILE, D_MODEL), F32)
    for j in range(TOP_K):
        moe = moe + buf[j] * gates[:, j:j + 1]
    x2 = x1_ref[...] + g2 * moe
    y = x2 * lax.rsqrt(jnp.mean(x2 * x2, axis=-1, keepdims=True) + NORM_EPS)
    o_ref[...] = y * fg_ref[...]


def _combine(dest_flat, yb, x1, gates, mod, final_g, row_offset):
    rows = x1.shape[0]
    tiles_per_mod = rows // mod.shape[0] // ROW_TILE
    tile_offset = row_offset // ROW_TILE
    row_spec = pl.BlockSpec((ROW_TILE, D_MODEL), lambda i: (i, 0))
    return pl.pallas_call(
        _combine_kernel,
        grid=(rows // ROW_TILE,),
        in_specs=[pl.BlockSpec((ROW_TILE * TOP_K,), lambda i: (i + tile_offset,), memory_space=pltpu.SMEM),
                  pl.BlockSpec(memory_space=pl.ANY),
                  row_spec,
                  pl.BlockSpec((ROW_TILE, TOP_K), lambda i: (i, 0)),
                  pl.BlockSpec((1, 1, 6 * D_MODEL), lambda i: (i // tiles_per_mod, 0, 0)),
                  _const_spec((1, D_MODEL))],
        out_specs=row_spec,
        out_shape=jax.ShapeDtypeStruct((rows, D_MODEL), F32),
        scratch_shapes=[pltpu.VMEM((TOP_K, ROW_TILE, D_MODEL), F32), pltpu.SemaphoreType.DMA(())],
        compiler_params=_params("arbitrary", disable_bounds_checks=True),
        name="moe_combine",
    )(dest_flat, yb, x1, gates, mod, final_g.reshape(1, D_MODEL))


def _block_diag_ones(width, block):
    i = jnp.arange(width) // block
    return (i[:, None] == i[None, :]).astype(F32)


def _mixers(x, mod, lp, s0, c0, n0, m0, grid):
    bsz, seq_len, _ = x.shape
    nc = seq_len // CHUNK
    x2 = x.reshape(bsz * seq_len, D_MODEL)
    zr, zm, zg, zgt = _inproj(x2, mod, lp['norm1_g'], lp['w_r'], lp['w_m'], lp['w_g'], lp['w_gt'])
    gt = zgt.reshape(N_GATES, bsz, nc, CHUNK).transpose(1, 2, 0, 3)
    yr, s_new = _rwkv(zr.reshape(bsz, seq_len, RWKV_COLS), s0, lp, grid)
    ym, c_new, n_new, m_new = _mlstm(zm.reshape(bsz, seq_len, MLSTM_MAIN), zg.reshape(bsz, seq_len, N_GATES), gt,
                                     c0, n0, m0.reshape(bsz, 2, H_B, 1), lp)
    x1, hn, idx, gates = _outproj(x2, yr.reshape(-1, DA), ym.reshape(-1, DB), mod, lp['norm2_g'], lp['w_out'],
                                  lp['router_w'], lp['router_b'])
    return x1, hn, idx, gates, (s_new, c_new, n_new, m_new[..., 0])


def kernel(x_prompt, x_sample, state_rwkv, state_mlstm_C, state_mlstm_n, state_mlstm_m, c, c_ctx, ada_w, ada_b, norm1_g, norm2_g, w_in, w_out, rwkv_mu, rwkv_w0, rwkv_w2, rwkv_a0, rwkv_a2, rwkv_g2, rwkv_kk, rwkv_ka, rwkv_rk, rwkv_gn_w, rwkv_gn_b, mlstm_conv, mlstm_bi, mlstm_bf, mlstm_norm_g, router_w, router_b, moe_w1, moe_b1, moe_w2, moe_b2, final_g):
    bp, lp_len, _ = x_prompt.shape
    bs, ls_len, _ = x_sample.shape
    w = w_in[0]
    lp = {
        'norm1_g': norm1_g[0], 'norm2_g': norm2_g[0],
        'w_r': w[:, :RWKV_COLS].astype(BF16),
        'w_m': w[:, RWKV_COLS:RWKV_COLS + MLSTM_MAIN].astype(BF16),
        'w_g': w[:, RWKV_COLS + MLSTM_MAIN:],
        'w_gt': w[:, RWKV_COLS + MLSTM_MAIN:].T,
        'w_out': w_out[0].astype(BF16),
        'mu': rwkv_mu[0].reshape(1, RWKV_COLS), 'kkw': rwkv_kk[0].reshape(1, DA), 'ka': rwkv_ka[0].reshape(1, DA),
        'rk': rwkv_rk[0].reshape(1, DA), 'gnw': rwkv_gn_w[0].reshape(1, DA), 'gnb': rwkv_gn_b[0].reshape(1, DA),
        'w0': rwkv_w0[0], 'w2': rwkv_w2[0], 'a0': rwkv_a0[0], 'a2': rwkv_a2[0], 'g2': rwkv_g2[0],
        'ones_bd': _block_diag_ones(PACK_W, HEAD_A).astype(BF16),
        'conv': mlstm_conv[0], 'bi': mlstm_bi[0].reshape(1, 2 * H_B), 'bit': mlstm_bi[0].reshape(2 * H_B, 1),
        'bf': mlstm_bf[0].reshape(1, 2 * H_B), 'bft': mlstm_bf[0].reshape(2 * H_B, 1),
        'ng': mlstm_norm_g[0].reshape(1, DB),
        'e_rep': jnp.repeat(jnp.eye(N_GATES, dtype=BF16), LANES, axis=1),
        'router_w': router_w[0], 'router_b': router_b[0],
    }
    cc = jnp.concatenate([c_ctx[None, :], c, jnp.zeros((2 * SUBLANES - 1 - bs, D_MODEL), F32)], axis=0)
    mod = _ada_mod(cc, ada_w[0], ada_b[0])
    mod_p = mod[0:1].reshape(1, 1, 6 * D_MODEL)
    mod_s = mod[1:1 + bs].reshape(bs, 1, 6 * D_MODEL)

    zeros = lambda *shape: jnp.zeros(shape, F32)
    x1p, hnp, idxp, gatesp, st = _mixers(x_prompt, mod_p, lp, zeros(bp, 2, H_A, HEAD_A, HEAD_A),
                                         zeros(bp, 2, H_B, HEAD_B, HEAD_B), zeros(bp, 2, H_B, HEAD_B),
                                         zeros(bp, 2, H_B), False)
    x1s, hns, idxs, gatess, _ = _mixers(x_sample, mod_s, lp, state_rwkv[:, 0], state_mlstm_C[:, 0],
                                        state_mlstm_n[:, 0], state_mlstm_m[:, 0], True)

    n_p = bp * lp_len
    idx = jnp.concatenate([idxp, idxs], axis=0)
    hn = jnp.concatenate([hnp, hns], axis=0)
    n_tok = idx.shape[0]
    rank, counts = _expert_ranks(idx)
    counts = counts[0]
    padded = (counts + MOE_BLOCK - 1) // MOE_BLOCK * MOE_BLOCK
    pad_end = jnp.cumsum(padded)
    pad_start = pad_end - padded
    n_blocks = n_tok * TOP_K // MOE_BLOCK + N_EXPERTS
    block_start = jnp.arange(n_blocks, dtype=jnp.int32) * MOE_BLOCK
    block_e = jnp.minimum(jnp.sum(pad_end[None, :] <= block_start[:, None], axis=1), N_EXPERTS - 1).astype(jnp.int32)
    n_used = (pad_end[-1:] // MOE_BLOCK).astype(jnp.int32)
    dest = (pad_start[idx] + rank).astype(jnp.int32).reshape(-1)
    xs = _dispatch(dest, pad_end.astype(jnp.int32), hn, n_blocks * MOE_BLOCK)
    yb = _experts(block_e, n_used, xs, moe_w1[0], moe_b1[0], moe_w2[0], moe_b2[0])
    y_prompt = _combine(dest, yb, x1p, gatesp, mod_p, final_g, 0).reshape(x_prompt.shape)
    y_sample = _combine(dest, yb, x1s, gatess, mod_s, final_g, n_p).reshape(x_sample.shape)

    s_new, c_new, n_new, m_new = st
    return (y_prompt, y_sample, s_new[:, None], c_new[:, None], n_new[:, None], m_new[:, None])
```

```python
import functools

import jax
import jax.numpy as jnp
from jax import lax
from jax.experimental import pallas as pl
from jax.experimental.pallas import tpu as pltpu

F32 = jnp.float32
BF16 = jnp.bfloat16
HIGHEST = lax.Precision.HIGHEST

D_MODEL = 1024
DA = 512
HEAD_A = 64
H_A = DA // HEAD_A
DB = 512
H_B = 4
HEAD_B = DB // H_B
DECAY_LORA = 64
AAA_LORA = 64
GATE_LORA = 128
RWKV_COLS = 3 * DA + DECAY_LORA + AAA_LORA + GATE_LORA
MLSTM_MAIN = 4 * DB
N_GATES = 4 * H_B
GRID_W = 64
CHUNK = 64
N_EXPERTS = 32
TOP_K = 4
D_FF = D_MODEL
SWIGLU_LIMIT = 7.0
SWIGLU_ALPHA = 1.702
MOE_BLOCK = 512
NORM_EPS = 1e-6
GN_EPS = 64e-5
ROW_TILE = 256
PROJ_TILE = 512
SUBLANES = 8
LANES = 128
LANE_ROWS = D_MODEL // LANES
VMEM_LIMIT = 56 * 1024 * 1024
PACK = 4
PACK_W = PACK * HEAD_A
N_PACKS = H_A // PACK
RWKV_GROUP = 2
MLSTM_GROUP = 2

NN = (((1,), (0,)), ((), ()))
NT = (((1,), (1,)), ((), ()))
TN = (((0,), (0,)), ((), ()))


def _mm(a, b, precision=HIGHEST):
    return lax.dot_general(a, b, NN, precision=precision, preferred_element_type=F32)


def _mm_nt(a, b, precision=HIGHEST):
    return lax.dot_general(a, b, NT, precision=precision, preferred_element_type=F32)


def _mm_tn(a, b, precision=HIGHEST):
    return lax.dot_general(a, b, TN, precision=precision, preferred_element_type=F32)


def _dot_bf16(a, b, dims=NN):
    return lax.dot_general(a.astype(BF16), b.astype(BF16), dims, preferred_element_type=F32)


def _sigmoid(x):
    return 1.0 / (1.0 + jnp.exp(-x))


def _log_sigmoid(x):
    return jnp.minimum(x, 0.0) - jnp.log(1.0 + jnp.exp(-jnp.abs(x)))


def _params(*sem, **kw):
    return pltpu.CompilerParams(dimension_semantics=sem, vmem_limit_bytes=VMEM_LIMIT, **kw)


def _split_bf16(x):
    hi = x.astype(BF16)
    return hi, (x - hi.astype(F32)).astype(BF16)


def _const_spec(shape):
    nd = len(shape)
    return pl.BlockSpec(shape, lambda *_: (0,) * nd)


def _tri(n, reverse, strict):
    t = lax.broadcasted_iota(jnp.int32, (n, n), 0)
    s = lax.broadcasted_iota(jnp.int32, (n, n), 1)
    if reverse:
        return (s > t) if strict else (s >= t)
    return (s < t) if strict else (s <= t)


def _seq_neighbours(ref, b, c, nc, cols):
    r0 = pl.multiple_of(c * CHUNK, CHUNK)
    zc = ref[b, pl.ds(r0, CHUNK), cols]
    row = lax.broadcasted_iota(jnp.int32, zc.shape, 0)
    p0 = pl.multiple_of(jnp.maximum(r0 - SUBLANES, 0), SUBLANES)
    n0 = pl.multiple_of(jnp.minimum(r0 + CHUNK, (nc - 1) * CHUNK), SUBLANES)
    before = ref[b, pl.ds(p0, SUBLANES), cols][SUBLANES - 1:SUBLANES]
    after = ref[b, pl.ds(n0, SUBLANES), cols][0:1]
    before = jnp.where(c > 0, before, 0.0)
    after = jnp.where(c < nc - 1, after, 0.0)
    prev = jnp.where(row == 0, before, pltpu.roll(zc, 1, 0))
    nxt = jnp.where(row == CHUNK - 1, after, pltpu.roll(zc, CHUNK - 1, 0))
    return zc, prev, nxt


def _ada_kernel(c_ref, w_ref, b_ref, o_ref):
    cc = c_ref[...]
    o_ref[...] = _dot_bf16(cc * _sigmoid(cc), w_ref[...]) + b_ref[...]


def _ada_mod(cc, ada_w, ada_b):
    rows = cc.shape[0]
    ncol = ada_w.shape[1]
    tn = 1536
    return pl.pallas_call(
        _ada_kernel,
        grid=(ncol // tn,),
        in_specs=[_const_spec((rows, D_MODEL)),
                  pl.BlockSpec((D_MODEL, tn), lambda j: (0, j)),
                  pl.BlockSpec((1, tn), lambda j: (0, j))],
        out_specs=pl.BlockSpec((rows, tn), lambda j: (0, j)),
        out_shape=jax.ShapeDtypeStruct((rows, ncol), F32),
        compiler_params=_params("arbitrary"),
        name="ada_mod",
    )(cc, ada_w, ada_b.reshape(1, ncol))


def _inproj_kernel(x_ref, mod_ref, g_ref, wr_ref, wm_ref, wg_ref, wgt_ref, zr_ref, zm_ref, zg_ref, zgt_ref):
    x = x_ref[...]
    mod = mod_ref[0]
    sh, sc = mod[:, 0:D_MODEL], mod[:, D_MODEL:2 * D_MODEL]
    y = x * lax.rsqrt(jnp.mean(x * x, axis=-1, keepdims=True) + NORM_EPS)
    hn = y * g_ref[...] * (1.0 + sc) + sh
    hb = hn.astype(BF16)
    zr_ref[...] = lax.dot_general(hb, wr_ref[...], NN, preferred_element_type=F32)
    zm_ref[...] = lax.dot_general(hb, wm_ref[...], NN, preferred_element_type=F32)
    zg_ref[...] = _dot_bf16(hb, wg_ref[...])
    zgt_ref[...] = _dot_bf16(wgt_ref[...], hb, NT)


def _inproj(x2, mod, norm_g, w_r, w_m, w_g, w_gt):
    rows = x2.shape[0]
    tiles_per_mod = rows // mod.shape[0] // PROJ_TILE
    return pl.pallas_call(
        _inproj_kernel,
        grid=(rows // PROJ_TILE,),
        in_specs=[pl.BlockSpec((PROJ_TILE, D_MODEL), lambda i: (i, 0)),
                  pl.BlockSpec((1, 1, 6 * D_MODEL), lambda i: (i // tiles_per_mod, 0, 0)),
                  _const_spec((1, D_MODEL)),
                  _const_spec(w_r.shape), _const_spec(w_m.shape), _const_spec(w_g.shape),
                  _const_spec(w_gt.shape)],
        out_specs=[pl.BlockSpec((PROJ_TILE, RWKV_COLS), lambda i: (i, 0)),
                   pl.BlockSpec((PROJ_TILE, MLSTM_MAIN), lambda i: (i, 0)),
                   pl.BlockSpec((PROJ_TILE, N_GATES), lambda i: (i, 0)),
                   pl.BlockSpec((N_GATES, PROJ_TILE), lambda i: (0, i))],
        out_shape=[jax.ShapeDtypeStruct((rows, RWKV_COLS), F32),
                   jax.ShapeDtypeStruct((rows, MLSTM_MAIN), F32),
                   jax.ShapeDtypeStruct((rows, N_GATES), F32),
                   jax.ShapeDtypeStruct((N_GATES, rows), F32)],
        compiler_params=_params("arbitrary"),
        name="inproj",
    )(x2, mod, norm_g.reshape(1, D_MODEL), w_r, w_m, w_g, w_gt)


def _group_sum(x, ones_bd):
    hi = x.astype(BF16)
    lo = (x - hi.astype(F32)).astype(BF16)
    return (lax.dot_general(hi, ones_bd, NN, preferred_element_type=F32)
            + lax.dot_general(lo, ones_bd, NN, preferred_element_type=F32))


def _block_diag(x, ones_bd):
    return jnp.concatenate([x.astype(BF16)] * PACK, axis=0) * ones_bd


def _rwkv_kernel(zr_ref, s0_ref, mu_ref, kkw_ref, ka_ref, rk_ref, gnw_ref, gnb_ref, w0_ref, w2_ref,
                 a0_ref, a2_ref, g2_ref, ones_ref, y_ref, sout_ref,
                 r_s, k_s, v_s, kk_s, gate_s, lx_s, st_s, *, seq_len, grid):
    nc = seq_len // CHUNK
    ones_bd = ones_ref[...]

    def mix_chunk(c, carry):
        r0 = pl.multiple_of(c * CHUNK, CHUNK)
        for b in range(RWKV_GROUP):
            if grid:
                zc = zr_ref[b, pl.ds(r0, CHUNK), :]
                row = lax.broadcasted_iota(jnp.int32, zc.shape, 0)
                up0 = pl.multiple_of(jnp.maximum(c - 1, 0) * CHUNK, CHUNK)
                dn0 = pl.multiple_of(jnp.minimum(c + 1, nc - 1) * CHUNK, CHUNK)
                up = jnp.where(c > 0, zr_ref[b, pl.ds(up0, CHUNK), :], 0.0)
                down = jnp.where(c < nc - 1, zr_ref[b, pl.ds(dn0, CHUNK), :], 0.0)
                left = jnp.where(row == 0, 0.0, pltpu.roll(zc, 1, 0))
                right = jnp.where(row == CHUNK - 1, 0.0, pltpu.roll(zc, CHUNK - 1, 0))
                local = 0.25 * (up + down + left + right)
            else:
                zc, prev, nxt = _seq_neighbours(zr_ref, b, c, nc, slice(None))
                local = 0.5 * (prev + nxt)
            z = zc + (local - zc) * mu_ref[...]
            r, k, v = z[:, 0:DA], z[:, DA:2 * DA], z[:, 2 * DA:3 * DA]
            lx = z[:, 3 * DA:3 * DA + DECAY_LORA + AAA_LORA]
            xg = z[:, 3 * DA + DECAY_LORA + AAA_LORA:]
            kk = k * kkw_ref[...]
            sq = kk * kk
            ss = jnp.concatenate([_group_sum(sq[:, p * PACK_W:(p + 1) * PACK_W], ones_bd) for p in range(N_PACKS)],
                                 axis=1)
            kk = kk / jnp.maximum(jnp.sqrt(ss), 1e-12)
            lane = lax.broadcasted_iota(jnp.int32, lx.shape, 1)
            r_s[b, pl.ds(r0, CHUNK), :] = r
            k_s[b, pl.ds(r0, CHUNK), :] = k
            v_s[b, pl.ds(r0, CHUNK), :] = v
            kk_s[b, pl.ds(r0, CHUNK), :] = kk
            gate_s[b, pl.ds(r0, CHUNK), :] = _dot_bf16(_sigmoid(xg), g2_ref[...])
            lx_s[b, pl.ds(r0, CHUNK), :] = jnp.where(lane < DECAY_LORA, jnp.tanh(lx), lx)
        return carry

    lax.fori_loop(0, nc, mix_chunk, 0)

    rowblk = lax.broadcasted_iota(jnp.int32, (PACK_W, PACK_W), 0) // HEAD_A
    colblk = lax.broadcasted_iota(jnp.int32, (PACK_W, PACK_W), 1) // HEAD_A
    for b in range(RWKV_GROUP):
        for d in range(2):
            for p in range(N_PACKS):
                rows_ = jnp.concatenate([s0_ref[b, d, p * PACK + h] for h in range(PACK)], axis=0)
                st_s[b, d, p] = jnp.where(rowblk == colblk, jnp.concatenate([rows_] * PACK, axis=1), 0.0)

    t_idx = lax.broadcasted_iota(jnp.int32, (CHUNK, PACK_W), 0)
    s_idx = lax.broadcasted_iota(jnp.int32, (CHUNK, PACK_W), 1) % CHUNK

    def scan_step(i, carry):
        first_touch = i < nc // 2
        dot = functools.partial(lax.dot_general, preferred_element_type=F32)
        chains, out_sel = [], []
        for b, d in [(b, d) for b in range(RWKV_GROUP) for d in range(2)]:
            reverse = d == 1
            c = nc - 1 - i if reverse else i
            rows = pl.ds(pl.multiple_of(c * CHUNK, CHUNK), CHUNK)
            out_sel.append((b, rows))
            strict = (s_idx > t_idx) if reverse else (s_idx < t_idx)
            incl = (s_idx >= t_idx) if reverse else (s_idx <= t_idx)
            last = 0 if reverse else CHUNK - 1
            r, k, v, kk = r_s[b, rows, :], k_s[b, rows, :], v_s[b, rows, :], kk_s[b, rows, :]
            lx = lx_s[b, rows, :]
            wl = w0_ref[d:d + 1, :] + _dot_bf16(lx[:, 0:DECAY_LORA], w2_ref[d])
            logw = -jnp.exp(_log_sigmoid(wl) - 0.5)
            a = _sigmoid(a0_ref[d:d + 1, :] + _dot_bf16(lx[:, DECAY_LORA:], a2_ref[d]))
            kd = k * (1.0 + (a - 1.0) * ka_ref[...])
            kb = kk * a
            tri = _tri(CHUNK, reverse, False).astype(BF16)
            w1, w_rest = _split_bf16(logw)
            w2, w3 = _split_bf16(w_rest)
            cum = dot(tri, w1, NN) + dot(tri, w2, NN) + dot(tri, w3, NN)
            clast = cum[last:last + 1]
            rt = r * jnp.exp(cum)
            at = -kk * jnp.exp(cum - logw)
            einv = jnp.exp(-cum)
            kt, bt = kd * einv, kb * einv
            edec = jnp.exp(clast - cum)
            kp, bp = kd * edec, kb * edec
            ptot = jnp.exp(clast)
            rkd = r * kd * rk_ref[...]
            for p in range(N_PACKS):
                cs = slice(p * PACK_W, (p + 1) * PACK_W)
                chains.append(dict(
                    out=len(out_sel) - 1, st=(b, d, p), cs=cs, strict=strict, incl=incl, v=v[:, cs],
                    rkd=rkd[:, cs], ptot=ptot[:, cs],
                    ar=jnp.concatenate([at[:, cs], rt[:, cs]], axis=0).astype(BF16),
                    kt_bd=_block_diag(kt[:, cs], ones_bd), bt_bd=_block_diag(bt[:, cs], ones_bd),
                    v_bd=_block_diag(v[:, cs], ones_bd),
                    kbp=jnp.concatenate([kp[:, cs], bp[:, cs]], axis=0)))

        for ch in chains:
            ch['s_prev'] = st_s[ch['st']]
            ch['a_k'] = dot(ch['ar'], ch['kt_bd'], NT)
            ch['a_b'] = dot(ch['ar'], ch['bt_bd'], NT)
            ch['ars'] = dot(ch['ar'], ch['s_prev'].astype(BF16), NT)
        for ch in chains:
            a_ak = jnp.where(ch['strict'], ch['a_k'][:CHUNK], 0.0)
            a_rk = jnp.where(ch['incl'], ch['a_k'][CHUNK:], 0.0)
            ch['a_rb'] = jnp.where(ch['incl'], ch['a_b'][CHUNK:], 0.0).astype(BF16)
            n = jnp.where(ch['strict'], ch['a_b'][:CHUNK], 0.0)
            ch['n_hi'], n_lo = _split_bf16(n)
            ch['n_both'] = jnp.concatenate([ch['n_hi'], n_lo], axis=0)
            ch['pw'] = ch['n_hi']
            ch['m'] = jnp.where(s_idx == t_idx, 1.0, 0.0) + n
            akv = dot(jnp.concatenate([a_ak, a_rk], axis=0).astype(BF16), ch['v_bd'], NN)
            ch['rhs'] = ch['ars'][:CHUNK] + akv[:CHUNK]
            ch['y'] = ch['ars'][CHUNK:] + akv[CHUNK:]
        for ch in chains:
            ch['pw'] = dot(ch['pw'], _block_diag(ch['pw'], ones_bd), NN).astype(BF16)
        for j in range(1, 5):
            for ch in chains:
                pw_bd = _block_diag(ch['pw'], ones_bd)
                if j < 4:
                    both = dot(jnp.concatenate([ch['m'].astype(BF16), ch['pw']], axis=0), pw_bd, NN)
                    ch['m'] = ch['m'] + both[:CHUNK]
                    ch['pw'] = both[CHUNK:].astype(BF16)
                else:
                    ch['m'] = ch['m'] + dot(ch['m'].astype(BF16), pw_bd, NN)
        for ch in chains:
            ch['mb'] = ch['m'].astype(BF16)
            ch['u'] = dot(ch['mb'], _block_diag(ch['rhs'], ones_bd), NN)
        for ch in chains:
            u_hi, u_lo = _split_bf16(ch['u'])
            both = dot(ch['n_both'], _block_diag(u_hi, ones_bd), NN)
            nu = both[:CHUNK] + both[CHUNK:] + dot(ch['n_hi'], _block_diag(u_lo, ones_bd), NN)
            ch['resid'] = ch['rhs'] - ch['u'] + nu
        for ch in chains:
            ch['u'] = ch['u'] + dot(ch['mb'], _block_diag(ch['resid'], ones_bd), NN)
        for ch in chains:
            u = ch['u']
            ch['y'] = ch['y'] + dot(ch['a_rb'], _block_diag(u, ones_bd), NN)
            vu_hi, vu_lo = _split_bf16(jnp.concatenate([ch['v'], u], axis=0))
            kbp_hi, kbp_lo = _split_bf16(ch['kbp'])
            grown = dot(vu_hi, kbp_hi, TN) + dot(vu_hi, kbp_lo, TN) + dot(vu_lo, kbp_hi, TN)
            st_s[ch['st']] = ch['s_prev'] * ch['ptot'] + grown * ones_bd.astype(F32)
        n_ch = len(chains)
        sums = dot(jnp.concatenate([ch['y'] for ch in chains] + [ch['rkd'] for ch in chains],
                                   axis=0).astype(BF16), ones_bd, NN)
        ycs = [ch['y'] - sums[q * CHUNK:(q + 1) * CHUNK] * (1.0 / HEAD_A) for q, ch in enumerate(chains)]
        var = dot(jnp.concatenate([yc * yc for yc in ycs], axis=0).astype(BF16), ones_bd, NN) * (1.0 / HEAD_A)
        outs = [[] for _ in out_sel]
        for q, ch in enumerate(chains):
            cs = ch['cs']
            out = ycs[q] * lax.rsqrt(var[q * CHUNK:(q + 1) * CHUNK] + GN_EPS) * gnw_ref[:, cs] + gnb_ref[:, cs]
            outs[ch['out']].append(out + sums[(n_ch + q) * CHUNK:(n_ch + q + 1) * CHUNK] * ch['v'])
        for (b, rows), parts in zip(out_sel, outs):
            out = jnp.concatenate(parts, axis=1)

            @pl.when(first_touch)
            def _(b=b, rows=rows, out=out):
                y_ref[b, rows, :] = out

            @pl.when(jnp.logical_not(first_touch))
            def _(b=b, rows=rows, out=out):
                y_ref[b, rows, :] = (y_ref[b, rows, :] + out) * gate_s[b, rows, :]
        return carry

    lax.fori_loop(0, nc, scan_step, 0)

    for b in range(RWKV_GROUP):
        for d in range(2):
            for p in range(N_PACKS):
                s_fin = st_s[b, d, p]
                for h in range(PACK):
                    sout_ref[b, d, p * PACK + h] = s_fin[h * HEAD_A:(h + 1) * HEAD_A, h * HEAD_A:(h + 1) * HEAD_A]


def _rwkv(zr, s0, p, grid):
    bsz, seq_len, _ = zr.shape
    assert (seq_len // CHUNK) % 2 == 0
    kern = functools.partial(_rwkv_kernel, seq_len=seq_len, grid=grid)
    consts = [p['mu'], p['kkw'], p['ka'], p['rk'], p['gnw'], p['gnb'], p['w0'], p['w2'], p['a0'], p['a2'],
              p['g2'], p['ones_bd']]
    assert bsz % RWKV_GROUP == 0
    return pl.pallas_call(
        kern,
        grid=(bsz // RWKV_GROUP,),
        in_specs=[pl.BlockSpec((RWKV_GROUP, seq_len, RWKV_COLS), lambda b: (b, 0, 0),
                               pipeline_mode=pl.Buffered(1)),
                  pl.BlockSpec((RWKV_GROUP, 2, H_A, HEAD_A, HEAD_A), lambda b: (b, 0, 0, 0, 0))]
                 + [_const_spec(a.shape) for a in consts],
        out_specs=[pl.BlockSpec((RWKV_GROUP, seq_len, DA), lambda b: (b, 0, 0)),
                   pl.BlockSpec((RWKV_GROUP, 2, H_A, HEAD_A, HEAD_A), lambda b: (b, 0, 0, 0, 0))],
        out_shape=[jax.ShapeDtypeStruct((bsz, seq_len, DA), F32),
                   jax.ShapeDtypeStruct((bsz, 2, H_A, HEAD_A, HEAD_A), F32)],
        scratch_shapes=[pltpu.VMEM((RWKV_GROUP, seq_len, DA), F32)] * 5
                       + [pltpu.VMEM((RWKV_GROUP, seq_len, DECAY_LORA + AAA_LORA), F32),
                          pltpu.VMEM((RWKV_GROUP, 2, N_PACKS, PACK_W, PACK_W), F32)],
        compiler_params=_params("arbitrary"),
        name="rwkv_scan",
    )(zr, s0, *consts)


def _mlstm_kernel(zm_ref, g_ref, gt_ref, c0_ref, n0_ref, m0_ref, conv_ref, bi_ref, bit_ref, bf_ref, bft_ref,
                  ng_ref, erep_ref, y_ref, cout_ref, nout_ref, mout_ref,
                  qk_s, h_s, c_s, n_s, m_s, *, seq_len):
    nc = seq_len // CHUNK

    def conv_chunk(c, carry):
        for b in range(MLSTM_GROUP):
            zc, prev, nxt = _seq_neighbours(zm_ref, b, c, nc, slice(0, 2 * DB))
            u = conv_ref[0:1, :] * prev + conv_ref[1:2, :] * zc + conv_ref[2:3, :] * nxt
            qk = u * _sigmoid(u)
            lane = lax.broadcasted_iota(jnp.int32, qk.shape, 1)
            qk_s[b, pl.ds(pl.multiple_of(c * CHUNK, CHUNK), CHUNK), :] = jnp.where(lane >= DB,
                                                                                  qk * (HEAD_B ** -0.5), qk)
        return carry

    lax.fori_loop(0, nc, conv_chunk, 0)

    for b in range(MLSTM_GROUP):
        for d in range(2):
            c_s[b, d] = c0_ref[b, d]
            n_s[b, d] = n0_ref[b, d]
            m_s[b, d] = jnp.broadcast_to(m0_ref[b, d], (H_B, LANES))

    e_rep = erep_ref[...]
    ones_tl = jnp.ones((CHUNK, LANES), BF16)
    row_id = lax.broadcasted_iota(jnp.int32, (CHUNK, LANES), 0)

    def scan_max(x, reverse):
        shift = 1
        while shift < CHUNK:
            if reverse:
                moved = jnp.where(row_id < CHUNK - shift, pltpu.roll(x, CHUNK - shift, 0), -jnp.inf)
            else:
                moved = jnp.where(row_id >= shift, pltpu.roll(x, shift, 0), -jnp.inf)
            x = jnp.maximum(x, moved)
            shift *= 2
        return x

    def dot3(a, b, dims):
        a_hi, a_lo = _split_bf16(a)
        b_hi, b_lo = _split_bf16(b)
        dg = functools.partial(lax.dot_general, dimension_numbers=dims, preferred_element_type=F32)
        return dg(a_hi, b_hi) + dg(a_hi, b_lo) + dg(a_lo, b_hi)

    def scan_step(i, carry):
        dot = functools.partial(lax.dot_general, preferred_element_type=F32)
        chains = []
        for b, d in [(b, d) for b in range(MLSTM_GROUP) for d in range(2)]:
            reverse = d == 1
            incl = _tri(CHUNK, reverse, False)
            tri_b = incl.astype(BF16)
            tri_row = _tri(CHUNK, not reverse, False).astype(F32)
            last = 0 if reverse else CHUNK - 1
            gsl = slice(d * H_B, (d + 1) * H_B)
            fsl = slice(2 * H_B + d * H_B, 2 * H_B + (d + 1) * H_B)
            c = nc - 1 - i if reverse else i
            rows = pl.ds(pl.multiple_of(c * CHUNK, CHUNK), CHUNK)
            g1, g_rest = _split_bf16(g_ref[b, rows, :])
            g2, g3 = _split_bf16(g_rest)
            g_rep = dot(g1, e_rep, NN) + dot(g2, e_rep, NN) + dot(g3, e_rep, NN)
            grow = gt_ref[b, c]
            i_row = grow[gsl, :] + bit_ref[gsl, :]
            f_row = _log_sigmoid(grow[fsl, :] + bft_ref[gsl, :])
            b_row = _mm(f_row, tri_row)
            for h in range(H_B):
                ji, jf = d * H_B + h, 2 * H_B + d * H_B + h
                ic = g_rep[:, ji * LANES:(ji + 1) * LANES] + bi_ref[:, ji:ji + 1]
                fc = _log_sigmoid(g_rep[:, jf * LANES:(jf + 1) * LANES] + bf_ref[:, ji:ji + 1])
                f1, f_rest = _split_bf16(fc)
                f2, f3 = _split_bf16(f_rest)
                bc = dot(tri_b, f1, NN) + dot(tri_b, f2, NN) + dot(tri_b, f3, NN)
                q = qk_s[b, rows, h * HEAD_B:(h + 1) * HEAD_B]
                chains.append(dict(
                    b=b, d=d, h=h, rows=rows, incl=incl, reverse=reverse, q=q, qb=q.astype(BF16),
                    k=qk_s[b, rows, DB + h * HEAD_B:DB + (h + 1) * HEAD_B],
                    v=zm_ref[b, rows, 2 * DB + h * HEAD_B:2 * DB + (h + 1) * HEAD_B],
                    bc=bc, ic=ic, bl=bc[last:last + 1, :], br=b_row[h:h + 1, :], ir=i_row[h:h + 1, :]))
        for ch in chains:
            b, d, h = ch['b'], ch['d'], ch['h']
            ch['qk'] = _dot_bf16(ch['qb'], ch['k'], NT)
            ch['c_prev'] = c_s[b, d, h]
            ch['qc'] = _dot_bf16(ch['qb'], ch['c_prev'])
            ch['n_prev'] = n_s[b, d, h:h + 1, :]
            ch['qn'] = dot3(ch['q'], jnp.broadcast_to(ch['n_prev'], (HEAD_B, HEAD_B)), NT)
        for ch in chains:
            b, d, h = ch['b'], ch['d'], ch['h']
            bc, ic, bl = ch['bc'], ch['ic'], ch['bl']
            m_prev = m_s[b, d, h:h + 1, :]
            m_t = bc + jnp.maximum(m_prev, scan_max(ic - bc, ch['reverse']))
            dmat = jnp.where(ch['incl'], bc[:, :CHUNK] - ch['br'] + ch['ir'], -jnp.inf)
            s = ch['qk'] * jnp.exp(dmat - m_t[:, :CHUNK])
            carry_w = jnp.exp(bc + m_prev - m_t)
            s_hi, s_lo = _split_bf16(s)
            den = dot(s_hi, ones_tl, NN) + dot(s_lo, ones_tl, NN) + carry_w * ch['qn']
            num = dot(s_hi, ch['v'].astype(BF16), NN) + carry_w * ch['qc']
            h_s[b, d, ch['rows'], h * HEAD_B:(h + 1) * HEAD_B] = num / jnp.maximum(jnp.abs(den), jnp.exp(-m_t))
            g = bl - bc + ic
            m_new = jnp.maximum(bl + m_prev, jnp.max(g, axis=0, keepdims=True))
            wk = jnp.exp(g - m_new)
            decay = jnp.exp(bl + m_prev - m_new)
            c_s[b, d, h] = decay * ch['c_prev'] + _dot_bf16(ch['k'], wk * ch['v'], TN)
            n_s[b, d, h:h + 1, :] = decay * ch['n_prev'] + jnp.sum(wk * ch['k'], axis=0, keepdims=True)
            m_s[b, d, h:h + 1, :] = m_new
        return carry

    lax.fori_loop(0, nc, scan_step, 0)
    cout_ref[...] = c_s[...]
    nout_ref[...] = n_s[...]
    mout_ref[...] = m_s[...]

    def norm_chunk(c, carry):
        rows = pl.ds(pl.multiple_of(c * CHUNK, CHUNK), CHUNK)
        for b in range(MLSTM_GROUP):
            og = zm_ref[b, rows, 3 * DB:4 * DB]
            hm = h_s[b, 0, rows, :] + h_s[b, 1, rows, :]
            parts = []
            for h in range(H_B):
                hh = hm[:, h * HEAD_B:(h + 1) * HEAD_B]
                parts.append(hh * lax.rsqrt(jnp.mean(hh * hh, axis=-1, keepdims=True) + NORM_EPS))
            y_ref[b, rows, :] = jnp.concatenate(parts, axis=1) * ng_ref[...] * _sigmoid(og)
        return carry

    lax.fori_loop(0, nc, norm_chunk, 0)


def _mlstm(zm, g, gt, c0, n0, m0, p):
    bsz, seq_len, _ = zm.shape
    nc = seq_len // CHUNK
    kern = functools.partial(_mlstm_kernel, seq_len=seq_len)
    consts = [p['conv'], p['bi'], p['bit'], p['bf'], p['bft'], p['ng'], p['e_rep']]
    grp = MLSTM_GROUP
    assert bsz % grp == 0
    return pl.pallas_call(
        kern,
        grid=(bsz // grp,),
        in_specs=[pl.BlockSpec((grp, seq_len, MLSTM_MAIN), lambda b: (b, 0, 0), pipeline_mode=pl.Buffered(1)),
                  pl.BlockSpec((grp, seq_len, N_GATES), lambda b: (b, 0, 0)),
                  pl.BlockSpec((grp, nc, N_GATES, CHUNK), lambda b: (b, 0, 0, 0)),
                  pl.BlockSpec((grp, 2, H_B, HEAD_B, HEAD_B), lambda b: (b, 0, 0, 0, 0)),
                  pl.BlockSpec((grp, 2, H_B, HEAD_B), lambda b: (b, 0, 0, 0)),
                  pl.BlockSpec((grp, 2, H_B, 1), lambda b: (b, 0, 0, 0))]
                 + [_const_spec(a.shape) for a in consts],
        out_specs=[pl.BlockSpec((grp, seq_len, DB), lambda b: (b, 0, 0)),
                   pl.BlockSpec((grp, 2, H_B, HEAD_B, HEAD_B), lambda b: (b, 0, 0, 0, 0)),
                   pl.BlockSpec((grp, 2, H_B, HEAD_B), lambda b: (b, 0, 0, 0)),
                   pl.BlockSpec((grp, 2, H_B, LANES), lambda b: (b, 0, 0, 0))],
        out_shape=[jax.ShapeDtypeStruct((bsz, seq_len, DB), F32),
                   jax.ShapeDtypeStruct((bsz, 2, H_B, HEAD_B, HEAD_B), F32),
                   jax.ShapeDtypeStruct((bsz, 2, H_B, HEAD_B), F32),
                   jax.ShapeDtypeStruct((bsz, 2, H_B, LANES), F32)],
        scratch_shapes=[pltpu.VMEM((grp, seq_len, 2 * DB), F32), pltpu.VMEM((grp, 2, seq_len, DB), F32),
                        pltpu.VMEM((grp, 2, H_B, HEAD_B, HEAD_B), F32), pltpu.VMEM((grp, 2, H_B, HEAD_B), F32),
                        pltpu.VMEM((grp, 2, H_B, LANES), F32)],
        compiler_params=_params("arbitrary"),
        name="mlstm_scan",
    )(zm, g, gt, c0, n0, m0, *consts)


def _outproj_kernel(x_ref, yr_ref, ym_ref, mod_ref, g_ref, wo_ref, rw_ref, rb_ref,
                    x1_ref, hn_ref, idx_ref, gate_ref):
    mod = mod_ref[0]
    g1 = mod[:, 2 * D_MODEL:3 * D_MODEL]
    sh2, sc2 = mod[:, 3 * D_MODEL:4 * D_MODEL], mod[:, 4 * D_MODEL:5 * D_MODEL]
    mix = _dot_bf16(yr_ref[...], wo_ref[0:DA, :]) + _dot_bf16(ym_ref[...], wo_ref[DA:, :])
    x1 = x_ref[...] + g1 * mix
    x1_ref[...] = x1
    y = x1 * lax.rsqrt(jnp.mean(x1 * x1, axis=-1, keepdims=True) + NORM_EPS)
    hn = y * g_ref[...] * (1.0 + sc2) + sh2
    hn_ref[...] = hn
    logits = _dot_bf16(hn, rw_ref[...]) + rb_ref[...]
    lane = lax.broadcasted_iota(jnp.int32, logits.shape, 1)
    vals, idxs = [], []
    for _ in range(TOP_K):
        top = jnp.max(logits, axis=-1, keepdims=True)
        pick = jnp.min(jnp.where(logits == top, lane, N_EXPERTS), axis=-1, keepdims=True)
        vals.append(top)
        idxs.append(pick)
        logits = jnp.where(lane == pick, -jnp.inf, logits)
    exps = [jnp.exp(v - vals[0]) for v in vals]
    total = exps[0] + exps[1] + exps[2] + exps[3]
    k_lane = lax.broadcasted_iota(jnp.int32, (x1.shape[0], TOP_K), 1)
    gates = jnp.zeros((x1.shape[0], TOP_K), F32)
    picks = jnp.zeros((x1.shape[0], TOP_K), jnp.int32)
    for j in range(TOP_K):
        gates = jnp.where(k_lane == j, exps[j] / total, gates)
        picks = jnp.where(k_lane == j, idxs[j], picks)
    idx_ref[...] = picks
    gate_ref[...] = gates


def _outproj(x2, yr, ym, mod, norm_g, w_out, router_w, router_b):
    rows = x2.shape[0]
    tiles_per_mod = rows // mod.shape[0] // PROJ_TILE
    row_spec = lambda w: pl.BlockSpec((PROJ_TILE, w), lambda i: (i, 0))
    return pl.pallas_call(
        _outproj_kernel,
        grid=(rows // PROJ_TILE,),
        in_specs=[row_spec(D_MODEL), row_spec(DA), row_spec(DB),
                  pl.BlockSpec((1, 1, 6 * D_MODEL), lambda i: (i // tiles_per_mod, 0, 0)),
                  _const_spec((1, D_MODEL)), _const_spec(w_out.shape), _const_spec(router_w.shape),
                  _const_spec((1, N_EXPERTS))],
        out_specs=[row_spec(D_MODEL), row_spec(D_MODEL), row_spec(TOP_K), row_spec(TOP_K)],
        out_shape=[jax.ShapeDtypeStruct((rows, D_MODEL), F32), jax.ShapeDtypeStruct((rows, D_MODEL), F32),
                   jax.ShapeDtypeStruct((rows, TOP_K), jnp.int32), jax.ShapeDtypeStruct((rows, TOP_K), F32)],
        compiler_params=_params("arbitrary"),
        name="outproj_router",
    )(x2, yr, ym, mod, norm_g.reshape(1, D_MODEL), w_out, router_w, router_b.reshape(1, N_EXPERTS))


def _rank_kernel(idx_ref, rank_ref, count_ref, run_s):
    i = pl.program_id(0)

    @pl.when(i == 0)
    def _():
        run_s[...] = jnp.zeros_like(run_s)

    idx = idx_ref[...]
    rows = idx.shape[0]
    lane = lax.broadcasted_iota(jnp.int32, (rows, N_EXPERTS), 1)
    hot = jnp.zeros((rows, N_EXPERTS), F32)
    for j in range(TOP_K):
        hot = hot + jnp.where(lane == idx[:, j:j + 1], 1.0, 0.0)
    before = _dot_bf16(_tri(rows, False, True).astype(F32), hot) + run_s[...]
    k_lane = lax.broadcasted_iota(jnp.int32, (rows, TOP_K), 1)
    rank = jnp.zeros((rows, TOP_K), F32)
    for j in range(TOP_K):
        rj = jnp.sum(jnp.where(lane == idx[:, j:j + 1], before, 0.0), axis=-1, keepdims=True)
        rank = jnp.where(k_lane == j, rj, rank)
    rank_ref[...] = rank.astype(jnp.int32)
    run_s[...] = run_s[...] + jnp.sum(hot, axis=0, keepdims=True)
    count_ref[...] = run_s[...].astype(jnp.int32)


def _expert_ranks(idx):
    rows = idx.shape[0]
    return pl.pallas_call(
        _rank_kernel,
        grid=(rows // ROW_TILE,),
        in_specs=[pl.BlockSpec((ROW_TILE, TOP_K), lambda i: (i, 0))],
        out_specs=[pl.BlockSpec((ROW_TILE, TOP_K), lambda i: (i, 0)), _const_spec((1, N_EXPERTS))],
        out_shape=[jax.ShapeDtypeStruct((rows, TOP_K), jnp.int32),
                   jax.ShapeDtypeStruct((1, N_EXPERTS), jnp.int32)],
        scratch_shapes=[pltpu.VMEM((1, N_EXPERTS), F32)],
        compiler_params=_params("arbitrary"),
        name="expert_ranks",
    )(idx)


def _dispatch_kernel(dest_ref, pend_ref, x_ref, xs_ref, zero_s, sem):
    @pl.when(pl.program_id(0) == 0)
    def _():
        zero_s[...] = jnp.zeros_like(zero_s)

        def zero_block(start):
            return pltpu.make_async_copy(zero_s, xs_ref.at[pl.ds(pl.multiple_of(start, MOE_BLOCK), MOE_BLOCK)], sem)

        def has_rows(e):
            return pend_ref[e] > (pend_ref[e - 1] if e else 0)

        used_end = pend_ref[N_EXPERTS - 1]
        n_tail = (xs_ref.shape[0] - used_end) // MOE_BLOCK

        def start_tail(b, carry):
            zero_block(used_end + b * MOE_BLOCK).start()
            return carry

        def wait_tail(b, carry):
            zero_block(0).wait()
            return carry

        for e in range(N_EXPERTS):
            @pl.when(has_rows(e))
            def _(e=e):
                zero_block(pend_ref[e] - MOE_BLOCK).start()
        lax.fori_loop(0, n_tail, start_tail, 0)
        for e in range(N_EXPERTS):
            @pl.when(has_rows(e))
            def _():
                zero_block(0).wait()
        lax.fori_loop(0, n_tail, wait_tail, 0)

    def issue(g, carry):
        for q in range(SUBLANES):
            for j in range(TOP_K):
                slot = dest_ref[g * (SUBLANES * TOP_K) + q * TOP_K + j]
                pltpu.make_async_copy(x_ref.at[g, pl.ds(q, 1)], xs_ref.at[pl.ds(slot, 1)], sem).start(priority=j % 2)
        return carry

    lax.fori_loop(0, ROW_TILE // SUBLANES, issue, 0)
    all_rows = xs_ref.at[pl.ds(0, ROW_TILE * TOP_K)]
    pltpu.make_async_copy(all_rows, all_rows, sem).wait()


def _dispatch(dest_flat, pad_end, x2, n_slots):
    rows = x2.shape[0]
    return pl.pallas_call(
        _dispatch_kernel,
        grid=(rows // ROW_TILE,),
        in_specs=[pl.BlockSpec((ROW_TILE * TOP_K,), lambda i: (i,), memory_space=pltpu.SMEM),
                  pl.BlockSpec(memory_space=pltpu.SMEM),
                  pl.BlockSpec((ROW_TILE // SUBLANES, SUBLANES, D_MODEL), lambda i: (i, 0, 0))],
        out_specs=pl.BlockSpec(memory_space=pl.ANY),
        out_shape=jax.ShapeDtypeStruct((n_slots, D_MODEL), x2.dtype),
        scratch_shapes=[pltpu.VMEM((MOE_BLOCK, D_MODEL), F32), pltpu.SemaphoreType.DMA(())],
        compiler_params=_params("arbitrary", disable_bounds_checks=True),
        name="moe_dispatch",
    )(dest_flat, pad_end, x2.reshape(rows // SUBLANES, SUBLANES, D_MODEL))


def _expert_kernel(be_ref, nb_ref, xs_ref, w1_ref, b1_ref, w2_ref, b2_ref, y_ref, w1_s, w2_s):
    i = pl.program_id(0)
    changed = jnp.logical_or(i == 0, be_ref[i] != be_ref[jnp.maximum(i - 1, 0)])

    @pl.when(jnp.logical_and(changed, i < nb_ref[0]))
    def _():
        w1_s[...] = w1_ref[0].astype(BF16)
        w2_s[...] = w2_ref[0].astype(BF16)

    @pl.when(i < nb_ref[0])
    def _():
        xb = xs_ref[...].astype(BF16)
        hu = lax.dot_general(xb, w1_s[...], NN, preferred_element_type=F32) + b1_ref[0]
        glu = jnp.minimum(hu[:, :D_FF], SWIGLU_LIMIT)
        lin = jnp.clip(hu[:, D_FF:], -SWIGLU_LIMIT, SWIGLU_LIMIT)
        act = glu * _sigmoid(SWIGLU_ALPHA * glu) * (lin + 1.0)
        y_ref[...] = lax.dot_general(act.astype(BF16), w2_s[...], NN, preferred_element_type=F32) + b2_ref[0]

    @pl.when(i >= nb_ref[0])
    def _():
        y_ref[...] = jnp.zeros_like(y_ref)


def _experts(block_e, n_used, xs, w1, b1, w2, b2):
    n_blocks = xs.shape[0] // MOE_BLOCK
    grid_spec = pltpu.PrefetchScalarGridSpec(
        num_scalar_prefetch=2,
        grid=(n_blocks,),
        in_specs=[pl.BlockSpec((MOE_BLOCK, D_MODEL), lambda i, be, nb: (jnp.minimum(i, nb[0] - 1), 0)),
                  pl.BlockSpec((1, D_MODEL, 2 * D_FF), lambda i, be, nb: (be[i], 0, 0)),
                  pl.BlockSpec((1, 1, 2 * D_FF), lambda i, be, nb: (be[i], 0, 0)),
                  pl.BlockSpec((1, D_FF, D_MODEL), lambda i, be, nb: (be[i], 0, 0)),
                  pl.BlockSpec((1, 1, D_MODEL), lambda i, be, nb: (be[i], 0, 0))],
        out_specs=pl.BlockSpec((MOE_BLOCK, D_MODEL), lambda i, be, nb: (i, 0)),
        scratch_shapes=[pltpu.VMEM((D_MODEL, 2 * D_FF), BF16), pltpu.VMEM((D_FF, D_MODEL), BF16)],
    )
    return pl.pallas_call(
        _expert_kernel,
        grid_spec=grid_spec,
        out_shape=jax.ShapeDtypeStruct(xs.shape, F32),
        compiler_params=_params("arbitrary"),
        name="moe_experts",
    )(block_e, n_used, xs, w1, b1.reshape(N_EXPERTS, 1, 2 * D_FF), w2, b2.reshape(N_EXPERTS, 1, D_MODEL))


def _combine_kernel(dest_ref, yb_ref, x1_ref, gate_ref, mod_ref, fg_ref, o_ref, buf, sem):
    def issue(g, carry):
        for q in range(SUBLANES):
            for j in range(TOP_K):
                slot = dest_ref[g * (SUBLANES * TOP_K) + q * TOP_K + j]
                pltpu.make_async_copy(yb_ref.at[pl.ds(slot, 1)], buf.at[j, g, pl.ds(q, 1)], sem).start(priority=j % 2)
        return carry

    lax.fori_loop(0, ROW_TILE // SUBLANES, issue, 0)
    pltpu.make_async_copy(buf, buf, sem).wait()

    g2 = mod_ref[0][:, 5 * D_MODEL:6 * D_MODEL]
    gates = gate_ref[...]
    moe = jnp.zeros((ROW_TILE, D_MODEL), F32)
    for j in range(TOP_K):
        moe = moe + buf[j].reshape(ROW_TILE, D_MODEL) * gates[:, j:j + 1]
    x2 = x1_ref[...] + g2 * moe
    y = x2 * lax.rsqrt(jnp.mean(x2 * x2, axis=-1, keepdims=True) + NORM_EPS)
    o_ref[...] = y * fg_ref[...]


def _combine(dest_flat, yb, x1, gates, mod, final_g, row_offset):
    rows = x1.shape[0]
    tiles_per_mod = rows // mod.shape[0] // ROW_TILE
    tile_offset = row_offset // ROW_TILE
    row_spec = pl.BlockSpec((ROW_TILE, D_MODEL), lambda i: (i, 0))
    return pl.pallas_call(
        _combine_kernel,
        grid=(rows // ROW_TILE,),
        in_specs=[pl.BlockSpec((ROW_TILE * TOP_K,), lambda i: (i + tile_offset,), memory_space=pltpu.SMEM),
                  pl.BlockSpec(memory_space=pl.ANY),
                  row_spec,
                  pl.BlockSpec((ROW_TILE, TOP_K), lambda i: (i, 0)),
                  pl.BlockSpec((1, 1, 6 * D_MODEL), lambda i: (i // tiles_per_mod, 0, 0)),
                  _const_spec((1, D_MODEL))],
        out_specs=row_spec,
        out_shape=jax.ShapeDtypeStruct((rows, D_MODEL), F32),
        scratch_shapes=[pltpu.VMEM((TOP_K, ROW_TILE // SUBLANES, SUBLANES, D_MODEL), F32),
                        pltpu.SemaphoreType.DMA(())],
        compiler_params=_params("arbitrary", disable_bounds_checks=True),
        name="moe_combine",
    )(dest_flat, yb, x1, gates, mod, final_g.reshape(1, D_MODEL))


def _block_diag_ones(width, block):
    i = jnp.arange(width) // block
    return (i[:, None] == i[None, :]).astype(F32)


def _mixers(x, mod, lp, s0, c0, n0, m0, grid):
    bsz, seq_len, _ = x.shape
    nc = seq_len // CHUNK
    x2 = x.reshape(bsz * seq_len, D_MODEL)
    zr, zm, zg, zgt = _inproj(x2, mod, lp['norm1_g'], lp['w_r'], lp['w_m'], lp['w_g'], lp['w_gt'])
    gt = zgt.reshape(N_GATES, bsz, nc, CHUNK).transpose(1, 2, 0, 3)
    yr, s_new = _rwkv(zr.reshape(bsz, seq_len, RWKV_COLS), s0, lp, grid)
    ym, c_new, n_new, m_new = _mlstm(zm.reshape(bsz, seq_len, MLSTM_MAIN), zg.reshape(bsz, seq_len, N_GATES), gt,
                                     c0, n0, m0.reshape(bsz, 2, H_B, 1), lp)
    x1, hn, idx, gates = _outproj(x2, yr.reshape(-1, DA), ym.reshape(-1, DB), mod, lp['norm2_g'], lp['w_out'],
                                  lp['router_w'], lp['router_b'])
    return x1, hn, idx, gates, (s_new, c_new, n_new, m_new[..., 0])


def kernel(x_prompt, x_sample, state_rwkv, state_mlstm_C, state_mlstm_n, state_mlstm_m, c, c_ctx, ada_w, ada_b, norm1_g, norm2_g, w_in, w_out, rwkv_mu, rwkv_w0, rwkv_w2, rwkv_a0, rwkv_a2, rwkv_g2, rwkv_kk, rwkv_ka, rwkv_rk, rwkv_gn_w, rwkv_gn_b, mlstm_conv, mlstm_bi, mlstm_bf, mlstm_norm_g, router_w, router_b, moe_w1, moe_b1, moe_w2, moe_b2, final_g):
    bp, lp_len, _ = x_prompt.shape
    bs, ls_len, _ = x_sample.shape
    w = w_in[0]
    lp = {
        'norm1_g': norm1_g[0], 'norm2_g': norm2_g[0],
        'w_r': w[:, :RWKV_COLS].astype(BF16),
        'w_m': w[:, RWKV_COLS:RWKV_COLS + MLSTM_MAIN].astype(BF16),
        'w_g': w[:, RWKV_COLS + MLSTM_MAIN:],
        'w_gt': w[:, RWKV_COLS + MLSTM_MAIN:].T,
        'w_out': w_out[0].astype(BF16),
        'mu': rwkv_mu[0].reshape(1, RWKV_COLS), 'kkw': rwkv_kk[0].reshape(1, DA), 'ka': rwkv_ka[0].reshape(1, DA),
        'rk': rwkv_rk[0].reshape(1, DA), 'gnw': rwkv_gn_w[0].reshape(1, DA), 'gnb': rwkv_gn_b[0].reshape(1, DA),
        'w0': rwkv_w0[0], 'w2': rwkv_w2[0], 'a0': rwkv_a0[0], 'a2': rwkv_a2[0], 'g2': rwkv_g2[0],
        'ones_bd': _block_diag_ones(PACK_W, HEAD_A).astype(BF16),
        'conv': mlstm_conv[0], 'bi': mlstm_bi[0].reshape(1, 2 * H_B), 'bit': mlstm_bi[0].reshape(2 * H_B, 1),
        'bf': mlstm_bf[0].reshape(1, 2 * H_B), 'bft': mlstm_bf[0].reshape(2 * H_B, 1),
        'ng': mlstm_norm_g[0].reshape(1, DB),
        'e_rep': jnp.repeat(jnp.eye(N_GATES, dtype=BF16), LANES, axis=1),
        'router_w': router_w[0], 'router_b': router_b[0],
    }
    cc = jnp.concatenate([c_ctx[None, :], c, jnp.zeros((2 * SUBLANES - 1 - bs, D_MODEL), F32)], axis=0)
    mod = _ada_mod(cc, ada_w[0], ada_b[0])
    mod_p = mod[0:1].reshape(1, 1, 6 * D_MODEL)
    mod_s = mod[1:1 + bs].reshape(bs, 1, 6 * D_MODEL)

    zeros = lambda *shape: jnp.zeros(shape, F32)
    x1p, hnp, idxp, gatesp, st = _mixers(x_prompt, mod_p, lp, zeros(bp, 2, H_A, HEAD_A, HEAD_A),
                                         zeros(bp, 2, H_B, HEAD_B, HEAD_B), zeros(bp, 2, H_B, HEAD_B),
                                         zeros(bp, 2, H_B), False)
    x1s, hns, idxs, gatess, _ = _mixers(x_sample, mod_s, lp, state_rwkv[:, 0], state_mlstm_C[:, 0],
                                        state_mlstm_n[:, 0], state_mlstm_m[:, 0], True)

    n_p = bp * lp_len
    idx = jnp.concatenate([idxp, idxs], axis=0)
    hn = jnp.concatenate([hnp, hns], axis=0)
    n_tok = idx.shape[0]
    rank, counts = _expert_ranks(idx)
    counts = counts[0]
    padded = (counts + MOE_BLOCK - 1) // MOE_BLOCK * MOE_BLOCK
    pad_end = jnp.cumsum(padded)
    pad_start = pad_end - padded
    n_blocks = n_tok * TOP_K // MOE_BLOCK + N_EXPERTS
    block_start = jnp.arange(n_blocks, dtype=jnp.int32) * MOE_BLOCK
    block_e = jnp.minimum(jnp.sum(pad_end[None, :] <= block_start[:, None], axis=1), N_EXPERTS - 1).astype(jnp.int32)
    n_used = (pad_end[-1:] // MOE_BLOCK).astype(jnp.int32)
    dest = (pad_start[idx] + rank).astype(jnp.int32).reshape(-1)
    xs = _dispatch(dest, pad_end.astype(jnp.int32), hn, n_blocks * MOE_BLOCK)
    yb = _experts(block_e, n_used, xs, moe_w1[0], moe_b1[0], moe_w2[0], moe_b2[0])
    y_prompt = _combine(dest, yb, x1p, gatesp, mod_p, final_g, 0).reshape(x_prompt.shape)
    y_sample = _combine(dest, yb, x1s, gatess, mod_s, final_g, n_p).reshape(x_sample.shape)

    s_new, c_new, n_new, m_new = st
    return (y_prompt, y_sample, s_new[:, None], c_new[:, None], n_new[:, None], m_new[:, None])
```

```python
import functools

import jax
import jax.numpy as jnp
from jax import lax
from jax.experimental import pallas as pl
from jax.experimental.pallas import tpu as pltpu

F32 = jnp.float32
BF16 = jnp.bfloat16
HIGHEST = lax.Precision.HIGHEST

D_MODEL = 1024
DA = 512
HEAD_A = 64
H_A = DA // HEAD_A
DB = 512
H_B = 4
HEAD_B = DB // H_B
DECAY_LORA = 64
AAA_LORA = 64
GATE_LORA = 128
RWKV_COLS = 3 * DA + DECAY_LORA + AAA_LORA + GATE_LORA
MLSTM_MAIN = 4 * DB
N_GATES = 4 * H_B
GRID_W = 64
CHUNK = 64
N_EXPERTS = 32
TOP_K = 4
D_FF = D_MODEL
SWIGLU_LIMIT = 7.0
SWIGLU_ALPHA = 1.702
MOE_BLOCK = 512
NORM_EPS = 1e-6
GN_EPS = 64e-5
ROW_TILE = 256
PROJ_TILE = 512
SUBLANES = 8
LANES = 128
LANE_ROWS = D_MODEL // LANES
VMEM_LIMIT = 56 * 1024 * 1024
PACK = 4
PACK_W = PACK * HEAD_A
N_PACKS = H_A // PACK
SCAN_GROUP = 2
SCAN_DOUBLE_BUFFER_BYTES = 12 * 1024 * 1024

NN = (((1,), (0,)), ((), ()))
NT = (((1,), (1,)), ((), ()))
TN = (((0,), (0,)), ((), ()))


def _mm(a, b, precision=HIGHEST):
    return lax.dot_general(a, b, NN, precision=precision, preferred_element_type=F32)


def _mm_nt(a, b, precision=HIGHEST):
    return lax.dot_general(a, b, NT, precision=precision, preferred_element_type=F32)


def _mm_tn(a, b, precision=HIGHEST):
    return lax.dot_general(a, b, TN, precision=precision, preferred_element_type=F32)


def _dot_bf16(a, b, dims=NN):
    return lax.dot_general(a.astype(BF16), b.astype(BF16), dims, preferred_element_type=F32)


def _sigmoid(x):
    return 1.0 / (1.0 + jnp.exp(-x))


def _log_sigmoid(x):
    return jnp.minimum(x, 0.0) - jnp.log(1.0 + jnp.exp(-jnp.abs(x)))


def _params(*sem, **kw):
    return pltpu.CompilerParams(dimension_semantics=sem, vmem_limit_bytes=VMEM_LIMIT, **kw)


def _split_bf16(x):
    hi = x.astype(BF16)
    return hi, (x - hi.astype(F32)).astype(BF16)


def _const_spec(shape):
    nd = len(shape)
    return pl.BlockSpec(shape, lambda *_: (0,) * nd)


def _tri(n, reverse, strict):
    t = lax.broadcasted_iota(jnp.int32, (n, n), 0)
    s = lax.broadcasted_iota(jnp.int32, (n, n), 1)
    if reverse:
        return (s > t) if strict else (s >= t)
    return (s < t) if strict else (s <= t)


def _seq_neighbours(ref, b, c, nc, cols):
    r0 = pl.multiple_of(c * CHUNK, CHUNK)
    zc = ref[b, pl.ds(r0, CHUNK), cols]
    row = lax.broadcasted_iota(jnp.int32, zc.shape, 0)
    p0 = pl.multiple_of(jnp.maximum(r0 - SUBLANES, 0), SUBLANES)
    n0 = pl.multiple_of(jnp.minimum(r0 + CHUNK, (nc - 1) * CHUNK), SUBLANES)
    before = ref[b, pl.ds(p0, SUBLANES), cols][SUBLANES - 1:SUBLANES]
    after = ref[b, pl.ds(n0, SUBLANES), cols][0:1]
    before = jnp.where(c > 0, before, 0.0)
    after = jnp.where(c < nc - 1, after, 0.0)
    prev = jnp.where(row == 0, before, pltpu.roll(zc, 1, 0))
    nxt = jnp.where(row == CHUNK - 1, after, pltpu.roll(zc, CHUNK - 1, 0))
    return zc, prev, nxt


def _ada_kernel(c_ref, w_ref, b_ref, o_ref):
    cc = c_ref[...]
    o_ref[...] = _dot_bf16(cc * _sigmoid(cc), w_ref[...]) + b_ref[...]


def _ada_mod(cc, ada_w, ada_b):
    rows = cc.shape[0]
    ncol = ada_w.shape[1]
    tn = 1536
    return pl.pallas_call(
        _ada_kernel,
        grid=(ncol // tn,),
        in_specs=[_const_spec((rows, D_MODEL)),
                  pl.BlockSpec((D_MODEL, tn), lambda j: (0, j)),
                  pl.BlockSpec((1, tn), lambda j: (0, j))],
        out_specs=pl.BlockSpec((rows, tn), lambda j: (0, j)),
        out_shape=jax.ShapeDtypeStruct((rows, ncol), F32),
        compiler_params=_params("arbitrary"),
        name="ada_mod",
    )(cc, ada_w, ada_b.reshape(1, ncol))


def _inproj_kernel(x_ref, mod_ref, g_ref, wr_ref, wm_ref, wg_ref, wgt_ref, zr_ref, zm_ref, zg_ref, zgt_ref):
    x = x_ref[...]
    mod = mod_ref[0]
    sh, sc = mod[:, 0:D_MODEL], mod[:, D_MODEL:2 * D_MODEL]
    y = x * lax.rsqrt(jnp.mean(x * x, axis=-1, keepdims=True) + NORM_EPS)
    hn = y * g_ref[...] * (1.0 + sc) + sh
    hb = hn.astype(BF16)
    zr_ref[...] = lax.dot_general(hb, wr_ref[...], NN, preferred_element_type=F32)
    zm_ref[...] = lax.dot_general(hb, wm_ref[...], NN, preferred_element_type=F32)
    zg_ref[...] = _dot_bf16(hb, wg_ref[...])
    zgt_ref[...] = _dot_bf16(wgt_ref[...], hb, NT)


def _inproj(x2, mod, norm_g, w_r, w_m, w_g, w_gt):
    rows = x2.shape[0]
    tiles_per_mod = rows // mod.shape[0] // PROJ_TILE
    return pl.pallas_call(
        _inproj_kernel,
        grid=(rows // PROJ_TILE,),
        in_specs=[pl.BlockSpec((PROJ_TILE, D_MODEL), lambda i: (i, 0)),
                  pl.BlockSpec((1, 1, 6 * D_MODEL), lambda i: (i // tiles_per_mod, 0, 0)),
                  _const_spec((1, D_MODEL)),
                  _const_spec(w_r.shape), _const_spec(w_m.shape), _const_spec(w_g.shape),
                  _const_spec(w_gt.shape)],
        out_specs=[pl.BlockSpec((PROJ_TILE, RWKV_COLS), lambda i: (i, 0)),
                   pl.BlockSpec((PROJ_TILE, MLSTM_MAIN), lambda i: (i, 0)),
                   pl.BlockSpec((PROJ_TILE, N_GATES), lambda i: (i, 0)),
                   pl.BlockSpec((N_GATES, PROJ_TILE), lambda i: (0, i))],
        out_shape=[jax.ShapeDtypeStruct((rows, RWKV_COLS), F32),
                   jax.ShapeDtypeStruct((rows, MLSTM_MAIN), F32),
                   jax.ShapeDtypeStruct((rows, N_GATES), F32),
                   jax.ShapeDtypeStruct((N_GATES, rows), F32)],
        compiler_params=_params("arbitrary"),
        name="inproj",
    )(x2, mod, norm_g.reshape(1, D_MODEL), w_r, w_m, w_g, w_gt)


def _group_sum(x, ones_bd):
    hi = x.astype(BF16)
    lo = (x - hi.astype(F32)).astype(BF16)
    return (lax.dot_general(hi, ones_bd, NN, preferred_element_type=F32)
            + lax.dot_general(lo, ones_bd, NN, preferred_element_type=F32))


def _block_diag(x, ones_bd):
    return jnp.concatenate([x.astype(BF16)] * PACK, axis=0) * ones_bd


def _rwkv_body(zr_ref, s0_ref, mu_ref, kkw_ref, ka_ref, rk_ref, gnw_ref, gnb_ref, w0_ref, w2_ref,
                 a0_ref, a2_ref, g2_ref, ones_ref, y_ref, sout_ref,
                 r_s, k_s, v_s, kk_s, gate_s, lx_s, st_s, *, seq_len, grid, group):
    nc = seq_len // CHUNK
    ones_bd = ones_ref[...]

    def mix_chunk(c, carry):
        r0 = pl.multiple_of(c * CHUNK, CHUNK)
        for b in range(group):
            if grid:
                zc = zr_ref[b, pl.ds(r0, CHUNK), :]
                row = lax.broadcasted_iota(jnp.int32, zc.shape, 0)
                up0 = pl.multiple_of(jnp.maximum(c - 1, 0) * CHUNK, CHUNK)
                dn0 = pl.multiple_of(jnp.minimum(c + 1, nc - 1) * CHUNK, CHUNK)
                up = jnp.where(c > 0, zr_ref[b, pl.ds(up0, CHUNK), :], 0.0)
                down = jnp.where(c < nc - 1, zr_ref[b, pl.ds(dn0, CHUNK), :], 0.0)
                left = jnp.where(row == 0, 0.0, pltpu.roll(zc, 1, 0))
                right = jnp.where(row == CHUNK - 1, 0.0, pltpu.roll(zc, CHUNK - 1, 0))
                local = 0.25 * (up + down + left + right)
            else:
                zc, prev, nxt = _seq_neighbours(zr_ref, b, c, nc, slice(None))
                local = 0.5 * (prev + nxt)
            z = zc + (local - zc) * mu_ref[...]
            r, k, v = z[:, 0:DA], z[:, DA:2 * DA], z[:, 2 * DA:3 * DA]
            lx = z[:, 3 * DA:3 * DA + DECAY_LORA + AAA_LORA]
            xg = z[:, 3 * DA + DECAY_LORA + AAA_LORA:]
            kk = k * kkw_ref[...]
            sq = kk * kk
            ss = jnp.concatenate([_group_sum(sq[:, p * PACK_W:(p + 1) * PACK_W], ones_bd) for p in range(N_PACKS)],
                                 axis=1)
            kk = kk / jnp.maximum(jnp.sqrt(ss), 1e-12)
            lane = lax.broadcasted_iota(jnp.int32, lx.shape, 1)
            r_s[b, pl.ds(r0, CHUNK), :] = r
            k_s[b, pl.ds(r0, CHUNK), :] = k
            v_s[b, pl.ds(r0, CHUNK), :] = v
            kk_s[b, pl.ds(r0, CHUNK), :] = kk
            gate_s[b, pl.ds(r0, CHUNK), :] = _dot_bf16(_sigmoid(xg), g2_ref[...])
            lx_s[b, pl.ds(r0, CHUNK), :] = jnp.where(lane < DECAY_LORA, jnp.tanh(lx), lx)
        return carry

    lax.fori_loop(0, nc, mix_chunk, 0)

    rowblk = lax.broadcasted_iota(jnp.int32, (PACK_W, PACK_W), 0) // HEAD_A
    colblk = lax.broadcasted_iota(jnp.int32, (PACK_W, PACK_W), 1) // HEAD_A
    for b in range(group):
        for d in range(2):
            for p in range(N_PACKS):
                rows_ = jnp.concatenate([s0_ref[b, d, p * PACK + h] for h in range(PACK)], axis=0)
                st_s[b, d, p] = jnp.where(rowblk == colblk, jnp.concatenate([rows_] * PACK, axis=1), 0.0)

    t_idx = lax.broadcasted_iota(jnp.int32, (CHUNK, PACK_W), 0)
    s_idx = lax.broadcasted_iota(jnp.int32, (CHUNK, PACK_W), 1) % CHUNK

    def stages(i):
        first_touch = i < nc // 2
        dot = functools.partial(lax.dot_general, preferred_element_type=F32)
        chains, out_sel = [], []
        for b, d in [(b, d) for b in range(group) for d in range(2)]:
            reverse = d == 1
            c = nc - 1 - i if reverse else i
            rows = pl.ds(pl.multiple_of(c * CHUNK, CHUNK), CHUNK)
            out_sel.append((b, rows))
            strict = (s_idx > t_idx) if reverse else (s_idx < t_idx)
            incl = (s_idx >= t_idx) if reverse else (s_idx <= t_idx)
            last = 0 if reverse else CHUNK - 1
            r, k, v, kk = r_s[b, rows, :], k_s[b, rows, :], v_s[b, rows, :], kk_s[b, rows, :]
            lx = lx_s[b, rows, :]
            wl = w0_ref[d:d + 1, :] + _dot_bf16(lx[:, 0:DECAY_LORA], w2_ref[d])
            logw = -jnp.exp(_log_sigmoid(wl) - 0.5)
            a = _sigmoid(a0_ref[d:d + 1, :] + _dot_bf16(lx[:, DECAY_LORA:], a2_ref[d]))
            kd = k * (1.0 + (a - 1.0) * ka_ref[...])
            kb = kk * a
            tri = _tri(CHUNK, reverse, False).astype(BF16)
            w1, w_rest = _split_bf16(logw)
            w2, w3 = _split_bf16(w_rest)
            cum = dot(tri, w1, NN) + dot(tri, w2, NN) + dot(tri, w3, NN)
            clast = cum[last:last + 1]
            rt = r * jnp.exp(cum)
            at = -kk * jnp.exp(cum - logw)
            einv = jnp.exp(-cum)
            kt, bt = kd * einv, kb * einv
            edec = jnp.exp(clast - cum)
            kp, bp = kd * edec, kb * edec
            ptot = jnp.exp(clast)
            rkd = r * kd * rk_ref[...]
            for p in range(N_PACKS):
                cs = slice(p * PACK_W, (p + 1) * PACK_W)
                chains.append(dict(
                    out=len(out_sel) - 1, st=(b, d, p), cs=cs, strict=strict, incl=incl, v=v[:, cs],
                    rkd=rkd[:, cs], ptot=ptot[:, cs],
                    ar=jnp.concatenate([at[:, cs], rt[:, cs]], axis=0).astype(BF16),
                    kt_bd=_block_diag(kt[:, cs], ones_bd), bt_bd=_block_diag(bt[:, cs], ones_bd),
                    v_bd=_block_diag(v[:, cs], ones_bd),
                    kbp=jnp.concatenate([kp[:, cs], bp[:, cs]], axis=0)))
            yield

        for ch in chains:
            ch['s_prev'] = st_s[ch['st']]
            ch['a_k'] = dot(ch['ar'], ch['kt_bd'], NT)
            ch['a_b'] = dot(ch['ar'], ch['bt_bd'], NT)
            ch['ars'] = dot(ch['ar'], ch['s_prev'].astype(BF16), NT)
        yield
        for ch in chains:
            a_ak = jnp.where(ch['strict'], ch['a_k'][:CHUNK], 0.0)
            a_rk = jnp.where(ch['incl'], ch['a_k'][CHUNK:], 0.0)
            ch['a_rb'] = jnp.where(ch['incl'], ch['a_b'][CHUNK:], 0.0).astype(BF16)
            n = jnp.where(ch['strict'], ch['a_b'][:CHUNK], 0.0)
            ch['n_hi'], n_lo = _split_bf16(n)
            ch['n_both'] = jnp.concatenate([ch['n_hi'], n_lo], axis=0)
            ch['pw'] = ch['n_hi']
            ch['m'] = jnp.where(s_idx == t_idx, 1.0, 0.0) + n
            akv = dot(jnp.concatenate([a_ak, a_rk], axis=0).astype(BF16), ch['v_bd'], NN)
            ch['rhs'] = ch['ars'][:CHUNK] + akv[:CHUNK]
            ch['y'] = ch['ars'][CHUNK:] + akv[CHUNK:]
        yield
        for ch in chains:
            ch['pw'] = dot(ch['pw'], _block_diag(ch['pw'], ones_bd), NN).astype(BF16)
        yield
        for j in range(1, 5):
            for ch in chains:
                pw_bd = _block_diag(ch['pw'], ones_bd)
                if j < 4:
                    both = dot(jnp.concatenate([ch['m'].astype(BF16), ch['pw']], axis=0), pw_bd, NN)
                    ch['m'] = ch['m'] + both[:CHUNK]
                    ch['pw'] = both[CHUNK:].astype(BF16)
                else:
                    ch['m'] = ch['m'] + dot(ch['m'].astype(BF16), pw_bd, NN)
            yield
        for ch in chains:
            ch['mb'] = ch['m'].astype(BF16)
            ch['u'] = dot(ch['mb'], _block_diag(ch['rhs'], ones_bd), NN)
        yield
        for ch in chains:
            u_hi, u_lo = _split_bf16(ch['u'])
            both = dot(ch['n_both'], _block_diag(u_hi, ones_bd), NN)
            nu = both[:CHUNK] + both[CHUNK:] + dot(ch['n_hi'], _block_diag(u_lo, ones_bd), NN)
            ch['resid'] = ch['rhs'] - ch['u'] + nu
        yield
        for ch in chains:
            ch['u'] = ch['u'] + dot(ch['mb'], _block_diag(ch['resid'], ones_bd), NN)
        yield
        for ch in chains:
            u = ch['u']
            ch['y'] = ch['y'] + dot(ch['a_rb'], _block_diag(u, ones_bd), NN)
            vu_hi, vu_lo = _split_bf16(jnp.concatenate([ch['v'], u], axis=0))
            kbp_hi, kbp_lo = _split_bf16(ch['kbp'])
            grown = dot(vu_hi, kbp_hi, TN) + dot(vu_hi, kbp_lo, TN) + dot(vu_lo, kbp_hi, TN)
            st_s[ch['st']] = ch['s_prev'] * ch['ptot'] + grown * ones_bd.astype(F32)
        yield
        n_ch = len(chains)
        sums = dot(jnp.concatenate([ch['y'] for ch in chains] + [ch['rkd'] for ch in chains],
                                   axis=0).astype(BF16), ones_bd, NN)
        ycs = [ch['y'] - sums[q * CHUNK:(q + 1) * CHUNK] * (1.0 / HEAD_A) for q, ch in enumerate(chains)]
        var = dot(jnp.concatenate([yc * yc for yc in ycs], axis=0).astype(BF16), ones_bd, NN) * (1.0 / HEAD_A)
        outs = [[] for _ in out_sel]
        for q, ch in enumerate(chains):
            cs = ch['cs']
            out = ycs[q] * lax.rsqrt(var[q * CHUNK:(q + 1) * CHUNK] + GN_EPS) * gnw_ref[:, cs] + gnb_ref[:, cs]
            outs[ch['out']].append(out + sums[(n_ch + q) * CHUNK:(n_ch + q + 1) * CHUNK] * ch['v'])
        for (b, rows), parts in zip(out_sel, outs):
            out = jnp.concatenate(parts, axis=1)

            @pl.when(first_touch)
            def _(b=b, rows=rows, out=out):
                y_ref[b, rows, :] = out

            @pl.when(jnp.logical_not(first_touch))
            def _(b=b, rows=rows, out=out):
                y_ref[b, rows, :] = (y_ref[b, rows, :] + out) * gate_s[b, rows, :]

    def finish():
        for b in range(group):
            for d in range(2):
                for p in range(N_PACKS):
                    s_fin = st_s[b, d, p]
                    for h in range(PACK):
                        sout_ref[b, d, p * PACK + h] = s_fin[h * HEAD_A:(h + 1) * HEAD_A,
                                                             h * HEAD_A:(h + 1) * HEAD_A]

    return stages, finish


def _mlstm_body(zm_ref, g_ref, gt_ref, c0_ref, n0_ref, m0_ref, conv_ref, bi_ref, bit_ref, bf_ref, bft_ref,
                  ng_ref, erep_ref, y_ref, cout_ref, nout_ref, mout_ref,
                  qk_s, h_s, c_s, n_s, m_s, *, seq_len, group):
    nc = seq_len // CHUNK

    def conv_chunk(c, carry):
        for b in range(group):
            zc, prev, nxt = _seq_neighbours(zm_ref, b, c, nc, slice(0, 2 * DB))
            u = conv_ref[0:1, :] * prev + conv_ref[1:2, :] * zc + conv_ref[2:3, :] * nxt
            qk = u * _sigmoid(u)
            lane = lax.broadcasted_iota(jnp.int32, qk.shape, 1)
            qk_s[b, pl.ds(pl.multiple_of(c * CHUNK, CHUNK), CHUNK), :] = jnp.where(lane >= DB,
                                                                                  qk * (HEAD_B ** -0.5), qk)
        return carry

    lax.fori_loop(0, nc, conv_chunk, 0)

    for b in range(group):
        for d in range(2):
            c_s[b, d] = c0_ref[b, d]
            n_s[b, d] = n0_ref[b, d]
            m_s[b, d] = jnp.broadcast_to(m0_ref[b, d], (H_B, LANES))

    e_rep = erep_ref[...]
    ones_tl = jnp.ones((CHUNK, LANES), BF16)
    row_id = lax.broadcasted_iota(jnp.int32, (CHUNK, LANES), 0)

    def scan_max(x, reverse):
        shift = 1
        while shift < CHUNK:
            if reverse:
                moved = jnp.where(row_id < CHUNK - shift, pltpu.roll(x, CHUNK - shift, 0), -jnp.inf)
            else:
                moved = jnp.where(row_id >= shift, pltpu.roll(x, shift, 0), -jnp.inf)
            x = jnp.maximum(x, moved)
            shift *= 2
        return x

    def dot3(a, b, dims):
        a_hi, a_lo = _split_bf16(a)
        b_hi, b_lo = _split_bf16(b)
        dg = functools.partial(lax.dot_general, dimension_numbers=dims, preferred_element_type=F32)
        return dg(a_hi, b_hi) + dg(a_hi, b_lo) + dg(a_lo, b_hi)

    def stages(i):
        dot = functools.partial(lax.dot_general, preferred_element_type=F32)
        chains = []
        for b, d in [(b, d) for b in range(group) for d in range(2)]:
            reverse = d == 1
            incl = _tri(CHUNK, reverse, False)
            tri_b = incl.astype(BF16)
            tri_row = _tri(CHUNK, not reverse, False).astype(F32)
            last = 0 if reverse else CHUNK - 1
            gsl = slice(d * H_B, (d + 1) * H_B)
            fsl = slice(2 * H_B + d * H_B, 2 * H_B + (d + 1) * H_B)
            c = nc - 1 - i if reverse else i
            rows = pl.ds(pl.multiple_of(c * CHUNK, CHUNK), CHUNK)
            g1, g_rest = _split_bf16(g_ref[b, rows, :])
            g2, g3 = _split_bf16(g_rest)
            g_rep = dot(g1, e_rep, NN) + dot(g2, e_rep, NN) + dot(g3, e_rep, NN)
            grow = gt_ref[b, c]
            i_row = grow[gsl, :] + bit_ref[gsl, :]
            f_row = _log_sigmoid(grow[fsl, :] + bft_ref[gsl, :])
            b_row = _mm(f_row, tri_row)
            for h in range(H_B):
                ji, jf = d * H_B + h, 2 * H_B + d * H_B + h
                ic = g_rep[:, ji * LANES:(ji + 1) * LANES] + bi_ref[:, ji:ji + 1]
                fc = _log_sigmoid(g_rep[:, jf * LANES:(jf + 1) * LANES] + bf_ref[:, ji:ji + 1])
                f1, f_rest = _split_bf16(fc)
                f2, f3 = _split_bf16(f_rest)
                bc = dot(tri_b, f1, NN) + dot(tri_b, f2, NN) + dot(tri_b, f3, NN)
                q = qk_s[b, rows, h * HEAD_B:(h + 1) * HEAD_B]
                chains.append(dict(
                    b=b, d=d, h=h, rows=rows, incl=incl, reverse=reverse, q=q, qb=q.astype(BF16),
                    k=qk_s[b, rows, DB + h * HEAD_B:DB + (h + 1) * HEAD_B],
                    v=zm_ref[b, rows, 2 * DB + h * HEAD_B:2 * DB + (h + 1) * HEAD_B],
                    bc=bc, ic=ic, bl=bc[last:last + 1, :], br=b_row[h:h + 1, :], ir=i_row[h:h + 1, :]))
            yield
        for q, ch in enumerate(chains):
            b, d, h = ch['b'], ch['d'], ch['h']
            ch['qk'] = _dot_bf16(ch['qb'], ch['k'], NT)
            ch['c_prev'] = c_s[b, d, h]
            ch['qc'] = _dot_bf16(ch['qb'], ch['c_prev'])
            ch['n_prev'] = n_s[b, d, h:h + 1, :]
            ch['qn'] = dot3(ch['q'], jnp.broadcast_to(ch['n_prev'], (HEAD_B, HEAD_B)), NT)
            if q % 2 == 1:
                yield
        for q, ch in enumerate(chains):
            b, d, h = ch['b'], ch['d'], ch['h']
            bc, ic, bl = ch['bc'], ch['ic'], ch['bl']
            m_prev = m_s[b, d, h:h + 1, :]
            m_t = bc + jnp.maximum(m_prev, scan_max(ic - bc, ch['reverse']))
            dmat = jnp.where(ch['incl'], bc[:, :CHUNK] - ch['br'] + ch['ir'], -jnp.inf)
            s = ch['qk'] * jnp.exp(dmat - m_t[:, :CHUNK])
            carry_w = jnp.exp(bc + m_prev - m_t)
            s_hi, s_lo = _split_bf16(s)
            den = dot(s_hi, ones_tl, NN) + dot(s_lo, ones_tl, NN) + carry_w * ch['qn']
            num = dot(s_hi, ch['v'].astype(BF16), NN) + carry_w * ch['qc']
            h_s[b, d, ch['rows'], h * HEAD_B:(h + 1) * HEAD_B] = num / jnp.maximum(jnp.abs(den), jnp.exp(-m_t))
            g = bl - bc + ic
            m_new = jnp.maximum(bl + m_prev, jnp.max(g, axis=0, keepdims=True))
            wk = jnp.exp(g - m_new)
            decay = jnp.exp(bl + m_prev - m_new)
            c_s[b, d, h] = decay * ch['c_prev'] + _dot_bf16(ch['k'], wk * ch['v'], TN)
            n_s[b, d, h:h + 1, :] = decay * ch['n_prev'] + jnp.sum(wk * ch['k'], axis=0, keepdims=True)
            m_s[b, d, h:h + 1, :] = m_new
            if q % 2 == 1:
                yield

    def norm_chunk(c, carry):
        rows = pl.ds(pl.multiple_of(c * CHUNK, CHUNK), CHUNK)
        for b in range(group):
            og = zm_ref[b, rows, 3 * DB:4 * DB]
            hm = h_s[b, 0, rows, :] + h_s[b, 1, rows, :]
            parts = []
            for h in range(H_B):
                hh = hm[:, h * HEAD_B:(h + 1) * HEAD_B]
                parts.append(hh * lax.rsqrt(jnp.mean(hh * hh, axis=-1, keepdims=True) + NORM_EPS))
            y_ref[b, rows, :] = jnp.concatenate(parts, axis=1) * ng_ref[...] * _sigmoid(og)
        return carry

    def finish():
        cout_ref[...] = c_s[...]
        nout_ref[...] = n_s[...]
        mout_ref[...] = m_s[...]
        lax.fori_loop(0, nc, norm_chunk, 0)

    return stages, finish


def _scan_kernel(*refs, body, n_steps, **static):
    stages, finish = body(*refs, **static)

    def scan_step(i, carry):
        for _ in stages(i):
            pass
        return carry

    lax.fori_loop(0, n_steps, scan_step, 0)
    finish()


def _scan_specs(grp):
    lead = lambda *rest: pl.BlockSpec((grp,) + rest, lambda b: (b,) + (0,) * len(rest))

    def big(seq_len, cols):
        if 2 * grp * seq_len * cols * 4 <= SCAN_DOUBLE_BUFFER_BYTES:
            return lead(seq_len, cols)
        return pl.BlockSpec((grp, seq_len, cols), lambda b: (b, 0, 0), pipeline_mode=pl.Buffered(1))

    return lead, big


def _rwkv(zr, s0, p, grid):
    bsz, seq_len, _ = zr.shape
    nc = seq_len // CHUNK
    grp = SCAN_GROUP
    assert nc % 2 == 0 and bsz % grp == 0
    kern = functools.partial(_scan_kernel, body=_rwkv_body, n_steps=nc, seq_len=seq_len, grid=grid, group=grp)
    consts = [p['mu'], p['kkw'], p['ka'], p['rk'], p['gnw'], p['gnb'], p['w0'], p['w2'], p['a0'], p['a2'],
              p['g2'], p['ones_bd']]
    lead, big = _scan_specs(grp)
    return pl.pallas_call(
        kern,
        grid=(bsz // grp,),
        in_specs=[big(seq_len, RWKV_COLS), lead(2, H_A, HEAD_A, HEAD_A)] + [_const_spec(a.shape) for a in consts],
        out_specs=[lead(seq_len, DA), lead(2, H_A, HEAD_A, HEAD_A)],
        out_shape=[jax.ShapeDtypeStruct((bsz, seq_len, DA), F32),
                   jax.ShapeDtypeStruct((bsz, 2, H_A, HEAD_A, HEAD_A), F32)],
        scratch_shapes=[pltpu.VMEM((grp, seq_len, DA), F32)] * 5
                       + [pltpu.VMEM((grp, seq_len, DECAY_LORA + AAA_LORA), F32),
                          pltpu.VMEM((grp, 2, N_PACKS, PACK_W, PACK_W), F32)],
        compiler_params=_params("arbitrary"),
        name="rwkv_scan",
    )(zr, s0, *consts)


def _mlstm(zm, g, gt, c0, n0, m0, p):
    bsz, seq_len, _ = zm.shape
    nc = seq_len // CHUNK
    grp = SCAN_GROUP
    assert bsz % grp == 0
    kern = functools.partial(_scan_kernel, body=_mlstm_body, n_steps=nc, seq_len=seq_len, group=grp)
    consts = [p['conv'], p['bi'], p['bit'], p['bf'], p['bft'], p['ng'], p['e_rep']]
    lead, big = _scan_specs(grp)
    return pl.pallas_call(
        kern,
        grid=(bsz // grp,),
        in_specs=[big(seq_len, MLSTM_MAIN), lead(seq_len, N_GATES), lead(nc, N_GATES, CHUNK),
                  lead(2, H_B, HEAD_B, HEAD_B), lead(2, H_B, HEAD_B), lead(2, H_B, 1)]
                 + [_const_spec(a.shape) for a in consts],
        out_specs=[lead(seq_len, DB), lead(2, H_B, HEAD_B, HEAD_B), lead(2, H_B, HEAD_B), lead(2, H_B, LANES)],
        out_shape=[jax.ShapeDtypeStruct((bsz, seq_len, DB), F32),
                   jax.ShapeDtypeStruct((bsz, 2, H_B, HEAD_B, HEAD_B), F32),
                   jax.ShapeDtypeStruct((bsz, 2, H_B, HEAD_B), F32),
                   jax.ShapeDtypeStruct((bsz, 2, H_B, LANES), F32)],
        scratch_shapes=[pltpu.VMEM((grp, seq_len, 2 * DB), F32), pltpu.VMEM((grp, 2, seq_len, DB), F32),
                        pltpu.VMEM((grp, 2, H_B, HEAD_B, HEAD_B), F32), pltpu.VMEM((grp, 2, H_B, HEAD_B), F32),
                        pltpu.VMEM((grp, 2, H_B, LANES), F32)],
        compiler_params=_params("arbitrary"),
        name="mlstm_scan",
    )(zm, g, gt, c0, n0, m0, *consts)


def _outproj_kernel(x_ref, yr_ref, ym_ref, mod_ref, g_ref, wo_ref, rw_ref, rb_ref,
                    x1_ref, hn_ref, idx_ref, gate_ref):
    mod = mod_ref[0]
    g1 = mod[:, 2 * D_MODEL:3 * D_MODEL]
    sh2, sc2 = mod[:, 3 * D_MODEL:4 * D_MODEL], mod[:, 4 * D_MODEL:5 * D_MODEL]
    mix = _dot_bf16(yr_ref[...], wo_ref[0:DA, :]) + _dot_bf16(ym_ref[...], wo_ref[DA:, :])
    x1 = x_ref[...] + g1 * mix
    x1_ref[...] = x1
    y = x1 * lax.rsqrt(jnp.mean(x1 * x1, axis=-1, keepdims=True) + NORM_EPS)
    hn = y * g_ref[...] * (1.0 + sc2) + sh2
    hn_ref[...] = hn
    logits = _dot_bf16(hn, rw_ref[...]) + rb_ref[...]
    lane = lax.broadcasted_iota(jnp.int32, logits.shape, 1)
    vals, idxs = [], []
    for _ in range(TOP_K):
        top = jnp.max(logits, axis=-1, keepdims=True)
        pick = jnp.min(jnp.where(logits == top, lane, N_EXPERTS), axis=-1, keepdims=True)
        vals.append(top)
        idxs.append(pick)
        logits = jnp.where(lane == pick, -jnp.inf, logits)
    exps = [jnp.exp(v - vals[0]) for v in vals]
    total = exps[0] + exps[1] + exps[2] + exps[3]
    k_lane = lax.broadcasted_iota(jnp.int32, (x1.shape[0], TOP_K), 1)
    gates = jnp.zeros((x1.shape[0], TOP_K), F32)
    picks = jnp.zeros((x1.shape[0], TOP_K), jnp.int32)
    for j in range(TOP_K):
        gates = jnp.where(k_lane == j, exps[j] / total, gates)
        picks = jnp.where(k_lane == j, idxs[j], picks)
    idx_ref[...] = picks
    gate_ref[...] = gates


def _outproj(x2, yr, ym, mod, norm_g, w_out, router_w, router_b):
    rows = x2.shape[0]
    tiles_per_mod = rows // mod.shape[0] // PROJ_TILE
    row_spec = lambda w: pl.BlockSpec((PROJ_TILE, w), lambda i: (i, 0))
    return pl.pallas_call(
        _outproj_kernel,
        grid=(rows // PROJ_TILE,),
        in_specs=[row_spec(D_MODEL), row_spec(DA), row_spec(DB),
                  pl.BlockSpec((1, 1, 6 * D_MODEL), lambda i: (i // tiles_per_mod, 0, 0)),
                  _const_spec((1, D_MODEL)), _const_spec(w_out.shape), _const_spec(router_w.shape),
                  _const_spec((1, N_EXPERTS))],
        out_specs=[row_spec(D_MODEL), row_spec(D_MODEL), row_spec(TOP_K), row_spec(TOP_K)],
        out_shape=[jax.ShapeDtypeStruct((rows, D_MODEL), F32), jax.ShapeDtypeStruct((rows, D_MODEL), F32),
                   jax.ShapeDtypeStruct((rows, TOP_K), jnp.int32), jax.ShapeDtypeStruct((rows, TOP_K), F32)],
        compiler_params=_params("arbitrary"),
        name="outproj_router",
    )(x2, yr, ym, mod, norm_g.reshape(1, D_MODEL), w_out, router_w, router_b.reshape(1, N_EXPERTS))


def _rank_kernel(idx_ref, rank_ref, count_ref, run_s):
    i = pl.program_id(0)

    @pl.when(i == 0)
    def _():
        run_s[...] = jnp.zeros_like(run_s)

    idx = idx_ref[...]
    rows = idx.shape[0]
    lane = lax.broadcasted_iota(jnp.int32, (rows, N_EXPERTS), 1)
    hot = jnp.zeros((rows, N_EXPERTS), F32)
    for j in range(TOP_K):
        hot = hot + jnp.where(lane == idx[:, j:j + 1], 1.0, 0.0)
    before = _dot_bf16(_tri(rows, False, True).astype(F32), hot) + run_s[...]
    k_lane = lax.broadcasted_iota(jnp.int32, (rows, TOP_K), 1)
    rank = jnp.zeros((rows, TOP_K), F32)
    for j in range(TOP_K):
        rj = jnp.sum(jnp.where(lane == idx[:, j:j + 1], before, 0.0), axis=-1, keepdims=True)
        rank = jnp.where(k_lane == j, rj, rank)
    rank_ref[...] = rank.astype(jnp.int32)
    run_s[...] = run_s[...] + jnp.sum(hot, axis=0, keepdims=True)
    count_ref[...] = run_s[...].astype(jnp.int32)


def _expert_ranks(idx):
    rows = idx.shape[0]
    return pl.pallas_call(
        _rank_kernel,
        grid=(rows // ROW_TILE,),
        in_specs=[pl.BlockSpec((ROW_TILE, TOP_K), lambda i: (i, 0))],
        out_specs=[pl.BlockSpec((ROW_TILE, TOP_K), lambda i: (i, 0)), _const_spec((1, N_EXPERTS))],
        out_shape=[jax.ShapeDtypeStruct((rows, TOP_K), jnp.int32),
                   jax.ShapeDtypeStruct((1, N_EXPERTS), jnp.int32)],
        scratch_shapes=[pltpu.VMEM((1, N_EXPERTS), F32)],
        compiler_params=_params("arbitrary"),
        name="expert_ranks",
    )(idx)


def _dispatch_kernel(dest_ref, pend_ref, xa_ref, xb_ref, xs_ref, zero_s, sem, *, n_first_tiles):
    @pl.when(pl.program_id(0) == 0)
    def _():
        zero_s[...] = jnp.zeros_like(zero_s)

        def zero_block(start):
            return pltpu.make_async_copy(zero_s, xs_ref.at[pl.ds(pl.multiple_of(start, MOE_BLOCK), MOE_BLOCK)], sem)

        def has_rows(e):
            return pend_ref[e] > (pend_ref[e - 1] if e else 0)

        used_end = pend_ref[N_EXPERTS - 1]
        n_tail = (xs_ref.shape[0] - used_end) // MOE_BLOCK

        def start_tail(b, carry):
            zero_block(used_end + b * MOE_BLOCK).start()
            return carry

        def wait_tail(b, carry):
            zero_block(0).wait()
            return carry

        for e in range(N_EXPERTS):
            @pl.when(has_rows(e))
            def _(e=e):
                zero_block(pend_ref[e] - MOE_BLOCK).start()
        lax.fori_loop(0, n_tail, start_tail, 0)
        for e in range(N_EXPERTS):
            @pl.when(has_rows(e))
            def _():
                zero_block(0).wait()
        lax.fori_loop(0, n_tail, wait_tail, 0)

    def issue_from(x_ref):
        def issue(g, carry):
            for q in range(SUBLANES):
                for j in range(TOP_K):
                    slot = dest_ref[g * (SUBLANES * TOP_K) + q * TOP_K + j]
                    pltpu.make_async_copy(x_ref.at[g, pl.ds(q, 1)], xs_ref.at[pl.ds(slot, 1)],
                                          sem).start(priority=j % 2)
            return carry
        lax.fori_loop(0, ROW_TILE // SUBLANES, issue, 0)

    pl.when(pl.program_id(0) < n_first_tiles)(lambda: issue_from(xa_ref))
    pl.when(pl.program_id(0) >= n_first_tiles)(lambda: issue_from(xb_ref))
    all_rows = xs_ref.at[pl.ds(0, ROW_TILE * TOP_K)]
    pltpu.make_async_copy(all_rows, all_rows, sem).wait()


def _dispatch(dest_flat, pad_end, xa, xb, n_slots):
    na, nb = xa.shape[0] // ROW_TILE, xb.shape[0] // ROW_TILE
    tiles = lambda x: x.reshape(x.shape[0] // SUBLANES, SUBLANES, D_MODEL)
    block = (ROW_TILE // SUBLANES, SUBLANES, D_MODEL)
    return pl.pallas_call(
        functools.partial(_dispatch_kernel, n_first_tiles=na),
        grid=(na + nb,),
        in_specs=[pl.BlockSpec((ROW_TILE * TOP_K,), lambda i: (i,), memory_space=pltpu.SMEM),
                  pl.BlockSpec(memory_space=pltpu.SMEM),
                  pl.BlockSpec(block, lambda i: (jnp.minimum(i, na - 1), 0, 0)),
                  pl.BlockSpec(block, lambda i: (jnp.maximum(i - na, 0), 0, 0))],
        out_specs=pl.BlockSpec(memory_space=pl.ANY),
        out_shape=jax.ShapeDtypeStruct((n_slots, D_MODEL), xa.dtype),
        scratch_shapes=[pltpu.VMEM((MOE_BLOCK, D_MODEL), F32), pltpu.SemaphoreType.DMA(())],
        compiler_params=_params("arbitrary", disable_bounds_checks=True),
        name="moe_dispatch",
    )(dest_flat, pad_end, tiles(xa), tiles(xb))


def _expert_kernel(be_ref, nb_ref, xs_ref, w1_ref, b1_ref, w2_ref, b2_ref, y_ref, w1_s, w2_s):
    i = pl.program_id(0)
    changed = jnp.logical_or(i == 0, be_ref[i] != be_ref[jnp.maximum(i - 1, 0)])

    @pl.when(jnp.logical_and(changed, i < nb_ref[0]))
    def _():
        w1_s[...] = w1_ref[0].astype(BF16)
        w2_s[...] = w2_ref[0].astype(BF16)

    @pl.when(i < nb_ref[0])
    def _():
        xb = xs_ref[...].astype(BF16)
        hu = lax.dot_general(xb, w1_s[...], NN, preferred_element_type=F32) + b1_ref[0]
        glu = jnp.minimum(hu[:, :D_FF], SWIGLU_LIMIT)
        lin = jnp.clip(hu[:, D_FF:], -SWIGLU_LIMIT, SWIGLU_LIMIT)
        act = glu * _sigmoid(SWIGLU_ALPHA * glu) * (lin + 1.0)
        y_ref[...] = lax.dot_general(act.astype(BF16), w2_s[...], NN, preferred_element_type=F32) + b2_ref[0]

    @pl.when(i >= nb_ref[0])
    def _():
        y_ref[...] = jnp.zeros_like(y_ref)


def _experts(block_e, n_used, xs, w1, b1, w2, b2):
    n_blocks = xs.shape[0] // MOE_BLOCK
    grid_spec = pltpu.PrefetchScalarGridSpec(
        num_scalar_prefetch=2,
        grid=(n_blocks,),
        in_specs=[pl.BlockSpec((MOE_BLOCK, D_MODEL), lambda i, be, nb: (jnp.minimum(i, nb[0] - 1), 0)),
                  pl.BlockSpec((1, D_MODEL, 2 * D_FF), lambda i, be, nb: (be[i], 0, 0)),
                  pl.BlockSpec((1, 1, 2 * D_FF), lambda i, be, nb: (be[i], 0, 0)),
                  pl.BlockSpec((1, D_FF, D_MODEL), lambda i, be, nb: (be[i], 0, 0)),
                  pl.BlockSpec((1, 1, D_MODEL), lambda i, be, nb: (be[i], 0, 0))],
        out_specs=pl.BlockSpec((MOE_BLOCK, D_MODEL), lambda i, be, nb: (i, 0)),
        scratch_shapes=[pltpu.VMEM((D_MODEL, 2 * D_FF), BF16), pltpu.VMEM((D_FF, D_MODEL), BF16)],
    )
    return pl.pallas_call(
        _expert_kernel,
        grid_spec=grid_spec,
        out_shape=jax.ShapeDtypeStruct(xs.shape, F32),
        compiler_params=_params("arbitrary"),
        name="moe_experts",
    )(block_e, n_used, xs, w1, b1.reshape(N_EXPERTS, 1, 2 * D_FF), w2, b2.reshape(N_EXPERTS, 1, D_MODEL))


def _combine_kernel(dest_ref, yb_ref, x1_ref, gate_ref, mod_ref, fg_ref, o_ref, buf, sem):
    def issue(g, carry):
        for q in range(SUBLANES):
            for j in range(TOP_K):
                slot = dest_ref[g * (SUBLANES * TOP_K) + q * TOP_K + j]
                pltpu.make_async_copy(yb_ref.at[pl.ds(slot, 1)], buf.at[j, g, pl.ds(q, 1)], sem).start(priority=j % 2)
        return carry

    lax.fori_loop(0, ROW_TILE // SUBLANES, issue, 0)
    pltpu.make_async_copy(buf, buf, sem).wait()

    g2 = mod_ref[0][:, 5 * D_MODEL:6 * D_MODEL]
    gates = gate_ref[...]
    moe = jnp.zeros((ROW_TILE, D_MODEL), F32)
    for j in range(TOP_K):
        moe = moe + buf[j].reshape(ROW_TILE, D_MODEL) * gates[:, j:j + 1]
    x2 = x1_ref[...] + g2 * moe
    y = x2 * lax.rsqrt(jnp.mean(x2 * x2, axis=-1, keepdims=True) + NORM_EPS)
    o_ref[...] = y * fg_ref[...]


def _combine(dest_flat, yb, x1, gates, mod, final_g, row_offset):
    rows = x1.shape[0]
    tiles_per_mod = rows // mod.shape[0] // ROW_TILE
    tile_offset = row_offset // ROW_TILE
    row_spec = pl.BlockSpec((ROW_TILE, D_MODEL), lambda i: (i, 0))
    return pl.pallas_call(
        _combine_kernel,
        grid=(rows // ROW_TILE,),
        in_specs=[pl.BlockSpec((ROW_TILE * TOP_K,), lambda i: (i + tile_offset,), memory_space=pltpu.SMEM),
                  pl.BlockSpec(memory_space=pl.ANY),
                  row_spec,
                  pl.BlockSpec((ROW_TILE, TOP_K), lambda i: (i, 0)),
                  pl.BlockSpec((1, 1, 6 * D_MODEL), lambda i: (i // tiles_per_mod, 0, 0)),
                  _const_spec((1, D_MODEL))],
        out_specs=row_spec,
        out_shape=jax.ShapeDtypeStruct((rows, D_MODEL), F32),
        scratch_shapes=[pltpu.VMEM((TOP_K, ROW_TILE // SUBLANES, SUBLANES, D_MODEL), F32),
                        pltpu.SemaphoreType.DMA(())],
        compiler_params=_params("arbitrary", disable_bounds_checks=True),
        name="moe_combine",
    )(dest_flat, yb, x1, gates, mod, final_g.reshape(1, D_MODEL))


def _block_diag_ones(width, block):
    i = jnp.arange(width) // block
    return (i[:, None] == i[None, :]).astype(F32)


def _mixers(x, mod, lp, s0, c0, n0, m0, grid):
    bsz, seq_len, _ = x.shape
    nc = seq_len // CHUNK
    x2 = x.reshape(bsz * seq_len, D_MODEL)
    zr, zm, zg, zgt = _inproj(x2, mod, lp['norm1_g'], lp['w_r'], lp['w_m'], lp['w_g'], lp['w_gt'])
    gt = zgt.reshape(N_GATES, bsz, nc, CHUNK).transpose(1, 2, 0, 3)
    yr, s_new = _rwkv(zr.reshape(bsz, seq_len, RWKV_COLS), s0, lp, grid)
    ym, c_new, n_new, m_new = _mlstm(zm.reshape(bsz, seq_len, MLSTM_MAIN), zg.reshape(bsz, seq_len, N_GATES), gt,
                                     c0, n0, m0.reshape(bsz, 2, H_B, 1), lp)
    x1, hn, idx, gates = _outproj(x2, yr.reshape(-1, DA), ym.reshape(-1, DB), mod, lp['norm2_g'], lp['w_out'],
                                  lp['router_w'], lp['router_b'])
    return x1, hn, idx, gates, (s_new, c_new, n_new, m_new[..., 0])


def kernel(x_prompt, x_sample, state_rwkv, state_mlstm_C, state_mlstm_n, state_mlstm_m, c, c_ctx, ada_w, ada_b, norm1_g, norm2_g, w_in, w_out, rwkv_mu, rwkv_w0, rwkv_w2, rwkv_a0, rwkv_a2, rwkv_g2, rwkv_kk, rwkv_ka, rwkv_rk, rwkv_gn_w, rwkv_gn_b, mlstm_conv, mlstm_bi, mlstm_bf, mlstm_norm_g, router_w, router_b, moe_w1, moe_b1, moe_w2, moe_b2, final_g):
    bp, lp_len, _ = x_prompt.shape
    bs, ls_len, _ = x_sample.shape
    w = w_in[0]
    lp = {
        'norm1_g': norm1_g[0], 'norm2_g': norm2_g[0],
        'w_r': w[:, :RWKV_COLS].astype(BF16),
        'w_m': w[:, RWKV_COLS:RWKV_COLS + MLSTM_MAIN].astype(BF16),
        'w_g': w[:, RWKV_COLS + MLSTM_MAIN:],
        'w_gt': w[:, RWKV_COLS + MLSTM_MAIN:].T,
        'w_out': w_out[0].astype(BF16),
        'mu': rwkv_mu[0].reshape(1, RWKV_COLS), 'kkw': rwkv_kk[0].reshape(1, DA), 'ka': rwkv_ka[0].reshape(1, DA),
        'rk': rwkv_rk[0].reshape(1, DA), 'gnw': rwkv_gn_w[0].reshape(1, DA), 'gnb': rwkv_gn_b[0].reshape(1, DA),
        'w0': rwkv_w0[0], 'w2': rwkv_w2[0], 'a0': rwkv_a0[0], 'a2': rwkv_a2[0], 'g2': rwkv_g2[0],
        'ones_bd': _block_diag_ones(PACK_W, HEAD_A).astype(BF16),
        'conv': mlstm_conv[0], 'bi': mlstm_bi[0].reshape(1, 2 * H_B), 'bit': mlstm_bi[0].reshape(2 * H_B, 1),
        'bf': mlstm_bf[0].reshape(1, 2 * H_B), 'bft': mlstm_bf[0].reshape(2 * H_B, 1),
        'ng': mlstm_norm_g[0].reshape(1, DB),
        'e_rep': jnp.repeat(jnp.eye(N_GATES, dtype=BF16), LANES, axis=1),
        'router_w': router_w[0], 'router_b': router_b[0],
    }
    cc = jnp.concatenate([c_ctx[None, :], c, jnp.zeros((2 * SUBLANES - 1 - bs, D_MODEL), F32)], axis=0)
    mod = _ada_mod(cc, ada_w[0], ada_b[0])
    mod_p = mod[0:1].reshape(1, 1, 6 * D_MODEL)
    mod_s = mod[1:1 + bs].reshape(bs, 1, 6 * D_MODEL)

    zeros = lambda *shape: jnp.zeros(shape, F32)
    x1p, hnp, idxp, gatesp, st = _mixers(x_prompt, mod_p, lp, zeros(bp, 2, H_A, HEAD_A, HEAD_A),
                                         zeros(bp, 2, H_B, HEAD_B, HEAD_B), zeros(bp, 2, H_B, HEAD_B),
                                         zeros(bp, 2, H_B), False)
    x1s, hns, idxs, gatess, _ = _mixers(x_sample, mod_s, lp, state_rwkv[:, 0], state_mlstm_C[:, 0],
                                        state_mlstm_n[:, 0], state_mlstm_m[:, 0], True)

    n_p = bp * lp_len
    idx = jnp.concatenate([idxp, idxs], axis=0)
    n_tok = idx.shape[0]
    rank, counts = _expert_ranks(idx)
    counts = counts[0]
    padded = (counts + MOE_BLOCK - 1) // MOE_BLOCK * MOE_BLOCK
    pad_end = jnp.cumsum(padded)
    pad_start = pad_end - padded
    n_blocks = n_tok * TOP_K // MOE_BLOCK + N_EXPERTS
    block_start = jnp.arange(n_blocks, dtype=jnp.int32) * MOE_BLOCK
    block_e = jnp.minimum(jnp.sum(pad_end[None, :] <= block_start[:, None], axis=1), N_EXPERTS - 1).astype(jnp.int32)
    n_used = (pad_end[-1:] // MOE_BLOCK).astype(jnp.int32)
    dest = (pad_start[idx] + rank).astype(jnp.int32).reshape(-1)
    xs = _dispatch(dest, pad_end.astype(jnp.int32), hnp, hns, n_blocks * MOE_BLOCK)
    yb = _experts(block_e, n_used, xs, moe_w1[0], moe_b1[0], moe_w2[0], moe_b2[0])
    y_prompt = _combine(dest, yb, x1p, gatesp, mod_p, final_g, 0).reshape(x_prompt.shape)
    y_sample = _combine(dest, yb, x1s, gatess, mod_s, final_g, n_p).reshape(x_sample.shape)

    s_new, c_new, n_new, m_new = st
    return (y_prompt, y_sample, s_new[:, None], c_new[:, None], n_new[:, None], m_new[:, None])
```

```python
import functools

import jax
import jax.numpy as jnp
from jax import lax
from jax.experimental import pallas as pl
from jax.experimental.pallas import tpu as pltpu

F32 = jnp.float32
BF16 = jnp.bfloat16
HIGHEST = lax.Precision.HIGHEST

D_MODEL = 1024
DA = 512
HEAD_A = 64
H_A = DA // HEAD_A
DB = 512
H_B = 4
HEAD_B = DB // H_B
DECAY_LORA = 64
AAA_LORA = 64
GATE_LORA = 128
RWKV_COLS = 3 * DA + DECAY_LORA + AAA_LORA + GATE_LORA
MLSTM_MAIN = 4 * DB
N_GATES = 4 * H_B
GRID_W = 64
CHUNK = 64
N_EXPERTS = 32
TOP_K = 4
D_FF = D_MODEL
SWIGLU_LIMIT = 7.0
SWIGLU_ALPHA = 1.702
MOE_BLOCK = 512
NORM_EPS = 1e-6
GN_EPS = 64e-5
ROW_TILE = 256
PROJ_TILE = 512
SUBLANES = 8
LANES = 128
LANE_ROWS = D_MODEL // LANES
VMEM_LIMIT = 56 * 1024 * 1024
PACK = 4
PACK_W = PACK * HEAD_A
N_PACKS = H_A // PACK
SCAN_GROUP = 2
SCAN_DOUBLE_BUFFER_BYTES = 16 * 1024 * 1024

NN = (((1,), (0,)), ((), ()))
NT = (((1,), (1,)), ((), ()))
TN = (((0,), (0,)), ((), ()))


def _mm(a, b, precision=HIGHEST):
    return lax.dot_general(a, b, NN, precision=precision, preferred_element_type=F32)


def _mm_nt(a, b, precision=HIGHEST):
    return lax.dot_general(a, b, NT, precision=precision, preferred_element_type=F32)


def _mm_tn(a, b, precision=HIGHEST):
    return lax.dot_general(a, b, TN, precision=precision, preferred_element_type=F32)


def _dot_bf16(a, b, dims=NN):
    return lax.dot_general(a.astype(BF16), b.astype(BF16), dims, preferred_element_type=F32)


def _sigmoid(x):
    return 1.0 / (1.0 + jnp.exp(-x))


def _log_sigmoid(x):
    return jnp.minimum(x, 0.0) - jnp.log(1.0 + jnp.exp(-jnp.abs(x)))


def _params(*sem, **kw):
    return pltpu.CompilerParams(dimension_semantics=sem, vmem_limit_bytes=VMEM_LIMIT, **kw)


def _split_bf16(x):
    hi = x.astype(BF16)
    return hi, (x - hi.astype(F32)).astype(BF16)


def _const_spec(shape):
    nd = len(shape)
    return pl.BlockSpec(shape, lambda *_: (0,) * nd)


def _tri(n, reverse, strict):
    t = lax.broadcasted_iota(jnp.int32, (n, n), 0)
    s = lax.broadcasted_iota(jnp.int32, (n, n), 1)
    if reverse:
        return (s > t) if strict else (s >= t)
    return (s < t) if strict else (s <= t)


def _seq_neighbours(ref, b, c, nc, cols):
    r0 = pl.multiple_of(c * CHUNK, CHUNK)
    zc = ref[b, pl.ds(r0, CHUNK), cols]
    row = lax.broadcasted_iota(jnp.int32, zc.shape, 0)
    p0 = pl.multiple_of(jnp.maximum(r0 - SUBLANES, 0), SUBLANES)
    n0 = pl.multiple_of(jnp.minimum(r0 + CHUNK, (nc - 1) * CHUNK), SUBLANES)
    before = ref[b, pl.ds(p0, SUBLANES), cols][SUBLANES - 1:SUBLANES]
    after = ref[b, pl.ds(n0, SUBLANES), cols][0:1]
    before = jnp.where(c > 0, before, 0.0)
    after = jnp.where(c < nc - 1, after, 0.0)
    prev = jnp.where(row == 0, before, pltpu.roll(zc, 1, 0))
    nxt = jnp.where(row == CHUNK - 1, after, pltpu.roll(zc, CHUNK - 1, 0))
    return zc, prev, nxt


def _ada_kernel(c_ref, w_ref, b_ref, o_ref):
    cc = c_ref[...]
    o_ref[...] = _dot_bf16(cc * _sigmoid(cc), w_ref[...]) + b_ref[...]


def _ada_mod(cc, ada_w, ada_b):
    rows = cc.shape[0]
    ncol = ada_w.shape[1]
    tn = 1536
    return pl.pallas_call(
        _ada_kernel,
        grid=(ncol // tn,),
        in_specs=[_const_spec((rows, D_MODEL)),
                  pl.BlockSpec((D_MODEL, tn), lambda j: (0, j)),
                  pl.BlockSpec((1, tn), lambda j: (0, j))],
        out_specs=pl.BlockSpec((rows, tn), lambda j: (0, j)),
        out_shape=jax.ShapeDtypeStruct((rows, ncol), F32),
        compiler_params=_params("arbitrary"),
        name="ada_mod",
    )(cc, ada_w, ada_b.reshape(1, ncol))


def _inproj_kernel(x_ref, mod_ref, g_ref, wr_ref, wm_ref, wg_ref, wgt_ref, zr_ref, zm_ref, zg_ref, zgt_ref):
    x = x_ref[...]
    mod = mod_ref[0]
    sh, sc = mod[:, 0:D_MODEL], mod[:, D_MODEL:2 * D_MODEL]
    y = x * lax.rsqrt(jnp.mean(x * x, axis=-1, keepdims=True) + NORM_EPS)
    hn = y * g_ref[...] * (1.0 + sc) + sh
    hb = hn.astype(BF16)
    zr_ref[...] = lax.dot_general(hb, wr_ref[...], NN, preferred_element_type=F32)
    zm_ref[...] = lax.dot_general(hb, wm_ref[...], NN, preferred_element_type=F32)
    zg_ref[...] = _dot_bf16(hb, wg_ref[...])
    zgt_ref[...] = _dot_bf16(wgt_ref[...], hb, NT)


def _inproj(x2, mod, norm_g, w_r, w_m, w_g, w_gt):
    rows = x2.shape[0]
    tiles_per_mod = rows // mod.shape[0] // PROJ_TILE
    return pl.pallas_call(
        _inproj_kernel,
        grid=(rows // PROJ_TILE,),
        in_specs=[pl.BlockSpec((PROJ_TILE, D_MODEL), lambda i: (i, 0)),
                  pl.BlockSpec((1, 1, 6 * D_MODEL), lambda i: (i // tiles_per_mod, 0, 0)),
                  _const_spec((1, D_MODEL)),
                  _const_spec(w_r.shape), _const_spec(w_m.shape), _const_spec(w_g.shape),
                  _const_spec(w_gt.shape)],
        out_specs=[pl.BlockSpec((PROJ_TILE, RWKV_COLS), lambda i: (i, 0)),
                   pl.BlockSpec((PROJ_TILE, MLSTM_MAIN), lambda i: (i, 0)),
                   pl.BlockSpec((PROJ_TILE, N_GATES), lambda i: (i, 0)),
                   pl.BlockSpec((N_GATES, PROJ_TILE), lambda i: (0, i))],
        out_shape=[jax.ShapeDtypeStruct((rows, RWKV_COLS), F32),
                   jax.ShapeDtypeStruct((rows, MLSTM_MAIN), F32),
                   jax.ShapeDtypeStruct((rows, N_GATES), F32),
                   jax.ShapeDtypeStruct((N_GATES, rows), F32)],
        compiler_params=_params("arbitrary"),
        name="inproj",
    )(x2, mod, norm_g.reshape(1, D_MODEL), w_r, w_m, w_g, w_gt)


def _group_sum(x, ones_bd):
    hi = x.astype(BF16)
    lo = (x - hi.astype(F32)).astype(BF16)
    return (lax.dot_general(hi, ones_bd, NN, preferred_element_type=F32)
            + lax.dot_general(lo, ones_bd, NN, preferred_element_type=F32))


def _block_diag(x, ones_bd):
    return jnp.concatenate([x.astype(BF16)] * PACK, axis=0) * ones_bd


def _rwkv_body(zr_ref, s0_ref, mu_ref, kkw_ref, ka_ref, rk_ref, gnw_ref, gnb_ref, w0_ref, w2_ref,
                 a0_ref, a2_ref, g2_ref, ones_ref, y_ref, sout_ref,
                 r_s, k_s, v_s, kk_s, gate_s, lx_s, st_s, *, seq_len, grid, group):
    nc = seq_len // CHUNK
    ones_bd = ones_ref[...]

    def mix_chunk(c, carry):
        r0 = pl.multiple_of(c * CHUNK, CHUNK)
        for b in range(group):
            if grid:
                zc = zr_ref[b, pl.ds(r0, CHUNK), :]
                row = lax.broadcasted_iota(jnp.int32, zc.shape, 0)
                up0 = pl.multiple_of(jnp.maximum(c - 1, 0) * CHUNK, CHUNK)
                dn0 = pl.multiple_of(jnp.minimum(c + 1, nc - 1) * CHUNK, CHUNK)
                up = jnp.where(c > 0, zr_ref[b, pl.ds(up0, CHUNK), :], 0.0)
                down = jnp.where(c < nc - 1, zr_ref[b, pl.ds(dn0, CHUNK), :], 0.0)
                left = jnp.where(row == 0, 0.0, pltpu.roll(zc, 1, 0))
                right = jnp.where(row == CHUNK - 1, 0.0, pltpu.roll(zc, CHUNK - 1, 0))
                local = 0.25 * (up + down + left + right)
            else:
                zc, prev, nxt = _seq_neighbours(zr_ref, b, c, nc, slice(None))
                local = 0.5 * (prev + nxt)
            z = zc + (local - zc) * mu_ref[...]
            r, k, v = z[:, 0:DA], z[:, DA:2 * DA], z[:, 2 * DA:3 * DA]
            lx = z[:, 3 * DA:3 * DA + DECAY_LORA + AAA_LORA]
            xg = z[:, 3 * DA + DECAY_LORA + AAA_LORA:]
            kk = k * kkw_ref[...]
            sq = kk * kk
            ss = jnp.concatenate([_group_sum(sq[:, p * PACK_W:(p + 1) * PACK_W], ones_bd) for p in range(N_PACKS)],
                                 axis=1)
            kk = kk / jnp.maximum(jnp.sqrt(ss), 1e-12)
            lane = lax.broadcasted_iota(jnp.int32, lx.shape, 1)
            r_s[b, pl.ds(r0, CHUNK), :] = r
            k_s[b, pl.ds(r0, CHUNK), :] = k
            v_s[b, pl.ds(r0, CHUNK), :] = v
            kk_s[b, pl.ds(r0, CHUNK), :] = kk
            gate_s[b, pl.ds(r0, CHUNK), :] = _dot_bf16(_sigmoid(xg), g2_ref[...])
            lx_s[b, pl.ds(r0, CHUNK), :] = jnp.where(lane < DECAY_LORA, jnp.tanh(lx), lx)
        return carry

    lax.fori_loop(0, nc, mix_chunk, 0)

    rowblk = lax.broadcasted_iota(jnp.int32, (PACK_W, PACK_W), 0) // HEAD_A
    colblk = lax.broadcasted_iota(jnp.int32, (PACK_W, PACK_W), 1) // HEAD_A
    for b in range(group):
        for d in range(2):
            for p in range(N_PACKS):
                rows_ = jnp.concatenate([s0_ref[b, d, p * PACK + h] for h in range(PACK)], axis=0)
                st_s[b, d, p] = jnp.where(rowblk == colblk, jnp.concatenate([rows_] * PACK, axis=1), 0.0)

    t_idx = lax.broadcasted_iota(jnp.int32, (CHUNK, PACK_W), 0)
    s_idx = lax.broadcasted_iota(jnp.int32, (CHUNK, PACK_W), 1) % CHUNK

    def stages(i):
        first_touch = i < nc // 2
        dot = functools.partial(lax.dot_general, preferred_element_type=F32)
        chains, out_sel = [], []
        for b, d in [(b, d) for b in range(group) for d in range(2)]:
            reverse = d == 1
            c = nc - 1 - i if reverse else i
            rows = pl.ds(pl.multiple_of(c * CHUNK, CHUNK), CHUNK)
            out_sel.append((b, rows))
            strict = (s_idx > t_idx) if reverse else (s_idx < t_idx)
            incl = (s_idx >= t_idx) if reverse else (s_idx <= t_idx)
            last = 0 if reverse else CHUNK - 1
            r, k, v, kk = r_s[b, rows, :], k_s[b, rows, :], v_s[b, rows, :], kk_s[b, rows, :]
            lx = lx_s[b, rows, :]
            wl = w0_ref[d:d + 1, :] + _dot_bf16(lx[:, 0:DECAY_LORA], w2_ref[d])
            logw = -jnp.exp(_log_sigmoid(wl) - 0.5)
            a = _sigmoid(a0_ref[d:d + 1, :] + _dot_bf16(lx[:, DECAY_LORA:], a2_ref[d]))
            kd = k * (1.0 + (a - 1.0) * ka_ref[...])
            kb = kk * a
            tri = _tri(CHUNK, reverse, False).astype(BF16)
            w1, w_rest = _split_bf16(logw)
            w2, w3 = _split_bf16(w_rest)
            cum = dot(tri, w1, NN) + dot(tri, w2, NN) + dot(tri, w3, NN)
            clast = cum[last:last + 1]
            rt = r * jnp.exp(cum)
            at = -kk * jnp.exp(cum - logw)
            einv = jnp.exp(-cum)
            kt, bt = kd * einv, kb * einv
            edec = jnp.exp(clast - cum)
            kp, bp = kd * edec, kb * edec
            ptot = jnp.exp(clast)
            rkd = r * kd * rk_ref[...]
            for p in range(N_PACKS):
                cs = slice(p * PACK_W, (p + 1) * PACK_W)
                chains.append(dict(
                    out=len(out_sel) - 1, st=(b, d, p), cs=cs, strict=strict, incl=incl, v=v[:, cs],
                    rkd=rkd[:, cs], ptot=ptot[:, cs],
                    ar=jnp.concatenate([at[:, cs], rt[:, cs]], axis=0).astype(BF16),
                    kt_bd=_block_diag(kt[:, cs], ones_bd), bt_bd=_block_diag(bt[:, cs], ones_bd),
                    v_bd=_block_diag(v[:, cs], ones_bd),
                    kbp=jnp.concatenate([kp[:, cs], bp[:, cs]], axis=0)))
            yield

        for ch in chains:
            ch['s_prev'] = st_s[ch['st']]
            ch['a_k'] = dot(ch['ar'], ch['kt_bd'], NT)
            ch['a_b'] = dot(ch['ar'], ch['bt_bd'], NT)
            ch['ars'] = dot(ch['ar'], ch['s_prev'].astype(BF16), NT)
        yield
        for ch in chains:
            a_ak = jnp.where(ch['strict'], ch['a_k'][:CHUNK], 0.0)
            a_rk = jnp.where(ch['incl'], ch['a_k'][CHUNK:], 0.0)
            ch['a_rb'] = jnp.where(ch['incl'], ch['a_b'][CHUNK:], 0.0).astype(BF16)
            n = jnp.where(ch['strict'], ch['a_b'][:CHUNK], 0.0)
            ch['n_hi'], n_lo = _split_bf16(n)
            ch['n_both'] = jnp.concatenate([ch['n_hi'], n_lo], axis=0)
            ch['pw'] = ch['n_hi']
            ch['m'] = jnp.where(s_idx == t_idx, 1.0, 0.0) + n
            akv = dot(jnp.concatenate([a_ak, a_rk], axis=0).astype(BF16), ch['v_bd'], NN)
            ch['rhs'] = ch['ars'][:CHUNK] + akv[:CHUNK]
            ch['y'] = ch['ars'][CHUNK:] + akv[CHUNK:]
        yield
        for ch in chains:
            ch['pw'] = dot(ch['pw'], _block_diag(ch['pw'], ones_bd), NN).astype(BF16)
        yield
        for j in range(1, 5):
            for ch in chains:
                pw_bd = _block_diag(ch['pw'], ones_bd)
                if j < 4:
                    both = dot(jnp.concatenate([ch['m'].astype(BF16), ch['pw']], axis=0), pw_bd, NN)
                    ch['m'] = ch['m'] + both[:CHUNK]
                    ch['pw'] = both[CHUNK:].astype(BF16)
                else:
                    ch['m'] = ch['m'] + dot(ch['m'].astype(BF16), pw_bd, NN)
            yield
        for ch in chains:
            ch['mb'] = ch['m'].astype(BF16)
            ch['u'] = dot(ch['mb'], _block_diag(ch['rhs'], ones_bd), NN)
        yield
        for ch in chains:
            u_hi, u_lo = _split_bf16(ch['u'])
            both = dot(ch['n_both'], _block_diag(u_hi, ones_bd), NN)
            nu = both[:CHUNK] + both[CHUNK:] + dot(ch['n_hi'], _block_diag(u_lo, ones_bd), NN)
            ch['resid'] = ch['rhs'] - ch['u'] + nu
        yield
        for ch in chains:
            ch['u'] = ch['u'] + dot(ch['mb'], _block_diag(ch['resid'], ones_bd), NN)
        yield
        for ch in chains:
            u = ch['u']
            ch['y'] = ch['y'] + dot(ch['a_rb'], _block_diag(u, ones_bd), NN)
            vu_hi, vu_lo = _split_bf16(jnp.concatenate([ch['v'], u], axis=0))
            kbp_hi, kbp_lo = _split_bf16(ch['kbp'])
            grown = dot(vu_hi, kbp_hi, TN) + dot(vu_hi, kbp_lo, TN) + dot(vu_lo, kbp_hi, TN)
            st_s[ch['st']] = ch['s_prev'] * ch['ptot'] + grown * ones_bd.astype(F32)
        yield
        n_ch = len(chains)
        sums = dot(jnp.concatenate([ch['y'] for ch in chains] + [ch['rkd'] for ch in chains],
                                   axis=0).astype(BF16), ones_bd, NN)
        ycs = [ch['y'] - sums[q * CHUNK:(q + 1) * CHUNK] * (1.0 / HEAD_A) for q, ch in enumerate(chains)]
        var = dot(jnp.concatenate([yc * yc for yc in ycs], axis=0).astype(BF16), ones_bd, NN) * (1.0 / HEAD_A)
        outs = [[] for _ in out_sel]
        for q, ch in enumerate(chains):
            cs = ch['cs']
            out = ycs[q] * lax.rsqrt(var[q * CHUNK:(q + 1) * CHUNK] + GN_EPS) * gnw_ref[:, cs] + gnb_ref[:, cs]
            outs[ch['out']].append(out + sums[(n_ch + q) * CHUNK:(n_ch + q + 1) * CHUNK] * ch['v'])
        for (b, rows), parts in zip(out_sel, outs):
            out = jnp.concatenate(parts, axis=1)

            @pl.when(first_touch)
            def _(b=b, rows=rows, out=out):
                y_ref[b, rows, :] = out

            @pl.when(jnp.logical_not(first_touch))
            def _(b=b, rows=rows, out=out):
                y_ref[b, rows, :] = (y_ref[b, rows, :] + out) * gate_s[b, rows, :]

    def finish():
        for b in range(group):
            for d in range(2):
                for p in range(N_PACKS):
                    s_fin = st_s[b, d, p]
                    for h in range(PACK):
                        sout_ref[b, d, p * PACK + h] = s_fin[h * HEAD_A:(h + 1) * HEAD_A,
                                                             h * HEAD_A:(h + 1) * HEAD_A]

    return stages, finish


def _mlstm_body(zm_ref, g_ref, gt_ref, c0_ref, n0_ref, m0_ref, conv_ref, bi_ref, bit_ref, bf_ref, bft_ref,
                  ng_ref, erep_ref, y_ref, cout_ref, nout_ref, mout_ref,
                  qk_s, h_s, c_s, n_s, m_s, *, seq_len, group):
    nc = seq_len // CHUNK

    def conv_chunk(c, carry):
        for b in range(group):
            zc, prev, nxt = _seq_neighbours(zm_ref, b, c, nc, slice(0, 2 * DB))
            u = conv_ref[0:1, :] * prev + conv_ref[1:2, :] * zc + conv_ref[2:3, :] * nxt
            qk = u * _sigmoid(u)
            lane = lax.broadcasted_iota(jnp.int32, qk.shape, 1)
            qk_s[b, pl.ds(pl.multiple_of(c * CHUNK, CHUNK), CHUNK), :] = jnp.where(lane >= DB,
                                                                                  qk * (HEAD_B ** -0.5), qk)
        return carry

    lax.fori_loop(0, nc, conv_chunk, 0)

    for b in range(group):
        for d in range(2):
            c_s[b, d] = c0_ref[b, d]
            n_s[b, d] = n0_ref[b, d]
            m_s[b, d] = jnp.broadcast_to(m0_ref[b, d], (H_B, LANES))

    e_rep = erep_ref[...]
    ones_tl = jnp.ones((CHUNK, LANES), BF16)
    row_id = lax.broadcasted_iota(jnp.int32, (CHUNK, LANES), 0)

    def scan_max(x, reverse):
        shift = 1
        while shift < CHUNK:
            if reverse:
                moved = jnp.where(row_id < CHUNK - shift, pltpu.roll(x, CHUNK - shift, 0), -jnp.inf)
            else:
                moved = jnp.where(row_id >= shift, pltpu.roll(x, shift, 0), -jnp.inf)
            x = jnp.maximum(x, moved)
            shift *= 2
        return x

    def dot3(a, b, dims):
        a_hi, a_lo = _split_bf16(a)
        b_hi, b_lo = _split_bf16(b)
        dg = functools.partial(lax.dot_general, dimension_numbers=dims, preferred_element_type=F32)
        return dg(a_hi, b_hi) + dg(a_hi, b_lo) + dg(a_lo, b_hi)

    def stages(i):
        dot = functools.partial(lax.dot_general, preferred_element_type=F32)
        chains = []
        for b, d in [(b, d) for b in range(group) for d in range(2)]:
            reverse = d == 1
            incl = _tri(CHUNK, reverse, False)
            tri_b = incl.astype(BF16)
            tri_row = _tri(CHUNK, not reverse, False).astype(F32)
            last = 0 if reverse else CHUNK - 1
            gsl = slice(d * H_B, (d + 1) * H_B)
            fsl = slice(2 * H_B + d * H_B, 2 * H_B + (d + 1) * H_B)
            c = nc - 1 - i if reverse else i
            rows = pl.ds(pl.multiple_of(c * CHUNK, CHUNK), CHUNK)
            g1, g_rest = _split_bf16(g_ref[b, rows, :])
            g2, g3 = _split_bf16(g_rest)
            g_rep = dot(g1, e_rep, NN) + dot(g2, e_rep, NN) + dot(g3, e_rep, NN)
            grow = gt_ref[b, c]
            i_row = grow[gsl, :] + bit_ref[gsl, :]
            f_row = _log_sigmoid(grow[fsl, :] + bft_ref[gsl, :])
            b_row = _mm(f_row, tri_row)
            for h in range(H_B):
                ji, jf = d * H_B + h, 2 * H_B + d * H_B + h
                ic = g_rep[:, ji * LANES:(ji + 1) * LANES] + bi_ref[:, ji:ji + 1]
                fc = _log_sigmoid(g_rep[:, jf * LANES:(jf + 1) * LANES] + bf_ref[:, ji:ji + 1])
                f1, f_rest = _split_bf16(fc)
                f2, f3 = _split_bf16(f_rest)
                bc = dot(tri_b, f1, NN) + dot(tri_b, f2, NN) + dot(tri_b, f3, NN)
                q = qk_s[b, rows, h * HEAD_B:(h + 1) * HEAD_B]
                chains.append(dict(
                    b=b, d=d, h=h, rows=rows, incl=incl, reverse=reverse, q=q, qb=q.astype(BF16),
                    k=qk_s[b, rows, DB + h * HEAD_B:DB + (h + 1) * HEAD_B],
                    v=zm_ref[b, rows, 2 * DB + h * HEAD_B:2 * DB + (h + 1) * HEAD_B],
                    bc=bc, ic=ic, bl=bc[last:last + 1, :], br=b_row[h:h + 1, :], ir=i_row[h:h + 1, :]))
            yield
        for q, ch in enumerate(chains):
            b, d, h = ch['b'], ch['d'], ch['h']
            ch['qk'] = _dot_bf16(ch['qb'], ch['k'], NT)
            ch['c_prev'] = c_s[b, d, h]
            ch['qc'] = _dot_bf16(ch['qb'], ch['c_prev'])
            ch['n_prev'] = n_s[b, d, h:h + 1, :]
            ch['qn'] = dot3(ch['q'], jnp.broadcast_to(ch['n_prev'], (HEAD_B, HEAD_B)), NT)
            if q % 2 == 1:
                yield
        for q, ch in enumerate(chains):
            b, d, h = ch['b'], ch['d'], ch['h']
            bc, ic, bl = ch['bc'], ch['ic'], ch['bl']
            m_prev = m_s[b, d, h:h + 1, :]
            m_t = bc + jnp.maximum(m_prev, scan_max(ic - bc, ch['reverse']))
            dmat = jnp.where(ch['incl'], bc[:, :CHUNK] - ch['br'] + ch['ir'], -jnp.inf)
            s = ch['qk'] * jnp.exp(dmat - m_t[:, :CHUNK])
            carry_w = jnp.exp(bc + m_prev - m_t)
            s_hi, s_lo = _split_bf16(s)
            den = dot(s_hi, ones_tl, NN) + dot(s_lo, ones_tl, NN) + carry_w * ch['qn']
            num = dot(s_hi, ch['v'].astype(BF16), NN) + carry_w * ch['qc']
            h_s[b, d, ch['rows'], h * HEAD_B:(h + 1) * HEAD_B] = num / jnp.maximum(jnp.abs(den), jnp.exp(-m_t))
            g = bl - bc + ic
            m_new = jnp.maximum(bl + m_prev, jnp.max(g, axis=0, keepdims=True))
            wk = jnp.exp(g - m_new)
            decay = jnp.exp(bl + m_prev - m_new)
            c_s[b, d, h] = decay * ch['c_prev'] + _dot_bf16(ch['k'], wk * ch['v'], TN)
            n_s[b, d, h:h + 1, :] = decay * ch['n_prev'] + jnp.sum(wk * ch['k'], axis=0, keepdims=True)
            m_s[b, d, h:h + 1, :] = m_new
            if q % 2 == 1:
                yield

    def norm_chunk(c, carry):
        rows = pl.ds(pl.multiple_of(c * CHUNK, CHUNK), CHUNK)
        for b in range(group):
            og = zm_ref[b, rows, 3 * DB:4 * DB]
            hm = h_s[b, 0, rows, :] + h_s[b, 1, rows, :]
            parts = []
            for h in range(H_B):
                hh = hm[:, h * HEAD_B:(h + 1) * HEAD_B]
                parts.append(hh * lax.rsqrt(jnp.mean(hh * hh, axis=-1, keepdims=True) + NORM_EPS))
            y_ref[b, rows, :] = jnp.concatenate(parts, axis=1) * ng_ref[...] * _sigmoid(og)
        return carry

    def finish():
        cout_ref[...] = c_s[...]
        nout_ref[...] = n_s[...]
        mout_ref[...] = m_s[...]
        lax.fori_loop(0, nc, norm_chunk, 0)

    return stages, finish


def _scan_kernel(*refs, body, n_steps, **static):
    stages, finish = body(*refs, **static)

    def scan_step(i, carry):
        for _ in stages(i):
            pass
        return carry

    lax.fori_loop(0, n_steps, scan_step, 0)
    finish()


def _scan_specs(grp):
    lead = lambda *rest: pl.BlockSpec((grp,) + rest, lambda b: (b,) + (0,) * len(rest))

    def big(seq_len, cols):
        if 2 * grp * seq_len * cols * 4 <= SCAN_DOUBLE_BUFFER_BYTES:
            return lead(seq_len, cols)
        return pl.BlockSpec((grp, seq_len, cols), lambda b: (b, 0, 0), pipeline_mode=pl.Buffered(1))

    return lead, big


def _rwkv(zr, s0, p, grid):
    bsz, seq_len, _ = zr.shape
    nc = seq_len // CHUNK
    grp = 2 * SCAN_GROUP if 4 * SCAN_GROUP * seq_len * RWKV_COLS * 4 <= SCAN_DOUBLE_BUFFER_BYTES else SCAN_GROUP
    assert nc % 2 == 0 and bsz % grp == 0
    kern = functools.partial(_scan_kernel, body=_rwkv_body, n_steps=nc, seq_len=seq_len, grid=grid, group=grp)
    consts = [p['mu'], p['kkw'], p['ka'], p['rk'], p['gnw'], p['gnb'], p['w0'], p['w2'], p['a0'], p['a2'],
              p['g2'], p['ones_bd']]
    lead, big = _scan_specs(grp)
    return pl.pallas_call(
        kern,
        grid=(bsz // grp,),
        in_specs=[big(seq_len, RWKV_COLS), lead(2, H_A, HEAD_A, HEAD_A)] + [_const_spec(a.shape) for a in consts],
        out_specs=[lead(seq_len, DA), lead(2, H_A, HEAD_A, HEAD_A)],
        out_shape=[jax.ShapeDtypeStruct((bsz, seq_len, DA), F32),
                   jax.ShapeDtypeStruct((bsz, 2, H_A, HEAD_A, HEAD_A), F32)],
        scratch_shapes=[pltpu.VMEM((grp, seq_len, DA), F32)] * 5
                       + [pltpu.VMEM((grp, seq_len, DECAY_LORA + AAA_LORA), F32),
                          pltpu.VMEM((grp, 2, N_PACKS, PACK_W, PACK_W), F32)],
        compiler_params=_params("arbitrary"),
        name="rwkv_scan",
    )(zr, s0, *consts)


def _mlstm(zm, g, gt, c0, n0, m0, p):
    bsz, seq_len, _ = zm.shape
    nc = seq_len // CHUNK
    grp = SCAN_GROUP
    assert bsz % grp == 0
    kern = functools.partial(_scan_kernel, body=_mlstm_body, n_steps=nc, seq_len=seq_len, group=grp)
    consts = [p['conv'], p['bi'], p['bit'], p['bf'], p['bft'], p['ng'], p['e_rep']]
    lead, big = _scan_specs(grp)
    return pl.pallas_call(
        kern,
        grid=(bsz // grp,),
        in_specs=[big(seq_len, MLSTM_MAIN), lead(seq_len, N_GATES), lead(nc, N_GATES, CHUNK),
                  lead(2, H_B, HEAD_B, HEAD_B), lead(2, H_B, HEAD_B), lead(2, H_B, 1)]
                 + [_const_spec(a.shape) for a in consts],
        out_specs=[lead(seq_len, DB), lead(2, H_B, HEAD_B, HEAD_B), lead(2, H_B, HEAD_B), lead(2, H_B, LANES)],
        out_shape=[jax.ShapeDtypeStruct((bsz, seq_len, DB), F32),
                   jax.ShapeDtypeStruct((bsz, 2, H_B, HEAD_B, HEAD_B), F32),
                   jax.ShapeDtypeStruct((bsz, 2, H_B, HEAD_B), F32),
                   jax.ShapeDtypeStruct((bsz, 2, H_B, LANES), F32)],
        scratch_shapes=[pltpu.VMEM((grp, seq_len, 2 * DB), F32), pltpu.VMEM((grp, 2, seq_len, DB), F32),
                        pltpu.VMEM((grp, 2, H_B, HEAD_B, HEAD_B), F32), pltpu.VMEM((grp, 2, H_B, HEAD_B), F32),
                        pltpu.VMEM((grp, 2, H_B, LANES), F32)],
        compiler_params=_params("arbitrary"),
        name="mlstm_scan",
    )(zm, g, gt, c0, n0, m0, *consts)


def _outproj_kernel(x_ref, yr_ref, ym_ref, mod_ref, g_ref, wo_ref, rw_ref, rb_ref,
                    x1_ref, hn_ref, idx_ref, gate_ref):
    mod = mod_ref[0]
    g1 = mod[:, 2 * D_MODEL:3 * D_MODEL]
    sh2, sc2 = mod[:, 3 * D_MODEL:4 * D_MODEL], mod[:, 4 * D_MODEL:5 * D_MODEL]
    mix = _dot_bf16(yr_ref[...], wo_ref[0:DA, :]) + _dot_bf16(ym_ref[...], wo_ref[DA:, :])
    x1 = x_ref[...] + g1 * mix
    x1_ref[...] = x1
    y = x1 * lax.rsqrt(jnp.mean(x1 * x1, axis=-1, keepdims=True) + NORM_EPS)
    hn = y * g_ref[...] * (1.0 + sc2) + sh2
    hn_ref[...] = hn
    logits = _dot_bf16(hn, rw_ref[...]) + rb_ref[...]
    lane = lax.broadcasted_iota(jnp.int32, logits.shape, 1)
    vals, idxs = [], []
    for _ in range(TOP_K):
        top = jnp.max(logits, axis=-1, keepdims=True)
        pick = jnp.min(jnp.where(logits == top, lane, N_EXPERTS), axis=-1, keepdims=True)
        vals.append(top)
        idxs.append(pick)
        logits = jnp.where(lane == pick, -jnp.inf, logits)
    exps = [jnp.exp(v - vals[0]) for v in vals]
    total = exps[0] + exps[1] + exps[2] + exps[3]
    k_lane = lax.broadcasted_iota(jnp.int32, (x1.shape[0], TOP_K), 1)
    gates = jnp.zeros((x1.shape[0], TOP_K), F32)
    picks = jnp.zeros((x1.shape[0], TOP_K), jnp.int32)
    for j in range(TOP_K):
        gates = jnp.where(k_lane == j, exps[j] / total, gates)
        picks = jnp.where(k_lane == j, idxs[j], picks)
    idx_ref[...] = picks
    gate_ref[...] = gates


def _outproj(x2, yr, ym, mod, norm_g, w_out, router_w, router_b):
    rows = x2.shape[0]
    tiles_per_mod = rows // mod.shape[0] // PROJ_TILE
    row_spec = lambda w: pl.BlockSpec((PROJ_TILE, w), lambda i: (i, 0))
    return pl.pallas_call(
        _outproj_kernel,
        grid=(rows // PROJ_TILE,),
        in_specs=[row_spec(D_MODEL), row_spec(DA), row_spec(DB),
                  pl.BlockSpec((1, 1, 6 * D_MODEL), lambda i: (i // tiles_per_mod, 0, 0)),
                  _const_spec((1, D_MODEL)), _const_spec(w_out.shape), _const_spec(router_w.shape),
                  _const_spec((1, N_EXPERTS))],
        out_specs=[row_spec(D_MODEL), row_spec(D_MODEL), row_spec(TOP_K), row_spec(TOP_K)],
        out_shape=[jax.ShapeDtypeStruct((rows, D_MODEL), F32), jax.ShapeDtypeStruct((rows, D_MODEL), F32),
                   jax.ShapeDtypeStruct((rows, TOP_K), jnp.int32), jax.ShapeDtypeStruct((rows, TOP_K), F32)],
        compiler_params=_params("arbitrary"),
        name="outproj_router",
    )(x2, yr, ym, mod, norm_g.reshape(1, D_MODEL), w_out, router_w, router_b.reshape(1, N_EXPERTS))


def _rank_kernel(idx_ref, rank_ref, count_ref, run_s):
    i = pl.program_id(0)

    @pl.when(i == 0)
    def _():
        run_s[...] = jnp.zeros_like(run_s)

    idx = idx_ref[...]
    rows = idx.shape[0]
    lane = lax.broadcasted_iota(jnp.int32, (rows, N_EXPERTS), 1)
    hot = jnp.zeros((rows, N_EXPERTS), F32)
    for j in range(TOP_K):
        hot = hot + jnp.where(lane == idx[:, j:j + 1], 1.0, 0.0)
    before = _dot_bf16(_tri(rows, False, True).astype(F32), hot) + run_s[...]
    k_lane = lax.broadcasted_iota(jnp.int32, (rows, TOP_K), 1)
    rank = jnp.zeros((rows, TOP_K), F32)
    for j in range(TOP_K):
        rj = jnp.sum(jnp.where(lane == idx[:, j:j + 1], before, 0.0), axis=-1, keepdims=True)
        rank = jnp.where(k_lane == j, rj, rank)
    rank_ref[...] = rank.astype(jnp.int32)
    run_s[...] = run_s[...] + jnp.sum(hot, axis=0, keepdims=True)
    count_ref[...] = run_s[...].astype(jnp.int32)


def _expert_ranks(idx):
    rows = idx.shape[0]
    return pl.pallas_call(
        _rank_kernel,
        grid=(rows // ROW_TILE,),
        in_specs=[pl.BlockSpec((ROW_TILE, TOP_K), lambda i: (i, 0))],
        out_specs=[pl.BlockSpec((ROW_TILE, TOP_K), lambda i: (i, 0)), _const_spec((1, N_EXPERTS))],
        out_shape=[jax.ShapeDtypeStruct((rows, TOP_K), jnp.int32),
                   jax.ShapeDtypeStruct((1, N_EXPERTS), jnp.int32)],
        scratch_shapes=[pltpu.VMEM((1, N_EXPERTS), F32)],
        compiler_params=_params("arbitrary"),
        name="expert_ranks",
    )(idx)


def _dispatch_kernel(dest_ref, pend_ref, xa_ref, xb_ref, xs_ref, zero_s, sem, *, n_first_tiles):
    @pl.when(pl.program_id(0) == 0)
    def _():
        zero_s[...] = jnp.zeros_like(zero_s)

        def zero_block(start):
            return pltpu.make_async_copy(zero_s, xs_ref.at[pl.ds(pl.multiple_of(start, MOE_BLOCK), MOE_BLOCK)], sem)

        def has_rows(e):
            return pend_ref[e] > (pend_ref[e - 1] if e else 0)

        used_end = pend_ref[N_EXPERTS - 1]
        n_tail = (xs_ref.shape[0] - used_end) // MOE_BLOCK

        def start_tail(b, carry):
            zero_block(used_end + b * MOE_BLOCK).start()
            return carry

        def wait_tail(b, carry):
            zero_block(0).wait()
            return carry

        for e in range(N_EXPERTS):
            @pl.when(has_rows(e))
            def _(e=e):
                zero_block(pend_ref[e] - MOE_BLOCK).start()
        lax.fori_loop(0, n_tail, start_tail, 0)
        for e in range(N_EXPERTS):
            @pl.when(has_rows(e))
            def _():
                zero_block(0).wait()
        lax.fori_loop(0, n_tail, wait_tail, 0)

    def issue_from(x_ref):
        def issue(g, carry):
            for q in range(SUBLANES):
                for j in range(TOP_K):
                    slot = dest_ref[g * (SUBLANES * TOP_K) + q * TOP_K + j]
                    pltpu.make_async_copy(x_ref.at[g, pl.ds(q, 1)], xs_ref.at[pl.ds(slot, 1)],
                                          sem).start(priority=j % 2)
            return carry
        lax.fori_loop(0, ROW_TILE // SUBLANES, issue, 0)

    pl.when(pl.program_id(0) < n_first_tiles)(lambda: issue_from(xa_ref))
    pl.when(pl.program_id(0) >= n_first_tiles)(lambda: issue_from(xb_ref))
    all_rows = xs_ref.at[pl.ds(0, ROW_TILE * TOP_K)]
    pltpu.make_async_copy(all_rows, all_rows, sem).wait()


def _dispatch(dest_flat, pad_end, xa, xb, n_slots):
    na, nb = xa.shape[0] // ROW_TILE, xb.shape[0] // ROW_TILE
    tiles = lambda x: x.reshape(x.shape[0] // SUBLANES, SUBLANES, D_MODEL)
    block = (ROW_TILE // SUBLANES, SUBLANES, D_MODEL)
    return pl.pallas_call(
        functools.partial(_dispatch_kernel, n_first_tiles=na),
        grid=(na + nb,),
        in_specs=[pl.BlockSpec((ROW_TILE * TOP_K,), lambda i: (i,), memory_space=pltpu.SMEM),
                  pl.BlockSpec(memory_space=pltpu.SMEM),
                  pl.BlockSpec(block, lambda i: (jnp.minimum(i, na - 1), 0, 0)),
                  pl.BlockSpec(block, lambda i: (jnp.maximum(i - na, 0), 0, 0))],
        out_specs=pl.BlockSpec(memory_space=pl.ANY),
        out_shape=jax.ShapeDtypeStruct((n_slots, D_MODEL), xa.dtype),
        scratch_shapes=[pltpu.VMEM((MOE_BLOCK, D_MODEL), F32), pltpu.SemaphoreType.DMA(())],
        compiler_params=_params("arbitrary", disable_bounds_checks=True),
        name="moe_dispatch",
    )(dest_flat, pad_end, tiles(xa), tiles(xb))


def _expert_kernel(be_ref, nb_ref, xs_ref, w1_ref, b1_ref, w2_ref, b2_ref, y_ref, w1_s, w2_s):
    i = pl.program_id(0)
    changed = jnp.logical_or(i == 0, be_ref[i] != be_ref[jnp.maximum(i - 1, 0)])

    @pl.when(jnp.logical_and(changed, i < nb_ref[0]))
    def _():
        w1_s[...] = w1_ref[0].astype(BF16)
        w2_s[...] = w2_ref[0].astype(BF16)

    @pl.when(i < nb_ref[0])
    def _():
        xb = xs_ref[...].astype(BF16)
        hu = lax.dot_general(xb, w1_s[...], NN, preferred_element_type=F32) + b1_ref[0]
        glu = jnp.minimum(hu[:, :D_FF], SWIGLU_LIMIT)
        lin = jnp.clip(hu[:, D_FF:], -SWIGLU_LIMIT, SWIGLU_LIMIT)
        act = glu * _sigmoid(SWIGLU_ALPHA * glu) * (lin + 1.0)
        y_ref[...] = lax.dot_general(act.astype(BF16), w2_s[...], NN, preferred_element_type=F32) + b2_ref[0]

    @pl.when(i >= nb_ref[0])
    def _():
        y_ref[...] = jnp.zeros_like(y_ref)


def _experts(block_e, n_used, xs, w1, b1, w2, b2):
    n_blocks = xs.shape[0] // MOE_BLOCK
    grid_spec = pltpu.PrefetchScalarGridSpec(
        num_scalar_prefetch=2,
        grid=(n_blocks,),
        in_specs=[pl.BlockSpec((MOE_BLOCK, D_MODEL), lambda i, be, nb: (jnp.minimum(i, nb[0] - 1), 0)),
                  pl.BlockSpec((1, D_MODEL, 2 * D_FF), lambda i, be, nb: (be[i], 0, 0)),
                  pl.BlockSpec((1, 1, 2 * D_FF), lambda i, be, nb: (be[i], 0, 0)),
                  pl.BlockSpec((1, D_FF, D_MODEL), lambda i, be, nb: (be[i], 0, 0)),
                  pl.BlockSpec((1, 1, D_MODEL), lambda i, be, nb: (be[i], 0, 0))],
        out_specs=pl.BlockSpec((MOE_BLOCK, D_MODEL), lambda i, be, nb: (i, 0)),
        scratch_shapes=[pltpu.VMEM((D_MODEL, 2 * D_FF), BF16), pltpu.VMEM((D_FF, D_MODEL), BF16)],
    )
    return pl.pallas_call(
        _expert_kernel,
        grid_spec=grid_spec,
        out_shape=jax.ShapeDtypeStruct(xs.shape, F32),
        compiler_params=_params("arbitrary"),
        name="moe_experts",
    )(block_e, n_used, xs, w1, b1.reshape(N_EXPERTS, 1, 2 * D_FF), w2, b2.reshape(N_EXPERTS, 1, D_MODEL))


def _combine_kernel(dest_ref, yb_ref, x1_ref, gate_ref, mod_ref, fg_ref, o_ref, buf, sem):
    def issue(g, carry):
        for q in range(SUBLANES):
            for j in range(TOP_K):
                slot = dest_ref[g * (SUBLANES * TOP_K) + q * TOP_K + j]
                pltpu.make_async_copy(yb_ref.at[pl.ds(slot, 1)], buf.at[g, j, pl.ds(q, 1)], sem).start(priority=j % 2)
        return carry

    lax.fori_loop(0, ROW_TILE // SUBLANES, issue, 0)
    pltpu.make_async_copy(buf, buf, sem).wait()

    g2 = mod_ref[0][:, 5 * D_MODEL:6 * D_MODEL]
    gates = gate_ref[...]
    moe = jnp.zeros((ROW_TILE, D_MODEL), F32)
    for j in range(TOP_K):
        moe = moe + buf[:, j].reshape(ROW_TILE, D_MODEL) * gates[:, j:j + 1]
    x2 = x1_ref[...] + g2 * moe
    y = x2 * lax.rsqrt(jnp.mean(x2 * x2, axis=-1, keepdims=True) + NORM_EPS)
    o_ref[...] = y * fg_ref[...]


def _combine(dest_flat, yb, x1, gates, mod, final_g, row_offset):
    rows = x1.shape[0]
    tiles_per_mod = rows // mod.shape[0] // ROW_TILE
    tile_offset = row_offset // ROW_TILE
    row_spec = pl.BlockSpec((ROW_TILE, D_MODEL), lambda i: (i, 0))
    return pl.pallas_call(
        _combine_kernel,
        grid=(rows // ROW_TILE,),
        in_specs=[pl.BlockSpec((ROW_TILE * TOP_K,), lambda i: (i + tile_offset,), memory_space=pltpu.SMEM),
                  pl.BlockSpec(memory_space=pl.ANY),
                  row_spec,
                  pl.BlockSpec((ROW_TILE, TOP_K), lambda i: (i, 0)),
                  pl.BlockSpec((1, 1, 6 * D_MODEL), lambda i: (i // tiles_per_mod, 0, 0)),
                  _const_spec((1, D_MODEL))],
        out_specs=row_spec,
        out_shape=jax.ShapeDtypeStruct((rows, D_MODEL), F32),
        scratch_shapes=[pltpu.VMEM((ROW_TILE // SUBLANES, TOP_K, SUBLANES, D_MODEL), F32),
                        pltpu.SemaphoreType.DMA(())],
        compiler_params=_params("arbitrary", disable_bounds_checks=True),
        name="moe_combine",
    )(dest_flat, yb, x1, gates, mod, final_g.reshape(1, D_MODEL))


def _block_diag_ones(width, block):
    i = jnp.arange(width) // block
    return (i[:, None] == i[None, :]).astype(F32)


def _mixers(x, mod, lp, s0, c0, n0, m0, grid):
    bsz, seq_len, _ = x.shape
    nc = seq_len // CHUNK
    x2 = x.reshape(bsz * seq_len, D_MODEL)
    zr, zm, zg, zgt = _inproj(x2, mod, lp['norm1_g'], lp['w_r'], lp['w_m'], lp['w_g'], lp['w_gt'])
    gt = zgt.reshape(N_GATES, bsz, nc, CHUNK).transpose(1, 2, 0, 3)
    yr, s_new = _rwkv(zr.reshape(bsz, seq_len, RWKV_COLS), s0, lp, grid)
    ym, c_new, n_new, m_new = _mlstm(zm.reshape(bsz, seq_len, MLSTM_MAIN), zg.reshape(bsz, seq_len, N_GATES), gt,
                                     c0, n0, m0.reshape(bsz, 2, H_B, 1), lp)
    x1, hn, idx, gates = _outproj(x2, yr.reshape(-1, DA), ym.reshape(-1, DB), mod, lp['norm2_g'], lp['w_out'],
                                  lp['router_w'], lp['router_b'])
    return x1, hn, idx, gates, (s_new, c_new, n_new, m_new[..., 0])


def kernel(x_prompt, x_sample, state_rwkv, state_mlstm_C, state_mlstm_n, state_mlstm_m, c, c_ctx, ada_w, ada_b, norm1_g, norm2_g, w_in, w_out, rwkv_mu, rwkv_w0, rwkv_w2, rwkv_a0, rwkv_a2, rwkv_g2, rwkv_kk, rwkv_ka, rwkv_rk, rwkv_gn_w, rwkv_gn_b, mlstm_conv, mlstm_bi, mlstm_bf, mlstm_norm_g, router_w, router_b, moe_w1, moe_b1, moe_w2, moe_b2, final_g):
    bp, lp_len, _ = x_prompt.shape
    bs, ls_len, _ = x_sample.shape
    w = w_in[0]
    lp = {
        'norm1_g': norm1_g[0], 'norm2_g': norm2_g[0],
        'w_r': w[:, :RWKV_COLS].astype(BF16),
        'w_m': w[:, RWKV_COLS:RWKV_COLS + MLSTM_MAIN].astype(BF16),
        'w_g': w[:, RWKV_COLS + MLSTM_MAIN:],
        'w_gt': w[:, RWKV_COLS + MLSTM_MAIN:].T,
        'w_out': w_out[0].astype(BF16),
        'mu': rwkv_mu[0].reshape(1, RWKV_COLS), 'kkw': rwkv_kk[0].reshape(1, DA), 'ka': rwkv_ka[0].reshape(1, DA),
        'rk': rwkv_rk[0].reshape(1, DA), 'gnw': rwkv_gn_w[0].reshape(1, DA), 'gnb': rwkv_gn_b[0].reshape(1, DA),
        'w0': rwkv_w0[0], 'w2': rwkv_w2[0], 'a0': rwkv_a0[0], 'a2': rwkv_a2[0], 'g2': rwkv_g2[0],
        'ones_bd': _block_diag_ones(PACK_W, HEAD_A).astype(BF16),
        'conv': mlstm_conv[0], 'bi': mlstm_bi[0].reshape(1, 2 * H_B), 'bit': mlstm_bi[0].reshape(2 * H_B, 1),
        'bf': mlstm_bf[0].reshape(1, 2 * H_B), 'bft': mlstm_bf[0].reshape(2 * H_B, 1),
        'ng': mlstm_norm_g[0].reshape(1, DB),
        'e_rep': jnp.repeat(jnp.eye(N_GATES, dtype=BF16), LANES, axis=1),
        'router_w': router_w[0], 'router_b': router_b[0],
    }
    cc = jnp.concatenate([c_ctx[None, :], c, jnp.zeros((2 * SUBLANES - 1 - bs, D_MODEL), F32)], axis=0)
    mod = _ada_mod(cc, ada_w[0], ada_b[0])
    mod_p = mod[0:1].reshape(1, 1, 6 * D_MODEL)
    mod_s = mod[1:1 + bs].reshape(bs, 1, 6 * D_MODEL)

    zeros = lambda *shape: jnp.zeros(shape, F32)
    x1p, hnp, idxp, gatesp, st = _mixers(x_prompt, mod_p, lp, zeros(bp, 2, H_A, HEAD_A, HEAD_A),
                                         zeros(bp, 2, H_B, HEAD_B, HEAD_B), zeros(bp, 2, H_B, HEAD_B),
                                         zeros(bp, 2, H_B), False)
    x1s, hns, idxs, gatess, _ = _mixers(x_sample, mod_s, lp, state_rwkv[:, 0], state_mlstm_C[:, 0],
                                        state_mlstm_n[:, 0], state_mlstm_m[:, 0], True)

    n_p = bp * lp_len
    idx = jnp.concatenate([idxp, idxs], axis=0)
    n_tok = idx.shape[0]
    rank, counts = _expert_ranks(idx)
    counts = counts[0]
    padded = (counts + MOE_BLOCK - 1) // MOE_BLOCK * MOE_BLOCK
    pad_end = jnp.cumsum(padded)
    pad_start = pad_end - padded
    n_blocks = n_tok * TOP_K // MOE_BLOCK + N_EXPERTS
    block_start = jnp.arange(n_blocks, dtype=jnp.int32) * MOE_BLOCK
    block_e = jnp.minimum(jnp.sum(pad_end[None, :] <= block_start[:, None], axis=1), N_EXPERTS - 1).astype(jnp.int32)
    n_used = (pad_end[-1:] // MOE_BLOCK).astype(jnp.int32)
    expert_ids = jnp.arange(N_EXPERTS, dtype=jnp.int32)
    start_of = jnp.sum(jnp.where(idx[:, :, None] == expert_ids, pad_start.astype(jnp.int32), 0), axis=-1)
    dest = (start_of + rank).astype(jnp.int32).reshape(-1)
    xs = _dispatch(dest, pad_end.astype(jnp.int32), hnp, hns, n_blocks * MOE_BLOCK)
    yb = _experts(block_e, n_used, xs, moe_w1[0], moe_b1[0], moe_w2[0], moe_b2[0])
    y_prompt = _combine(dest, yb, x1p, gatesp, mod_p, final_g, 0).reshape(x_prompt.shape)
    y_sample = _combine(dest, yb, x1s, gatess, mod_s, final_g, n_p).reshape(x_sample.shape)

    s_new, c_new, n_new, m_new = st
    return (y_prompt, y_sample, s_new[:, None], c_new[:, None], n_new[:, None], m_new[:, None])
```

```python
import functools

import jax
import jax.numpy as jnp
from jax import lax
from jax.experimental import pallas as pl
from jax.experimental.pallas import tpu as pltpu

F32 = jnp.float32
BF16 = jnp.bfloat16
HIGHEST = lax.Precision.HIGHEST

D_MODEL = 1024
DA = 512
HEAD_A = 64
H_A = DA // HEAD_A
DB = 512
H_B = 4
HEAD_B = DB // H_B
DECAY_LORA = 64
AAA_LORA = 64
GATE_LORA = 128
RWKV_COLS = 3 * DA + DECAY_LORA + AAA_LORA + GATE_LORA
MLSTM_MAIN = 4 * DB
N_GATES = 4 * H_B
GRID_W = 64
CHUNK = 64
N_EXPERTS = 32
TOP_K = 4
D_FF = D_MODEL
SWIGLU_LIMIT = 7.0
SWIGLU_ALPHA = 1.702
MOE_BLOCK = 512
NORM_EPS = 1e-6
GN_EPS = 64e-5
ROW_TILE = 512
PROJ_TILE = 512
SUBLANES = 8
LANES = 128
LANE_ROWS = D_MODEL // LANES
VMEM_LIMIT = 56 * 1024 * 1024
PACK = 4
PACK_W = PACK * HEAD_A
N_PACKS = H_A // PACK
SCAN_GROUP = 2
SCAN_DOUBLE_BUFFER_BYTES = 16 * 1024 * 1024

NN = (((1,), (0,)), ((), ()))
NT = (((1,), (1,)), ((), ()))
TN = (((0,), (0,)), ((), ()))


def _mm(a, b, precision=HIGHEST):
    return lax.dot_general(a, b, NN, precision=precision, preferred_element_type=F32)


def _mm_nt(a, b, precision=HIGHEST):
    return lax.dot_general(a, b, NT, precision=precision, preferred_element_type=F32)


def _mm_tn(a, b, precision=HIGHEST):
    return lax.dot_general(a, b, TN, precision=precision, preferred_element_type=F32)


def _dot_bf16(a, b, dims=NN):
    return lax.dot_general(a.astype(BF16), b.astype(BF16), dims, preferred_element_type=F32)


def _sigmoid(x):
    return 1.0 / (1.0 + jnp.exp(-x))


def _log_sigmoid(x):
    return jnp.minimum(x, 0.0) - jnp.log(1.0 + jnp.exp(-jnp.abs(x)))


def _params(*sem, **kw):
    return pltpu.CompilerParams(dimension_semantics=sem, vmem_limit_bytes=VMEM_LIMIT, **kw)


def _split_bf16(x):
    hi = x.astype(BF16)
    return hi, (x - hi.astype(F32)).astype(BF16)


def _const_spec(shape):
    nd = len(shape)
    return pl.BlockSpec(shape, lambda *_: (0,) * nd)


def _tri(n, reverse, strict):
    t = lax.broadcasted_iota(jnp.int32, (n, n), 0)
    s = lax.broadcasted_iota(jnp.int32, (n, n), 1)
    if reverse:
        return (s > t) if strict else (s >= t)
    return (s < t) if strict else (s <= t)


def _seq_neighbours(ref, b, c, nc, cols):
    r0 = pl.multiple_of(c * CHUNK, CHUNK)
    zc = ref[b, pl.ds(r0, CHUNK), cols]
    row = lax.broadcasted_iota(jnp.int32, zc.shape, 0)
    p0 = pl.multiple_of(jnp.maximum(r0 - SUBLANES, 0), SUBLANES)
    n0 = pl.multiple_of(jnp.minimum(r0 + CHUNK, (nc - 1) * CHUNK), SUBLANES)
    before = ref[b, pl.ds(p0, SUBLANES), cols][SUBLANES - 1:SUBLANES]
    after = ref[b, pl.ds(n0, SUBLANES), cols][0:1]
    before = jnp.where(c > 0, before, 0.0)
    after = jnp.where(c < nc - 1, after, 0.0)
    prev = jnp.where(row == 0, before, pltpu.roll(zc, 1, 0))
    nxt = jnp.where(row == CHUNK - 1, after, pltpu.roll(zc, CHUNK - 1, 0))
    return zc, prev, nxt


def _ada_kernel(c_ref, w_ref, b_ref, o_ref):
    cc = c_ref[...]
    o_ref[...] = _dot_bf16(cc * _sigmoid(cc), w_ref[...]) + b_ref[...]


def _ada_mod(cc, ada_w, ada_b):
    rows = cc.shape[0]
    ncol = ada_w.shape[1]
    tn = 1536
    return pl.pallas_call(
        _ada_kernel,
        grid=(ncol // tn,),
        in_specs=[_const_spec((rows, D_MODEL)),
                  pl.BlockSpec((D_MODEL, tn), lambda j: (0, j)),
                  pl.BlockSpec((1, tn), lambda j: (0, j))],
        out_specs=pl.BlockSpec((rows, tn), lambda j: (0, j)),
        out_shape=jax.ShapeDtypeStruct((rows, ncol), F32),
        compiler_params=_params("arbitrary"),
        name="ada_mod",
    )(cc, ada_w, ada_b.reshape(1, ncol))


def _inproj_kernel(x_ref, mod_ref, g_ref, wr_ref, wm_ref, wg_ref, wgt_ref, zr_ref, zm_ref, zg_ref, zgt_ref):
    x = x_ref[...]
    mod = mod_ref[0]
    sh, sc = mod[:, 0:D_MODEL], mod[:, D_MODEL:2 * D_MODEL]
    y = x * lax.rsqrt(jnp.mean(x * x, axis=-1, keepdims=True) + NORM_EPS)
    hn = y * g_ref[...] * (1.0 + sc) + sh
    hb = hn.astype(BF16)
    zr_ref[...] = lax.dot_general(hb, wr_ref[...], NN, preferred_element_type=F32)
    zm_ref[...] = lax.dot_general(hb, wm_ref[...], NN, preferred_element_type=F32)
    zg_ref[...] = _dot_bf16(hb, wg_ref[...])
    zgt_ref[...] = _dot_bf16(wgt_ref[...], hb, NT)


def _inproj(x2, mod, norm_g, w_r, w_m, w_g, w_gt):
    rows = x2.shape[0]
    tiles_per_mod = rows // mod.shape[0] // PROJ_TILE
    return pl.pallas_call(
        _inproj_kernel,
        grid=(rows // PROJ_TILE,),
        in_specs=[pl.BlockSpec((PROJ_TILE, D_MODEL), lambda i: (i, 0)),
                  pl.BlockSpec((1, 1, 6 * D_MODEL), lambda i: (i // tiles_per_mod, 0, 0)),
                  _const_spec((1, D_MODEL)),
                  _const_spec(w_r.shape), _const_spec(w_m.shape), _const_spec(w_g.shape),
                  _const_spec(w_gt.shape)],
        out_specs=[pl.BlockSpec((PROJ_TILE, RWKV_COLS), lambda i: (i, 0)),
                   pl.BlockSpec((PROJ_TILE, MLSTM_MAIN), lambda i: (i, 0)),
                   pl.BlockSpec((PROJ_TILE, N_GATES), lambda i: (i, 0)),
                   pl.BlockSpec((N_GATES, PROJ_TILE), lambda i: (0, i))],
        out_shape=[jax.ShapeDtypeStruct((rows, RWKV_COLS), F32),
                   jax.ShapeDtypeStruct((rows, MLSTM_MAIN), F32),
                   jax.ShapeDtypeStruct((rows, N_GATES), F32),
                   jax.ShapeDtypeStruct((N_GATES, rows), F32)],
        compiler_params=_params("arbitrary"),
        name="inproj",
    )(x2, mod, norm_g.reshape(1, D_MODEL), w_r, w_m, w_g, w_gt)


def _group_sum(x, ones_bd):
    hi = x.astype(BF16)
    lo = (x - hi.astype(F32)).astype(BF16)
    return (lax.dot_general(hi, ones_bd, NN, preferred_element_type=F32)
            + lax.dot_general(lo, ones_bd, NN, preferred_element_type=F32))


def _block_diag(x, ones_bd):
    return jnp.concatenate([x.astype(BF16)] * PACK, axis=0) * ones_bd


def _rwkv_body(zr_ref, s0_ref, mu_ref, kkw_ref, ka_ref, rk_ref, gnw_ref, gnb_ref, w0_ref, w2_ref,
                 a0_ref, a2_ref, g2_ref, ones_ref, y_ref, sout_ref,
                 r_s, k_s, v_s, kk_s, gate_s, lx_s, st_s, *, seq_len, grid, group):
    nc = seq_len // CHUNK
    ones_bd = ones_ref[...]

    def mix_chunk(c, carry):
        r0 = pl.multiple_of(c * CHUNK, CHUNK)
        for b in range(group):
            if grid:
                zc = zr_ref[b, pl.ds(r0, CHUNK), :]
                row = lax.broadcasted_iota(jnp.int32, zc.shape, 0)
                up0 = pl.multiple_of(jnp.maximum(c - 1, 0) * CHUNK, CHUNK)
                dn0 = pl.multiple_of(jnp.minimum(c + 1, nc - 1) * CHUNK, CHUNK)
                up = jnp.where(c > 0, zr_ref[b, pl.ds(up0, CHUNK), :], 0.0)
                down = jnp.where(c < nc - 1, zr_ref[b, pl.ds(dn0, CHUNK), :], 0.0)
                left = jnp.where(row == 0, 0.0, pltpu.roll(zc, 1, 0))
                right = jnp.where(row == CHUNK - 1, 0.0, pltpu.roll(zc, CHUNK - 1, 0))
                local = 0.25 * (up + down + left + right)
            else:
                zc, prev, nxt = _seq_neighbours(zr_ref, b, c, nc, slice(None))
                local = 0.5 * (prev + nxt)
            z = zc + (local - zc) * mu_ref[...]
            r, k, v = z[:, 0:DA], z[:, DA:2 * DA], z[:, 2 * DA:3 * DA]
            lx = z[:, 3 * DA:3 * DA + DECAY_LORA + AAA_LORA]
            xg = z[:, 3 * DA + DECAY_LORA + AAA_LORA:]
            kk = k * kkw_ref[...]
            sq = kk * kk
            ss = jnp.concatenate([_group_sum(sq[:, p * PACK_W:(p + 1) * PACK_W], ones_bd) for p in range(N_PACKS)],
                                 axis=1)
            kk = kk / jnp.maximum(jnp.sqrt(ss), 1e-12)
            lane = lax.broadcasted_iota(jnp.int32, lx.shape, 1)
            r_s[b, pl.ds(r0, CHUNK), :] = r
            k_s[b, pl.ds(r0, CHUNK), :] = k
            v_s[b, pl.ds(r0, CHUNK), :] = v
            kk_s[b, pl.ds(r0, CHUNK), :] = kk
            gate_s[b, pl.ds(r0, CHUNK), :] = _dot_bf16(_sigmoid(xg), g2_ref[...])
            lx_s[b, pl.ds(r0, CHUNK), :] = jnp.where(lane < DECAY_LORA, jnp.tanh(lx), lx)
        return carry

    lax.fori_loop(0, nc, mix_chunk, 0)

    rowblk = lax.broadcasted_iota(jnp.int32, (PACK_W, PACK_W), 0) // HEAD_A
    colblk = lax.broadcasted_iota(jnp.int32, (PACK_W, PACK_W), 1) // HEAD_A
    for b in range(group):
        for d in range(2):
            for p in range(N_PACKS):
                rows_ = jnp.concatenate([s0_ref[b, d, p * PACK + h] for h in range(PACK)], axis=0)
                st_s[b, d, p] = jnp.where(rowblk == colblk, jnp.concatenate([rows_] * PACK, axis=1), 0.0)

    t_idx = lax.broadcasted_iota(jnp.int32, (CHUNK, PACK_W), 0)
    s_idx = lax.broadcasted_iota(jnp.int32, (CHUNK, PACK_W), 1) % CHUNK

    def stages(i):
        first_touch = i < nc // 2
        dot = functools.partial(lax.dot_general, preferred_element_type=F32)
        chains, out_sel = [], []
        for b, d in [(b, d) for b in range(group) for d in range(2)]:
            reverse = d == 1
            c = nc - 1 - i if reverse else i
            rows = pl.ds(pl.multiple_of(c * CHUNK, CHUNK), CHUNK)
            out_sel.append((b, rows))
            strict = (s_idx > t_idx) if reverse else (s_idx < t_idx)
            incl = (s_idx >= t_idx) if reverse else (s_idx <= t_idx)
            last = 0 if reverse else CHUNK - 1
            r, k, v, kk = r_s[b, rows, :], k_s[b, rows, :], v_s[b, rows, :], kk_s[b, rows, :]
            lx = lx_s[b, rows, :]
            wl = w0_ref[d:d + 1, :] + _dot_bf16(lx[:, 0:DECAY_LORA], w2_ref[d])
            logw = -jnp.exp(_log_sigmoid(wl) - 0.5)
            a = _sigmoid(a0_ref[d:d + 1, :] + _dot_bf16(lx[:, DECAY_LORA:], a2_ref[d]))
            kd = k * (1.0 + (a - 1.0) * ka_ref[...])
            kb = kk * a
            tri = _tri(CHUNK, reverse, False).astype(BF16)
            w1, w_rest = _split_bf16(logw)
            w2, w3 = _split_bf16(w_rest)
            cum = dot(tri, w1, NN) + dot(tri, w2, NN) + dot(tri, w3, NN)
            clast = cum[last:last + 1]
            rt = r * jnp.exp(cum)
            at = -kk * jnp.exp(cum - logw)
            einv = jnp.exp(-cum)
            kt, bt = kd * einv, kb * einv
            edec = jnp.exp(clast - cum)
            kp, bp = kd * edec, kb * edec
            ptot = jnp.exp(clast)
            rkd = r * kd * rk_ref[...]
            for p in range(N_PACKS):
                cs = slice(p * PACK_W, (p + 1) * PACK_W)
                chains.append(dict(
                    out=len(out_sel) - 1, st=(b, d, p), cs=cs, strict=strict, incl=incl, v=v[:, cs],
                    rkd=rkd[:, cs], ptot=ptot[:, cs],
                    ar=jnp.concatenate([at[:, cs], rt[:, cs]], axis=0).astype(BF16),
                    kt_bd=_block_diag(kt[:, cs], ones_bd), bt_bd=_block_diag(bt[:, cs], ones_bd),
                    v_bd=_block_diag(v[:, cs], ones_bd),
                    kbp=jnp.concatenate([kp[:, cs], bp[:, cs]], axis=0)))
            yield

        for ch in chains:
            ch['s_prev'] = st_s[ch['st']]
            ch['a_k'] = dot(ch['ar'], ch['kt_bd'], NT)
            ch['a_b'] = dot(ch['ar'], ch['bt_bd'], NT)
            ch['ars'] = dot(ch['ar'], ch['s_prev'].astype(BF16), NT)
        yield
        for ch in chains:
            a_ak = jnp.where(ch['strict'], ch['a_k'][:CHUNK], 0.0)
            a_rk = jnp.where(ch['incl'], ch['a_k'][CHUNK:], 0.0)
            ch['a_rb'] = jnp.where(ch['incl'], ch['a_b'][CHUNK:], 0.0).astype(BF16)
            n = jnp.where(ch['strict'], ch['a_b'][:CHUNK], 0.0)
            ch['n_hi'], n_lo = _split_bf16(n)
            ch['n_both'] = jnp.concatenate([ch['n_hi'], n_lo], axis=0)
            ch['pw'] = ch['n_hi']
            ch['m'] = jnp.where(s_idx == t_idx, 1.0, 0.0) + n
            akv = dot(jnp.concatenate([a_ak, a_rk], axis=0).astype(BF16), ch['v_bd'], NN)
            ch['rhs'] = ch['ars'][:CHUNK] + akv[:CHUNK]
            ch['y'] = ch['ars'][CHUNK:] + akv[CHUNK:]
        yield
        for ch in chains:
            ch['pw'] = dot(ch['pw'], _block_diag(ch['pw'], ones_bd), NN).astype(BF16)
        yield
        for j in range(1, 5):
            for ch in chains:
                pw_bd = _block_diag(ch['pw'], ones_bd)
                if j < 4:
                    both = dot(jnp.concatenate([ch['m'].astype(BF16), ch['pw']], axis=0), pw_bd, NN)
                    ch['m'] = ch['m'] + both[:CHUNK]
                    ch['pw'] = both[CHUNK:].astype(BF16)
                else:
                    ch['m'] = ch['m'] + dot(ch['m'].astype(BF16), pw_bd, NN)
            yield
        for ch in chains:
            ch['mb'] = ch['m'].astype(BF16)
            ch['u'] = dot(ch['mb'], _block_diag(ch['rhs'], ones_bd), NN)
        yield
        for ch in chains:
            u_hi, u_lo = _split_bf16(ch['u'])
            both = dot(ch['n_both'], _block_diag(u_hi, ones_bd), NN)
            nu = both[:CHUNK] + both[CHUNK:] + dot(ch['n_hi'], _block_diag(u_lo, ones_bd), NN)
            ch['resid'] = ch['rhs'] - ch['u'] + nu
        yield
        for ch in chains:
            ch['u'] = ch['u'] + dot(ch['mb'], _block_diag(ch['resid'], ones_bd), NN)
        yield
        for ch in chains:
            u = ch['u']
            ch['y'] = ch['y'] + dot(ch['a_rb'], _block_diag(u, ones_bd), NN)
            vu_hi, vu_lo = _split_bf16(jnp.concatenate([ch['v'], u], axis=0))
            kbp_hi, kbp_lo = _split_bf16(ch['kbp'])
            grown = dot(vu_hi, kbp_hi, TN) + dot(vu_hi, kbp_lo, TN) + dot(vu_lo, kbp_hi, TN)
            st_s[ch['st']] = ch['s_prev'] * ch['ptot'] + grown * ones_bd.astype(F32)
        yield
        n_ch = len(chains)
        sums = dot(jnp.concatenate([ch['y'] for ch in chains] + [ch['rkd'] for ch in chains],
                                   axis=0).astype(BF16), ones_bd, NN)
        ycs = [ch['y'] - sums[q * CHUNK:(q + 1) * CHUNK] * (1.0 / HEAD_A) for q, ch in enumerate(chains)]
        var = dot(jnp.concatenate([yc * yc for yc in ycs], axis=0).astype(BF16), ones_bd, NN) * (1.0 / HEAD_A)
        outs = [[] for _ in out_sel]
        for q, ch in enumerate(chains):
            cs = ch['cs']
            out = ycs[q] * lax.rsqrt(var[q * CHUNK:(q + 1) * CHUNK] + GN_EPS) * gnw_ref[:, cs] + gnb_ref[:, cs]
            outs[ch['out']].append(out + sums[(n_ch + q) * CHUNK:(n_ch + q + 1) * CHUNK] * ch['v'])
        for (b, rows), parts in zip(out_sel, outs):
            out = jnp.concatenate(parts, axis=1)

            @pl.when(first_touch)
            def _(b=b, rows=rows, out=out):
                y_ref[b, rows, :] = out

            @pl.when(jnp.logical_not(first_touch))
            def _(b=b, rows=rows, out=out):
                y_ref[b, rows, :] = (y_ref[b, rows, :] + out) * gate_s[b, rows, :]

    def finish():
        for b in range(group):
            for d in range(2):
                for p in range(N_PACKS):
                    s_fin = st_s[b, d, p]
                    for h in range(PACK):
                        sout_ref[b, d, p * PACK + h] = s_fin[h * HEAD_A:(h + 1) * HEAD_A,
                                                             h * HEAD_A:(h + 1) * HEAD_A]

    return stages, finish


def _mlstm_body(zm_ref, g_ref, gt_ref, c0_ref, n0_ref, m0_ref, conv_ref, bi_ref, bit_ref, bf_ref, bft_ref,
                  ng_ref, erep_ref, y_ref, cout_ref, nout_ref, mout_ref,
                  qk_s, h_s, c_s, n_s, m_s, *, seq_len, group):
    nc = seq_len // CHUNK

    def conv_chunk(c, carry):
        for b in range(group):
            zc, prev, nxt = _seq_neighbours(zm_ref, b, c, nc, slice(0, 2 * DB))
            u = conv_ref[0:1, :] * prev + conv_ref[1:2, :] * zc + conv_ref[2:3, :] * nxt
            qk = u * _sigmoid(u)
            lane = lax.broadcasted_iota(jnp.int32, qk.shape, 1)
            qk_s[b, pl.ds(pl.multiple_of(c * CHUNK, CHUNK), CHUNK), :] = jnp.where(lane >= DB,
                                                                                  qk * (HEAD_B ** -0.5), qk)
        return carry

    lax.fori_loop(0, nc, conv_chunk, 0)

    for b in range(group):
        for d in range(2):
            c_s[b, d] = c0_ref[b, d]
            n_s[b, d] = n0_ref[b, d]
            m_s[b, d] = jnp.broadcast_to(m0_ref[b, d], (H_B, LANES))

    e_rep = erep_ref[...]
    ones_tl = jnp.ones((CHUNK, LANES), BF16)
    row_id = lax.broadcasted_iota(jnp.int32, (CHUNK, LANES), 0)

    def scan_max(x, reverse):
        shift = 1
        while shift < CHUNK:
            if reverse:
                moved = jnp.where(row_id < CHUNK - shift, pltpu.roll(x, CHUNK - shift, 0), -jnp.inf)
            else:
                moved = jnp.where(row_id >= shift, pltpu.roll(x, shift, 0), -jnp.inf)
            x = jnp.maximum(x, moved)
            shift *= 2
        return x

    def dot3(a, b, dims):
        a_hi, a_lo = _split_bf16(a)
        b_hi, b_lo = _split_bf16(b)
        dg = functools.partial(lax.dot_general, dimension_numbers=dims, preferred_element_type=F32)
        return dg(a_hi, b_hi) + dg(a_hi, b_lo) + dg(a_lo, b_hi)

    def stages(i):
        dot = functools.partial(lax.dot_general, preferred_element_type=F32)
        chains = []
        for b, d in [(b, d) for b in range(group) for d in range(2)]:
            reverse = d == 1
            incl = _tri(CHUNK, reverse, False)
            tri_b = incl.astype(BF16)
            tri_row = _tri(CHUNK, not reverse, False).astype(F32)
            last = 0 if reverse else CHUNK - 1
            gsl = slice(d * H_B, (d + 1) * H_B)
            fsl = slice(2 * H_B + d * H_B, 2 * H_B + (d + 1) * H_B)
            c = nc - 1 - i if reverse else i
            rows = pl.ds(pl.multiple_of(c * CHUNK, CHUNK), CHUNK)
            g1, g_rest = _split_bf16(g_ref[b, rows, :])
            g2, g3 = _split_bf16(g_rest)
            g_rep = dot(g1, e_rep, NN) + dot(g2, e_rep, NN) + dot(g3, e_rep, NN)
            grow = gt_ref[b, c]
            i_row = grow[gsl, :] + bit_ref[gsl, :]
            f_row = _log_sigmoid(grow[fsl, :] + bft_ref[gsl, :])
            b_row = _mm(f_row, tri_row)
            for h in range(H_B):
                ji, jf = d * H_B + h, 2 * H_B + d * H_B + h
                ic = g_rep[:, ji * LANES:(ji + 1) * LANES] + bi_ref[:, ji:ji + 1]
                fc = _log_sigmoid(g_rep[:, jf * LANES:(jf + 1) * LANES] + bf_ref[:, ji:ji + 1])
                f1, f_rest = _split_bf16(fc)
                f2, f3 = _split_bf16(f_rest)
                bc = dot(tri_b, f1, NN) + dot(tri_b, f2, NN) + dot(tri_b, f3, NN)
                q = qk_s[b, rows, h * HEAD_B:(h + 1) * HEAD_B]
                chains.append(dict(
                    b=b, d=d, h=h, rows=rows, incl=incl, reverse=reverse, q=q, qb=q.astype(BF16),
                    k=qk_s[b, rows, DB + h * HEAD_B:DB + (h + 1) * HEAD_B],
                    v=zm_ref[b, rows, 2 * DB + h * HEAD_B:2 * DB + (h + 1) * HEAD_B],
                    bc=bc, ic=ic, bl=bc[last:last + 1, :], br=b_row[h:h + 1, :], ir=i_row[h:h + 1, :]))
            yield
        for q, ch in enumerate(chains):
            b, d, h = ch['b'], ch['d'], ch['h']
            ch['qk'] = _dot_bf16(ch['qb'], ch['k'], NT)
            ch['c_prev'] = c_s[b, d, h]
            ch['qc'] = _dot_bf16(ch['qb'], ch['c_prev'])
            ch['n_prev'] = n_s[b, d, h:h + 1, :]
            ch['qn'] = dot3(ch['q'], jnp.broadcast_to(ch['n_prev'], (HEAD_B, HEAD_B)), NT)
            if q % 2 == 1:
                yield
        for q, ch in enumerate(chains):
            b, d, h = ch['b'], ch['d'], ch['h']
            bc, ic, bl = ch['bc'], ch['ic'], ch['bl']
            m_prev = m_s[b, d, h:h + 1, :]
            m_t = bc + jnp.maximum(m_prev, scan_max(ic - bc, ch['reverse']))
            dmat = jnp.where(ch['incl'], bc[:, :CHUNK] - ch['br'] + ch['ir'], -jnp.inf)
            s = ch['qk'] * jnp.exp(dmat - m_t[:, :CHUNK])
            carry_w = jnp.exp(bc + m_prev - m_t)
            s_hi, s_lo = _split_bf16(s)
            den = dot(s_hi, ones_tl, NN) + dot(s_lo, ones_tl, NN) + carry_w * ch['qn']
            num = dot(s_hi, ch['v'].astype(BF16), NN) + carry_w * ch['qc']
            h_s[b, d, ch['rows'], h * HEAD_B:(h + 1) * HEAD_B] = num / jnp.maximum(jnp.abs(den), jnp.exp(-m_t))
            g = bl - bc + ic
            m_new = jnp.maximum(bl + m_prev, jnp.max(g, axis=0, keepdims=True))
            wk = jnp.exp(g - m_new)
            decay = jnp.exp(bl + m_prev - m_new)
            c_s[b, d, h] = decay * ch['c_prev'] + _dot_bf16(ch['k'], wk * ch['v'], TN)
            n_s[b, d, h:h + 1, :] = decay * ch['n_prev'] + jnp.sum(wk * ch['k'], axis=0, keepdims=True)
            m_s[b, d, h:h + 1, :] = m_new
            if q % 2 == 1:
                yield

    def norm_chunk(c, carry):
        rows = pl.ds(pl.multiple_of(c * CHUNK, CHUNK), CHUNK)
        for b in range(group):
            og = zm_ref[b, rows, 3 * DB:4 * DB]
            hm = h_s[b, 0, rows, :] + h_s[b, 1, rows, :]
            parts = []
            for h in range(H_B):
                hh = hm[:, h * HEAD_B:(h + 1) * HEAD_B]
                parts.append(hh * lax.rsqrt(jnp.mean(hh * hh, axis=-1, keepdims=True) + NORM_EPS))
            y_ref[b, rows, :] = jnp.concatenate(parts, axis=1) * ng_ref[...] * _sigmoid(og)
        return carry

    def finish():
        cout_ref[...] = c_s[...]
        nout_ref[...] = n_s[...]
        mout_ref[...] = m_s[...]
        lax.fori_loop(0, nc, norm_chunk, 0)

    return stages, finish


def _scan_kernel(*refs, body, n_steps, **static):
    stages, finish = body(*refs, **static)

    def scan_step(i, carry):
        for _ in stages(i):
            pass
        return carry

    lax.fori_loop(0, n_steps, scan_step, 0)
    finish()


def _scan_specs(grp):
    lead = lambda *rest: pl.BlockSpec((grp,) + rest, lambda b: (b,) + (0,) * len(rest))

    def big(seq_len, cols):
        if 2 * grp * seq_len * cols * 4 <= SCAN_DOUBLE_BUFFER_BYTES:
            return lead(seq_len, cols)
        return pl.BlockSpec((grp, seq_len, cols), lambda b: (b, 0, 0), pipeline_mode=pl.Buffered(1))

    return lead, big


def _rwkv(zr, s0, p, grid):
    bsz, seq_len, _ = zr.shape
    nc = seq_len // CHUNK
    grp = 2 * SCAN_GROUP if 4 * SCAN_GROUP * seq_len * RWKV_COLS * 4 <= SCAN_DOUBLE_BUFFER_BYTES else SCAN_GROUP
    assert nc % 2 == 0 and bsz % grp == 0
    kern = functools.partial(_scan_kernel, body=_rwkv_body, n_steps=nc, seq_len=seq_len, grid=grid, group=grp)
    consts = [p['mu'], p['kkw'], p['ka'], p['rk'], p['gnw'], p['gnb'], p['w0'], p['w2'], p['a0'], p['a2'],
              p['g2'], p['ones_bd']]
    lead, big = _scan_specs(grp)
    return pl.pallas_call(
        kern,
        grid=(bsz // grp,),
        in_specs=[big(seq_len, RWKV_COLS), lead(2, H_A, HEAD_A, HEAD_A)] + [_const_spec(a.shape) for a in consts],
        out_specs=[lead(seq_len, DA), lead(2, H_A, HEAD_A, HEAD_A)],
        out_shape=[jax.ShapeDtypeStruct((bsz, seq_len, DA), F32),
                   jax.ShapeDtypeStruct((bsz, 2, H_A, HEAD_A, HEAD_A), F32)],
        scratch_shapes=[pltpu.VMEM((grp, seq_len, DA), F32)] * 5
                       + [pltpu.VMEM((grp, seq_len, DECAY_LORA + AAA_LORA), F32),
                          pltpu.VMEM((grp, 2, N_PACKS, PACK_W, PACK_W), F32)],
        compiler_params=_params("arbitrary"),
        name="rwkv_scan",
    )(zr, s0, *consts)


def _mlstm(zm, g, gt, c0, n0, m0, p):
    bsz, seq_len, _ = zm.shape
    nc = seq_len // CHUNK
    grp = SCAN_GROUP
    assert bsz % grp == 0
    kern = functools.partial(_scan_kernel, body=_mlstm_body, n_steps=nc, seq_len=seq_len, group=grp)
    consts = [p['conv'], p['bi'], p['bit'], p['bf'], p['bft'], p['ng'], p['e_rep']]
    lead, big = _scan_specs(grp)
    return pl.pallas_call(
        kern,
        grid=(bsz // grp,),
        in_specs=[big(seq_len, MLSTM_MAIN), lead(seq_len, N_GATES), lead(nc, N_GATES, CHUNK),
                  lead(2, H_B, HEAD_B, HEAD_B), lead(2, H_B, HEAD_B), lead(2, H_B, 1)]
                 + [_const_spec(a.shape) for a in consts],
        out_specs=[lead(seq_len, DB), lead(2, H_B, HEAD_B, HEAD_B), lead(2, H_B, HEAD_B), lead(2, H_B, LANES)],
        out_shape=[jax.ShapeDtypeStruct((bsz, seq_len, DB), F32),
                   jax.ShapeDtypeStruct((bsz, 2, H_B, HEAD_B, HEAD_B), F32),
                   jax.ShapeDtypeStruct((bsz, 2, H_B, HEAD_B), F32),
                   jax.ShapeDtypeStruct((bsz, 2, H_B, LANES), F32)],
        scratch_shapes=[pltpu.VMEM((grp, seq_len, 2 * DB), F32), pltpu.VMEM((grp, 2, seq_len, DB), F32),
                        pltpu.VMEM((grp, 2, H_B, HEAD_B, HEAD_B), F32), pltpu.VMEM((grp, 2, H_B, HEAD_B), F32),
                        pltpu.VMEM((grp, 2, H_B, LANES), F32)],
        compiler_params=_params("arbitrary"),
        name="mlstm_scan",
    )(zm, g, gt, c0, n0, m0, *consts)


def _outproj_kernel(x_ref, yr_ref, ym_ref, mod_ref, g_ref, wo_ref, rw_ref, rb_ref,
                    x1_ref, hn_ref, idx_ref, gate_ref):
    mod = mod_ref[0]
    g1 = mod[:, 2 * D_MODEL:3 * D_MODEL]
    sh2, sc2 = mod[:, 3 * D_MODEL:4 * D_MODEL], mod[:, 4 * D_MODEL:5 * D_MODEL]
    mix = _dot_bf16(yr_ref[...], wo_ref[0:DA, :]) + _dot_bf16(ym_ref[...], wo_ref[DA:, :])
    x1 = x_ref[...] + g1 * mix
    x1_ref[...] = x1
    y = x1 * lax.rsqrt(jnp.mean(x1 * x1, axis=-1, keepdims=True) + NORM_EPS)
    hn = y * g_ref[...] * (1.0 + sc2) + sh2
    hn_ref[...] = hn
    logits = _dot_bf16(hn, rw_ref[...]) + rb_ref[...]
    lane = lax.broadcasted_iota(jnp.int32, logits.shape, 1)
    vals, idxs = [], []
    for _ in range(TOP_K):
        top = jnp.max(logits, axis=-1, keepdims=True)
        pick = jnp.min(jnp.where(logits == top, lane, N_EXPERTS), axis=-1, keepdims=True)
        vals.append(top)
        idxs.append(pick)
        logits = jnp.where(lane == pick, -jnp.inf, logits)
    exps = [jnp.exp(v - vals[0]) for v in vals]
    total = exps[0] + exps[1] + exps[2] + exps[3]
    k_lane = lax.broadcasted_iota(jnp.int32, (x1.shape[0], TOP_K), 1)
    gates = jnp.zeros((x1.shape[0], TOP_K), F32)
    picks = jnp.zeros((x1.shape[0], TOP_K), jnp.int32)
    for j in range(TOP_K):
        gates = jnp.where(k_lane == j, exps[j] / total, gates)
        picks = jnp.where(k_lane == j, idxs[j], picks)
    idx_ref[...] = picks
    gate_ref[...] = gates


def _outproj(x2, yr, ym, mod, norm_g, w_out, router_w, router_b):
    rows = x2.shape[0]
    tiles_per_mod = rows // mod.shape[0] // PROJ_TILE
    row_spec = lambda w: pl.BlockSpec((PROJ_TILE, w), lambda i: (i, 0))
    return pl.pallas_call(
        _outproj_kernel,
        grid=(rows // PROJ_TILE,),
        in_specs=[row_spec(D_MODEL), row_spec(DA), row_spec(DB),
                  pl.BlockSpec((1, 1, 6 * D_MODEL), lambda i: (i // tiles_per_mod, 0, 0)),
                  _const_spec((1, D_MODEL)), _const_spec(w_out.shape), _const_spec(router_w.shape),
                  _const_spec((1, N_EXPERTS))],
        out_specs=[row_spec(D_MODEL), row_spec(D_MODEL), row_spec(TOP_K), row_spec(TOP_K)],
        out_shape=[jax.ShapeDtypeStruct((rows, D_MODEL), F32), jax.ShapeDtypeStruct((rows, D_MODEL), F32),
                   jax.ShapeDtypeStruct((rows, TOP_K), jnp.int32), jax.ShapeDtypeStruct((rows, TOP_K), F32)],
        compiler_params=_params("arbitrary"),
        name="outproj_router",
    )(x2, yr, ym, mod, norm_g.reshape(1, D_MODEL), w_out, router_w, router_b.reshape(1, N_EXPERTS))


def _rank_kernel(idx_ref, rank_ref, count_ref, run_s):
    i = pl.program_id(0)

    @pl.when(i == 0)
    def _():
        run_s[...] = jnp.zeros_like(run_s)

    idx = idx_ref[...]
    rows = idx.shape[0]
    lane = lax.broadcasted_iota(jnp.int32, (rows, N_EXPERTS), 1)
    hot = jnp.zeros((rows, N_EXPERTS), F32)
    for j in range(TOP_K):
        hot = hot + jnp.where(lane == idx[:, j:j + 1], 1.0, 0.0)
    before = _dot_bf16(_tri(rows, False, True).astype(F32), hot) + run_s[...]
    k_lane = lax.broadcasted_iota(jnp.int32, (rows, TOP_K), 1)
    rank = jnp.zeros((rows, TOP_K), F32)
    for j in range(TOP_K):
        rj = jnp.sum(jnp.where(lane == idx[:, j:j + 1], before, 0.0), axis=-1, keepdims=True)
        rank = jnp.where(k_lane == j, rj, rank)
    rank_ref[...] = rank.astype(jnp.int32)
    run_s[...] = run_s[...] + jnp.sum(hot, axis=0, keepdims=True)
    count_ref[...] = run_s[...].astype(jnp.int32)


def _expert_ranks(idx):
    rows = idx.shape[0]
    return pl.pallas_call(
        _rank_kernel,
        grid=(rows // ROW_TILE,),
        in_specs=[pl.BlockSpec((ROW_TILE, TOP_K), lambda i: (i, 0))],
        out_specs=[pl.BlockSpec((ROW_TILE, TOP_K), lambda i: (i, 0)), _const_spec((1, N_EXPERTS))],
        out_shape=[jax.ShapeDtypeStruct((rows, TOP_K), jnp.int32),
                   jax.ShapeDtypeStruct((1, N_EXPERTS), jnp.int32)],
        scratch_shapes=[pltpu.VMEM((1, N_EXPERTS), F32)],
        compiler_params=_params("arbitrary"),
        name="expert_ranks",
    )(idx)


def _dispatch_kernel(dest_ref, pend_ref, xa_ref, xb_ref, xs_ref, zero_s, sem, *, n_first_tiles):
    @pl.when(pl.program_id(0) == 0)
    def _():
        zero_s[...] = jnp.zeros_like(zero_s)

        def zero_block(start):
            return pltpu.make_async_copy(zero_s, xs_ref.at[pl.ds(pl.multiple_of(start, MOE_BLOCK), MOE_BLOCK)], sem)

        def has_rows(e):
            return pend_ref[e] > (pend_ref[e - 1] if e else 0)

        used_end = pend_ref[N_EXPERTS - 1]
        n_tail = (xs_ref.shape[0] - used_end) // MOE_BLOCK

        def start_tail(b, carry):
            zero_block(used_end + b * MOE_BLOCK).start()
            return carry

        def wait_tail(b, carry):
            zero_block(0).wait()
            return carry

        for e in range(N_EXPERTS):
            @pl.when(has_rows(e))
            def _(e=e):
                zero_block(pend_ref[e] - MOE_BLOCK).start()
        lax.fori_loop(0, n_tail, start_tail, 0)
        for e in range(N_EXPERTS):
            @pl.when(has_rows(e))
            def _():
                zero_block(0).wait()
        lax.fori_loop(0, n_tail, wait_tail, 0)

    def issue_from(x_ref):
        def issue(g, carry):
            for q in range(SUBLANES):
                for j in range(TOP_K):
                    slot = dest_ref[g * (SUBLANES * TOP_K) + q * TOP_K + j]
                    pltpu.make_async_copy(x_ref.at[g, pl.ds(q, 1)], xs_ref.at[pl.ds(slot, 1)],
                                          sem).start(priority=j % 2)
            return carry
        lax.fori_loop(0, ROW_TILE // SUBLANES, issue, 0)

    pl.when(pl.program_id(0) < n_first_tiles)(lambda: issue_from(xa_ref))
    pl.when(pl.program_id(0) >= n_first_tiles)(lambda: issue_from(xb_ref))
    all_rows = xs_ref.at[pl.ds(0, ROW_TILE * TOP_K)]
    pltpu.make_async_copy(all_rows, all_rows, sem).wait()


def _dispatch(dest_flat, pad_end, xa, xb, n_slots):
    na, nb = xa.shape[0] // ROW_TILE, xb.shape[0] // ROW_TILE
    tiles = lambda x: x.reshape(x.shape[0] // SUBLANES, SUBLANES, D_MODEL)
    block = (ROW_TILE // SUBLANES, SUBLANES, D_MODEL)
    return pl.pallas_call(
        functools.partial(_dispatch_kernel, n_first_tiles=na),
        grid=(na + nb,),
        in_specs=[pl.BlockSpec((ROW_TILE * TOP_K,), lambda i: (i,), memory_space=pltpu.SMEM),
                  pl.BlockSpec(memory_space=pltpu.SMEM),
                  pl.BlockSpec(block, lambda i: (jnp.minimum(i, na - 1), 0, 0)),
                  pl.BlockSpec(block, lambda i: (jnp.maximum(i - na, 0), 0, 0))],
        out_specs=pl.BlockSpec(memory_space=pl.ANY),
        out_shape=jax.ShapeDtypeStruct((n_slots, D_MODEL), xa.dtype),
        scratch_shapes=[pltpu.VMEM((MOE_BLOCK, D_MODEL), F32), pltpu.SemaphoreType.DMA(())],
        compiler_params=_params("arbitrary", disable_bounds_checks=True),
        name="moe_dispatch",
    )(dest_flat, pad_end, tiles(xa), tiles(xb))


def _expert_kernel(be_ref, nb_ref, xs_ref, w1_ref, b1_ref, w2_ref, b2_ref, y_ref, w1_s, w2_s):
    i = pl.program_id(0)
    changed = jnp.logical_or(i == 0, be_ref[i] != be_ref[jnp.maximum(i - 1, 0)])

    @pl.when(jnp.logical_and(changed, i < nb_ref[0]))
    def _():
        w1_s[...] = w1_ref[0].astype(BF16)
        w2_s[...] = w2_ref[0].astype(BF16)

    @pl.when(i < nb_ref[0])
    def _():
        xb = xs_ref[...].astype(BF16)
        hu = lax.dot_general(xb, w1_s[...], NN, preferred_element_type=F32) + b1_ref[0]
        glu = jnp.minimum(hu[:, :D_FF], SWIGLU_LIMIT)
        lin = jnp.clip(hu[:, D_FF:], -SWIGLU_LIMIT, SWIGLU_LIMIT)
        act = glu * _sigmoid(SWIGLU_ALPHA * glu) * (lin + 1.0)
        y_ref[...] = lax.dot_general(act.astype(BF16), w2_s[...], NN, preferred_element_type=F32) + b2_ref[0]

    @pl.when(i >= nb_ref[0])
    def _():
        y_ref[...] = jnp.zeros_like(y_ref)


def _experts(block_e, n_used, xs, w1, b1, w2, b2):
    n_blocks = xs.shape[0] // MOE_BLOCK
    grid_spec = pltpu.PrefetchScalarGridSpec(
        num_scalar_prefetch=2,
        grid=(n_blocks,),
        in_specs=[pl.BlockSpec((MOE_BLOCK, D_MODEL), lambda i, be, nb: (jnp.minimum(i, nb[0] - 1), 0)),
                  pl.BlockSpec((1, D_MODEL, 2 * D_FF), lambda i, be, nb: (be[i], 0, 0)),
                  pl.BlockSpec((1, 1, 2 * D_FF), lambda i, be, nb: (be[i], 0, 0)),
                  pl.BlockSpec((1, D_FF, D_MODEL), lambda i, be, nb: (be[i], 0, 0)),
                  pl.BlockSpec((1, 1, D_MODEL), lambda i, be, nb: (be[i], 0, 0))],
        out_specs=pl.BlockSpec((MOE_BLOCK, D_MODEL), lambda i, be, nb: (i, 0)),
        scratch_shapes=[pltpu.VMEM((D_MODEL, 2 * D_FF), BF16), pltpu.VMEM((D_FF, D_MODEL), BF16)],
    )
    return pl.pallas_call(
        _expert_kernel,
        grid_spec=grid_spec,
        out_shape=jax.ShapeDtypeStruct(xs.shape, F32),
        compiler_params=_params("arbitrary"),
        name="moe_experts",
    )(block_e, n_used, xs, w1, b1.reshape(N_EXPERTS, 1, 2 * D_FF), w2, b2.reshape(N_EXPERTS, 1, D_MODEL))


def _combine_kernel(dest_ref, yb_ref, x1_ref, gate_ref, mod_ref, fg_ref, o_ref, buf, sem):
    def issue(g, carry):
        for q in range(SUBLANES):
            for j in range(TOP_K):
                slot = dest_ref[g * (SUBLANES * TOP_K) + q * TOP_K + j]
                pltpu.make_async_copy(yb_ref.at[pl.ds(slot, 1)], buf.at[g, j, pl.ds(q, 1)], sem).start(priority=j % 2)
        return carry

    lax.fori_loop(0, ROW_TILE // SUBLANES, issue, 0)
    pltpu.make_async_copy(buf, buf, sem).wait()

    g2 = mod_ref[0][:, 5 * D_MODEL:6 * D_MODEL]
    gates = gate_ref[...]
    moe = jnp.zeros((ROW_TILE, D_MODEL), F32)
    for j in range(TOP_K):
        moe = moe + buf[:, j].reshape(ROW_TILE, D_MODEL) * gates[:, j:j + 1]
    x2 = x1_ref[...] + g2 * moe
    y = x2 * lax.rsqrt(jnp.mean(x2 * x2, axis=-1, keepdims=True) + NORM_EPS)
    o_ref[...] = y * fg_ref[...]


def _combine(dest_flat, yb, x1, gates, mod, final_g, row_offset):
    rows = x1.shape[0]
    tiles_per_mod = rows // mod.shape[0] // ROW_TILE
    tile_offset = row_offset // ROW_TILE
    row_spec = pl.BlockSpec((ROW_TILE, D_MODEL), lambda i: (i, 0))
    return pl.pallas_call(
        _combine_kernel,
        grid=(rows // ROW_TILE,),
        in_specs=[pl.BlockSpec((ROW_TILE * TOP_K,), lambda i: (i + tile_offset,), memory_space=pltpu.SMEM),
                  pl.BlockSpec(memory_space=pl.ANY),
                  row_spec,
                  pl.BlockSpec((ROW_TILE, TOP_K), lambda i: (i, 0)),
                  pl.BlockSpec((1, 1, 6 * D_MODEL), lambda i: (i // tiles_per_mod, 0, 0)),
                  _const_spec((1, D_MODEL))],
        out_specs=row_spec,
        out_shape=jax.ShapeDtypeStruct((rows, D_MODEL), F32),
        scratch_shapes=[pltpu.VMEM((ROW_TILE // SUBLANES, TOP_K, SUBLANES, D_MODEL), F32),
                        pltpu.SemaphoreType.DMA(())],
        compiler_params=_params("arbitrary", disable_bounds_checks=True),
        name="moe_combine",
    )(dest_flat, yb, x1, gates, mod, final_g.reshape(1, D_MODEL))


def _block_diag_ones(width, block):
    i = jnp.arange(width) // block
    return (i[:, None] == i[None, :]).astype(F32)


def _mixers(x, mod, lp, s0, c0, n0, m0, grid):
    bsz, seq_len, _ = x.shape
    nc = seq_len // CHUNK
    x2 = x.reshape(bsz * seq_len, D_MODEL)
    zr, zm, zg, zgt = _inproj(x2, mod, lp['norm1_g'], lp['w_r'], lp['w_m'], lp['w_g'], lp['w_gt'])
    gt = zgt.reshape(N_GATES, bsz, nc, CHUNK).transpose(1, 2, 0, 3)
    yr, s_new = _rwkv(zr.reshape(bsz, seq_len, RWKV_COLS), s0, lp, grid)
    ym, c_new, n_new, m_new = _mlstm(zm.reshape(bsz, seq_len, MLSTM_MAIN), zg.reshape(bsz, seq_len, N_GATES), gt,
                                     c0, n0, m0.reshape(bsz, 2, H_B, 1), lp)
    x1, hn, idx, gates = _outproj(x2, yr.reshape(-1, DA), ym.reshape(-1, DB), mod, lp['norm2_g'], lp['w_out'],
                                  lp['router_w'], lp['router_b'])
    return x1, hn, idx, gates, (s_new, c_new, n_new, m_new[..., 0])


def kernel(x_prompt, x_sample, state_rwkv, state_mlstm_C, state_mlstm_n, state_mlstm_m, c, c_ctx, ada_w, ada_b, norm1_g, norm2_g, w_in, w_out, rwkv_mu, rwkv_w0, rwkv_w2, rwkv_a0, rwkv_a2, rwkv_g2, rwkv_kk, rwkv_ka, rwkv_rk, rwkv_gn_w, rwkv_gn_b, mlstm_conv, mlstm_bi, mlstm_bf, mlstm_norm_g, router_w, router_b, moe_w1, moe_b1, moe_w2, moe_b2, final_g):
    bp, lp_len, _ = x_prompt.shape
    bs, ls_len, _ = x_sample.shape
    w = w_in[0]
    lp = {
        'norm1_g': norm1_g[0], 'norm2_g': norm2_g[0],
        'w_r': w[:, :RWKV_COLS].astype(BF16),
        'w_m': w[:, RWKV_COLS:RWKV_COLS + MLSTM_MAIN].astype(BF16),
        'w_g': w[:, RWKV_COLS + MLSTM_MAIN:],
        'w_gt': w[:, RWKV_COLS + MLSTM_MAIN:].T,
        'w_out': w_out[0].astype(BF16),
        'mu': rwkv_mu[0].reshape(1, RWKV_COLS), 'kkw': rwkv_kk[0].reshape(1, DA), 'ka': rwkv_ka[0].reshape(1, DA),
        'rk': rwkv_rk[0].reshape(1, DA), 'gnw': rwkv_gn_w[0].reshape(1, DA), 'gnb': rwkv_gn_b[0].reshape(1, DA),
        'w0': rwkv_w0[0], 'w2': rwkv_w2[0], 'a0': rwkv_a0[0], 'a2': rwkv_a2[0], 'g2': rwkv_g2[0],
        'ones_bd': _block_diag_ones(PACK_W, HEAD_A).astype(BF16),
        'conv': mlstm_conv[0], 'bi': mlstm_bi[0].reshape(1, 2 * H_B), 'bit': mlstm_bi[0].reshape(2 * H_B, 1),
        'bf': mlstm_bf[0].reshape(1, 2 * H_B), 'bft': mlstm_bf[0].reshape(2 * H_B, 1),
        'ng': mlstm_norm_g[0].reshape(1, DB),
        'e_rep': jnp.repeat(jnp.eye(N_GATES, dtype=BF16), LANES, axis=1),
        'router_w': router_w[0], 'router_b': router_b[0],
    }
    cc = jnp.concatenate([c_ctx[None, :], c, jnp.zeros((2 * SUBLANES - 1 - bs, D_MODEL), F32)], axis=0)
    mod = _ada_mod(cc, ada_w[0], ada_b[0])
    mod_p = mod[0:1].reshape(1, 1, 6 * D_MODEL)
    mod_s = mod[1:1 + bs].reshape(bs, 1, 6 * D_MODEL)

    zeros = lambda *shape: jnp.zeros(shape, F32)
    x1p, hnp, idxp, gatesp, st = _mixers(x_prompt, mod_p, lp, zeros(bp, 2, H_A, HEAD_A, HEAD_A),
                                         zeros(bp, 2, H_B, HEAD_B, HEAD_B), zeros(bp, 2, H_B, HEAD_B),
                                         zeros(bp, 2, H_B), False)
    x1s, hns, idxs, gatess, _ = _mixers(x_sample, mod_s, lp, state_rwkv[:, 0], state_mlstm_C[:, 0],
                                        state_mlstm_n[:, 0], state_mlstm_m[:, 0], True)

    n_p = bp * lp_len
    idx = jnp.concatenate([idxp, idxs], axis=0)
    n_tok = idx.shape[0]
    rank, counts = _expert_ranks(idx)
    counts = counts[0]
    padded = (counts + MOE_BLOCK - 1) // MOE_BLOCK * MOE_BLOCK
    pad_end = jnp.cumsum(padded)
    pad_start = pad_end - padded
    n_blocks = n_tok * TOP_K // MOE_BLOCK + N_EXPERTS
    block_start = jnp.arange(n_blocks, dtype=jnp.int32) * MOE_BLOCK
    block_e = jnp.minimum(jnp.sum(pad_end[None, :] <= block_start[:, None], axis=1), N_EXPERTS - 1).astype(jnp.int32)
    n_used = (pad_end[-1:] // MOE_BLOCK).astype(jnp.int32)
    expert_ids = jnp.arange(N_EXPERTS, dtype=jnp.int32)
    start_of = jnp.sum(jnp.where(idx[:, :, None] == expert_ids, pad_start.astype(jnp.int32), 0), axis=-1)
    dest = (start_of + rank).astype(jnp.int32).reshape(-1)
    xs = _dispatch(dest, pad_end.astype(jnp.int32), hnp, hns, n_blocks * MOE_BLOCK)
    yb = _experts(block_e, n_used, xs, moe_w1[0], moe_b1[0], moe_w2[0], moe_b2[0])
    y_prompt = _combine(dest, yb, x1p, gatesp, mod_p, final_g, 0).reshape(x_prompt.shape)
    y_sample = _combine(dest, yb, x1s, gatess, mod_s, final_g, n_p).reshape(x_sample.shape)

    s_new, c_new, n_new, m_new = st
    return (y_prompt, y_sample, s_new[:, None], c_new[:, None], n_new[:, None], m_new[:, None])
```

```python
import functools

import jax
import jax.numpy as jnp
from jax import lax
from jax.experimental import pallas as pl
from jax.experimental.pallas import tpu as pltpu

F32 = jnp.float32
BF16 = jnp.bfloat16
HIGHEST = lax.Precision.HIGHEST

D_MODEL = 1024
DA = 512
HEAD_A = 64
H_A = DA // HEAD_A
DB = 512
H_B = 4
HEAD_B = DB // H_B
DECAY_LORA = 64
AAA_LORA = 64
GATE_LORA = 128
RWKV_COLS = 3 * DA + DECAY_LORA + AAA_LORA + GATE_LORA
MLSTM_MAIN = 4 * DB
N_GATES = 4 * H_B
GRID_W = 64
CHUNK = 64
N_EXPERTS = 32
TOP_K = 4
D_FF = D_MODEL
SWIGLU_LIMIT = 7.0
SWIGLU_ALPHA = 1.702
MOE_BLOCK = 512
NORM_EPS = 1e-6
GN_EPS = 64e-5
ROW_TILE = 1024
PROJ_TILE = 512
SUBLANES = 8
LANES = 128
LANE_ROWS = D_MODEL // LANES
VMEM_LIMIT = 56 * 1024 * 1024
PACK = 4
PACK_W = PACK * HEAD_A
N_PACKS = H_A // PACK
SCAN_GROUP = 2
SCAN_DOUBLE_BUFFER_BYTES = 16 * 1024 * 1024

NN = (((1,), (0,)), ((), ()))
NT = (((1,), (1,)), ((), ()))
TN = (((0,), (0,)), ((), ()))


def _mm(a, b, precision=HIGHEST):
    return lax.dot_general(a, b, NN, precision=precision, preferred_element_type=F32)


def _mm_nt(a, b, precision=HIGHEST):
    return lax.dot_general(a, b, NT, precision=precision, preferred_element_type=F32)


def _mm_tn(a, b, precision=HIGHEST):
    return lax.dot_general(a, b, TN, precision=precision, preferred_element_type=F32)


def _dot_bf16(a, b, dims=NN):
    return lax.dot_general(a.astype(BF16), b.astype(BF16), dims, preferred_element_type=F32)


def _sigmoid(x):
    return 1.0 / (1.0 + jnp.exp(-x))


def _log_sigmoid(x):
    return jnp.minimum(x, 0.0) - jnp.log(1.0 + jnp.exp(-jnp.abs(x)))


def _params(*sem, **kw):
    return pltpu.CompilerParams(dimension_semantics=sem, vmem_limit_bytes=VMEM_LIMIT, **kw)


def _split_bf16(x):
    hi = x.astype(BF16)
    return hi, (x - hi.astype(F32)).astype(BF16)


def _const_spec(shape):
    nd = len(shape)
    return pl.BlockSpec(shape, lambda *_: (0,) * nd)


def _tri(n, reverse, strict):
    t = lax.broadcasted_iota(jnp.int32, (n, n), 0)
    s = lax.broadcasted_iota(jnp.int32, (n, n), 1)
    if reverse:
        return (s > t) if strict else (s >= t)
    return (s < t) if strict else (s <= t)


def _seq_neighbours(ref, b, c, nc, cols):
    r0 = pl.multiple_of(c * CHUNK, CHUNK)
    zc = ref[b, pl.ds(r0, CHUNK), cols]
    row = lax.broadcasted_iota(jnp.int32, zc.shape, 0)
    p0 = pl.multiple_of(jnp.maximum(r0 - SUBLANES, 0), SUBLANES)
    n0 = pl.multiple_of(jnp.minimum(r0 + CHUNK, (nc - 1) * CHUNK), SUBLANES)
    before = ref[b, pl.ds(p0, SUBLANES), cols][SUBLANES - 1:SUBLANES]
    after = ref[b, pl.ds(n0, SUBLANES), cols][0:1]
    before = jnp.where(c > 0, before, 0.0)
    after = jnp.where(c < nc - 1, after, 0.0)
    prev = jnp.where(row == 0, before, pltpu.roll(zc, 1, 0))
    nxt = jnp.where(row == CHUNK - 1, after, pltpu.roll(zc, CHUNK - 1, 0))
    return zc, prev, nxt


def _ada_kernel(c_ref, w_ref, b_ref, o_ref):
    cc = c_ref[...]
    o_ref[...] = _dot_bf16(cc * _sigmoid(cc), w_ref[...]) + b_ref[...]


def _ada_mod(cc, ada_w, ada_b):
    rows = cc.shape[0]
    ncol = ada_w.shape[1]
    tn = 1536
    return pl.pallas_call(
        _ada_kernel,
        grid=(ncol // tn,),
        in_specs=[_const_spec((rows, D_MODEL)),
                  pl.BlockSpec((D_MODEL, tn), lambda j: (0, j)),
                  pl.BlockSpec((1, tn), lambda j: (0, j))],
        out_specs=pl.BlockSpec((rows, tn), lambda j: (0, j)),
        out_shape=jax.ShapeDtypeStruct((rows, ncol), F32),
        compiler_params=_params("arbitrary"),
        name="ada_mod",
    )(cc, ada_w, ada_b.reshape(1, ncol))


def _inproj_kernel(x_ref, mod_ref, g_ref, wr_ref, wm_ref, wg_ref, wgt_ref, zr_ref, zm_ref, zg_ref, zgt_ref):
    x = x_ref[...]
    mod = mod_ref[0]
    sh, sc = mod[:, 0:D_MODEL], mod[:, D_MODEL:2 * D_MODEL]
    y = x * lax.rsqrt(jnp.mean(x * x, axis=-1, keepdims=True) + NORM_EPS)
    hn = y * g_ref[...] * (1.0 + sc) + sh
    hb = hn.astype(BF16)
    zr_ref[...] = lax.dot_general(hb, wr_ref[...], NN, preferred_element_type=F32)
    zm_ref[...] = lax.dot_general(hb, wm_ref[...], NN, preferred_element_type=F32)
    zg_ref[...] = _dot_bf16(hb, wg_ref[...])
    zgt_ref[...] = _dot_bf16(wgt_ref[...], hb, NT)


def _inproj(x2, mod, norm_g, w_r, w_m, w_g, w_gt):
    rows = x2.shape[0]
    tiles_per_mod = rows // mod.shape[0] // PROJ_TILE
    return pl.pallas_call(
        _inproj_kernel,
        grid=(rows // PROJ_TILE,),
        in_specs=[pl.BlockSpec((PROJ_TILE, D_MODEL), lambda i: (i, 0)),
                  pl.BlockSpec((1, 1, 6 * D_MODEL), lambda i: (i // tiles_per_mod, 0, 0)),
                  _const_spec((1, D_MODEL)),
                  _const_spec(w_r.shape), _const_spec(w_m.shape), _const_spec(w_g.shape),
                  _const_spec(w_gt.shape)],
        out_specs=[pl.BlockSpec((PROJ_TILE, RWKV_COLS), lambda i: (i, 0)),
                   pl.BlockSpec((PROJ_TILE, MLSTM_MAIN), lambda i: (i, 0)),
                   pl.BlockSpec((PROJ_TILE, N_GATES), lambda i: (i, 0)),
                   pl.BlockSpec((N_GATES, PROJ_TILE), lambda i: (0, i))],
        out_shape=[jax.ShapeDtypeStruct((rows, RWKV_COLS), F32),
                   jax.ShapeDtypeStruct((rows, MLSTM_MAIN), F32),
                   jax.ShapeDtypeStruct((rows, N_GATES), F32),
                   jax.ShapeDtypeStruct((N_GATES, rows), F32)],
        compiler_params=_params("arbitrary"),
        name="inproj",
    )(x2, mod, norm_g.reshape(1, D_MODEL), w_r, w_m, w_g, w_gt)


def _group_sum(x, ones_bd):
    hi = x.astype(BF16)
    lo = (x - hi.astype(F32)).astype(BF16)
    return (lax.dot_general(hi, ones_bd, NN, preferred_element_type=F32)
            + lax.dot_general(lo, ones_bd, NN, preferred_element_type=F32))


def _block_diag(x, ones_bd):
    return jnp.concatenate([x.astype(BF16)] * PACK, axis=0) * ones_bd


def _rwkv_body(zr_ref, s0_ref, mu_ref, kkw_ref, ka_ref, rk_ref, gnw_ref, gnb_ref, w0_ref, w2_ref,
                 a0_ref, a2_ref, g2_ref, ones_ref, y_ref, sout_ref,
                 r_s, k_s, v_s, kk_s, gate_s, lx_s, st_s, *, seq_len, grid, group):
    nc = seq_len // CHUNK
    ones_bd = ones_ref[...]

    def mix_chunk(c, carry):
        r0 = pl.multiple_of(c * CHUNK, CHUNK)
        for b in range(group):
            if grid:
                zc = zr_ref[b, pl.ds(r0, CHUNK), :]
                row = lax.broadcasted_iota(jnp.int32, zc.shape, 0)
                up0 = pl.multiple_of(jnp.maximum(c - 1, 0) * CHUNK, CHUNK)
                dn0 = pl.multiple_of(jnp.minimum(c + 1, nc - 1) * CHUNK, CHUNK)
                up = jnp.where(c > 0, zr_ref[b, pl.ds(up0, CHUNK), :], 0.0)
                down = jnp.where(c < nc - 1, zr_ref[b, pl.ds(dn0, CHUNK), :], 0.0)
                left = jnp.where(row == 0, 0.0, pltpu.roll(zc, 1, 0))
                right = jnp.where(row == CHUNK - 1, 0.0, pltpu.roll(zc, CHUNK - 1, 0))
                local = 0.25 * (up + down + left + right)
            else:
                zc, prev, nxt = _seq_neighbours(zr_ref, b, c, nc, slice(None))
                local = 0.5 * (prev + nxt)
            z = zc + (local - zc) * mu_ref[...]
            r, k, v = z[:, 0:DA], z[:, DA:2 * DA], z[:, 2 * DA:3 * DA]
            lx = z[:, 3 * DA:3 * DA + DECAY_LORA + AAA_LORA]
            xg = z[:, 3 * DA + DECAY_LORA + AAA_LORA:]
            kk = k * kkw_ref[...]
            sq = kk * kk
            ss = jnp.concatenate([_group_sum(sq[:, p * PACK_W:(p + 1) * PACK_W], ones_bd) for p in range(N_PACKS)],
                                 axis=1)
            kk = kk / jnp.maximum(jnp.sqrt(ss), 1e-12)
            lane = lax.broadcasted_iota(jnp.int32, lx.shape, 1)
            r_s[b, pl.ds(r0, CHUNK), :] = r
            k_s[b, pl.ds(r0, CHUNK), :] = k
            v_s[b, pl.ds(r0, CHUNK), :] = v
            kk_s[b, pl.ds(r0, CHUNK), :] = kk
            gate_s[b, pl.ds(r0, CHUNK), :] = _dot_bf16(_sigmoid(xg), g2_ref[...])
            lx_s[b, pl.ds(r0, CHUNK), :] = jnp.where(lane < DECAY_LORA, jnp.tanh(lx), lx)
        return carry

    lax.fori_loop(0, nc, mix_chunk, 0)

    rowblk = lax.broadcasted_iota(jnp.int32, (PACK_W, PACK_W), 0) // HEAD_A
    colblk = lax.broadcasted_iota(jnp.int32, (PACK_W, PACK_W), 1) // HEAD_A
    for b in range(group):
        for d in range(2):
            for p in range(N_PACKS):
                rows_ = jnp.concatenate([s0_ref[b, d, p * PACK + h] for h in range(PACK)], axis=0)
                st_s[b, d, p] = jnp.where(rowblk == colblk, jnp.concatenate([rows_] * PACK, axis=1), 0.0)

    t_idx = lax.broadcasted_iota(jnp.int32, (CHUNK, PACK_W), 0)
    s_idx = lax.broadcasted_iota(jnp.int32, (CHUNK, PACK_W), 1) % CHUNK

    def stages(i):
        first_touch = i < nc // 2
        dot = functools.partial(lax.dot_general, preferred_element_type=F32)
        chains, out_sel = [], []
        for b, d in [(b, d) for b in range(group) for d in range(2)]:
            reverse = d == 1
            c = nc - 1 - i if reverse else i
            rows = pl.ds(pl.multiple_of(c * CHUNK, CHUNK), CHUNK)
            out_sel.append((b, rows))
            strict = (s_idx > t_idx) if reverse else (s_idx < t_idx)
            incl = (s_idx >= t_idx) if reverse else (s_idx <= t_idx)
            last = 0 if reverse else CHUNK - 1
            r, k, v, kk = r_s[b, rows, :], k_s[b, rows, :], v_s[b, rows, :], kk_s[b, rows, :]
            lx = lx_s[b, rows, :]
            wl = w0_ref[d:d + 1, :] + _dot_bf16(lx[:, 0:DECAY_LORA], w2_ref[d])
            logw = -jnp.exp(_log_sigmoid(wl) - 0.5)
            a = _sigmoid(a0_ref[d:d + 1, :] + _dot_bf16(lx[:, DECAY_LORA:], a2_ref[d]))
            kd = k * (1.0 + (a - 1.0) * ka_ref[...])
            kb = kk * a
            tri = _tri(CHUNK, reverse, False).astype(BF16)
            w1, w_rest = _split_bf16(logw)
            w2, w3 = _split_bf16(w_rest)
            cum = dot(tri, w1, NN) + dot(tri, w2, NN) + dot(tri, w3, NN)
            clast = cum[last:last + 1]
            rt = r * jnp.exp(cum)
            at = -kk * jnp.exp(cum - logw)
            einv = jnp.exp(-cum)
            kt, bt = kd * einv, kb * einv
            edec = jnp.exp(clast - cum)
            kp, bp = kd * edec, kb * edec
            ptot = jnp.exp(clast)
            rkd = r * kd * rk_ref[...]
            for p in range(N_PACKS):
                cs = slice(p * PACK_W, (p + 1) * PACK_W)
                chains.append(dict(
                    out=len(out_sel) - 1, st=(b, d, p), cs=cs, strict=strict, incl=incl, v=v[:, cs],
                    rkd=rkd[:, cs], ptot=ptot[:, cs],
                    ar=jnp.concatenate([at[:, cs], rt[:, cs]], axis=0).astype(BF16),
                    kt_bd=_block_diag(kt[:, cs], ones_bd), bt_bd=_block_diag(bt[:, cs], ones_bd),
                    v_bd=_block_diag(v[:, cs], ones_bd),
                    kbp=jnp.concatenate([kp[:, cs], bp[:, cs]], axis=0)))
            yield

        for ch in chains:
            ch['s_prev'] = st_s[ch['st']]
            ch['a_k'] = dot(ch['ar'], ch['kt_bd'], NT)
            ch['a_b'] = dot(ch['ar'], ch['bt_bd'], NT)
            ch['ars'] = dot(ch['ar'], ch['s_prev'].astype(BF16), NT)
        yield
        for ch in chains:
            a_ak = jnp.where(ch['strict'], ch['a_k'][:CHUNK], 0.0)
            a_rk = jnp.where(ch['incl'], ch['a_k'][CHUNK:], 0.0)
            ch['a_rb'] = jnp.where(ch['incl'], ch['a_b'][CHUNK:], 0.0).astype(BF16)
            n = jnp.where(ch['strict'], ch['a_b'][:CHUNK], 0.0)
            ch['n_hi'], n_lo = _split_bf16(n)
            ch['n_both'] = jnp.concatenate([ch['n_hi'], n_lo], axis=0)
            ch['pw'] = ch['n_hi']
            ch['m'] = jnp.where(s_idx == t_idx, 1.0, 0.0) + n
            akv = dot(jnp.concatenate([a_ak, a_rk], axis=0).astype(BF16), ch['v_bd'], NN)
            ch['rhs'] = ch['ars'][:CHUNK] + akv[:CHUNK]
            ch['y'] = ch['ars'][CHUNK:] + akv[CHUNK:]
        yield
        for ch in chains:
            ch['pw'] = dot(ch['pw'], _block_diag(ch['pw'], ones_bd), NN).astype(BF16)
        yield
        for j in range(1, 5):
            for ch in chains:
                pw_bd = _block_diag(ch['pw'], ones_bd)
                if j < 4:
                    both = dot(jnp.concatenate([ch['m'].astype(BF16), ch['pw']], axis=0), pw_bd, NN)
                    ch['m'] = ch['m'] + both[:CHUNK]
                    ch['pw'] = both[CHUNK:].astype(BF16)
                else:
                    ch['m'] = ch['m'] + dot(ch['m'].astype(BF16), pw_bd, NN)
            yield
        for ch in chains:
            ch['mb'] = ch['m'].astype(BF16)
            ch['u'] = dot(ch['mb'], _block_diag(ch['rhs'], ones_bd), NN)
        yield
        for ch in chains:
            u_hi, u_lo = _split_bf16(ch['u'])
            both = dot(ch['n_both'], _block_diag(u_hi, ones_bd), NN)
            nu = both[:CHUNK] + both[CHUNK:] + dot(ch['n_hi'], _block_diag(u_lo, ones_bd), NN)
            ch['resid'] = ch['rhs'] - ch['u'] + nu
        yield
        for ch in chains:
            ch['u'] = ch['u'] + dot(ch['mb'], _block_diag(ch['resid'], ones_bd), NN)
        yield
        for ch in chains:
            u = ch['u']
            ch['y'] = ch['y'] + dot(ch['a_rb'], _block_diag(u, ones_bd), NN)
            vu_hi, vu_lo = _split_bf16(jnp.concatenate([ch['v'], u], axis=0))
            kbp_hi, kbp_lo = _split_bf16(ch['kbp'])
            grown = dot(vu_hi, kbp_hi, TN) + dot(vu_hi, kbp_lo, TN) + dot(vu_lo, kbp_hi, TN)
            st_s[ch['st']] = ch['s_prev'] * ch['ptot'] + grown * ones_bd.astype(F32)
        yield
        n_ch = len(chains)
        sums = dot(jnp.concatenate([ch['y'] for ch in chains] + [ch['rkd'] for ch in chains],
                                   axis=0).astype(BF16), ones_bd, NN)
        ycs = [ch['y'] - sums[q * CHUNK:(q + 1) * CHUNK] * (1.0 / HEAD_A) for q, ch in enumerate(chains)]
        var = dot(jnp.concatenate([yc * yc for yc in ycs], axis=0).astype(BF16), ones_bd, NN) * (1.0 / HEAD_A)
        outs = [[] for _ in out_sel]
        for q, ch in enumerate(chains):
            cs = ch['cs']
            out = ycs[q] * lax.rsqrt(var[q * CHUNK:(q + 1) * CHUNK] + GN_EPS) * gnw_ref[:, cs] + gnb_ref[:, cs]
            outs[ch['out']].append(out + sums[(n_ch + q) * CHUNK:(n_ch + q + 1) * CHUNK] * ch['v'])
        for (b, rows), parts in zip(out_sel, outs):
            out = jnp.concatenate(parts, axis=1)

            @pl.when(first_touch)
            def _(b=b, rows=rows, out=out):
                y_ref[b, rows, :] = out

            @pl.when(jnp.logical_not(first_touch))
            def _(b=b, rows=rows, out=out):
                y_ref[b, rows, :] = (y_ref[b, rows, :] + out) * gate_s[b, rows, :]

    def finish():
        for b in range(group):
            for d in range(2):
                for p in range(N_PACKS):
                    s_fin = st_s[b, d, p]
                    for h in range(PACK):
                        sout_ref[b, d, p * PACK + h] = s_fin[h * HEAD_A:(h + 1) * HEAD_A,
                                                             h * HEAD_A:(h + 1) * HEAD_A]

    return stages, finish


def _mlstm_body(zm_ref, g_ref, gt_ref, c0_ref, n0_ref, m0_ref, conv_ref, bi_ref, bit_ref, bf_ref, bft_ref,
                  ng_ref, erep_ref, y_ref, cout_ref, nout_ref, mout_ref,
                  qk_s, h_s, c_s, n_s, m_s, *, seq_len, group):
    nc = seq_len // CHUNK

    def conv_chunk(c, carry):
        for b in range(group):
            zc, prev, nxt = _seq_neighbours(zm_ref, b, c, nc, slice(0, 2 * DB))
            u = conv_ref[0:1, :] * prev + conv_ref[1:2, :] * zc + conv_ref[2:3, :] * nxt
            qk = u * _sigmoid(u)
            lane = lax.broadcasted_iota(jnp.int32, qk.shape, 1)
            qk_s[b, pl.ds(pl.multiple_of(c * CHUNK, CHUNK), CHUNK), :] = jnp.where(lane >= DB,
                                                                                  qk * (HEAD_B ** -0.5), qk)
        return carry

    lax.fori_loop(0, nc, conv_chunk, 0)

    for b in range(group):
        for d in range(2):
            c_s[b, d] = c0_ref[b, d]
            n_s[b, d] = n0_ref[b, d]
            m_s[b, d] = jnp.broadcast_to(m0_ref[b, d], (H_B, LANES))

    e_rep = erep_ref[...]
    ones_tl = jnp.ones((CHUNK, LANES), BF16)
    row_id = lax.broadcasted_iota(jnp.int32, (CHUNK, LANES), 0)

    def scan_max(x, reverse):
        shift = 1
        while shift < CHUNK:
            if reverse:
                moved = jnp.where(row_id < CHUNK - shift, pltpu.roll(x, CHUNK - shift, 0), -jnp.inf)
            else:
                moved = jnp.where(row_id >= shift, pltpu.roll(x, shift, 0), -jnp.inf)
            x = jnp.maximum(x, moved)
            shift *= 2
        return x

    def dot3(a, b, dims):
        a_hi, a_lo = _split_bf16(a)
        b_hi, b_lo = _split_bf16(b)
        dg = functools.partial(lax.dot_general, dimension_numbers=dims, preferred_element_type=F32)
        return dg(a_hi, b_hi) + dg(a_hi, b_lo) + dg(a_lo, b_hi)

    def stages(i):
        dot = functools.partial(lax.dot_general, preferred_element_type=F32)
        chains = []
        for b, d in [(b, d) for b in range(group) for d in range(2)]:
            reverse = d == 1
            incl = _tri(CHUNK, reverse, False)
            tri_b = incl.astype(BF16)
            tri_row = _tri(CHUNK, not reverse, False).astype(F32)
            last = 0 if reverse else CHUNK - 1
            gsl = slice(d * H_B, (d + 1) * H_B)
            fsl = slice(2 * H_B + d * H_B, 2 * H_B + (d + 1) * H_B)
            c = nc - 1 - i if reverse else i
            rows = pl.ds(pl.multiple_of(c * CHUNK, CHUNK), CHUNK)
            g1, g_rest = _split_bf16(g_ref[b, rows, :])
            g2, g3 = _split_bf16(g_rest)
            g_rep = dot(g1, e_rep, NN) + dot(g2, e_rep, NN) + dot(g3, e_rep, NN)
            grow = gt_ref[b, c]
            i_row = grow[gsl, :] + bit_ref[gsl, :]
            f_row = _log_sigmoid(grow[fsl, :] + bft_ref[gsl, :])
            b_row = _mm(f_row, tri_row)
            for h in range(H_B):
                ji, jf = d * H_B + h, 2 * H_B + d * H_B + h
                ic = g_rep[:, ji * LANES:(ji + 1) * LANES] + bi_ref[:, ji:ji + 1]
                fc = _log_sigmoid(g_rep[:, jf * LANES:(jf + 1) * LANES] + bf_ref[:, ji:ji + 1])
                f1, f_rest = _split_bf16(fc)
                f2, f3 = _split_bf16(f_rest)
                bc = dot(tri_b, f1, NN) + dot(tri_b, f2, NN) + dot(tri_b, f3, NN)
                q = qk_s[b, rows, h * HEAD_B:(h + 1) * HEAD_B]
                chains.append(dict(
                    b=b, d=d, h=h, rows=rows, incl=incl, reverse=reverse, q=q, qb=q.astype(BF16),
                    k=qk_s[b, rows, DB + h * HEAD_B:DB + (h + 1) * HEAD_B],
                    v=zm_ref[b, rows, 2 * DB + h * HEAD_B:2 * DB + (h + 1) * HEAD_B],
                    bc=bc, ic=ic, bl=bc[last:last + 1, :], br=b_row[h:h + 1, :], ir=i_row[h:h + 1, :]))
            yield
        for q, ch in enumerate(chains):
            b, d, h = ch['b'], ch['d'], ch['h']
            ch['qk'] = _dot_bf16(ch['qb'], ch['k'], NT)
            ch['c_prev'] = c_s[b, d, h]
            ch['qc'] = _dot_bf16(ch['qb'], ch['c_prev'])
            ch['n_prev'] = n_s[b, d, h:h + 1, :]
            ch['qn'] = dot3(ch['q'], jnp.broadcast_to(ch['n_prev'], (HEAD_B, HEAD_B)), NT)
            if q % 2 == 1:
                yield
        for q, ch in enumerate(chains):
            b, d, h = ch['b'], ch['d'], ch['h']
            bc, ic, bl = ch['bc'], ch['ic'], ch['bl']
            m_prev = m_s[b, d, h:h + 1, :]
            m_t = bc + jnp.maximum(m_prev, scan_max(ic - bc, ch['reverse']))
            dmat = jnp.where(ch['incl'], bc[:, :CHUNK] - ch['br'] + ch['ir'], -jnp.inf)
            s = ch['qk'] * jnp.exp(dmat - m_t[:, :CHUNK])
            carry_w = jnp.exp(bc + m_prev - m_t)
            s_hi, s_lo = _split_bf16(s)
            den = dot(s_hi, ones_tl, NN) + dot(s_lo, ones_tl, NN) + carry_w * ch['qn']
            num = dot(s_hi, ch['v'].astype(BF16), NN) + carry_w * ch['qc']
            h_s[b, d, ch['rows'], h * HEAD_B:(h + 1) * HEAD_B] = num / jnp.maximum(jnp.abs(den), jnp.exp(-m_t))
            g = bl - bc + ic
            m_new = jnp.maximum(bl + m_prev, jnp.max(g, axis=0, keepdims=True))
            wk = jnp.exp(g - m_new)
            decay = jnp.exp(bl + m_prev - m_new)
            c_s[b, d, h] = decay * ch['c_prev'] + _dot_bf16(ch['k'], wk * ch['v'], TN)
            n_s[b, d, h:h + 1, :] = decay * ch['n_prev'] + jnp.sum(wk * ch['k'], axis=0, keepdims=True)
            m_s[b, d, h:h + 1, :] = m_new
            if q % 2 == 1:
                yield

    def norm_chunk(c, carry):
        rows = pl.ds(pl.multiple_of(c * CHUNK, CHUNK), CHUNK)
        for b in range(group):
            og = zm_ref[b, rows, 3 * DB:4 * DB]
            hm = h_s[b, 0, rows, :] + h_s[b, 1, rows, :]
            parts = []
            for h in range(H_B):
                hh = hm[:, h * HEAD_B:(h + 1) * HEAD_B]
                parts.append(hh * lax.rsqrt(jnp.mean(hh * hh, axis=-1, keepdims=True) + NORM_EPS))
            y_ref[b, rows, :] = jnp.concatenate(parts, axis=1) * ng_ref[...] * _sigmoid(og)
        return carry

    def finish():
        cout_ref[...] = c_s[...]
        nout_ref[...] = n_s[...]
        mout_ref[...] = m_s[...]
        lax.fori_loop(0, nc, norm_chunk, 0)

    return stages, finish


def _scan_kernel(*refs, body, n_steps, **static):
    stages, finish = body(*refs, **static)

    def scan_step(i, carry):
        for _ in stages(i):
            pass
        return carry

    lax.fori_loop(0, n_steps, scan_step, 0)
    finish()


def _scan_specs(grp):
    lead = lambda *rest: pl.BlockSpec((grp,) + rest, lambda b: (b,) + (0,) * len(rest))

    def big(seq_len, cols):
        if 2 * grp * seq_len * cols * 4 <= SCAN_DOUBLE_BUFFER_BYTES:
            return lead(seq_len, cols)
        return pl.BlockSpec((grp, seq_len, cols), lambda b: (b, 0, 0), pipeline_mode=pl.Buffered(1))

    return lead, big


def _rwkv(zr, s0, p, grid):
    bsz, seq_len, _ = zr.shape
    nc = seq_len // CHUNK
    grp = 2 * SCAN_GROUP if 4 * SCAN_GROUP * seq_len * RWKV_COLS * 4 <= SCAN_DOUBLE_BUFFER_BYTES else SCAN_GROUP
    assert nc % 2 == 0 and bsz % grp == 0
    kern = functools.partial(_scan_kernel, body=_rwkv_body, n_steps=nc, seq_len=seq_len, grid=grid, group=grp)
    consts = [p['mu'], p['kkw'], p['ka'], p['rk'], p['gnw'], p['gnb'], p['w0'], p['w2'], p['a0'], p['a2'],
              p['g2'], p['ones_bd']]
    lead, big = _scan_specs(grp)
    return pl.pallas_call(
        kern,
        grid=(bsz // grp,),
        in_specs=[big(seq_len, RWKV_COLS), lead(2, H_A, HEAD_A, HEAD_A)] + [_const_spec(a.shape) for a in consts],
        out_specs=[lead(seq_len, DA), lead(2, H_A, HEAD_A, HEAD_A)],
        out_shape=[jax.ShapeDtypeStruct((bsz, seq_len, DA), F32),
                   jax.ShapeDtypeStruct((bsz, 2, H_A, HEAD_A, HEAD_A), F32)],
        scratch_shapes=[pltpu.VMEM((grp, seq_len, DA), F32)] * 5
                       + [pltpu.VMEM((grp, seq_len, DECAY_LORA + AAA_LORA), F32),
                          pltpu.VMEM((grp, 2, N_PACKS, PACK_W, PACK_W), F32)],
        compiler_params=_params("arbitrary"),
        name="rwkv_scan",
    )(zr, s0, *consts)


def _mlstm(zm, g, gt, c0, n0, m0, p):
    bsz, seq_len, _ = zm.shape
    nc = seq_len // CHUNK
    grp = SCAN_GROUP
    assert bsz % grp == 0
    kern = functools.partial(_scan_kernel, body=_mlstm_body, n_steps=nc, seq_len=seq_len, group=grp)
    consts = [p['conv'], p['bi'], p['bit'], p['bf'], p['bft'], p['ng'], p['e_rep']]
    lead, big = _scan_specs(grp)
    return pl.pallas_call(
        kern,
        grid=(bsz // grp,),
        in_specs=[big(seq_len, MLSTM_MAIN), lead(seq_len, N_GATES), lead(nc, N_GATES, CHUNK),
                  lead(2, H_B, HEAD_B, HEAD_B), lead(2, H_B, HEAD_B), lead(2, H_B, 1)]
                 + [_const_spec(a.shape) for a in consts],
        out_specs=[lead(seq_len, DB), lead(2, H_B, HEAD_B, HEAD_B), lead(2, H_B, HEAD_B), lead(2, H_B, LANES)],
        out_shape=[jax.ShapeDtypeStruct((bsz, seq_len, DB), F32),
                   jax.ShapeDtypeStruct((bsz, 2, H_B, HEAD_B, HEAD_B), F32),
                   jax.ShapeDtypeStruct((bsz, 2, H_B, HEAD_B), F32),
                   jax.ShapeDtypeStruct((bsz, 2, H_B, LANES), F32)],
        scratch_shapes=[pltpu.VMEM((grp, seq_len, 2 * DB), F32), pltpu.VMEM((grp, 2, seq_len, DB), F32),
                        pltpu.VMEM((grp, 2, H_B, HEAD_B, HEAD_B), F32), pltpu.VMEM((grp, 2, H_B, HEAD_B), F32),
                        pltpu.VMEM((grp, 2, H_B, LANES), F32)],
        compiler_params=_params("arbitrary"),
        name="mlstm_scan",
    )(zm, g, gt, c0, n0, m0, *consts)


def _outproj_kernel(x_ref, yr_ref, ym_ref, mod_ref, g_ref, wo_ref, rw_ref, rb_ref,
                    x1_ref, hn_ref, idx_ref, gate_ref):
    mod = mod_ref[0]
    g1 = mod[:, 2 * D_MODEL:3 * D_MODEL]
    sh2, sc2 = mod[:, 3 * D_MODEL:4 * D_MODEL], mod[:, 4 * D_MODEL:5 * D_MODEL]
    mix = _dot_bf16(yr_ref[...], wo_ref[0:DA, :]) + _dot_bf16(ym_ref[...], wo_ref[DA:, :])
    x1 = x_ref[...] + g1 * mix
    x1_ref[...] = x1
    y = x1 * lax.rsqrt(jnp.mean(x1 * x1, axis=-1, keepdims=True) + NORM_EPS)
    hn = y * g_ref[...] * (1.0 + sc2) + sh2
    hn_ref[...] = hn
    logits = _dot_bf16(hn, rw_ref[...]) + rb_ref[...]
    lane = lax.broadcasted_iota(jnp.int32, logits.shape, 1)
    vals, idxs = [], []
    for _ in range(TOP_K):
        top = jnp.max(logits, axis=-1, keepdims=True)
        pick = jnp.min(jnp.where(logits == top, lane, N_EXPERTS), axis=-1, keepdims=True)
        vals.append(top)
        idxs.append(pick)
        logits = jnp.where(lane == pick, -jnp.inf, logits)
    exps = [jnp.exp(v - vals[0]) for v in vals]
    total = exps[0] + exps[1] + exps[2] + exps[3]
    k_lane = lax.broadcasted_iota(jnp.int32, (x1.shape[0], TOP_K), 1)
    gates = jnp.zeros((x1.shape[0], TOP_K), F32)
    picks = jnp.zeros((x1.shape[0], TOP_K), jnp.int32)
    for j in range(TOP_K):
        gates = jnp.where(k_lane == j, exps[j] / total, gates)
        picks = jnp.where(k_lane == j, idxs[j], picks)
    idx_ref[...] = picks
    gate_ref[...] = gates


def _outproj(x2, yr, ym, mod, norm_g, w_out, router_w, router_b):
    rows = x2.shape[0]
    tiles_per_mod = rows // mod.shape[0] // PROJ_TILE
    row_spec = lambda w: pl.BlockSpec((PROJ_TILE, w), lambda i: (i, 0))
    return pl.pallas_call(
        _outproj_kernel,
        grid=(rows // PROJ_TILE,),
        in_specs=[row_spec(D_MODEL), row_spec(DA), row_spec(DB),
                  pl.BlockSpec((1, 1, 6 * D_MODEL), lambda i: (i // tiles_per_mod, 0, 0)),
                  _const_spec((1, D_MODEL)), _const_spec(w_out.shape), _const_spec(router_w.shape),
                  _const_spec((1, N_EXPERTS))],
        out_specs=[row_spec(D_MODEL), row_spec(D_MODEL), row_spec(TOP_K), row_spec(TOP_K)],
        out_shape=[jax.ShapeDtypeStruct((rows, D_MODEL), F32), jax.ShapeDtypeStruct((rows, D_MODEL), F32),
                   jax.ShapeDtypeStruct((rows, TOP_K), jnp.int32), jax.ShapeDtypeStruct((rows, TOP_K), F32)],
        compiler_params=_params("arbitrary"),
        name="outproj_router",
    )(x2, yr, ym, mod, norm_g.reshape(1, D_MODEL), w_out, router_w, router_b.reshape(1, N_EXPERTS))


def _rank_kernel(idx_ref, rank_ref, count_ref, run_s):
    i = pl.program_id(0)

    @pl.when(i == 0)
    def _():
        run_s[...] = jnp.zeros_like(run_s)

    idx = idx_ref[...]
    rows = idx.shape[0]
    lane = lax.broadcasted_iota(jnp.int32, (rows, N_EXPERTS), 1)
    hot = jnp.zeros((rows, N_EXPERTS), F32)
    for j in range(TOP_K):
        hot = hot + jnp.where(lane == idx[:, j:j + 1], 1.0, 0.0)
    before = _dot_bf16(_tri(rows, False, True).astype(F32), hot) + run_s[...]
    k_lane = lax.broadcasted_iota(jnp.int32, (rows, TOP_K), 1)
    rank = jnp.zeros((rows, TOP_K), F32)
    for j in range(TOP_K):
        rj = jnp.sum(jnp.where(lane == idx[:, j:j + 1], before, 0.0), axis=-1, keepdims=True)
        rank = jnp.where(k_lane == j, rj, rank)
    rank_ref[...] = rank.astype(jnp.int32)
    run_s[...] = run_s[...] + jnp.sum(hot, axis=0, keepdims=True)
    count_ref[...] = run_s[...].astype(jnp.int32)


def _expert_ranks(idx):
    rows = idx.shape[0]
    return pl.pallas_call(
        _rank_kernel,
        grid=(rows // ROW_TILE,),
        in_specs=[pl.BlockSpec((ROW_TILE, TOP_K), lambda i: (i, 0))],
        out_specs=[pl.BlockSpec((ROW_TILE, TOP_K), lambda i: (i, 0)), _const_spec((1, N_EXPERTS))],
        out_shape=[jax.ShapeDtypeStruct((rows, TOP_K), jnp.int32),
                   jax.ShapeDtypeStruct((1, N_EXPERTS), jnp.int32)],
        scratch_shapes=[pltpu.VMEM((1, N_EXPERTS), F32)],
        compiler_params=_params("arbitrary"),
        name="expert_ranks",
    )(idx)


def _dispatch_kernel(dest_ref, pend_ref, xa_ref, xb_ref, xs_ref, zero_s, sem, *, n_first_tiles):
    @pl.when(pl.program_id(0) == 0)
    def _():
        zero_s[...] = jnp.zeros_like(zero_s)

        def zero_block(start):
            return pltpu.make_async_copy(zero_s, xs_ref.at[pl.ds(pl.multiple_of(start, MOE_BLOCK), MOE_BLOCK)], sem)

        def has_rows(e):
            return pend_ref[e] > (pend_ref[e - 1] if e else 0)

        used_end = pend_ref[N_EXPERTS - 1]
        n_tail = (xs_ref.shape[0] - used_end) // MOE_BLOCK

        def start_tail(b, carry):
            zero_block(used_end + b * MOE_BLOCK).start()
            return carry

        def wait_tail(b, carry):
            zero_block(0).wait()
            return carry

        for e in range(N_EXPERTS):
            @pl.when(has_rows(e))
            def _(e=e):
                zero_block(pend_ref[e] - MOE_BLOCK).start()
        lax.fori_loop(0, n_tail, start_tail, 0)
        for e in range(N_EXPERTS):
            @pl.when(has_rows(e))
            def _():
                zero_block(0).wait()
        lax.fori_loop(0, n_tail, wait_tail, 0)

    def issue_from(x_ref):
        def issue(g, carry):
            for q in range(SUBLANES):
                for j in range(TOP_K):
                    slot = dest_ref[g * (SUBLANES * TOP_K) + q * TOP_K + j]
                    pltpu.make_async_copy(x_ref.at[g, pl.ds(q, 1)], xs_ref.at[pl.ds(slot, 1)],
                                          sem).start(priority=j % 2)
            return carry
        lax.fori_loop(0, ROW_TILE // SUBLANES, issue, 0)

    pl.when(pl.program_id(0) < n_first_tiles)(lambda: issue_from(xa_ref))
    pl.when(pl.program_id(0) >= n_first_tiles)(lambda: issue_from(xb_ref))
    all_rows = xs_ref.at[pl.ds(0, ROW_TILE * TOP_K)]
    pltpu.make_async_copy(all_rows, all_rows, sem).wait()


def _dispatch(dest_flat, pad_end, xa, xb, n_slots):
    na, nb = xa.shape[0] // ROW_TILE, xb.shape[0] // ROW_TILE
    tiles = lambda x: x.reshape(x.shape[0] // SUBLANES, SUBLANES, D_MODEL)
    block = (ROW_TILE // SUBLANES, SUBLANES, D_MODEL)
    return pl.pallas_call(
        functools.partial(_dispatch_kernel, n_first_tiles=na),
        grid=(na + nb,),
        in_specs=[pl.BlockSpec((ROW_TILE * TOP_K,), lambda i: (i,), memory_space=pltpu.SMEM),
                  pl.BlockSpec(memory_space=pltpu.SMEM),
                  pl.BlockSpec(block, lambda i: (jnp.minimum(i, na - 1), 0, 0)),
                  pl.BlockSpec(block, lambda i: (jnp.maximum(i - na, 0), 0, 0))],
        out_specs=pl.BlockSpec(memory_space=pl.ANY),
        out_shape=jax.ShapeDtypeStruct((n_slots, D_MODEL), xa.dtype),
        scratch_shapes=[pltpu.VMEM((MOE_BLOCK, D_MODEL), F32), pltpu.SemaphoreType.DMA(())],
        compiler_params=_params("arbitrary", disable_bounds_checks=True),
        name="moe_dispatch",
    )(dest_flat, pad_end, tiles(xa), tiles(xb))


def _expert_kernel(be_ref, nb_ref, next_ref, par_ref, xs_ref, w1_hbm, b1_ref, w2_hbm, b2_ref, y_ref,
                   w1_f, w2_f, w1_s, w2_s, sem1, sem2):
    i = pl.program_id(0)
    used = i < nb_ref[0]
    changed = jnp.logical_or(i == 0, be_ref[i] != be_ref[jnp.maximum(i - 1, 0)])

    def fetch(e, slot):
        return (pltpu.make_async_copy(w1_hbm.at[e], w1_f.at[slot], sem1.at[slot]),
                pltpu.make_async_copy(w2_hbm.at[e], w2_f.at[slot], sem2.at[slot]))

    @pl.when(i == 0)
    def _():
        for copy in fetch(be_ref[0], 0):
            copy.start()

    @pl.when(jnp.logical_and(changed, used))
    def _():
        slot = par_ref[i]
        for copy in fetch(be_ref[i], slot):
            copy.wait()

        @pl.when(next_ref[i] != be_ref[i])
        def _():
            for copy in fetch(next_ref[i], 1 - slot):
                copy.start()

        w1_s[...] = w1_f[slot].astype(BF16)
        w2_s[...] = w2_f[slot].astype(BF16)

    @pl.when(used)
    def _():
        xb = xs_ref[...].astype(BF16)
        hu = lax.dot_general(xb, w1_s[...], NN, preferred_element_type=F32) + b1_ref[0]
        glu = jnp.minimum(hu[:, :D_FF], SWIGLU_LIMIT)
        lin = jnp.clip(hu[:, D_FF:], -SWIGLU_LIMIT, SWIGLU_LIMIT)
        act = glu * _sigmoid(SWIGLU_ALPHA * glu) * (lin + 1.0)
        y_ref[...] = lax.dot_general(act.astype(BF16), w2_s[...], NN, preferred_element_type=F32) + b2_ref[0]

    @pl.when(i >= nb_ref[0])
    def _():
        y_ref[...] = jnp.zeros_like(y_ref)


def _experts(block_e, n_used, next_e, parity, xs, w1, b1, w2, b2):
    n_blocks = xs.shape[0] // MOE_BLOCK
    grid_spec = pltpu.PrefetchScalarGridSpec(
        num_scalar_prefetch=4,
        grid=(n_blocks,),
        in_specs=[pl.BlockSpec((MOE_BLOCK, D_MODEL), lambda i, be, nb, nx, pr: (jnp.minimum(i, nb[0] - 1), 0)),
                  pl.BlockSpec(memory_space=pl.ANY),
                  pl.BlockSpec((1, 1, 2 * D_FF), lambda i, be, nb, nx, pr: (be[i], 0, 0)),
                  pl.BlockSpec(memory_space=pl.ANY),
                  pl.BlockSpec((1, 1, D_MODEL), lambda i, be, nb, nx, pr: (be[i], 0, 0))],
        out_specs=pl.BlockSpec((MOE_BLOCK, D_MODEL), lambda i, be, nb, nx, pr: (i, 0)),
        scratch_shapes=[pltpu.VMEM((2, D_MODEL, 2 * D_FF), F32), pltpu.VMEM((2, D_FF, D_MODEL), F32),
                        pltpu.VMEM((D_MODEL, 2 * D_FF), BF16), pltpu.VMEM((D_FF, D_MODEL), BF16),
                        pltpu.SemaphoreType.DMA((2,)), pltpu.SemaphoreType.DMA((2,))],
    )
    return pl.pallas_call(
        _expert_kernel,
        grid_spec=grid_spec,
        out_shape=jax.ShapeDtypeStruct(xs.shape, F32),
        compiler_params=_params("arbitrary"),
        name="moe_experts",
    )(block_e, n_used, next_e, parity, xs, w1, b1.reshape(N_EXPERTS, 1, 2 * D_FF), w2,
      b2.reshape(N_EXPERTS, 1, D_MODEL))


def _combine_kernel(dest_ref, yb_ref, x1_ref, gate_ref, mod_ref, fg_ref, o_ref, buf, sem):
    def issue(g, carry):
        for q in range(SUBLANES):
            for j in range(TOP_K):
                slot = dest_ref[g * (SUBLANES * TOP_K) + q * TOP_K + j]
                pltpu.make_async_copy(yb_ref.at[pl.ds(slot, 1)], buf.at[g, j, pl.ds(q, 1)], sem).start(priority=j % 2)
        return carry

    lax.fori_loop(0, ROW_TILE // SUBLANES, issue, 0)
    pltpu.make_async_copy(buf, buf, sem).wait()

    g2 = mod_ref[0][:, 5 * D_MODEL:6 * D_MODEL]
    gates = gate_ref[...]
    moe = jnp.zeros((ROW_TILE, D_MODEL), F32)
    for j in range(TOP_K):
        moe = moe + buf[:, j].reshape(ROW_TILE, D_MODEL) * gates[:, j:j + 1]
    x2 = x1_ref[...] + g2 * moe
    y = x2 * lax.rsqrt(jnp.mean(x2 * x2, axis=-1, keepdims=True) + NORM_EPS)
    o_ref[...] = y * fg_ref[...]


def _combine(dest_flat, yb, x1, gates, mod, final_g, row_offset):
    rows = x1.shape[0]
    tiles_per_mod = rows // mod.shape[0] // ROW_TILE
    tile_offset = row_offset // ROW_TILE
    row_spec = pl.BlockSpec((ROW_TILE, D_MODEL), lambda i: (i, 0))
    return pl.pallas_call(
        _combine_kernel,
        grid=(rows // ROW_TILE,),
        in_specs=[pl.BlockSpec((ROW_TILE * TOP_K,), lambda i: (i + tile_offset,), memory_space=pltpu.SMEM),
                  pl.BlockSpec(memory_space=pl.ANY),
                  row_spec,
                  pl.BlockSpec((ROW_TILE, TOP_K), lambda i: (i, 0)),
                  pl.BlockSpec((1, 1, 6 * D_MODEL), lambda i: (i // tiles_per_mod, 0, 0)),
                  _const_spec((1, D_MODEL))],
        out_specs=row_spec,
        out_shape=jax.ShapeDtypeStruct((rows, D_MODEL), F32),
        scratch_shapes=[pltpu.VMEM((ROW_TILE // SUBLANES, TOP_K, SUBLANES, D_MODEL), F32),
                        pltpu.SemaphoreType.DMA(())],
        compiler_params=_params("arbitrary", disable_bounds_checks=True),
        name="moe_combine",
    )(dest_flat, yb, x1, gates, mod, final_g.reshape(1, D_MODEL))


def _block_diag_ones(width, block):
    i = jnp.arange(width) // block
    return (i[:, None] == i[None, :]).astype(F32)


def _mixers(x, mod, lp, s0, c0, n0, m0, grid):
    bsz, seq_len, _ = x.shape
    nc = seq_len // CHUNK
    x2 = x.reshape(bsz * seq_len, D_MODEL)
    zr, zm, zg, zgt = _inproj(x2, mod, lp['norm1_g'], lp['w_r'], lp['w_m'], lp['w_g'], lp['w_gt'])
    gt = zgt.reshape(N_GATES, bsz, nc, CHUNK).transpose(1, 2, 0, 3)
    yr, s_new = _rwkv(zr.reshape(bsz, seq_len, RWKV_COLS), s0, lp, grid)
    ym, c_new, n_new, m_new = _mlstm(zm.reshape(bsz, seq_len, MLSTM_MAIN), zg.reshape(bsz, seq_len, N_GATES), gt,
                                     c0, n0, m0.reshape(bsz, 2, H_B, 1), lp)
    x1, hn, idx, gates = _outproj(x2, yr.reshape(-1, DA), ym.reshape(-1, DB), mod, lp['norm2_g'], lp['w_out'],
                                  lp['router_w'], lp['router_b'])
    return x1, hn, idx, gates, (s_new, c_new, n_new, m_new[..., 0])


def kernel(x_prompt, x_sample, state_rwkv, state_mlstm_C, state_mlstm_n, state_mlstm_m, c, c_ctx, ada_w, ada_b, norm1_g, norm2_g, w_in, w_out, rwkv_mu, rwkv_w0, rwkv_w2, rwkv_a0, rwkv_a2, rwkv_g2, rwkv_kk, rwkv_ka, rwkv_rk, rwkv_gn_w, rwkv_gn_b, mlstm_conv, mlstm_bi, mlstm_bf, mlstm_norm_g, router_w, router_b, moe_w1, moe_b1, moe_w2, moe_b2, final_g):
    bp, lp_len, _ = x_prompt.shape
    bs, ls_len, _ = x_sample.shape
    w = w_in[0]
    lp = {
        'norm1_g': norm1_g[0], 'norm2_g': norm2_g[0],
        'w_r': w[:, :RWKV_COLS].astype(BF16),
        'w_m': w[:, RWKV_COLS:RWKV_COLS + MLSTM_MAIN].astype(BF16),
        'w_g': w[:, RWKV_COLS + MLSTM_MAIN:],
        'w_gt': w[:, RWKV_COLS + MLSTM_MAIN:].T,
        'w_out': w_out[0].astype(BF16),
        'mu': rwkv_mu[0].reshape(1, RWKV_COLS), 'kkw': rwkv_kk[0].reshape(1, DA), 'ka': rwkv_ka[0].reshape(1, DA),
        'rk': rwkv_rk[0].reshape(1, DA), 'gnw': rwkv_gn_w[0].reshape(1, DA), 'gnb': rwkv_gn_b[0].reshape(1, DA),
        'w0': rwkv_w0[0], 'w2': rwkv_w2[0], 'a0': rwkv_a0[0], 'a2': rwkv_a2[0], 'g2': rwkv_g2[0],
        'ones_bd': _block_diag_ones(PACK_W, HEAD_A).astype(BF16),
        'conv': mlstm_conv[0], 'bi': mlstm_bi[0].reshape(1, 2 * H_B), 'bit': mlstm_bi[0].reshape(2 * H_B, 1),
        'bf': mlstm_bf[0].reshape(1, 2 * H_B), 'bft': mlstm_bf[0].reshape(2 * H_B, 1),
        'ng': mlstm_norm_g[0].reshape(1, DB),
        'e_rep': jnp.repeat(jnp.eye(N_GATES, dtype=BF16), LANES, axis=1),
        'router_w': router_w[0], 'router_b': router_b[0],
    }
    cc = jnp.concatenate([c_ctx[None, :], c, jnp.zeros((2 * SUBLANES - 1 - bs, D_MODEL), F32)], axis=0)
    mod = _ada_mod(cc, ada_w[0], ada_b[0])
    mod_p = mod[0:1].reshape(1, 1, 6 * D_MODEL)
    mod_s = mod[1:1 + bs].reshape(bs, 1, 6 * D_MODEL)

    zeros = lambda *shape: jnp.zeros(shape, F32)
    x1p, hnp, idxp, gatesp, st = _mixers(x_prompt, mod_p, lp, zeros(bp, 2, H_A, HEAD_A, HEAD_A),
                                         zeros(bp, 2, H_B, HEAD_B, HEAD_B), zeros(bp, 2, H_B, HEAD_B),
                                         zeros(bp, 2, H_B), False)
    x1s, hns, idxs, gatess, _ = _mixers(x_sample, mod_s, lp, state_rwkv[:, 0], state_mlstm_C[:, 0],
                                        state_mlstm_n[:, 0], state_mlstm_m[:, 0], True)

    n_p = bp * lp_len
    idx = jnp.concatenate([idxp, idxs], axis=0)
    n_tok = idx.shape[0]
    rank, counts = _expert_ranks(idx)
    counts = counts[0]
    padded = (counts + MOE_BLOCK - 1) // MOE_BLOCK * MOE_BLOCK
    pad_end = jnp.cumsum(padded)
    pad_start = pad_end - padded
    n_blocks = n_tok * TOP_K // MOE_BLOCK + N_EXPERTS
    block_start = jnp.arange(n_blocks, dtype=jnp.int32) * MOE_BLOCK
    block_e = jnp.minimum(jnp.sum(pad_end[None, :] <= block_start[:, None], axis=1), N_EXPERTS - 1).astype(jnp.int32)
    n_used = (pad_end[-1:] // MOE_BLOCK).astype(jnp.int32)
    expert_ids = jnp.arange(N_EXPERTS, dtype=jnp.int32)
    start_of = jnp.sum(jnp.where(idx[:, :, None] == expert_ids, pad_start.astype(jnp.int32), 0), axis=-1)
    dest = (start_of + rank).astype(jnp.int32).reshape(-1)
    xs = _dispatch(dest, pad_end.astype(jnp.int32), hnp, hns, n_blocks * MOE_BLOCK)
    has_rows = counts > 0
    later = jnp.where(has_rows[None, :] & (expert_ids[None, :] > expert_ids[:, None]), expert_ids[None, :], N_EXPERTS)
    next_with_rows = jnp.min(later, axis=1)
    next_of = jnp.where(next_with_rows < N_EXPERTS, next_with_rows, expert_ids)
    place = jnp.cumsum(has_rows.astype(jnp.int32)) - 1
    yb = _experts(block_e, n_used, next_of[block_e].astype(jnp.int32), (place[block_e] % 2).astype(jnp.int32), xs,
                  moe_w1[0], moe_b1[0], moe_w2[0], moe_b2[0])
    y_prompt = _combine(dest, yb, x1p, gatesp, mod_p, final_g, 0).reshape(x_prompt.shape)
    y_sample = _combine(dest, yb, x1s, gatess, mod_s, final_g, n_p).reshape(x_sample.shape)

    s_new, c_new, n_new, m_new = st
    return (y_prompt, y_sample, s_new[:, None], c_new[:, None], n_new[:, None], m_new[:, None])
```

```python
import functools

import jax
import jax.numpy as jnp
from jax import lax
from jax.experimental import pallas as pl
from jax.experimental.pallas import tpu as pltpu

F32 = jnp.float32
BF16 = jnp.bfloat16
HIGHEST = lax.Precision.HIGHEST

D_MODEL = 1024
DA = 512
HEAD_A = 64
H_A = DA // HEAD_A
DB = 512
H_B = 4
HEAD_B = DB // H_B
DECAY_LORA = 64
AAA_LORA = 64
GATE_LORA = 128
RWKV_COLS = 3 * DA + DECAY_LORA + AAA_LORA + GATE_LORA
MLSTM_MAIN = 4 * DB
N_GATES = 4 * H_B
GRID_W = 64
CHUNK = 64
N_EXPERTS = 32
TOP_K = 4
D_FF = D_MODEL
SWIGLU_LIMIT = 7.0
SWIGLU_ALPHA = 1.702
MOE_BLOCK = 512
NORM_EPS = 1e-6
GN_EPS = 64e-5
ROW_TILE = 1024
PROJ_TILE = 512
SUBLANES = 8
LANES = 128
LANE_ROWS = D_MODEL // LANES
VMEM_LIMIT = 56 * 1024 * 1024
PACK = 4
PACK_W = PACK * HEAD_A
N_PACKS = H_A // PACK
SCAN_GROUP = 2
SCAN_DOUBLE_BUFFER_BYTES = 16 * 1024 * 1024

NN = (((1,), (0,)), ((), ()))
NT = (((1,), (1,)), ((), ()))
TN = (((0,), (0,)), ((), ()))


def _mm(a, b, precision=HIGHEST):
    return lax.dot_general(a, b, NN, precision=precision, preferred_element_type=F32)


def _mm_nt(a, b, precision=HIGHEST):
    return lax.dot_general(a, b, NT, precision=precision, preferred_element_type=F32)


def _mm_tn(a, b, precision=HIGHEST):
    return lax.dot_general(a, b, TN, precision=precision, preferred_element_type=F32)


def _dot_bf16(a, b, dims=NN):
    return lax.dot_general(a.astype(BF16), b.astype(BF16), dims, preferred_element_type=F32)


def _sigmoid(x):
    return 1.0 / (1.0 + jnp.exp(-x))


def _log_sigmoid(x):
    return jnp.minimum(x, 0.0) - jnp.log(1.0 + jnp.exp(-jnp.abs(x)))


def _params(*sem, **kw):
    return pltpu.CompilerParams(dimension_semantics=sem, vmem_limit_bytes=VMEM_LIMIT, **kw)


def _split_bf16(x):
    hi = x.astype(BF16)
    return hi, (x - hi.astype(F32)).astype(BF16)


def _const_spec(shape):
    nd = len(shape)
    return pl.BlockSpec(shape, lambda *_: (0,) * nd)


def _tri(n, reverse, strict):
    t = lax.broadcasted_iota(jnp.int32, (n, n), 0)
    s = lax.broadcasted_iota(jnp.int32, (n, n), 1)
    if reverse:
        return (s > t) if strict else (s >= t)
    return (s < t) if strict else (s <= t)


def _seq_neighbours(ref, b, c, nc, cols):
    r0 = pl.multiple_of(c * CHUNK, CHUNK)
    zc = ref[b, pl.ds(r0, CHUNK), cols]
    row = lax.broadcasted_iota(jnp.int32, zc.shape, 0)
    p0 = pl.multiple_of(jnp.maximum(r0 - SUBLANES, 0), SUBLANES)
    n0 = pl.multiple_of(jnp.minimum(r0 + CHUNK, (nc - 1) * CHUNK), SUBLANES)
    before = ref[b, pl.ds(p0, SUBLANES), cols][SUBLANES - 1:SUBLANES]
    after = ref[b, pl.ds(n0, SUBLANES), cols][0:1]
    before = jnp.where(c > 0, before, 0.0)
    after = jnp.where(c < nc - 1, after, 0.0)
    prev = jnp.where(row == 0, before, pltpu.roll(zc, 1, 0))
    nxt = jnp.where(row == CHUNK - 1, after, pltpu.roll(zc, CHUNK - 1, 0))
    return zc, prev, nxt


def _ada_kernel(c_ref, w_ref, b_ref, o_ref):
    cc = c_ref[...]
    o_ref[...] = _dot_bf16(cc * _sigmoid(cc), w_ref[...]) + b_ref[...]


def _ada_mod(cc, ada_w, ada_b):
    rows = cc.shape[0]
    ncol = ada_w.shape[1]
    tn = 1536
    return pl.pallas_call(
        _ada_kernel,
        grid=(ncol // tn,),
        in_specs=[_const_spec((rows, D_MODEL)),
                  pl.BlockSpec((D_MODEL, tn), lambda j: (0, j)),
                  pl.BlockSpec((1, tn), lambda j: (0, j))],
        out_specs=pl.BlockSpec((rows, tn), lambda j: (0, j)),
        out_shape=jax.ShapeDtypeStruct((rows, ncol), F32),
        compiler_params=_params("arbitrary"),
        name="ada_mod",
    )(cc, ada_w, ada_b.reshape(1, ncol))


def _inproj_kernel(x_ref, mod_ref, g_ref, wr_ref, wm_ref, wg_ref, wgt_ref, zr_ref, zm_ref, zg_ref, zgt_ref):
    x = x_ref[...]
    mod = mod_ref[0]
    sh, sc = mod[:, 0:D_MODEL], mod[:, D_MODEL:2 * D_MODEL]
    y = x * lax.rsqrt(jnp.mean(x * x, axis=-1, keepdims=True) + NORM_EPS)
    hn = y * g_ref[...] * (1.0 + sc) + sh
    hb = hn.astype(BF16)
    zr_ref[...] = lax.dot_general(hb, wr_ref[...], NN, preferred_element_type=F32)
    zm_ref[...] = lax.dot_general(hb, wm_ref[...], NN, preferred_element_type=F32)
    zg_ref[...] = _dot_bf16(hb, wg_ref[...])
    zgt_ref[...] = _dot_bf16(wgt_ref[...], hb, NT)


def _inproj(x2, mod, norm_g, w_r, w_m, w_g, w_gt):
    rows = x2.shape[0]
    tiles_per_mod = rows // mod.shape[0] // PROJ_TILE
    return pl.pallas_call(
        _inproj_kernel,
        grid=(rows // PROJ_TILE,),
        in_specs=[pl.BlockSpec((PROJ_TILE, D_MODEL), lambda i: (i, 0)),
                  pl.BlockSpec((1, 1, 6 * D_MODEL), lambda i: (i // tiles_per_mod, 0, 0)),
                  _const_spec((1, D_MODEL)),
                  _const_spec(w_r.shape), _const_spec(w_m.shape), _const_spec(w_g.shape),
                  _const_spec(w_gt.shape)],
        out_specs=[pl.BlockSpec((PROJ_TILE, RWKV_COLS), lambda i: (i, 0)),
                   pl.BlockSpec((PROJ_TILE, MLSTM_MAIN), lambda i: (i, 0)),
                   pl.BlockSpec((PROJ_TILE, N_GATES), lambda i: (i, 0)),
                   pl.BlockSpec((N_GATES, PROJ_TILE), lambda i: (0, i))],
        out_shape=[jax.ShapeDtypeStruct((rows, RWKV_COLS), F32),
                   jax.ShapeDtypeStruct((rows, MLSTM_MAIN), F32),
                   jax.ShapeDtypeStruct((rows, N_GATES), F32),
                   jax.ShapeDtypeStruct((N_GATES, rows), F32)],
        compiler_params=_params("arbitrary"),
        name="inproj",
    )(x2, mod, norm_g.reshape(1, D_MODEL), w_r, w_m, w_g, w_gt)


def _group_sum(x, ones_bd):
    hi = x.astype(BF16)
    lo = (x - hi.astype(F32)).astype(BF16)
    return (lax.dot_general(hi, ones_bd, NN, preferred_element_type=F32)
            + lax.dot_general(lo, ones_bd, NN, preferred_element_type=F32))


def _block_diag(x, ones_bd):
    return jnp.concatenate([x.astype(BF16)] * PACK, axis=0) * ones_bd


def _rwkv_body(zr_ref, s0_ref, mu_ref, kkw_ref, ka_ref, rk_ref, gnw_ref, gnb_ref, w0_ref, w2_ref,
                 a0_ref, a2_ref, g2_ref, ones_ref, y_ref, sout_ref,
                 r_s, k_s, v_s, kk_s, gate_s, lx_s, st_s, *, seq_len, grid, group):
    nc = seq_len // CHUNK
    ones_bd = ones_ref[...]

    def mix_chunk(c, carry):
        r0 = pl.multiple_of(c * CHUNK, CHUNK)
        for b in range(group):
            if grid:
                zc = zr_ref[b, pl.ds(r0, CHUNK), :]
                row = lax.broadcasted_iota(jnp.int32, zc.shape, 0)
                up0 = pl.multiple_of(jnp.maximum(c - 1, 0) * CHUNK, CHUNK)
                dn0 = pl.multiple_of(jnp.minimum(c + 1, nc - 1) * CHUNK, CHUNK)
                up = jnp.where(c > 0, zr_ref[b, pl.ds(up0, CHUNK), :], 0.0)
                down = jnp.where(c < nc - 1, zr_ref[b, pl.ds(dn0, CHUNK), :], 0.0)
                left = jnp.where(row == 0, 0.0, pltpu.roll(zc, 1, 0))
                right = jnp.where(row == CHUNK - 1, 0.0, pltpu.roll(zc, CHUNK - 1, 0))
                local = 0.25 * (up + down + left + right)
            else:
                zc, prev, nxt = _seq_neighbours(zr_ref, b, c, nc, slice(None))
                local = 0.5 * (prev + nxt)
            z = zc + (local - zc) * mu_ref[...]
            r, k, v = z[:, 0:DA], z[:, DA:2 * DA], z[:, 2 * DA:3 * DA]
            lx = z[:, 3 * DA:3 * DA + DECAY_LORA + AAA_LORA]
            xg = z[:, 3 * DA + DECAY_LORA + AAA_LORA:]
            kk = k * kkw_ref[...]
            sq = kk * kk
            ss = jnp.concatenate([_group_sum(sq[:, p * PACK_W:(p + 1) * PACK_W], ones_bd) for p in range(N_PACKS)],
                                 axis=1)
            kk = kk / jnp.maximum(jnp.sqrt(ss), 1e-12)
            lane = lax.broadcasted_iota(jnp.int32, lx.shape, 1)
            r_s[b, pl.ds(r0, CHUNK), :] = r
            k_s[b, pl.ds(r0, CHUNK), :] = k
            v_s[b, pl.ds(r0, CHUNK), :] = v
            kk_s[b, pl.ds(r0, CHUNK), :] = kk
            gate_s[b, pl.ds(r0, CHUNK), :] = _dot_bf16(_sigmoid(xg), g2_ref[...])
            lx_s[b, pl.ds(r0, CHUNK), :] = jnp.where(lane < DECAY_LORA, jnp.tanh(lx), lx)
        return carry

    lax.fori_loop(0, nc, mix_chunk, 0)

    rowblk = lax.broadcasted_iota(jnp.int32, (PACK_W, PACK_W), 0) // HEAD_A
    colblk = lax.broadcasted_iota(jnp.int32, (PACK_W, PACK_W), 1) // HEAD_A
    for b in range(group):
        for d in range(2):
            for p in range(N_PACKS):
                if s0_ref is None:
                    st_s[b, d, p] = jnp.zeros((PACK_W, PACK_W), F32)
                    continue
                rows_ = jnp.concatenate([s0_ref[b, d, p * PACK + h] for h in range(PACK)], axis=0)
                st_s[b, d, p] = jnp.where(rowblk == colblk, jnp.concatenate([rows_] * PACK, axis=1), 0.0)

    t_idx = lax.broadcasted_iota(jnp.int32, (CHUNK, PACK_W), 0)
    s_idx = lax.broadcasted_iota(jnp.int32, (CHUNK, PACK_W), 1) % CHUNK

    def stages(i):
        first_touch = i < nc // 2
        dot = functools.partial(lax.dot_general, preferred_element_type=F32)
        chains, out_sel = [], []
        for b, d in [(b, d) for b in range(group) for d in range(2)]:
            reverse = d == 1
            c = nc - 1 - i if reverse else i
            rows = pl.ds(pl.multiple_of(c * CHUNK, CHUNK), CHUNK)
            out_sel.append((b, rows))
            strict = (s_idx > t_idx) if reverse else (s_idx < t_idx)
            incl = (s_idx >= t_idx) if reverse else (s_idx <= t_idx)
            last = 0 if reverse else CHUNK - 1
            r, k, v, kk = r_s[b, rows, :], k_s[b, rows, :], v_s[b, rows, :], kk_s[b, rows, :]
            lx = lx_s[b, rows, :]
            wl = w0_ref[d:d + 1, :] + _dot_bf16(lx[:, 0:DECAY_LORA], w2_ref[d])
            logw = -jnp.exp(_log_sigmoid(wl) - 0.5)
            a = _sigmoid(a0_ref[d:d + 1, :] + _dot_bf16(lx[:, DECAY_LORA:], a2_ref[d]))
            kd = k * (1.0 + (a - 1.0) * ka_ref[...])
            kb = kk * a
            tri = _tri(CHUNK, reverse, False).astype(BF16)
            w1, w_rest = _split_bf16(logw)
            w2, w3 = _split_bf16(w_rest)
            cum = dot(tri, w1, NN) + dot(tri, w2, NN) + dot(tri, w3, NN)
            clast = cum[last:last + 1]
            rt = r * jnp.exp(cum)
            at = -kk * jnp.exp(cum - logw)
            einv = jnp.exp(-cum)
            kt, bt = kd * einv, kb * einv
            edec = jnp.exp(clast - cum)
            kp, bp = kd * edec, kb * edec
            ptot = jnp.exp(clast)
            rkd = r * kd * rk_ref[...]
            for p in range(N_PACKS):
                cs = slice(p * PACK_W, (p + 1) * PACK_W)
                chains.append(dict(
                    out=len(out_sel) - 1, st=(b, d, p), cs=cs, strict=strict, incl=incl, v=v[:, cs],
                    rkd=rkd[:, cs], ptot=ptot[:, cs],
                    ar=jnp.concatenate([at[:, cs], rt[:, cs]], axis=0).astype(BF16),
                    kt_bd=_block_diag(kt[:, cs], ones_bd), bt_bd=_block_diag(bt[:, cs], ones_bd),
                    v_bd=_block_diag(v[:, cs], ones_bd),
                    kbp=jnp.concatenate([kp[:, cs], bp[:, cs]], axis=0)))
            yield

        for ch in chains:
            ch['s_prev'] = st_s[ch['st']]
            ch['a_k'] = dot(ch['ar'], ch['kt_bd'], NT)
            ch['a_b'] = dot(ch['ar'], ch['bt_bd'], NT)
            ch['ars'] = dot(ch['ar'], ch['s_prev'].astype(BF16), NT)
        yield
        for ch in chains:
            a_ak = jnp.where(ch['strict'], ch['a_k'][:CHUNK], 0.0)
            a_rk = jnp.where(ch['incl'], ch['a_k'][CHUNK:], 0.0)
            ch['a_rb'] = jnp.where(ch['incl'], ch['a_b'][CHUNK:], 0.0).astype(BF16)
            n = jnp.where(ch['strict'], ch['a_b'][:CHUNK], 0.0)
            ch['n_hi'], n_lo = _split_bf16(n)
            ch['n_both'] = jnp.concatenate([ch['n_hi'], n_lo], axis=0)
            ch['pw'] = ch['n_hi']
            ch['m'] = jnp.where(s_idx == t_idx, 1.0, 0.0) + n
            akv = dot(jnp.concatenate([a_ak, a_rk], axis=0).astype(BF16), ch['v_bd'], NN)
            ch['rhs'] = ch['ars'][:CHUNK] + akv[:CHUNK]
            ch['y'] = ch['ars'][CHUNK:] + akv[CHUNK:]
        yield
        for ch in chains:
            ch['pw'] = dot(ch['pw'], _block_diag(ch['pw'], ones_bd), NN).astype(BF16)
        yield
        for j in range(1, 5):
            for ch in chains:
                pw_bd = _block_diag(ch['pw'], ones_bd)
                if j < 4:
                    both = dot(jnp.concatenate([ch['m'].astype(BF16), ch['pw']], axis=0), pw_bd, NN)
                    ch['m'] = ch['m'] + both[:CHUNK]
                    ch['pw'] = both[CHUNK:].astype(BF16)
                else:
                    ch['m'] = ch['m'] + dot(ch['m'].astype(BF16), pw_bd, NN)
            yield
        for ch in chains:
            ch['mb'] = ch['m'].astype(BF16)
            ch['u'] = dot(ch['mb'], _block_diag(ch['rhs'], ones_bd), NN)
        yield
        for ch in chains:
            u_hi, u_lo = _split_bf16(ch['u'])
            both = dot(ch['n_both'], _block_diag(u_hi, ones_bd), NN)
            nu = both[:CHUNK] + both[CHUNK:] + dot(ch['n_hi'], _block_diag(u_lo, ones_bd), NN)
            ch['resid'] = ch['rhs'] - ch['u'] + nu
        yield
        for ch in chains:
            ch['u'] = ch['u'] + dot(ch['mb'], _block_diag(ch['resid'], ones_bd), NN)
        yield
        for ch in chains:
            u = ch['u']
            ch['y'] = ch['y'] + dot(ch['a_rb'], _block_diag(u, ones_bd), NN)
            vu_hi, vu_lo = _split_bf16(jnp.concatenate([ch['v'], u], axis=0))
            kbp_hi, kbp_lo = _split_bf16(ch['kbp'])
            grown = dot(vu_hi, kbp_hi, TN) + dot(vu_hi, kbp_lo, TN) + dot(vu_lo, kbp_hi, TN)
            st_s[ch['st']] = ch['s_prev'] * ch['ptot'] + grown * ones_bd.astype(F32)
        yield
        n_ch = len(chains)
        sums = dot(jnp.concatenate([ch['y'] for ch in chains] + [ch['rkd'] for ch in chains],
                                   axis=0).astype(BF16), ones_bd, NN)
        ycs = [ch['y'] - sums[q * CHUNK:(q + 1) * CHUNK] * (1.0 / HEAD_A) for q, ch in enumerate(chains)]
        var = dot(jnp.concatenate([yc * yc for yc in ycs], axis=0).astype(BF16), ones_bd, NN) * (1.0 / HEAD_A)
        outs = [[] for _ in out_sel]
        for q, ch in enumerate(chains):
            cs = ch['cs']
            out = ycs[q] * lax.rsqrt(var[q * CHUNK:(q + 1) * CHUNK] + GN_EPS) * gnw_ref[:, cs] + gnb_ref[:, cs]
            outs[ch['out']].append(out + sums[(n_ch + q) * CHUNK:(n_ch + q + 1) * CHUNK] * ch['v'])
        for (b, rows), parts in zip(out_sel, outs):
            out = jnp.concatenate(parts, axis=1)

            @pl.when(first_touch)
            def _(b=b, rows=rows, out=out):
                y_ref[b, rows, :] = out

            @pl.when(jnp.logical_not(first_touch))
            def _(b=b, rows=rows, out=out):
                y_ref[b, rows, :] = (y_ref[b, rows, :] + out) * gate_s[b, rows, :]

    def finish():
        for b in range(group):
            for d in range(2):
                for p in range(N_PACKS):
                    s_fin = st_s[b, d, p]
                    for h in range(PACK):
                        sout_ref[b, d, p * PACK + h] = s_fin[h * HEAD_A:(h + 1) * HEAD_A,
                                                             h * HEAD_A:(h + 1) * HEAD_A]

    return stages, finish


def _mlstm_body(zm_ref, g_ref, gt_ref, c0_ref, n0_ref, m0_ref, conv_ref, bi_ref, bit_ref, bf_ref, bft_ref,
                  ng_ref, erep_ref, y_ref, cout_ref, nout_ref, mout_ref,
                  qk_s, h_s, c_s, n_s, m_s, *, seq_len, group):
    nc = seq_len // CHUNK

    def conv_chunk(c, carry):
        for b in range(group):
            zc, prev, nxt = _seq_neighbours(zm_ref, b, c, nc, slice(0, 2 * DB))
            u = conv_ref[0:1, :] * prev + conv_ref[1:2, :] * zc + conv_ref[2:3, :] * nxt
            qk = u * _sigmoid(u)
            lane = lax.broadcasted_iota(jnp.int32, qk.shape, 1)
            qk_s[b, pl.ds(pl.multiple_of(c * CHUNK, CHUNK), CHUNK), :] = jnp.where(lane >= DB,
                                                                                  qk * (HEAD_B ** -0.5), qk)
        return carry

    lax.fori_loop(0, nc, conv_chunk, 0)

    for b in range(group):
        for d in range(2):
            if c0_ref is None:
                c_s[b, d] = jnp.zeros((H_B, HEAD_B, HEAD_B), F32)
                n_s[b, d] = jnp.zeros((H_B, HEAD_B), F32)
                m_s[b, d] = jnp.zeros((H_B, LANES), F32)
                continue
            c_s[b, d] = c0_ref[b, d]
            n_s[b, d] = n0_ref[b, d]
            m_s[b, d] = jnp.broadcast_to(m0_ref[b, d], (H_B, LANES))

    e_rep = erep_ref[...]
    ones_tl = jnp.ones((CHUNK, LANES), BF16)
    row_id = lax.broadcasted_iota(jnp.int32, (CHUNK, LANES), 0)

    def scan_max(x, reverse):
        shift = 1
        while shift < CHUNK:
            if reverse:
                moved = jnp.where(row_id < CHUNK - shift, pltpu.roll(x, CHUNK - shift, 0), -jnp.inf)
            else:
                moved = jnp.where(row_id >= shift, pltpu.roll(x, shift, 0), -jnp.inf)
            x = jnp.maximum(x, moved)
            shift *= 2
        return x

    def dot3(a, b, dims):
        a_hi, a_lo = _split_bf16(a)
        b_hi, b_lo = _split_bf16(b)
        dg = functools.partial(lax.dot_general, dimension_numbers=dims, preferred_element_type=F32)
        return dg(a_hi, b_hi) + dg(a_hi, b_lo) + dg(a_lo, b_hi)

    def stages(i):
        dot = functools.partial(lax.dot_general, preferred_element_type=F32)
        chains = []
        for b, d in [(b, d) for b in range(group) for d in range(2)]:
            reverse = d == 1
            incl = _tri(CHUNK, reverse, False)
            tri_b = incl.astype(BF16)
            tri_row = _tri(CHUNK, not reverse, False).astype(F32)
            last = 0 if reverse else CHUNK - 1
            gsl = slice(d * H_B, (d + 1) * H_B)
            fsl = slice(2 * H_B + d * H_B, 2 * H_B + (d + 1) * H_B)
            c = nc - 1 - i if reverse else i
            rows = pl.ds(pl.multiple_of(c * CHUNK, CHUNK), CHUNK)
            g1, g_rest = _split_bf16(g_ref[b, rows, :])
            g2, g3 = _split_bf16(g_rest)
            g_rep = dot(g1, e_rep, NN) + dot(g2, e_rep, NN) + dot(g3, e_rep, NN)
            grow = gt_ref[b, c]
            i_row = grow[gsl, :] + bit_ref[gsl, :]
            f_row = _log_sigmoid(grow[fsl, :] + bft_ref[gsl, :])
            b_row = _mm(f_row, tri_row)
            for h in range(H_B):
                ji, jf = d * H_B + h, 2 * H_B + d * H_B + h
                ic = g_rep[:, ji * LANES:(ji + 1) * LANES] + bi_ref[:, ji:ji + 1]
                fc = _log_sigmoid(g_rep[:, jf * LANES:(jf + 1) * LANES] + bf_ref[:, ji:ji + 1])
                f1, f_rest = _split_bf16(fc)
                f2, f3 = _split_bf16(f_rest)
                bc = dot(tri_b, f1, NN) + dot(tri_b, f2, NN) + dot(tri_b, f3, NN)
                q = qk_s[b, rows, h * HEAD_B:(h + 1) * HEAD_B]
                chains.append(dict(
                    b=b, d=d, h=h, rows=rows, incl=incl, reverse=reverse, q=q, qb=q.astype(BF16),
                    k=qk_s[b, rows, DB + h * HEAD_B:DB + (h + 1) * HEAD_B],
                    v=zm_ref[b, rows, 2 * DB + h * HEAD_B:2 * DB + (h + 1) * HEAD_B],
                    bc=bc, ic=ic, bl=bc[last:last + 1, :], br=b_row[h:h + 1, :], ir=i_row[h:h + 1, :]))
            yield
        for q, ch in enumerate(chains):
            b, d, h = ch['b'], ch['d'], ch['h']
            ch['qk'] = _dot_bf16(ch['qb'], ch['k'], NT)
            ch['c_prev'] = c_s[b, d, h]
            ch['qc'] = _dot_bf16(ch['qb'], ch['c_prev'])
            ch['n_prev'] = n_s[b, d, h:h + 1, :]
            ch['qn'] = dot3(ch['q'], jnp.broadcast_to(ch['n_prev'], (HEAD_B, HEAD_B)), NT)
            if q % 2 == 1:
                yield
        for q, ch in enumerate(chains):
            b, d, h = ch['b'], ch['d'], ch['h']
            bc, ic, bl = ch['bc'], ch['ic'], ch['bl']
            m_prev = m_s[b, d, h:h + 1, :]
            m_t = bc + jnp.maximum(m_prev, scan_max(ic - bc, ch['reverse']))
            dmat = jnp.where(ch['incl'], bc[:, :CHUNK] - ch['br'] + ch['ir'], -jnp.inf)
            s = ch['qk'] * jnp.exp(dmat - m_t[:, :CHUNK])
            carry_w = jnp.exp(bc + m_prev - m_t)
            s_hi, s_lo = _split_bf16(s)
            den = dot(s_hi, ones_tl, NN) + dot(s_lo, ones_tl, NN) + carry_w * ch['qn']
            num = dot(s_hi, ch['v'].astype(BF16), NN) + carry_w * ch['qc']
            h_s[b, d, ch['rows'], h * HEAD_B:(h + 1) * HEAD_B] = num / jnp.maximum(jnp.abs(den), jnp.exp(-m_t))
            g = bl - bc + ic
            m_new = jnp.maximum(bl + m_prev, jnp.max(g, axis=0, keepdims=True))
            wk = jnp.exp(g - m_new)
            decay = jnp.exp(bl + m_prev - m_new)
            c_s[b, d, h] = decay * ch['c_prev'] + _dot_bf16(ch['k'], wk * ch['v'], TN)
            n_s[b, d, h:h + 1, :] = decay * ch['n_prev'] + jnp.sum(wk * ch['k'], axis=0, keepdims=True)
            m_s[b, d, h:h + 1, :] = m_new
            if q % 2 == 1:
                yield

    def norm_chunk(c, carry):
        rows = pl.ds(pl.multiple_of(c * CHUNK, CHUNK), CHUNK)
        for b in range(group):
            og = zm_ref[b, rows, 3 * DB:4 * DB]
            hm = h_s[b, 0, rows, :] + h_s[b, 1, rows, :]
            parts = []
            for h in range(H_B):
                hh = hm[:, h * HEAD_B:(h + 1) * HEAD_B]
                parts.append(hh * lax.rsqrt(jnp.mean(hh * hh, axis=-1, keepdims=True) + NORM_EPS))
            y_ref[b, rows, :] = jnp.concatenate(parts, axis=1) * ng_ref[...] * _sigmoid(og)
        return carry

    def finish():
        cout_ref[...] = c_s[...]
        nout_ref[...] = n_s[...]
        mout_ref[...] = m_s[...]
        lax.fori_loop(0, nc, norm_chunk, 0)

    return stages, finish


def _scan_kernel(*refs, body, n_steps, absent=(), **static):
    refs = list(refs)
    for pos in absent:
        refs.insert(pos, None)
    stages, finish = body(*refs, **static)

    def scan_step(i, carry):
        for _ in stages(i):
            pass
        return carry

    lax.fori_loop(0, n_steps, scan_step, 0)
    finish()


def _scan_specs(grp):
    lead = lambda *rest: pl.BlockSpec((grp,) + rest, lambda b: (b,) + (0,) * len(rest))

    def big(seq_len, cols):
        if 2 * grp * seq_len * cols * 4 <= SCAN_DOUBLE_BUFFER_BYTES:
            return lead(seq_len, cols)
        return pl.BlockSpec((grp, seq_len, cols), lambda b: (b, 0, 0), pipeline_mode=pl.Buffered(1))

    return lead, big


def _rwkv(zr, s0, p, grid):
    bsz, seq_len, _ = zr.shape
    nc = seq_len // CHUNK
    grp = 2 * SCAN_GROUP if 4 * SCAN_GROUP * seq_len * RWKV_COLS * 4 <= SCAN_DOUBLE_BUFFER_BYTES else SCAN_GROUP
    assert nc % 2 == 0 and bsz % grp == 0
    states = [] if s0 is None else [s0]
    kern = functools.partial(_scan_kernel, body=_rwkv_body, n_steps=nc, seq_len=seq_len, grid=grid, group=grp,
                             absent=(1,) if s0 is None else ())
    consts = [p['mu'], p['kkw'], p['ka'], p['rk'], p['gnw'], p['gnb'], p['w0'], p['w2'], p['a0'], p['a2'],
              p['g2'], p['ones_bd']]
    lead, big = _scan_specs(grp)
    return pl.pallas_call(
        kern,
        grid=(bsz // grp,),
        in_specs=[big(seq_len, RWKV_COLS)] + [lead(2, H_A, HEAD_A, HEAD_A)] * len(states)
                 + [_const_spec(a.shape) for a in consts],
        out_specs=[lead(seq_len, DA), lead(2, H_A, HEAD_A, HEAD_A)],
        out_shape=[jax.ShapeDtypeStruct((bsz, seq_len, DA), F32),
                   jax.ShapeDtypeStruct((bsz, 2, H_A, HEAD_A, HEAD_A), F32)],
        scratch_shapes=[pltpu.VMEM((grp, seq_len, DA), F32)] * 5
                       + [pltpu.VMEM((grp, seq_len, DECAY_LORA + AAA_LORA), F32),
                          pltpu.VMEM((grp, 2, N_PACKS, PACK_W, PACK_W), F32)],
        compiler_params=_params("arbitrary"),
        name="rwkv_scan",
    )(zr, *states, *consts)


def _mlstm(zm, g, gt, c0, n0, m0, p):
    bsz, seq_len, _ = zm.shape
    nc = seq_len // CHUNK
    grp = SCAN_GROUP
    assert bsz % grp == 0
    zero_state = c0 is None
    states = [] if zero_state else [c0, n0, m0.reshape(bsz, 2, H_B, 1)]
    kern = functools.partial(_scan_kernel, body=_mlstm_body, n_steps=nc, seq_len=seq_len, group=grp,
                             absent=(3, 4, 5) if zero_state else ())
    consts = [p['conv'], p['bi'], p['bit'], p['bf'], p['bft'], p['ng'], p['e_rep']]
    lead, big = _scan_specs(grp)
    state_specs = [] if zero_state else [lead(2, H_B, HEAD_B, HEAD_B), lead(2, H_B, HEAD_B), lead(2, H_B, 1)]
    return pl.pallas_call(
        kern,
        grid=(bsz // grp,),
        in_specs=[big(seq_len, MLSTM_MAIN), lead(seq_len, N_GATES), lead(nc, N_GATES, CHUNK)] + state_specs
                 + [_const_spec(a.shape) for a in consts],
        out_specs=[lead(seq_len, DB), lead(2, H_B, HEAD_B, HEAD_B), lead(2, H_B, HEAD_B), lead(2, H_B, LANES)],
        out_shape=[jax.ShapeDtypeStruct((bsz, seq_len, DB), F32),
                   jax.ShapeDtypeStruct((bsz, 2, H_B, HEAD_B, HEAD_B), F32),
                   jax.ShapeDtypeStruct((bsz, 2, H_B, HEAD_B), F32),
                   jax.ShapeDtypeStruct((bsz, 2, H_B, LANES), F32)],
        scratch_shapes=[pltpu.VMEM((grp, seq_len, 2 * DB), F32), pltpu.VMEM((grp, 2, seq_len, DB), F32),
                        pltpu.VMEM((grp, 2, H_B, HEAD_B, HEAD_B), F32), pltpu.VMEM((grp, 2, H_B, HEAD_B), F32),
                        pltpu.VMEM((grp, 2, H_B, LANES), F32)],
        compiler_params=_params("arbitrary"),
        name="mlstm_scan",
    )(zm, g, gt, *states, *consts)


def _outproj_kernel(x_ref, yr_ref, ym_ref, mod_ref, g_ref, wo_ref, rw_ref, rb_ref,
                    x1_ref, hn_ref, idx_ref, gate_ref):
    mod = mod_ref[0]
    g1 = mod[:, 2 * D_MODEL:3 * D_MODEL]
    sh2, sc2 = mod[:, 3 * D_MODEL:4 * D_MODEL], mod[:, 4 * D_MODEL:5 * D_MODEL]
    mix = _dot_bf16(yr_ref[...], wo_ref[0:DA, :]) + _dot_bf16(ym_ref[...], wo_ref[DA:, :])
    x1 = x_ref[...] + g1 * mix
    x1_ref[...] = x1
    y = x1 * lax.rsqrt(jnp.mean(x1 * x1, axis=-1, keepdims=True) + NORM_EPS)
    hn = y * g_ref[...] * (1.0 + sc2) + sh2
    hn_ref[...] = hn
    logits = _dot_bf16(hn, rw_ref[...]) + rb_ref[...]
    lane = lax.broadcasted_iota(jnp.int32, logits.shape, 1)
    vals, idxs = [], []
    for _ in range(TOP_K):
        top = jnp.max(logits, axis=-1, keepdims=True)
        pick = jnp.min(jnp.where(logits == top, lane, N_EXPERTS), axis=-1, keepdims=True)
        vals.append(top)
        idxs.append(pick)
        logits = jnp.where(lane == pick, -jnp.inf, logits)
    exps = [jnp.exp(v - vals[0]) for v in vals]
    total = exps[0] + exps[1] + exps[2] + exps[3]
    k_lane = lax.broadcasted_iota(jnp.int32, (x1.shape[0], TOP_K), 1)
    gates = jnp.zeros((x1.shape[0], TOP_K), F32)
    picks = jnp.zeros((x1.shape[0], TOP_K), jnp.int32)
    for j in range(TOP_K):
        gates = jnp.where(k_lane == j, exps[j] / total, gates)
        picks = jnp.where(k_lane == j, idxs[j], picks)
    idx_ref[...] = picks
    gate_ref[...] = gates


def _outproj(x2, yr, ym, mod, norm_g, w_out, router_w, router_b):
    rows = x2.shape[0]
    tiles_per_mod = rows // mod.shape[0] // PROJ_TILE
    row_spec = lambda w: pl.BlockSpec((PROJ_TILE, w), lambda i: (i, 0))
    return pl.pallas_call(
        _outproj_kernel,
        grid=(rows // PROJ_TILE,),
        in_specs=[row_spec(D_MODEL), row_spec(DA), row_spec(DB),
                  pl.BlockSpec((1, 1, 6 * D_MODEL), lambda i: (i // tiles_per_mod, 0, 0)),
                  _const_spec((1, D_MODEL)), _const_spec(w_out.shape), _const_spec(router_w.shape),
                  _const_spec((1, N_EXPERTS))],
        out_specs=[row_spec(D_MODEL), row_spec(D_MODEL), row_spec(TOP_K), row_spec(TOP_K)],
        out_shape=[jax.ShapeDtypeStruct((rows, D_MODEL), F32), jax.ShapeDtypeStruct((rows, D_MODEL), F32),
                   jax.ShapeDtypeStruct((rows, TOP_K), jnp.int32), jax.ShapeDtypeStruct((rows, TOP_K), F32)],
        compiler_params=_params("arbitrary"),
        name="outproj_router",
    )(x2, yr, ym, mod, norm_g.reshape(1, D_MODEL), w_out, router_w, router_b.reshape(1, N_EXPERTS))


def _rank_kernel(idx_ref, rank_ref, count_ref, run_s):
    i = pl.program_id(0)

    @pl.when(i == 0)
    def _():
        run_s[...] = jnp.zeros_like(run_s)

    idx = idx_ref[...]
    rows = idx.shape[0]
    lane = lax.broadcasted_iota(jnp.int32, (rows, N_EXPERTS), 1)
    hot = jnp.zeros((rows, N_EXPERTS), F32)
    for j in range(TOP_K):
        hot = hot + jnp.where(lane == idx[:, j:j + 1], 1.0, 0.0)
    before = _dot_bf16(_tri(rows, False, True).astype(F32), hot) + run_s[...]
    k_lane = lax.broadcasted_iota(jnp.int32, (rows, TOP_K), 1)
    rank = jnp.zeros((rows, TOP_K), F32)
    for j in range(TOP_K):
        rj = jnp.sum(jnp.where(lane == idx[:, j:j + 1], before, 0.0), axis=-1, keepdims=True)
        rank = jnp.where(k_lane == j, rj, rank)
    rank_ref[...] = rank.astype(jnp.int32)
    run_s[...] = run_s[...] + jnp.sum(hot, axis=0, keepdims=True)
    count_ref[...] = run_s[...].astype(jnp.int32)


def _expert_ranks(idx):
    rows = idx.shape[0]
    return pl.pallas_call(
        _rank_kernel,
        grid=(rows // ROW_TILE,),
        in_specs=[pl.BlockSpec((ROW_TILE, TOP_K), lambda i: (i, 0))],
        out_specs=[pl.BlockSpec((ROW_TILE, TOP_K), lambda i: (i, 0)), _const_spec((1, N_EXPERTS))],
        out_shape=[jax.ShapeDtypeStruct((rows, TOP_K), jnp.int32),
                   jax.ShapeDtypeStruct((1, N_EXPERTS), jnp.int32)],
        scratch_shapes=[pltpu.VMEM((1, N_EXPERTS), F32)],
        compiler_params=_params("arbitrary"),
        name="expert_ranks",
    )(idx)


def _dispatch_kernel(dest_ref, pend_ref, xa_ref, xb_ref, xs_ref, zero_s, sem, *, n_first_tiles):
    @pl.when(pl.program_id(0) == 0)
    def _():
        zero_s[...] = jnp.zeros_like(zero_s)

        def zero_block(start):
            return pltpu.make_async_copy(zero_s, xs_ref.at[pl.ds(pl.multiple_of(start, MOE_BLOCK), MOE_BLOCK)], sem)

        def has_rows(e):
            return pend_ref[e] > (pend_ref[e - 1] if e else 0)

        used_end = pend_ref[N_EXPERTS - 1]
        n_tail = (xs_ref.shape[0] - used_end) // MOE_BLOCK

        def start_tail(b, carry):
            zero_block(used_end + b * MOE_BLOCK).start()
            return carry

        def wait_tail(b, carry):
            zero_block(0).wait()
            return carry

        for e in range(N_EXPERTS):
            @pl.when(has_rows(e))
            def _(e=e):
                zero_block(pend_ref[e] - MOE_BLOCK).start()
        lax.fori_loop(0, n_tail, start_tail, 0)
        for e in range(N_EXPERTS):
            @pl.when(has_rows(e))
            def _():
                zero_block(0).wait()
        lax.fori_loop(0, n_tail, wait_tail, 0)

    def issue_from(x_ref):
        def issue(g, carry):
            for q in range(SUBLANES):
                for j in range(TOP_K):
                    slot = dest_ref[g * (SUBLANES * TOP_K) + q * TOP_K + j]
                    pltpu.make_async_copy(x_ref.at[g, pl.ds(q, 1)], xs_ref.at[pl.ds(slot, 1)],
                                          sem).start(priority=j % 2)
            return carry
        lax.fori_loop(0, ROW_TILE // SUBLANES, issue, 0)

    pl.when(pl.program_id(0) < n_first_tiles)(lambda: issue_from(xa_ref))
    pl.when(pl.program_id(0) >= n_first_tiles)(lambda: issue_from(xb_ref))
    all_rows = xs_ref.at[pl.ds(0, ROW_TILE * TOP_K)]
    pltpu.make_async_copy(all_rows, all_rows, sem).wait()


def _dispatch(dest_flat, pad_end, xa, xb, n_slots):
    na, nb = xa.shape[0] // ROW_TILE, xb.shape[0] // ROW_TILE
    tiles = lambda x: x.reshape(x.shape[0] // SUBLANES, SUBLANES, D_MODEL)
    block = (ROW_TILE // SUBLANES, SUBLANES, D_MODEL)
    return pl.pallas_call(
        functools.partial(_dispatch_kernel, n_first_tiles=na),
        grid=(na + nb,),
        in_specs=[pl.BlockSpec((ROW_TILE * TOP_K,), lambda i: (i,), memory_space=pltpu.SMEM),
                  pl.BlockSpec(memory_space=pltpu.SMEM),
                  pl.BlockSpec(block, lambda i: (jnp.minimum(i, na - 1), 0, 0)),
                  pl.BlockSpec(block, lambda i: (jnp.maximum(i - na, 0), 0, 0))],
        out_specs=pl.BlockSpec(memory_space=pl.ANY),
        out_shape=jax.ShapeDtypeStruct((n_slots, D_MODEL), xa.dtype),
        scratch_shapes=[pltpu.VMEM((MOE_BLOCK, D_MODEL), F32), pltpu.SemaphoreType.DMA(())],
        compiler_params=_params("arbitrary", disable_bounds_checks=True),
        name="moe_dispatch",
    )(dest_flat, pad_end, tiles(xa), tiles(xb))


def _expert_kernel(be_ref, nb_ref, next_ref, par_ref, xs_ref, w1_hbm, b1_ref, w2_hbm, b2_ref, y_ref,
                   w1_f, w2_f, w1_s, w2_s, sem1, sem2):
    i = pl.program_id(0)
    used = i < nb_ref[0]
    changed = jnp.logical_or(i == 0, be_ref[i] != be_ref[jnp.maximum(i - 1, 0)])

    def fetch(e, slot):
        return (pltpu.make_async_copy(w1_hbm.at[e], w1_f.at[slot], sem1.at[slot]),
                pltpu.make_async_copy(w2_hbm.at[e], w2_f.at[slot], sem2.at[slot]))

    @pl.when(i == 0)
    def _():
        for copy in fetch(be_ref[0], 0):
            copy.start()

    @pl.when(jnp.logical_and(changed, used))
    def _():
        slot = par_ref[i]
        for copy in fetch(be_ref[i], slot):
            copy.wait()

        @pl.when(next_ref[i] != be_ref[i])
        def _():
            for copy in fetch(next_ref[i], 1 - slot):
                copy.start()

        w1_s[...] = w1_f[slot].astype(BF16)
        w2_s[...] = w2_f[slot].astype(BF16)

    @pl.when(used)
    def _():
        xb = xs_ref[...].astype(BF16)
        hu = lax.dot_general(xb, w1_s[...], NN, preferred_element_type=F32) + b1_ref[0]
        glu = jnp.minimum(hu[:, :D_FF], SWIGLU_LIMIT)
        lin = jnp.clip(hu[:, D_FF:], -SWIGLU_LIMIT, SWIGLU_LIMIT)
        act = glu * _sigmoid(SWIGLU_ALPHA * glu) * (lin + 1.0)
        y_ref[...] = lax.dot_general(act.astype(BF16), w2_s[...], NN, preferred_element_type=F32) + b2_ref[0]

    @pl.when(i >= nb_ref[0])
    def _():
        y_ref[...] = jnp.zeros_like(y_ref)


def _experts(block_e, n_used, next_e, parity, xs, w1, b1, w2, b2):
    n_blocks = xs.shape[0] // MOE_BLOCK
    grid_spec = pltpu.PrefetchScalarGridSpec(
        num_scalar_prefetch=4,
        grid=(n_blocks,),
        in_specs=[pl.BlockSpec((MOE_BLOCK, D_MODEL), lambda i, be, nb, nx, pr: (jnp.minimum(i, nb[0] - 1), 0)),
                  pl.BlockSpec(memory_space=pl.ANY),
                  pl.BlockSpec((1, 1, 2 * D_FF), lambda i, be, nb, nx, pr: (be[i], 0, 0)),
                  pl.BlockSpec(memory_space=pl.ANY),
                  pl.BlockSpec((1, 1, D_MODEL), lambda i, be, nb, nx, pr: (be[i], 0, 0))],
        out_specs=pl.BlockSpec((MOE_BLOCK, D_MODEL), lambda i, be, nb, nx, pr: (i, 0)),
        scratch_shapes=[pltpu.VMEM((2, D_MODEL, 2 * D_FF), F32), pltpu.VMEM((2, D_FF, D_MODEL), F32),
                        pltpu.VMEM((D_MODEL, 2 * D_FF), BF16), pltpu.VMEM((D_FF, D_MODEL), BF16),
                        pltpu.SemaphoreType.DMA((2,)), pltpu.SemaphoreType.DMA((2,))],
    )
    return pl.pallas_call(
        _expert_kernel,
        grid_spec=grid_spec,
        out_shape=jax.ShapeDtypeStruct(xs.shape, F32),
        compiler_params=_params("arbitrary"),
        name="moe_experts",
    )(block_e, n_used, next_e, parity, xs, w1, b1.reshape(N_EXPERTS, 1, 2 * D_FF), w2,
      b2.reshape(N_EXPERTS, 1, D_MODEL))


def _combine_kernel(dest_ref, yb_ref, x1_ref, gate_ref, mod_ref, fg_ref, o_ref, buf, sem):
    def issue(g, carry):
        for q in range(SUBLANES):
            for j in range(TOP_K):
                slot = dest_ref[g * (SUBLANES * TOP_K) + q * TOP_K + j]
                pltpu.make_async_copy(yb_ref.at[pl.ds(slot, 1)], buf.at[g, j, pl.ds(q, 1)], sem).start(priority=j % 2)
        return carry

    lax.fori_loop(0, ROW_TILE // SUBLANES, issue, 0)
    pltpu.make_async_copy(buf, buf, sem).wait()

    g2 = mod_ref[0][:, 5 * D_MODEL:6 * D_MODEL]
    gates = gate_ref[...]
    moe = jnp.zeros((ROW_TILE, D_MODEL), F32)
    for j in range(TOP_K):
        moe = moe + buf[:, j].reshape(ROW_TILE, D_MODEL) * gates[:, j:j + 1]
    x2 = x1_ref[...] + g2 * moe
    y = x2 * lax.rsqrt(jnp.mean(x2 * x2, axis=-1, keepdims=True) + NORM_EPS)
    o_ref[...] = y * fg_ref[...]


def _combine(dest_flat, yb, x1, gates, mod, final_g, row_offset):
    rows = x1.shape[0]
    tiles_per_mod = rows // mod.shape[0] // ROW_TILE
    tile_offset = row_offset // ROW_TILE
    row_spec = pl.BlockSpec((ROW_TILE, D_MODEL), lambda i: (i, 0))
    return pl.pallas_call(
        _combine_kernel,
        grid=(rows // ROW_TILE,),
        in_specs=[pl.BlockSpec((ROW_TILE * TOP_K,), lambda i: (i + tile_offset,), memory_space=pltpu.SMEM),
                  pl.BlockSpec(memory_space=pl.ANY),
                  row_spec,
                  pl.BlockSpec((ROW_TILE, TOP_K), lambda i: (i, 0)),
                  pl.BlockSpec((1, 1, 6 * D_MODEL), lambda i: (i // tiles_per_mod, 0, 0)),
                  _const_spec((1, D_MODEL))],
        out_specs=row_spec,
        out_shape=jax.ShapeDtypeStruct((rows, D_MODEL), F32),
        scratch_shapes=[pltpu.VMEM((ROW_TILE // SUBLANES, TOP_K, SUBLANES, D_MODEL), F32),
                        pltpu.SemaphoreType.DMA(())],
        compiler_params=_params("arbitrary", disable_bounds_checks=True),
        name="moe_combine",
    )(dest_flat, yb, x1, gates, mod, final_g.reshape(1, D_MODEL))


def _block_diag_ones(width, block):
    i = jnp.arange(width) // block
    return (i[:, None] == i[None, :]).astype(F32)


def _mixers(x, mod, lp, s0, c0, n0, m0, grid):
    bsz, seq_len, _ = x.shape
    nc = seq_len // CHUNK
    x2 = x.reshape(bsz * seq_len, D_MODEL)
    zr, zm, zg, zgt = _inproj(x2, mod, lp['norm1_g'], lp['w_r'], lp['w_m'], lp['w_g'], lp['w_gt'])
    gt = zgt.reshape(N_GATES, bsz, nc, CHUNK).transpose(1, 2, 0, 3)
    yr, s_new = _rwkv(zr.reshape(bsz, seq_len, RWKV_COLS), s0, lp, grid)
    ym, c_new, n_new, m_new = _mlstm(zm.reshape(bsz, seq_len, MLSTM_MAIN), zg.reshape(bsz, seq_len, N_GATES), gt,
                                     c0, n0, m0, lp)
    x1, hn, idx, gates = _outproj(x2, yr.reshape(-1, DA), ym.reshape(-1, DB), mod, lp['norm2_g'], lp['w_out'],
                                  lp['router_w'], lp['router_b'])
    return x1, hn, idx, gates, (s_new, c_new, n_new, m_new[..., 0])


def kernel(x_prompt, x_sample, state_rwkv, state_mlstm_C, state_mlstm_n, state_mlstm_m, c, c_ctx, ada_w, ada_b, norm1_g, norm2_g, w_in, w_out, rwkv_mu, rwkv_w0, rwkv_w2, rwkv_a0, rwkv_a2, rwkv_g2, rwkv_kk, rwkv_ka, rwkv_rk, rwkv_gn_w, rwkv_gn_b, mlstm_conv, mlstm_bi, mlstm_bf, mlstm_norm_g, router_w, router_b, moe_w1, moe_b1, moe_w2, moe_b2, final_g):
    bp, lp_len, _ = x_prompt.shape
    bs, ls_len, _ = x_sample.shape
    w = w_in[0]
    lp = {
        'norm1_g': norm1_g[0], 'norm2_g': norm2_g[0],
        'w_r': w[:, :RWKV_COLS].astype(BF16),
        'w_m': w[:, RWKV_COLS:RWKV_COLS + MLSTM_MAIN].astype(BF16),
        'w_g': w[:, RWKV_COLS + MLSTM_MAIN:],
        'w_gt': w[:, RWKV_COLS + MLSTM_MAIN:].T,
        'w_out': w_out[0].astype(BF16),
        'mu': rwkv_mu[0].reshape(1, RWKV_COLS), 'kkw': rwkv_kk[0].reshape(1, DA), 'ka': rwkv_ka[0].reshape(1, DA),
        'rk': rwkv_rk[0].reshape(1, DA), 'gnw': rwkv_gn_w[0].reshape(1, DA), 'gnb': rwkv_gn_b[0].reshape(1, DA),
        'w0': rwkv_w0[0], 'w2': rwkv_w2[0], 'a0': rwkv_a0[0], 'a2': rwkv_a2[0], 'g2': rwkv_g2[0],
        'ones_bd': _block_diag_ones(PACK_W, HEAD_A).astype(BF16),
        'conv': mlstm_conv[0], 'bi': mlstm_bi[0].reshape(1, 2 * H_B), 'bit': mlstm_bi[0].reshape(2 * H_B, 1),
        'bf': mlstm_bf[0].reshape(1, 2 * H_B), 'bft': mlstm_bf[0].reshape(2 * H_B, 1),
        'ng': mlstm_norm_g[0].reshape(1, DB),
        'e_rep': jnp.repeat(jnp.eye(N_GATES, dtype=BF16), LANES, axis=1),
        'router_w': router_w[0], 'router_b': router_b[0],
    }
    cc = jnp.concatenate([c_ctx[None, :], c, jnp.zeros((2 * SUBLANES - 1 - bs, D_MODEL), F32)], axis=0)
    mod = _ada_mod(cc, ada_w[0], ada_b[0])
    mod_p = mod[0:1].reshape(1, 1, 6 * D_MODEL)
    mod_s = mod[1:1 + bs].reshape(bs, 1, 6 * D_MODEL)

    x1p, hnp, idxp, gatesp, st = _mixers(x_prompt, mod_p, lp, None, None, None, None, False)
    x1s, hns, idxs, gatess, _ = _mixers(x_sample, mod_s, lp, state_rwkv[:, 0], state_mlstm_C[:, 0],
                                        state_mlstm_n[:, 0], state_mlstm_m[:, 0], True)

    n_p = bp * lp_len
    idx = jnp.concatenate([idxp, idxs], axis=0)
    n_tok = idx.shape[0]
    rank, counts = _expert_ranks(idx)
    counts = counts[0]
    padded = (counts + MOE_BLOCK - 1) // MOE_BLOCK * MOE_BLOCK
    pad_end = jnp.cumsum(padded)
    pad_start = pad_end - padded
    n_blocks = n_tok * TOP_K // MOE_BLOCK + N_EXPERTS
    block_start = jnp.arange(n_blocks, dtype=jnp.int32) * MOE_BLOCK
    block_e = jnp.minimum(jnp.sum(pad_end[None, :] <= block_start[:, None], axis=1), N_EXPERTS - 1).astype(jnp.int32)
    n_used = (pad_end[-1:] // MOE_BLOCK).astype(jnp.int32)
    expert_ids = jnp.arange(N_EXPERTS, dtype=jnp.int32)
    start_of = jnp.sum(jnp.where(idx[:, :, None] == expert_ids, pad_start.astype(jnp.int32), 0), axis=-1)
    dest = (start_of + rank).astype(jnp.int32).reshape(-1)
    xs = _dispatch(dest, pad_end.astype(jnp.int32), hnp, hns, n_blocks * MOE_BLOCK)
    has_rows = counts > 0
    later = jnp.where(has_rows[None, :] & (expert_ids[None, :] > expert_ids[:, None]), expert_ids[None, :], N_EXPERTS)
    next_with_rows = jnp.min(later, axis=1)
    next_of = jnp.where(next_with_rows < N_EXPERTS, next_with_rows, expert_ids)
    place = jnp.cumsum(has_rows.astype(jnp.int32)) - 1
    yb = _experts(block_e, n_used, next_of[block_e].astype(jnp.int32), (place[block_e] % 2).astype(jnp.int32), xs,
                  moe_w1[0], moe_b1[0], moe_w2[0], moe_b2[0])
    y_prompt = _combine(dest, yb, x1p, gatesp, mod_p, final_g, 0).reshape(x_prompt.shape)
    y_sample = _combine(dest, yb, x1s, gatess, mod_s, final_g, n_p).reshape(x_sample.shape)

    s_new, c_new, n_new, m_new = st
    return (y_prompt, y_sample, s_new[:, None], c_new[:, None], n_new[:, None], m_new[:, None])
```

```python
import functools

import jax
import jax.numpy as jnp
from jax import lax
from jax.experimental import pallas as pl
from jax.experimental.pallas import tpu as pltpu

F32 = jnp.float32
BF16 = jnp.bfloat16
HIGHEST = lax.Precision.HIGHEST

D_MODEL = 1024
DA = 512
HEAD_A = 64
H_A = DA // HEAD_A
DB = 512
H_B = 4
HEAD_B = DB // H_B
DECAY_LORA = 64
AAA_LORA = 64
GATE_LORA = 128
RWKV_COLS = 3 * DA + DECAY_LORA + AAA_LORA + GATE_LORA
MLSTM_MAIN = 4 * DB
N_GATES = 4 * H_B
GRID_W = 64
CHUNK = 64
N_EXPERTS = 32
TOP_K = 4
D_FF = D_MODEL
SWIGLU_LIMIT = 7.0
SWIGLU_ALPHA = 1.702
MOE_BLOCK = 512
NORM_EPS = 1e-6
GN_EPS = 64e-5
ROW_TILE = 1024
PROJ_TILE = 512
SUBLANES = 8
LANES = 128
LANE_ROWS = D_MODEL // LANES
VMEM_LIMIT = 56 * 1024 * 1024
PACK = 4
PACK_W = PACK * HEAD_A
N_PACKS = H_A // PACK
SCAN_GROUP = 2
SCAN_DOUBLE_BUFFER_BYTES = 16 * 1024 * 1024

NN = (((1,), (0,)), ((), ()))
NT = (((1,), (1,)), ((), ()))
TN = (((0,), (0,)), ((), ()))


def _mm(a, b, precision=HIGHEST):
    return lax.dot_general(a, b, NN, precision=precision, preferred_element_type=F32)


def _mm_nt(a, b, precision=HIGHEST):
    return lax.dot_general(a, b, NT, precision=precision, preferred_element_type=F32)


def _mm_tn(a, b, precision=HIGHEST):
    return lax.dot_general(a, b, TN, precision=precision, preferred_element_type=F32)


def _dot_bf16(a, b, dims=NN):
    return lax.dot_general(a.astype(BF16), b.astype(BF16), dims, preferred_element_type=F32)


def _sigmoid(x):
    return 1.0 / (1.0 + jnp.exp(-x))


def _log_sigmoid(x):
    return jnp.minimum(x, 0.0) - jnp.log(1.0 + jnp.exp(-jnp.abs(x)))


def _params(*sem, **kw):
    return pltpu.CompilerParams(dimension_semantics=sem, vmem_limit_bytes=VMEM_LIMIT, **kw)


def _split_bf16(x):
    hi = x.astype(BF16)
    return hi, (x - hi.astype(F32)).astype(BF16)


def _const_spec(shape):
    nd = len(shape)
    return pl.BlockSpec(shape, lambda *_: (0,) * nd)


def _tri(n, reverse, strict):
    t = lax.broadcasted_iota(jnp.int32, (n, n), 0)
    s = lax.broadcasted_iota(jnp.int32, (n, n), 1)
    if reverse:
        return (s > t) if strict else (s >= t)
    return (s < t) if strict else (s <= t)


def _seq_neighbours(ref, b, c, nc, cols):
    r0 = pl.multiple_of(c * CHUNK, CHUNK)
    zc = ref[b, pl.ds(r0, CHUNK), cols]
    row = lax.broadcasted_iota(jnp.int32, zc.shape, 0)
    p0 = pl.multiple_of(jnp.maximum(r0 - SUBLANES, 0), SUBLANES)
    n0 = pl.multiple_of(jnp.minimum(r0 + CHUNK, (nc - 1) * CHUNK), SUBLANES)
    before = ref[b, pl.ds(p0, SUBLANES), cols][SUBLANES - 1:SUBLANES]
    after = ref[b, pl.ds(n0, SUBLANES), cols][0:1]
    before = jnp.where(c > 0, before, 0.0)
    after = jnp.where(c < nc - 1, after, 0.0)
    prev = jnp.where(row == 0, before, pltpu.roll(zc, 1, 0))
    nxt = jnp.where(row == CHUNK - 1, after, pltpu.roll(zc, CHUNK - 1, 0))
    return zc, prev, nxt


def _ada_kernel(c_ref, w_ref, b_ref, o_ref):
    cc = c_ref[...]
    o_ref[...] = _dot_bf16(cc * _sigmoid(cc), w_ref[...]) + b_ref[...]


def _ada_mod(cc, ada_w, ada_b):
    rows = cc.shape[0]
    ncol = ada_w.shape[1]
    tn = 1536
    return pl.pallas_call(
        _ada_kernel,
        grid=(ncol // tn,),
        in_specs=[_const_spec((rows, D_MODEL)),
                  pl.BlockSpec((D_MODEL, tn), lambda j: (0, j)),
                  pl.BlockSpec((1, tn), lambda j: (0, j))],
        out_specs=pl.BlockSpec((rows, tn), lambda j: (0, j)),
        out_shape=jax.ShapeDtypeStruct((rows, ncol), F32),
        compiler_params=_params("arbitrary"),
        name="ada_mod",
    )(cc, ada_w, ada_b.reshape(1, ncol))


def _inproj_kernel(x_ref, mod_ref, g_ref, wr_ref, wm_ref, wg_ref, wgt_ref, zr_ref, zm_ref, zg_ref, zgt_ref):
    x = x_ref[...]
    mod = mod_ref[0]
    sh, sc = mod[:, 0:D_MODEL], mod[:, D_MODEL:2 * D_MODEL]
    y = x * lax.rsqrt(jnp.mean(x * x, axis=-1, keepdims=True) + NORM_EPS)
    hn = y * g_ref[...] * (1.0 + sc) + sh
    hb = hn.astype(BF16)
    zr_ref[...] = lax.dot_general(hb, wr_ref[...], NN, preferred_element_type=F32)
    zm_ref[...] = lax.dot_general(hb, wm_ref[...], NN, preferred_element_type=F32)
    zg_ref[...] = _dot_bf16(hb, wg_ref[...])
    zgt_ref[...] = _dot_bf16(wgt_ref[...], hb, NT)


def _inproj(x2, mod, norm_g, w_r, w_m, w_g, w_gt):
    rows = x2.shape[0]
    tiles_per_mod = rows // mod.shape[0] // PROJ_TILE
    return pl.pallas_call(
        _inproj_kernel,
        grid=(rows // PROJ_TILE,),
        in_specs=[pl.BlockSpec((PROJ_TILE, D_MODEL), lambda i: (i, 0)),
                  pl.BlockSpec((1, 1, 6 * D_MODEL), lambda i: (i // tiles_per_mod, 0, 0)),
                  _const_spec((1, D_MODEL)),
                  _const_spec(w_r.shape), _const_spec(w_m.shape), _const_spec(w_g.shape),
                  _const_spec(w_gt.shape)],
        out_specs=[pl.BlockSpec((PROJ_TILE, RWKV_COLS), lambda i: (i, 0)),
                   pl.BlockSpec((PROJ_TILE, MLSTM_MAIN), lambda i: (i, 0)),
                   pl.BlockSpec((PROJ_TILE, N_GATES), lambda i: (i, 0)),
                   pl.BlockSpec((N_GATES, PROJ_TILE), lambda i: (0, i))],
        out_shape=[jax.ShapeDtypeStruct((rows, RWKV_COLS), F32),
                   jax.ShapeDtypeStruct((rows, MLSTM_MAIN), F32),
                   jax.ShapeDtypeStruct((rows, N_GATES), F32),
                   jax.ShapeDtypeStruct((N_GATES, rows), F32)],
        compiler_params=_params("arbitrary"),
        name="inproj",
    )(x2, mod, norm_g.reshape(1, D_MODEL), w_r, w_m, w_g, w_gt)


def _group_sum(x, ones_bd):
    hi = x.astype(BF16)
    lo = (x - hi.astype(F32)).astype(BF16)
    return (lax.dot_general(hi, ones_bd, NN, preferred_element_type=F32)
            + lax.dot_general(lo, ones_bd, NN, preferred_element_type=F32))


def _block_diag(x, ones_bd):
    return jnp.concatenate([x.astype(BF16)] * PACK, axis=0) * ones_bd


def _rwkv_body(zr_hbm, s0_ref, mu_ref, kkw_ref, ka_ref, rk_ref, gnw_ref, gnb_ref, w0_ref, w2_ref,
                 a0_ref, a2_ref, g2_ref, ones_ref, y_ref, sout_ref,
                 r_s, k_s, v_s, kk_s, gate_s, lx_s, st_s, zr_ref, zr_sem, *, seq_len, grid, group):
    nc = seq_len // CHUNK
    ones_bd = ones_ref[...]
    step, n_steps = pl.program_id(0), pl.num_programs(0)

    def fetch(s):
        return pltpu.make_async_copy(zr_hbm.at[pl.ds(s * group, group)], zr_ref, zr_sem)

    @pl.when(step == 0)
    def _():
        fetch(0).start()

    fetch(step).wait()

    def mix_chunk(c, carry):
        r0 = pl.multiple_of(c * CHUNK, CHUNK)
        for b in range(group):
            if grid:
                zc = zr_ref[b, pl.ds(r0, CHUNK), :]
                row = lax.broadcasted_iota(jnp.int32, zc.shape, 0)
                up0 = pl.multiple_of(jnp.maximum(c - 1, 0) * CHUNK, CHUNK)
                dn0 = pl.multiple_of(jnp.minimum(c + 1, nc - 1) * CHUNK, CHUNK)
                up = jnp.where(c > 0, zr_ref[b, pl.ds(up0, CHUNK), :], 0.0)
                down = jnp.where(c < nc - 1, zr_ref[b, pl.ds(dn0, CHUNK), :], 0.0)
                left = jnp.where(row == 0, 0.0, pltpu.roll(zc, 1, 0))
                right = jnp.where(row == CHUNK - 1, 0.0, pltpu.roll(zc, CHUNK - 1, 0))
                local = 0.25 * (up + down + left + right)
            else:
                zc, prev, nxt = _seq_neighbours(zr_ref, b, c, nc, slice(None))
                local = 0.5 * (prev + nxt)
            z = zc + (local - zc) * mu_ref[...]
            r, k, v = z[:, 0:DA], z[:, DA:2 * DA], z[:, 2 * DA:3 * DA]
            lx = z[:, 3 * DA:3 * DA + DECAY_LORA + AAA_LORA]
            xg = z[:, 3 * DA + DECAY_LORA + AAA_LORA:]
            kk = k * kkw_ref[...]
            sq = kk * kk
            ss = jnp.concatenate([_group_sum(sq[:, p * PACK_W:(p + 1) * PACK_W], ones_bd) for p in range(N_PACKS)],
                                 axis=1)
            kk = kk / jnp.maximum(jnp.sqrt(ss), 1e-12)
            lane = lax.broadcasted_iota(jnp.int32, lx.shape, 1)
            r_s[b, pl.ds(r0, CHUNK), :] = r
            k_s[b, pl.ds(r0, CHUNK), :] = k
            v_s[b, pl.ds(r0, CHUNK), :] = v
            kk_s[b, pl.ds(r0, CHUNK), :] = kk
            gate_s[b, pl.ds(r0, CHUNK), :] = _dot_bf16(_sigmoid(xg), g2_ref[...])
            lx_s[b, pl.ds(r0, CHUNK), :] = jnp.where(lane < DECAY_LORA, jnp.tanh(lx), lx)
        return carry

    lax.fori_loop(0, nc, mix_chunk, 0)

    @pl.when(step + 1 < n_steps)
    def _():
        fetch(step + 1).start()

    rowblk = lax.broadcasted_iota(jnp.int32, (PACK_W, PACK_W), 0) // HEAD_A
    colblk = lax.broadcasted_iota(jnp.int32, (PACK_W, PACK_W), 1) // HEAD_A
    for b in range(group):
        for d in range(2):
            for p in range(N_PACKS):
                if s0_ref is None:
                    st_s[b, d, p] = jnp.zeros((PACK_W, PACK_W), F32)
                    continue
                rows_ = jnp.concatenate([s0_ref[b, d, p * PACK + h] for h in range(PACK)], axis=0)
                st_s[b, d, p] = jnp.where(rowblk == colblk, jnp.concatenate([rows_] * PACK, axis=1), 0.0)

    t_idx = lax.broadcasted_iota(jnp.int32, (CHUNK, PACK_W), 0)
    s_idx = lax.broadcasted_iota(jnp.int32, (CHUNK, PACK_W), 1) % CHUNK

    def stages(i):
        first_touch = i < nc // 2
        dot = functools.partial(lax.dot_general, preferred_element_type=F32)
        chains, out_sel = [], []
        for b, d in [(b, d) for b in range(group) for d in range(2)]:
            reverse = d == 1
            c = nc - 1 - i if reverse else i
            rows = pl.ds(pl.multiple_of(c * CHUNK, CHUNK), CHUNK)
            out_sel.append((b, rows))
            strict = (s_idx > t_idx) if reverse else (s_idx < t_idx)
            incl = (s_idx >= t_idx) if reverse else (s_idx <= t_idx)
            last = 0 if reverse else CHUNK - 1
            r, k, v, kk = r_s[b, rows, :], k_s[b, rows, :], v_s[b, rows, :], kk_s[b, rows, :]
            lx = lx_s[b, rows, :]
            wl = w0_ref[d:d + 1, :] + _dot_bf16(lx[:, 0:DECAY_LORA], w2_ref[d])
            logw = -jnp.exp(_log_sigmoid(wl) - 0.5)
            a = _sigmoid(a0_ref[d:d + 1, :] + _dot_bf16(lx[:, DECAY_LORA:], a2_ref[d]))
            kd = k * (1.0 + (a - 1.0) * ka_ref[...])
            kb = kk * a
            tri = _tri(CHUNK, reverse, False).astype(BF16)
            w1, w_rest = _split_bf16(logw)
            w2, w3 = _split_bf16(w_rest)
            cum = dot(tri, w1, NN) + dot(tri, w2, NN) + dot(tri, w3, NN)
            clast = cum[last:last + 1]
            rt = r * jnp.exp(cum)
            at = -kk * jnp.exp(cum - logw)
            einv = jnp.exp(-cum)
            kt, bt = kd * einv, kb * einv
            edec = jnp.exp(clast - cum)
            kp, bp = kd * edec, kb * edec
            ptot = jnp.exp(clast)
            rkd = r * kd * rk_ref[...]
            for p in range(N_PACKS):
                cs = slice(p * PACK_W, (p + 1) * PACK_W)
                chains.append(dict(
                    out=len(out_sel) - 1, st=(b, d, p), cs=cs, strict=strict, incl=incl, v=v[:, cs],
                    rkd=rkd[:, cs], ptot=ptot[:, cs],
                    ar=jnp.concatenate([at[:, cs], rt[:, cs]], axis=0).astype(BF16),
                    kt_bd=_block_diag(kt[:, cs], ones_bd), bt_bd=_block_diag(bt[:, cs], ones_bd),
                    v_bd=_block_diag(v[:, cs], ones_bd),
                    kbp=jnp.concatenate([kp[:, cs], bp[:, cs]], axis=0)))
            yield

        for ch in chains:
            ch['s_prev'] = st_s[ch['st']]
            ch['a_k'] = dot(ch['ar'], ch['kt_bd'], NT)
            ch['a_b'] = dot(ch['ar'], ch['bt_bd'], NT)
            ch['ars'] = dot(ch['ar'], ch['s_prev'].astype(BF16), NT)
        yield
        for ch in chains:
            a_ak = jnp.where(ch['strict'], ch['a_k'][:CHUNK], 0.0)
            a_rk = jnp.where(ch['incl'], ch['a_k'][CHUNK:], 0.0)
            ch['a_rb'] = jnp.where(ch['incl'], ch['a_b'][CHUNK:], 0.0).astype(BF16)
            n = jnp.where(ch['strict'], ch['a_b'][:CHUNK], 0.0)
            ch['n_hi'], n_lo = _split_bf16(n)
            ch['n_both'] = jnp.concatenate([ch['n_hi'], n_lo], axis=0)
            ch['pw'] = ch['n_hi']
            ch['m'] = jnp.where(s_idx == t_idx, 1.0, 0.0) + n
            akv = dot(jnp.concatenate([a_ak, a_rk], axis=0).astype(BF16), ch['v_bd'], NN)
            ch['rhs'] = ch['ars'][:CHUNK] + akv[:CHUNK]
            ch['y'] = ch['ars'][CHUNK:] + akv[CHUNK:]
        yield
        for ch in chains:
            ch['pw'] = dot(ch['pw'], _block_diag(ch['pw'], ones_bd), NN).astype(BF16)
        yield
        for j in range(1, 5):
            for ch in chains:
                pw_bd = _block_diag(ch['pw'], ones_bd)
                if j < 4:
                    both = dot(jnp.concatenate([ch['m'].astype(BF16), ch['pw']], axis=0), pw_bd, NN)
                    ch['m'] = ch['m'] + both[:CHUNK]
                    ch['pw'] = both[CHUNK:].astype(BF16)
                else:
                    ch['m'] = ch['m'] + dot(ch['m'].astype(BF16), pw_bd, NN)
            yield
        for ch in chains:
            ch['mb'] = ch['m'].astype(BF16)
            ch['u'] = dot(ch['mb'], _block_diag(ch['rhs'], ones_bd), NN)
        yield
        for ch in chains:
            u_hi, u_lo = _split_bf16(ch['u'])
            both = dot(ch['n_both'], _block_diag(u_hi, ones_bd), NN)
            nu = both[:CHUNK] + both[CHUNK:] + dot(ch['n_hi'], _block_diag(u_lo, ones_bd), NN)
            ch['resid'] = ch['rhs'] - ch['u'] + nu
        yield
        for ch in chains:
            ch['u'] = ch['u'] + dot(ch['mb'], _block_diag(ch['resid'], ones_bd), NN)
        yield
        for ch in chains:
            u = ch['u']
            ch['y'] = ch['y'] + dot(ch['a_rb'], _block_diag(u, ones_bd), NN)
            vu_hi, vu_lo = _split_bf16(jnp.concatenate([ch['v'], u], axis=0))
            kbp_hi, kbp_lo = _split_bf16(ch['kbp'])
            grown = dot(vu_hi, kbp_hi, TN) + dot(vu_hi, kbp_lo, TN) + dot(vu_lo, kbp_hi, TN)
            st_s[ch['st']] = ch['s_prev'] * ch['ptot'] + grown * ones_bd.astype(F32)
        yield
        n_ch = len(chains)
        sums = dot(jnp.concatenate([ch['y'] for ch in chains] + [ch['rkd'] for ch in chains],
                                   axis=0).astype(BF16), ones_bd, NN)
        ycs = [ch['y'] - sums[q * CHUNK:(q + 1) * CHUNK] * (1.0 / HEAD_A) for q, ch in enumerate(chains)]
        var = dot(jnp.concatenate([yc * yc for yc in ycs], axis=0).astype(BF16), ones_bd, NN) * (1.0 / HEAD_A)
        outs = [[] for _ in out_sel]
        for q, ch in enumerate(chains):
            cs = ch['cs']
            out = ycs[q] * lax.rsqrt(var[q * CHUNK:(q + 1) * CHUNK] + GN_EPS) * gnw_ref[:, cs] + gnb_ref[:, cs]
            outs[ch['out']].append(out + sums[(n_ch + q) * CHUNK:(n_ch + q + 1) * CHUNK] * ch['v'])
        for (b, rows), parts in zip(out_sel, outs):
            out = jnp.concatenate(parts, axis=1)

            @pl.when(first_touch)
            def _(b=b, rows=rows, out=out):
                y_ref[b, rows, :] = out

            @pl.when(jnp.logical_not(first_touch))
            def _(b=b, rows=rows, out=out):
                y_ref[b, rows, :] = (y_ref[b, rows, :] + out) * gate_s[b, rows, :]

    def finish():
        for b in range(group):
            for d in range(2):
                for p in range(N_PACKS):
                    s_fin = st_s[b, d, p]
                    for h in range(PACK):
                        sout_ref[b, d, p * PACK + h] = s_fin[h * HEAD_A:(h + 1) * HEAD_A,
                                                             h * HEAD_A:(h + 1) * HEAD_A]

    return stages, finish


def _mlstm_body(zm_hbm, zvo_ref, g_ref, gt_ref, c0_ref, n0_ref, m0_ref, conv_ref, bi_ref, bit_ref, bf_ref, bft_ref,
                  ng_ref, erep_ref, y_ref, cout_ref, nout_ref, mout_ref,
                  qk_s, h_s, c_s, n_s, m_s, zqk_ref, zqk_sem, *, seq_len, group):
    nc = seq_len // CHUNK
    step, n_steps = pl.program_id(0), pl.num_programs(0)

    def fetch(s):
        return pltpu.make_async_copy(zm_hbm.at[pl.ds(s * group, group), :, pl.ds(0, 2 * DB)], zqk_ref, zqk_sem)

    @pl.when(step == 0)
    def _():
        fetch(0).start()

    fetch(step).wait()

    def conv_chunk(c, carry):
        for b in range(group):
            zc, prev, nxt = _seq_neighbours(zqk_ref, b, c, nc, slice(None))
            u = conv_ref[0:1, :] * prev + conv_ref[1:2, :] * zc + conv_ref[2:3, :] * nxt
            qk = u * _sigmoid(u)
            lane = lax.broadcasted_iota(jnp.int32, qk.shape, 1)
            qk_s[b, pl.ds(pl.multiple_of(c * CHUNK, CHUNK), CHUNK), :] = jnp.where(lane >= DB,
                                                                                  qk * (HEAD_B ** -0.5), qk)
        return carry

    lax.fori_loop(0, nc, conv_chunk, 0)

    @pl.when(step + 1 < n_steps)
    def _():
        fetch(step + 1).start()

    for b in range(group):
        for d in range(2):
            if c0_ref is None:
                c_s[b, d] = jnp.zeros((H_B, HEAD_B, HEAD_B), F32)
                n_s[b, d] = jnp.zeros((H_B, HEAD_B), F32)
                m_s[b, d] = jnp.zeros((H_B, LANES), F32)
                continue
            c_s[b, d] = c0_ref[b, d]
            n_s[b, d] = n0_ref[b, d]
            m_s[b, d] = jnp.broadcast_to(m0_ref[b, d], (H_B, LANES))

    e_rep = erep_ref[...]
    ones_tl = jnp.ones((CHUNK, LANES), BF16)
    row_id = lax.broadcasted_iota(jnp.int32, (CHUNK, LANES), 0)

    def scan_max(x, reverse):
        shift = 1
        while shift < CHUNK:
            if reverse:
                moved = jnp.where(row_id < CHUNK - shift, pltpu.roll(x, CHUNK - shift, 0), -jnp.inf)
            else:
                moved = jnp.where(row_id >= shift, pltpu.roll(x, shift, 0), -jnp.inf)
            x = jnp.maximum(x, moved)
            shift *= 2
        return x

    def dot3(a, b, dims):
        a_hi, a_lo = _split_bf16(a)
        b_hi, b_lo = _split_bf16(b)
        dg = functools.partial(lax.dot_general, dimension_numbers=dims, preferred_element_type=F32)
        return dg(a_hi, b_hi) + dg(a_hi, b_lo) + dg(a_lo, b_hi)

    def stages(i):
        dot = functools.partial(lax.dot_general, preferred_element_type=F32)
        chains = []
        for b, d in [(b, d) for b in range(group) for d in range(2)]:
            reverse = d == 1
            incl = _tri(CHUNK, reverse, False)
            tri_b = incl.astype(BF16)
            tri_row = _tri(CHUNK, not reverse, False).astype(F32)
            last = 0 if reverse else CHUNK - 1
            gsl = slice(d * H_B, (d + 1) * H_B)
            fsl = slice(2 * H_B + d * H_B, 2 * H_B + (d + 1) * H_B)
            c = nc - 1 - i if reverse else i
            rows = pl.ds(pl.multiple_of(c * CHUNK, CHUNK), CHUNK)
            g1, g_rest = _split_bf16(g_ref[b, rows, :])
            g2, g3 = _split_bf16(g_rest)
            g_rep = dot(g1, e_rep, NN) + dot(g2, e_rep, NN) + dot(g3, e_rep, NN)
            grow = gt_ref[b, c]
            i_row = grow[gsl, :] + bit_ref[gsl, :]
            f_row = _log_sigmoid(grow[fsl, :] + bft_ref[gsl, :])
            b_row = _mm(f_row, tri_row)
            for h in range(H_B):
                ji, jf = d * H_B + h, 2 * H_B + d * H_B + h
                ic = g_rep[:, ji * LANES:(ji + 1) * LANES] + bi_ref[:, ji:ji + 1]
                fc = _log_sigmoid(g_rep[:, jf * LANES:(jf + 1) * LANES] + bf_ref[:, ji:ji + 1])
                f1, f_rest = _split_bf16(fc)
                f2, f3 = _split_bf16(f_rest)
                bc = dot(tri_b, f1, NN) + dot(tri_b, f2, NN) + dot(tri_b, f3, NN)
                q = qk_s[b, rows, h * HEAD_B:(h + 1) * HEAD_B]
                chains.append(dict(
                    b=b, d=d, h=h, rows=rows, incl=incl, reverse=reverse, q=q, qb=q.astype(BF16),
                    k=qk_s[b, rows, DB + h * HEAD_B:DB + (h + 1) * HEAD_B],
                    v=zvo_ref[b, rows, h * HEAD_B:(h + 1) * HEAD_B],
                    bc=bc, ic=ic, bl=bc[last:last + 1, :], br=b_row[h:h + 1, :], ir=i_row[h:h + 1, :]))
            yield
        for q, ch in enumerate(chains):
            b, d, h = ch['b'], ch['d'], ch['h']
            ch['qk'] = _dot_bf16(ch['qb'], ch['k'], NT)
            ch['c_prev'] = c_s[b, d, h]
            ch['qc'] = _dot_bf16(ch['qb'], ch['c_prev'])
            ch['n_prev'] = n_s[b, d, h:h + 1, :]
            ch['qn'] = dot3(ch['q'], jnp.broadcast_to(ch['n_prev'], (HEAD_B, HEAD_B)), NT)
            if q % 2 == 1:
                yield
        for q, ch in enumerate(chains):
            b, d, h = ch['b'], ch['d'], ch['h']
            bc, ic, bl = ch['bc'], ch['ic'], ch['bl']
            m_prev = m_s[b, d, h:h + 1, :]
            m_t = bc + jnp.maximum(m_prev, scan_max(ic - bc, ch['reverse']))
            dmat = jnp.where(ch['incl'], bc[:, :CHUNK] - ch['br'] + ch['ir'], -jnp.inf)
            s = ch['qk'] * jnp.exp(dmat - m_t[:, :CHUNK])
            carry_w = jnp.exp(bc + m_prev - m_t)
            s_hi, s_lo = _split_bf16(s)
            den = dot(s_hi, ones_tl, NN) + dot(s_lo, ones_tl, NN) + carry_w * ch['qn']
            num = dot(s_hi, ch['v'].astype(BF16), NN) + carry_w * ch['qc']
            h_s[b, d, ch['rows'], h * HEAD_B:(h + 1) * HEAD_B] = num / jnp.maximum(jnp.abs(den), jnp.exp(-m_t))
            g = bl - bc + ic
            m_new = jnp.maximum(bl + m_prev, jnp.max(g, axis=0, keepdims=True))
            wk = jnp.exp(g - m_new)
            decay = jnp.exp(bl + m_prev - m_new)
            c_s[b, d, h] = decay * ch['c_prev'] + _dot_bf16(ch['k'], wk * ch['v'], TN)
            n_s[b, d, h:h + 1, :] = decay * ch['n_prev'] + jnp.sum(wk * ch['k'], axis=0, keepdims=True)
            m_s[b, d, h:h + 1, :] = m_new
            if q % 2 == 1:
                yield

    def norm_chunk(c, carry):
        rows = pl.ds(pl.multiple_of(c * CHUNK, CHUNK), CHUNK)
        for b in range(group):
            og = zvo_ref[b, rows, DB:2 * DB]
            hm = h_s[b, 0, rows, :] + h_s[b, 1, rows, :]
            parts = []
            for h in range(H_B):
                hh = hm[:, h * HEAD_B:(h + 1) * HEAD_B]
                parts.append(hh * lax.rsqrt(jnp.mean(hh * hh, axis=-1, keepdims=True) + NORM_EPS))
            y_ref[b, rows, :] = jnp.concatenate(parts, axis=1) * ng_ref[...] * _sigmoid(og)
        return carry

    def finish():
        cout_ref[...] = c_s[...]
        nout_ref[...] = n_s[...]
        mout_ref[...] = m_s[...]
        lax.fori_loop(0, nc, norm_chunk, 0)

    return stages, finish


def _scan_kernel(*refs, body, n_steps, absent=(), **static):
    refs = list(refs)
    for pos in absent:
        refs.insert(pos, None)
    stages, finish = body(*refs, **static)

    def scan_step(i, carry):
        for _ in stages(i):
            pass
        return carry

    lax.fori_loop(0, n_steps, scan_step, 0)
    finish()


def _scan_specs(grp):
    lead = lambda *rest: pl.BlockSpec((grp,) + rest, lambda b: (b,) + (0,) * len(rest))

    def big(seq_len, cols, col_block):
        single = 2 * grp * seq_len * cols * 4 >= SCAN_DOUBLE_BUFFER_BYTES
        return pl.BlockSpec((grp, seq_len, cols), lambda b: (b, 0, col_block),
                            **(dict(pipeline_mode=pl.Buffered(1)) if single else {}))

    return lead, big


def _rwkv(zr, s0, p, grid):
    bsz, seq_len, _ = zr.shape
    nc = seq_len // CHUNK
    grp = 2 * SCAN_GROUP if 4 * SCAN_GROUP * seq_len * RWKV_COLS * 4 <= SCAN_DOUBLE_BUFFER_BYTES else SCAN_GROUP
    assert nc % 2 == 0 and bsz % grp == 0
    states = [] if s0 is None else [s0]
    kern = functools.partial(_scan_kernel, body=_rwkv_body, n_steps=nc, seq_len=seq_len, grid=grid, group=grp,
                             absent=(1,) if s0 is None else ())
    consts = [p['mu'], p['kkw'], p['ka'], p['rk'], p['gnw'], p['gnb'], p['w0'], p['w2'], p['a0'], p['a2'],
              p['g2'], p['ones_bd']]
    lead, big = _scan_specs(grp)
    return pl.pallas_call(
        kern,
        grid=(bsz // grp,),
        in_specs=[pl.BlockSpec(memory_space=pl.ANY)] + [lead(2, H_A, HEAD_A, HEAD_A)] * len(states)
                 + [_const_spec(a.shape) for a in consts],
        out_specs=[lead(seq_len, DA), lead(2, H_A, HEAD_A, HEAD_A)],
        out_shape=[jax.ShapeDtypeStruct((bsz, seq_len, DA), F32),
                   jax.ShapeDtypeStruct((bsz, 2, H_A, HEAD_A, HEAD_A), F32)],
        scratch_shapes=[pltpu.VMEM((grp, seq_len, DA), F32)] * 5
                       + [pltpu.VMEM((grp, seq_len, DECAY_LORA + AAA_LORA), F32),
                          pltpu.VMEM((grp, 2, N_PACKS, PACK_W, PACK_W), F32),
                          pltpu.VMEM((grp, seq_len, RWKV_COLS), F32), pltpu.SemaphoreType.DMA(())],
        compiler_params=_params("arbitrary"),
        name="rwkv_scan",
    )(zr, *states, *consts)


def _mlstm(zm, g, gt, c0, n0, m0, p):
    bsz, seq_len, _ = zm.shape
    nc = seq_len // CHUNK
    grp = SCAN_GROUP
    assert bsz % grp == 0
    zero_state = c0 is None
    states = [] if zero_state else [c0, n0, m0.reshape(bsz, 2, H_B, 1)]
    kern = functools.partial(_scan_kernel, body=_mlstm_body, n_steps=nc, seq_len=seq_len, group=grp,
                             absent=(4, 5, 6) if zero_state else ())
    consts = [p['conv'], p['bi'], p['bit'], p['bf'], p['bft'], p['ng'], p['e_rep']]
    lead, big = _scan_specs(grp)
    state_specs = [] if zero_state else [lead(2, H_B, HEAD_B, HEAD_B), lead(2, H_B, HEAD_B), lead(2, H_B, 1)]
    vo_spec = big(seq_len, 2 * DB, 1)
    return pl.pallas_call(
        kern,
        grid=(bsz // grp,),
        in_specs=[pl.BlockSpec(memory_space=pl.ANY), vo_spec, lead(seq_len, N_GATES), lead(nc, N_GATES, CHUNK)]
                 + state_specs + [_const_spec(a.shape) for a in consts],
        out_specs=[lead(seq_len, DB), lead(2, H_B, HEAD_B, HEAD_B), lead(2, H_B, HEAD_B), lead(2, H_B, LANES)],
        out_shape=[jax.ShapeDtypeStruct((bsz, seq_len, DB), F32),
                   jax.ShapeDtypeStruct((bsz, 2, H_B, HEAD_B, HEAD_B), F32),
                   jax.ShapeDtypeStruct((bsz, 2, H_B, HEAD_B), F32),
                   jax.ShapeDtypeStruct((bsz, 2, H_B, LANES), F32)],
        scratch_shapes=[pltpu.VMEM((grp, seq_len, 2 * DB), F32), pltpu.VMEM((grp, 2, seq_len, DB), F32),
                        pltpu.VMEM((grp, 2, H_B, HEAD_B, HEAD_B), F32), pltpu.VMEM((grp, 2, H_B, HEAD_B), F32),
                        pltpu.VMEM((grp, 2, H_B, LANES), F32),
                        pltpu.VMEM((grp, seq_len, 2 * DB), F32), pltpu.SemaphoreType.DMA(())],
        compiler_params=_params("arbitrary"),
        name="mlstm_scan",
    )(zm, zm, g, gt, *states, *consts)


def _outproj_kernel(x_ref, yr_ref, ym_ref, mod_ref, g_ref, wo_ref, rw_ref, rb_ref,
                    x1_ref, hn_ref, idx_ref, gate_ref):
    mod = mod_ref[0]
    g1 = mod[:, 2 * D_MODEL:3 * D_MODEL]
    sh2, sc2 = mod[:, 3 * D_MODEL:4 * D_MODEL], mod[:, 4 * D_MODEL:5 * D_MODEL]
    mix = _dot_bf16(yr_ref[...], wo_ref[0:DA, :]) + _dot_bf16(ym_ref[...], wo_ref[DA:, :])
    x1 = x_ref[...] + g1 * mix
    x1_ref[...] = x1
    y = x1 * lax.rsqrt(jnp.mean(x1 * x1, axis=-1, keepdims=True) + NORM_EPS)
    hn = y * g_ref[...] * (1.0 + sc2) + sh2
    hn_ref[...] = hn
    logits = _dot_bf16(hn, rw_ref[...]) + rb_ref[...]
    lane = lax.broadcasted_iota(jnp.int32, logits.shape, 1)
    vals, idxs = [], []
    for _ in range(TOP_K):
        top = jnp.max(logits, axis=-1, keepdims=True)
        pick = jnp.min(jnp.where(logits == top, lane, N_EXPERTS), axis=-1, keepdims=True)
        vals.append(top)
        idxs.append(pick)
        logits = jnp.where(lane == pick, -jnp.inf, logits)
    exps = [jnp.exp(v - vals[0]) for v in vals]
    total = exps[0] + exps[1] + exps[2] + exps[3]
    k_lane = lax.broadcasted_iota(jnp.int32, (x1.shape[0], TOP_K), 1)
    gates = jnp.zeros((x1.shape[0], TOP_K), F32)
    picks = jnp.zeros((x1.shape[0], TOP_K), jnp.int32)
    for j in range(TOP_K):
        gates = jnp.where(k_lane == j, exps[j] / total, gates)
        picks = jnp.where(k_lane == j, idxs[j], picks)
    idx_ref[...] = picks
    gate_ref[...] = gates


def _outproj(x2, yr, ym, mod, norm_g, w_out, router_w, router_b):
    rows = x2.shape[0]
    tiles_per_mod = rows // mod.shape[0] // PROJ_TILE
    row_spec = lambda w: pl.BlockSpec((PROJ_TILE, w), lambda i: (i, 0))
    return pl.pallas_call(
        _outproj_kernel,
        grid=(rows // PROJ_TILE,),
        in_specs=[row_spec(D_MODEL), row_spec(DA), row_spec(DB),
                  pl.BlockSpec((1, 1, 6 * D_MODEL), lambda i: (i // tiles_per_mod, 0, 0)),
                  _const_spec((1, D_MODEL)), _const_spec(w_out.shape), _const_spec(router_w.shape),
                  _const_spec((1, N_EXPERTS))],
        out_specs=[row_spec(D_MODEL), row_spec(D_MODEL), row_spec(TOP_K), row_spec(TOP_K)],
        out_shape=[jax.ShapeDtypeStruct((rows, D_MODEL), F32), jax.ShapeDtypeStruct((rows, D_MODEL), F32),
                   jax.ShapeDtypeStruct((rows, TOP_K), jnp.int32), jax.ShapeDtypeStruct((rows, TOP_K), F32)],
        compiler_params=_params("arbitrary"),
        name="outproj_router",
    )(x2, yr, ym, mod, norm_g.reshape(1, D_MODEL), w_out, router_w, router_b.reshape(1, N_EXPERTS))


def _rank_kernel(idx_ref, rank_ref, count_ref, run_s):
    i = pl.program_id(0)

    @pl.when(i == 0)
    def _():
        run_s[...] = jnp.zeros_like(run_s)

    idx = idx_ref[...]
    rows = idx.shape[0]
    lane = lax.broadcasted_iota(jnp.int32, (rows, N_EXPERTS), 1)
    hot = jnp.zeros((rows, N_EXPERTS), F32)
    for j in range(TOP_K):
        hot = hot + jnp.where(lane == idx[:, j:j + 1], 1.0, 0.0)
    before = _dot_bf16(_tri(rows, False, True).astype(F32), hot) + run_s[...]
    k_lane = lax.broadcasted_iota(jnp.int32, (rows, TOP_K), 1)
    rank = jnp.zeros((rows, TOP_K), F32)
    for j in range(TOP_K):
        rj = jnp.sum(jnp.where(lane == idx[:, j:j + 1], before, 0.0), axis=-1, keepdims=True)
        rank = jnp.where(k_lane == j, rj, rank)
    rank_ref[...] = rank.astype(jnp.int32)
    run_s[...] = run_s[...] + jnp.sum(hot, axis=0, keepdims=True)
    count_ref[...] = run_s[...].astype(jnp.int32)


def _expert_ranks(idx):
    rows = idx.shape[0]
    return pl.pallas_call(
        _rank_kernel,
        grid=(rows // ROW_TILE,),
        in_specs=[pl.BlockSpec((ROW_TILE, TOP_K), lambda i: (i, 0))],
        out_specs=[pl.BlockSpec((ROW_TILE, TOP_K), lambda i: (i, 0)), _const_spec((1, N_EXPERTS))],
        out_shape=[jax.ShapeDtypeStruct((rows, TOP_K), jnp.int32),
                   jax.ShapeDtypeStruct((1, N_EXPERTS), jnp.int32)],
        scratch_shapes=[pltpu.VMEM((1, N_EXPERTS), F32)],
        compiler_params=_params("arbitrary"),
        name="expert_ranks",
    )(idx)


def _dispatch_kernel(dest_ref, pend_ref, xa_ref, xb_ref, xs_ref, zero_s, sem, *, n_first_tiles):
    @pl.when(pl.program_id(0) == 0)
    def _():
        zero_s[...] = jnp.zeros_like(zero_s)

        def zero_block(start):
            return pltpu.make_async_copy(zero_s, xs_ref.at[pl.ds(pl.multiple_of(start, MOE_BLOCK), MOE_BLOCK)], sem)

        def has_rows(e):
            return pend_ref[e] > (pend_ref[e - 1] if e else 0)

        used_end = pend_ref[N_EXPERTS - 1]
        n_tail = (xs_ref.shape[0] - used_end) // MOE_BLOCK

        def start_tail(b, carry):
            zero_block(used_end + b * MOE_BLOCK).start()
            return carry

        def wait_tail(b, carry):
            zero_block(0).wait()
            return carry

        for e in range(N_EXPERTS):
            @pl.when(has_rows(e))
            def _(e=e):
                zero_block(pend_ref[e] - MOE_BLOCK).start()
        lax.fori_loop(0, n_tail, start_tail, 0)
        for e in range(N_EXPERTS):
            @pl.when(has_rows(e))
            def _():
                zero_block(0).wait()
        lax.fori_loop(0, n_tail, wait_tail, 0)

    def issue_from(x_ref):
        def issue(g, carry):
            for q in range(SUBLANES):
                for j in range(TOP_K):
                    slot = dest_ref[g * (SUBLANES * TOP_K) + q * TOP_K + j]
                    pltpu.make_async_copy(x_ref.at[g, pl.ds(q, 1)], xs_ref.at[pl.ds(slot, 1)],
                                          sem).start(priority=j % 2)
            return carry
        lax.fori_loop(0, ROW_TILE // SUBLANES, issue, 0)

    pl.when(pl.program_id(0) < n_first_tiles)(lambda: issue_from(xa_ref))
    pl.when(pl.program_id(0) >= n_first_tiles)(lambda: issue_from(xb_ref))
    all_rows = xs_ref.at[pl.ds(0, ROW_TILE * TOP_K)]
    pltpu.make_async_copy(all_rows, all_rows, sem).wait()


def _dispatch(dest_flat, pad_end, xa, xb, n_slots):
    na, nb = xa.shape[0] // ROW_TILE, xb.shape[0] // ROW_TILE
    tiles = lambda x: x.reshape(x.shape[0] // SUBLANES, SUBLANES, D_MODEL)
    block = (ROW_TILE // SUBLANES, SUBLANES, D_MODEL)
    return pl.pallas_call(
        functools.partial(_dispatch_kernel, n_first_tiles=na),
        grid=(na + nb,),
        in_specs=[pl.BlockSpec((ROW_TILE * TOP_K,), lambda i: (i,), memory_space=pltpu.SMEM),
                  pl.BlockSpec(memory_space=pltpu.SMEM),
                  pl.BlockSpec(block, lambda i: (jnp.minimum(i, na - 1), 0, 0)),
                  pl.BlockSpec(block, lambda i: (jnp.maximum(i - na, 0), 0, 0))],
        out_specs=pl.BlockSpec(memory_space=pl.ANY),
        out_shape=jax.ShapeDtypeStruct((n_slots, D_MODEL), xa.dtype),
        scratch_shapes=[pltpu.VMEM((MOE_BLOCK, D_MODEL), F32), pltpu.SemaphoreType.DMA(())],
        compiler_params=_params("arbitrary", disable_bounds_checks=True),
        name="moe_dispatch",
    )(dest_flat, pad_end, tiles(xa), tiles(xb))


def _expert_kernel(be_ref, nb_ref, next_ref, par_ref, xs_ref, w1_hbm, b1_ref, w2_hbm, b2_ref, y_ref,
                   w1_f, w2_f, w1_s, w2_s, sem1, sem2):
    i = pl.program_id(0)
    used = i < nb_ref[0]
    changed = jnp.logical_or(i == 0, be_ref[i] != be_ref[jnp.maximum(i - 1, 0)])

    def fetch(e, slot):
        return (pltpu.make_async_copy(w1_hbm.at[e], w1_f.at[slot], sem1.at[slot]),
                pltpu.make_async_copy(w2_hbm.at[e], w2_f.at[slot], sem2.at[slot]))

    @pl.when(i == 0)
    def _():
        for copy in fetch(be_ref[0], 0):
            copy.start()

    @pl.when(jnp.logical_and(changed, used))
    def _():
        slot = par_ref[i]
        for copy in fetch(be_ref[i], slot):
            copy.wait()

        @pl.when(next_ref[i] != be_ref[i])
        def _():
            for copy in fetch(next_ref[i], 1 - slot):
                copy.start()

        w1_s[...] = w1_f[slot].astype(BF16)
        w2_s[...] = w2_f[slot].astype(BF16)

    @pl.when(used)
    def _():
        xb = xs_ref[...].astype(BF16)
        hu = lax.dot_general(xb, w1_s[...], NN, preferred_element_type=F32) + b1_ref[0]
        glu = jnp.minimum(hu[:, :D_FF], SWIGLU_LIMIT)
        lin = jnp.clip(hu[:, D_FF:], -SWIGLU_LIMIT, SWIGLU_LIMIT)
        act = glu * _sigmoid(SWIGLU_ALPHA * glu) * (lin + 1.0)
        y_ref[...] = lax.dot_general(act.astype(BF16), w2_s[...], NN, preferred_element_type=F32) + b2_ref[0]

    @pl.when(i >= nb_ref[0])
    def _():
        y_ref[...] = jnp.zeros_like(y_ref)


def _experts(block_e, n_used, next_e, parity, xs, w1, b1, w2, b2):
    n_blocks = xs.shape[0] // MOE_BLOCK
    grid_spec = pltpu.PrefetchScalarGridSpec(
        num_scalar_prefetch=4,
        grid=(n_blocks,),
        in_specs=[pl.BlockSpec((MOE_BLOCK, D_MODEL), lambda i, be, nb, nx, pr: (jnp.minimum(i, nb[0] - 1), 0)),
                  pl.BlockSpec(memory_space=pl.ANY),
                  pl.BlockSpec((1, 1, 2 * D_FF), lambda i, be, nb, nx, pr: (be[i], 0, 0)),
                  pl.BlockSpec(memory_space=pl.ANY),
                  pl.BlockSpec((1, 1, D_MODEL), lambda i, be, nb, nx, pr: (be[i], 0, 0))],
        out_specs=pl.BlockSpec((MOE_BLOCK, D_MODEL), lambda i, be, nb, nx, pr: (i, 0)),
        scratch_shapes=[pltpu.VMEM((2, D_MODEL, 2 * D_FF), F32), pltpu.VMEM((2, D_FF, D_MODEL), F32),
                        pltpu.VMEM((D_MODEL, 2 * D_FF), BF16), pltpu.VMEM((D_FF, D_MODEL), BF16),
                        pltpu.SemaphoreType.DMA((2,)), pltpu.SemaphoreType.DMA((2,))],
    )
    return pl.pallas_call(
        _expert_kernel,
        grid_spec=grid_spec,
        out_shape=jax.ShapeDtypeStruct(xs.shape, F32),
        compiler_params=_params("arbitrary"),
        name="moe_experts",
    )(block_e, n_used, next_e, parity, xs, w1, b1.reshape(N_EXPERTS, 1, 2 * D_FF), w2,
      b2.reshape(N_EXPERTS, 1, D_MODEL))


def _combine_kernel(dest_ref, yb_ref, x1_ref, gate_ref, mod_ref, fg_ref, o_ref, buf, sem):
    def issue(g, carry):
        for q in range(SUBLANES):
            for j in range(TOP_K):
                slot = dest_ref[g * (SUBLANES * TOP_K) + q * TOP_K + j]
                pltpu.make_async_copy(yb_ref.at[pl.ds(slot, 1)], buf.at[g, j, pl.ds(q, 1)], sem).start(priority=j % 2)
        return carry

    lax.fori_loop(0, ROW_TILE // SUBLANES, issue, 0)
    pltpu.make_async_copy(buf, buf, sem).wait()

    g2 = mod_ref[0][:, 5 * D_MODEL:6 * D_MODEL]
    gates = gate_ref[...]
    moe = jnp.zeros((ROW_TILE, D_MODEL), F32)
    for j in range(TOP_K):
        moe = moe + buf[:, j].reshape(ROW_TILE, D_MODEL) * gates[:, j:j + 1]
    x2 = x1_ref[...] + g2 * moe
    y = x2 * lax.rsqrt(jnp.mean(x2 * x2, axis=-1, keepdims=True) + NORM_EPS)
    o_ref[...] = y * fg_ref[...]


def _combine(dest_flat, yb, x1, gates, mod, final_g, row_offset):
    rows = x1.shape[0]
    tiles_per_mod = rows // mod.shape[0] // ROW_TILE
    tile_offset = row_offset // ROW_TILE
    row_spec = pl.BlockSpec((ROW_TILE, D_MODEL), lambda i: (i, 0))
    return pl.pallas_call(
        _combine_kernel,
        grid=(rows // ROW_TILE,),
        in_specs=[pl.BlockSpec((ROW_TILE * TOP_K,), lambda i: (i + tile_offset,), memory_space=pltpu.SMEM),
                  pl.BlockSpec(memory_space=pl.ANY),
                  row_spec,
                  pl.BlockSpec((ROW_TILE, TOP_K), lambda i: (i, 0)),
                  pl.BlockSpec((1, 1, 6 * D_MODEL), lambda i: (i // tiles_per_mod, 0, 0)),
                  _const_spec((1, D_MODEL))],
        out_specs=row_spec,
        out_shape=jax.ShapeDtypeStruct((rows, D_MODEL), F32),
        scratch_shapes=[pltpu.VMEM((ROW_TILE // SUBLANES, TOP_K, SUBLANES, D_MODEL), F32),
                        pltpu.SemaphoreType.DMA(())],
        compiler_params=_params("arbitrary", disable_bounds_checks=True),
        name="moe_combine",
    )(dest_flat, yb, x1, gates, mod, final_g.reshape(1, D_MODEL))


def _block_diag_ones(width, block):
    i = jnp.arange(width) // block
    return (i[:, None] == i[None, :]).astype(F32)


def _mixers(x, mod, lp, s0, c0, n0, m0, grid):
    bsz, seq_len, _ = x.shape
    nc = seq_len // CHUNK
    x2 = x.reshape(bsz * seq_len, D_MODEL)
    zr, zm, zg, zgt = _inproj(x2, mod, lp['norm1_g'], lp['w_r'], lp['w_m'], lp['w_g'], lp['w_gt'])
    gt = zgt.reshape(N_GATES, bsz, nc, CHUNK).transpose(1, 2, 0, 3)
    yr, s_new = _rwkv(zr.reshape(bsz, seq_len, RWKV_COLS), s0, lp, grid)
    ym, c_new, n_new, m_new = _mlstm(zm.reshape(bsz, seq_len, MLSTM_MAIN), zg.reshape(bsz, seq_len, N_GATES), gt,
                                     c0, n0, m0, lp)
    x1, hn, idx, gates = _outproj(x2, yr.reshape(-1, DA), ym.reshape(-1, DB), mod, lp['norm2_g'], lp['w_out'],
                                  lp['router_w'], lp['router_b'])
    return x1, hn, idx, gates, (s_new, c_new, n_new, m_new[..., 0])


def kernel(x_prompt, x_sample, state_rwkv, state_mlstm_C, state_mlstm_n, state_mlstm_m, c, c_ctx, ada_w, ada_b, norm1_g, norm2_g, w_in, w_out, rwkv_mu, rwkv_w0, rwkv_w2, rwkv_a0, rwkv_a2, rwkv_g2, rwkv_kk, rwkv_ka, rwkv_rk, rwkv_gn_w, rwkv_gn_b, mlstm_conv, mlstm_bi, mlstm_bf, mlstm_norm_g, router_w, router_b, moe_w1, moe_b1, moe_w2, moe_b2, final_g):
    bp, lp_len, _ = x_prompt.shape
    bs, ls_len, _ = x_sample.shape
    w = w_in[0]
    lp = {
        'norm1_g': norm1_g[0], 'norm2_g': norm2_g[0],
        'w_r': w[:, :RWKV_COLS].astype(BF16),
        'w_m': w[:, RWKV_COLS:RWKV_COLS + MLSTM_MAIN].astype(BF16),
        'w_g': w[:, RWKV_COLS + MLSTM_MAIN:],
        'w_gt': w[:, RWKV_COLS + MLSTM_MAIN:].T,
        'w_out': w_out[0].astype(BF16),
        'mu': rwkv_mu[0].reshape(1, RWKV_COLS), 'kkw': rwkv_kk[0].reshape(1, DA), 'ka': rwkv_ka[0].reshape(1, DA),
        'rk': rwkv_rk[0].reshape(1, DA), 'gnw': rwkv_gn_w[0].reshape(1, DA), 'gnb': rwkv_gn_b[0].reshape(1, DA),
        'w0': rwkv_w0[0], 'w2': rwkv_w2[0], 'a0': rwkv_a0[0], 'a2': rwkv_a2[0], 'g2': rwkv_g2[0],
        'ones_bd': _block_diag_ones(PACK_W, HEAD_A).astype(BF16),
        'conv': mlstm_conv[0], 'bi': mlstm_bi[0].reshape(1, 2 * H_B), 'bit': mlstm_bi[0].reshape(2 * H_B, 1),
        'bf': mlstm_bf[0].reshape(1, 2 * H_B), 'bft': mlstm_bf[0].reshape(2 * H_B, 1),
        'ng': mlstm_norm_g[0].reshape(1, DB),
        'e_rep': jnp.repeat(jnp.eye(N_GATES, dtype=BF16), LANES, axis=1),
        'router_w': router_w[0], 'router_b': router_b[0],
    }
    cc = jnp.concatenate([c_ctx[None, :], c, jnp.zeros((2 * SUBLANES - 1 - bs, D_MODEL), F32)], axis=0)
    mod = _ada_mod(cc, ada_w[0], ada_b[0])
    mod_p = mod[0:1].reshape(1, 1, 6 * D_MODEL)
    mod_s = mod[1:1 + bs].reshape(bs, 1, 6 * D_MODEL)

    x1p, hnp, idxp, gatesp, st = _mixers(x_prompt, mod_p, lp, None, None, None, None, False)
    x1s, hns, idxs, gatess, _ = _mixers(x_sample, mod_s, lp, state_rwkv[:, 0], state_mlstm_C[:, 0],
                                        state_mlstm_n[:, 0], state_mlstm_m[:, 0], True)

    n_p = bp * lp_len
    idx = jnp.concatenate([idxp, idxs], axis=0)
    n_tok = idx.shape[0]
    rank, counts = _expert_ranks(idx)
    counts = counts[0]
    padded = (counts + MOE_BLOCK - 1) // MOE_BLOCK * MOE_BLOCK
    pad_end = jnp.cumsum(padded)
    pad_start = pad_end - padded
    n_blocks = n_tok * TOP_K // MOE_BLOCK + N_EXPERTS
    block_start = jnp.arange(n_blocks, dtype=jnp.int32) * MOE_BLOCK
    block_e = jnp.minimum(jnp.sum(pad_end[None, :] <= block_start[:, None], axis=1), N_EXPERTS - 1).astype(jnp.int32)
    n_used = (pad_end[-1:] // MOE_BLOCK).astype(jnp.int32)
    expert_ids = jnp.arange(N_EXPERTS, dtype=jnp.int32)
    start_of = jnp.sum(jnp.where(idx[:, :, None] == expert_ids, pad_start.astype(jnp.int32), 0), axis=-1)
    dest = (start_of + rank).astype(jnp.int32).reshape(-1)
    xs = _dispatch(dest, pad_end.astype(jnp.int32), hnp, hns, n_blocks * MOE_BLOCK)
    has_rows = counts > 0
    later = jnp.where(has_rows[None, :] & (expert_ids[None, :] > expert_ids[:, None]), expert_ids[None, :], N_EXPERTS)
    next_with_rows = jnp.min(later, axis=1)
    next_of = jnp.where(next_with_rows < N_EXPERTS, next_with_rows, expert_ids)
    place = jnp.cumsum(has_rows.astype(jnp.int32)) - 1
    yb = _experts(block_e, n_used, next_of[block_e].astype(jnp.int32), (place[block_e] % 2).astype(jnp.int32), xs,
                  moe_w1[0], moe_b1[0], moe_w2[0], moe_b2[0])
    y_prompt = _combine(dest, yb, x1p, gatesp, mod_p, final_g, 0).reshape(x_prompt.shape)
    y_sample = _combine(dest, yb, x1s, gatess, mod_s, final_g, n_p).reshape(x_sample.shape)

    s_new, c_new, n_new, m_new = st
    return (y_prompt, y_sample, s_new[:, None], c_new[:, None], n_new[:, None], m_new[:, None])
```

```python
import functools

import jax
import jax.numpy as jnp
from jax import lax
from jax.experimental import pallas as pl
from jax.experimental.pallas import tpu as pltpu

F32 = jnp.float32
BF16 = jnp.bfloat16
HIGHEST = lax.Precision.HIGHEST

D_MODEL = 1024
DA = 512
HEAD_A = 64
H_A = DA // HEAD_A
DB = 512
H_B = 4
HEAD_B = DB // H_B
DECAY_LORA = 64
AAA_LORA = 64
GATE_LORA = 128
RWKV_COLS = 3 * DA + DECAY_LORA + AAA_LORA + GATE_LORA
MLSTM_MAIN = 4 * DB
N_GATES = 4 * H_B
CHUNK = 64
N_EXPERTS = 32
TOP_K = 4
D_FF = D_MODEL
SWIGLU_LIMIT = 7.0
SWIGLU_ALPHA = 1.702
MOE_BLOCK = 512
NORM_EPS = 1e-6
GN_EPS = 64e-5
ROW_TILE = 1024
PROJ_TILE = 512
SUBLANES = 8
LANES = 128
VMEM_LIMIT = 56 * 1024 * 1024
PACK = 4
PACK_W = PACK * HEAD_A
N_PACKS = H_A // PACK
SCAN_GROUP = 2
SCAN_DOUBLE_BUFFER_BYTES = 16 * 1024 * 1024

NN = (((1,), (0,)), ((), ()))
NT = (((1,), (1,)), ((), ()))
TN = (((0,), (0,)), ((), ()))


def _mm(a, b, precision=HIGHEST):
    return lax.dot_general(a, b, NN, precision=precision, preferred_element_type=F32)


def _dot_bf16(a, b, dims=NN):
    return lax.dot_general(a.astype(BF16), b.astype(BF16), dims, preferred_element_type=F32)


def _sigmoid(x):
    return 1.0 / (1.0 + jnp.exp(-x))


def _log_sigmoid(x):
    return jnp.minimum(x, 0.0) - jnp.log(1.0 + jnp.exp(-jnp.abs(x)))


def _params(*sem, **kw):
    return pltpu.CompilerParams(dimension_semantics=sem, vmem_limit_bytes=VMEM_LIMIT, **kw)


def _split_bf16(x):
    hi = x.astype(BF16)
    return hi, (x - hi.astype(F32)).astype(BF16)


def _const_spec(shape):
    nd = len(shape)
    return pl.BlockSpec(shape, lambda *_: (0,) * nd)


def _tri(n, reverse, strict):
    t = lax.broadcasted_iota(jnp.int32, (n, n), 0)
    s = lax.broadcasted_iota(jnp.int32, (n, n), 1)
    if reverse:
        return (s > t) if strict else (s >= t)
    return (s < t) if strict else (s <= t)


def _seq_neighbours(ref, b, c, nc, cols):
    r0 = pl.multiple_of(c * CHUNK, CHUNK)
    zc = ref[b, pl.ds(r0, CHUNK), cols]
    row = lax.broadcasted_iota(jnp.int32, zc.shape, 0)
    p0 = pl.multiple_of(jnp.maximum(r0 - SUBLANES, 0), SUBLANES)
    n0 = pl.multiple_of(jnp.minimum(r0 + CHUNK, (nc - 1) * CHUNK), SUBLANES)
    before = ref[b, pl.ds(p0, SUBLANES), cols][SUBLANES - 1:SUBLANES]
    after = ref[b, pl.ds(n0, SUBLANES), cols][0:1]
    before = jnp.where(c > 0, before, 0.0)
    after = jnp.where(c < nc - 1, after, 0.0)
    prev = jnp.where(row == 0, before, pltpu.roll(zc, 1, 0))
    nxt = jnp.where(row == CHUNK - 1, after, pltpu.roll(zc, CHUNK - 1, 0))
    return zc, prev, nxt


def _ada_kernel(c_ref, w_ref, b_ref, o_ref):
    cc = c_ref[...]
    o_ref[...] = _dot_bf16(cc * _sigmoid(cc), w_ref[...]) + b_ref[...]


def _ada_mod(cc, ada_w, ada_b):
    rows = cc.shape[0]
    ncol = ada_w.shape[1]
    tn = 1536
    return pl.pallas_call(
        _ada_kernel,
        grid=(ncol // tn,),
        in_specs=[_const_spec((rows, D_MODEL)),
                  pl.BlockSpec((D_MODEL, tn), lambda j: (0, j)),
                  pl.BlockSpec((1, tn), lambda j: (0, j))],
        out_specs=pl.BlockSpec((rows, tn), lambda j: (0, j)),
        out_shape=jax.ShapeDtypeStruct((rows, ncol), F32),
        compiler_params=_params("arbitrary"),
        name="ada_mod",
    )(cc, ada_w, ada_b.reshape(1, ncol))


def _inproj_kernel(x_ref, mod_ref, g_ref, wr_ref, wm_ref, wg_ref, wgt_ref, zr_ref, zm_ref, zg_ref, zgt_ref):
    x = x_ref[...]
    mod = mod_ref[0]
    sh, sc = mod[:, 0:D_MODEL], mod[:, D_MODEL:2 * D_MODEL]
    y = x * lax.rsqrt(jnp.mean(x * x, axis=-1, keepdims=True) + NORM_EPS)
    hn = y * g_ref[...] * (1.0 + sc) + sh
    hb = hn.astype(BF16)
    zr_ref[...] = lax.dot_general(hb, wr_ref[...], NN, preferred_element_type=F32)
    zm_ref[...] = lax.dot_general(hb, wm_ref[...], NN, preferred_element_type=F32)
    zg_ref[...] = _dot_bf16(hb, wg_ref[...])
    zgt_ref[...] = _dot_bf16(wgt_ref[...], hb, NT)


def _inproj(x2, mod, norm_g, w_r, w_m, w_g, w_gt):
    rows = x2.shape[0]
    tiles_per_mod = rows // mod.shape[0] // PROJ_TILE
    return pl.pallas_call(
        _inproj_kernel,
        grid=(rows // PROJ_TILE,),
        in_specs=[pl.BlockSpec((PROJ_TILE, D_MODEL), lambda i: (i, 0)),
                  pl.BlockSpec((1, 1, 6 * D_MODEL), lambda i: (i // tiles_per_mod, 0, 0)),
                  _const_spec((1, D_MODEL)),
                  _const_spec(w_r.shape), _const_spec(w_m.shape), _const_spec(w_g.shape),
                  _const_spec(w_gt.shape)],
        out_specs=[pl.BlockSpec((PROJ_TILE, RWKV_COLS), lambda i: (i, 0)),
                   pl.BlockSpec((PROJ_TILE, MLSTM_MAIN), lambda i: (i, 0)),
                   pl.BlockSpec((PROJ_TILE, N_GATES), lambda i: (i, 0)),
                   pl.BlockSpec((N_GATES, PROJ_TILE), lambda i: (0, i))],
        out_shape=[jax.ShapeDtypeStruct((rows, RWKV_COLS), F32),
                   jax.ShapeDtypeStruct((rows, MLSTM_MAIN), F32),
                   jax.ShapeDtypeStruct((rows, N_GATES), F32),
                   jax.ShapeDtypeStruct((N_GATES, rows), F32)],
        compiler_params=_params("arbitrary"),
        name="inproj",
    )(x2, mod, norm_g.reshape(1, D_MODEL), w_r, w_m, w_g, w_gt)


def _group_sum(x, ones_bd):
    hi = x.astype(BF16)
    lo = (x - hi.astype(F32)).astype(BF16)
    return (lax.dot_general(hi, ones_bd, NN, preferred_element_type=F32)
            + lax.dot_general(lo, ones_bd, NN, preferred_element_type=F32))


def _block_diag(x, ones_bd):
    return jnp.concatenate([x.astype(BF16)] * PACK, axis=0) * ones_bd


def _rwkv_body(zr_hbm, s0_ref, mu_ref, kkw_ref, ka_ref, rk_ref, gnw_ref, gnb_ref, w0_ref, w2_ref,
                 a0_ref, a2_ref, g2_ref, ones_ref, y_ref, sout_ref,
                 r_s, k_s, v_s, kk_s, gate_s, lx_s, st_s, zr_ref, zr_sem, *, seq_len, grid, group):
    nc = seq_len // CHUNK
    ones_bd = ones_ref[...]
    step, n_steps = pl.program_id(0), pl.num_programs(0)

    def fetch(s):
        return pltpu.make_async_copy(zr_hbm.at[pl.ds(s * group, group)], zr_ref, zr_sem)

    @pl.when(step == 0)
    def _():
        fetch(0).start()

    fetch(step).wait()

    def mix_chunk(c, carry):
        r0 = pl.multiple_of(c * CHUNK, CHUNK)
        for b in range(group):
            if grid:
                zc = zr_ref[b, pl.ds(r0, CHUNK), :]
                row = lax.broadcasted_iota(jnp.int32, zc.shape, 0)
                up0 = pl.multiple_of(jnp.maximum(c - 1, 0) * CHUNK, CHUNK)
                dn0 = pl.multiple_of(jnp.minimum(c + 1, nc - 1) * CHUNK, CHUNK)
                up = jnp.where(c > 0, zr_ref[b, pl.ds(up0, CHUNK), :], 0.0)
                down = jnp.where(c < nc - 1, zr_ref[b, pl.ds(dn0, CHUNK), :], 0.0)
                left = jnp.where(row == 0, 0.0, pltpu.roll(zc, 1, 0))
                right = jnp.where(row == CHUNK - 1, 0.0, pltpu.roll(zc, CHUNK - 1, 0))
                local = 0.25 * (up + down + left + right)
            else:
                zc, prev, nxt = _seq_neighbours(zr_ref, b, c, nc, slice(None))
                local = 0.5 * (prev + nxt)
            z = zc + (local - zc) * mu_ref[...]
            r, k, v = z[:, 0:DA], z[:, DA:2 * DA], z[:, 2 * DA:3 * DA]
            lx = z[:, 3 * DA:3 * DA + DECAY_LORA + AAA_LORA]
            xg = z[:, 3 * DA + DECAY_LORA + AAA_LORA:]
            kk = k * kkw_ref[...]
            sq = kk * kk
            ss = jnp.concatenate([_group_sum(sq[:, p * PACK_W:(p + 1) * PACK_W], ones_bd) for p in range(N_PACKS)],
                                 axis=1)
            kk = kk / jnp.maximum(jnp.sqrt(ss), 1e-12)
            lane = lax.broadcasted_iota(jnp.int32, lx.shape, 1)
            r_s[b, pl.ds(r0, CHUNK), :] = r
            k_s[b, pl.ds(r0, CHUNK), :] = k
            v_s[b, pl.ds(r0, CHUNK), :] = v
            kk_s[b, pl.ds(r0, CHUNK), :] = kk
            gate_s[b, pl.ds(r0, CHUNK), :] = _dot_bf16(_sigmoid(xg), g2_ref[...])
            lx_s[b, pl.ds(r0, CHUNK), :] = jnp.where(lane < DECAY_LORA, jnp.tanh(lx), lx)
        return carry

    lax.fori_loop(0, nc, mix_chunk, 0)

    @pl.when(step + 1 < n_steps)
    def _():
        fetch(step + 1).start()

    rowblk = lax.broadcasted_iota(jnp.int32, (PACK_W, PACK_W), 0) // HEAD_A
    colblk = lax.broadcasted_iota(jnp.int32, (PACK_W, PACK_W), 1) // HEAD_A
    for b in range(group):
        for d in range(2):
            for p in range(N_PACKS):
                if s0_ref is None:
                    st_s[b, d, p] = jnp.zeros((PACK_W, PACK_W), F32)
                    continue
                rows_ = jnp.concatenate([s0_ref[b, d, p * PACK + h] for h in range(PACK)], axis=0)
                st_s[b, d, p] = jnp.where(rowblk == colblk, jnp.concatenate([rows_] * PACK, axis=1), 0.0)

    t_idx = lax.broadcasted_iota(jnp.int32, (CHUNK, PACK_W), 0)
    s_idx = lax.broadcasted_iota(jnp.int32, (CHUNK, PACK_W), 1) % CHUNK

    def stages(i):
        first_touch = i < nc // 2
        dot = functools.partial(lax.dot_general, preferred_element_type=F32)
        chains, out_sel = [], []
        for b, d in [(b, d) for b in range(group) for d in range(2)]:
            reverse = d == 1
            c = nc - 1 - i if reverse else i
            rows = pl.ds(pl.multiple_of(c * CHUNK, CHUNK), CHUNK)
            out_sel.append((b, rows))
            strict = (s_idx > t_idx) if reverse else (s_idx < t_idx)
            incl = (s_idx >= t_idx) if reverse else (s_idx <= t_idx)
            last = 0 if reverse else CHUNK - 1
            r, k, v, kk = r_s[b, rows, :], k_s[b, rows, :], v_s[b, rows, :], kk_s[b, rows, :]
            lx = lx_s[b, rows, :]
            wl = w0_ref[d:d + 1, :] + _dot_bf16(lx[:, 0:DECAY_LORA], w2_ref[d])
            logw = -jnp.exp(_log_sigmoid(wl) - 0.5)
            a = _sigmoid(a0_ref[d:d + 1, :] + _dot_bf16(lx[:, DECAY_LORA:], a2_ref[d]))
            kd = k * (1.0 + (a - 1.0) * ka_ref[...])
            kb = kk * a
            tri = _tri(CHUNK, reverse, False).astype(BF16)
            w1, w_rest = _split_bf16(logw)
            w2, w3 = _split_bf16(w_rest)
            cum = dot(tri, w1, NN) + dot(tri, w2, NN) + dot(tri, w3, NN)
            clast = cum[last:last + 1]
            rt = r * jnp.exp(cum)
            at = -kk * jnp.exp(cum - logw)
            einv = jnp.exp(-cum)
            kt, bt = kd * einv, kb * einv
            edec = jnp.exp(clast - cum)
            kp, bp = kd * edec, kb * edec
            ptot = jnp.exp(clast)
            rkd = r * kd * rk_ref[...]
            for p in range(N_PACKS):
                cs = slice(p * PACK_W, (p + 1) * PACK_W)
                chains.append(dict(
                    out=len(out_sel) - 1, st=(b, d, p), cs=cs, strict=strict, incl=incl, v=v[:, cs],
                    rkd=rkd[:, cs], ptot=ptot[:, cs],
                    ar=jnp.concatenate([at[:, cs], rt[:, cs]], axis=0).astype(BF16),
                    kt_bd=_block_diag(kt[:, cs], ones_bd), bt_bd=_block_diag(bt[:, cs], ones_bd),
                    v_bd=_block_diag(v[:, cs], ones_bd),
                    kbp=jnp.concatenate([kp[:, cs], bp[:, cs]], axis=0)))
            yield

        for ch in chains:
            ch['s_prev'] = st_s[ch['st']]
            ch['a_k'] = dot(ch['ar'], ch['kt_bd'], NT)
            ch['a_b'] = dot(ch['ar'], ch['bt_bd'], NT)
            ch['ars'] = dot(ch['ar'], ch['s_prev'].astype(BF16), NT)
        yield
        for ch in chains:
            a_ak = jnp.where(ch['strict'], ch['a_k'][:CHUNK], 0.0)
            a_rk = jnp.where(ch['incl'], ch['a_k'][CHUNK:], 0.0)
            ch['a_rb'] = jnp.where(ch['incl'], ch['a_b'][CHUNK:], 0.0).astype(BF16)
            n = jnp.where(ch['strict'], ch['a_b'][:CHUNK], 0.0)
            ch['n_hi'], n_lo = _split_bf16(n)
            ch['n_both'] = jnp.concatenate([ch['n_hi'], n_lo], axis=0)
            ch['pw'] = ch['n_hi']
            ch['m'] = jnp.where(s_idx == t_idx, 1.0, 0.0) + n
            akv = dot(jnp.concatenate([a_ak, a_rk], axis=0).astype(BF16), ch['v_bd'], NN)
            ch['rhs'] = ch['ars'][:CHUNK] + akv[:CHUNK]
            ch['y'] = ch['ars'][CHUNK:] + akv[CHUNK:]
        yield
        for ch in chains:
            ch['pw'] = dot(ch['pw'], _block_diag(ch['pw'], ones_bd), NN).astype(BF16)
        yield
        for j in range(1, 5):
            for ch in chains:
                pw_bd = _block_diag(ch['pw'], ones_bd)
                if j < 4:
                    both = dot(jnp.concatenate([ch['m'].astype(BF16), ch['pw']], axis=0), pw_bd, NN)
                    ch['m'] = ch['m'] + both[:CHUNK]
                    ch['pw'] = both[CHUNK:].astype(BF16)
                else:
                    ch['m'] = ch['m'] + dot(ch['m'].astype(BF16), pw_bd, NN)
            yield
        for ch in chains:
            ch['mb'] = ch['m'].astype(BF16)
            ch['u'] = dot(ch['mb'], _block_diag(ch['rhs'], ones_bd), NN)
        yield
        for ch in chains:
            u_hi, u_lo = _split_bf16(ch['u'])
            both = dot(ch['n_both'], _block_diag(u_hi, ones_bd), NN)
            nu = both[:CHUNK] + both[CHUNK:] + dot(ch['n_hi'], _block_diag(u_lo, ones_bd), NN)
            ch['resid'] = ch['rhs'] - ch['u'] + nu
        yield
        for ch in chains:
            ch['u'] = ch['u'] + dot(ch['mb'], _block_diag(ch['resid'], ones_bd), NN)
        yield
        for ch in chains:
            u = ch['u']
            ch['y'] = ch['y'] + dot(ch['a_rb'], _block_diag(u, ones_bd), NN)
            vu_hi, vu_lo = _split_bf16(jnp.concatenate([ch['v'], u], axis=0))
            kbp_hi, kbp_lo = _split_bf16(ch['kbp'])
            grown = dot(vu_hi, kbp_hi, TN) + dot(vu_hi, kbp_lo, TN) + dot(vu_lo, kbp_hi, TN)
            st_s[ch['st']] = ch['s_prev'] * ch['ptot'] + grown * ones_bd.astype(F32)
        yield
        n_ch = len(chains)
        sums = dot(jnp.concatenate([ch['y'] for ch in chains] + [ch['rkd'] for ch in chains],
                                   axis=0).astype(BF16), ones_bd, NN)
        ycs = [ch['y'] - sums[q * CHUNK:(q + 1) * CHUNK] * (1.0 / HEAD_A) for q, ch in enumerate(chains)]
        var = dot(jnp.concatenate([yc * yc for yc in ycs], axis=0).astype(BF16), ones_bd, NN) * (1.0 / HEAD_A)
        outs = [[] for _ in out_sel]
        for q, ch in enumerate(chains):
            cs = ch['cs']
            out = ycs[q] * lax.rsqrt(var[q * CHUNK:(q + 1) * CHUNK] + GN_EPS) * gnw_ref[:, cs] + gnb_ref[:, cs]
            outs[ch['out']].append(out + sums[(n_ch + q) * CHUNK:(n_ch + q + 1) * CHUNK] * ch['v'])
        for (b, rows), parts in zip(out_sel, outs):
            out = jnp.concatenate(parts, axis=1)

            @pl.when(first_touch)
            def _(b=b, rows=rows, out=out):
                y_ref[b, rows, :] = out

            @pl.when(jnp.logical_not(first_touch))
            def _(b=b, rows=rows, out=out):
                y_ref[b, rows, :] = (y_ref[b, rows, :] + out) * gate_s[b, rows, :]

    def finish():
        for b in range(group):
            for d in range(2):
                for p in range(N_PACKS):
                    s_fin = st_s[b, d, p]
                    for h in range(PACK):
                        sout_ref[b, d, p * PACK + h] = s_fin[h * HEAD_A:(h + 1) * HEAD_A,
                                                             h * HEAD_A:(h + 1) * HEAD_A]

    return stages, finish


def _mlstm_body(zm_hbm, zvo_ref, g_ref, gt_ref, c0_ref, n0_ref, m0_ref, conv_ref, bi_ref, bit_ref, bf_ref, bft_ref,
                  ng_ref, erep_ref, y_ref, cout_ref, nout_ref, mout_ref,
                  qk_s, h_s, c_s, n_s, m_s, zqk_ref, zqk_sem, *, seq_len, group):
    nc = seq_len // CHUNK
    step, n_steps = pl.program_id(0), pl.num_programs(0)

    def fetch(s):
        return pltpu.make_async_copy(zm_hbm.at[pl.ds(s * group, group), :, pl.ds(0, 2 * DB)], zqk_ref, zqk_sem)

    @pl.when(step == 0)
    def _():
        fetch(0).start()

    fetch(step).wait()

    def conv_chunk(c, carry):
        for b in range(group):
            zc, prev, nxt = _seq_neighbours(zqk_ref, b, c, nc, slice(None))
            u = conv_ref[0:1, :] * prev + conv_ref[1:2, :] * zc + conv_ref[2:3, :] * nxt
            qk = u * _sigmoid(u)
            lane = lax.broadcasted_iota(jnp.int32, qk.shape, 1)
            qk_s[b, pl.ds(pl.multiple_of(c * CHUNK, CHUNK), CHUNK), :] = jnp.where(lane >= DB,
                                                                                  qk * (HEAD_B ** -0.5), qk)
        return carry

    lax.fori_loop(0, nc, conv_chunk, 0)

    @pl.when(step + 1 < n_steps)
    def _():
        fetch(step + 1).start()

    for b in range(group):
        for d in range(2):
            if c0_ref is None:
                c_s[b, d] = jnp.zeros((H_B, HEAD_B, HEAD_B), F32)
                n_s[b, d] = jnp.zeros((H_B, HEAD_B), F32)
                m_s[b, d] = jnp.zeros((H_B, LANES), F32)
                continue
            c_s[b, d] = c0_ref[b, d]
            n_s[b, d] = n0_ref[b, d]
            m_s[b, d] = jnp.broadcast_to(m0_ref[b, d], (H_B, LANES))

    e_rep = erep_ref[...]
    ones_tl = jnp.ones((CHUNK, LANES), BF16)
    row_id = lax.broadcasted_iota(jnp.int32, (CHUNK, LANES), 0)

    def scan_max(x, reverse):
        shift = 1
        while shift < CHUNK:
            if reverse:
                moved = jnp.where(row_id < CHUNK - shift, pltpu.roll(x, CHUNK - shift, 0), -jnp.inf)
            else:
                moved = jnp.where(row_id >= shift, pltpu.roll(x, shift, 0), -jnp.inf)
            x = jnp.maximum(x, moved)
            shift *= 2
        return x

    def dot3(a, b, dims):
        a_hi, a_lo = _split_bf16(a)
        b_hi, b_lo = _split_bf16(b)
        dg = functools.partial(lax.dot_general, dimension_numbers=dims, preferred_element_type=F32)
        return dg(a_hi, b_hi) + dg(a_hi, b_lo) + dg(a_lo, b_hi)

    def stages(i):
        dot = functools.partial(lax.dot_general, preferred_element_type=F32)
        chains = []
        for b, d in [(b, d) for b in range(group) for d in range(2)]:
            reverse = d == 1
            incl = _tri(CHUNK, reverse, False)
            tri_b = incl.astype(BF16)
            tri_row = _tri(CHUNK, not reverse, False).astype(F32)
            last = 0 if reverse else CHUNK - 1
            gsl = slice(d * H_B, (d + 1) * H_B)
            fsl = slice(2 * H_B + d * H_B, 2 * H_B + (d + 1) * H_B)
            c = nc - 1 - i if reverse else i
            rows = pl.ds(pl.multiple_of(c * CHUNK, CHUNK), CHUNK)
            g1, g_rest = _split_bf16(g_ref[b, rows, :])
            g2, g3 = _split_bf16(g_rest)
            g_rep = dot(g1, e_rep, NN) + dot(g2, e_rep, NN) + dot(g3, e_rep, NN)
            grow = gt_ref[b, c]
            i_row = grow[gsl, :] + bit_ref[gsl, :]
            f_row = _log_sigmoid(grow[fsl, :] + bft_ref[gsl, :])
            b_row = _mm(f_row, tri_row)
            for h in range(H_B):
                ji, jf = d * H_B + h, 2 * H_B + d * H_B + h
                ic = g_rep[:, ji * LANES:(ji + 1) * LANES] + bi_ref[:, ji:ji + 1]
                fc = _log_sigmoid(g_rep[:, jf * LANES:(jf + 1) * LANES] + bf_ref[:, ji:ji + 1])
                f1, f_rest = _split_bf16(fc)
                f2, f3 = _split_bf16(f_rest)
                bc = dot(tri_b, f1, NN) + dot(tri_b, f2, NN) + dot(tri_b, f3, NN)
                q = qk_s[b, rows, h * HEAD_B:(h + 1) * HEAD_B]
                chains.append(dict(
                    b=b, d=d, h=h, rows=rows, incl=incl, reverse=reverse, q=q, qb=q.astype(BF16),
                    k=qk_s[b, rows, DB + h * HEAD_B:DB + (h + 1) * HEAD_B],
                    v=zvo_ref[b, rows, h * HEAD_B:(h + 1) * HEAD_B],
                    bc=bc, ic=ic, bl=bc[last:last + 1, :], br=b_row[h:h + 1, :], ir=i_row[h:h + 1, :]))
            yield
        for q, ch in enumerate(chains):
            b, d, h = ch['b'], ch['d'], ch['h']
            ch['qk'] = _dot_bf16(ch['qb'], ch['k'], NT)
            ch['c_prev'] = c_s[b, d, h]
            ch['qc'] = _dot_bf16(ch['qb'], ch['c_prev'])
            ch['n_prev'] = n_s[b, d, h:h + 1, :]
            ch['qn'] = dot3(ch['q'], jnp.broadcast_to(ch['n_prev'], (HEAD_B, HEAD_B)), NT)
            if q % 2 == 1:
                yield
        for q, ch in enumerate(chains):
            b, d, h = ch['b'], ch['d'], ch['h']
            bc, ic, bl = ch['bc'], ch['ic'], ch['bl']
            m_prev = m_s[b, d, h:h + 1, :]
            m_t = bc + jnp.maximum(m_prev, scan_max(ic - bc, ch['reverse']))
            dmat = jnp.where(ch['incl'], bc[:, :CHUNK] - ch['br'] + ch['ir'], -jnp.inf)
            s = ch['qk'] * jnp.exp(dmat - m_t[:, :CHUNK])
            carry_w = jnp.exp(bc + m_prev - m_t)
            s_hi, s_lo = _split_bf16(s)
            den = dot(s_hi, ones_tl, NN) + dot(s_lo, ones_tl, NN) + carry_w * ch['qn']
            num = dot(s_hi, ch['v'].astype(BF16), NN) + carry_w * ch['qc']
            h_s[b, d, ch['rows'], h * HEAD_B:(h + 1) * HEAD_B] = num / jnp.maximum(jnp.abs(den), jnp.exp(-m_t))
            g = bl - bc + ic
            m_new = jnp.maximum(bl + m_prev, jnp.max(g, axis=0, keepdims=True))
            wk = jnp.exp(g - m_new)
            decay = jnp.exp(bl + m_prev - m_new)
            c_s[b, d, h] = decay * ch['c_prev'] + _dot_bf16(ch['k'], wk * ch['v'], TN)
            n_s[b, d, h:h + 1, :] = decay * ch['n_prev'] + jnp.sum(wk * ch['k'], axis=0, keepdims=True)
            m_s[b, d, h:h + 1, :] = m_new
            if q % 2 == 1:
                yield

    def norm_chunk(c, carry):
        rows = pl.ds(pl.multiple_of(c * CHUNK, CHUNK), CHUNK)
        for b in range(group):
            og = zvo_ref[b, rows, DB:2 * DB]
            hm = h_s[b, 0, rows, :] + h_s[b, 1, rows, :]
            parts = []
            for h in range(H_B):
                hh = hm[:, h * HEAD_B:(h + 1) * HEAD_B]
                parts.append(hh * lax.rsqrt(jnp.mean(hh * hh, axis=-1, keepdims=True) + NORM_EPS))
            y_ref[b, rows, :] = jnp.concatenate(parts, axis=1) * ng_ref[...] * _sigmoid(og)
        return carry

    def finish():
        cout_ref[...] = c_s[...]
        nout_ref[...] = n_s[...]
        mout_ref[...] = m_s[...]
        lax.fori_loop(0, nc, norm_chunk, 0)

    return stages, finish


def _scan_kernel(*refs, body, n_steps, absent=(), **static):
    refs = list(refs)
    for pos in absent:
        refs.insert(pos, None)
    stages, finish = body(*refs, **static)

    def scan_step(i, carry):
        for _ in stages(i):
            pass
        return carry

    lax.fori_loop(0, n_steps, scan_step, 0)
    finish()


def _scan_specs(grp):
    lead = lambda *rest: pl.BlockSpec((grp,) + rest, lambda b: (b,) + (0,) * len(rest))

    def big(seq_len, cols, col_block):
        single = 2 * grp * seq_len * cols * 4 >= SCAN_DOUBLE_BUFFER_BYTES
        return pl.BlockSpec((grp, seq_len, cols), lambda b: (b, 0, col_block),
                            **(dict(pipeline_mode=pl.Buffered(1)) if single else {}))

    return lead, big


def _rwkv(zr, s0, p, grid):
    bsz, seq_len, _ = zr.shape
    nc = seq_len // CHUNK
    grp = 2 * SCAN_GROUP if 4 * SCAN_GROUP * seq_len * RWKV_COLS * 4 <= SCAN_DOUBLE_BUFFER_BYTES else SCAN_GROUP
    assert nc % 2 == 0 and bsz % grp == 0
    states = [] if s0 is None else [s0]
    kern = functools.partial(_scan_kernel, body=_rwkv_body, n_steps=nc, seq_len=seq_len, grid=grid, group=grp,
                             absent=(1,) if s0 is None else ())
    consts = [p['mu'], p['kkw'], p['ka'], p['rk'], p['gnw'], p['gnb'], p['w0'], p['w2'], p['a0'], p['a2'],
              p['g2'], p['ones_bd']]
    lead, big = _scan_specs(grp)
    return pl.pallas_call(
        kern,
        grid=(bsz // grp,),
        in_specs=[pl.BlockSpec(memory_space=pl.ANY)] + [lead(2, H_A, HEAD_A, HEAD_A)] * len(states)
                 + [_const_spec(a.shape) for a in consts],
        out_specs=[lead(seq_len, DA), lead(2, H_A, HEAD_A, HEAD_A)],
        out_shape=[jax.ShapeDtypeStruct((bsz, seq_len, DA), F32),
                   jax.ShapeDtypeStruct((bsz, 2, H_A, HEAD_A, HEAD_A), F32)],
        scratch_shapes=[pltpu.VMEM((grp, seq_len, DA), F32)] * 5
                       + [pltpu.VMEM((grp, seq_len, DECAY_LORA + AAA_LORA), F32),
                          pltpu.VMEM((grp, 2, N_PACKS, PACK_W, PACK_W), F32),
                          pltpu.VMEM((grp, seq_len, RWKV_COLS), F32), pltpu.SemaphoreType.DMA(())],
        compiler_params=_params("arbitrary"),
        name="rwkv_scan",
    )(zr, *states, *consts)


def _mlstm(zm, g, gt, c0, n0, m0, p):
    bsz, seq_len, _ = zm.shape
    nc = seq_len // CHUNK
    grp = 2 * SCAN_GROUP if 4 * SCAN_GROUP * seq_len * MLSTM_MAIN * 4 <= SCAN_DOUBLE_BUFFER_BYTES else SCAN_GROUP
    assert bsz % grp == 0
    zero_state = c0 is None
    states = [] if zero_state else [c0, n0, m0.reshape(bsz, 2, H_B, 1)]
    kern = functools.partial(_scan_kernel, body=_mlstm_body, n_steps=nc, seq_len=seq_len, group=grp,
                             absent=(4, 5, 6) if zero_state else ())
    consts = [p['conv'], p['bi'], p['bit'], p['bf'], p['bft'], p['ng'], p['e_rep']]
    lead, big = _scan_specs(grp)
    state_specs = [] if zero_state else [lead(2, H_B, HEAD_B, HEAD_B), lead(2, H_B, HEAD_B), lead(2, H_B, 1)]
    vo_spec = big(seq_len, 2 * DB, 1)
    return pl.pallas_call(
        kern,
        grid=(bsz // grp,),
        in_specs=[pl.BlockSpec(memory_space=pl.ANY), vo_spec, lead(seq_len, N_GATES), lead(nc, N_GATES, CHUNK)]
                 + state_specs + [_const_spec(a.shape) for a in consts],
        out_specs=[lead(seq_len, DB), lead(2, H_B, HEAD_B, HEAD_B), lead(2, H_B, HEAD_B), lead(2, H_B, LANES)],
        out_shape=[jax.ShapeDtypeStruct((bsz, seq_len, DB), F32),
                   jax.ShapeDtypeStruct((bsz, 2, H_B, HEAD_B, HEAD_B), F32),
                   jax.ShapeDtypeStruct((bsz, 2, H_B, HEAD_B), F32),
                   jax.ShapeDtypeStruct((bsz, 2, H_B, LANES), F32)],
        scratch_shapes=[pltpu.VMEM((grp, seq_len, 2 * DB), F32), pltpu.VMEM((grp, 2, seq_len, DB), F32),
                        pltpu.VMEM((grp, 2, H_B, HEAD_B, HEAD_B), F32), pltpu.VMEM((grp, 2, H_B, HEAD_B), F32),
                        pltpu.VMEM((grp, 2, H_B, LANES), F32),
                        pltpu.VMEM((grp, seq_len, 2 * DB), F32), pltpu.SemaphoreType.DMA(())],
        compiler_params=_params("arbitrary"),
        name="mlstm_scan",
    )(zm, zm, g, gt, *states, *consts)


def _outproj_kernel(x_ref, yr_ref, ym_ref, mod_ref, g_ref, wo_ref, rw_ref, rb_ref,
                    x1_ref, hn_ref, idx_ref, gate_ref):
    mod = mod_ref[0]
    g1 = mod[:, 2 * D_MODEL:3 * D_MODEL]
    sh2, sc2 = mod[:, 3 * D_MODEL:4 * D_MODEL], mod[:, 4 * D_MODEL:5 * D_MODEL]
    mix = _dot_bf16(yr_ref[...], wo_ref[0:DA, :]) + _dot_bf16(ym_ref[...], wo_ref[DA:, :])
    x1 = x_ref[...] + g1 * mix
    x1_ref[...] = x1
    y = x1 * lax.rsqrt(jnp.mean(x1 * x1, axis=-1, keepdims=True) + NORM_EPS)
    hn = y * g_ref[...] * (1.0 + sc2) + sh2
    hn_ref[...] = hn
    logits = _dot_bf16(hn, rw_ref[...]) + rb_ref[...]
    lane = lax.broadcasted_iota(jnp.int32, logits.shape, 1)
    vals, idxs = [], []
    for _ in range(TOP_K):
        top = jnp.max(logits, axis=-1, keepdims=True)
        pick = jnp.min(jnp.where(logits == top, lane, N_EXPERTS), axis=-1, keepdims=True)
        vals.append(top)
        idxs.append(pick)
        logits = jnp.where(lane == pick, -jnp.inf, logits)
    exps = [jnp.exp(v - vals[0]) for v in vals]
    total = exps[0] + exps[1] + exps[2] + exps[3]
    k_lane = lax.broadcasted_iota(jnp.int32, (x1.shape[0], TOP_K), 1)
    gates = jnp.zeros((x1.shape[0], TOP_K), F32)
    picks = jnp.zeros((x1.shape[0], TOP_K), jnp.int32)
    for j in range(TOP_K):
        gates = jnp.where(k_lane == j, exps[j] / total, gates)
        picks = jnp.where(k_lane == j, idxs[j], picks)
    idx_ref[...] = picks
    gate_ref[...] = gates


def _outproj(x2, yr, ym, mod, norm_g, w_out, router_w, router_b):
    rows = x2.shape[0]
    tiles_per_mod = rows // mod.shape[0] // PROJ_TILE
    row_spec = lambda w: pl.BlockSpec((PROJ_TILE, w), lambda i: (i, 0))
    return pl.pallas_call(
        _outproj_kernel,
        grid=(rows // PROJ_TILE,),
        in_specs=[row_spec(D_MODEL), row_spec(DA), row_spec(DB),
                  pl.BlockSpec((1, 1, 6 * D_MODEL), lambda i: (i // tiles_per_mod, 0, 0)),
                  _const_spec((1, D_MODEL)), _const_spec(w_out.shape), _const_spec(router_w.shape),
                  _const_spec((1, N_EXPERTS))],
        out_specs=[row_spec(D_MODEL), row_spec(D_MODEL), row_spec(TOP_K), row_spec(TOP_K)],
        out_shape=[jax.ShapeDtypeStruct((rows, D_MODEL), F32), jax.ShapeDtypeStruct((rows, D_MODEL), F32),
                   jax.ShapeDtypeStruct((rows, TOP_K), jnp.int32), jax.ShapeDtypeStruct((rows, TOP_K), F32)],
        compiler_params=_params("arbitrary"),
        name="outproj_router",
    )(x2, yr, ym, mod, norm_g.reshape(1, D_MODEL), w_out, router_w, router_b.reshape(1, N_EXPERTS))


def _rank_kernel(idx_ref, rank_ref, count_ref, run_s):
    i = pl.program_id(0)

    @pl.when(i == 0)
    def _():
        run_s[...] = jnp.zeros_like(run_s)

    idx = idx_ref[...]
    rows = idx.shape[0]
    lane = lax.broadcasted_iota(jnp.int32, (rows, N_EXPERTS), 1)
    hot = jnp.zeros((rows, N_EXPERTS), F32)
    for j in range(TOP_K):
        hot = hot + jnp.where(lane == idx[:, j:j + 1], 1.0, 0.0)
    before = _dot_bf16(_tri(rows, False, True).astype(F32), hot) + run_s[...]
    k_lane = lax.broadcasted_iota(jnp.int32, (rows, TOP_K), 1)
    rank = jnp.zeros((rows, TOP_K), F32)
    for j in range(TOP_K):
        rj = jnp.sum(jnp.where(lane == idx[:, j:j + 1], before, 0.0), axis=-1, keepdims=True)
        rank = jnp.where(k_lane == j, rj, rank)
    rank_ref[...] = rank.astype(jnp.int32)
    run_s[...] = run_s[...] + jnp.sum(hot, axis=0, keepdims=True)
    count_ref[...] = run_s[...].astype(jnp.int32)


def _expert_ranks(idx):
    rows = idx.shape[0]
    return pl.pallas_call(
        _rank_kernel,
        grid=(rows // ROW_TILE,),
        in_specs=[pl.BlockSpec((ROW_TILE, TOP_K), lambda i: (i, 0))],
        out_specs=[pl.BlockSpec((ROW_TILE, TOP_K), lambda i: (i, 0)), _const_spec((1, N_EXPERTS))],
        out_shape=[jax.ShapeDtypeStruct((rows, TOP_K), jnp.int32),
                   jax.ShapeDtypeStruct((1, N_EXPERTS), jnp.int32)],
        scratch_shapes=[pltpu.VMEM((1, N_EXPERTS), F32)],
        compiler_params=_params("arbitrary"),
        name="expert_ranks",
    )(idx)


def _dispatch_kernel(dest_ref, pend_ref, xa_ref, xb_ref, xs_ref, zero_s, sem, *, n_first_tiles):
    @pl.when(pl.program_id(0) == 0)
    def _():
        zero_s[...] = jnp.zeros_like(zero_s)

        def zero_block(start):
            return pltpu.make_async_copy(zero_s, xs_ref.at[pl.ds(pl.multiple_of(start, MOE_BLOCK), MOE_BLOCK)], sem)

        def has_rows(e):
            return pend_ref[e] > (pend_ref[e - 1] if e else 0)

        used_end = pend_ref[N_EXPERTS - 1]
        n_tail = (xs_ref.shape[0] - used_end) // MOE_BLOCK

        def start_tail(b, carry):
            zero_block(used_end + b * MOE_BLOCK).start()
            return carry

        def wait_tail(b, carry):
            zero_block(0).wait()
            return carry

        for e in range(N_EXPERTS):
            @pl.when(has_rows(e))
            def _(e=e):
                zero_block(pend_ref[e] - MOE_BLOCK).start()
        lax.fori_loop(0, n_tail, start_tail, 0)
        for e in range(N_EXPERTS):
            @pl.when(has_rows(e))
            def _():
                zero_block(0).wait()
        lax.fori_loop(0, n_tail, wait_tail, 0)

    def issue_from(x_ref):
        def issue(g, carry):
            for q in range(SUBLANES):
                for j in range(TOP_K):
                    slot = dest_ref[g * (SUBLANES * TOP_K) + q * TOP_K + j]
                    pltpu.make_async_copy(x_ref.at[g, pl.ds(q, 1)], xs_ref.at[pl.ds(slot, 1)],
                                          sem).start(priority=j % 2)
            return carry
        lax.fori_loop(0, ROW_TILE // SUBLANES, issue, 0)

    pl.when(pl.program_id(0) < n_first_tiles)(lambda: issue_from(xa_ref))
    pl.when(pl.program_id(0) >= n_first_tiles)(lambda: issue_from(xb_ref))
    all_rows = xs_ref.at[pl.ds(0, ROW_TILE * TOP_K)]
    pltpu.make_async_copy(all_rows, all_rows, sem).wait()


def _dispatch(dest_flat, pad_end, xa, xb, n_slots):
    na, nb = xa.shape[0] // ROW_TILE, xb.shape[0] // ROW_TILE
    tiles = lambda x: x.reshape(x.shape[0] // SUBLANES, SUBLANES, D_MODEL)
    block = (ROW_TILE // SUBLANES, SUBLANES, D_MODEL)
    return pl.pallas_call(
        functools.partial(_dispatch_kernel, n_first_tiles=na),
        grid=(na + nb,),
        in_specs=[pl.BlockSpec((ROW_TILE * TOP_K,), lambda i: (i,), memory_space=pltpu.SMEM),
                  pl.BlockSpec(memory_space=pltpu.SMEM),
                  pl.BlockSpec(block, lambda i: (jnp.minimum(i, na - 1), 0, 0)),
                  pl.BlockSpec(block, lambda i: (jnp.maximum(i - na, 0), 0, 0))],
        out_specs=pl.BlockSpec(memory_space=pl.ANY),
        out_shape=jax.ShapeDtypeStruct((n_slots, D_MODEL), xa.dtype),
        scratch_shapes=[pltpu.VMEM((MOE_BLOCK, D_MODEL), F32), pltpu.SemaphoreType.DMA(())],
        compiler_params=_params("arbitrary", disable_bounds_checks=True),
        name="moe_dispatch",
    )(dest_flat, pad_end, tiles(xa), tiles(xb))


def _expert_kernel(be_ref, nb_ref, next_ref, par_ref, xs_ref, w1_hbm, b1_ref, w2_hbm, b2_ref, y_ref,
                   w1_f, w2_f, w1_s, w2_s, sem1, sem2):
    i = pl.program_id(0)
    used = i < nb_ref[0]
    changed = jnp.logical_or(i == 0, be_ref[i] != be_ref[jnp.maximum(i - 1, 0)])

    def fetch(e, slot):
        return (pltpu.make_async_copy(w1_hbm.at[e], w1_f.at[slot], sem1.at[slot]),
                pltpu.make_async_copy(w2_hbm.at[e], w2_f.at[slot], sem2.at[slot]))

    @pl.when(i == 0)
    def _():
        for copy in fetch(be_ref[0], 0):
            copy.start()

    @pl.when(jnp.logical_and(changed, used))
    def _():
        slot = par_ref[i]
        for copy in fetch(be_ref[i], slot):
            copy.wait()

        @pl.when(next_ref[i] != be_ref[i])
        def _():
            for copy in fetch(next_ref[i], 1 - slot):
                copy.start()

        w1_s[...] = w1_f[slot].astype(BF16)
        w2_s[...] = w2_f[slot].astype(BF16)

    @pl.when(used)
    def _():
        xb = xs_ref[...].astype(BF16)
        hu = lax.dot_general(xb, w1_s[...], NN, preferred_element_type=F32) + b1_ref[0]
        glu = jnp.minimum(hu[:, :D_FF], SWIGLU_LIMIT)
        lin = jnp.clip(hu[:, D_FF:], -SWIGLU_LIMIT, SWIGLU_LIMIT)
        act = glu * _sigmoid(SWIGLU_ALPHA * glu) * (lin + 1.0)
        y_ref[...] = lax.dot_general(act.astype(BF16), w2_s[...], NN, preferred_element_type=F32) + b2_ref[0]

    @pl.when(i >= nb_ref[0])
    def _():
        y_ref[...] = jnp.zeros_like(y_ref)


def _experts(block_e, n_used, next_e, parity, xs, w1, b1, w2, b2):
    n_blocks = xs.shape[0] // MOE_BLOCK
    grid_spec = pltpu.PrefetchScalarGridSpec(
        num_scalar_prefetch=4,
        grid=(n_blocks,),
        in_specs=[pl.BlockSpec((MOE_BLOCK, D_MODEL), lambda i, be, nb, nx, pr: (jnp.minimum(i, nb[0] - 1), 0)),
                  pl.BlockSpec(memory_space=pl.ANY),
                  pl.BlockSpec((1, 1, 2 * D_FF), lambda i, be, nb, nx, pr: (be[i], 0, 0)),
                  pl.BlockSpec(memory_space=pl.ANY),
                  pl.BlockSpec((1, 1, D_MODEL), lambda i, be, nb, nx, pr: (be[i], 0, 0))],
        out_specs=pl.BlockSpec((MOE_BLOCK, D_MODEL), lambda i, be, nb, nx, pr: (i, 0)),
        scratch_shapes=[pltpu.VMEM((2, D_MODEL, 2 * D_FF), F32), pltpu.VMEM((2, D_FF, D_MODEL), F32),
                        pltpu.VMEM((D_MODEL, 2 * D_FF), BF16), pltpu.VMEM((D_FF, D_MODEL), BF16),
                        pltpu.SemaphoreType.DMA((2,)), pltpu.SemaphoreType.DMA((2,))],
    )
    return pl.pallas_call(
        _expert_kernel,
        grid_spec=grid_spec,
        out_shape=jax.ShapeDtypeStruct(xs.shape, F32),
        compiler_params=_params("arbitrary"),
        name="moe_experts",
    )(block_e, n_used, next_e, parity, xs, w1, b1.reshape(N_EXPERTS, 1, 2 * D_FF), w2,
      b2.reshape(N_EXPERTS, 1, D_MODEL))


def _combine_kernel(dest_ref, yb_ref, x1_ref, gate_ref, mod_ref, fg_ref, o_ref, buf, sem):
    def issue(g, carry):
        for q in range(SUBLANES):
            for j in range(TOP_K):
                slot = dest_ref[g * (SUBLANES * TOP_K) + q * TOP_K + j]
                pltpu.make_async_copy(yb_ref.at[pl.ds(slot, 1)], buf.at[g, j, pl.ds(q, 1)], sem).start(priority=j % 2)
        return carry

    lax.fori_loop(0, ROW_TILE // SUBLANES, issue, 0)
    pltpu.make_async_copy(buf, buf, sem).wait()

    g2 = mod_ref[0][:, 5 * D_MODEL:6 * D_MODEL]
    gates = gate_ref[...]
    moe = jnp.zeros((ROW_TILE, D_MODEL), F32)
    for j in range(TOP_K):
        moe = moe + buf[:, j].reshape(ROW_TILE, D_MODEL) * gates[:, j:j + 1]
    x2 = x1_ref[...] + g2 * moe
    y = x2 * lax.rsqrt(jnp.mean(x2 * x2, axis=-1, keepdims=True) + NORM_EPS)
    o_ref[...] = y * fg_ref[...]


def _combine(dest_flat, yb, x1, gates, mod, final_g, row_offset):
    rows = x1.shape[0]
    tiles_per_mod = rows // mod.shape[0] // ROW_TILE
    tile_offset = row_offset // ROW_TILE
    row_spec = pl.BlockSpec((ROW_TILE, D_MODEL), lambda i: (i, 0))
    return pl.pallas_call(
        _combine_kernel,
        grid=(rows // ROW_TILE,),
        in_specs=[pl.BlockSpec((ROW_TILE * TOP_K,), lambda i: (i + tile_offset,), memory_space=pltpu.SMEM),
                  pl.BlockSpec(memory_space=pl.ANY),
                  row_spec,
                  pl.BlockSpec((ROW_TILE, TOP_K), lambda i: (i, 0)),
                  pl.BlockSpec((1, 1, 6 * D_MODEL), lambda i: (i // tiles_per_mod, 0, 0)),
                  _const_spec((1, D_MODEL))],
        out_specs=row_spec,
        out_shape=jax.ShapeDtypeStruct((rows, D_MODEL), F32),
        scratch_shapes=[pltpu.VMEM((ROW_TILE // SUBLANES, TOP_K, SUBLANES, D_MODEL), F32),
                        pltpu.SemaphoreType.DMA(())],
        compiler_params=_params("arbitrary", disable_bounds_checks=True),
        name="moe_combine",
    )(dest_flat, yb, x1, gates, mod, final_g.reshape(1, D_MODEL))


def _block_diag_ones(width, block):
    i = jnp.arange(width) // block
    return (i[:, None] == i[None, :]).astype(F32)


def _mixers(x, mod, lp, s0, c0, n0, m0, grid):
    bsz, seq_len, _ = x.shape
    nc = seq_len // CHUNK
    x2 = x.reshape(bsz * seq_len, D_MODEL)
    zr, zm, zg, zgt = _inproj(x2, mod, lp['norm1_g'], lp['w_r'], lp['w_m'], lp['w_g'], lp['w_gt'])
    gt = zgt.reshape(N_GATES, bsz, nc, CHUNK).transpose(1, 2, 0, 3)
    yr, s_new = _rwkv(zr.reshape(bsz, seq_len, RWKV_COLS), s0, lp, grid)
    ym, c_new, n_new, m_new = _mlstm(zm.reshape(bsz, seq_len, MLSTM_MAIN), zg.reshape(bsz, seq_len, N_GATES), gt,
                                     c0, n0, m0, lp)
    x1, hn, idx, gates = _outproj(x2, yr.reshape(-1, DA), ym.reshape(-1, DB), mod, lp['norm2_g'], lp['w_out'],
                                  lp['router_w'], lp['router_b'])
    return x1, hn, idx, gates, (s_new, c_new, n_new, m_new[..., 0])


def kernel(x_prompt, x_sample, state_rwkv, state_mlstm_C, state_mlstm_n, state_mlstm_m, c, c_ctx, ada_w, ada_b, norm1_g, norm2_g, w_in, w_out, rwkv_mu, rwkv_w0, rwkv_w2, rwkv_a0, rwkv_a2, rwkv_g2, rwkv_kk, rwkv_ka, rwkv_rk, rwkv_gn_w, rwkv_gn_b, mlstm_conv, mlstm_bi, mlstm_bf, mlstm_norm_g, router_w, router_b, moe_w1, moe_b1, moe_w2, moe_b2, final_g):
    bp, lp_len, _ = x_prompt.shape
    bs, ls_len, _ = x_sample.shape
    w = w_in[0]
    lp = {
        'norm1_g': norm1_g[0], 'norm2_g': norm2_g[0],
        'w_r': w[:, :RWKV_COLS].astype(BF16),
        'w_m': w[:, RWKV_COLS:RWKV_COLS + MLSTM_MAIN].astype(BF16),
        'w_g': w[:, RWKV_COLS + MLSTM_MAIN:],
        'w_gt': w[:, RWKV_COLS + MLSTM_MAIN:].T,
        'w_out': w_out[0].astype(BF16),
        'mu': rwkv_mu[0].reshape(1, RWKV_COLS), 'kkw': rwkv_kk[0].reshape(1, DA), 'ka': rwkv_ka[0].reshape(1, DA),
        'rk': rwkv_rk[0].reshape(1, DA), 'gnw': rwkv_gn_w[0].reshape(1, DA), 'gnb': rwkv_gn_b[0].reshape(1, DA),
        'w0': rwkv_w0[0], 'w2': rwkv_w2[0], 'a0': rwkv_a0[0], 'a2': rwkv_a2[0], 'g2': rwkv_g2[0],
        'ones_bd': _block_diag_ones(PACK_W, HEAD_A).astype(BF16),
        'conv': mlstm_conv[0], 'bi': mlstm_bi[0].reshape(1, 2 * H_B), 'bit': mlstm_bi[0].reshape(2 * H_B, 1),
        'bf': mlstm_bf[0].reshape(1, 2 * H_B), 'bft': mlstm_bf[0].reshape(2 * H_B, 1),
        'ng': mlstm_norm_g[0].reshape(1, DB),
        'e_rep': jnp.repeat(jnp.eye(N_GATES, dtype=BF16), LANES, axis=1),
        'router_w': router_w[0], 'router_b': router_b[0],
    }
    cc = jnp.concatenate([c_ctx[None, :], c, jnp.zeros((2 * SUBLANES - 1 - bs, D_MODEL), F32)], axis=0)
    mod = _ada_mod(cc, ada_w[0], ada_b[0])
    mod_p = mod[0:1].reshape(1, 1, 6 * D_MODEL)
    mod_s = mod[1:1 + bs].reshape(bs, 1, 6 * D_MODEL)

    x1p, hnp, idxp, gatesp, st = _mixers(x_prompt, mod_p, lp, None, None, None, None, False)
    x1s, hns, idxs, gatess, _ = _mixers(x_sample, mod_s, lp, state_rwkv[:, 0], state_mlstm_C[:, 0],
                                        state_mlstm_n[:, 0], state_mlstm_m[:, 0], True)

    n_p = bp * lp_len
    idx = jnp.concatenate([idxp, idxs], axis=0)
    n_tok = idx.shape[0]
    rank, counts = _expert_ranks(idx)
    counts = counts[0]
    padded = (counts + MOE_BLOCK - 1) // MOE_BLOCK * MOE_BLOCK
    pad_end = jnp.cumsum(padded)
    pad_start = pad_end - padded
    n_blocks = n_tok * TOP_K // MOE_BLOCK + N_EXPERTS
    block_start = jnp.arange(n_blocks, dtype=jnp.int32) * MOE_BLOCK
    block_e = jnp.minimum(jnp.sum(pad_end[None, :] <= block_start[:, None], axis=1), N_EXPERTS - 1).astype(jnp.int32)
    n_used = (pad_end[-1:] // MOE_BLOCK).astype(jnp.int32)
    expert_ids = jnp.arange(N_EXPERTS, dtype=jnp.int32)
    start_of = jnp.sum(jnp.where(idx[:, :, None] == expert_ids, pad_start.astype(jnp.int32), 0), axis=-1)
    dest = (start_of + rank).astype(jnp.int32).reshape(-1)
    xs = _dispatch(dest, pad_end.astype(jnp.int32), hnp, hns, n_blocks * MOE_BLOCK)
    has_rows = counts > 0
    later = jnp.where(has_rows[None, :] & (expert_ids[None, :] > expert_ids[:, None]), expert_ids[None, :], N_EXPERTS)
    next_with_rows = jnp.min(later, axis=1)
    next_of = jnp.where(next_with_rows < N_EXPERTS, next_with_rows, expert_ids)
    place = jnp.cumsum(has_rows.astype(jnp.int32)) - 1
    yb = _experts(block_e, n_used, next_of[block_e].astype(jnp.int32), (place[block_e] % 2).astype(jnp.int32), xs,
                  moe_w1[0], moe_b1[0], moe_w2[0], moe_b2[0])
    y_prompt = _combine(dest, yb, x1p, gatesp, mod_p, final_g, 0).reshape(x_prompt.shape)
    y_sample = _combine(dest, yb, x1s, gatess, mod_s, final_g, n_p).reshape(x_sample.shape)

    s_new, c_new, n_new, m_new = st
    return (y_prompt, y_sample, s_new[:, None], c_new[:, None], n_new[:, None], m_new[:, None])
```

```python
import functools

import jax
import jax.numpy as jnp
from jax import lax
from jax.experimental import pallas as pl
from jax.experimental.pallas import tpu as pltpu

F32 = jnp.float32
BF16 = jnp.bfloat16
HIGHEST = lax.Precision.HIGHEST

D_MODEL = 1024
DA = 512
HEAD_A = 64
H_A = DA // HEAD_A
DB = 512
H_B = 4
HEAD_B = DB // H_B
DECAY_LORA = 64
AAA_LORA = 64
GATE_LORA = 128
RWKV_COLS = 3 * DA + DECAY_LORA + AAA_LORA + GATE_LORA
MLSTM_MAIN = 4 * DB
N_GATES = 4 * H_B
CHUNK = 64
N_EXPERTS = 32
TOP_K = 4
D_FF = D_MODEL
SWIGLU_LIMIT = 7.0
SWIGLU_ALPHA = 1.702
MOE_BLOCK = 512
FF_CHUNK = 256
NORM_EPS = 1e-6
GN_EPS = 64e-5
ROW_TILE = 1024
PROJ_TILE = 512
SUBLANES = 8
LANES = 128
VMEM_LIMIT = 56 * 1024 * 1024
PACK = 4
PACK_W = PACK * HEAD_A
N_PACKS = H_A // PACK
SCAN_GROUP = 2
SCAN_DOUBLE_BUFFER_BYTES = 16 * 1024 * 1024

NN = (((1,), (0,)), ((), ()))
NT = (((1,), (1,)), ((), ()))
TN = (((0,), (0,)), ((), ()))


def _mm(a, b, precision=HIGHEST):
    return lax.dot_general(a, b, NN, precision=precision, preferred_element_type=F32)


def _dot_bf16(a, b, dims=NN):
    return lax.dot_general(a.astype(BF16), b.astype(BF16), dims, preferred_element_type=F32)


def _sigmoid(x):
    return 1.0 / (1.0 + jnp.exp(-x))


def _log_sigmoid(x):
    return jnp.minimum(x, 0.0) - jnp.log(1.0 + jnp.exp(-jnp.abs(x)))


def _params(*sem, **kw):
    return pltpu.CompilerParams(dimension_semantics=sem, vmem_limit_bytes=VMEM_LIMIT, **kw)


def _split_bf16(x):
    hi = x.astype(BF16)
    return hi, (x - hi.astype(F32)).astype(BF16)


def _const_spec(shape):
    nd = len(shape)
    return pl.BlockSpec(shape, lambda *_: (0,) * nd)


def _tri(n, reverse, strict):
    t = lax.broadcasted_iota(jnp.int32, (n, n), 0)
    s = lax.broadcasted_iota(jnp.int32, (n, n), 1)
    if reverse:
        return (s > t) if strict else (s >= t)
    return (s < t) if strict else (s <= t)


def _seq_neighbours(ref, b, c, nc, cols):
    r0 = pl.multiple_of(c * CHUNK, CHUNK)
    zc = ref[b, pl.ds(r0, CHUNK), cols]
    row = lax.broadcasted_iota(jnp.int32, zc.shape, 0)
    p0 = pl.multiple_of(jnp.maximum(r0 - SUBLANES, 0), SUBLANES)
    n0 = pl.multiple_of(jnp.minimum(r0 + CHUNK, (nc - 1) * CHUNK), SUBLANES)
    before = ref[b, pl.ds(p0, SUBLANES), cols][SUBLANES - 1:SUBLANES]
    after = ref[b, pl.ds(n0, SUBLANES), cols][0:1]
    before = jnp.where(c > 0, before, 0.0)
    after = jnp.where(c < nc - 1, after, 0.0)
    prev = jnp.where(row == 0, before, pltpu.roll(zc, 1, 0))
    nxt = jnp.where(row == CHUNK - 1, after, pltpu.roll(zc, CHUNK - 1, 0))
    return zc, prev, nxt


def _ada_kernel(c_ref, w_ref, b_ref, o_ref):
    cc = c_ref[...]
    o_ref[...] = _dot_bf16(cc * _sigmoid(cc), w_ref[...]) + b_ref[...]


def _ada_mod(cc, ada_w, ada_b):
    rows = cc.shape[0]
    ncol = ada_w.shape[1]
    tn = 1536
    return pl.pallas_call(
        _ada_kernel,
        grid=(ncol // tn,),
        in_specs=[_const_spec((rows, D_MODEL)),
                  pl.BlockSpec((D_MODEL, tn), lambda j: (0, j)),
                  pl.BlockSpec((1, tn), lambda j: (0, j))],
        out_specs=pl.BlockSpec((rows, tn), lambda j: (0, j)),
        out_shape=jax.ShapeDtypeStruct((rows, ncol), F32),
        compiler_params=_params("arbitrary"),
        name="ada_mod",
    )(cc, ada_w, ada_b.reshape(1, ncol))


def _inproj_kernel(x_ref, mod_ref, g_ref, wr_ref, wm_ref, wg_ref, wgt_ref, zr_ref, zm_ref, zg_ref, zgt_ref):
    x = x_ref[...]
    mod = mod_ref[0]
    sh, sc = mod[:, 0:D_MODEL], mod[:, D_MODEL:2 * D_MODEL]
    y = x * lax.rsqrt(jnp.mean(x * x, axis=-1, keepdims=True) + NORM_EPS)
    hn = y * g_ref[...] * (1.0 + sc) + sh
    hb = hn.astype(BF16)
    zr_ref[...] = lax.dot_general(hb, wr_ref[...], NN, preferred_element_type=F32)
    zm_ref[...] = lax.dot_general(hb, wm_ref[...], NN, preferred_element_type=F32)
    zg_ref[...] = _dot_bf16(hb, wg_ref[...])
    zgt_ref[...] = _dot_bf16(wgt_ref[...], hb, NT)


def _inproj(x2, mod, norm_g, w_r, w_m, w_g, w_gt):
    rows = x2.shape[0]
    tiles_per_mod = rows // mod.shape[0] // PROJ_TILE
    return pl.pallas_call(
        _inproj_kernel,
        grid=(rows // PROJ_TILE,),
        in_specs=[pl.BlockSpec((PROJ_TILE, D_MODEL), lambda i: (i, 0)),
                  pl.BlockSpec((1, 1, 6 * D_MODEL), lambda i: (i // tiles_per_mod, 0, 0)),
                  _const_spec((1, D_MODEL)),
                  _const_spec(w_r.shape), _const_spec(w_m.shape), _const_spec(w_g.shape),
                  _const_spec(w_gt.shape)],
        out_specs=[pl.BlockSpec((PROJ_TILE, RWKV_COLS), lambda i: (i, 0)),
                   pl.BlockSpec((PROJ_TILE, MLSTM_MAIN), lambda i: (i, 0)),
                   pl.BlockSpec((PROJ_TILE, N_GATES), lambda i: (i, 0)),
                   pl.BlockSpec((N_GATES, PROJ_TILE), lambda i: (0, i))],
        out_shape=[jax.ShapeDtypeStruct((rows, RWKV_COLS), F32),
                   jax.ShapeDtypeStruct((rows, MLSTM_MAIN), F32),
                   jax.ShapeDtypeStruct((rows, N_GATES), F32),
                   jax.ShapeDtypeStruct((N_GATES, rows), F32)],
        compiler_params=_params("arbitrary"),
        name="inproj",
    )(x2, mod, norm_g.reshape(1, D_MODEL), w_r, w_m, w_g, w_gt)


def _group_sum(x, ones_bd):
    hi = x.astype(BF16)
    lo = (x - hi.astype(F32)).astype(BF16)
    return (lax.dot_general(hi, ones_bd, NN, preferred_element_type=F32)
            + lax.dot_general(lo, ones_bd, NN, preferred_element_type=F32))


def _block_diag(x, ones_bd):
    return jnp.concatenate([x.astype(BF16)] * PACK, axis=0) * ones_bd


def _rwkv_body(zr_hbm, s0_ref, mu_ref, kkw_ref, ka_ref, rk_ref, gnw_ref, gnb_ref, w0_ref, w2_ref,
                 a0_ref, a2_ref, g2_ref, ones_ref, y_ref, sout_ref,
                 r_s, k_s, v_s, kk_s, gate_s, lx_s, st_s, zr_ref, zr_sem, *, seq_len, grid, group):
    nc = seq_len // CHUNK
    ones_bd = ones_ref[...]
    step, n_steps = pl.program_id(0), pl.num_programs(0)

    def fetch(s):
        return pltpu.make_async_copy(zr_hbm.at[pl.ds(s * group, group)], zr_ref, zr_sem)

    @pl.when(step == 0)
    def _():
        fetch(0).start()

    fetch(step).wait()

    def mix_chunk(c, carry):
        r0 = pl.multiple_of(c * CHUNK, CHUNK)
        for b in range(group):
            if grid:
                zc = zr_ref[b, pl.ds(r0, CHUNK), :]
                row = lax.broadcasted_iota(jnp.int32, zc.shape, 0)
                up0 = pl.multiple_of(jnp.maximum(c - 1, 0) * CHUNK, CHUNK)
                dn0 = pl.multiple_of(jnp.minimum(c + 1, nc - 1) * CHUNK, CHUNK)
                up = jnp.where(c > 0, zr_ref[b, pl.ds(up0, CHUNK), :], 0.0)
                down = jnp.where(c < nc - 1, zr_ref[b, pl.ds(dn0, CHUNK), :], 0.0)
                left = jnp.where(row == 0, 0.0, pltpu.roll(zc, 1, 0))
                right = jnp.where(row == CHUNK - 1, 0.0, pltpu.roll(zc, CHUNK - 1, 0))
                local = 0.25 * (up + down + left + right)
            else:
                zc, prev, nxt = _seq_neighbours(zr_ref, b, c, nc, slice(None))
                local = 0.5 * (prev + nxt)
            z = zc + (local - zc) * mu_ref[...]
            r, k, v = z[:, 0:DA], z[:, DA:2 * DA], z[:, 2 * DA:3 * DA]
            lx = z[:, 3 * DA:3 * DA + DECAY_LORA + AAA_LORA]
            xg = z[:, 3 * DA + DECAY_LORA + AAA_LORA:]
            kk = k * kkw_ref[...]
            sq = kk * kk
            ss = jnp.concatenate([_group_sum(sq[:, p * PACK_W:(p + 1) * PACK_W], ones_bd) for p in range(N_PACKS)],
                                 axis=1)
            kk = kk / jnp.maximum(jnp.sqrt(ss), 1e-12)
            lane = lax.broadcasted_iota(jnp.int32, lx.shape, 1)
            r_s[b, pl.ds(r0, CHUNK), :] = r
            k_s[b, pl.ds(r0, CHUNK), :] = k
            v_s[b, pl.ds(r0, CHUNK), :] = v
            kk_s[b, pl.ds(r0, CHUNK), :] = kk
            gate_s[b, pl.ds(r0, CHUNK), :] = _dot_bf16(_sigmoid(xg), g2_ref[...])
            lx_s[b, pl.ds(r0, CHUNK), :] = jnp.where(lane < DECAY_LORA, jnp.tanh(lx), lx)
        return carry

    lax.fori_loop(0, nc, mix_chunk, 0)

    @pl.when(step + 1 < n_steps)
    def _():
        fetch(step + 1).start()

    rowblk = lax.broadcasted_iota(jnp.int32, (PACK_W, PACK_W), 0) // HEAD_A
    colblk = lax.broadcasted_iota(jnp.int32, (PACK_W, PACK_W), 1) // HEAD_A
    for b in range(group):
        for d in range(2):
            for p in range(N_PACKS):
                if s0_ref is None:
                    st_s[b, d, p] = jnp.zeros((PACK_W, PACK_W), F32)
                    continue
                rows_ = jnp.concatenate([s0_ref[b, d, p * PACK + h] for h in range(PACK)], axis=0)
                st_s[b, d, p] = jnp.where(rowblk == colblk, jnp.concatenate([rows_] * PACK, axis=1), 0.0)

    t_idx = lax.broadcasted_iota(jnp.int32, (CHUNK, PACK_W), 0)
    s_idx = lax.broadcasted_iota(jnp.int32, (CHUNK, PACK_W), 1) % CHUNK

    def stages(i):
        first_touch = i < nc // 2
        dot = functools.partial(lax.dot_general, preferred_element_type=F32)
        chains, out_sel = [], []
        for b, d in [(b, d) for b in range(group) for d in range(2)]:
            reverse = d == 1
            c = nc - 1 - i if reverse else i
            rows = pl.ds(pl.multiple_of(c * CHUNK, CHUNK), CHUNK)
            out_sel.append((b, rows))
            strict = (s_idx > t_idx) if reverse else (s_idx < t_idx)
            incl = (s_idx >= t_idx) if reverse else (s_idx <= t_idx)
            last = 0 if reverse else CHUNK - 1
            r, k, v, kk = r_s[b, rows, :], k_s[b, rows, :], v_s[b, rows, :], kk_s[b, rows, :]
            lx = lx_s[b, rows, :]
            wl = w0_ref[d:d + 1, :] + _dot_bf16(lx[:, 0:DECAY_LORA], w2_ref[d])
            logw = -jnp.exp(_log_sigmoid(wl) - 0.5)
            a = _sigmoid(a0_ref[d:d + 1, :] + _dot_bf16(lx[:, DECAY_LORA:], a2_ref[d]))
            kd = k * (1.0 + (a - 1.0) * ka_ref[...])
            kb = kk * a
            tri = _tri(CHUNK, reverse, False).astype(BF16)
            w1, w_rest = _split_bf16(logw)
            w2, w3 = _split_bf16(w_rest)
            cum = dot(tri, w1, NN) + dot(tri, w2, NN) + dot(tri, w3, NN)
            clast = cum[last:last + 1]
            rt = r * jnp.exp(cum)
            at = -kk * jnp.exp(cum - logw)
            einv = jnp.exp(-cum)
            kt, bt = kd * einv, kb * einv
            edec = jnp.exp(clast - cum)
            kp, bp = kd * edec, kb * edec
            ptot = jnp.exp(clast)
            rkd = r * kd * rk_ref[...]
            for p in range(N_PACKS):
                cs = slice(p * PACK_W, (p + 1) * PACK_W)
                chains.append(dict(
                    out=len(out_sel) - 1, st=(b, d, p), cs=cs, strict=strict, incl=incl, v=v[:, cs],
                    rkd=rkd[:, cs], ptot=ptot[:, cs],
                    ar=jnp.concatenate([at[:, cs], rt[:, cs]], axis=0).astype(BF16),
                    kt_bd=_block_diag(kt[:, cs], ones_bd), bt_bd=_block_diag(bt[:, cs], ones_bd),
                    v_bd=_block_diag(v[:, cs], ones_bd),
                    kbp=jnp.concatenate([kp[:, cs], bp[:, cs]], axis=0)))
            yield

        for ch in chains:
            ch['s_prev'] = st_s[ch['st']]
            ch['a_k'] = dot(ch['ar'], ch['kt_bd'], NT)
            ch['a_b'] = dot(ch['ar'], ch['bt_bd'], NT)
            ch['ars'] = dot(ch['ar'], ch['s_prev'].astype(BF16), NT)
        yield
        for ch in chains:
            a_ak = jnp.where(ch['strict'], ch['a_k'][:CHUNK], 0.0)
            a_rk = jnp.where(ch['incl'], ch['a_k'][CHUNK:], 0.0)
            ch['a_rb'] = jnp.where(ch['incl'], ch['a_b'][CHUNK:], 0.0).astype(BF16)
            n = jnp.where(ch['strict'], ch['a_b'][:CHUNK], 0.0)
            ch['n_hi'], n_lo = _split_bf16(n)
            ch['n_both'] = jnp.concatenate([ch['n_hi'], n_lo], axis=0)
            ch['pw'] = ch['n_hi']
            ch['m'] = jnp.where(s_idx == t_idx, 1.0, 0.0) + n
            akv = dot(jnp.concatenate([a_ak, a_rk], axis=0).astype(BF16), ch['v_bd'], NN)
            ch['rhs'] = ch['ars'][:CHUNK] + akv[:CHUNK]
            ch['y'] = ch['ars'][CHUNK:] + akv[CHUNK:]
        yield
        for ch in chains:
            ch['pw'] = dot(ch['pw'], _block_diag(ch['pw'], ones_bd), NN).astype(BF16)
        yield
        for j in range(1, 5):
            for ch in chains:
                pw_bd = _block_diag(ch['pw'], ones_bd)
                if j < 4:
                    both = dot(jnp.concatenate([ch['m'].astype(BF16), ch['pw']], axis=0), pw_bd, NN)
                    ch['m'] = ch['m'] + both[:CHUNK]
                    ch['pw'] = both[CHUNK:].astype(BF16)
                else:
                    ch['m'] = ch['m'] + dot(ch['m'].astype(BF16), pw_bd, NN)
            yield
        for ch in chains:
            ch['mb'] = ch['m'].astype(BF16)
            ch['u'] = dot(ch['mb'], _block_diag(ch['rhs'], ones_bd), NN)
        yield
        for ch in chains:
            u_hi, u_lo = _split_bf16(ch['u'])
            both = dot(ch['n_both'], _block_diag(u_hi, ones_bd), NN)
            nu = both[:CHUNK] + both[CHUNK:] + dot(ch['n_hi'], _block_diag(u_lo, ones_bd), NN)
            ch['resid'] = ch['rhs'] - ch['u'] + nu
        yield
        for ch in chains:
            ch['u'] = ch['u'] + dot(ch['mb'], _block_diag(ch['resid'], ones_bd), NN)
        yield
        for ch in chains:
            u = ch['u']
            ch['y'] = ch['y'] + dot(ch['a_rb'], _block_diag(u, ones_bd), NN)
            vu_hi, vu_lo = _split_bf16(jnp.concatenate([ch['v'], u], axis=0))
            kbp_hi, kbp_lo = _split_bf16(ch['kbp'])
            grown = dot(vu_hi, kbp_hi, TN) + dot(vu_hi, kbp_lo, TN) + dot(vu_lo, kbp_hi, TN)
            st_s[ch['st']] = ch['s_prev'] * ch['ptot'] + grown * ones_bd.astype(F32)
        yield
        n_ch = len(chains)
        sums = dot(jnp.concatenate([ch['y'] for ch in chains] + [ch['rkd'] for ch in chains],
                                   axis=0).astype(BF16), ones_bd, NN)
        ycs = [ch['y'] - sums[q * CHUNK:(q + 1) * CHUNK] * (1.0 / HEAD_A) for q, ch in enumerate(chains)]
        var = dot(jnp.concatenate([yc * yc for yc in ycs], axis=0).astype(BF16), ones_bd, NN) * (1.0 / HEAD_A)
        outs = [[] for _ in out_sel]
        for q, ch in enumerate(chains):
            cs = ch['cs']
            out = ycs[q] * lax.rsqrt(var[q * CHUNK:(q + 1) * CHUNK] + GN_EPS) * gnw_ref[:, cs] + gnb_ref[:, cs]
            outs[ch['out']].append(out + sums[(n_ch + q) * CHUNK:(n_ch + q + 1) * CHUNK] * ch['v'])
        for (b, rows), parts in zip(out_sel, outs):
            out = jnp.concatenate(parts, axis=1)

            @pl.when(first_touch)
            def _(b=b, rows=rows, out=out):
                y_ref[b, rows, :] = out

            @pl.when(jnp.logical_not(first_touch))
            def _(b=b, rows=rows, out=out):
                y_ref[b, rows, :] = (y_ref[b, rows, :] + out) * gate_s[b, rows, :]

    def finish():
        for b in range(group):
            for d in range(2):
                for p in range(N_PACKS):
                    s_fin = st_s[b, d, p]
                    for h in range(PACK):
                        sout_ref[b, d, p * PACK + h] = s_fin[h * HEAD_A:(h + 1) * HEAD_A,
                                                             h * HEAD_A:(h + 1) * HEAD_A]

    return stages, finish


def _mlstm_body(zm_hbm, zvo_ref, g_ref, gt_ref, c0_ref, n0_ref, m0_ref, conv_ref, bi_ref, bit_ref, bf_ref, bft_ref,
                  ng_ref, erep_ref, y_ref, cout_ref, nout_ref, mout_ref,
                  qk_s, h_s, c_s, n_s, m_s, zqk_ref, zqk_sem, *, seq_len, group):
    nc = seq_len // CHUNK
    step, n_steps = pl.program_id(0), pl.num_programs(0)

    def fetch(s):
        return pltpu.make_async_copy(zm_hbm.at[pl.ds(s * group, group), :, pl.ds(0, 2 * DB)], zqk_ref, zqk_sem)

    @pl.when(step == 0)
    def _():
        fetch(0).start()

    fetch(step).wait()

    def conv_chunk(c, carry):
        for b in range(group):
            zc, prev, nxt = _seq_neighbours(zqk_ref, b, c, nc, slice(None))
            u = conv_ref[0:1, :] * prev + conv_ref[1:2, :] * zc + conv_ref[2:3, :] * nxt
            qk = u * _sigmoid(u)
            lane = lax.broadcasted_iota(jnp.int32, qk.shape, 1)
            qk_s[b, pl.ds(pl.multiple_of(c * CHUNK, CHUNK), CHUNK), :] = jnp.where(lane >= DB,
                                                                                  qk * (HEAD_B ** -0.5), qk)
        return carry

    lax.fori_loop(0, nc, conv_chunk, 0)

    @pl.when(step + 1 < n_steps)
    def _():
        fetch(step + 1).start()

    for b in range(group):
        for d in range(2):
            if c0_ref is None:
                c_s[b, d] = jnp.zeros((H_B, HEAD_B, HEAD_B), F32)
                n_s[b, d] = jnp.zeros((H_B, HEAD_B), F32)
                m_s[b, d] = jnp.zeros((H_B, LANES), F32)
                continue
            c_s[b, d] = c0_ref[b, d]
            n_s[b, d] = n0_ref[b, d]
            m_s[b, d] = jnp.broadcast_to(m0_ref[b, d], (H_B, LANES))

    e_rep = erep_ref[...]
    ones_tl = jnp.ones((CHUNK, LANES), BF16)
    row_id = lax.broadcasted_iota(jnp.int32, (CHUNK, LANES), 0)

    def scan_max(x, reverse):
        shift = 1
        while shift < CHUNK:
            if reverse:
                moved = jnp.where(row_id < CHUNK - shift, pltpu.roll(x, CHUNK - shift, 0), -jnp.inf)
            else:
                moved = jnp.where(row_id >= shift, pltpu.roll(x, shift, 0), -jnp.inf)
            x = jnp.maximum(x, moved)
            shift *= 2
        return x

    def dot3(a, b, dims):
        a_hi, a_lo = _split_bf16(a)
        b_hi, b_lo = _split_bf16(b)
        dg = functools.partial(lax.dot_general, dimension_numbers=dims, preferred_element_type=F32)
        return dg(a_hi, b_hi) + dg(a_hi, b_lo) + dg(a_lo, b_hi)

    def stages(i):
        dot = functools.partial(lax.dot_general, preferred_element_type=F32)
        chains = []
        for b, d in [(b, d) for b in range(group) for d in range(2)]:
            reverse = d == 1
            incl = _tri(CHUNK, reverse, False)
            tri_b = incl.astype(BF16)
            tri_row = _tri(CHUNK, not reverse, False).astype(F32)
            last = 0 if reverse else CHUNK - 1
            gsl = slice(d * H_B, (d + 1) * H_B)
            fsl = slice(2 * H_B + d * H_B, 2 * H_B + (d + 1) * H_B)
            c = nc - 1 - i if reverse else i
            rows = pl.ds(pl.multiple_of(c * CHUNK, CHUNK), CHUNK)
            g1, g_rest = _split_bf16(g_ref[b, rows, :])
            g2, g3 = _split_bf16(g_rest)
            g_rep = dot(g1, e_rep, NN) + dot(g2, e_rep, NN) + dot(g3, e_rep, NN)
            grow = gt_ref[b, c]
            i_row = grow[gsl, :] + bit_ref[gsl, :]
            f_row = _log_sigmoid(grow[fsl, :] + bft_ref[gsl, :])
            b_row = _mm(f_row, tri_row)
            for h in range(H_B):
                ji, jf = d * H_B + h, 2 * H_B + d * H_B + h
                ic = g_rep[:, ji * LANES:(ji + 1) * LANES] + bi_ref[:, ji:ji + 1]
                fc = _log_sigmoid(g_rep[:, jf * LANES:(jf + 1) * LANES] + bf_ref[:, ji:ji + 1])
                f1, f_rest = _split_bf16(fc)
                f2, f3 = _split_bf16(f_rest)
                bc = dot(tri_b, f1, NN) + dot(tri_b, f2, NN) + dot(tri_b, f3, NN)
                q = qk_s[b, rows, h * HEAD_B:(h + 1) * HEAD_B]
                chains.append(dict(
                    b=b, d=d, h=h, rows=rows, incl=incl, reverse=reverse, q=q, qb=q.astype(BF16),
                    k=qk_s[b, rows, DB + h * HEAD_B:DB + (h + 1) * HEAD_B],
                    v=zvo_ref[b, rows, h * HEAD_B:(h + 1) * HEAD_B],
                    bc=bc, ic=ic, bl=bc[last:last + 1, :], br=b_row[h:h + 1, :], ir=i_row[h:h + 1, :]))
            yield
        for q, ch in enumerate(chains):
            b, d, h = ch['b'], ch['d'], ch['h']
            ch['qk'] = _dot_bf16(ch['qb'], ch['k'], NT)
            ch['c_prev'] = c_s[b, d, h]
            ch['qc'] = _dot_bf16(ch['qb'], ch['c_prev'])
            ch['n_prev'] = n_s[b, d, h:h + 1, :]
            ch['qn'] = dot3(ch['q'], jnp.broadcast_to(ch['n_prev'], (HEAD_B, HEAD_B)), NT)
            if q % 2 == 1:
                yield
        for q, ch in enumerate(chains):
            b, d, h = ch['b'], ch['d'], ch['h']
            bc, ic, bl = ch['bc'], ch['ic'], ch['bl']
            m_prev = m_s[b, d, h:h + 1, :]
            m_t = bc + jnp.maximum(m_prev, scan_max(ic - bc, ch['reverse']))
            dmat = jnp.where(ch['incl'], bc[:, :CHUNK] - ch['br'] + ch['ir'], -jnp.inf)
            s = ch['qk'] * jnp.exp(dmat - m_t[:, :CHUNK])
            carry_w = jnp.exp(bc + m_prev - m_t)
            s_hi, s_lo = _split_bf16(s)
            den = dot(s_hi, ones_tl, NN) + dot(s_lo, ones_tl, NN) + carry_w * ch['qn']
            num = dot(s_hi, ch['v'].astype(BF16), NN) + carry_w * ch['qc']
            h_s[b, d, ch['rows'], h * HEAD_B:(h + 1) * HEAD_B] = num / jnp.maximum(jnp.abs(den), jnp.exp(-m_t))
            g = bl - bc + ic
            m_new = jnp.maximum(bl + m_prev, jnp.max(g, axis=0, keepdims=True))
            wk = jnp.exp(g - m_new)
            decay = jnp.exp(bl + m_prev - m_new)
            c_s[b, d, h] = decay * ch['c_prev'] + _dot_bf16(ch['k'], wk * ch['v'], TN)
            n_s[b, d, h:h + 1, :] = decay * ch['n_prev'] + jnp.sum(wk * ch['k'], axis=0, keepdims=True)
            m_s[b, d, h:h + 1, :] = m_new
            if q % 2 == 1:
                yield

    def norm_chunk(c, carry):
        rows = pl.ds(pl.multiple_of(c * CHUNK, CHUNK), CHUNK)
        for b in range(group):
            og = zvo_ref[b, rows, DB:2 * DB]
            hm = h_s[b, 0, rows, :] + h_s[b, 1, rows, :]
            parts = []
            for h in range(H_B):
                hh = hm[:, h * HEAD_B:(h + 1) * HEAD_B]
                parts.append(hh * lax.rsqrt(jnp.mean(hh * hh, axis=-1, keepdims=True) + NORM_EPS))
            y_ref[b, rows, :] = jnp.concatenate(parts, axis=1) * ng_ref[...] * _sigmoid(og)
        return carry

    def finish():
        cout_ref[...] = c_s[...]
        nout_ref[...] = n_s[...]
        mout_ref[...] = m_s[...]
        lax.fori_loop(0, nc, norm_chunk, 0)

    return stages, finish


def _scan_kernel(*refs, body, n_steps, absent=(), **static):
    refs = list(refs)
    for pos in absent:
        refs.insert(pos, None)
    stages, finish = body(*refs, **static)

    def scan_step(i, carry):
        for _ in stages(i):
            pass
        return carry

    lax.fori_loop(0, n_steps, scan_step, 0)
    finish()


def _scan_specs(grp):
    lead = lambda *rest: pl.BlockSpec((grp,) + rest, lambda b: (b,) + (0,) * len(rest))

    def big(seq_len, cols, col_block):
        single = 2 * grp * seq_len * cols * 4 >= SCAN_DOUBLE_BUFFER_BYTES
        return pl.BlockSpec((grp, seq_len, cols), lambda b: (b, 0, col_block),
                            **(dict(pipeline_mode=pl.Buffered(1)) if single else {}))

    return lead, big


def _rwkv(zr, s0, p, grid):
    bsz, seq_len, _ = zr.shape
    nc = seq_len // CHUNK
    grp = 2 * SCAN_GROUP if 4 * SCAN_GROUP * seq_len * RWKV_COLS * 4 <= SCAN_DOUBLE_BUFFER_BYTES else SCAN_GROUP
    assert nc % 2 == 0 and bsz % grp == 0
    states = [] if s0 is None else [s0]
    kern = functools.partial(_scan_kernel, body=_rwkv_body, n_steps=nc, seq_len=seq_len, grid=grid, group=grp,
                             absent=(1,) if s0 is None else ())
    consts = [p['mu'], p['kkw'], p['ka'], p['rk'], p['gnw'], p['gnb'], p['w0'], p['w2'], p['a0'], p['a2'],
              p['g2'], p['ones_bd']]
    lead, big = _scan_specs(grp)
    return pl.pallas_call(
        kern,
        grid=(bsz // grp,),
        in_specs=[pl.BlockSpec(memory_space=pl.ANY)] + [lead(2, H_A, HEAD_A, HEAD_A)] * len(states)
                 + [_const_spec(a.shape) for a in consts],
        out_specs=[lead(seq_len, DA), lead(2, H_A, HEAD_A, HEAD_A)],
        out_shape=[jax.ShapeDtypeStruct((bsz, seq_len, DA), F32),
                   jax.ShapeDtypeStruct((bsz, 2, H_A, HEAD_A, HEAD_A), F32)],
        scratch_shapes=[pltpu.VMEM((grp, seq_len, DA), F32)] * 5
                       + [pltpu.VMEM((grp, seq_len, DECAY_LORA + AAA_LORA), F32),
                          pltpu.VMEM((grp, 2, N_PACKS, PACK_W, PACK_W), F32),
                          pltpu.VMEM((grp, seq_len, RWKV_COLS), F32), pltpu.SemaphoreType.DMA(())],
        compiler_params=_params("arbitrary"),
        name="rwkv_scan",
    )(zr, *states, *consts)


def _mlstm(zm, g, gt, c0, n0, m0, p):
    bsz, seq_len, _ = zm.shape
    nc = seq_len // CHUNK
    grp = 2 * SCAN_GROUP if 4 * SCAN_GROUP * seq_len * MLSTM_MAIN * 4 <= SCAN_DOUBLE_BUFFER_BYTES else SCAN_GROUP
    assert bsz % grp == 0
    zero_state = c0 is None
    states = [] if zero_state else [c0, n0, m0.reshape(bsz, 2, H_B, 1)]
    kern = functools.partial(_scan_kernel, body=_mlstm_body, n_steps=nc, seq_len=seq_len, group=grp,
                             absent=(4, 5, 6) if zero_state else ())
    consts = [p['conv'], p['bi'], p['bit'], p['bf'], p['bft'], p['ng'], p['e_rep']]
    lead, big = _scan_specs(grp)
    state_specs = [] if zero_state else [lead(2, H_B, HEAD_B, HEAD_B), lead(2, H_B, HEAD_B), lead(2, H_B, 1)]
    vo_spec = big(seq_len, 2 * DB, 1)
    return pl.pallas_call(
        kern,
        grid=(bsz // grp,),
        in_specs=[pl.BlockSpec(memory_space=pl.ANY), vo_spec, lead(seq_len, N_GATES), lead(nc, N_GATES, CHUNK)]
                 + state_specs + [_const_spec(a.shape) for a in consts],
        out_specs=[lead(seq_len, DB), lead(2, H_B, HEAD_B, HEAD_B), lead(2, H_B, HEAD_B), lead(2, H_B, LANES)],
        out_shape=[jax.ShapeDtypeStruct((bsz, seq_len, DB), F32),
                   jax.ShapeDtypeStruct((bsz, 2, H_B, HEAD_B, HEAD_B), F32),
                   jax.ShapeDtypeStruct((bsz, 2, H_B, HEAD_B), F32),
                   jax.ShapeDtypeStruct((bsz, 2, H_B, LANES), F32)],
        scratch_shapes=[pltpu.VMEM((grp, seq_len, 2 * DB), F32), pltpu.VMEM((grp, 2, seq_len, DB), F32),
                        pltpu.VMEM((grp, 2, H_B, HEAD_B, HEAD_B), F32), pltpu.VMEM((grp, 2, H_B, HEAD_B), F32),
                        pltpu.VMEM((grp, 2, H_B, LANES), F32),
                        pltpu.VMEM((grp, seq_len, 2 * DB), F32), pltpu.SemaphoreType.DMA(())],
        compiler_params=_params("arbitrary"),
        name="mlstm_scan",
    )(zm, zm, g, gt, *states, *consts)


def _outproj_kernel(x_ref, yr_ref, ym_ref, mod_ref, g_ref, wo_ref, rw_ref, rb_ref,
                    x1_ref, hn_ref, idx_ref, gate_ref):
    mod = mod_ref[0]
    g1 = mod[:, 2 * D_MODEL:3 * D_MODEL]
    sh2, sc2 = mod[:, 3 * D_MODEL:4 * D_MODEL], mod[:, 4 * D_MODEL:5 * D_MODEL]
    mix = _dot_bf16(yr_ref[...], wo_ref[0:DA, :]) + _dot_bf16(ym_ref[...], wo_ref[DA:, :])
    x1 = x_ref[...] + g1 * mix
    x1_ref[...] = x1
    y = x1 * lax.rsqrt(jnp.mean(x1 * x1, axis=-1, keepdims=True) + NORM_EPS)
    hn = y * g_ref[...] * (1.0 + sc2) + sh2
    hn_ref[...] = hn
    logits = _dot_bf16(hn, rw_ref[...]) + rb_ref[...]
    lane = lax.broadcasted_iota(jnp.int32, logits.shape, 1)
    vals, idxs = [], []
    for _ in range(TOP_K):
        top = jnp.max(logits, axis=-1, keepdims=True)
        pick = jnp.min(jnp.where(logits == top, lane, N_EXPERTS), axis=-1, keepdims=True)
        vals.append(top)
        idxs.append(pick)
        logits = jnp.where(lane == pick, -jnp.inf, logits)
    exps = [jnp.exp(v - vals[0]) for v in vals]
    total = exps[0] + exps[1] + exps[2] + exps[3]
    k_lane = lax.broadcasted_iota(jnp.int32, (x1.shape[0], TOP_K), 1)
    gates = jnp.zeros((x1.shape[0], TOP_K), F32)
    picks = jnp.zeros((x1.shape[0], TOP_K), jnp.int32)
    for j in range(TOP_K):
        gates = jnp.where(k_lane == j, exps[j] / total, gates)
        picks = jnp.where(k_lane == j, idxs[j], picks)
    idx_ref[...] = picks
    gate_ref[...] = gates


def _outproj(x2, yr, ym, mod, norm_g, w_out, router_w, router_b):
    rows = x2.shape[0]
    tiles_per_mod = rows // mod.shape[0] // PROJ_TILE
    row_spec = lambda w: pl.BlockSpec((PROJ_TILE, w), lambda i: (i, 0))
    return pl.pallas_call(
        _outproj_kernel,
        grid=(rows // PROJ_TILE,),
        in_specs=[row_spec(D_MODEL), row_spec(DA), row_spec(DB),
                  pl.BlockSpec((1, 1, 6 * D_MODEL), lambda i: (i // tiles_per_mod, 0, 0)),
                  _const_spec((1, D_MODEL)), _const_spec(w_out.shape), _const_spec(router_w.shape),
                  _const_spec((1, N_EXPERTS))],
        out_specs=[row_spec(D_MODEL), row_spec(D_MODEL), row_spec(TOP_K), row_spec(TOP_K)],
        out_shape=[jax.ShapeDtypeStruct((rows, D_MODEL), F32), jax.ShapeDtypeStruct((rows, D_MODEL), F32),
                   jax.ShapeDtypeStruct((rows, TOP_K), jnp.int32), jax.ShapeDtypeStruct((rows, TOP_K), F32)],
        compiler_params=_params("arbitrary"),
        name="outproj_router",
    )(x2, yr, ym, mod, norm_g.reshape(1, D_MODEL), w_out, router_w, router_b.reshape(1, N_EXPERTS))


def _rank_kernel(idx_ref, rank_ref, count_ref, run_s):
    i = pl.program_id(0)

    @pl.when(i == 0)
    def _():
        run_s[...] = jnp.zeros_like(run_s)

    idx = idx_ref[...]
    rows = idx.shape[0]
    lane = lax.broadcasted_iota(jnp.int32, (rows, N_EXPERTS), 1)
    hot = jnp.zeros((rows, N_EXPERTS), F32)
    for j in range(TOP_K):
        hot = hot + jnp.where(lane == idx[:, j:j + 1], 1.0, 0.0)
    before = _dot_bf16(_tri(rows, False, True).astype(F32), hot) + run_s[...]
    k_lane = lax.broadcasted_iota(jnp.int32, (rows, TOP_K), 1)
    rank = jnp.zeros((rows, TOP_K), F32)
    for j in range(TOP_K):
        rj = jnp.sum(jnp.where(lane == idx[:, j:j + 1], before, 0.0), axis=-1, keepdims=True)
        rank = jnp.where(k_lane == j, rj, rank)
    rank_ref[...] = rank.astype(jnp.int32)
    run_s[...] = run_s[...] + jnp.sum(hot, axis=0, keepdims=True)
    count_ref[...] = run_s[...].astype(jnp.int32)


def _expert_ranks(idx):
    rows = idx.shape[0]
    return pl.pallas_call(
        _rank_kernel,
        grid=(rows // ROW_TILE,),
        in_specs=[pl.BlockSpec((ROW_TILE, TOP_K), lambda i: (i, 0))],
        out_specs=[pl.BlockSpec((ROW_TILE, TOP_K), lambda i: (i, 0)), _const_spec((1, N_EXPERTS))],
        out_shape=[jax.ShapeDtypeStruct((rows, TOP_K), jnp.int32),
                   jax.ShapeDtypeStruct((1, N_EXPERTS), jnp.int32)],
        scratch_shapes=[pltpu.VMEM((1, N_EXPERTS), F32)],
        compiler_params=_params("arbitrary"),
        name="expert_ranks",
    )(idx)


def _dispatch_kernel(dest_ref, pend_ref, xa_ref, xb_ref, xs_ref, zero_s, sem, *, n_first_tiles):
    @pl.when(pl.program_id(0) == 0)
    def _():
        zero_s[...] = jnp.zeros_like(zero_s)

        def zero_block(start):
            return pltpu.make_async_copy(zero_s, xs_ref.at[pl.ds(pl.multiple_of(start, MOE_BLOCK), MOE_BLOCK)], sem)

        def has_rows(e):
            return pend_ref[e] > (pend_ref[e - 1] if e else 0)

        used_end = pend_ref[N_EXPERTS - 1]
        n_tail = (xs_ref.shape[0] - used_end) // MOE_BLOCK

        def start_tail(b, carry):
            zero_block(used_end + b * MOE_BLOCK).start()
            return carry

        def wait_tail(b, carry):
            zero_block(0).wait()
            return carry

        for e in range(N_EXPERTS):
            @pl.when(has_rows(e))
            def _(e=e):
                zero_block(pend_ref[e] - MOE_BLOCK).start()
        lax.fori_loop(0, n_tail, start_tail, 0)
        for e in range(N_EXPERTS):
            @pl.when(has_rows(e))
            def _():
                zero_block(0).wait()
        lax.fori_loop(0, n_tail, wait_tail, 0)

    def issue_from(x_ref):
        def issue(g, carry):
            for q in range(SUBLANES):
                for j in range(TOP_K):
                    slot = dest_ref[g * (SUBLANES * TOP_K) + q * TOP_K + j]
                    pltpu.make_async_copy(x_ref.at[g, pl.ds(q, 1)], xs_ref.at[pl.ds(slot, 1)],
                                          sem).start(priority=j % 2)
            return carry
        lax.fori_loop(0, ROW_TILE // SUBLANES, issue, 0)

    pl.when(pl.program_id(0) < n_first_tiles)(lambda: issue_from(xa_ref))
    pl.when(pl.program_id(0) >= n_first_tiles)(lambda: issue_from(xb_ref))
    all_rows = xs_ref.at[pl.ds(0, ROW_TILE * TOP_K)]
    pltpu.make_async_copy(all_rows, all_rows, sem).wait()


def _dispatch(dest_flat, pad_end, xa, xb, n_slots):
    na, nb = xa.shape[0] // ROW_TILE, xb.shape[0] // ROW_TILE
    tiles = lambda x: x.reshape(x.shape[0] // SUBLANES, SUBLANES, D_MODEL)
    block = (ROW_TILE // SUBLANES, SUBLANES, D_MODEL)
    return pl.pallas_call(
        functools.partial(_dispatch_kernel, n_first_tiles=na),
        grid=(na + nb,),
        in_specs=[pl.BlockSpec((ROW_TILE * TOP_K,), lambda i: (i,), memory_space=pltpu.SMEM),
                  pl.BlockSpec(memory_space=pltpu.SMEM),
                  pl.BlockSpec(block, lambda i: (jnp.minimum(i, na - 1), 0, 0)),
                  pl.BlockSpec(block, lambda i: (jnp.maximum(i - na, 0), 0, 0))],
        out_specs=pl.BlockSpec(memory_space=pl.ANY),
        out_shape=jax.ShapeDtypeStruct((n_slots, D_MODEL), xa.dtype),
        scratch_shapes=[pltpu.VMEM((MOE_BLOCK, D_MODEL), F32), pltpu.SemaphoreType.DMA(())],
        compiler_params=_params("arbitrary", disable_bounds_checks=True),
        name="moe_dispatch",
    )(dest_flat, pad_end, tiles(xa), tiles(xb))


def _expert_kernel(be_ref, nb_ref, next_ref, par_ref, xs_ref, w1_hbm, b1_ref, w2_hbm, b2_ref, y_ref,
                   w1_f, w2_f, w1_s, w2_s, sem1, sem2):
    i = pl.program_id(0)
    used = i < nb_ref[0]
    changed = jnp.logical_or(i == 0, be_ref[i] != be_ref[jnp.maximum(i - 1, 0)])

    def fetch(e, slot):
        return (pltpu.make_async_copy(w1_hbm.at[e], w1_f.at[slot], sem1.at[slot]),
                pltpu.make_async_copy(w2_hbm.at[e], w2_f.at[slot], sem2.at[slot]))

    @pl.when(i == 0)
    def _():
        for copy in fetch(be_ref[0], 0):
            copy.start()

    @pl.when(jnp.logical_and(changed, used))
    def _():
        slot = par_ref[i]
        for copy in fetch(be_ref[i], slot):
            copy.wait()

        @pl.when(next_ref[i] != be_ref[i])
        def _():
            for copy in fetch(next_ref[i], 1 - slot):
                copy.start()

        w1_s[...] = w1_f[slot].astype(BF16)
        w2_s[...] = w2_f[slot].astype(BF16)

    @pl.when(used)
    def _():
        xb = xs_ref[...].astype(BF16)
        dot = functools.partial(lax.dot_general, dimension_numbers=NN, preferred_element_type=F32)

        def up(c):
            lo, hi = c * FF_CHUNK, (c + 1) * FF_CHUNK
            return (dot(xb, w1_s[:, lo:hi]) + b1_ref[0, :, lo:hi],
                    dot(xb, w1_s[:, D_FF + lo:D_FF + hi]) + b1_ref[0, :, D_FF + lo:D_FF + hi])

        acc = jnp.zeros((MOE_BLOCK, D_MODEL), F32) + b2_ref[0]
        nxt = up(0)
        for c in range(D_FF // FF_CHUNK):
            glu, lin = nxt
            if c + 1 < D_FF // FF_CHUNK:
                nxt = up(c + 1)
            glu = jnp.minimum(glu, SWIGLU_LIMIT)
            lin = jnp.clip(lin, -SWIGLU_LIMIT, SWIGLU_LIMIT)
            act = glu * _sigmoid(SWIGLU_ALPHA * glu) * (lin + 1.0)
            acc = acc + dot(act.astype(BF16), w2_s[c * FF_CHUNK:(c + 1) * FF_CHUNK, :])
        y_ref[...] = acc

    @pl.when(i >= nb_ref[0])
    def _():
        y_ref[...] = jnp.zeros_like(y_ref)


def _experts(block_e, n_used, next_e, parity, xs, w1, b1, w2, b2):
    n_blocks = xs.shape[0] // MOE_BLOCK
    grid_spec = pltpu.PrefetchScalarGridSpec(
        num_scalar_prefetch=4,
        grid=(n_blocks,),
        in_specs=[pl.BlockSpec((MOE_BLOCK, D_MODEL), lambda i, be, nb, nx, pr: (jnp.minimum(i, nb[0] - 1), 0)),
                  pl.BlockSpec(memory_space=pl.ANY),
                  pl.BlockSpec((1, 1, 2 * D_FF), lambda i, be, nb, nx, pr: (be[i], 0, 0)),
                  pl.BlockSpec(memory_space=pl.ANY),
                  pl.BlockSpec((1, 1, D_MODEL), lambda i, be, nb, nx, pr: (be[i], 0, 0))],
        out_specs=pl.BlockSpec((MOE_BLOCK, D_MODEL), lambda i, be, nb, nx, pr: (i, 0)),
        scratch_shapes=[pltpu.VMEM((2, D_MODEL, 2 * D_FF), F32), pltpu.VMEM((2, D_FF, D_MODEL), F32),
                        pltpu.VMEM((D_MODEL, 2 * D_FF), BF16), pltpu.VMEM((D_FF, D_MODEL), BF16),
                        pltpu.SemaphoreType.DMA((2,)), pltpu.SemaphoreType.DMA((2,))],
    )
    return pl.pallas_call(
        _expert_kernel,
        grid_spec=grid_spec,
        out_shape=jax.ShapeDtypeStruct(xs.shape, F32),
        compiler_params=_params("arbitrary"),
        name="moe_experts",
    )(block_e, n_used, next_e, parity, xs, w1, b1.reshape(N_EXPERTS, 1, 2 * D_FF), w2,
      b2.reshape(N_EXPERTS, 1, D_MODEL))


def _combine_kernel(dest_ref, yb_ref, x1_ref, gate_ref, mod_ref, fg_ref, o_ref, buf, sem):
    def issue(g, carry):
        for q in range(SUBLANES):
            for j in range(TOP_K):
                slot = dest_ref[g * (SUBLANES * TOP_K) + q * TOP_K + j]
                pltpu.make_async_copy(yb_ref.at[pl.ds(slot, 1)], buf.at[g, j, pl.ds(q, 1)], sem).start(priority=j % 2)
        return carry

    lax.fori_loop(0, ROW_TILE // SUBLANES, issue, 0)
    pltpu.make_async_copy(buf, buf, sem).wait()

    g2 = mod_ref[0][:, 5 * D_MODEL:6 * D_MODEL]
    gates = gate_ref[...]
    moe = jnp.zeros((ROW_TILE, D_MODEL), F32)
    for j in range(TOP_K):
        moe = moe + buf[:, j].reshape(ROW_TILE, D_MODEL) * gates[:, j:j + 1]
    x2 = x1_ref[...] + g2 * moe
    y = x2 * lax.rsqrt(jnp.mean(x2 * x2, axis=-1, keepdims=True) + NORM_EPS)
    o_ref[...] = y * fg_ref[...]


def _combine(dest_flat, yb, x1, gates, mod, final_g, row_offset):
    rows = x1.shape[0]
    tiles_per_mod = rows // mod.shape[0] // ROW_TILE
    tile_offset = row_offset // ROW_TILE
    row_spec = pl.BlockSpec((ROW_TILE, D_MODEL), lambda i: (i, 0))
    return pl.pallas_call(
        _combine_kernel,
        grid=(rows // ROW_TILE,),
        in_specs=[pl.BlockSpec((ROW_TILE * TOP_K,), lambda i: (i + tile_offset,), memory_space=pltpu.SMEM),
                  pl.BlockSpec(memory_space=pl.ANY),
                  row_spec,
                  pl.BlockSpec((ROW_TILE, TOP_K), lambda i: (i, 0)),
                  pl.BlockSpec((1, 1, 6 * D_MODEL), lambda i: (i // tiles_per_mod, 0, 0)),
                  _const_spec((1, D_MODEL))],
        out_specs=row_spec,
        out_shape=jax.ShapeDtypeStruct((rows, D_MODEL), F32),
        scratch_shapes=[pltpu.VMEM((ROW_TILE // SUBLANES, TOP_K, SUBLANES, D_MODEL), F32),
                        pltpu.SemaphoreType.DMA(())],
        compiler_params=_params("arbitrary", disable_bounds_checks=True),
        name="moe_combine",
    )(dest_flat, yb, x1, gates, mod, final_g.reshape(1, D_MODEL))


def _block_diag_ones(width, block):
    i = jnp.arange(width) // block
    return (i[:, None] == i[None, :]).astype(F32)


def _mixers(x, mod, lp, s0, c0, n0, m0, grid):
    bsz, seq_len, _ = x.shape
    nc = seq_len // CHUNK
    x2 = x.reshape(bsz * seq_len, D_MODEL)
    zr, zm, zg, zgt = _inproj(x2, mod, lp['norm1_g'], lp['w_r'], lp['w_m'], lp['w_g'], lp['w_gt'])
    gt = zgt.reshape(N_GATES, bsz, nc, CHUNK).transpose(1, 2, 0, 3)
    yr, s_new = _rwkv(zr.reshape(bsz, seq_len, RWKV_COLS), s0, lp, grid)
    ym, c_new, n_new, m_new = _mlstm(zm.reshape(bsz, seq_len, MLSTM_MAIN), zg.reshape(bsz, seq_len, N_GATES), gt,
                                     c0, n0, m0, lp)
    x1, hn, idx, gates = _outproj(x2, yr.reshape(-1, DA), ym.reshape(-1, DB), mod, lp['norm2_g'], lp['w_out'],
                                  lp['router_w'], lp['router_b'])
    return x1, hn, idx, gates, (s_new, c_new, n_new, m_new[..., 0])


def kernel(x_prompt, x_sample, state_rwkv, state_mlstm_C, state_mlstm_n, state_mlstm_m, c, c_ctx, ada_w, ada_b, norm1_g, norm2_g, w_in, w_out, rwkv_mu, rwkv_w0, rwkv_w2, rwkv_a0, rwkv_a2, rwkv_g2, rwkv_kk, rwkv_ka, rwkv_rk, rwkv_gn_w, rwkv_gn_b, mlstm_conv, mlstm_bi, mlstm_bf, mlstm_norm_g, router_w, router_b, moe_w1, moe_b1, moe_w2, moe_b2, final_g):
    bp, lp_len, _ = x_prompt.shape
    bs, ls_len, _ = x_sample.shape
    w = w_in[0]
    lp = {
        'norm1_g': norm1_g[0], 'norm2_g': norm2_g[0],
        'w_r': w[:, :RWKV_COLS].astype(BF16),
        'w_m': w[:, RWKV_COLS:RWKV_COLS + MLSTM_MAIN].astype(BF16),
        'w_g': w[:, RWKV_COLS + MLSTM_MAIN:],
        'w_gt': w[:, RWKV_COLS + MLSTM_MAIN:].T,
        'w_out': w_out[0].astype(BF16),
        'mu': rwkv_mu[0].reshape(1, RWKV_COLS), 'kkw': rwkv_kk[0].reshape(1, DA), 'ka': rwkv_ka[0].reshape(1, DA),
        'rk': rwkv_rk[0].reshape(1, DA), 'gnw': rwkv_gn_w[0].reshape(1, DA), 'gnb': rwkv_gn_b[0].reshape(1, DA),
        'w0': rwkv_w0[0], 'w2': rwkv_w2[0], 'a0': rwkv_a0[0], 'a2': rwkv_a2[0], 'g2': rwkv_g2[0],
        'ones_bd': _block_diag_ones(PACK_W, HEAD_A).astype(BF16),
        'conv': mlstm_conv[0], 'bi': mlstm_bi[0].reshape(1, 2 * H_B), 'bit': mlstm_bi[0].reshape(2 * H_B, 1),
        'bf': mlstm_bf[0].reshape(1, 2 * H_B), 'bft': mlstm_bf[0].reshape(2 * H_B, 1),
        'ng': mlstm_norm_g[0].reshape(1, DB),
        'e_rep': jnp.repeat(jnp.eye(N_GATES, dtype=BF16), LANES, axis=1),
        'router_w': router_w[0], 'router_b': router_b[0],
    }
    cc = jnp.concatenate([c_ctx[None, :], c, jnp.zeros((2 * SUBLANES - 1 - bs, D_MODEL), F32)], axis=0)
    mod = _ada_mod(cc, ada_w[0], ada_b[0])
    mod_p = mod[0:1].reshape(1, 1, 6 * D_MODEL)
    mod_s = mod[1:1 + bs].reshape(bs, 1, 6 * D_MODEL)

    x1p, hnp, idxp, gatesp, st = _mixers(x_prompt, mod_p, lp, None, None, None, None, False)
    x1s, hns, idxs, gatess, _ = _mixers(x_sample, mod_s, lp, state_rwkv[:, 0], state_mlstm_C[:, 0],
                                        state_mlstm_n[:, 0], state_mlstm_m[:, 0], True)

    n_p = bp * lp_len
    idx = jnp.concatenate([idxp, idxs], axis=0)
    n_tok = idx.shape[0]
    rank, counts = _expert_ranks(idx)
    counts = counts[0]
    padded = (counts + MOE_BLOCK - 1) // MOE_BLOCK * MOE_BLOCK
    pad_end = jnp.cumsum(padded)
    pad_start = pad_end - padded
    n_blocks = n_tok * TOP_K // MOE_BLOCK + N_EXPERTS
    block_start = jnp.arange(n_blocks, dtype=jnp.int32) * MOE_BLOCK
    block_e = jnp.minimum(jnp.sum(pad_end[None, :] <= block_start[:, None], axis=1), N_EXPERTS - 1).astype(jnp.int32)
    n_used = (pad_end[-1:] // MOE_BLOCK).astype(jnp.int32)
    expert_ids = jnp.arange(N_EXPERTS, dtype=jnp.int32)
    start_of = jnp.sum(jnp.where(idx[:, :, None] == expert_ids, pad_start.astype(jnp.int32), 0), axis=-1)
    dest = (start_of + rank).astype(jnp.int32).reshape(-1)
    xs = _dispatch(dest, pad_end.astype(jnp.int32), hnp, hns, n_blocks * MOE_BLOCK)
    has_rows = counts > 0
    later = jnp.where(has_rows[None, :] & (expert_ids[None, :] > expert_ids[:, None]), expert_ids[None, :], N_EXPERTS)
    next_with_rows = jnp.min(later, axis=1)
    next_of = jnp.where(next_with_rows < N_EXPERTS, next_with_rows, expert_ids)
    place = jnp.cumsum(has_rows.astype(jnp.int32)) - 1
    yb = _experts(block_e, n_used, next_of[block_e].astype(jnp.int32), (place[block_e] % 2).astype(jnp.int32), xs,
                  moe_w1[0], moe_b1[0], moe_w2[0], moe_b2[0])
    y_prompt = _combine(dest, yb, x1p, gatesp, mod_p, final_g, 0).reshape(x_prompt.shape)
    y_sample = _combine(dest, yb, x1s, gatess, mod_s, final_g, n_p).reshape(x_sample.shape)

    s_new, c_new, n_new, m_new = st
    return (y_prompt, y_sample, s_new[:, None], c_new[:, None], n_new[:, None], m_new[:, None])
```
